```python
import math
import jax, jax.numpy as jnp
from jax import lax
import numpy as np

D_MODEL = 1024
BATCH = 32
SEQ = 256
DEPTH = 2
DEC_BATCH = 2
DEC_SEQ = 1024
PAST_LEN = 256

GRID_W = 64
HEAD_DIM = 64
MIX_WIDTH = D_MODEL
GROUP_WIDTH = MIX_WIDTH // 4
A_HEADS = GROUP_WIDTH // HEAD_DIM
A_KV_HEADS = A_HEADS // 2
B_HEADS = GROUP_WIDTH // HEAD_DIM
NA_ROWS = 8
NA_COLS = 16
C_HEADS = GROUP_WIDTH // HEAD_DIM
RET_CHUNK = 128
S5_CH = 16
S5_GROUPS = GROUP_WIDTH // S5_CH
S5_STATE = 64
MOE_GROUPS = 4
MOE_PER_GROUP = 8
MOE_EXPERTS = MOE_GROUPS * MOE_PER_GROUP
MOE_TOPK = 2
MOE_HIDDEN = D_MODEL // 8
Q_BLOCK = 128
ROPE_THETA = 10000.0
EPS = 1e-6
IN_SIZES = (A_HEADS * HEAD_DIM, A_KV_HEADS * HEAD_DIM, A_KV_HEADS * HEAD_DIM,
            GROUP_WIDTH, GROUP_WIDTH, GROUP_WIDTH,
            GROUP_WIDTH, GROUP_WIDTH, GROUP_WIDTH, GROUP_WIDTH,
            GROUP_WIDTH)
IN_WIDTH = sum(IN_SIZES)
F32 = jnp.float32

kernel_name = 'hybrid_flow_backbone_step'


def rmsnorm(x, g):
    x32 = x.astype(F32)
    y = x32 * lax.rsqrt(jnp.mean(x32 * x32, axis=-1, keepdims=True) + EPS)
    return y.astype(x.dtype) * g


def modulation(cond, w, b):
    return jax.nn.silu(cond) @ w + b


def axial_rope(x):
    t = jnp.arange(x.shape[1])
    row = (t // GRID_W).astype(F32)
    col = (t % GRID_W).astype(F32)
    half = HEAD_DIM // 2
    nf = half // 2
    inv = ROPE_THETA ** (-jnp.arange(nf, dtype=F32) / nf)

    def rot(xp, pos):
        ang = pos[:, None] * inv[None, :]
        cos = jnp.cos(ang)[None, :, None, :].astype(x.dtype)
        sin = jnp.sin(ang)[None, :, None, :].astype(x.dtype)
        x1, x2 = xp[..., :nf], xp[..., nf:]
        return jnp.concatenate([x1 * cos - x2 * sin, x1 * sin + x2 * cos], axis=-1)

    return jnp.concatenate([rot(x[..., :half], row), rot(x[..., half:], col)], axis=-1)


def blocked_attention(q, k, v):
    bsz, L, hq, d = q.shape
    hkv = k.shape[2]
    grp = hq // hkv
    nb = L // Q_BLOCK
    qb = (q * (d ** -0.5)).reshape(bsz, nb, Q_BLOCK, hkv, grp, d).swapaxes(0, 1)

    def block(qc):
        s = jnp.einsum('bqkgd,bskd->bkgqs', qc, k).astype(F32)
        pr = jax.nn.softmax(s, axis=-1).astype(v.dtype)
        return jnp.einsum('bkgqs,bskd->bqkgd', pr, v)

    o = lax.map(block, qb)
    return o.swapaxes(0, 1).reshape(bsz, L, hq * d)


def neighbourhood_attention(q, k, v, k_ctx, v_ctx, rel_bias):
    bsz, T, H, d = q.shape
    rows = T // GRID_W
    wr = min(NA_ROWS, rows)
    row_start = np.clip(np.arange(rows) - wr // 2, 0, rows - wr)
    col_start = np.clip(np.arange(GRID_W) - NA_COLS // 2, 0, GRID_W - NA_COLS)
    key_rows = row_start[:, None] + np.arange(wr)
    key_cols = col_start[:, None] + np.arange(NA_COLS)
    idx = (key_rows[:, None, :, None] * GRID_W + key_cols[None, :, None, :]).reshape(rows, GRID_W, wr * NA_COLS)
    rel_r = key_rows - np.arange(rows)[:, None] + NA_ROWS - 1
    rel_c = key_cols - np.arange(GRID_W)[:, None] + NA_COLS - 1
    bias = rel_bias.astype(F32)[:, rel_r[:, None, :, None], rel_c[None, :, None, :]]
    bias = bias.reshape(H, rows, GRID_W, wr * NA_COLS).transpose(1, 0, 2, 3)
    qr = (q * (d ** -0.5)).reshape(bsz, rows, GRID_W, H, d).swapaxes(0, 1)
    n_loc = wr * NA_COLS

    def row_block(xs):
        q_blk, idx_blk, bias_blk = xs
        kw = k[:, idx_blk]
        vw = v[:, idx_blk]
        s_loc = jnp.einsum('bqhd,bqkhd->bhqk', q_blk, kw).astype(F32) + bias_blk[None]
        s_ctx = jnp.einsum('bqhd,bshd->bhqs', q_blk, k_ctx).astype(F32)
        pr = jax.nn.softmax(jnp.concatenate([s_loc, s_ctx], axis=-1), axis=-1).astype(v.dtype)
        return (jnp.einsum('bhqk,bqkhd->bqhd', pr[..., :n_loc], vw)
                + jnp.einsum('bhqs,bshd->bqhd', pr[..., n_loc:], v_ctx))

    o = lax.map(row_block, (qr, jnp.asarray(idx, dtype=jnp.int32), bias))
    return o.swapaxes(0, 1).reshape(bsz, T, H * d)


def retention_scan(q, k, v, log_gamma, s0):
    bsz, L, H, _ = q.shape
    C = RET_CHUNK
    n = L // C
    pos = jnp.arange(C, dtype=F32)
    diff = pos[:, None] - pos[None, :]
    intra = jnp.where(diff[None] >= 0, jnp.exp(log_gamma[:, None, None] * jnp.maximum(diff, 0.0)[None]), 0.0).astype(q.dtype)
    q_dec = jnp.exp(log_gamma[None, :] * (pos[:, None] + 1.0)).astype(q.dtype)
    k_dec = jnp.exp(log_gamma[None, :] * (C - 1.0 - pos[:, None])).astype(q.dtype)
    chunk_dec = jnp.exp(log_gamma * C).astype(q.dtype)

    def chunks(t):
        return t.reshape(bsz, n, C, H, t.shape[-1]).swapaxes(0, 1)

    def step(s, xs):
        qc, kc, vc = xs
        sc = jnp.einsum('bihd,bjhd->bhij', qc, kc) * intra[None]
        o = (jnp.einsum('bhij,bjhe->bihe', sc, vc)
             + jnp.einsum('bihd,bhde->bihe', qc, s) * q_dec[None, :, :, None])
        s = s * chunk_dec[None, :, None, None] + jnp.einsum('bjhd,bjhe->bhde', kc * k_dec[None, :, :, None], vc)
        return s, o

    s, o = lax.scan(step, s0, (chunks(q), chunks(k), chunks(v)))
    return o.swapaxes(0, 1).reshape(bsz, L, H, v.shape[-1]), s


def retention_bidir(q, k, v, g, s0, decay_logit, gn_g):
    lg = jax.nn.log_sigmoid(decay_logit.astype(F32))
    s0 = s0.astype(q.dtype)
    o_f, s_f = retention_scan(q, k, v, lg[0], s0[:, 0])
    o_b, s_b = retention_scan(q[:, ::-1], k[:, ::-1], v[:, ::-1], lg[1], s0[:, 1])
    o = (o_f + o_b[:, ::-1]).astype(F32)
    mu = jnp.mean(o, axis=-1, keepdims=True)
    var = jnp.mean(jnp.square(o - mu), axis=-1, keepdims=True)
    o = ((o - mu) * lax.rsqrt(var + EPS)).astype(q.dtype) * gn_g
    bsz, L = q.shape[:2]
    return o.reshape(bsz, L, -1) * jax.nn.silu(g), jnp.stack([s_f, s_b], axis=1)


def complex_affine_combine(e1, e2):
    a1r, a1i, b1r, b1i = e1
    a2r, a2i, b2r, b2i = e2
    return (a2r * a1r - a2i * a1i, a2r * a1i + a2i * a1r,
            a2r * b1r - a2i * b1i + b2r, a2r * b1i + a2i * b1r + b2i)


def s5_scan(u, lam_re, lam_im, log_dt, b_re, b_im, h_re, h_im):
    dt = jnp.exp(log_dt)[:, None]
    mag = jnp.exp(lam_re * dt)
    a_re = mag * jnp.cos(lam_im * dt)
    a_im = mag * jnp.sin(lam_im * dt)
    den = lam_re * lam_re + lam_im * lam_im
    r_re = ((a_re - 1.0) * lam_re + a_im * lam_im) / den
    r_im = (a_im * lam_re - (a_re - 1.0) * lam_im) / den
    bb_re = r_re[..., None] * b_re - r_im[..., None] * b_im
    bb_im = r_re[..., None] * b_im + r_im[..., None] * b_re
    x_re = jnp.einsum('gpc,blgc->blgp', bb_re, u)
    x_im = jnp.einsum('gpc,blgc->blgp', bb_im, u)
    x_re = x_re.at[:, 0].add(a_re * h_re - a_im * h_im)
    x_im = x_im.at[:, 0].add(a_re * h_im + a_im * h_re)
    A_re = jnp.broadcast_to(a_re, x_re.shape)
    A_im = jnp.broadcast_to(a_im, x_im.shape)
    _, _, s_re, s_im = lax.associative_scan(complex_affine_combine, (A_re, A_im, x_re, x_im), axis=1)
    return s_re, s_im


def s5_bidir(u, h0, p):
    bsz, L, _ = u.shape
    u32 = u.astype(F32).reshape(bsz, L, S5_GROUPS, S5_CH)
    h0 = h0.astype(F32)
    ys = []
    finals = []
    for d, seq in ((0, u32), (1, u32[:, ::-1])):
        s_re, s_im = s5_scan(seq, p['lam_re'][d].astype(F32), p['lam_im'][d].astype(F32),
                             p['log_dt'][d].astype(F32), p['b_re'][d].astype(F32), p['b_im'][d].astype(F32),
                             h0[:, d, 0], h0[:, d, 1])
        y = (jnp.einsum('gcp,blgp->blgc', p['c_re'][d].astype(F32), s_re)
             - jnp.einsum('gcp,blgp->blgc', p['c_im'][d].astype(F32), s_im))
        ys.append(y)
        finals.append(jnp.stack([s_re[:, -1], s_im[:, -1]], axis=1))
    y = ys[0] + ys[1][:, ::-1] + p['d'].astype(F32) * u32
    z = jax.nn.gelu(y.reshape(bsz, L, GROUP_WIDTH)).astype(u.dtype)
    return z * jax.nn.sigmoid(z @ p['glu_w']), jnp.stack(finals, axis=1)


def hier_moe(h, p):
    shp = h.shape
    n = h.reshape(-1, shp[-1])
    g_prob = jax.nn.softmax((n @ p['gw']).astype(F32) + p['gb'].astype(F32), axis=-1)
    g_top = jnp.argmax(g_prob, axis=-1)
    p_top = jnp.max(g_prob, axis=-1)
    e_logits = ((n @ p['ew']).astype(F32) + p['eb'].astype(F32)).reshape(-1, MOE_GROUPS, MOE_PER_GROUP)
    e_in_group = jnp.einsum('nge,ng->ne', e_logits, jax.nn.one_hot(g_top, MOE_GROUPS, dtype=F32))
    top_l, top_i = lax.top_k(e_in_group, MOE_TOPK)
    top_w = jax.nn.softmax(top_l, axis=-1) * p_top[:, None]
    expert_id = g_top[:, None] * MOE_PER_GROUP + top_i
    gates = jnp.einsum('nke,nk->ne', jax.nn.one_hot(expert_id, MOE_EXPERTS, dtype=F32), top_w).astype(h.dtype)
    hid = jax.nn.silu(jnp.einsum('nd,edf->nef', n, p['w1'])) * jnp.einsum('nd,edf->nef', n, p['w3'])
    out = jnp.einsum('nef,ne,efd->nd', hid, gates, p['w2'])
    return out.reshape(shp)


def project(h, p):
    bsz, L, _ = h.shape
    z = h @ p['w_in']
    offs = [int(o) for o in np.cumsum(IN_SIZES)[:-1]]
    aq, ak, av, bq, bk, bv, cq, ck, cv, cg, du = jnp.split(z, offs, axis=-1)

    def heads(t):
        return t.reshape(bsz, L, -1, HEAD_DIM)

    aq = rmsnorm(heads(aq), p['a_qn'])
    ak = rmsnorm(heads(ak), p['a_kn'])
    ck = heads(ck) * (HEAD_DIM ** -0.5)
    return aq, ak, heads(av), heads(bq), heads(bk), heads(bv), heads(cq), ck, heads(cv), cg, du


def context_mixers(h, p):
    aq, ak, av, bq, bk, bv, cq, ck, cv, cg, du = project(h, p)
    bsz = h.shape[0]
    oa = blocked_attention(aq, ak, av)
    ob = blocked_attention(bq, bk, bv)
    s0 = jnp.zeros((bsz, 2, C_HEADS, HEAD_DIM, HEAD_DIM), h.dtype)
    oc, ret_state = retention_bidir(cq, ck, cv, cg, s0, p['ret_decay'], p['ret_gn'])
    h0 = jnp.zeros((bsz, 2, 2, S5_GROUPS, S5_STATE), F32)
    od, ssm_state = s5_bidir(du, h0, p)
    mix = jnp.concatenate([oa, ob, oc, od], axis=-1)
    return mix, (ak, av, bk, bv, ret_state, ssm_state)


def latent_mixers(h, ka, va, kb, vb, s_ret, s_ssm, p):
    aq, ak, av, bq, bk, bv, cq, ck, cv, cg, du = project(h, p)
    aq = axial_rope(aq)
    ak = axial_rope(ak)
    oa = blocked_attention(aq, jnp.concatenate([ka.astype(ak.dtype), ak], axis=1),
                           jnp.concatenate([va.astype(av.dtype), av], axis=1))
    ob = neighbourhood_attention(bq, bk, bv, kb.astype(bk.dtype), vb.astype(bv.dtype), p['rel_bias'])
    oc, _ = retention_bidir(cq, ck, cv, cg, s_ret, p['ret_decay'], p['ret_gn'])
    od, _ = s5_bidir(du, s_ssm, p)
    return jnp.concatenate([oa, ob, oc, od], axis=-1)


def finish_layer(x, mix, m, p):
    x = x + m[2] * (mix @ p['w_out'])
    h = rmsnorm(x, p['norm2']) * (1.0 + m[4]) + m[3]
    return x + m[5] * hier_moe(h, p)


def setup_inputs(seed: int = 0) -> dict:
    key = jax.random.key(seed)
    keys = jax.random.split(key, 40)

    def nrm(i, shape, scale):
        return jax.random.normal(keys[i], shape, F32) * scale

    D = D_MODEL
    gam = 1.0 - 2.0 ** (-5.0 - jnp.arange(C_HEADS, dtype=F32))
    decay_base = jnp.log(gam) - jnp.log1p(-gam)
    n_idx = jnp.arange(S5_STATE, dtype=F32)
    return {
        'x_prompt': nrm(0, (BATCH, SEQ, D), 1.0),
        'x_sample': nrm(1, (DEC_BATCH, DEC_SEQ, D), 1.0),
        'cache_a_k': nrm(2, (DEC_BATCH, DEPTH, PAST_LEN, A_KV_HEADS, HEAD_DIM), 1.0),
        'cache_a_v': nrm(3, (DEC_BATCH, DEPTH, PAST_LEN, A_KV_HEADS, HEAD_DIM), 1.0),
        'cache_b_k': nrm(4, (DEC_BATCH, DEPTH, PAST_LEN, B_HEADS, HEAD_DIM), 1.0),
        'cache_b_v': nrm(5, (DEC_BATCH, DEPTH, PAST_LEN, B_HEADS, HEAD_DIM), 1.0),
        'state_ret': nrm(6, (DEC_BATCH, DEPTH, 2, C_HEADS, HEAD_DIM, HEAD_DIM), 1.0),
        'state_ssm': nrm(7, (DEC_BATCH, DEPTH, 2, 2, S5_GROUPS, S5_STATE), 0.5),
        'c': nrm(8, (DEC_BATCH, D), 1.0),
        'c_ctx': nrm(9, (D,), 1.0),
        'mod_w': nrm(10, (DEPTH, D, 6 * D), 0.5 * D ** -0.5),
        'mod_b': nrm(11, (DEPTH, 6 * D), 0.02),
        'norm1_g': 1.0 + nrm(12, (DEPTH, D), 0.02),
        'norm2_g': 1.0 + nrm(13, (DEPTH, D), 0.02),
        'w_in': nrm(14, (DEPTH, D, IN_WIDTH), D ** -0.5),
        'a_qn_g': 1.0 + nrm(15, (DEPTH, HEAD_DIM), 0.02),
        'a_kn_g': 1.0 + nrm(16, (DEPTH, HEAD_DIM), 0.02),
        'b_rel_bias': nrm(17, (DEPTH, B_HEADS, 2 * NA_ROWS - 1, 2 * NA_COLS - 1), 0.1),
        'ret_decay': decay_base[None, None, :] + nrm(18, (DEPTH, 2, C_HEADS), 0.1),
        'ret_gn_g': 1.0 + nrm(19, (DEPTH, C_HEADS, HEAD_DIM), 0.02),
        's5_lam_re': -0.5 + nrm(20, (DEPTH, 2, S5_GROUPS, S5_STATE), 0.01),
        's5_lam_im': jnp.pi * n_idx + nrm(21, (DEPTH, 2, S5_GROUPS, S5_STATE), 0.01),
        's5_log_dt': jax.random.uniform(keys[22], (DEPTH, 2, S5_GROUPS), F32, math.log(0.001), math.log(0.1)),
        's5_b_re': nrm(23, (DEPTH, 2, S5_GROUPS, S5_STATE, S5_CH), (2 * S5_CH) ** -0.5),
        's5_b_im': nrm(24, (DEPTH, 2, S5_GROUPS, S5_STATE, S5_CH), (2 * S5_CH) ** -0.5),
        's5_c_re': nrm(25, (DEPTH, 2, S5_GROUPS, S5_CH, S5_STATE), (2 * S5_STATE) ** -0.5),
        's5_c_im': nrm(26, (DEPTH, 2, S5_GROUPS, S5_CH, S5_STATE), (2 * S5_STATE) ** -0.5),
        's5_d': nrm(27, (DEPTH, S5_GROUPS, S5_CH), 1.0),
        's5_glu_w': nrm(28, (DEPTH, GROUP_WIDTH, GROUP_WIDTH), GROUP_WIDTH ** -0.5),
        'w_out': nrm(29, (DEPTH, MIX_WIDTH, D), MIX_WIDTH ** -0.5),
        'moe_gw': nrm(30, (DEPTH, D, MOE_GROUPS), D ** -0.5),
        'moe_gb': nrm(31, (DEPTH, MOE_GROUPS), 0.01),
        'moe_ew': nrm(32, (DEPTH, D, MOE_EXPERTS), D ** -0.5),
        'moe_eb': nrm(33, (DEPTH, MOE_EXPERTS), 0.01),
        'moe_w1': nrm(34, (DEPTH, MOE_EXPERTS, D, MOE_HIDDEN), D ** -0.5),
        'moe_w3': nrm(35, (DEPTH, MOE_EXPERTS, D, MOE_HIDDEN), D ** -0.5),
        'moe_w2': nrm(36, (DEPTH, MOE_EXPERTS, MOE_HIDDEN, D), MOE_HIDDEN ** -0.5),
        'final_norm_g': 1.0 + nrm(37, (D,), 0.02),
    }


def reference(x_prompt, x_sample, cache_a_k, cache_a_v, cache_b_k, cache_b_v, state_ret, state_ssm,
              c, c_ctx, mod_w, mod_b, norm1_g, norm2_g, w_in, a_qn_g, a_kn_g, b_rel_bias, ret_decay,
              ret_gn_g, s5_lam_re, s5_lam_im, s5_log_dt, s5_b_re, s5_b_im, s5_c_re, s5_c_im, s5_d,
              s5_glu_w, w_out, moe_gw, moe_gb, moe_ew, moe_eb, moe_w1, moe_w3, moe_w2, final_norm_g):
    xc = x_prompt
    xs = x_sample
    new_ak, new_av, new_bk, new_bv, new_ret, new_ssm = [], [], [], [], [], []
    for l in range(DEPTH):
        p = {'norm1': norm1_g[l], 'norm2': norm2_g[l], 'w_in': w_in[l], 'a_qn': a_qn_g[l], 'a_kn': a_kn_g[l],
             'rel_bias': b_rel_bias[l], 'ret_decay': ret_decay[l], 'ret_gn': ret_gn_g[l],
             'lam_re': s5_lam_re[l], 'lam_im': s5_lam_im[l], 'log_dt': s5_log_dt[l],
             'b_re': s5_b_re[l], 'b_im': s5_b_im[l], 'c_re': s5_c_re[l], 'c_im': s5_c_im[l],
             'd': s5_d[l], 'glu_w': s5_glu_w[l], 'w_out': w_out[l],
             'gw': moe_gw[l], 'gb': moe_gb[l], 'ew': moe_ew[l], 'eb': moe_eb[l],
             'w1': moe_w1[l], 'w3': moe_w3[l], 'w2': moe_w2[l]}
        mc = jnp.split(modulation(c_ctx, mod_w[l], mod_b[l]), 6, axis=-1)
        hc = rmsnorm(xc, p['norm1']) * (1.0 + mc[1]) + mc[0]
        mix_c, st = context_mixers(hc, p)
        xc = finish_layer(xc, mix_c, mc, p)
        new_ak.append(st[0])
        new_av.append(st[1])
        new_bk.append(st[2])
        new_bv.append(st[3])
        new_ret.append(st[4])
        new_ssm.append(st[5])
        ms = jnp.split(modulation(c, mod_w[l], mod_b[l])[:, None, :], 6, axis=-1)
        hs = rmsnorm(xs, p['norm1']) * (1.0 + ms[1]) + ms[0]
        mix_s = latent_mixers(hs, cache_a_k[:, l], cache_a_v[:, l], cache_b_k[:, l], cache_b_v[:, l],
                              state_ret[:, l], state_ssm[:, l], p)
        xs = finish_layer(xs, mix_s, ms, p)
    y_prompt = rmsnorm(xc, final_norm_g)
    y_sample = rmsnorm(xs, final_norm_g)
    new_cache_a_k = jnp.stack(new_ak, axis=1)
    new_cache_a_v = jnp.stack(new_av, axis=1)
    new_cache_b_k = jnp.stack(new_bk, axis=1)
    new_cache_b_v = jnp.stack(new_bv, axis=1)
    new_state_ret = jnp.stack(new_ret, axis=1)
    new_state_ssm = jnp.stack(new_ssm, axis=1)
    return (y_prompt, y_sample, new_cache_a_k, new_cache_a_v, new_cache_b_k, new_cache_b_v, new_state_ret, new_state_ssm)
```

```python
import functools
import math

import numpy as np
import jax
import jax.numpy as jnp
from jax import lax
from jax.experimental import pallas as pl
from jax.experimental.pallas import tpu as pltpu

F32 = jnp.float32
BF16 = jnp.bfloat16

D_MODEL = 1024
BATCH = 32
SEQ = 256
DEPTH = 2
DEC_BATCH = 2
DEC_SEQ = 1024
PAST_LEN = 256
GRID_W = 64
HEAD_DIM = 64
GROUP_WIDTH = 256
A_HEADS = 4
A_KV_HEADS = 2
B_HEADS = 4
NA_ROWS = 8
NA_COLS = 16
C_HEADS = 4
S5_CH = 16
S5_GROUPS = 16
S5_STATE = 64
MOE_GROUPS = 4
MOE_PER_GROUP = 8
MOE_EXPERTS = 32
MOE_HIDDEN = 128
ROPE_THETA = 10000.0
EPS = 1e-6
IN_WIDTH = 2560
Q_SCALE = HEAD_DIM ** -0.5

OFF_AQ, OFF_AK, OFF_AV = 0, 256, 384
OFF_BQ, OFF_BK, OFF_BV = 512, 768, 1024
OFF_CQ, OFF_CK, OFF_CV, OFF_CG = 1280, 1536, 1792, 2048
OFF_DU = 2304

LANES = 128
SUBLANES = 8
S5_SP = S5_GROUPS * S5_STATE
S5_SEG = 256
ROUTER_OFF = 4
NEG_BIG = -1e30
VMEM_LIMIT = 56 * 1024 * 1024


def _cparams(*sem):
    return pltpu.CompilerParams(dimension_semantics=sem, vmem_limit_bytes=VMEM_LIMIT)


def _bdot(a, b):
    return jnp.dot(a.astype(BF16), b.astype(BF16), preferred_element_type=F32)


def _bdot_nt(a, b):
    return lax.dot_general(a.astype(BF16), b.astype(BF16), (((1,), (1,)), ((), ())),
                           preferred_element_type=F32)


def _bdot_tn(a, b):
    return lax.dot_general(a.astype(BF16), b.astype(BF16), (((0,), (0,)), ((), ())),
                           preferred_element_type=F32)


def _split(a):
    hi = a.astype(BF16)
    lo = (a - hi.astype(F32)).astype(BF16)
    return hi, lo


def _dot_hilo_lhs(a, b_bf16):
    hi, lo = _split(a)
    return (jnp.dot(hi, b_bf16, preferred_element_type=F32)
            + jnp.dot(lo, b_bf16, preferred_element_type=F32))


def _dot3(a, b):
    ah, al = _split(a)
    bh, bl = _split(b)
    return (jnp.dot(ah, bh, preferred_element_type=F32)
            + jnp.dot(ah, bl, preferred_element_type=F32)
            + jnp.dot(al, bh, preferred_element_type=F32))


def _rms_rows(x):
    return x * lax.rsqrt(jnp.mean(x * x, axis=-1, keepdims=True) + EPS)


def _mod_kernel(cond_ref, w_ref, b_ref, o_ref):
    o_ref[...] = _bdot(jax.nn.silu(cond_ref[...]), w_ref[...]) + b_ref[...]


def _modulation(cond, mod_w, mod_b):
    tn = 1536
    return pl.pallas_call(
        _mod_kernel,
        grid=(DEPTH, 6 * D_MODEL // tn),
        in_specs=[pl.BlockSpec((SUBLANES, D_MODEL), lambda l, j: (0, 0)),
                  pl.BlockSpec((None, D_MODEL, tn), lambda l, j: (l, 0, j)),
                  pl.BlockSpec((None, 1, tn), lambda l, j: (l, 0, j))],
        out_specs=pl.BlockSpec((None, SUBLANES, tn), lambda l, j: (l, 0, j)),
        out_shape=jax.ShapeDtypeStruct((DEPTH, SUBLANES, 6 * D_MODEL), F32),
        compiler_params=_cparams("arbitrary", "arbitrary"),
        name="modulation",
    )(cond, mod_w, mod_b.reshape(DEPTH, 1, 6 * D_MODEL))


def _head_norm(t, g):
    w = t.shape[1]
    ri = lax.broadcasted_iota(jnp.int32, (w, w), 0) // HEAD_DIM
    ci = lax.broadcasted_iota(jnp.int32, (w, w), 1) // HEAD_DIM
    gm = jnp.where(ri == ci, 1.0 / HEAD_DIM, 0.0).astype(BF16)
    ms = _dot_hilo_lhs(t * t, gm)
    return t * lax.rsqrt(ms + EPS) * g


def _rope(t, cos, sa, sb):
    return (t * cos + pltpu.roll(t, LANES - 16, 1) * sa + pltpu.roll(t, 16, 1) * sb)


def _proj_kernel(x_ref, mod_ref, g1_ref, w_ref, qn_ref, kn_ref, *rest, rope):
    if rope:
        cos_ref, sa_ref, sb_ref, z_ref = rest
    else:
        (z_ref,) = rest
    h = _rms_rows(x_ref[...]) * g1_ref[...] * (1.0 + mod_ref[1:2, :]) + mod_ref[0:1, :]
    z = jnp.dot(h.astype(BF16), w_ref[...], preferred_element_type=F32)
    aq = _head_norm(z[:, OFF_AQ:OFF_AK], qn_ref[...])
    ak = _head_norm(z[:, OFF_AK:OFF_AV], kn_ref[...])
    for j in range(3):
        t = aq[:, j * LANES:(j + 1) * LANES] if j < 2 else ak
        if rope:
            cj = 0 if j == 2 else j
            sl = slice(cj * LANES, (cj + 1) * LANES)
            t = _rope(t, cos_ref[:, sl], sa_ref[:, sl], sb_ref[:, sl])
        z_ref[:, j * LANES:(j + 1) * LANES] = t
    z_ref[:, OFF_AV:OFF_CK] = z[:, OFF_AV:OFF_CK]
    z_ref[:, OFF_CK:OFF_CV] = z[:, OFF_CK:OFF_CV] * Q_SCALE
    z_ref[:, OFF_CV:] = z[:, OFF_CV:]


def _project(x, mod, g1, w_in_bf, qn, kn, rope_tabs, *, seq_len, tiles_per_mod, tm=256):
    n = x.shape[0]
    rope = rope_tabs is not None
    in_specs = [pl.BlockSpec((tm, D_MODEL), lambda i: (i, 0)),
                pl.BlockSpec((None, 6, D_MODEL), lambda i: (i // tiles_per_mod, 0, 0)),
                pl.BlockSpec((1, D_MODEL), lambda i: (0, 0)),
                pl.BlockSpec((D_MODEL, IN_WIDTH), lambda i: (0, 0)),
                pl.BlockSpec((1, 256), lambda i: (0, 0)),
                pl.BlockSpec((1, 128), lambda i: (0, 0))]
    args = [x, mod, g1, w_in_bf, qn, kn]
    if rope:
        tps = seq_len // tm
        in_specs += [pl.BlockSpec((tm, 256), lambda i: (i % tps, 0))] * 3
        args += list(rope_tabs)
    return pl.pallas_call(
        functools.partial(_proj_kernel, rope=rope),
        grid=(n // tm,),
        in_specs=in_specs,
        out_specs=pl.BlockSpec((tm, IN_WIDTH), lambda i: (i, 0)),
        out_shape=jax.ShapeDtypeStruct((n, IN_WIDTH), F32),
        compiler_params=_cparams("parallel"),
        name="project",
    )(*args)


def _rope_tables():
    t = jnp.arange(DEC_SEQ)
    row = (t // GRID_W).astype(F32)
    col = (t % GRID_W).astype(F32)
    nf = HEAD_DIM // 4
    inv = ROPE_THETA ** (-jnp.arange(nf, dtype=F32) / nf)
    ang_r = row[:, None] * inv[None, :]
    ang_c = col[:, None] * inv[None, :]
    zeros = jnp.zeros_like(ang_r)
    cos = jnp.concatenate([jnp.cos(ang_r), jnp.cos(ang_r), jnp.cos(ang_c), jnp.cos(ang_c)], axis=-1)
    sa = jnp.concatenate([-jnp.sin(ang_r), zeros, -jnp.sin(ang_c), zeros], axis=-1)
    sb = jnp.concatenate([zeros, jnp.sin(ang_r), zeros, jnp.sin(ang_c)], axis=-1)
    return tuple(jnp.tile(a, (1, 4)) for a in (cos, sa, sb))


def _softmax_pv(scores, values):
    m = functools.reduce(jnp.maximum, [jnp.max(s, axis=-1, keepdims=True) for s in scores])
    ps = [jnp.exp(s - m) for s in scores]
    denom = functools.reduce(jnp.add, [jnp.sum(p, axis=-1, keepdims=True) for p in ps])
    o = functools.reduce(jnp.add, [_bdot(p, v) for p, v in zip(ps, values)])
    return o / denom


def _head(ref, h):
    return ref[:, h * HEAD_DIM:(h + 1) * HEAD_DIM]


def _ctx_attn_kernel(aq_ref, ak_ref, av_ref, bq_ref, bk_ref, bv_ref, oa_ref, ob_ref):
    for q_ref, k_ref, v_ref, o_ref, hkv in ((aq_ref, ak_ref, av_ref, oa_ref, A_KV_HEADS),
                                            (bq_ref, bk_ref, bv_ref, ob_ref, B_HEADS)):
        grp = 4 // hkv
        for h in range(4):
            s = _bdot_nt(_head(q_ref, h) * Q_SCALE, _head(k_ref, h // grp))
            o_ref[:, h * HEAD_DIM:(h + 1) * HEAD_DIM] = _softmax_pv([s], [_head(v_ref, h // grp)])


def _ctx_attention(z, nb, seq_len):
    def col(width, off):
        return pl.BlockSpec((seq_len, width), lambda b: (b, off // width))
    return pl.pallas_call(
        _ctx_attn_kernel,
        grid=(nb,),
        in_specs=[col(256, OFF_AQ), col(128, OFF_AK), col(128, OFF_AV),
                  col(256, OFF_BQ), col(256, OFF_BK), col(256, OFF_BV)],
        out_specs=[pl.BlockSpec((seq_len, 256), lambda b: (b, 0))] * 2,
        out_shape=[jax.ShapeDtypeStruct((nb * seq_len, 256), F32)] * 2,
        compiler_params=_cparams("parallel"),
        name="ctx_attention",
    )(z, z, z, z, z, z)


def _lat_attn_a_kernel(q_ref, kn_ref, vn_ref, kc_ref, vc_ref, o_ref):
    grp = A_HEADS // A_KV_HEADS
    for h in range(A_HEADS):
        q = _head(q_ref, h) * Q_SCALE
        kv = h // grp
        s_c = _bdot_nt(q, _head(kc_ref, kv))
        s_n = _bdot_nt(q, _head(kn_ref, kv))
        o_ref[:, h * HEAD_DIM:(h + 1) * HEAD_DIM] = _softmax_pv(
            [s_c, s_n], [_head(vc_ref, kv), _head(vn_ref, kv)])


def _lat_attention_a(z, cache_k, cache_v, layer, tq=256):
    nq = DEC_SEQ // tq
    cache_spec = pl.BlockSpec((None, None, PAST_LEN, 128), lambda b, j: (b, layer, 0, 0))
    return pl.pallas_call(
        _lat_attn_a_kernel,
        grid=(DEC_BATCH, nq),
        in_specs=[pl.BlockSpec((tq, 256), lambda b, j: (b * nq + j, OFF_AQ // 256)),
                  pl.BlockSpec((DEC_SEQ, 128), lambda b, j: (b, OFF_AK // 128)),
                  pl.BlockSpec((DEC_SEQ, 128), lambda b, j: (b, OFF_AV // 128)),
                  cache_spec, cache_spec],
        out_specs=pl.BlockSpec((tq, 256), lambda b, j: (b * nq + j, 0)),
        out_shape=jax.ShapeDtypeStruct((DEC_BATCH * DEC_SEQ, 256), F32),
        compiler_params=_cparams("parallel", "parallel"),
        name="lat_attention_a",
    )(z, z, z, cache_k, cache_v)


NA_KEYS = NA_ROWS * GRID_W


def _na_kernel(q_ref, k_ref, v_ref, kc_ref, vc_ref, bias_ref, o_ref):
    r = pl.program_id(1)
    rows = DEC_SEQ // GRID_W
    start = pl.multiple_of(jnp.clip(r - NA_ROWS // 2, 0, rows - NA_ROWS) * GRID_W, GRID_W)
    kl = k_ref[pl.ds(start, NA_KEYS), :]
    vl = v_ref[pl.ds(start, NA_KEYS), :]
    for h in range(B_HEADS):
        sl = slice(h * HEAD_DIM, (h + 1) * HEAD_DIM)
        q = q_ref[:, sl] * Q_SCALE
        s_loc = _bdot_nt(q, kl[:, sl]) + bias_ref[h]
        s_ctx = _bdot_nt(q, kc_ref[:, sl])
        o_ref[:, sl] = _softmax_pv([s_loc, s_ctx], [vl[:, sl], vc_ref[:, sl]])


def _na_bias(rel_bias):
    rows = DEC_SEQ // GRID_W
    wr = min(NA_ROWS, rows)
    row_start = np.clip(np.arange(rows) - wr // 2, 0, rows - wr)
    col_start = np.clip(np.arange(GRID_W) - NA_COLS // 2, 0, GRID_W - NA_COLS)
    key_rows = row_start[:, None] + np.arange(wr)
    rel_r = key_rows - np.arange(rows)[:, None] + NA_ROWS - 1
    kc = np.arange(GRID_W)
    rel_c = kc[None, :] - np.arange(GRID_W)[:, None] + NA_COLS - 1
    inside = (kc[None, :] >= col_start[:, None]) & (kc[None, :] < col_start[:, None] + NA_COLS)
    rel_c = np.where(inside, rel_c, 0)
    b = rel_bias.astype(F32)[:, rel_r[:, None, :, None], rel_c[None, :, None, :]]
    b = jnp.where(jnp.asarray(inside)[None, None, :, None, :], b, NEG_BIG)
    return b.reshape(B_HEADS, rows, GRID_W, wr * GRID_W).transpose(1, 0, 2, 3)


def _lat_attention_b(z, cache_k, cache_v, bias, layer):
    rows = DEC_SEQ // GRID_W
    cache_spec = pl.BlockSpec((None, None, PAST_LEN, 256), lambda b, r: (b, layer, 0, 0))
    return pl.pallas_call(
        _na_kernel,
        grid=(DEC_BATCH, rows),
        in_specs=[pl.BlockSpec((GRID_W, 256), lambda b, r: (b * rows + r, OFF_BQ // 256)),
                  pl.BlockSpec((DEC_SEQ, 256), lambda b, r: (b, OFF_BK // 256)),
                  pl.BlockSpec((DEC_SEQ, 256), lambda b, r: (b, OFF_BV // 256)),
                  cache_spec, cache_spec,
                  pl.BlockSpec((None, B_HEADS, GRID_W, NA_KEYS), lambda b, r: (r, 0, 0, 0))],
        out_specs=pl.BlockSpec((GRID_W, 256), lambda b, r: (b * rows + r, 0)),
        out_shape=jax.ShapeDtypeStruct((DEC_BATCH * DEC_SEQ, 256), F32),
        compiler_params=_cparams("parallel", "parallel"),
        name="lat_attention_b",
    )(z, z, z, cache_k, cache_v, bias)


def _retention_kernel(q_ref, g_ref, k_ref, v_ref, dec_ref, gn_ref, *rest, seq_len, tq, has_state):
    if has_state:
        s0_ref, o_ref = rest
    else:
        o_ref, st_ref = rest
    lg = jax.nn.log_sigmoid(dec_ref[...])
    i0 = pl.program_id(1) * tq
    qi = (i0 + lax.broadcasted_iota(jnp.int32, (tq, 1), 0)).astype(F32)
    kj = lax.broadcasted_iota(jnp.int32, (1, seq_len), 1).astype(F32)
    diff = qi - kj
    for h in range(C_HEADS):
        sl = slice(h * HEAD_DIM, (h + 1) * HEAD_DIM)
        lgf = lg[h:h + 1, 0:1]
        lgb = lg[C_HEADS + h:C_HEADS + h + 1, 0:1]
        q = q_ref[:, sl]
        k = k_ref[:, sl]
        v = v_ref[:, sl]
        decay = (jnp.where(diff >= 0, jnp.exp(lgf * jnp.maximum(diff, 0.0)), 0.0)
                 + jnp.where(diff <= 0, jnp.exp(lgb * jnp.maximum(-diff, 0.0)), 0.0))
        o = _bdot(_bdot_nt(q, k) * decay, v)
        if has_state:
            o = (o + _bdot(q, s0_ref[0, h]) * jnp.exp(lgf * (qi + 1.0))
                 + _bdot(q, s0_ref[1, h]) * jnp.exp(lgb * (seq_len - qi)))
        mu = jnp.mean(o, axis=-1, keepdims=True)
        var = jnp.mean(jnp.square(o - mu), axis=-1, keepdims=True)
        on = (o - mu) * lax.rsqrt(var + EPS) * gn_ref[:, sl]
        o_ref[:, sl] = on * jax.nn.silu(g_ref[:, sl])
        if not has_state:
            kpos = lax.broadcasted_iota(jnp.int32, (seq_len, 1), 0).astype(F32)
            st_ref[0, h] = _bdot_tn(k * jnp.exp(lgf * (seq_len - 1.0 - kpos)), v)
            st_ref[1, h] = _bdot_tn(k * jnp.exp(lgb * kpos), v)


def _retention(z, dec, gn, s0, layer, *, nb, seq_len, tq=256):
    nq = seq_len // tq
    has_state = s0 is not None
    in_specs = [pl.BlockSpec((tq, 256), lambda b, j: (b * nq + j, OFF_CQ // 256)),
                pl.BlockSpec((tq, 256), lambda b, j: (b * nq + j, OFF_CG // 256)),
                pl.BlockSpec((seq_len, 256), lambda b, j: (b, OFF_CK // 256)),
                pl.BlockSpec((seq_len, 256), lambda b, j: (b, OFF_CV // 256)),
                pl.BlockSpec((SUBLANES, LANES), lambda b, j: (0, 0)),
                pl.BlockSpec((1, 256), lambda b, j: (0, 0))]
    args = [z, z, z, z, dec, gn]
    o_spec = pl.BlockSpec((tq, 256), lambda b, j: (b * nq + j, 0))
    o_shape = jax.ShapeDtypeStruct((nb * seq_len, 256), F32)
    if has_state:
        in_specs.append(pl.BlockSpec((None, None, 2, C_HEADS, HEAD_DIM, HEAD_DIM),
                                     lambda b, j: (b, layer, 0, 0, 0, 0)))
        args.append(s0)
        out_specs, out_shape = o_spec, o_shape
    else:
        assert nq == 1
        out_specs = [o_spec, pl.BlockSpec((None, 2, C_HEADS, HEAD_DIM, HEAD_DIM),
                                          lambda b, j: (b, 0, 0, 0, 0))]
        out_shape = [o_shape, jax.ShapeDtypeStruct((nb, 2, C_HEADS, HEAD_DIM, HEAD_DIM), F32)]
    return pl.pallas_call(
        functools.partial(_retention_kernel, seq_len=seq_len, tq=tq, has_state=has_state),
        grid=(nb, nq),
        in_specs=in_specs,
        out_specs=out_specs,
        out_shape=out_shape,
        compiler_params=_cparams("parallel", "parallel"),
        name="retention",
    )(*args)


def _s5_prep_kernel(lre_ref, lim_ref, ldt_ref, bre_ref, bim_ref, are_ref, aim_ref, bbre_ref, bbim_ref):
    lre = lre_ref[...]
    lim = lim_ref[...]
    dt = jnp.exp(ldt_ref[...])
    mag = jnp.exp(lre * dt)
    a_re = mag * jnp.cos(lim * dt)
    a_im = mag * jnp.sin(lim * dt)
    den = lre * lre + lim * lim
    r_re = ((a_re - 1.0) * lre + a_im * lim) / den
    r_im = (a_im * lre - (a_re - 1.0) * lim) / den
    are_ref[...] = a_re
    aim_ref[...] = a_im
    bbre_ref[...] = r_re * bre_ref[...] - r_im * bim_ref[...]
    bbim_ref[...] = r_re * bim_ref[...] + r_im * bre_ref[...]


def _s5_prepare(lam_re, lam_im, log_dt, b_re, b_im, c_re, c_im):
    lead = (DEPTH, 2, S5_GROUPS)
    full = lead + (S5_CH, S5_STATE)
    rows = DEPTH * 2 * S5_GROUPS * S5_CH

    def expand(t):
        return jnp.broadcast_to(t[:, :, :, None, :], full).reshape(rows, S5_STATE)

    ldt = jnp.broadcast_to(log_dt[:, :, :, None, None], full).reshape(rows, S5_STATE)
    bt = [jnp.swapaxes(t, -1, -2).reshape(rows, S5_STATE) for t in (b_re, b_im)]
    spec = pl.BlockSpec((rows, S5_STATE), lambda: (0, 0))
    a_re, a_im, bb_re, bb_im = pl.pallas_call(
        _s5_prep_kernel,
        in_specs=[spec] * 5,
        out_specs=[spec] * 4,
        out_shape=[jax.ShapeDtypeStruct((rows, S5_STATE), F32)] * 4,
        name="s5_prepare",
    )(expand(lam_re), expand(lam_im), ldt, bt[0], bt[1])
    a = jnp.stack([t.reshape(full)[:, :, :, 0, :].reshape(DEPTH, 2, S5_SP) for t in (a_re, a_im)], axis=2)
    eye = jnp.eye(S5_GROUPS, dtype=F32)

    def in_blockdiag(t):
        t = t.reshape(full)
        return (t[:, :, :, :, None, :] * eye[None, None, :, None, :, None]).reshape(DEPTH, 2, GROUP_WIDTH, S5_SP)

    def out_blockdiag(t):
        t = jnp.swapaxes(t, -1, -2)
        return (t[:, :, :, :, None, :] * eye[None, None, :, None, :, None]).reshape(DEPTH, 2, S5_SP, GROUP_WIDTH)

    bmat = jnp.concatenate([in_blockdiag(bb_re), in_blockdiag(bb_im)], axis=-1).astype(BF16)
    return a, bmat, out_blockdiag(c_re).astype(BF16), out_blockdiag(c_im).astype(BF16)


def _cmul(ar, ai, br, bi):
    return ar * br - ai * bi, ar * bi + ai * br


def _s5_kernel(u_ref, h0_ref, a_ref, bm_ref, cre_ref, cim_ref, dvec_ref, glu_ref, od_ref, fin_ref,
               x_scr, y_scr, *, nseg):
    steps = S5_SEG
    rows = steps * SUBLANES
    chunk = 256
    nchunk = rows // chunk
    seg = lax.broadcasted_iota(jnp.int32, (SUBLANES, S5_SP), 0) % nseg

    for d in range(2):
        def xbody(c, carry):
            r0 = pl.multiple_of(c * chunk, chunk)
            x_scr[pl.ds(r0, chunk), :] = jnp.dot(u_ref[pl.ds(r0, chunk), :].astype(BF16), bm_ref[d],
                                                 preferred_element_type=F32)
            return carry
        lax.fori_loop(0, nchunk, xbody, 0)

        ar = jnp.broadcast_to(a_ref[d, 0:1, :], (SUBLANES, S5_SP))
        ai = jnp.broadcast_to(a_ref[d, 1:2, :], (SUBLANES, S5_SP))

        def scan(init, store):
            def body(t, carry):
                sr, si = carry
                tt = t if d == 0 else steps - 1 - t
                r0 = pl.multiple_of(tt * SUBLANES, SUBLANES)
                pr, pi = _cmul(ar, ai, sr, si)
                nr = pr + x_scr[pl.ds(r0, SUBLANES), 0:S5_SP]
                ni = pi + x_scr[pl.ds(r0, SUBLANES), S5_SP:]
                if store:
                    x_scr[pl.ds(r0, SUBLANES), 0:S5_SP] = nr
                    x_scr[pl.ds(r0, SUBLANES), S5_SP:] = ni
                return nr, ni
            return lax.fori_loop(0, steps, body, init, unroll=4)

        init = (h0_ref[d, :, 0:S5_SP], h0_ref[d, :, S5_SP:])
        if nseg > 1:
            zero = jnp.zeros((SUBLANES, S5_SP), F32)
            fr, fi = scan((zero, zero), store=False)
            pr, pi = ar, ai
            for _ in range(int(math.log2(steps))):
                pr, pi = _cmul(pr, pi, pr, pi)
            cr, ci = init
            shift = 1 if d == 0 else SUBLANES - 1
            order = range(1, nseg) if d == 0 else range(nseg - 2, -1, -1)
            for s in order:
                ncr, nci = pltpu.roll(cr, shift, 0), pltpu.roll(ci, shift, 0)
                nfr, nfi = pltpu.roll(fr, shift, 0), pltpu.roll(fi, shift, 0)
                qr, qi = _cmul(pr, pi, ncr, nci)
                cr = jnp.where(seg == s, qr + nfr, cr)
                ci = jnp.where(seg == s, qi + nfi, ci)
            init = (cr, ci)
        sr, si = scan(init, store=True)
        fin_ref[d, :, 0:S5_SP] = sr
        fin_ref[d, :, S5_SP:] = si

        def ybody(c, carry):
            r0 = pl.multiple_of(c * chunk, chunk)
            y = (_bdot(x_scr[pl.ds(r0, chunk), 0:S5_SP], cre_ref[d])
                 - _bdot(x_scr[pl.ds(r0, chunk), S5_SP:], cim_ref[d]))
            if d == 0:
                y_scr[pl.ds(r0, chunk), :] = y
            else:
                y_scr[pl.ds(r0, chunk), :] += y
            return carry
        lax.fori_loop(0, nchunk, ybody, 0)

    def obody(c, carry):
        r0 = pl.multiple_of(c * chunk, chunk)
        y = y_scr[pl.ds(r0, chunk), :] + dvec_ref[...] * u_ref[pl.ds(r0, chunk), :]
        zz = jax.nn.gelu(y)
        od_ref[pl.ds(r0, chunk), :] = zz * jax.nn.sigmoid(_bdot(zz, glu_ref[...]))
        return carry
    lax.fori_loop(0, nchunk, obody, 0)


def _s5(u_tm, h0, a, bmat, cre, cim, dvec, glu_bf, *, nseg):
    nblk = u_tm.shape[0]
    rows = S5_SEG * SUBLANES
    return pl.pallas_call(
        functools.partial(_s5_kernel, nseg=nseg),
        grid=(nblk,),
        in_specs=[pl.BlockSpec((None, rows, GROUP_WIDTH), lambda i: (i, 0, 0)),
                  pl.BlockSpec((None, 2, SUBLANES, 2 * S5_SP), lambda i: (i, 0, 0, 0)),
                  pl.BlockSpec((2, 2, S5_SP), lambda i: (0, 0, 0)),
                  pl.BlockSpec((2, GROUP_WIDTH, 2 * S5_SP), lambda i: (0, 0, 0)),
                  pl.BlockSpec((2, S5_SP, GROUP_WIDTH), lambda i: (0, 0, 0)),
                  pl.BlockSpec((2, S5_SP, GROUP_WIDTH), lambda i: (0, 0, 0)),
                  pl.BlockSpec((1, GROUP_WIDTH), lambda i: (0, 0)),
                  pl.BlockSpec((GROUP_WIDTH, GROUP_WIDTH), lambda i: (0, 0))],
        out_specs=[pl.BlockSpec((None, rows, GROUP_WIDTH), lambda i: (i, 0, 0)),
                   pl.BlockSpec((None, 2, SUBLANES, 2 * S5_SP), lambda i: (i, 0, 0, 0))],
        out_shape=[jax.ShapeDtypeStruct((nblk, rows, GROUP_WIDTH), F32),
                   jax.ShapeDtypeStruct((nblk, 2, SUBLANES, 2 * S5_SP), F32)],
        scratch_shapes=[pltpu.VMEM((rows, 2 * S5_SP), F32), pltpu.VMEM((rows, GROUP_WIDTH), F32)],
        compiler_params=_cparams("parallel"),
        name="s5",
    )(u_tm, h0, a, bmat, cre, cim, dvec, glu_bf)


def _out_kernel(x_ref, oa_ref, ob_ref, oc_ref, od_ref, mod_ref, wo_ref, g2_ref, wr_ref, br_ref,
                xm_ref, h2_ref, gate_ref):
    mix = functools.reduce(jnp.add, [
        _bdot(o_ref[...], wo_ref[i * GROUP_WIDTH:(i + 1) * GROUP_WIDTH, :])
        for i, o_ref in enumerate((oa_ref, ob_ref, oc_ref, od_ref))])
    xm = x_ref[...] + mod_ref[2:3, :] * mix
    xm_ref[...] = xm
    h2 = _rms_rows(xm) * g2_ref[...] * (1.0 + mod_ref[4:5, :]) + mod_ref[3:4, :]
    h2_ref[...] = h2.astype(BF16)

    logits = _dot3(h2, wr_ref[...]) + br_ref[...]
    lane = lax.broadcasted_iota(jnp.int32, logits.shape, 1)
    big = jnp.int32(2 ** 30)
    gmask = lane < MOE_GROUPS
    gl = jnp.where(gmask, logits, -jnp.inf)
    gmax = jnp.max(gl, axis=-1, keepdims=True)
    p_top = 1.0 / jnp.sum(jnp.exp(gl - gmax), axis=-1, keepdims=True)
    g_top = jnp.min(jnp.where(gl == gmax, lane, big), axis=-1, keepdims=True)
    e_lane = lane - ROUTER_OFF
    emask = (e_lane >= 0) & (e_lane < MOE_EXPERTS) & ((e_lane // MOE_PER_GROUP) == g_top)
    el = jnp.where(emask, logits, -jnp.inf)
    m1 = jnp.max(el, axis=-1, keepdims=True)
    i1 = jnp.min(jnp.where(el == m1, lane, big), axis=-1, keepdims=True)
    el2 = jnp.where(lane == i1, -jnp.inf, el)
    m2 = jnp.max(el2, axis=-1, keepdims=True)
    i2 = jnp.min(jnp.where(el2 == m2, lane, big), axis=-1, keepdims=True)
    e2 = jnp.exp(m2 - m1)
    den = 1.0 + e2
    gate_ref[...] = (jnp.where(lane == i1, (1.0 / den) * p_top, 0.0)
                     + jnp.where(lane == i2, (e2 / den) * p_top, 0.0))


def _output_stage(x, mixes, mod, wo_bf, g2, wr, br, *, tiles_per_mod, tm=256):
    n = x.shape[0]
    row = lambda w: pl.BlockSpec((tm, w), lambda i: (i, 0))
    const = lambda shape: pl.BlockSpec(shape, lambda i: (0,) * len(shape))
    return pl.pallas_call(
        _out_kernel,
        grid=(n // tm,),
        in_specs=[row(D_MODEL), row(256), row(256), row(256), row(256),
                  pl.BlockSpec((None, 6, D_MODEL), lambda i: (i // tiles_per_mod, 0, 0)),
                  const((D_MODEL, D_MODEL)), const((1, D_MODEL)),
                  const((D_MODEL, LANES)), const((1, LANES))],
        out_specs=[row(D_MODEL), row(D_MODEL), row(LANES)],
        out_shape=[jax.ShapeDtypeStruct((n, D_MODEL), F32),
                   jax.ShapeDtypeStruct((n, D_MODEL), BF16),
                   jax.ShapeDtypeStruct((n, LANES), F32)],
        compiler_params=_cparams("parallel"),
        name="output_stage",
    )(x, *mixes, mod, wo_bf, g2, wr, br)


GROUP_HID = MOE_PER_GROUP * MOE_HIDDEN


def _moe_kernel(h2_ref, gate_ref, xm_ref, mod_ref, w1_ref, w3_ref, w2_ref, fg_ref, o_ref, acc_ref, *, final):
    g = pl.program_id(1)
    h2 = h2_ref[...]
    a = jnp.dot(h2, w1_ref[...], preferred_element_type=F32)
    b = jnp.dot(h2, w3_ref[...], preferred_element_type=F32)
    src = lax.broadcasted_iota(jnp.int32, (LANES, GROUP_HID), 0)
    dst = lax.broadcasted_iota(jnp.int32, (LANES, GROUP_HID), 1) // MOE_HIDDEN
    spread = jnp.where(src == ROUTER_OFF + g * MOE_PER_GROUP + dst, 1.0, 0.0).astype(BF16)
    ge = _dot_hilo_lhs(gate_ref[...], spread)
    part = jnp.dot((jax.nn.silu(a) * b * ge).astype(BF16), w2_ref[...], preferred_element_type=F32)

    @pl.when(g == 0)
    def _():
        acc_ref[...] = part

    @pl.when(g > 0)
    def _():
        acc_ref[...] += part

    @pl.when(g == MOE_GROUPS - 1)
    def _():
        out = xm_ref[...] + mod_ref[5:6, :] * acc_ref[...]
        if final:
            out = _rms_rows(out) * fg_ref[...]
        o_ref[...] = out


def _moe(h2, gates, xm, mod, w1g, w3g, w2g, fg, *, tiles_per_mod, final, tm=512):
    n = h2.shape[0]
    row = lambda w: pl.BlockSpec((tm, w), lambda i, g: (i, 0))
    return pl.pallas_call(
        functools.partial(_moe_kernel, final=final),
        grid=(n // tm, MOE_GROUPS),
        in_specs=[row(D_MODEL), row(LANES), row(D_MODEL),
                  pl.BlockSpec((None, 6, D_MODEL), lambda i, g: (i // tiles_per_mod, 0, 0)),
                  pl.BlockSpec((None, D_MODEL, GROUP_HID), lambda i, g: (g, 0, 0)),
                  pl.BlockSpec((None, D_MODEL, GROUP_HID), lambda i, g: (g, 0, 0)),
                  pl.BlockSpec((None, GROUP_HID, D_MODEL), lambda i, g: (g, 0, 0)),
                  pl.BlockSpec((1, D_MODEL), lambda i, g: (0, 0))],
        out_specs=row(D_MODEL),
        out_shape=jax.ShapeDtypeStruct((n, D_MODEL), F32),
        scratch_shapes=[pltpu.VMEM((tm, D_MODEL), F32)],
        compiler_params=_cparams("parallel", "arbitrary"),
        name="moe",
    )(h2, gates, xm, mod, w1g, w3g, w2g, fg)


def _to_time_major(du, nblk):
    t = du.reshape(nblk, SUBLANES, S5_SEG, GROUP_WIDTH)
    return jnp.swapaxes(t, 1, 2).reshape(nblk, S5_SEG * SUBLANES, GROUP_WIDTH)


def _from_time_major(od, nblk):
    t = od.reshape(nblk, S5_SEG, SUBLANES, GROUP_WIDTH)
    return jnp.swapaxes(t, 1, 2).reshape(nblk * SUBLANES * S5_SEG, GROUP_WIDTH)


def kernel(x_prompt, x_sample, cache_a_k, cache_a_v, cache_b_k, cache_b_v, state_ret, state_ssm, c, c_ctx, mod_w, mod_b, norm1_g, norm2_g, w_in, a_qn_g, a_kn_g, b_rel_bias, ret_decay, ret_gn_g, s5_lam_re, s5_lam_im, s5_log_dt, s5_b_re, s5_b_im, s5_c_re, s5_c_im, s5_d, s5_glu_w, w_out, moe_gw, moe_gb, moe_ew, moe_eb, moe_w1, moe_w3, moe_w2, final_norm_g):
    n_ctx = BATCH * SEQ
    n_lat = DEC_BATCH * DEC_SEQ
    lat_seg = DEC_SEQ // S5_SEG

    cond = jnp.zeros((SUBLANES, D_MODEL), F32).at[0].set(c_ctx).at[1:1 + DEC_BATCH].set(c)
    mods = _modulation(cond, mod_w, mod_b).reshape(DEPTH, SUBLANES, 6, D_MODEL)

    rope_tabs = _rope_tables()
    s5_a, s5_bm, s5_cre, s5_cim = _s5_prepare(s5_lam_re, s5_lam_im, s5_log_dt, s5_b_re, s5_b_im,
                                              s5_c_re, s5_c_im)
    cak = cache_a_k.reshape(DEC_BATCH, DEPTH, PAST_LEN, A_KV_HEADS * HEAD_DIM)
    cav = cache_a_v.reshape(DEC_BATCH, DEPTH, PAST_LEN, A_KV_HEADS * HEAD_DIM)
    cbk = cache_b_k.reshape(DEC_BATCH, DEPTH, PAST_LEN, B_HEADS * HEAD_DIM)
    cbv = cache_b_v.reshape(DEC_BATCH, DEPTH, PAST_LEN, B_HEADS * HEAD_DIM)

    xc = x_prompt.reshape(n_ctx, D_MODEL)
    xs = x_sample.reshape(n_lat, D_MODEL)
    new_ak, new_av, new_bk, new_bv, new_ret, new_ssm = [], [], [], [], [], []
    for l in range(DEPTH):
        final = l == DEPTH - 1
        w_in_bf = w_in[l].astype(BF16)
        wo_bf = w_out[l].astype(BF16)
        glu_bf = s5_glu_w[l].astype(BF16)
        g1 = norm1_g[l].reshape(1, D_MODEL)
        g2 = norm2_g[l].reshape(1, D_MODEL)
        fg = final_norm_g.reshape(1, D_MODEL)
        qn = jnp.tile(a_qn_g[l], A_HEADS).reshape(1, 256)
        kn = jnp.tile(a_kn_g[l], A_KV_HEADS).reshape(1, 128)
        dec = jnp.broadcast_to(ret_decay[l].reshape(2 * C_HEADS, 1), (2 * C_HEADS, LANES))
        gn = ret_gn_g[l].reshape(1, 256)
        dvec = s5_d[l].reshape(1, GROUP_WIDTH)
        wr = jnp.zeros((D_MODEL, LANES), F32).at[:, :MOE_GROUPS].set(moe_gw[l]).at[
            :, ROUTER_OFF:ROUTER_OFF + MOE_EXPERTS].set(moe_ew[l])
        br = jnp.zeros((1, LANES), F32).at[0, :MOE_GROUPS].set(moe_gb[l]).at[
            0, ROUTER_OFF:ROUTER_OFF + MOE_EXPERTS].set(moe_eb[l])
        w1g = moe_w1[l].reshape(MOE_GROUPS, MOE_PER_GROUP, D_MODEL, MOE_HIDDEN).transpose(0, 2, 1, 3).reshape(
            MOE_GROUPS, D_MODEL, GROUP_HID).astype(BF16)
        w3g = moe_w3[l].reshape(MOE_GROUPS, MOE_PER_GROUP, D_MODEL, MOE_HIDDEN).transpose(0, 2, 1, 3).reshape(
            MOE_GROUPS, D_MODEL, GROUP_HID).astype(BF16)
        w2g = moe_w2[l].reshape(MOE_GROUPS, GROUP_HID, D_MODEL).astype(BF16)
        na_bias = _na_bias(b_rel_bias[l])
        mod_c = mods[l, 0:1]
        mod_s = mods[l, 1:1 + DEC_BATCH]

        tiles_c = n_ctx // 256
        zc = _project(xc, mod_c, g1, w_in_bf, qn, kn, None, seq_len=SEQ, tiles_per_mod=tiles_c)
        oa, ob = _ctx_attention(zc, BATCH, SEQ)
        oc, ret_state = _retention(zc, dec, gn, None, l, nb=BATCH, seq_len=SEQ)
        nblk = BATCH // SUBLANES
        od_tm, fin = _s5(_to_time_major(zc[:, OFF_DU:], nblk),
                         jnp.zeros((nblk, 2, SUBLANES, 2 * S5_SP), F32),
                         s5_a[l], s5_bm[l], s5_cre[l], s5_cim[l], dvec, glu_bf, nseg=1)
        od = _from_time_major(od_tm, nblk)
        xm, h2, gates = _output_stage(xc, (oa, ob, oc, od), mod_c, wo_bf, g2, wr, br, tiles_per_mod=tiles_c)
        xc = _moe(h2, gates, xm, mod_c, w1g, w3g, w2g, fg, tiles_per_mod=n_ctx // 512, final=final)
        new_ak.append(zc[:, OFF_AK:OFF_AV].reshape(BATCH, SEQ, A_KV_HEADS, HEAD_DIM))
        new_av.append(zc[:, OFF_AV:OFF_BQ].reshape(BATCH, SEQ, A_KV_HEADS, HEAD_DIM))
        new_bk.append(zc[:, OFF_BK:OFF_BV].reshape(BATCH, SEQ, B_HEADS, HEAD_DIM))
        new_bv.append(zc[:, OFF_BV:OFF_CQ].reshape(BATCH, SEQ, B_HEADS, HEAD_DIM))
        new_ret.append(ret_state)
        new_ssm.append(fin.reshape(nblk, 2, SUBLANES, 2, S5_GROUPS, S5_STATE).transpose(0, 2, 1, 3, 4, 5).reshape(
            BATCH, 2, 2, S5_GROUPS, S5_STATE))

        zs = _project(xs, mod_s, g1, w_in_bf, qn, kn, rope_tabs, seq_len=DEC_SEQ, tiles_per_mod=DEC_SEQ // 256)
        oa = _lat_attention_a(zs, cak, cav, l)
        ob = _lat_attention_b(zs, cbk, cbv, na_bias, l)
        oc = _retention(zs, dec, gn, state_ret, l, nb=DEC_BATCH, seq_len=DEC_SEQ)
        h0 = state_ssm[:, l].reshape(DEC_BATCH, 2, 2 * S5_SP).transpose(1, 0, 2)
        h0_seg = jnp.zeros((2, DEC_BATCH, lat_seg, 2 * S5_SP), F32)
        h0_seg = h0_seg.at[0, :, 0].set(h0[0]).at[1, :, lat_seg - 1].set(h0[1])
        od_tm, _ = _s5(_to_time_major(zs[:, OFF_DU:], 1), h0_seg.reshape(1, 2, SUBLANES, 2 * S5_SP),
                       s5_a[l], s5_bm[l], s5_cre[l], s5_cim[l], dvec, glu_bf, nseg=lat_seg)
        od = _from_time_major(od_tm, 1)
        xm, h2, gates = _output_stage(xs, (oa, ob, oc, od), mod_s, wo_bf, g2, wr, br,
                                      tiles_per_mod=DEC_SEQ // 256)
        xs = _moe(h2, gates, xm, mod_s, w1g, w3g, w2g, fg, tiles_per_mod=DEC_SEQ // 512, final=final)

    return (xc.reshape(BATCH, SEQ, D_MODEL), xs.reshape(DEC_BATCH, DEC_SEQ, D_MODEL),
            jnp.stack(new_ak, axis=1), jnp.stack(new_av, axis=1),
            jnp.stack(new_bk, axis=1), jnp.stack(new_bv, axis=1),
            jnp.stack(new_ret, axis=1), jnp.stack(new_ssm, axis=1))
```

```python
import functools
import math

import numpy as np
import jax
import jax.numpy as jnp
from jax import lax
from jax.experimental import pallas as pl
from jax.experimental.pallas import tpu as pltpu

F32 = jnp.float32
BF16 = jnp.bfloat16

D_MODEL = 1024
BATCH = 32
SEQ = 256
DEPTH = 2
DEC_BATCH = 2
DEC_SEQ = 1024
PAST_LEN = 256
GRID_W = 64
HEAD_DIM = 64
GROUP_WIDTH = 256
A_HEADS = 4
A_KV_HEADS = 2
B_HEADS = 4
NA_ROWS = 8
NA_COLS = 16
C_HEADS = 4
S5_CH = 16
S5_GROUPS = 16
S5_STATE = 64
MOE_GROUPS = 4
MOE_PER_GROUP = 8
MOE_EXPERTS = 32
MOE_HIDDEN = 128
ROPE_THETA = 10000.0
EPS = 1e-6
IN_WIDTH = 2560
Q_SCALE = HEAD_DIM ** -0.5

OFF_AQ, OFF_AK, OFF_AV = 0, 256, 384
OFF_BQ, OFF_BK, OFF_BV = 512, 768, 1024
OFF_CQ, OFF_CK, OFF_CV, OFF_CG = 1280, 1536, 1792, 2048
OFF_DU = 2304

LANES = 128
SUBLANES = 8
S5_SP = S5_GROUPS * S5_STATE
S5_SEG = 256
ROUTER_OFF = 4
NEG_BIG = -1e30
VMEM_LIMIT = 56 * 1024 * 1024


def _cparams(*sem):
    return pltpu.CompilerParams(dimension_semantics=sem, vmem_limit_bytes=VMEM_LIMIT)


def _bdot(a, b):
    return jnp.dot(a.astype(BF16), b.astype(BF16), preferred_element_type=F32)


def _bdot_nt(a, b):
    return lax.dot_general(a.astype(BF16), b.astype(BF16), (((1,), (1,)), ((), ())),
                           preferred_element_type=F32)


def _bdot_tn(a, b):
    return lax.dot_general(a.astype(BF16), b.astype(BF16), (((0,), (0,)), ((), ())),
                           preferred_element_type=F32)


def _split(a):
    hi = a.astype(BF16)
    lo = (a - hi.astype(F32)).astype(BF16)
    return hi, lo


def _dot_hilo_lhs(a, b_bf16):
    hi, lo = _split(a)
    return (jnp.dot(hi, b_bf16, preferred_element_type=F32)
            + jnp.dot(lo, b_bf16, preferred_element_type=F32))


def _dot3(a, b):
    ah, al = _split(a)
    bh, bl = _split(b)
    return (jnp.dot(ah, bh, preferred_element_type=F32)
            + jnp.dot(ah, bl, preferred_element_type=F32)
            + jnp.dot(al, bh, preferred_element_type=F32))


def _rms_rows(x):
    return x * lax.rsqrt(jnp.mean(x * x, axis=-1, keepdims=True) + EPS)


def _mod_kernel(cond_ref, w_ref, b_ref, o_ref):
    o_ref[...] = _bdot(jax.nn.silu(cond_ref[...]), w_ref[...]) + b_ref[...]


def _modulation(cond, mod_w, mod_b):
    tn = 1536
    return pl.pallas_call(
        _mod_kernel,
        grid=(DEPTH, 6 * D_MODEL // tn),
        in_specs=[pl.BlockSpec((SUBLANES, D_MODEL), lambda l, j: (0, 0)),
                  pl.BlockSpec((None, D_MODEL, tn), lambda l, j: (l, 0, j)),
                  pl.BlockSpec((None, 1, tn), lambda l, j: (l, 0, j))],
        out_specs=pl.BlockSpec((None, SUBLANES, tn), lambda l, j: (l, 0, j)),
        out_shape=jax.ShapeDtypeStruct((DEPTH, SUBLANES, 6 * D_MODEL), F32),
        compiler_params=_cparams("arbitrary", "arbitrary"),
        name="modulation",
    )(cond, mod_w, mod_b.reshape(DEPTH, 1, 6 * D_MODEL))


def _head_norm(t, g):
    w = t.shape[1]
    ri = lax.broadcasted_iota(jnp.int32, (w, w), 0) // HEAD_DIM
    ci = lax.broadcasted_iota(jnp.int32, (w, w), 1) // HEAD_DIM
    gm = jnp.where(ri == ci, 1.0 / HEAD_DIM, 0.0).astype(BF16)
    ms = _dot_hilo_lhs(t * t, gm)
    return t * lax.rsqrt(ms + EPS) * g


def _rope(t, cos, sa, sb):
    return (t * cos + pltpu.roll(t, LANES - 16, 1) * sa + pltpu.roll(t, 16, 1) * sb)


def _proj_kernel(*refs, rope, n_alias, with_cache):
    x_ref, mod_ref, g1_ref, w_ref, qn_ref, kn_ref = refs[:6]
    n_in = 6
    if rope:
        cos_ref, sa_ref, sb_ref = refs[6:9]
        n_in = 9
    outs = refs[n_in + n_alias:]
    z_ref, du_ref = outs[:2]
    h = _rms_rows(x_ref[...]) * g1_ref[...] * (1.0 + mod_ref[1:2, :]) + mod_ref[0:1, :]
    z = jnp.dot(h.astype(BF16), w_ref[...], preferred_element_type=F32)
    aq = _head_norm(z[:, OFF_AQ:OFF_AK], qn_ref[...])
    ak = _head_norm(z[:, OFF_AK:OFF_AV], kn_ref[...])
    for j in range(3):
        t = aq[:, j * LANES:(j + 1) * LANES] if j < 2 else ak
        if rope:
            cj = 0 if j == 2 else j
            sl = slice(cj * LANES, (cj + 1) * LANES)
            t = _rope(t, cos_ref[:, sl], sa_ref[:, sl], sb_ref[:, sl])
        z_ref[:, j * LANES:(j + 1) * LANES] = t
    z_ref[:, OFF_AV:OFF_CK] = z[:, OFF_AV:OFF_CK]
    z_ref[:, OFF_CK:OFF_CV] = z[:, OFF_CK:OFF_CV] * Q_SCALE
    z_ref[:, OFF_CV:OFF_DU] = z[:, OFF_CV:OFF_DU]
    du_ref[...] = z[:, OFF_DU:]
    if with_cache:
        ak_ref, av_ref, bk_ref, bv_ref = outs[2:6]
        ak_ref[...] = ak
        av_ref[...] = z[:, OFF_AV:OFF_BQ]
        bk_ref[...] = z[:, OFF_BK:OFF_BV]
        bv_ref[...] = z[:, OFF_BV:OFF_CQ]


def _du_spec(grid_rank):
    if grid_rank == 1:
        return pl.BlockSpec((None, S5_SEG, GROUP_WIDTH), lambda i: (i // SUBLANES, 0, i % SUBLANES))
    return pl.BlockSpec((None, S5_SEG, GROUP_WIDTH), lambda i, g: (i // SUBLANES, 0, i % SUBLANES))


def _project(x, mod, g1, w_in_bf, qn, kn, rope_tabs, *, seq_len, tiles_per_mod, cache_layer=None,
             prev_caches=None):
    tm = S5_SEG
    n = x.shape[0]
    rope = rope_tabs is not None
    with_cache = cache_layer is not None
    in_specs = [pl.BlockSpec((tm, D_MODEL), lambda i: (i, 0)),
                pl.BlockSpec((None, 6, D_MODEL), lambda i: (i // tiles_per_mod, 0, 0)),
                pl.BlockSpec((1, D_MODEL), lambda i: (0, 0)),
                pl.BlockSpec((D_MODEL, IN_WIDTH), lambda i: (0, 0)),
                pl.BlockSpec((1, 256), lambda i: (0, 0)),
                pl.BlockSpec((1, 128), lambda i: (0, 0))]
    args = [x, mod, g1, w_in_bf, qn, kn]
    if rope:
        tps = seq_len // tm
        in_specs += [pl.BlockSpec((tm, 256), lambda i: (i % tps, 0))] * 3
        args += list(rope_tabs)
    out_specs = [pl.BlockSpec((tm, OFF_DU), lambda i: (i, 0)), _du_spec(1)]
    out_shape = [jax.ShapeDtypeStruct((n, OFF_DU), F32),
                 jax.ShapeDtypeStruct((n // (tm * SUBLANES), S5_SEG, SUBLANES * GROUP_WIDTH), F32)]
    aliases = {}
    n_alias = 0
    if with_cache:
        assert tm == seq_len
        nb = n // seq_len
        for w in (128, 128, 256, 256):
            out_specs.append(pl.BlockSpec((None, None, seq_len, w), lambda i: (i, cache_layer, 0, 0)))
            out_shape.append(jax.ShapeDtypeStruct((nb, DEPTH, seq_len, w), F32))
        if prev_caches is not None:
            n_alias = len(prev_caches)
            for k, arr in enumerate(prev_caches):
                aliases[len(args)] = 2 + k
                in_specs.append(pl.BlockSpec(memory_space=pl.ANY))
                args.append(arr)
    return pl.pallas_call(
        functools.partial(_proj_kernel, rope=rope, n_alias=n_alias, with_cache=with_cache),
        grid=(n // tm,),
        in_specs=in_specs,
        out_specs=out_specs,
        out_shape=out_shape,
        input_output_aliases=aliases,
        compiler_params=_cparams("parallel"),
        name="project",
    )(*args)


def _rope_tables():
    t = jnp.arange(DEC_SEQ)
    row = (t // GRID_W).astype(F32)
    col = (t % GRID_W).astype(F32)
    nf = HEAD_DIM // 4
    inv = ROPE_THETA ** (-jnp.arange(nf, dtype=F32) / nf)
    ang_r = row[:, None] * inv[None, :]
    ang_c = col[:, None] * inv[None, :]
    zeros = jnp.zeros_like(ang_r)
    cos = jnp.concatenate([jnp.cos(ang_r), jnp.cos(ang_r), jnp.cos(ang_c), jnp.cos(ang_c)], axis=-1)
    sa = jnp.concatenate([-jnp.sin(ang_r), zeros, -jnp.sin(ang_c), zeros], axis=-1)
    sb = jnp.concatenate([zeros, jnp.sin(ang_r), zeros, jnp.sin(ang_c)], axis=-1)
    return tuple(jnp.tile(a, (1, 4)) for a in (cos, sa, sb))


def _softmax_pv(scores, values):
    m = functools.reduce(jnp.maximum, [jnp.max(s, axis=-1, keepdims=True) for s in scores])
    ps = [jnp.exp(s - m) for s in scores]
    denom = functools.reduce(jnp.add, [jnp.sum(p, axis=-1, keepdims=True) for p in ps])
    o = functools.reduce(jnp.add, [_bdot(p, v) for p, v in zip(ps, values)])
    return o / denom


def _head(ref, h):
    return ref[:, h * HEAD_DIM:(h + 1) * HEAD_DIM]


def _ctx_attn_kernel(aq_ref, ak_ref, av_ref, bq_ref, bk_ref, bv_ref, oa_ref, ob_ref):
    for q_ref, k_ref, v_ref, o_ref, hkv in ((aq_ref, ak_ref, av_ref, oa_ref, A_KV_HEADS),
                                            (bq_ref, bk_ref, bv_ref, ob_ref, B_HEADS)):
        grp = 4 // hkv
        for h in range(4):
            s = _bdot_nt(_head(q_ref, h) * Q_SCALE, _head(k_ref, h // grp))
            o_ref[:, h * HEAD_DIM:(h + 1) * HEAD_DIM] = _softmax_pv([s], [_head(v_ref, h // grp)])


def _ctx_attention(z, nb, seq_len):
    def col(width, off):
        return pl.BlockSpec((seq_len, width), lambda b: (b, off // width))
    return pl.pallas_call(
        _ctx_attn_kernel,
        grid=(nb,),
        in_specs=[col(256, OFF_AQ), col(128, OFF_AK), col(128, OFF_AV),
                  col(256, OFF_BQ), col(256, OFF_BK), col(256, OFF_BV)],
        out_specs=[pl.BlockSpec((seq_len, 256), lambda b: (b, 0))] * 2,
        out_shape=[jax.ShapeDtypeStruct((nb * seq_len, 256), F32)] * 2,
        compiler_params=_cparams("parallel"),
        name="ctx_attention",
    )(z, z, z, z, z, z)


def _lat_attn_a_kernel(q_ref, kn_ref, vn_ref, kc_ref, vc_ref, o_ref):
    grp = A_HEADS // A_KV_HEADS
    for h in range(A_HEADS):
        q = _head(q_ref, h) * Q_SCALE
        kv = h // grp
        s_c = _bdot_nt(q, _head(kc_ref, kv))
        s_n = _bdot_nt(q, _head(kn_ref, kv))
        o_ref[:, h * HEAD_DIM:(h + 1) * HEAD_DIM] = _softmax_pv(
            [s_c, s_n], [_head(vc_ref, kv), _head(vn_ref, kv)])


def _lat_attention_a(z, cache_k, cache_v, layer, tq=256):
    nq = DEC_SEQ // tq
    cache_spec = pl.BlockSpec((None, None, PAST_LEN, 128), lambda b, j: (b, layer, 0, 0))
    return pl.pallas_call(
        _lat_attn_a_kernel,
        grid=(DEC_BATCH, nq),
        in_specs=[pl.BlockSpec((tq, 256), lambda b, j: (b * nq + j, OFF_AQ // 256)),
                  pl.BlockSpec((DEC_SEQ, 128), lambda b, j: (b, OFF_AK // 128)),
                  pl.BlockSpec((DEC_SEQ, 128), lambda b, j: (b, OFF_AV // 128)),
                  cache_spec, cache_spec],
        out_specs=pl.BlockSpec((tq, 256), lambda b, j: (b * nq + j, 0)),
        out_shape=jax.ShapeDtypeStruct((DEC_BATCH * DEC_SEQ, 256), F32),
        compiler_params=_cparams("parallel", "parallel"),
        name="lat_attention_a",
    )(z, z, z, cache_k, cache_v)


NA_KEYS = NA_ROWS * GRID_W


NA_PAIRS = 2 * NA_ROWS - 2


def _na_kernel(q_ref, k_ref, v_ref, kc_ref, vc_ref, bias_ref, o_ref):
    r = pl.program_id(1)
    rows = DEC_SEQ // GRID_W
    row_start = jnp.clip(r - NA_ROWS // 2, 0, rows - NA_ROWS)
    start = pl.multiple_of(row_start * GRID_W, GRID_W)
    rel0 = row_start - r + NA_ROWS - 1
    kl = k_ref[pl.ds(start, NA_KEYS), :]
    vl = v_ref[pl.ds(start, NA_KEYS), :]
    for h in range(B_HEADS):
        sl = slice(h * HEAD_DIM, (h + 1) * HEAD_DIM)
        q = q_ref[:, sl] * Q_SCALE
        bias = jnp.concatenate([bias_ref[h, rel0 + 2 * jp] for jp in range(NA_ROWS // 2)], axis=1)
        s_loc = _bdot_nt(q, kl[:, sl]) + bias
        s_ctx = _bdot_nt(q, kc_ref[:, sl])
        o_ref[:, sl] = _softmax_pv([s_loc, s_ctx], [vl[:, sl], vc_ref[:, sl]])


def _na_bias(rel_bias):
    nrel = 2 * NA_COLS - 1
    period = 2 * GRID_W
    b = rel_bias.astype(F32)
    ext = jnp.concatenate([b[..., NA_COLS - 1:],
                           jnp.zeros(b.shape[:-1] + (period - nrel,), F32),
                           b[..., :NA_COLS - 1]], axis=-1)
    flat = jnp.tile(ext, (1, 1, GRID_W))[..., :GRID_W * (period - 1)]
    toe = flat.reshape(b.shape[:-1] + (GRID_W, period - 1))[..., :GRID_W]
    col_start = np.clip(np.arange(GRID_W) - NA_COLS // 2, 0, GRID_W - NA_COLS)
    kc = np.arange(GRID_W)
    inside = (kc[None, :] >= col_start[:, None]) & (kc[None, :] < col_start[:, None] + NA_COLS)
    toe = jnp.where(jnp.asarray(inside), toe, NEG_BIG)
    return jnp.concatenate([toe[:, :-1], toe[:, 1:]], axis=-1)


def _lat_attention_b(z, cache_k, cache_v, bias, layer):
    rows = DEC_SEQ // GRID_W
    cache_spec = pl.BlockSpec((None, None, PAST_LEN, 256), lambda b, r: (b, layer, 0, 0))
    return pl.pallas_call(
        _na_kernel,
        grid=(DEC_BATCH, rows),
        in_specs=[pl.BlockSpec((GRID_W, 256), lambda b, r: (b * rows + r, OFF_BQ // 256)),
                  pl.BlockSpec((DEC_SEQ, 256), lambda b, r: (b, OFF_BK // 256)),
                  pl.BlockSpec((DEC_SEQ, 256), lambda b, r: (b, OFF_BV // 256)),
                  cache_spec, cache_spec,
                  pl.BlockSpec((B_HEADS, NA_PAIRS, GRID_W, 2 * GRID_W), lambda b, r: (0, 0, 0, 0))],
        out_specs=pl.BlockSpec((GRID_W, 256), lambda b, r: (b * rows + r, 0)),
        out_shape=jax.ShapeDtypeStruct((DEC_BATCH * DEC_SEQ, 256), F32),
        compiler_params=_cparams("parallel", "parallel"),
        name="lat_attention_b",
    )(z, z, z, cache_k, cache_v, bias)


def _retention_kernel(q_ref, g_ref, k_ref, v_ref, dec_ref, gn_ref, *rest, seq_len, tq, has_state):
    if has_state:
        s0_ref, o_ref = rest
    else:
        o_ref, st_ref = rest[-2:]
    lg = jax.nn.log_sigmoid(dec_ref[...])
    i0 = pl.program_id(1) * tq
    qi = (i0 + lax.broadcasted_iota(jnp.int32, (tq, 1), 0)).astype(F32)
    kj = lax.broadcasted_iota(jnp.int32, (1, seq_len), 1).astype(F32)
    diff = qi - kj
    for h in range(C_HEADS):
        sl = slice(h * HEAD_DIM, (h + 1) * HEAD_DIM)
        lgf = lg[h:h + 1, 0:1]
        lgb = lg[C_HEADS + h:C_HEADS + h + 1, 0:1]
        q = q_ref[:, sl]
        k = k_ref[:, sl]
        v = v_ref[:, sl]
        decay = (jnp.where(diff >= 0, jnp.exp(lgf * jnp.maximum(diff, 0.0)), 0.0)
                 + jnp.where(diff <= 0, jnp.exp(lgb * jnp.maximum(-diff, 0.0)), 0.0))
        o = _bdot(_bdot_nt(q, k) * decay, v)
        if has_state:
            o = (o + _bdot(q, s0_ref[0, h]) * jnp.exp(lgf * (qi + 1.0))
                 + _bdot(q, s0_ref[1, h]) * jnp.exp(lgb * (seq_len - qi)))
        mu = jnp.mean(o, axis=-1, keepdims=True)
        var = jnp.mean(jnp.square(o - mu), axis=-1, keepdims=True)
        on = (o - mu) * lax.rsqrt(var + EPS) * gn_ref[:, sl]
        o_ref[:, sl] = on * jax.nn.silu(g_ref[:, sl])
        if not has_state:
            kpos = lax.broadcasted_iota(jnp.int32, (seq_len, 1), 0).astype(F32)
            st_ref[0, h] = _bdot_tn(k * jnp.exp(lgf * (seq_len - 1.0 - kpos)), v)
            st_ref[1, h] = _bdot_tn(k * jnp.exp(lgb * kpos), v)


def _retention(z, dec, gn, s0, layer, *, nb, seq_len, prev_state=None, tq=256):
    nq = seq_len // tq
    has_state = s0 is not None
    aliases = {}
    in_specs = [pl.BlockSpec((tq, 256), lambda b, j: (b * nq + j, OFF_CQ // 256)),
                pl.BlockSpec((tq, 256), lambda b, j: (b * nq + j, OFF_CG // 256)),
                pl.BlockSpec((seq_len, 256), lambda b, j: (b, OFF_CK // 256)),
                pl.BlockSpec((seq_len, 256), lambda b, j: (b, OFF_CV // 256)),
                pl.BlockSpec((SUBLANES, LANES), lambda b, j: (0, 0)),
                pl.BlockSpec((1, 256), lambda b, j: (0, 0))]
    args = [z, z, z, z, dec, gn]
    o_spec = pl.BlockSpec((tq, 256), lambda b, j: (b * nq + j, 0))
    o_shape = jax.ShapeDtypeStruct((nb * seq_len, 256), F32)
    if has_state:
        in_specs.append(pl.BlockSpec((None, None, 2, C_HEADS, HEAD_DIM, HEAD_DIM),
                                     lambda b, j: (b, layer, 0, 0, 0, 0)))
        args.append(s0)
        out_specs, out_shape = o_spec, o_shape
    else:
        assert nq == 1
        out_specs = [o_spec, pl.BlockSpec((None, None, 2, C_HEADS, HEAD_DIM, HEAD_DIM),
                                          lambda b, j: (b, layer, 0, 0, 0, 0))]
        out_shape = [o_shape, jax.ShapeDtypeStruct((nb, DEPTH, 2, C_HEADS, HEAD_DIM, HEAD_DIM), F32)]
        if prev_state is not None:
            aliases[len(args)] = 1
            in_specs.append(pl.BlockSpec(memory_space=pl.ANY))
            args.append(prev_state)
    return pl.pallas_call(
        functools.partial(_retention_kernel, seq_len=seq_len, tq=tq, has_state=has_state),
        grid=(nb, nq),
        in_specs=in_specs,
        out_specs=out_specs,
        out_shape=out_shape,
        input_output_aliases=aliases,
        compiler_params=_cparams("parallel", "parallel"),
        name="retention",
    )(*args)


def _s5_prep_kernel(lre_ref, lim_ref, ldt_ref, bre_ref, bim_ref, are_ref, aim_ref, bbre_ref, bbim_ref):
    lre = lre_ref[...]
    lim = lim_ref[...]
    dt = jnp.exp(ldt_ref[...])
    mag = jnp.exp(lre * dt)
    a_re = mag * jnp.cos(lim * dt)
    a_im = mag * jnp.sin(lim * dt)
    den = lre * lre + lim * lim
    r_re = ((a_re - 1.0) * lre + a_im * lim) / den
    r_im = (a_im * lre - (a_re - 1.0) * lim) / den
    are_ref[...] = a_re
    aim_ref[...] = a_im
    bbre_ref[...] = r_re * bre_ref[...] - r_im * bim_ref[...]
    bbim_ref[...] = r_re * bim_ref[...] + r_im * bre_ref[...]


def _s5_prepare(lam_re, lam_im, log_dt, b_re, b_im, c_re, c_im):
    lead = (DEPTH, 2, S5_GROUPS)
    full = lead + (S5_CH, S5_STATE)
    rows = DEPTH * 2 * S5_GROUPS * S5_CH

    def expand(t):
        return jnp.broadcast_to(t[:, :, :, None, :], full).reshape(rows, S5_STATE)

    ldt = jnp.broadcast_to(log_dt[:, :, :, None, None], full).reshape(rows, S5_STATE)
    bt = [jnp.swapaxes(t, -1, -2).reshape(rows, S5_STATE) for t in (b_re, b_im)]
    spec = pl.BlockSpec((rows, S5_STATE), lambda: (0, 0))
    a_re, a_im, bb_re, bb_im = pl.pallas_call(
        _s5_prep_kernel,
        in_specs=[spec] * 5,
        out_specs=[spec] * 4,
        out_shape=[jax.ShapeDtypeStruct((rows, S5_STATE), F32)] * 4,
        name="s5_prepare",
    )(expand(lam_re), expand(lam_im), ldt, bt[0], bt[1])
    a = jnp.stack([t.reshape(full)[:, :, :, 0, :].reshape(DEPTH, 2, S5_SP) for t in (a_re, a_im)], axis=2)
    eye = jnp.eye(S5_GROUPS, dtype=F32)

    def in_blockdiag(t):
        t = t.reshape(full)
        return (t[:, :, :, :, None, :] * eye[None, None, :, None, :, None]).reshape(DEPTH, 2, GROUP_WIDTH, S5_SP)

    def out_blockdiag(t):
        t = jnp.swapaxes(t, -1, -2)
        return (t[:, :, :, :, None, :] * eye[None, None, :, None, :, None]).reshape(DEPTH, 2, S5_SP, GROUP_WIDTH)

    bmat = jnp.concatenate([in_blockdiag(bb_re), in_blockdiag(bb_im)], axis=-1).astype(BF16)
    return a, bmat, out_blockdiag(c_re).astype(BF16), out_blockdiag(c_im).astype(BF16)


def _cmul(ar, ai, br, bi):
    return ar * br - ai * bi, ar * bi + ai * br


def _s5_kernel(u_ref, h0_ref, a_ref, bm_ref, cre_ref, cim_ref, dvec_ref, glu_ref, *rest, nseg):
    od_ref, fin_ref, x_scr, y_scr = rest[-4:]
    steps = S5_SEG
    rows = steps * SUBLANES
    chunk = 256
    nchunk = rows // chunk
    seg = lax.broadcasted_iota(jnp.int32, (SUBLANES, S5_SP), 0) % nseg

    for d in range(2):
        def xbody(c, carry):
            r0 = pl.multiple_of(c * chunk, chunk)
            x_scr[pl.ds(r0, chunk), :] = jnp.dot(u_ref[pl.ds(r0, chunk), :].astype(BF16), bm_ref[d],
                                                 preferred_element_type=F32)
            return carry
        lax.fori_loop(0, nchunk, xbody, 0)

        ar = jnp.broadcast_to(a_ref[d, 0:1, :], (SUBLANES, S5_SP))
        ai = jnp.broadcast_to(a_ref[d, 1:2, :], (SUBLANES, S5_SP))

        def scan(init, store):
            def body(t, carry):
                sr, si = carry
                tt = t if d == 0 else steps - 1 - t
                r0 = pl.multiple_of(tt * SUBLANES, SUBLANES)
                pr, pi = _cmul(ar, ai, sr, si)
                nr = pr + x_scr[pl.ds(r0, SUBLANES), 0:S5_SP]
                ni = pi + x_scr[pl.ds(r0, SUBLANES), S5_SP:]
                if store:
                    x_scr[pl.ds(r0, SUBLANES), 0:S5_SP] = nr
                    x_scr[pl.ds(r0, SUBLANES), S5_SP:] = ni
                return nr, ni
            return lax.fori_loop(0, steps, body, init, unroll=4)

        init = (h0_ref[d, :, 0:S5_SP], h0_ref[d, :, S5_SP:])
        if nseg > 1:
            zero = jnp.zeros((SUBLANES, S5_SP), F32)
            fr, fi = scan((zero, zero), store=False)
            pr, pi = ar, ai
            for _ in range(int(math.log2(steps))):
                pr, pi = _cmul(pr, pi, pr, pi)
            cr, ci = init
            shift = 1 if d == 0 else SUBLANES - 1
            order = range(1, nseg) if d == 0 else range(nseg - 2, -1, -1)
            for s in order:
                ncr, nci = pltpu.roll(cr, shift, 0), pltpu.roll(ci, shift, 0)
                nfr, nfi = pltpu.roll(fr, shift, 0), pltpu.roll(fi, shift, 0)
                qr, qi = _cmul(pr, pi, ncr, nci)
                cr = jnp.where(seg == s, qr + nfr, cr)
                ci = jnp.where(seg == s, qi + nfi, ci)
            init = (cr, ci)
        sr, si = scan(init, store=True)
        fin_ref[:, 2 * d * S5_SP:(2 * d + 1) * S5_SP] = sr
        fin_ref[:, (2 * d + 1) * S5_SP:(2 * d + 2) * S5_SP] = si

        def ybody(c, carry):
            r0 = pl.multiple_of(c * chunk, chunk)
            y = (_bdot(x_scr[pl.ds(r0, chunk), 0:S5_SP], cre_ref[d])
                 - _bdot(x_scr[pl.ds(r0, chunk), S5_SP:], cim_ref[d]))
            if d == 0:
                y_scr[pl.ds(r0, chunk), :] = y
            else:
                y_scr[pl.ds(r0, chunk), :] += y
            return carry
        lax.fori_loop(0, nchunk, ybody, 0)

    def obody(c, carry):
        r0 = pl.multiple_of(c * chunk, chunk)
        y = y_scr[pl.ds(r0, chunk), :] + dvec_ref[...] * u_ref[pl.ds(r0, chunk), :]
        zz = jax.nn.gelu(y)
        od_ref[pl.ds(r0, chunk), :] = zz * jax.nn.sigmoid(_bdot(zz, glu_ref[...]))
        return carry
    lax.fori_loop(0, nchunk, obody, 0)


def _s5(du_tm, h0, a, bmat, cre, cim, dvec, glu_bf, *, nseg, fin_layer=0, fin_layers=1, prev_fin=None):
    nblk = du_tm.shape[0]
    rows = S5_SEG * SUBLANES
    fin_w = 4 * S5_SP
    in_specs = [pl.BlockSpec((None, rows, GROUP_WIDTH), lambda i: (i, 0, 0)),
                pl.BlockSpec((2, SUBLANES, 2 * S5_SP), lambda i: (0, 0, 0)),
                pl.BlockSpec((2, 2, S5_SP), lambda i: (0, 0, 0)),
                pl.BlockSpec((2, GROUP_WIDTH, 2 * S5_SP), lambda i: (0, 0, 0)),
                pl.BlockSpec((2, S5_SP, GROUP_WIDTH), lambda i: (0, 0, 0)),
                pl.BlockSpec((2, S5_SP, GROUP_WIDTH), lambda i: (0, 0, 0)),
                pl.BlockSpec((1, GROUP_WIDTH), lambda i: (0, 0)),
                pl.BlockSpec((GROUP_WIDTH, GROUP_WIDTH), lambda i: (0, 0))]
    args = [du_tm.reshape(nblk, rows, GROUP_WIDTH), h0, a, bmat, cre, cim, dvec, glu_bf]
    aliases = {}
    if prev_fin is not None:
        aliases[len(args)] = 1
        in_specs.append(pl.BlockSpec(memory_space=pl.ANY))
        args.append(prev_fin)
    od, fin = pl.pallas_call(
        functools.partial(_s5_kernel, nseg=nseg),
        grid=(nblk,),
        in_specs=in_specs,
        out_specs=[pl.BlockSpec((None, rows, GROUP_WIDTH), lambda i: (i, 0, 0)),
                   pl.BlockSpec((SUBLANES, fin_w), lambda i: (i, fin_layer))],
        out_shape=[jax.ShapeDtypeStruct((nblk, rows, GROUP_WIDTH), F32),
                   jax.ShapeDtypeStruct((nblk * SUBLANES, fin_layers * fin_w), F32)],
        scratch_shapes=[pltpu.VMEM((rows, 2 * S5_SP), F32), pltpu.VMEM((rows, GROUP_WIDTH), F32)],
        input_output_aliases=aliases,
        compiler_params=_cparams("parallel"),
        name="s5",
    )(*args)
    return od.reshape(nblk, S5_SEG, SUBLANES * GROUP_WIDTH), fin


def _out_kernel(x_ref, oa_ref, ob_ref, oc_ref, od_ref, mod_ref, wo_ref, g2_ref, wr_ref, br_ref,
                xm_ref, h2_ref, gate_ref):
    mix = functools.reduce(jnp.add, [
        _bdot(o_ref[...], wo_ref[i * GROUP_WIDTH:(i + 1) * GROUP_WIDTH, :])
        for i, o_ref in enumerate((oa_ref, ob_ref, oc_ref, od_ref))])
    xm = x_ref[...] + mod_ref[2:3, :] * mix
    xm_ref[...] = xm
    h2 = _rms_rows(xm) * g2_ref[...] * (1.0 + mod_ref[4:5, :]) + mod_ref[3:4, :]
    h2_ref[...] = h2.astype(BF16)

    logits = _dot3(h2, wr_ref[...]) + br_ref[...]
    lane = lax.broadcasted_iota(jnp.int32, logits.shape, 1)
    big = jnp.int32(2 ** 30)
    gmask = lane < MOE_GROUPS
    gl = jnp.where(gmask, logits, -jnp.inf)
    gmax = jnp.max(gl, axis=-1, keepdims=True)
    p_top = 1.0 / jnp.sum(jnp.exp(gl - gmax), axis=-1, keepdims=True)
    g_top = jnp.min(jnp.where(gl == gmax, lane, big), axis=-1, keepdims=True)
    e_lane = lane - ROUTER_OFF
    emask = (e_lane >= 0) & (e_lane < MOE_EXPERTS) & ((e_lane // MOE_PER_GROUP) == g_top)
    el = jnp.where(emask, logits, -jnp.inf)
    m1 = jnp.max(el, axis=-1, keepdims=True)
    i1 = jnp.min(jnp.where(el == m1, lane, big), axis=-1, keepdims=True)
    el2 = jnp.where(lane == i1, -jnp.inf, el)
    m2 = jnp.max(el2, axis=-1, keepdims=True)
    i2 = jnp.min(jnp.where(el2 == m2, lane, big), axis=-1, keepdims=True)
    e2 = jnp.exp(m2 - m1)
    den = 1.0 + e2
    gate_ref[...] = (jnp.where(lane == i1, (1.0 / den) * p_top, 0.0)
                     + jnp.where(lane == i2, (e2 / den) * p_top, 0.0))


def _output_stage(x, mixes, mod, wo_bf, g2, wr, br, *, tiles_per_mod):
    tm = S5_SEG
    n = x.shape[0]
    row = lambda w: pl.BlockSpec((tm, w), lambda i: (i, 0))
    const = lambda shape: pl.BlockSpec(shape, lambda i: (0,) * len(shape))
    return pl.pallas_call(
        _out_kernel,
        grid=(n // tm,),
        in_specs=[row(D_MODEL), row(256), row(256), row(256), _du_spec(1),
                  pl.BlockSpec((None, 6, D_MODEL), lambda i: (i // tiles_per_mod, 0, 0)),
                  const((D_MODEL, D_MODEL)), const((1, D_MODEL)),
                  const((D_MODEL, LANES)), const((1, LANES))],
        out_specs=[row(D_MODEL), row(D_MODEL), row(LANES)],
        out_shape=[jax.ShapeDtypeStruct((n, D_MODEL), F32),
                   jax.ShapeDtypeStruct((n, D_MODEL), BF16),
                   jax.ShapeDtypeStruct((n, LANES), F32)],
        compiler_params=_cparams("parallel"),
        name="output_stage",
    )(x, *mixes, mod, wo_bf, g2, wr, br)


GROUP_HID = MOE_PER_GROUP * MOE_HIDDEN


def _moe_kernel(h2_ref, gate_ref, xm_ref, mod_ref, w1_ref, w3_ref, w2_ref, fg_ref, o_ref, acc_ref, *, final):
    g = pl.program_id(1)
    h2 = h2_ref[...]
    a = jnp.dot(h2, w1_ref[...], preferred_element_type=F32)
    b = jnp.dot(h2, w3_ref[...], preferred_element_type=F32)
    src = lax.broadcasted_iota(jnp.int32, (LANES, GROUP_HID), 0)
    dst = lax.broadcasted_iota(jnp.int32, (LANES, GROUP_HID), 1) // MOE_HIDDEN
    spread = jnp.where(src == ROUTER_OFF + g * MOE_PER_GROUP + dst, 1.0, 0.0).astype(BF16)
    ge = _dot_hilo_lhs(gate_ref[...], spread)
    part = jnp.dot((jax.nn.silu(a) * b * ge).astype(BF16), w2_ref[...], preferred_element_type=F32)

    @pl.when(g == 0)
    def _():
        acc_ref[...] = part

    @pl.when(g > 0)
    def _():
        acc_ref[...] += part

    @pl.when(g == MOE_GROUPS - 1)
    def _():
        out = xm_ref[...] + mod_ref[5:6, :] * acc_ref[...]
        if final:
            out = _rms_rows(out) * fg_ref[...]
        o_ref[...] = out


def _moe(h2, gates, xm, mod, w1g, w3g, w2g, fg, *, tiles_per_mod, final, tm=512):
    n = h2.shape[0]
    row = lambda w: pl.BlockSpec((tm, w), lambda i, g: (i, 0))
    return pl.pallas_call(
        functools.partial(_moe_kernel, final=final),
        grid=(n // tm, MOE_GROUPS),
        in_specs=[row(D_MODEL), row(LANES), row(D_MODEL),
                  pl.BlockSpec((None, 6, D_MODEL), lambda i, g: (i // tiles_per_mod, 0, 0)),
                  pl.BlockSpec((None, D_MODEL, GROUP_HID), lambda i, g: (g, 0, 0)),
                  pl.BlockSpec((None, D_MODEL, GROUP_HID), lambda i, g: (g, 0, 0)),
                  pl.BlockSpec((None, GROUP_HID, D_MODEL), lambda i, g: (g, 0, 0)),
                  pl.BlockSpec((1, D_MODEL), lambda i, g: (0, 0))],
        out_specs=row(D_MODEL),
        out_shape=jax.ShapeDtypeStruct((n, D_MODEL), F32),
        scratch_shapes=[pltpu.VMEM((tm, D_MODEL), F32)],
        compiler_params=_cparams("parallel", "arbitrary"),
        name="moe",
    )(h2, gates, xm, mod, w1g, w3g, w2g, fg)


def kernel(x_prompt, x_sample, cache_a_k, cache_a_v, cache_b_k, cache_b_v, state_ret, state_ssm, c, c_ctx, mod_w, mod_b, norm1_g, norm2_g, w_in, a_qn_g, a_kn_g, b_rel_bias, ret_decay, ret_gn_g, s5_lam_re, s5_lam_im, s5_log_dt, s5_b_re, s5_b_im, s5_c_re, s5_c_im, s5_d, s5_glu_w, w_out, moe_gw, moe_gb, moe_ew, moe_eb, moe_w1, moe_w3, moe_w2, final_norm_g):
    n_ctx = BATCH * SEQ
    n_lat = DEC_BATCH * DEC_SEQ
    lat_seg = DEC_SEQ // S5_SEG

    cond = jnp.zeros((SUBLANES, D_MODEL), F32).at[0].set(c_ctx).at[1:1 + DEC_BATCH].set(c)
    mods = _modulation(cond, mod_w, mod_b).reshape(DEPTH, SUBLANES, 6, D_MODEL)

    rope_tabs = _rope_tables()
    s5_a, s5_bm, s5_cre, s5_cim = _s5_prepare(s5_lam_re, s5_lam_im, s5_log_dt, s5_b_re, s5_b_im,
                                              s5_c_re, s5_c_im)
    cak = cache_a_k.reshape(DEC_BATCH, DEPTH, PAST_LEN, A_KV_HEADS * HEAD_DIM)
    cav = cache_a_v.reshape(DEC_BATCH, DEPTH, PAST_LEN, A_KV_HEADS * HEAD_DIM)
    cbk = cache_b_k.reshape(DEC_BATCH, DEPTH, PAST_LEN, B_HEADS * HEAD_DIM)
    cbv = cache_b_v.reshape(DEC_BATCH, DEPTH, PAST_LEN, B_HEADS * HEAD_DIM)

    xc = x_prompt.reshape(n_ctx, D_MODEL)
    xs = x_sample.reshape(n_lat, D_MODEL)
    caches = ret_states = ssm_states = None
    h0_zero = jnp.zeros((2, SUBLANES, 2 * S5_SP), F32)
    for l in range(DEPTH):
        final = l == DEPTH - 1
        w_in_bf = w_in[l].astype(BF16)
        wo_bf = w_out[l].astype(BF16)
        glu_bf = s5_glu_w[l].astype(BF16)
        g1 = norm1_g[l].reshape(1, D_MODEL)
        g2 = norm2_g[l].reshape(1, D_MODEL)
        fg = final_norm_g.reshape(1, D_MODEL)
        qn = jnp.tile(a_qn_g[l], A_HEADS).reshape(1, 256)
        kn = jnp.tile(a_kn_g[l], A_KV_HEADS).reshape(1, 128)
        dec = jnp.broadcast_to(ret_decay[l].reshape(2 * C_HEADS, 1), (2 * C_HEADS, LANES))
        gn = ret_gn_g[l].reshape(1, 256)
        dvec = s5_d[l].reshape(1, GROUP_WIDTH)
        wr = jnp.zeros((D_MODEL, LANES), F32).at[:, :MOE_GROUPS].set(moe_gw[l]).at[
            :, ROUTER_OFF:ROUTER_OFF + MOE_EXPERTS].set(moe_ew[l])
        br = jnp.zeros((1, LANES), F32).at[0, :MOE_GROUPS].set(moe_gb[l]).at[
            0, ROUTER_OFF:ROUTER_OFF + MOE_EXPERTS].set(moe_eb[l])
        w1g = moe_w1[l].reshape(MOE_GROUPS, MOE_PER_GROUP, D_MODEL, MOE_HIDDEN).transpose(0, 2, 1, 3).reshape(
            MOE_GROUPS, D_MODEL, GROUP_HID).astype(BF16)
        w3g = moe_w3[l].reshape(MOE_GROUPS, MOE_PER_GROUP, D_MODEL, MOE_HIDDEN).transpose(0, 2, 1, 3).reshape(
            MOE_GROUPS, D_MODEL, GROUP_HID).astype(BF16)
        w2g = moe_w2[l].reshape(MOE_GROUPS, GROUP_HID, D_MODEL).astype(BF16)
        na_bias = _na_bias(b_rel_bias[l])
        mod_c = mods[l, 0:1]
        mod_s = mods[l, 1:1 + DEC_BATCH]

        tiles_c = n_ctx // 256
        zc, du_tm, *caches = _project(xc, mod_c, g1, w_in_bf, qn, kn, None, seq_len=SEQ, tiles_per_mod=tiles_c,
                                      cache_layer=l, prev_caches=caches)
        oa, ob = _ctx_attention(zc, BATCH, SEQ)
        oc, ret_states = _retention(zc, dec, gn, None, l, nb=BATCH, seq_len=SEQ, prev_state=ret_states)
        od_tm, ssm_states = _s5(du_tm, h0_zero, s5_a[l], s5_bm[l], s5_cre[l], s5_cim[l], dvec, glu_bf,
                                nseg=1, fin_layer=l, fin_layers=DEPTH, prev_fin=ssm_states)
        xm, h2, gates = _output_stage(xc, (oa, ob, oc, od_tm), mod_c, wo_bf, g2, wr, br, tiles_per_mod=tiles_c)
        xc = _moe(h2, gates, xm, mod_c, w1g, w3g, w2g, fg, tiles_per_mod=n_ctx // 512, final=final)

        zs, du_tm = _project(xs, mod_s, g1, w_in_bf, qn, kn, rope_tabs, seq_len=DEC_SEQ,
                             tiles_per_mod=DEC_SEQ // 256)
        oa = _lat_attention_a(zs, cak, cav, l)
        ob = _lat_attention_b(zs, cbk, cbv, na_bias, l)
        oc = _retention(zs, dec, gn, state_ret, l, nb=DEC_BATCH, seq_len=DEC_SEQ)
        h0 = state_ssm[:, l].reshape(DEC_BATCH, 2, 2 * S5_SP).transpose(1, 0, 2)
        h0_seg = jnp.zeros((2, DEC_BATCH, lat_seg, 2 * S5_SP), F32)
        h0_seg = h0_seg.at[0, :, 0].set(h0[0]).at[1, :, lat_seg - 1].set(h0[1])
        od_tm, _ = _s5(du_tm, h0_seg.reshape(2, SUBLANES, 2 * S5_SP),
                       s5_a[l], s5_bm[l], s5_cre[l], s5_cim[l], dvec, glu_bf, nseg=lat_seg)
        xm, h2, gates = _output_stage(xs, (oa, ob, oc, od_tm), mod_s, wo_bf, g2, wr, br,
                                      tiles_per_mod=DEC_SEQ // 256)
        xs = _moe(h2, gates, xm, mod_s, w1g, w3g, w2g, fg, tiles_per_mod=DEC_SEQ // 512, final=final)

    new_ak, new_av, new_bk, new_bv = caches
    return (xc.reshape(BATCH, SEQ, D_MODEL), xs.reshape(DEC_BATCH, DEC_SEQ, D_MODEL),
            new_ak.reshape(BATCH, DEPTH, SEQ, A_KV_HEADS, HEAD_DIM),
            new_av.reshape(BATCH, DEPTH, SEQ, A_KV_HEADS, HEAD_DIM),
            new_bk.reshape(BATCH, DEPTH, SEQ, B_HEADS, HEAD_DIM),
            new_bv.reshape(BATCH, DEPTH, SEQ, B_HEADS, HEAD_DIM),
            ret_states,
            ssm_states.reshape(BATCH, DEPTH, 2, 2, S5_GROUPS, S5_STATE))
```

```python
import functools
import math

import numpy as np
import jax
import jax.numpy as jnp
from jax import lax
from jax.experimental import pallas as pl
from jax.experimental.pallas import tpu as pltpu

F32 = jnp.float32
BF16 = jnp.bfloat16

D_MODEL = 1024
BATCH = 32
SEQ = 256
DEPTH = 2
DEC_BATCH = 2
DEC_SEQ = 1024
PAST_LEN = 256
GRID_W = 64
HEAD_DIM = 64
GROUP_WIDTH = 256
A_HEADS = 4
A_KV_HEADS = 2
B_HEADS = 4
NA_ROWS = 8
NA_COLS = 16
C_HEADS = 4
S5_CH = 16
S5_GROUPS = 16
S5_STATE = 64
MOE_GROUPS = 4
MOE_PER_GROUP = 8
MOE_EXPERTS = 32
MOE_HIDDEN = 128
ROPE_THETA = 10000.0
EPS = 1e-6
IN_WIDTH = 2560
Q_SCALE = HEAD_DIM ** -0.5

OFF_AQ, OFF_AK, OFF_AV = 0, 256, 384
OFF_BQ, OFF_BK, OFF_BV = 512, 768, 1024
OFF_CQ, OFF_CK, OFF_CV, OFF_CG = 1280, 1536, 1792, 2048
OFF_DU = 2304

LANES = 128
SUBLANES = 8
S5_SP = S5_GROUPS * S5_STATE
S5_SEG = 256
ROUTER_OFF = 4
NEG_BIG = -1e30
VMEM_LIMIT = 56 * 1024 * 1024


def _cparams(*sem):
    return pltpu.CompilerParams(dimension_semantics=sem, vmem_limit_bytes=VMEM_LIMIT)


def _bdot(a, b):
    return jnp.dot(a.astype(BF16), b.astype(BF16), preferred_element_type=F32)


def _bdot_nt(a, b):
    return lax.dot_general(a.astype(BF16), b.astype(BF16), (((1,), (1,)), ((), ())),
                           preferred_element_type=F32)


def _bdot_tn(a, b):
    return lax.dot_general(a.astype(BF16), b.astype(BF16), (((0,), (0,)), ((), ())),
                           preferred_element_type=F32)


def _split(a):
    hi = a.astype(BF16)
    lo = (a - hi.astype(F32)).astype(BF16)
    return hi, lo


def _dot_hilo_lhs(a, b_bf16):
    hi, lo = _split(a)
    return (jnp.dot(hi, b_bf16, preferred_element_type=F32)
            + jnp.dot(lo, b_bf16, preferred_element_type=F32))


def _dot3(a, b):
    ah, al = _split(a)
    bh, bl = _split(b)
    return (jnp.dot(ah, bh, preferred_element_type=F32)
            + jnp.dot(ah, bl, preferred_element_type=F32)
            + jnp.dot(al, bh, preferred_element_type=F32))


def _rms_rows(x):
    return x * lax.rsqrt(jnp.mean(x * x, axis=-1, keepdims=True) + EPS)


def _mod_kernel(cond_ref, w_ref, b_ref, o_ref):
    o_ref[...] = _bdot(jax.nn.silu(cond_ref[...]), w_ref[...]) + b_ref[...]


def _modulation(cond, mod_w, mod_b):
    tn = 1536
    return pl.pallas_call(
        _mod_kernel,
        grid=(DEPTH, 6 * D_MODEL // tn),
        in_specs=[pl.BlockSpec((SUBLANES, D_MODEL), lambda l, j: (0, 0)),
                  pl.BlockSpec((None, D_MODEL, tn), lambda l, j: (l, 0, j)),
                  pl.BlockSpec((None, 1, tn), lambda l, j: (l, 0, j))],
        out_specs=pl.BlockSpec((None, SUBLANES, tn), lambda l, j: (l, 0, j)),
        out_shape=jax.ShapeDtypeStruct((DEPTH, SUBLANES, 6 * D_MODEL), F32),
        compiler_params=_cparams("arbitrary", "arbitrary"),
        name="modulation",
    )(cond, mod_w, mod_b.reshape(DEPTH, 1, 6 * D_MODEL))


def _group_mean_matrix(w):
    ri = lax.broadcasted_iota(jnp.int32, (w, w), 0) // HEAD_DIM
    ci = lax.broadcasted_iota(jnp.int32, (w, w), 1) // HEAD_DIM
    return jnp.where(ri == ci, 1.0 / HEAD_DIM, 0.0).astype(BF16)


def _head_norm(t, g):
    ms = _dot_hilo_lhs(t * t, _group_mean_matrix(t.shape[1]))
    return t * lax.rsqrt(ms + EPS) * g


def _rope(t, cos, sa, sb):
    return (t * cos + pltpu.roll(t, LANES - 16, 1) * sa + pltpu.roll(t, 16, 1) * sb)


def _proj_kernel(*refs, rope, n_alias, with_cache):
    x_ref, mod_ref, g1_ref, w_ref, qn_ref, kn_ref = refs[:6]
    n_in = 6
    if rope:
        cos_ref, sa_ref, sb_ref = refs[6:9]
        n_in = 9
    outs = refs[n_in + n_alias:]
    z_ref, du_ref = outs[:2]
    h = _rms_rows(x_ref[...]) * g1_ref[...] * (1.0 + mod_ref[1:2, :]) + mod_ref[0:1, :]
    z = jnp.dot(h.astype(BF16), w_ref[...], preferred_element_type=F32)
    aq = _head_norm(z[:, OFF_AQ:OFF_AK], qn_ref[...])
    ak = _head_norm(z[:, OFF_AK:OFF_AV], kn_ref[...])
    for j in range(3):
        t = aq[:, j * LANES:(j + 1) * LANES] if j < 2 else ak
        if rope:
            cj = 0 if j == 2 else j
            sl = slice(cj * LANES, (cj + 1) * LANES)
            t = _rope(t, cos_ref[:, sl], sa_ref[:, sl], sb_ref[:, sl])
        z_ref[:, j * LANES:(j + 1) * LANES] = t
    z_ref[:, OFF_AV:OFF_CK] = z[:, OFF_AV:OFF_CK]
    z_ref[:, OFF_CK:OFF_CV] = z[:, OFF_CK:OFF_CV] * Q_SCALE
    z_ref[:, OFF_CV:OFF_DU] = z[:, OFF_CV:OFF_DU]
    du_ref[...] = z[:, OFF_DU:]
    if with_cache:
        ak_ref, av_ref, bk_ref, bv_ref = outs[2:6]
        ak_ref[...] = ak
        av_ref[...] = z[:, OFF_AV:OFF_BQ]
        bk_ref[...] = z[:, OFF_BK:OFF_BV]
        bv_ref[...] = z[:, OFF_BV:OFF_CQ]


def _du_spec(grid_rank):
    if grid_rank == 1:
        return pl.BlockSpec((None, S5_SEG, GROUP_WIDTH), lambda i: (i // SUBLANES, 0, i % SUBLANES))
    return pl.BlockSpec((None, S5_SEG, GROUP_WIDTH), lambda i, g: (i // SUBLANES, 0, i % SUBLANES))


def _project(x, mod, g1, w_in_bf, qn, kn, rope_tabs, *, seq_len, tiles_per_mod, cache_layer=None,
             prev_caches=None):
    tm = S5_SEG
    n = x.shape[0]
    rope = rope_tabs is not None
    with_cache = cache_layer is not None
    in_specs = [pl.BlockSpec((tm, D_MODEL), lambda i: (i, 0)),
                pl.BlockSpec((None, 6, D_MODEL), lambda i: (i // tiles_per_mod, 0, 0)),
                pl.BlockSpec((1, D_MODEL), lambda i: (0, 0)),
                pl.BlockSpec((D_MODEL, IN_WIDTH), lambda i: (0, 0)),
                pl.BlockSpec((1, 256), lambda i: (0, 0)),
                pl.BlockSpec((1, 128), lambda i: (0, 0))]
    args = [x, mod, g1, w_in_bf, qn, kn]
    if rope:
        tps = seq_len // tm
        in_specs += [pl.BlockSpec((tm, 256), lambda i: (i % tps, 0))] * 3
        args += list(rope_tabs)
    out_specs = [pl.BlockSpec((tm, OFF_DU), lambda i: (i, 0)), _du_spec(1)]
    out_shape = [jax.ShapeDtypeStruct((n, OFF_DU), F32),
                 jax.ShapeDtypeStruct((n // (tm * SUBLANES), S5_SEG, SUBLANES * GROUP_WIDTH), F32)]
    aliases = {}
    n_alias = 0
    if with_cache:
        assert tm == seq_len
        nb = n // seq_len
        for w in (128, 128, 256, 256):
            out_specs.append(pl.BlockSpec((None, None, seq_len, w), lambda i: (i, cache_layer, 0, 0)))
            out_shape.append(jax.ShapeDtypeStruct((nb, DEPTH, seq_len, w), F32))
        if prev_caches is not None:
            n_alias = len(prev_caches)
            for k, arr in enumerate(prev_caches):
                aliases[len(args)] = 2 + k
                in_specs.append(pl.BlockSpec(memory_space=pl.ANY))
                args.append(arr)
    return pl.pallas_call(
        functools.partial(_proj_kernel, rope=rope, n_alias=n_alias, with_cache=with_cache),
        grid=(n // tm,),
        in_specs=in_specs,
        out_specs=out_specs,
        out_shape=out_shape,
        input_output_aliases=aliases,
        compiler_params=_cparams("parallel"),
        name="project",
    )(*args)


def _rope_tables():
    t = jnp.arange(DEC_SEQ)
    row = (t // GRID_W).astype(F32)
    col = (t % GRID_W).astype(F32)
    nf = HEAD_DIM // 4
    inv = ROPE_THETA ** (-jnp.arange(nf, dtype=F32) / nf)
    ang_r = row[:, None] * inv[None, :]
    ang_c = col[:, None] * inv[None, :]
    zeros = jnp.zeros_like(ang_r)
    cos = jnp.concatenate([jnp.cos(ang_r), jnp.cos(ang_r), jnp.cos(ang_c), jnp.cos(ang_c)], axis=-1)
    sa = jnp.concatenate([-jnp.sin(ang_r), zeros, -jnp.sin(ang_c), zeros], axis=-1)
    sb = jnp.concatenate([zeros, jnp.sin(ang_r), zeros, jnp.sin(ang_c)], axis=-1)
    return tuple(jnp.tile(a, (1, 4)) for a in (cos, sa, sb))


N_HEADS = 4


def _lane_head(width):
    return lax.broadcasted_iota(jnp.int32, (1, width), 1) // HEAD_DIM


def _stack_heads(q):
    head = _lane_head(q.shape[1])
    return jnp.concatenate([jnp.where(head == h, q, 0.0) for h in range(N_HEADS)], axis=0).astype(BF16)


def _stack_heads_gqa(q):
    lo = lax.broadcasted_iota(jnp.int32, (1, LANES), 1) < HEAD_DIM
    q01, q23 = q[:, :LANES], q[:, LANES:]
    blocks = [jnp.where(lo, q01, 0.0), jnp.where(lo, pltpu.roll(q01, HEAD_DIM, 1), 0.0),
              jnp.where(lo, 0.0, pltpu.roll(q23, HEAD_DIM, 1)), jnp.where(lo, 0.0, q23)]
    return jnp.concatenate(blocks, axis=0).astype(BF16)


def _spread_kv_gqa(v):
    lo = lax.broadcasted_iota(jnp.int32, (1, LANES), 1) < HEAD_DIM
    vr = pltpu.roll(v, HEAD_DIM, 1)
    return jnp.concatenate([jnp.where(lo, v, vr), jnp.where(lo, vr, v)], axis=1)


def _mha(qs, blocks, tq):
    scores = []
    for k, _, bias in blocks:
        s = _bdot_nt(qs, k)
        scores.append(s if bias is None else s + bias)
    m = functools.reduce(jnp.maximum, [jnp.max(s, axis=-1, keepdims=True) for s in scores])
    es = [jnp.exp(s - m) for s in scores]
    denom = functools.reduce(jnp.add, [jnp.sum(e, axis=-1, keepdims=True) for e in es])
    ps = [e.astype(BF16) for e in es]
    head = _lane_head(N_HEADS * HEAD_DIM)
    vals = [v.astype(BF16) for _, v, _ in blocks]
    o = None
    dall = None
    for h in range(N_HEADS):
        rows = slice(h * tq, (h + 1) * tq)
        for p, v in zip(ps, vals):
            t = jnp.dot(p[rows], jnp.where(head == h, v, jnp.zeros_like(v)), preferred_element_type=F32)
            o = t if o is None else o + t
        d = jnp.where(head == h, denom[rows], 0.0)
        dall = d if dall is None else dall + d
    return o / dall


def _ctx_attn_kernel(aq_ref, ak_ref, av_ref, bq_ref, bk_ref, bv_ref, oa_ref, ob_ref):
    tq = aq_ref.shape[0]
    oa_ref[...] = _mha(_stack_heads_gqa(aq_ref[...] * Q_SCALE),
                       [(ak_ref[...], _spread_kv_gqa(av_ref[...]), None)], tq)
    ob_ref[...] = _mha(_stack_heads(bq_ref[...] * Q_SCALE), [(bk_ref[...], bv_ref[...], None)], tq)


def _ctx_attention(z, nb, seq_len):
    def col(width, off):
        return pl.BlockSpec((seq_len, width), lambda b: (b, off // width))
    return pl.pallas_call(
        _ctx_attn_kernel,
        grid=(nb,),
        in_specs=[col(256, OFF_AQ), col(128, OFF_AK), col(128, OFF_AV),
                  col(256, OFF_BQ), col(256, OFF_BK), col(256, OFF_BV)],
        out_specs=[pl.BlockSpec((seq_len, 256), lambda b: (b, 0))] * 2,
        out_shape=[jax.ShapeDtypeStruct((nb * seq_len, 256), F32)] * 2,
        compiler_params=_cparams("parallel"),
        name="ctx_attention",
    )(z, z, z, z, z, z)


def _lat_attn_a_kernel(q_ref, kn_ref, vn_ref, kc_ref, vc_ref, o_ref):
    o_ref[...] = _mha(_stack_heads_gqa(q_ref[...] * Q_SCALE),
                      [(kc_ref[...], _spread_kv_gqa(vc_ref[...]), None),
                       (kn_ref[...], _spread_kv_gqa(vn_ref[...]), None)], q_ref.shape[0])


def _lat_attention_a(z, cache_k, cache_v, layer, tq=256):
    nq = DEC_SEQ // tq
    cache_spec = pl.BlockSpec((None, None, PAST_LEN, 128), lambda b, j: (b, layer, 0, 0))
    return pl.pallas_call(
        _lat_attn_a_kernel,
        grid=(DEC_BATCH, nq),
        in_specs=[pl.BlockSpec((tq, 256), lambda b, j: (b * nq + j, OFF_AQ // 256)),
                  pl.BlockSpec((DEC_SEQ, 128), lambda b, j: (b, OFF_AK // 128)),
                  pl.BlockSpec((DEC_SEQ, 128), lambda b, j: (b, OFF_AV // 128)),
                  cache_spec, cache_spec],
        out_specs=pl.BlockSpec((tq, 256), lambda b, j: (b * nq + j, 0)),
        out_shape=jax.ShapeDtypeStruct((DEC_BATCH * DEC_SEQ, 256), F32),
        compiler_params=_cparams("parallel", "parallel"),
        name="lat_attention_a",
    )(z, z, z, cache_k, cache_v)


NA_KEYS = NA_ROWS * GRID_W


NA_PAIRS = 2 * NA_ROWS - 2


def _na_kernel(q_ref, k_ref, v_ref, kc_ref, vc_ref, bias_ref, o_ref):
    r = pl.program_id(1)
    rows = DEC_SEQ // GRID_W
    row_start = jnp.clip(r - NA_ROWS // 2, 0, rows - NA_ROWS)
    start = pl.multiple_of(row_start * GRID_W, GRID_W)
    rel0 = row_start - r + NA_ROWS - 1
    kl = k_ref[pl.ds(start, NA_KEYS), :]
    vl = v_ref[pl.ds(start, NA_KEYS), :]
    bias = jnp.concatenate(
        [jnp.concatenate([bias_ref[h, rel0 + 2 * jp] for jp in range(NA_ROWS // 2)], axis=1)
         for h in range(B_HEADS)], axis=0)
    o_ref[...] = _mha(_stack_heads(q_ref[...] * Q_SCALE),
                      [(kl, vl, bias), (kc_ref[...], vc_ref[...], None)], GRID_W)


def _na_bias(rel_bias):
    nrel = 2 * NA_COLS - 1
    period = 2 * GRID_W
    b = rel_bias.astype(F32)
    ext = jnp.concatenate([b[..., NA_COLS - 1:],
                           jnp.zeros(b.shape[:-1] + (period - nrel,), F32),
                           b[..., :NA_COLS - 1]], axis=-1)
    flat = jnp.tile(ext, (1, 1, GRID_W))[..., :GRID_W * (period - 1)]
    toe = flat.reshape(b.shape[:-1] + (GRID_W, period - 1))[..., :GRID_W]
    col_start = np.clip(np.arange(GRID_W) - NA_COLS // 2, 0, GRID_W - NA_COLS)
    kc = np.arange(GRID_W)
    inside = (kc[None, :] >= col_start[:, None]) & (kc[None, :] < col_start[:, None] + NA_COLS)
    toe = jnp.where(jnp.asarray(inside), toe, NEG_BIG)
    return jnp.concatenate([toe[:, :-1], toe[:, 1:]], axis=-1)


def _lat_attention_b(z, cache_k, cache_v, bias, layer):
    rows = DEC_SEQ // GRID_W
    cache_spec = pl.BlockSpec((None, None, PAST_LEN, 256), lambda b, r: (b, layer, 0, 0))
    return pl.pallas_call(
        _na_kernel,
        grid=(DEC_BATCH, rows),
        in_specs=[pl.BlockSpec((GRID_W, 256), lambda b, r: (b * rows + r, OFF_BQ // 256)),
                  pl.BlockSpec((DEC_SEQ, 256), lambda b, r: (b, OFF_BK // 256)),
                  pl.BlockSpec((DEC_SEQ, 256), lambda b, r: (b, OFF_BV // 256)),
                  cache_spec, cache_spec,
                  pl.BlockSpec((B_HEADS, NA_PAIRS, GRID_W, 2 * GRID_W), lambda b, r: (0, 0, 0, 0))],
        out_specs=pl.BlockSpec((GRID_W, 256), lambda b, r: (b * rows + r, 0)),
        out_shape=jax.ShapeDtypeStruct((DEC_BATCH * DEC_SEQ, 256), F32),
        compiler_params=_cparams("parallel", "parallel"),
        name="lat_attention_b",
    )(z, z, z, cache_k, cache_v, bias)


def _retention_kernel(q_ref, g_ref, k_ref, v_ref, dec_ref, gn_ref, *rest, seq_len, tq, has_state,
                      hoist_decay):
    if has_state:
        s0_ref, o_ref, dec_scr = rest
    else:
        o_ref, st_ref, dec_scr = rest[-3:]
    head = _lane_head(C_HEADS * HEAD_DIM)
    lg = jax.nn.log_sigmoid(dec_ref[...])

    def per_lane(row0):
        out = jnp.zeros((1, C_HEADS * HEAD_DIM), F32)
        for h in range(C_HEADS):
            out = jnp.where(head == h, lg[row0 + h:row0 + h + 1, 0:1], out)
        return out

    lgf_l, lgb_l = per_lane(0), per_lane(C_HEADS)
    i0 = pl.program_id(1) * tq
    qi = (i0 + lax.broadcasted_iota(jnp.int32, (tq, 1), 0)).astype(F32)

    def fill_decay():
        kj = lax.broadcasted_iota(jnp.int32, (1, seq_len), 1).astype(F32)
        diff = qi - kj
        for h in range(C_HEADS):
            lgf = lg[h:h + 1, 0:1]
            lgb = lg[C_HEADS + h:C_HEADS + h + 1, 0:1]
            dec_scr[h * tq:(h + 1) * tq, :] = (
                jnp.where(diff >= 0, jnp.exp(lgf * jnp.maximum(diff, 0.0)), 0.0)
                + jnp.where(diff <= 0, jnp.exp(lgb * jnp.maximum(-diff, 0.0)), 0.0))

    if hoist_decay:
        pl.when(pl.program_id(0) == 0)(fill_decay)
    else:
        fill_decay()

    q = q_ref[...]
    k = k_ref[...]
    v = v_ref[...].astype(BF16)
    sc = (_bdot_nt(_stack_heads(q), k) * dec_scr[...]).astype(BF16)
    o = None
    for h in range(C_HEADS):
        t = jnp.dot(sc[h * tq:(h + 1) * tq], jnp.where(head == h, v, jnp.zeros_like(v)),
                    preferred_element_type=F32)
        o = t if o is None else o + t
    if has_state:
        o = (o + _bdot(q, s0_ref[0]) * jnp.exp(lgf_l * (qi + 1.0))
             + _bdot(q, s0_ref[1]) * jnp.exp(lgb_l * (seq_len - qi)))
    gm = _group_mean_matrix(C_HEADS * HEAD_DIM)
    dlt = o - _dot_hilo_lhs(o, gm)
    var = _dot_hilo_lhs(dlt * dlt, gm)
    o_ref[...] = dlt * lax.rsqrt(var + EPS) * gn_ref[...] * jax.nn.silu(g_ref[...])
    if not has_state:
        kpos = lax.broadcasted_iota(jnp.int32, (seq_len, 1), 0).astype(F32)
        sf = _bdot_tn(k * jnp.exp(lgf_l * (seq_len - 1.0 - kpos)), v)
        sb = _bdot_tn(k * jnp.exp(lgb_l * kpos), v)
        for h in range(C_HEADS):
            sl = slice(h * HEAD_DIM, (h + 1) * HEAD_DIM)
            st_ref[0, h] = sf[sl, sl]
            st_ref[1, h] = sb[sl, sl]


def _retention(z, dec, gn, s0, layer, *, nb, seq_len, prev_state=None, tq=256):
    nq = seq_len // tq
    has_state = s0 is not None
    aliases = {}
    in_specs = [pl.BlockSpec((tq, 256), lambda b, j: (b * nq + j, OFF_CQ // 256)),
                pl.BlockSpec((tq, 256), lambda b, j: (b * nq + j, OFF_CG // 256)),
                pl.BlockSpec((seq_len, 256), lambda b, j: (b, OFF_CK // 256)),
                pl.BlockSpec((seq_len, 256), lambda b, j: (b, OFF_CV // 256)),
                pl.BlockSpec((SUBLANES, LANES), lambda b, j: (0, 0)),
                pl.BlockSpec((1, 256), lambda b, j: (0, 0))]
    args = [z, z, z, z, dec, gn]
    o_spec = pl.BlockSpec((tq, 256), lambda b, j: (b * nq + j, 0))
    o_shape = jax.ShapeDtypeStruct((nb * seq_len, 256), F32)
    if has_state:
        in_specs.append(pl.BlockSpec((None, None, 2, 256, 256), lambda b, j: (b, layer, 0, 0, 0)))
        args.append(s0)
        out_specs, out_shape = o_spec, o_shape
    else:
        assert nq == 1
        out_specs = [o_spec, pl.BlockSpec((None, None, 2, C_HEADS, HEAD_DIM, HEAD_DIM),
                                          lambda b, j: (b, layer, 0, 0, 0, 0))]
        out_shape = [o_shape, jax.ShapeDtypeStruct((nb, DEPTH, 2, C_HEADS, HEAD_DIM, HEAD_DIM), F32)]
        if prev_state is not None:
            aliases[len(args)] = 1
            in_specs.append(pl.BlockSpec(memory_space=pl.ANY))
            args.append(prev_state)
    return pl.pallas_call(
        functools.partial(_retention_kernel, seq_len=seq_len, tq=tq, has_state=has_state,
                          hoist_decay=nq == 1),
        grid=(nb, nq),
        in_specs=in_specs,
        out_specs=out_specs,
        out_shape=out_shape,
        scratch_shapes=[pltpu.VMEM((C_HEADS * tq, seq_len), F32)],
        input_output_aliases=aliases,
        compiler_params=_cparams("arbitrary", "arbitrary"),
        name="retention",
    )(*args)


def _s5_prep_kernel(lre_ref, lim_ref, ldt_ref, bre_ref, bim_ref, are_ref, aim_ref, bbre_ref, bbim_ref):
    lre = lre_ref[...]
    lim = lim_ref[...]
    dt = jnp.exp(ldt_ref[...])
    mag = jnp.exp(lre * dt)
    a_re = mag * jnp.cos(lim * dt)
    a_im = mag * jnp.sin(lim * dt)
    den = lre * lre + lim * lim
    r_re = ((a_re - 1.0) * lre + a_im * lim) / den
    r_im = (a_im * lre - (a_re - 1.0) * lim) / den
    are_ref[...] = a_re
    aim_ref[...] = a_im
    bbre_ref[...] = r_re * bre_ref[...] - r_im * bim_ref[...]
    bbim_ref[...] = r_re * bim_ref[...] + r_im * bre_ref[...]


def _s5_prepare(lam_re, lam_im, log_dt, b_re, b_im, c_re, c_im):
    lead = (DEPTH, 2, S5_GROUPS)
    full = lead + (S5_CH, S5_STATE)
    rows = DEPTH * 2 * S5_GROUPS * S5_CH

    def expand(t):
        return jnp.broadcast_to(t[:, :, :, None, :], full).reshape(rows, S5_STATE)

    ldt = jnp.broadcast_to(log_dt[:, :, :, None, None], full).reshape(rows, S5_STATE)
    bt = [jnp.swapaxes(t, -1, -2).reshape(rows, S5_STATE) for t in (b_re, b_im)]
    spec = pl.BlockSpec((rows, S5_STATE), lambda: (0, 0))
    a_re, a_im, bb_re, bb_im = pl.pallas_call(
        _s5_prep_kernel,
        in_specs=[spec] * 5,
        out_specs=[spec] * 4,
        out_shape=[jax.ShapeDtypeStruct((rows, S5_STATE), F32)] * 4,
        name="s5_prepare",
    )(expand(lam_re), expand(lam_im), ldt, bt[0], bt[1])
    a = jnp.stack([t.reshape(full)[:, :, :, 0, :].reshape(DEPTH, 2, S5_SP) for t in (a_re, a_im)], axis=2)
    eye = jnp.eye(S5_GROUPS, dtype=F32)

    def in_blockdiag(t):
        t = t.reshape(full)
        return (t[:, :, :, :, None, :] * eye[None, None, :, None, :, None]).reshape(DEPTH, 2, GROUP_WIDTH, S5_SP)

    def out_blockdiag(t):
        t = jnp.swapaxes(t, -1, -2)
        return (t[:, :, :, :, None, :] * eye[None, None, :, None, :, None]).reshape(DEPTH, 2, S5_SP, GROUP_WIDTH)

    bmat = jnp.concatenate([in_blockdiag(bb_re), in_blockdiag(bb_im)], axis=-1).astype(BF16)
    return a, bmat, out_blockdiag(c_re).astype(BF16), out_blockdiag(c_im).astype(BF16)


def _cmul(ar, ai, br, bi):
    return ar * br - ai * bi, ar * bi + ai * br


def _s5_kernel(u_ref, h0_ref, a_ref, bm_ref, cre_ref, cim_ref, dvec_ref, glu_ref, *rest, nseg):
    od_ref, fin_ref, x_scr, y_scr = rest[-4:]
    steps = S5_SEG
    rows = steps * SUBLANES
    chunk = 256
    nchunk = rows // chunk
    seg = lax.broadcasted_iota(jnp.int32, (SUBLANES, S5_SP), 0) % nseg

    for d in range(2):
        def xbody(c, carry):
            r0 = pl.multiple_of(c * chunk, chunk)
            x_scr[pl.ds(r0, chunk), :] = jnp.dot(u_ref[pl.ds(r0, chunk), :].astype(BF16), bm_ref[d],
                                                 preferred_element_type=F32)
            return carry
        lax.fori_loop(0, nchunk, xbody, 0)

        ar = jnp.broadcast_to(a_ref[d, 0:1, :], (SUBLANES, S5_SP))
        ai = jnp.broadcast_to(a_ref[d, 1:2, :], (SUBLANES, S5_SP))

        def scan(init, store):
            def body(t, carry):
                sr, si = carry
                tt = t if d == 0 else steps - 1 - t
                r0 = pl.multiple_of(tt * SUBLANES, SUBLANES)
                pr, pi = _cmul(ar, ai, sr, si)
                nr = pr + x_scr[pl.ds(r0, SUBLANES), 0:S5_SP]
                ni = pi + x_scr[pl.ds(r0, SUBLANES), S5_SP:]
                if store:
                    x_scr[pl.ds(r0, SUBLANES), 0:S5_SP] = nr
                    x_scr[pl.ds(r0, SUBLANES), S5_SP:] = ni
                return nr, ni
            return lax.fori_loop(0, steps, body, init, unroll=4)

        init = (h0_ref[d, :, 0:S5_SP], h0_ref[d, :, S5_SP:])
        if nseg > 1:
            zero = jnp.zeros((SUBLANES, S5_SP), F32)
            fr, fi = scan((zero, zero), store=False)
            pr, pi = ar, ai
            for _ in range(int(math.log2(steps))):
                pr, pi = _cmul(pr, pi, pr, pi)
            cr, ci = init
            shift = 1 if d == 0 else SUBLANES - 1
            order = range(1, nseg) if d == 0 else range(nseg - 2, -1, -1)
            for s in order:
                ncr, nci = pltpu.roll(cr, shift, 0), pltpu.roll(ci, shift, 0)
                nfr, nfi = pltpu.roll(fr, shift, 0), pltpu.roll(fi, shift, 0)
                qr, qi = _cmul(pr, pi, ncr, nci)
                cr = jnp.where(seg == s, qr + nfr, cr)
                ci = jnp.where(seg == s, qi + nfi, ci)
            init = (cr, ci)
        sr, si = scan(init, store=True)
        fin_ref[:, 2 * d * S5_SP:(2 * d + 1) * S5_SP] = sr
        fin_ref[:, (2 * d + 1) * S5_SP:(2 * d + 2) * S5_SP] = si

        def ybody(c, carry):
            r0 = pl.multiple_of(c * chunk, chunk)
            y = (_bdot(x_scr[pl.ds(r0, chunk), 0:S5_SP], cre_ref[d])
                 - _bdot(x_scr[pl.ds(r0, chunk), S5_SP:], cim_ref[d]))
            if d == 0:
                y_scr[pl.ds(r0, chunk), :] = y
            else:
                y_scr[pl.ds(r0, chunk), :] += y
            return carry
        lax.fori_loop(0, nchunk, ybody, 0)

    def obody(c, carry):
        r0 = pl.multiple_of(c * chunk, chunk)
        y = y_scr[pl.ds(r0, chunk), :] + dvec_ref[...] * u_ref[pl.ds(r0, chunk), :]
        zz = jax.nn.gelu(y)
        od_ref[pl.ds(r0, chunk), :] = zz * jax.nn.sigmoid(_bdot(zz, glu_ref[...]))
        return carry
    lax.fori_loop(0, nchunk, obody, 0)


def _s5(du_tm, h0, a, bmat, cre, cim, dvec, glu_bf, *, nseg, fin_layer=0, fin_layers=1, prev_fin=None):
    nblk = du_tm.shape[0]
    rows = S5_SEG * SUBLANES
    fin_w = 4 * S5_SP
    in_specs = [pl.BlockSpec((None, rows, GROUP_WIDTH), lambda i: (i, 0, 0)),
                pl.BlockSpec((2, SUBLANES, 2 * S5_SP), lambda i: (0, 0, 0)),
                pl.BlockSpec((2, 2, S5_SP), lambda i: (0, 0, 0)),
                pl.BlockSpec((2, GROUP_WIDTH, 2 * S5_SP), lambda i: (0, 0, 0)),
                pl.BlockSpec((2, S5_SP, GROUP_WIDTH), lambda i: (0, 0, 0)),
                pl.BlockSpec((2, S5_SP, GROUP_WIDTH), lambda i: (0, 0, 0)),
                pl.BlockSpec((1, GROUP_WIDTH), lambda i: (0, 0)),
                pl.BlockSpec((GROUP_WIDTH, GROUP_WIDTH), lambda i: (0, 0))]
    args = [du_tm.reshape(nblk, rows, GROUP_WIDTH), h0, a, bmat, cre, cim, dvec, glu_bf]
    aliases = {}
    if prev_fin is not None:
        aliases[len(args)] = 1
        in_specs.append(pl.BlockSpec(memory_space=pl.ANY))
        args.append(prev_fin)
    od, fin = pl.pallas_call(
        functools.partial(_s5_kernel, nseg=nseg),
        grid=(nblk,),
        in_specs=in_specs,
        out_specs=[pl.BlockSpec((None, rows, GROUP_WIDTH), lambda i: (i, 0, 0)),
                   pl.BlockSpec((SUBLANES, fin_w), lambda i: (i, fin_layer))],
        out_shape=[jax.ShapeDtypeStruct((nblk, rows, GROUP_WIDTH), F32),
                   jax.ShapeDtypeStruct((nblk * SUBLANES, fin_layers * fin_w), F32)],
        scratch_shapes=[pltpu.VMEM((rows, 2 * S5_SP), F32), pltpu.VMEM((rows, GROUP_WIDTH), F32)],
        input_output_aliases=aliases,
        compiler_params=_cparams("parallel"),
        name="s5",
    )(*args)
    return od.reshape(nblk, S5_SEG, SUBLANES * GROUP_WIDTH), fin


def _out_kernel(x_ref, oa_ref, ob_ref, oc_ref, od_ref, mod_ref, wo_ref, g2_ref, wr_ref, br_ref,
                xm_ref, h2_ref, gate_ref):
    mix = functools.reduce(jnp.add, [
        _bdot(o_ref[...], wo_ref[i * GROUP_WIDTH:(i + 1) * GROUP_WIDTH, :])
        for i, o_ref in enumerate((oa_ref, ob_ref, oc_ref, od_ref))])
    xm = x_ref[...] + mod_ref[2:3, :] * mix
    xm_ref[...] = xm
    h2 = _rms_rows(xm) * g2_ref[...] * (1.0 + mod_ref[4:5, :]) + mod_ref[3:4, :]
    h2_ref[...] = h2.astype(BF16)

    logits = _dot3(h2, wr_ref[...]) + br_ref[...]
    lane = lax.broadcasted_iota(jnp.int32, logits.shape, 1)
    big = jnp.int32(2 ** 30)
    gmask = lane < MOE_GROUPS
    gl = jnp.where(gmask, logits, -jnp.inf)
    gmax = jnp.max(gl, axis=-1, keepdims=True)
    p_top = 1.0 / jnp.sum(jnp.exp(gl - gmax), axis=-1, keepdims=True)
    g_top = jnp.min(jnp.where(gl == gmax, lane, big), axis=-1, keepdims=True)
    e_lane = lane - ROUTER_OFF
    emask = (e_lane >= 0) & (e_lane < MOE_EXPERTS) & ((e_lane // MOE_PER_GROUP) == g_top)
    el = jnp.where(emask, logits, -jnp.inf)
    m1 = jnp.max(el, axis=-1, keepdims=True)
    i1 = jnp.min(jnp.where(el == m1, lane, big), axis=-1, keepdims=True)
    el2 = jnp.where(lane == i1, -jnp.inf, el)
    m2 = jnp.max(el2, axis=-1, keepdims=True)
    i2 = jnp.min(jnp.where(el2 == m2, lane, big), axis=-1, keepdims=True)
    e2 = jnp.exp(m2 - m1)
    den = 1.0 + e2
    gates = (jnp.where(lane == i1, (1.0 / den) * p_top, 0.0)
             + jnp.where(lane == i2, (e2 / den) * p_top, 0.0))
    for g in range(MOE_GROUPS):
        gate_ref[g] = pltpu.roll(gates, LANES - ROUTER_OFF - g * MOE_PER_GROUP, 1)


def _output_stage(x, mixes, mod, wo_bf, g2, wr, br, *, tiles_per_mod):
    tm = S5_SEG
    n = x.shape[0]
    row = lambda w: pl.BlockSpec((tm, w), lambda i: (i, 0))
    const = lambda shape: pl.BlockSpec(shape, lambda i: (0,) * len(shape))
    return pl.pallas_call(
        _out_kernel,
        grid=(n // tm,),
        in_specs=[row(D_MODEL), row(256), row(256), row(256), _du_spec(1),
                  pl.BlockSpec((None, 6, D_MODEL), lambda i: (i // tiles_per_mod, 0, 0)),
                  const((D_MODEL, D_MODEL)), const((1, D_MODEL)),
                  const((D_MODEL, LANES)), const((1, LANES))],
        out_specs=[row(D_MODEL), row(D_MODEL), pl.BlockSpec((MOE_GROUPS, tm, LANES), lambda i: (0, i, 0))],
        out_shape=[jax.ShapeDtypeStruct((n, D_MODEL), F32),
                   jax.ShapeDtypeStruct((n, D_MODEL), BF16),
                   jax.ShapeDtypeStruct((MOE_GROUPS, n, LANES), F32)],
        compiler_params=_cparams("parallel"),
        name="output_stage",
    )(x, *mixes, mod, wo_bf, g2, wr, br)


GROUP_HID = MOE_PER_GROUP * MOE_HIDDEN


def _moe_kernel(h2_ref, gate_ref, xm_ref, mod_ref, w1_ref, w3_ref, w2_ref, fg_ref, o_ref, acc_ref, *, final):
    g = pl.program_id(1)
    h2 = h2_ref[...]
    a = jnp.dot(h2, w1_ref[...], preferred_element_type=F32)
    b = jnp.dot(h2, w3_ref[...], preferred_element_type=F32)
    gates = gate_ref[...]
    hid = []
    for e in range(MOE_PER_GROUP):
        sl = slice(e * MOE_HIDDEN, (e + 1) * MOE_HIDDEN)
        hid.append((jax.nn.silu(a[:, sl]) * b[:, sl] * gates[:, e:e + 1]).astype(BF16))
    part = jnp.dot(jnp.concatenate(hid, axis=1), w2_ref[...], preferred_element_type=F32)

    @pl.when(g == 0)
    def _():
        acc_ref[...] = part

    @pl.when(g > 0)
    def _():
        acc_ref[...] += part

    @pl.when(g == MOE_GROUPS - 1)
    def _():
        out = xm_ref[...] + mod_ref[5:6, :] * acc_ref[...]
        if final:
            out = _rms_rows(out) * fg_ref[...]
        o_ref[...] = out


def _moe_weight_kernel(w1_ref, w3_ref, w2_ref, o1_ref, o3_ref, o2_ref):
    for e in range(MOE_PER_GROUP):
        sl = slice(e * MOE_HIDDEN, (e + 1) * MOE_HIDDEN)
        o1_ref[:, sl] = w1_ref[e].astype(BF16)
        o3_ref[:, sl] = w3_ref[e].astype(BF16)
        o2_ref[sl, :] = w2_ref[e].astype(BF16)


def _moe_weights(w1, w3, w2):
    up = pl.BlockSpec((None, MOE_PER_GROUP, D_MODEL, MOE_HIDDEN), lambda l, g: (l, g, 0, 0))
    down = pl.BlockSpec((None, MOE_PER_GROUP, MOE_HIDDEN, D_MODEL), lambda l, g: (l, g, 0, 0))
    out = pl.BlockSpec((None, None, D_MODEL, GROUP_HID), lambda l, g: (l, g, 0, 0))
    shape = jax.ShapeDtypeStruct((DEPTH, MOE_GROUPS, D_MODEL, GROUP_HID), BF16)
    return pl.pallas_call(
        _moe_weight_kernel,
        grid=(DEPTH, MOE_GROUPS),
        in_specs=[up, up, down],
        out_specs=[out, out, out],
        out_shape=[shape, shape, shape],
        compiler_params=_cparams("parallel", "parallel"),
        name="moe_weights",
    )(w1, w3, w2)


def _moe(h2, gates, xm, mod, w1g, w3g, w2g, fg, *, tiles_per_mod, final, tm=512):
    n = h2.shape[0]
    row = lambda w: pl.BlockSpec((tm, w), lambda i, g: (i, 0))
    return pl.pallas_call(
        functools.partial(_moe_kernel, final=final),
        grid=(n // tm, MOE_GROUPS),
        in_specs=[row(D_MODEL), pl.BlockSpec((None, tm, LANES), lambda i, g: (g, i, 0)), row(D_MODEL),
                  pl.BlockSpec((None, 6, D_MODEL), lambda i, g: (i // tiles_per_mod, 0, 0)),
                  pl.BlockSpec((None, D_MODEL, GROUP_HID), lambda i, g: (g, 0, 0)),
                  pl.BlockSpec((None, D_MODEL, GROUP_HID), lambda i, g: (g, 0, 0)),
                  pl.BlockSpec((None, GROUP_HID, D_MODEL), lambda i, g: (g, 0, 0)),
                  pl.BlockSpec((1, D_MODEL), lambda i, g: (0, 0))],
        out_specs=row(D_MODEL),
        out_shape=jax.ShapeDtypeStruct((n, D_MODEL), F32),
        scratch_shapes=[pltpu.VMEM((tm, D_MODEL), F32)],
        compiler_params=_cparams("parallel", "arbitrary"),
        name="moe",
    )(h2, gates, xm, mod, w1g, w3g, w2g, fg)


def kernel(x_prompt, x_sample, cache_a_k, cache_a_v, cache_b_k, cache_b_v, state_ret, state_ssm, c, c_ctx, mod_w, mod_b, norm1_g, norm2_g, w_in, a_qn_g, a_kn_g, b_rel_bias, ret_decay, ret_gn_g, s5_lam_re, s5_lam_im, s5_log_dt, s5_b_re, s5_b_im, s5_c_re, s5_c_im, s5_d, s5_glu_w, w_out, moe_gw, moe_gb, moe_ew, moe_eb, moe_w1, moe_w3, moe_w2, final_norm_g):
    n_ctx = BATCH * SEQ
    n_lat = DEC_BATCH * DEC_SEQ
    lat_seg = DEC_SEQ // S5_SEG

    cond = jnp.zeros((SUBLANES, D_MODEL), F32).at[0].set(c_ctx).at[1:1 + DEC_BATCH].set(c)
    mods = _modulation(cond, mod_w, mod_b).reshape(DEPTH, SUBLANES, 6, D_MODEL)

    rope_tabs = _rope_tables()
    s5_a, s5_bm, s5_cre, s5_cim = _s5_prepare(s5_lam_re, s5_lam_im, s5_log_dt, s5_b_re, s5_b_im,
                                              s5_c_re, s5_c_im)
    cak = cache_a_k.reshape(DEC_BATCH, DEPTH, PAST_LEN, A_KV_HEADS * HEAD_DIM)
    cav = cache_a_v.reshape(DEC_BATCH, DEPTH, PAST_LEN, A_KV_HEADS * HEAD_DIM)
    cbk = cache_b_k.reshape(DEC_BATCH, DEPTH, PAST_LEN, B_HEADS * HEAD_DIM)
    cbv = cache_b_v.reshape(DEC_BATCH, DEPTH, PAST_LEN, B_HEADS * HEAD_DIM)

    xc = x_prompt.reshape(n_ctx, D_MODEL)
    xs = x_sample.reshape(n_lat, D_MODEL)
    w1_all, w3_all, w2_all = _moe_weights(moe_w1, moe_w3, moe_w2)
    eye_h = jnp.eye(C_HEADS, dtype=F32)
    s0_bd = (state_ret[:, :, :, :, :, None, :] * eye_h[None, None, None, :, None, :, None]).reshape(
        DEC_BATCH, DEPTH, 2, C_HEADS * HEAD_DIM, C_HEADS * HEAD_DIM)
    caches = ret_states = ssm_states = None
    h0_zero = jnp.zeros((2, SUBLANES, 2 * S5_SP), F32)
    for l in range(DEPTH):
        final = l == DEPTH - 1
        w_in_bf = w_in[l].astype(BF16)
        wo_bf = w_out[l].astype(BF16)
        glu_bf = s5_glu_w[l].astype(BF16)
        g1 = norm1_g[l].reshape(1, D_MODEL)
        g2 = norm2_g[l].reshape(1, D_MODEL)
        fg = final_norm_g.reshape(1, D_MODEL)
        qn = jnp.tile(a_qn_g[l], A_HEADS).reshape(1, 256)
        kn = jnp.tile(a_kn_g[l], A_KV_HEADS).reshape(1, 128)
        dec = jnp.broadcast_to(ret_decay[l].reshape(2 * C_HEADS, 1), (2 * C_HEADS, LANES))
        gn = ret_gn_g[l].reshape(1, 256)
        dvec = s5_d[l].reshape(1, GROUP_WIDTH)
        wr = jnp.zeros((D_MODEL, LANES), F32).at[:, :MOE_GROUPS].set(moe_gw[l]).at[
            :, ROUTER_OFF:ROUTER_OFF + MOE_EXPERTS].set(moe_ew[l])
        br = jnp.zeros((1, LANES), F32).at[0, :MOE_GROUPS].set(moe_gb[l]).at[
            0, ROUTER_OFF:ROUTER_OFF + MOE_EXPERTS].set(moe_eb[l])
        w1g, w3g, w2g = w1_all[l], w3_all[l], w2_all[l]
        na_bias = _na_bias(b_rel_bias[l])
        mod_c = mods[l, 0:1]
        mod_s = mods[l, 1:1 + DEC_BATCH]

        tiles_c = n_ctx // 256
        zc, du_tm, *caches = _project(xc, mod_c, g1, w_in_bf, qn, kn, None, seq_len=SEQ, tiles_per_mod=tiles_c,
                                      cache_layer=l, prev_caches=caches)
        oa, ob = _ctx_attention(zc, BATCH, SEQ)
        oc, ret_states = _retention(zc, dec, gn, None, l, nb=BATCH, seq_len=SEQ, prev_state=ret_states)
        od_tm, ssm_states = _s5(du_tm, h0_zero, s5_a[l], s5_bm[l], s5_cre[l], s5_cim[l], dvec, glu_bf,
                                nseg=1, fin_layer=l, fin_layers=DEPTH, prev_fin=ssm_states)
        xm, h2, gates = _output_stage(xc, (oa, ob, oc, od_tm), mod_c, wo_bf, g2, wr, br, tiles_per_mod=tiles_c)
        xc = _moe(h2, gates, xm, mod_c, w1g, w3g, w2g, fg, tiles_per_mod=n_ctx // 512, final=final)

        zs, du_tm = _project(xs, mod_s, g1, w_in_bf, qn, kn, rope_tabs, seq_len=DEC_SEQ,
                             tiles_per_mod=DEC_SEQ // 256)
        oa = _lat_attention_a(zs, cak, cav, l)
        ob = _lat_attention_b(zs, cbk, cbv, na_bias, l)
        oc = _retention(zs, dec, gn, s0_bd, l, nb=DEC_BATCH, seq_len=DEC_SEQ)
        h0 = state_ssm[:, l].reshape(DEC_BATCH, 2, 2 * S5_SP).transpose(1, 0, 2)
        h0_seg = jnp.zeros((2, DEC_BATCH, lat_seg, 2 * S5_SP), F32)
        h0_seg = h0_seg.at[0, :, 0].set(h0[0]).at[1, :, lat_seg - 1].set(h0[1])
        od_tm, _ = _s5(du_tm, h0_seg.reshape(2, SUBLANES, 2 * S5_SP),
                       s5_a[l], s5_bm[l], s5_cre[l], s5_cim[l], dvec, glu_bf, nseg=lat_seg)
        xm, h2, gates = _output_stage(xs, (oa, ob, oc, od_tm), mod_s, wo_bf, g2, wr, br,
                                      tiles_per_mod=DEC_SEQ // 256)
        xs = _moe(h2, gates, xm, mod_s, w1g, w3g, w2g, fg, tiles_per_mod=DEC_SEQ // 512, final=final)

    new_ak, new_av, new_bk, new_bv = caches
    return (xc.reshape(BATCH, SEQ, D_MODEL), xs.reshape(DEC_BATCH, DEC_SEQ, D_MODEL),
            new_ak.reshape(BATCH, DEPTH, SEQ, A_KV_HEADS, HEAD_DIM),
            new_av.reshape(BATCH, DEPTH, SEQ, A_KV_HEADS, HEAD_DIM),
            new_bk.reshape(BATCH, DEPTH, SEQ, B_HEADS, HEAD_DIM),
            new_bv.reshape(BATCH, DEPTH, SEQ, B_HEADS, HEAD_DIM),
            ret_states,
            ssm_states.reshape(BATCH, DEPTH, 2, 2, S5_GROUPS, S5_STATE))
```

```python
import functools
import math

import numpy as np
import jax
import jax.numpy as jnp
from jax import lax
from jax.experimental import pallas as pl
from jax.experimental.pallas import tpu as pltpu

F32 = jnp.float32
BF16 = jnp.bfloat16

D_MODEL = 1024
BATCH = 32
SEQ = 256
DEPTH = 2
DEC_BATCH = 2
DEC_SEQ = 1024
PAST_LEN = 256
GRID_W = 64
HEAD_DIM = 64
GROUP_WIDTH = 256
A_HEADS = 4
A_KV_HEADS = 2
B_HEADS = 4
NA_ROWS = 8
NA_COLS = 16
C_HEADS = 4
S5_CH = 16
S5_GROUPS = 16
S5_STATE = 64
MOE_GROUPS = 4
MOE_PER_GROUP = 8
MOE_EXPERTS = 32
MOE_HIDDEN = 128
ROPE_THETA = 10000.0
EPS = 1e-6
IN_WIDTH = 2560
Q_SCALE = HEAD_DIM ** -0.5

OFF_AQ, OFF_AK, OFF_AV = 0, 256, 384
OFF_BQ, OFF_BK, OFF_BV = 512, 768, 1024
OFF_CQ, OFF_CK, OFF_CV, OFF_CG = 1280, 1536, 1792, 2048
OFF_DU = 2304

LANES = 128
SUBLANES = 8
S5_SP = S5_GROUPS * S5_STATE
S5_SEG = 256
ROUTER_OFF = 4
NEG_BIG = -1e30
VMEM_LIMIT = 56 * 1024 * 1024


def _cparams(*sem):
    return pltpu.CompilerParams(dimension_semantics=sem, vmem_limit_bytes=VMEM_LIMIT)


def _mod_spec(layer, first_row, tiles_per_row, grid_rank):
    if grid_rank == 1:
        return pl.BlockSpec((None, None, 6, D_MODEL), lambda i: (layer, first_row + i // tiles_per_row, 0, 0))
    return pl.BlockSpec((None, None, 6, D_MODEL), lambda i, g: (layer, first_row + i // tiles_per_row, 0, 0))


def _bdot(a, b):
    return jnp.dot(a.astype(BF16), b.astype(BF16), preferred_element_type=F32)


def _bdot_nt(a, b):
    return lax.dot_general(a.astype(BF16), b.astype(BF16), (((1,), (1,)), ((), ())),
                           preferred_element_type=F32)


def _bdot_tn(a, b):
    return lax.dot_general(a.astype(BF16), b.astype(BF16), (((0,), (0,)), ((), ())),
                           preferred_element_type=F32)


def _split(a):
    hi = a.astype(BF16)
    lo = (a - hi.astype(F32)).astype(BF16)
    return hi, lo


def _dot_hilo_lhs(a, b_bf16):
    hi, lo = _split(a)
    return (jnp.dot(hi, b_bf16, preferred_element_type=F32)
            + jnp.dot(lo, b_bf16, preferred_element_type=F32))


def _dot3(a, b):
    ah, al = _split(a)
    bh, bl = _split(b)
    return (jnp.dot(ah, bh, preferred_element_type=F32)
            + jnp.dot(ah, bl, preferred_element_type=F32)
            + jnp.dot(al, bh, preferred_element_type=F32))


def _rms_rows(x):
    return x * lax.rsqrt(jnp.mean(x * x, axis=-1, keepdims=True) + EPS)


def _mod_kernel(cond_ref, w_ref, b_ref, o_ref):
    o_ref[...] = _bdot(jax.nn.silu(cond_ref[...]), w_ref[...]) + b_ref[...]


def _modulation(cond, mod_w, mod_b):
    tn = 1536
    return pl.pallas_call(
        _mod_kernel,
        grid=(DEPTH, 6 * D_MODEL // tn),
        in_specs=[pl.BlockSpec((SUBLANES, D_MODEL), lambda l, j: (0, 0)),
                  pl.BlockSpec((None, D_MODEL, tn), lambda l, j: (l, 0, j)),
                  pl.BlockSpec((None, 1, tn), lambda l, j: (l, 0, j))],
        out_specs=pl.BlockSpec((None, SUBLANES, tn), lambda l, j: (l, 0, j)),
        out_shape=jax.ShapeDtypeStruct((DEPTH, SUBLANES, 6 * D_MODEL), F32),
        compiler_params=_cparams("arbitrary", "arbitrary"),
        name="modulation",
    )(cond, mod_w, mod_b.reshape(DEPTH, 1, 6 * D_MODEL))


def _group_mean_matrix(w):
    ri = lax.broadcasted_iota(jnp.int32, (w, w), 0) // HEAD_DIM
    ci = lax.broadcasted_iota(jnp.int32, (w, w), 1) // HEAD_DIM
    return jnp.where(ri == ci, 1.0 / HEAD_DIM, 0.0).astype(BF16)


def _head_norm(t, g):
    ms = _dot_hilo_lhs(t * t, _group_mean_matrix(t.shape[1]))
    return t * lax.rsqrt(ms + EPS) * g


def _rope(t, cos, sa, sb):
    return (t * cos + pltpu.roll(t, LANES - 16, 1) * sa + pltpu.roll(t, 16, 1) * sb)


def _proj_kernel(*refs, rope, n_alias, with_cache):
    x_ref, mod_ref, g1_ref, w_ref, qn_ref, kn_ref = refs[:6]
    n_in = 6
    if rope:
        cos_ref, sa_ref, sb_ref = refs[6:9]
        n_in = 9
    outs = refs[n_in + n_alias:]
    z_ref, cg_ref, du_ref = outs[:3]
    h = _rms_rows(x_ref[...]) * g1_ref[...] * (1.0 + mod_ref[1:2, :]) + mod_ref[0:1, :]
    z = jnp.dot(h.astype(BF16), w_ref[...], preferred_element_type=F32)
    aq = _head_norm(z[:, OFF_AQ:OFF_AK], qn_ref[...])
    ak = _head_norm(z[:, OFF_AK:OFF_AV], kn_ref[...])
    for j in range(3):
        t = aq[:, j * LANES:(j + 1) * LANES] if j < 2 else ak
        if rope:
            cj = 0 if j == 2 else j
            sl = slice(cj * LANES, (cj + 1) * LANES)
            t = _rope(t, cos_ref[:, sl], sa_ref[:, sl], sb_ref[:, sl])
        if j == 2:
            ak = t
        z_ref[:, j * LANES:(j + 1) * LANES] = t.astype(BF16)
    z_ref[:, OFF_AV:OFF_CK] = z[:, OFF_AV:OFF_CK].astype(BF16)
    z_ref[:, OFF_CK:OFF_CV] = (z[:, OFF_CK:OFF_CV] * Q_SCALE).astype(BF16)
    z_ref[:, OFF_CV:OFF_CG] = z[:, OFF_CV:OFF_CG].astype(BF16)
    cg_ref[...] = z[:, OFF_CG:OFF_DU]
    du_ref[...] = z[:, OFF_DU:]
    if with_cache:
        ak_ref, av_ref, bk_ref, bv_ref = outs[3:7]
        ak_ref[...] = ak
        av_ref[...] = z[:, OFF_AV:OFF_BQ]
        bk_ref[...] = z[:, OFF_BK:OFF_BV]
        bv_ref[...] = z[:, OFF_BV:OFF_CQ]


def _du_spec(grid_rank):
    if grid_rank == 1:
        return pl.BlockSpec((None, S5_SEG, GROUP_WIDTH), lambda i: (i // SUBLANES, 0, i % SUBLANES))
    return pl.BlockSpec((None, S5_SEG, GROUP_WIDTH), lambda i, g: (i // SUBLANES, 0, i % SUBLANES))


def _project(x, mods, mod_row, mod_tokens, g1, w_in_bf, qn, kn, rope_tabs, layer, *, seq_len,
             with_cache=False, prev_caches=None):
    tm = S5_SEG
    n = x.shape[0]
    rope = rope_tabs is not None
    in_specs = [pl.BlockSpec((tm, D_MODEL), lambda i: (i, 0)),
                _mod_spec(layer, mod_row, mod_tokens // tm, 1),
                pl.BlockSpec((1, D_MODEL), lambda i: (0, 0)),
                pl.BlockSpec((None, D_MODEL, IN_WIDTH), lambda i: (layer, 0, 0)),
                pl.BlockSpec((1, 256), lambda i: (0, 0)),
                pl.BlockSpec((1, 128), lambda i: (0, 0))]
    args = [x, mods, g1, w_in_bf, qn, kn]
    if rope:
        tps = seq_len // tm
        in_specs += [pl.BlockSpec((tm, 256), lambda i: (i % tps, 0))] * 3
        args += list(rope_tabs)
    out_specs = [pl.BlockSpec((tm, OFF_CG), lambda i: (i, 0)),
                 pl.BlockSpec((tm, GROUP_WIDTH), lambda i: (i, 0)), _du_spec(1)]
    out_shape = [jax.ShapeDtypeStruct((n, OFF_CG), BF16),
                 jax.ShapeDtypeStruct((n, GROUP_WIDTH), F32),
                 jax.ShapeDtypeStruct((n // (tm * SUBLANES), S5_SEG, SUBLANES * GROUP_WIDTH), F32)]
    aliases = {}
    n_alias = 0
    if with_cache:
        assert tm == seq_len
        nb = n // seq_len
        for w in (128, 128, 256, 256):
            out_specs.append(pl.BlockSpec((None, None, seq_len, w), lambda i: (i, layer, 0, 0)))
            out_shape.append(jax.ShapeDtypeStruct((nb, DEPTH, seq_len, w), F32))
        if prev_caches is not None:
            n_alias = len(prev_caches)
            for k, arr in enumerate(prev_caches):
                aliases[len(args)] = 3 + k
                in_specs.append(pl.BlockSpec(memory_space=pl.ANY))
                args.append(arr)
    return pl.pallas_call(
        functools.partial(_proj_kernel, rope=rope, n_alias=n_alias, with_cache=with_cache),
        grid=(n // tm,),
        in_specs=in_specs,
        out_specs=out_specs,
        out_shape=out_shape,
        input_output_aliases=aliases,
        compiler_params=_cparams("parallel"),
        name="project",
    )(*args)


def _rope_tables():
    t = jnp.arange(DEC_SEQ)
    row = (t // GRID_W).astype(F32)
    col = (t % GRID_W).astype(F32)
    nf = HEAD_DIM // 4
    inv = ROPE_THETA ** (-jnp.arange(nf, dtype=F32) / nf)
    ang_r = row[:, None] * inv[None, :]
    ang_c = col[:, None] * inv[None, :]
    zeros = jnp.zeros_like(ang_r)
    cos = jnp.concatenate([jnp.cos(ang_r), jnp.cos(ang_r), jnp.cos(ang_c), jnp.cos(ang_c)], axis=-1)
    sa = jnp.concatenate([-jnp.sin(ang_r), zeros, -jnp.sin(ang_c), zeros], axis=-1)
    sb = jnp.concatenate([zeros, jnp.sin(ang_r), zeros, jnp.sin(ang_c)], axis=-1)
    return tuple(jnp.tile(a, (1, 4)) for a in (cos, sa, sb))


N_HEADS = 4


def _lane_head(width):
    return lax.broadcasted_iota(jnp.int32, (1, width), 1) // HEAD_DIM


def _stack_heads(q):
    head = _lane_head(q.shape[1])
    return jnp.concatenate([jnp.where(head == h, q, 0.0) for h in range(N_HEADS)], axis=0).astype(BF16)


def _stack_heads_gqa(q):
    lo = lax.broadcasted_iota(jnp.int32, (1, LANES), 1) < HEAD_DIM
    q = q.astype(F32)
    q01, q23 = q[:, :LANES], q[:, LANES:]
    blocks = [jnp.where(lo, q01, 0.0), jnp.where(lo, pltpu.roll(q01, HEAD_DIM, 1), 0.0),
              jnp.where(lo, 0.0, pltpu.roll(q23, HEAD_DIM, 1)), jnp.where(lo, 0.0, q23)]
    return jnp.concatenate(blocks, axis=0).astype(BF16)


def _spread_kv_gqa(v):
    lo = lax.broadcasted_iota(jnp.int32, (1, LANES), 1) < HEAD_DIM
    v = v.astype(F32)
    vr = pltpu.roll(v, HEAD_DIM, 1)
    return jnp.concatenate([jnp.where(lo, v, vr), jnp.where(lo, vr, v)], axis=1)


def _mha(qs, blocks, tq):
    scores = []
    for k, _, bias in blocks:
        s = _bdot_nt(qs, k)
        scores.append(s if bias is None else s + bias)
    m = functools.reduce(jnp.maximum, [jnp.max(s, axis=-1, keepdims=True) for s in scores])
    es = [jnp.exp(s - m) for s in scores]
    denom = functools.reduce(jnp.add, [jnp.sum(e, axis=-1, keepdims=True) for e in es])
    ps = [e.astype(BF16) for e in es]
    head = _lane_head(N_HEADS * HEAD_DIM)
    vals = [v.astype(BF16) for _, v, _ in blocks]
    o = None
    dall = None
    for h in range(N_HEADS):
        rows = slice(h * tq, (h + 1) * tq)
        for p, v in zip(ps, vals):
            t = jnp.dot(p[rows], jnp.where(head == h, v, jnp.zeros_like(v)), preferred_element_type=F32)
            o = t if o is None else o + t
        d = jnp.where(head == h, denom[rows], 0.0)
        dall = d if dall is None else dall + d
    return o / dall


def _ctx_attn_kernel(aq_ref, ak_ref, av_ref, bq_ref, bk_ref, bv_ref, oa_ref, ob_ref):
    tq = aq_ref.shape[0]
    oa_ref[...] = _mha(_stack_heads_gqa(aq_ref[...] * Q_SCALE),
                       [(ak_ref[...], _spread_kv_gqa(av_ref[...]), None)], tq)
    ob_ref[...] = _mha(_stack_heads(bq_ref[...] * Q_SCALE), [(bk_ref[...], bv_ref[...], None)], tq)


def _ctx_attention(z, nb, seq_len):
    def col(width, off):
        return pl.BlockSpec((seq_len, width), lambda b: (b, off // width))
    return pl.pallas_call(
        _ctx_attn_kernel,
        grid=(nb,),
        in_specs=[col(256, OFF_AQ), col(128, OFF_AK), col(128, OFF_AV),
                  col(256, OFF_BQ), col(256, OFF_BK), col(256, OFF_BV)],
        out_specs=[pl.BlockSpec((seq_len, 256), lambda b: (b, 0))] * 2,
        out_shape=[jax.ShapeDtypeStruct((nb * seq_len, 256), F32)] * 2,
        compiler_params=_cparams("parallel"),
        name="ctx_attention",
    )(z, z, z, z, z, z)


def _lat_attn_a_kernel(q_ref, kn_ref, vn_ref, kc_ref, vc_ref, o_ref):
    o_ref[...] = _mha(_stack_heads_gqa(q_ref[...] * Q_SCALE),
                      [(kc_ref[...], _spread_kv_gqa(vc_ref[...]), None),
                       (kn_ref[...], _spread_kv_gqa(vn_ref[...]), None)], q_ref.shape[0])


def _lat_attention_a(z, cache_k, cache_v, layer, tq=256):
    nq = DEC_SEQ // tq
    cache_spec = pl.BlockSpec((None, None, PAST_LEN, 128), lambda b, j: (b, layer, 0, 0))
    return pl.pallas_call(
        _lat_attn_a_kernel,
        grid=(DEC_BATCH, nq),
        in_specs=[pl.BlockSpec((tq, 256), lambda b, j: (b * nq + j, OFF_AQ // 256)),
                  pl.BlockSpec((DEC_SEQ, 128), lambda b, j: (b, OFF_AK // 128)),
                  pl.BlockSpec((DEC_SEQ, 128), lambda b, j: (b, OFF_AV // 128)),
                  cache_spec, cache_spec],
        out_specs=pl.BlockSpec((tq, 256), lambda b, j: (b * nq + j, 0)),
        out_shape=jax.ShapeDtypeStruct((DEC_BATCH * DEC_SEQ, 256), F32),
        compiler_params=_cparams("parallel", "parallel"),
        name="lat_attention_a",
    )(z, z, z, cache_k, cache_v)


NA_KEYS = NA_ROWS * GRID_W


NA_PAIRS = 2 * NA_ROWS - 2


def _na_kernel(q_ref, k_ref, v_ref, kc_ref, vc_ref, bias_ref, o_ref):
    r = pl.program_id(1)
    rows = DEC_SEQ // GRID_W
    row_start = jnp.clip(r - NA_ROWS // 2, 0, rows - NA_ROWS)
    start = pl.multiple_of(row_start * GRID_W, GRID_W)
    rel0 = row_start - r + NA_ROWS - 1
    kl = k_ref[pl.ds(start, NA_KEYS), :]
    vl = v_ref[pl.ds(start, NA_KEYS), :]
    bias = jnp.concatenate(
        [jnp.concatenate([bias_ref[h, rel0 + 2 * jp] for jp in range(NA_ROWS // 2)], axis=1)
         for h in range(B_HEADS)], axis=0)
    o_ref[...] = _mha(_stack_heads(q_ref[...] * Q_SCALE),
                      [(kl, vl, bias), (kc_ref[...], vc_ref[...], None)], GRID_W)


def _na_bias(rel_bias):
    nrel = 2 * NA_COLS - 1
    period = 2 * GRID_W
    b = rel_bias.astype(F32)
    ext = jnp.concatenate([b[..., NA_COLS - 1:],
                           jnp.zeros(b.shape[:-1] + (period - nrel,), F32),
                           b[..., :NA_COLS - 1]], axis=-1)
    flat = jnp.tile(ext, (1, 1, GRID_W))[..., :GRID_W * (period - 1)]
    toe = flat.reshape(b.shape[:-1] + (GRID_W, period - 1))[..., :GRID_W]
    col_start = np.clip(np.arange(GRID_W) - NA_COLS // 2, 0, GRID_W - NA_COLS)
    kc = np.arange(GRID_W)
    inside = (kc[None, :] >= col_start[:, None]) & (kc[None, :] < col_start[:, None] + NA_COLS)
    toe = jnp.where(jnp.asarray(inside), toe, NEG_BIG)
    return jnp.concatenate([toe[:, :-1], toe[:, 1:]], axis=-1)


def _lat_attention_b(z, cache_k, cache_v, bias, layer):
    rows = DEC_SEQ // GRID_W
    cache_spec = pl.BlockSpec((None, None, PAST_LEN, 256), lambda b, r: (b, layer, 0, 0))
    return pl.pallas_call(
        _na_kernel,
        grid=(DEC_BATCH, rows),
        in_specs=[pl.BlockSpec((GRID_W, 256), lambda b, r: (b * rows + r, OFF_BQ // 256)),
                  pl.BlockSpec((DEC_SEQ, 256), lambda b, r: (b, OFF_BK // 256)),
                  pl.BlockSpec((DEC_SEQ, 256), lambda b, r: (b, OFF_BV // 256)),
                  cache_spec, cache_spec,
                  pl.BlockSpec((B_HEADS, NA_PAIRS, GRID_W, 2 * GRID_W), lambda b, r: (0, 0, 0, 0))],
        out_specs=pl.BlockSpec((GRID_W, 256), lambda b, r: (b * rows + r, 0)),
        out_shape=jax.ShapeDtypeStruct((DEC_BATCH * DEC_SEQ, 256), F32),
        compiler_params=_cparams("parallel", "parallel"),
        name="lat_attention_b",
    )(z, z, z, cache_k, cache_v, bias)


def _retention_kernel(q_ref, g_ref, k_ref, v_ref, dec_ref, gn_ref, *rest, seq_len, tq, has_state,
                      hoist_decay):
    if has_state:
        s0_ref, o_ref, dec_scr = rest
    else:
        o_ref, st_ref, dec_scr = rest[-3:]
    head = _lane_head(C_HEADS * HEAD_DIM)
    lg = jax.nn.log_sigmoid(dec_ref[...])

    def per_lane(row0):
        out = jnp.zeros((1, C_HEADS * HEAD_DIM), F32)
        for h in range(C_HEADS):
            out = jnp.where(head == h, lg[row0 + h:row0 + h + 1, 0:1], out)
        return out

    lgf_l, lgb_l = per_lane(0), per_lane(C_HEADS)
    i0 = pl.program_id(1) * tq
    qi = (i0 + lax.broadcasted_iota(jnp.int32, (tq, 1), 0)).astype(F32)

    def fill_decay():
        kj = lax.broadcasted_iota(jnp.int32, (1, seq_len), 1).astype(F32)
        diff = qi - kj
        for h in range(C_HEADS):
            lgf = lg[h:h + 1, 0:1]
            lgb = lg[C_HEADS + h:C_HEADS + h + 1, 0:1]
            dec_scr[h * tq:(h + 1) * tq, :] = (
                jnp.where(diff >= 0, jnp.exp(lgf * jnp.maximum(diff, 0.0)), 0.0)
                + jnp.where(diff <= 0, jnp.exp(lgb * jnp.maximum(-diff, 0.0)), 0.0))

    if hoist_decay:
        pl.when(pl.program_id(0) == 0)(fill_decay)
    else:
        fill_decay()

    q = q_ref[...]
    k = k_ref[...]
    v = v_ref[...].astype(BF16)
    sc = (_bdot_nt(_stack_heads(q), k) * dec_scr[...]).astype(BF16)
    o = None
    for h in range(C_HEADS):
        t = jnp.dot(sc[h * tq:(h + 1) * tq], jnp.where(head == h, v, jnp.zeros_like(v)),
                    preferred_element_type=F32)
        o = t if o is None else o + t
    if has_state:
        o = (o + _bdot(q, s0_ref[0]) * jnp.exp(lgf_l * (qi + 1.0))
             + _bdot(q, s0_ref[1]) * jnp.exp(lgb_l * (seq_len - qi)))
    gm = _group_mean_matrix(C_HEADS * HEAD_DIM)
    dlt = o - _dot_hilo_lhs(o, gm)
    var = _dot_hilo_lhs(dlt * dlt, gm)
    o_ref[...] = dlt * lax.rsqrt(var + EPS) * gn_ref[...] * jax.nn.silu(g_ref[...])
    if not has_state:
        kpos = lax.broadcasted_iota(jnp.int32, (seq_len, 1), 0).astype(F32)
        sf = _bdot_tn(k * jnp.exp(lgf_l * (seq_len - 1.0 - kpos)), v)
        sb = _bdot_tn(k * jnp.exp(lgb_l * kpos), v)
        for h in range(C_HEADS):
            sl = slice(h * HEAD_DIM, (h + 1) * HEAD_DIM)
            st_ref[0, h] = sf[sl, sl]
            st_ref[1, h] = sb[sl, sl]


def _retention(z, cg, dec, gn, s0, layer, *, nb, seq_len, prev_state=None, tq=256):
    nq = seq_len // tq
    has_state = s0 is not None
    aliases = {}
    in_specs = [pl.BlockSpec((tq, 256), lambda b, j: (b * nq + j, OFF_CQ // 256)),
                pl.BlockSpec((tq, 256), lambda b, j: (b * nq + j, 0)),
                pl.BlockSpec((seq_len, 256), lambda b, j: (b, OFF_CK // 256)),
                pl.BlockSpec((seq_len, 256), lambda b, j: (b, OFF_CV // 256)),
                pl.BlockSpec((SUBLANES, LANES), lambda b, j: (0, 0)),
                pl.BlockSpec((1, 256), lambda b, j: (0, 0))]
    args = [z, cg, z, z, dec, gn]
    o_spec = pl.BlockSpec((tq, 256), lambda b, j: (b * nq + j, 0))
    o_shape = jax.ShapeDtypeStruct((nb * seq_len, 256), F32)
    if has_state:
        in_specs.append(pl.BlockSpec((None, None, 2, 256, 256), lambda b, j: (b, layer, 0, 0, 0)))
        args.append(s0)
        out_specs, out_shape = o_spec, o_shape
    else:
        assert nq == 1
        out_specs = [o_spec, pl.BlockSpec((None, None, 2, C_HEADS, HEAD_DIM, HEAD_DIM),
                                          lambda b, j: (b, layer, 0, 0, 0, 0))]
        out_shape = [o_shape, jax.ShapeDtypeStruct((nb, DEPTH, 2, C_HEADS, HEAD_DIM, HEAD_DIM), F32)]
        if prev_state is not None:
            aliases[len(args)] = 1
            in_specs.append(pl.BlockSpec(memory_space=pl.ANY))
            args.append(prev_state)
    return pl.pallas_call(
        functools.partial(_retention_kernel, seq_len=seq_len, tq=tq, has_state=has_state,
                          hoist_decay=nq == 1),
        grid=(nb, nq),
        in_specs=in_specs,
        out_specs=out_specs,
        out_shape=out_shape,
        scratch_shapes=[pltpu.VMEM((C_HEADS * tq, seq_len), F32)],
        input_output_aliases=aliases,
        compiler_params=_cparams("arbitrary", "arbitrary"),
        name="retention",
    )(*args)


def _s5_prep_kernel(lre_ref, lim_ref, ldt_ref, bre_ref, bim_ref, are_ref, aim_ref, bbre_ref, bbim_ref):
    lre = lre_ref[...]
    lim = lim_ref[...]
    dt = jnp.exp(ldt_ref[...])
    mag = jnp.exp(lre * dt)
    a_re = mag * jnp.cos(lim * dt)
    a_im = mag * jnp.sin(lim * dt)
    den = lre * lre + lim * lim
    r_re = ((a_re - 1.0) * lre + a_im * lim) / den
    r_im = (a_im * lre - (a_re - 1.0) * lim) / den
    are_ref[...] = a_re
    aim_ref[...] = a_im
    bbre_ref[...] = r_re * bre_ref[...] - r_im * bim_ref[...]
    bbim_ref[...] = r_re * bim_ref[...] + r_im * bre_ref[...]


def _s5_prepare(lam_re, lam_im, log_dt, b_re, b_im, c_re, c_im):
    lead = (DEPTH, 2, S5_GROUPS)
    full = lead + (S5_CH, S5_STATE)
    rows = DEPTH * 2 * S5_GROUPS * S5_CH

    def expand(t):
        return jnp.broadcast_to(t[:, :, :, None, :], full).reshape(rows, S5_STATE)

    ldt = jnp.broadcast_to(log_dt[:, :, :, None, None], full).reshape(rows, S5_STATE)
    bt = [jnp.swapaxes(t, -1, -2).reshape(rows, S5_STATE) for t in (b_re, b_im)]
    spec = pl.BlockSpec((rows, S5_STATE), lambda: (0, 0))
    a_re, a_im, bb_re, bb_im = pl.pallas_call(
        _s5_prep_kernel,
        in_specs=[spec] * 5,
        out_specs=[spec] * 4,
        out_shape=[jax.ShapeDtypeStruct((rows, S5_STATE), F32)] * 4,
        name="s5_prepare",
    )(expand(lam_re), expand(lam_im), ldt, bt[0], bt[1])
    a = jnp.stack([t.reshape(full)[:, :, :, 0, :].reshape(DEPTH, 2, S5_SP) for t in (a_re, a_im)], axis=2)
    eye = jnp.eye(S5_GROUPS, dtype=F32)

    def in_blockdiag(t):
        t = t.reshape(full)
        return (t[:, :, :, :, None, :] * eye[None, None, :, None, :, None]).reshape(DEPTH, 2, GROUP_WIDTH, S5_SP)

    def out_blockdiag(t):
        t = jnp.swapaxes(t, -1, -2)
        return (t[:, :, :, :, None, :] * eye[None, None, :, None, :, None]).reshape(DEPTH, 2, S5_SP, GROUP_WIDTH)

    bmat = jnp.concatenate([in_blockdiag(bb_re), in_blockdiag(bb_im)], axis=-1).astype(BF16)
    return a, bmat, out_blockdiag(c_re).astype(BF16), out_blockdiag(c_im).astype(BF16)


def _cmul(ar, ai, br, bi):
    return ar * br - ai * bi, ar * bi + ai * br


def _s5_kernel(u_ref, h0_ref, a_ref, bm_ref, cre_ref, cim_ref, dvec_ref, glu_ref, *rest, nseg):
    od_ref, fin_ref, x_scr, y_scr = rest[-4:]
    steps = S5_SEG
    rows = steps * SUBLANES
    chunk = 256
    nchunk = rows // chunk
    seg = lax.broadcasted_iota(jnp.int32, (SUBLANES, S5_SP), 0) % nseg

    for d in range(2):
        def xbody(c, carry):
            r0 = pl.multiple_of(c * chunk, chunk)
            x_scr[pl.ds(r0, chunk), :] = jnp.dot(u_ref[pl.ds(r0, chunk), :].astype(BF16), bm_ref[d],
                                                 preferred_element_type=F32)
            return carry
        lax.fori_loop(0, nchunk, xbody, 0)

        ar = jnp.broadcast_to(a_ref[d, 0:1, :], (SUBLANES, S5_SP))
        ai = jnp.broadcast_to(a_ref[d, 1:2, :], (SUBLANES, S5_SP))

        def scan(init, store):
            def body(t, carry):
                sr, si = carry
                tt = t if d == 0 else steps - 1 - t
                r0 = pl.multiple_of(tt * SUBLANES, SUBLANES)
                pr, pi = _cmul(ar, ai, sr, si)
                nr = pr + x_scr[pl.ds(r0, SUBLANES), 0:S5_SP]
                ni = pi + x_scr[pl.ds(r0, SUBLANES), S5_SP:]
                if store:
                    x_scr[pl.ds(r0, SUBLANES), 0:S5_SP] = nr
                    x_scr[pl.ds(r0, SUBLANES), S5_SP:] = ni
                return nr, ni
            return lax.fori_loop(0, steps, body, init, unroll=4)

        init = (h0_ref[d, :, 0:S5_SP], h0_ref[d, :, S5_SP:])
        if nseg > 1:
            zero = jnp.zeros((SUBLANES, S5_SP), F32)
            fr, fi = scan((zero, zero), store=False)
            pr, pi = ar, ai
            for _ in range(int(math.log2(steps))):
                pr, pi = _cmul(pr, pi, pr, pi)
            cr, ci = init
            shift = 1 if d == 0 else SUBLANES - 1
            order = range(1, nseg) if d == 0 else range(nseg - 2, -1, -1)
            for s in order:
                ncr, nci = pltpu.roll(cr, shift, 0), pltpu.roll(ci, shift, 0)
                nfr, nfi = pltpu.roll(fr, shift, 0), pltpu.roll(fi, shift, 0)
                qr, qi = _cmul(pr, pi, ncr, nci)
                cr = jnp.where(seg == s, qr + nfr, cr)
                ci = jnp.where(seg == s, qi + nfi, ci)
            init = (cr, ci)
        sr, si = scan(init, store=True)
        fin_ref[:, 2 * d * S5_SP:(2 * d + 1) * S5_SP] = sr
        fin_ref[:, (2 * d + 1) * S5_SP:(2 * d + 2) * S5_SP] = si

        def ybody(c, carry):
            r0 = pl.multiple_of(c * chunk, chunk)
            y = (_bdot(x_scr[pl.ds(r0, chunk), 0:S5_SP], cre_ref[d])
                 - _bdot(x_scr[pl.ds(r0, chunk), S5_SP:], cim_ref[d]))
            if d == 0:
                y_scr[pl.ds(r0, chunk), :] = y
            else:
                y_scr[pl.ds(r0, chunk), :] += y
            return carry
        lax.fori_loop(0, nchunk, ybody, 0)

    def obody(c, carry):
        r0 = pl.multiple_of(c * chunk, chunk)
        y = y_scr[pl.ds(r0, chunk), :] + dvec_ref[...] * u_ref[pl.ds(r0, chunk), :]
        zz = jax.nn.gelu(y)
        od_ref[pl.ds(r0, chunk), :] = zz * jax.nn.sigmoid(_bdot(zz, glu_ref[...]))
        return carry
    lax.fori_loop(0, nchunk, obody, 0)


def _s5(du_tm, h0, a, bmat, cre, cim, dvec, glu_bf, layer, *, nseg, fin_layer=0, fin_layers=1,
        prev_fin=None):
    nblk = du_tm.shape[0]
    rows = S5_SEG * SUBLANES
    fin_w = 4 * S5_SP
    in_specs = [pl.BlockSpec((None, rows, GROUP_WIDTH), lambda i: (i, 0, 0)),
                pl.BlockSpec((2, SUBLANES, 2 * S5_SP), lambda i: (0, 0, 0)),
                pl.BlockSpec((None, 2, 2, S5_SP), lambda i: (layer, 0, 0, 0)),
                pl.BlockSpec((None, 2, GROUP_WIDTH, 2 * S5_SP), lambda i: (layer, 0, 0, 0)),
                pl.BlockSpec((None, 2, S5_SP, GROUP_WIDTH), lambda i: (layer, 0, 0, 0)),
                pl.BlockSpec((None, 2, S5_SP, GROUP_WIDTH), lambda i: (layer, 0, 0, 0)),
                pl.BlockSpec((1, GROUP_WIDTH), lambda i: (0, 0)),
                pl.BlockSpec((None, GROUP_WIDTH, GROUP_WIDTH), lambda i: (layer, 0, 0))]
    args = [du_tm.reshape(nblk, rows, GROUP_WIDTH), h0, a, bmat, cre, cim, dvec, glu_bf]
    aliases = {}
    if prev_fin is not None:
        aliases[len(args)] = 1
        in_specs.append(pl.BlockSpec(memory_space=pl.ANY))
        args.append(prev_fin)
    od, fin = pl.pallas_call(
        functools.partial(_s5_kernel, nseg=nseg),
        grid=(nblk,),
        in_specs=in_specs,
        out_specs=[pl.BlockSpec((None, rows, GROUP_WIDTH), lambda i: (i, 0, 0)),
                   pl.BlockSpec((SUBLANES, fin_w), lambda i: (i, fin_layer))],
        out_shape=[jax.ShapeDtypeStruct((nblk, rows, GROUP_WIDTH), F32),
                   jax.ShapeDtypeStruct((nblk * SUBLANES, fin_layers * fin_w), F32)],
        scratch_shapes=[pltpu.VMEM((rows, 2 * S5_SP), F32), pltpu.VMEM((rows, GROUP_WIDTH), F32)],
        input_output_aliases=aliases,
        compiler_params=_cparams("parallel"),
        name="s5",
    )(*args)
    return od.reshape(nblk, S5_SEG, SUBLANES * GROUP_WIDTH), fin


def _out_kernel(x_ref, oa_ref, ob_ref, oc_ref, od_ref, mod_ref, wo_ref, g2_ref, wr_ref, br_ref,
                xm_ref, h2_ref, gate_ref):
    mix = functools.reduce(jnp.add, [
        _bdot(o_ref[...], wo_ref[i * GROUP_WIDTH:(i + 1) * GROUP_WIDTH, :])
        for i, o_ref in enumerate((oa_ref, ob_ref, oc_ref, od_ref))])
    xm = x_ref[...] + mod_ref[2:3, :] * mix
    xm_ref[...] = xm
    h2 = _rms_rows(xm) * g2_ref[...] * (1.0 + mod_ref[4:5, :]) + mod_ref[3:4, :]
    h2_ref[...] = h2.astype(BF16)

    logits = _dot3(h2, wr_ref[...]) + br_ref[...]
    lane = lax.broadcasted_iota(jnp.int32, logits.shape, 1)
    big = jnp.int32(2 ** 30)
    gmask = lane < MOE_GROUPS
    gl = jnp.where(gmask, logits, -jnp.inf)
    gmax = jnp.max(gl, axis=-1, keepdims=True)
    p_top = 1.0 / jnp.sum(jnp.exp(gl - gmax), axis=-1, keepdims=True)
    g_top = jnp.min(jnp.where(gl == gmax, lane, big), axis=-1, keepdims=True)
    e_lane = lane - ROUTER_OFF
    emask = (e_lane >= 0) & (e_lane < MOE_EXPERTS) & ((e_lane // MOE_PER_GROUP) == g_top)
    el = jnp.where(emask, logits, -jnp.inf)
    m1 = jnp.max(el, axis=-1, keepdims=True)
    i1 = jnp.min(jnp.where(el == m1, lane, big), axis=-1, keepdims=True)
    el2 = jnp.where(lane == i1, -jnp.inf, el)
    m2 = jnp.max(el2, axis=-1, keepdims=True)
    i2 = jnp.min(jnp.where(el2 == m2, lane, big), axis=-1, keepdims=True)
    e2 = jnp.exp(m2 - m1)
    den = 1.0 + e2
    gates = (jnp.where(lane == i1, (1.0 / den) * p_top, 0.0)
             + jnp.where(lane == i2, (e2 / den) * p_top, 0.0))
    for g in range(MOE_GROUPS):
        gate_ref[g] = pltpu.roll(gates, LANES - ROUTER_OFF - g * MOE_PER_GROUP, 1)


def _output_stage(x, mixes, mods, mod_row, mod_tokens, wo_bf, g2, wr, br, layer):
    tm = S5_SEG
    n = x.shape[0]
    row = lambda w: pl.BlockSpec((tm, w), lambda i: (i, 0))
    const = lambda shape: pl.BlockSpec(shape, lambda i: (0,) * len(shape))
    return pl.pallas_call(
        _out_kernel,
        grid=(n // tm,),
        in_specs=[row(D_MODEL), row(256), row(256), row(256), _du_spec(1),
                  _mod_spec(layer, mod_row, mod_tokens // tm, 1),
                  pl.BlockSpec((None, D_MODEL, D_MODEL), lambda i: (layer, 0, 0)), const((1, D_MODEL)),
                  const((D_MODEL, LANES)), const((1, LANES))],
        out_specs=[row(D_MODEL), row(D_MODEL), pl.BlockSpec((MOE_GROUPS, tm, LANES), lambda i: (0, i, 0))],
        out_shape=[jax.ShapeDtypeStruct((n, D_MODEL), F32),
                   jax.ShapeDtypeStruct((n, D_MODEL), BF16),
                   jax.ShapeDtypeStruct((MOE_GROUPS, n, LANES), F32)],
        compiler_params=_cparams("parallel"),
        name="output_stage",
    )(x, *mixes, mods, wo_bf, g2, wr, br)


GROUP_HID = MOE_PER_GROUP * MOE_HIDDEN


def _moe_kernel(h2_ref, gate_ref, xm_ref, mod_ref, w1_ref, w3_ref, w2_ref, fg_ref, o_ref, acc_ref, *, final):
    g = pl.program_id(1)
    h2 = h2_ref[...]
    a = jnp.dot(h2, w1_ref[...], preferred_element_type=F32)
    b = jnp.dot(h2, w3_ref[...], preferred_element_type=F32)
    gates = gate_ref[...]
    hid = []
    for e in range(MOE_PER_GROUP):
        sl = slice(e * MOE_HIDDEN, (e + 1) * MOE_HIDDEN)
        hid.append((jax.nn.silu(a[:, sl]) * b[:, sl] * gates[:, e:e + 1]).astype(BF16))
    part = jnp.dot(jnp.concatenate(hid, axis=1), w2_ref[...], preferred_element_type=F32)

    @pl.when(g == 0)
    def _():
        acc_ref[...] = part

    @pl.when(g > 0)
    def _():
        acc_ref[...] += part

    @pl.when(g == MOE_GROUPS - 1)
    def _():
        out = xm_ref[...] + mod_ref[5:6, :] * acc_ref[...]
        if final:
            out = _rms_rows(out) * fg_ref[...]
        o_ref[...] = out


def _moe_weight_kernel(w1_ref, w3_ref, w2_ref, o1_ref, o3_ref, o2_ref):
    for e in range(MOE_PER_GROUP):
        sl = slice(e * MOE_HIDDEN, (e + 1) * MOE_HIDDEN)
        o1_ref[:, sl] = w1_ref[e].astype(BF16)
        o3_ref[:, sl] = w3_ref[e].astype(BF16)
        o2_ref[sl, :] = w2_ref[e].astype(BF16)


def _moe_weights(w1, w3, w2):
    up = pl.BlockSpec((None, MOE_PER_GROUP, D_MODEL, MOE_HIDDEN), lambda l, g: (l, g, 0, 0))
    down = pl.BlockSpec((None, MOE_PER_GROUP, MOE_HIDDEN, D_MODEL), lambda l, g: (l, g, 0, 0))
    out = pl.BlockSpec((None, None, D_MODEL, GROUP_HID), lambda l, g: (l, g, 0, 0))
    shape = jax.ShapeDtypeStruct((DEPTH, MOE_GROUPS, D_MODEL, GROUP_HID), BF16)
    return pl.pallas_call(
        _moe_weight_kernel,
        grid=(DEPTH, MOE_GROUPS),
        in_specs=[up, up, down],
        out_specs=[out, out, out],
        out_shape=[shape, shape, shape],
        compiler_params=_cparams("parallel", "parallel"),
        name="moe_weights",
    )(w1, w3, w2)


def _moe(h2, gates, xm, mods, mod_row, mod_tokens, w1g, w3g, w2g, fg, layer, *, final, tm=512):
    n = h2.shape[0]
    row = lambda w: pl.BlockSpec((tm, w), lambda i, g: (i, 0))
    wspec = pl.BlockSpec((None, None, D_MODEL, GROUP_HID), lambda i, g: (layer, g, 0, 0))
    return pl.pallas_call(
        functools.partial(_moe_kernel, final=final),
        grid=(n // tm, MOE_GROUPS),
        in_specs=[row(D_MODEL), pl.BlockSpec((None, tm, LANES), lambda i, g: (g, i, 0)), row(D_MODEL),
                  _mod_spec(layer, mod_row, mod_tokens // tm, 2),
                  wspec, wspec, wspec,
                  pl.BlockSpec((1, D_MODEL), lambda i, g: (0, 0))],
        out_specs=row(D_MODEL),
        out_shape=jax.ShapeDtypeStruct((n, D_MODEL), F32),
        scratch_shapes=[pltpu.VMEM((tm, D_MODEL), F32)],
        compiler_params=_cparams("parallel", "arbitrary"),
        name="moe",
    )(h2, gates, xm, mods, w1g, w3g, w2g, fg)


def kernel(x_prompt, x_sample, cache_a_k, cache_a_v, cache_b_k, cache_b_v, state_ret, state_ssm, c, c_ctx, mod_w, mod_b, norm1_g, norm2_g, w_in, a_qn_g, a_kn_g, b_rel_bias, ret_decay, ret_gn_g, s5_lam_re, s5_lam_im, s5_log_dt, s5_b_re, s5_b_im, s5_c_re, s5_c_im, s5_d, s5_glu_w, w_out, moe_gw, moe_gb, moe_ew, moe_eb, moe_w1, moe_w3, moe_w2, final_norm_g):
    n_ctx = BATCH * SEQ
    n_lat = DEC_BATCH * DEC_SEQ
    lat_seg = DEC_SEQ // S5_SEG

    cond = jnp.zeros((SUBLANES, D_MODEL), F32).at[0].set(c_ctx).at[1:1 + DEC_BATCH].set(c)
    mods = _modulation(cond, mod_w, mod_b).reshape(DEPTH, SUBLANES, 6, D_MODEL)

    rope_tabs = _rope_tables()
    s5_a, s5_bm, s5_cre, s5_cim = _s5_prepare(s5_lam_re, s5_lam_im, s5_log_dt, s5_b_re, s5_b_im,
                                              s5_c_re, s5_c_im)
    cak = cache_a_k.reshape(DEC_BATCH, DEPTH, PAST_LEN, A_KV_HEADS * HEAD_DIM)
    cav = cache_a_v.reshape(DEC_BATCH, DEPTH, PAST_LEN, A_KV_HEADS * HEAD_DIM)
    cbk = cache_b_k.reshape(DEC_BATCH, DEPTH, PAST_LEN, B_HEADS * HEAD_DIM)
    cbv = cache_b_v.reshape(DEC_BATCH, DEPTH, PAST_LEN, B_HEADS * HEAD_DIM)

    xc = x_prompt.reshape(n_ctx, D_MODEL)
    xs = x_sample.reshape(n_lat, D_MODEL)
    w1_all, w3_all, w2_all = _moe_weights(moe_w1, moe_w3, moe_w2)
    eye_h = jnp.eye(C_HEADS, dtype=F32)
    s0_bd = (state_ret[:, :, :, :, :, None, :] * eye_h[None, None, None, :, None, :, None]).reshape(
        DEC_BATCH, DEPTH, 2, C_HEADS * HEAD_DIM, C_HEADS * HEAD_DIM)
    caches = ret_states = ssm_states = None
    h0_zero = jnp.zeros((2, SUBLANES, 2 * S5_SP), F32)
    w_in_bf = w_in.astype(BF16)
    wo_bf = w_out.astype(BF16)
    glu_bf = s5_glu_w.astype(BF16)
    for l in range(DEPTH):
        final = l == DEPTH - 1
        g1 = norm1_g[l].reshape(1, D_MODEL)
        g2 = norm2_g[l].reshape(1, D_MODEL)
        fg = final_norm_g.reshape(1, D_MODEL)
        qn = jnp.tile(a_qn_g[l], A_HEADS).reshape(1, 256)
        kn = jnp.tile(a_kn_g[l], A_KV_HEADS).reshape(1, 128)
        dec = jnp.broadcast_to(ret_decay[l].reshape(2 * C_HEADS, 1), (2 * C_HEADS, LANES))
        gn = ret_gn_g[l].reshape(1, 256)
        dvec = s5_d[l].reshape(1, GROUP_WIDTH)
        wr = jnp.zeros((D_MODEL, LANES), F32).at[:, :MOE_GROUPS].set(moe_gw[l]).at[
            :, ROUTER_OFF:ROUTER_OFF + MOE_EXPERTS].set(moe_ew[l])
        br = jnp.zeros((1, LANES), F32).at[0, :MOE_GROUPS].set(moe_gb[l]).at[
            0, ROUTER_OFF:ROUTER_OFF + MOE_EXPERTS].set(moe_eb[l])
        na_bias = _na_bias(b_rel_bias[l])

        zc, cg, du_tm, *caches = _project(xc, mods, 0, n_ctx, g1, w_in_bf, qn, kn, None, l, seq_len=SEQ,
                                          with_cache=True, prev_caches=caches)
        oa, ob = _ctx_attention(zc, BATCH, SEQ)
        oc, ret_states = _retention(zc, cg, dec, gn, None, l, nb=BATCH, seq_len=SEQ, prev_state=ret_states)
        od_tm, ssm_states = _s5(du_tm, h0_zero, s5_a, s5_bm, s5_cre, s5_cim, dvec, glu_bf, l,
                                nseg=1, fin_layer=l, fin_layers=DEPTH, prev_fin=ssm_states)
        xm, h2, gates = _output_stage(xc, (oa, ob, oc, od_tm), mods, 0, n_ctx, wo_bf, g2, wr, br, l)
        xc = _moe(h2, gates, xm, mods, 0, n_ctx, w1_all, w3_all, w2_all, fg, l, final=final)

        zs, cg, du_tm = _project(xs, mods, 1, DEC_SEQ, g1, w_in_bf, qn, kn, rope_tabs, l, seq_len=DEC_SEQ)
        oa = _lat_attention_a(zs, cak, cav, l)
        ob = _lat_attention_b(zs, cbk, cbv, na_bias, l)
        oc = _retention(zs, cg, dec, gn, s0_bd, l, nb=DEC_BATCH, seq_len=DEC_SEQ)
        h0 = state_ssm[:, l].reshape(DEC_BATCH, 2, 2 * S5_SP).transpose(1, 0, 2)
        h0_seg = jnp.zeros((2, DEC_BATCH, lat_seg, 2 * S5_SP), F32)
        h0_seg = h0_seg.at[0, :, 0].set(h0[0]).at[1, :, lat_seg - 1].set(h0[1])
        od_tm, _ = _s5(du_tm, h0_seg.reshape(2, SUBLANES, 2 * S5_SP),
                       s5_a, s5_bm, s5_cre, s5_cim, dvec, glu_bf, l, nseg=lat_seg)
        xm, h2, gates = _output_stage(xs, (oa, ob, oc, od_tm), mods, 1, DEC_SEQ, wo_bf, g2, wr, br, l)
        xs = _moe(h2, gates, xm, mods, 1, DEC_SEQ, w1_all, w3_all, w2_all, fg, l, final=final)

    new_ak, new_av, new_bk, new_bv = caches
    return (xc.reshape(BATCH, SEQ, D_MODEL), xs.reshape(DEC_BATCH, DEC_SEQ, D_MODEL),
            new_ak.reshape(BATCH, DEPTH, SEQ, A_KV_HEADS, HEAD_DIM),
            new_av.reshape(BATCH, DEPTH, SEQ, A_KV_HEADS, HEAD_DIM),
            new_bk.reshape(BATCH, DEPTH, SEQ, B_HEADS, HEAD_DIM),
            new_bv.reshape(BATCH, DEPTH, SEQ, B_HEADS, HEAD_DIM),
            ret_states,
            ssm_states.reshape(BATCH, DEPTH, 2, 2, S5_GROUPS, S5_STATE))
```

```python
import functools
import math

import numpy as np
import jax
import jax.numpy as jnp
from jax import lax
from jax.experimental import pallas as pl
from jax.experimental.pallas import tpu as pltpu

F32 = jnp.float32
BF16 = jnp.bfloat16

D_MODEL = 1024
BATCH = 32
SEQ = 256
DEPTH = 2
DEC_BATCH = 2
DEC_SEQ = 1024
PAST_LEN = 256
GRID_W = 64
HEAD_DIM = 64
GROUP_WIDTH = 256
A_HEADS = 4
A_KV_HEADS = 2
B_HEADS = 4
NA_ROWS = 8
NA_COLS = 16
C_HEADS = 4
S5_CH = 16
S5_GROUPS = 16
S5_STATE = 64
MOE_GROUPS = 4
MOE_PER_GROUP = 8
MOE_EXPERTS = 32
MOE_HIDDEN = 128
ROPE_THETA = 10000.0
EPS = 1e-6
IN_WIDTH = 2560
Q_SCALE = HEAD_DIM ** -0.5

OFF_AQ, OFF_AK, OFF_AV = 0, 256, 384
OFF_BQ, OFF_BK, OFF_BV = 512, 768, 1024
OFF_CQ, OFF_CK, OFF_CV, OFF_CG = 1280, 1536, 1792, 2048
OFF_DU = 2304

LANES = 128
SUBLANES = 8
S5_SP = S5_GROUPS * S5_STATE
S5_SEG = 256
ROUTER_OFF = 4
NEG_BIG = -1e30
VMEM_LIMIT = 56 * 1024 * 1024


def _cparams(*sem):
    return pltpu.CompilerParams(dimension_semantics=sem, vmem_limit_bytes=VMEM_LIMIT)


def _mod_spec(layer, first_row, tiles_per_row, grid_rank):
    if grid_rank == 1:
        return pl.BlockSpec((None, None, 6, D_MODEL), lambda i: (layer, first_row + i // tiles_per_row, 0, 0))
    return pl.BlockSpec((None, None, 6, D_MODEL), lambda i, g: (layer, first_row + i // tiles_per_row, 0, 0))


def _bdot(a, b):
    return jnp.dot(a.astype(BF16), b.astype(BF16), preferred_element_type=F32)


def _bdot_nt(a, b):
    return lax.dot_general(a.astype(BF16), b.astype(BF16), (((1,), (1,)), ((), ())),
                           preferred_element_type=F32)


def _bdot_tn(a, b):
    return lax.dot_general(a.astype(BF16), b.astype(BF16), (((0,), (0,)), ((), ())),
                           preferred_element_type=F32)


def _split(a):
    hi = a.astype(BF16)
    lo = (a - hi.astype(F32)).astype(BF16)
    return hi, lo


def _dot_hilo_lhs(a, b_bf16):
    hi, lo = _split(a)
    return (jnp.dot(hi, b_bf16, preferred_element_type=F32)
            + jnp.dot(lo, b_bf16, preferred_element_type=F32))


def _dot3(a, b):
    ah, al = _split(a)
    bh, bl = _split(b)
    return (jnp.dot(ah, bh, preferred_element_type=F32)
            + jnp.dot(ah, bl, preferred_element_type=F32)
            + jnp.dot(al, bh, preferred_element_type=F32))


def _rms_rows(x):
    return x * lax.rsqrt(jnp.mean(x * x, axis=-1, keepdims=True) + EPS)


def _mod_kernel(cond_ref, w_ref, b_ref, o_ref):
    o_ref[...] = _bdot(jax.nn.silu(cond_ref[...]), w_ref[...]) + b_ref[...]


def _modulation(cond, mod_w, mod_b):
    tn = 1536
    return pl.pallas_call(
        _mod_kernel,
        grid=(DEPTH, 6 * D_MODEL // tn),
        in_specs=[pl.BlockSpec((SUBLANES, D_MODEL), lambda l, j: (0, 0)),
                  pl.BlockSpec((None, D_MODEL, tn), lambda l, j: (l, 0, j)),
                  pl.BlockSpec((None, 1, tn), lambda l, j: (l, 0, j))],
        out_specs=pl.BlockSpec((None, SUBLANES, tn), lambda l, j: (l, 0, j)),
        out_shape=jax.ShapeDtypeStruct((DEPTH, SUBLANES, 6 * D_MODEL), F32),
        compiler_params=_cparams("arbitrary", "arbitrary"),
        name="modulation",
    )(cond, mod_w, mod_b.reshape(DEPTH, 1, 6 * D_MODEL))


def _group_mean_matrix(w):
    ri = lax.broadcasted_iota(jnp.int32, (w, w), 0) // HEAD_DIM
    ci = lax.broadcasted_iota(jnp.int32, (w, w), 1) // HEAD_DIM
    return jnp.where(ri == ci, 1.0 / HEAD_DIM, 0.0).astype(BF16)


def _head_norm(t, g):
    ms = _dot_hilo_lhs(t * t, _group_mean_matrix(t.shape[1]))
    return t * lax.rsqrt(ms + EPS) * g


def _rope(t, cos, sa, sb):
    return (t * cos + pltpu.roll(t, LANES - 16, 1) * sa + pltpu.roll(t, 16, 1) * sb)


def _proj_kernel(*refs, rope, n_alias, with_cache):
    x_ref, mod_ref, g1_ref, w_ref, qn_ref, kn_ref = refs[:6]
    n_in = 6
    if rope:
        cos_ref, sa_ref, sb_ref = refs[6:9]
        n_in = 9
    outs = refs[n_in + n_alias:]
    z_ref, cg_ref, du_ref = outs[:3]
    h = _rms_rows(x_ref[...]) * g1_ref[...] * (1.0 + mod_ref[1:2, :]) + mod_ref[0:1, :]
    z = jnp.dot(h.astype(BF16), w_ref[...], preferred_element_type=F32)
    aq = _head_norm(z[:, OFF_AQ:OFF_AK], qn_ref[...])
    ak = _head_norm(z[:, OFF_AK:OFF_AV], kn_ref[...])
    for j in range(3):
        t = aq[:, j * LANES:(j + 1) * LANES] if j < 2 else ak
        if rope:
            cj = 0 if j == 2 else j
            sl = slice(cj * LANES, (cj + 1) * LANES)
            t = _rope(t, cos_ref[:, sl], sa_ref[:, sl], sb_ref[:, sl])
        if j == 2:
            ak = t
        z_ref[:, j * LANES:(j + 1) * LANES] = t.astype(BF16)
    z_ref[:, OFF_AV:OFF_CK] = z[:, OFF_AV:OFF_CK].astype(BF16)
    z_ref[:, OFF_CK:OFF_CV] = (z[:, OFF_CK:OFF_CV] * Q_SCALE).astype(BF16)
    z_ref[:, OFF_CV:OFF_CG] = z[:, OFF_CV:OFF_CG].astype(BF16)
    cg_ref[...] = z[:, OFF_CG:OFF_DU]
    du_ref[...] = z[:, OFF_DU:]
    if with_cache:
        ak_ref, av_ref, bk_ref, bv_ref = outs[3:7]
        ak_ref[...] = ak
        av_ref[...] = z[:, OFF_AV:OFF_BQ]
        bk_ref[...] = z[:, OFF_BK:OFF_BV]
        bv_ref[...] = z[:, OFF_BV:OFF_CQ]


def _du_spec(grid_rank):
    if grid_rank == 1:
        return pl.BlockSpec((None, S5_SEG, GROUP_WIDTH), lambda i: (i // SUBLANES, 0, i % SUBLANES))
    return pl.BlockSpec((None, S5_SEG, GROUP_WIDTH), lambda i, g: (i // SUBLANES, 0, i % SUBLANES))


def _project(x, mods, mod_row, mod_tokens, g1, w_in_bf, qn, kn, rope_tabs, layer, *, seq_len,
             with_cache=False, prev_caches=None):
    tm = S5_SEG
    n = x.shape[0]
    rope = rope_tabs is not None
    in_specs = [pl.BlockSpec((tm, D_MODEL), lambda i: (i, 0)),
                _mod_spec(layer, mod_row, mod_tokens // tm, 1),
                pl.BlockSpec((1, D_MODEL), lambda i: (0, 0)),
                pl.BlockSpec((None, D_MODEL, IN_WIDTH), lambda i: (layer, 0, 0)),
                pl.BlockSpec((1, 256), lambda i: (0, 0)),
                pl.BlockSpec((1, 128), lambda i: (0, 0))]
    args = [x, mods, g1, w_in_bf, qn, kn]
    if rope:
        tps = seq_len // tm
        in_specs += [pl.BlockSpec((tm, 256), lambda i: (i % tps, 0))] * 3
        args += list(rope_tabs)
    out_specs = [pl.BlockSpec((tm, OFF_CG), lambda i: (i, 0)),
                 pl.BlockSpec((tm, GROUP_WIDTH), lambda i: (i, 0)), _du_spec(1)]
    out_shape = [jax.ShapeDtypeStruct((n, OFF_CG), BF16),
                 jax.ShapeDtypeStruct((n, GROUP_WIDTH), F32),
                 jax.ShapeDtypeStruct((n // (tm * SUBLANES), S5_SEG, SUBLANES * GROUP_WIDTH), F32)]
    aliases = {}
    n_alias = 0
    if with_cache:
        assert tm == seq_len
        nb = n // seq_len
        for w in (128, 128, 256, 256):
            out_specs.append(pl.BlockSpec((None, None, seq_len, w), lambda i: (i, layer, 0, 0)))
            out_shape.append(jax.ShapeDtypeStruct((nb, DEPTH, seq_len, w), F32))
        if prev_caches is not None:
            n_alias = len(prev_caches)
            for k, arr in enumerate(prev_caches):
                aliases[len(args)] = 3 + k
                in_specs.append(pl.BlockSpec(memory_space=pl.ANY))
                args.append(arr)
    return pl.pallas_call(
        functools.partial(_proj_kernel, rope=rope, n_alias=n_alias, with_cache=with_cache),
        grid=(n // tm,),
        in_specs=in_specs,
        out_specs=out_specs,
        out_shape=out_shape,
        input_output_aliases=aliases,
        compiler_params=_cparams("parallel"),
        name="project",
    )(*args)


def _rope_tables():
    t = jnp.arange(DEC_SEQ)
    row = (t // GRID_W).astype(F32)
    col = (t % GRID_W).astype(F32)
    nf = HEAD_DIM // 4
    inv = ROPE_THETA ** (-jnp.arange(nf, dtype=F32) / nf)
    ang_r = row[:, None] * inv[None, :]
    ang_c = col[:, None] * inv[None, :]
    zeros = jnp.zeros_like(ang_r)
    cos = jnp.concatenate([jnp.cos(ang_r), jnp.cos(ang_r), jnp.cos(ang_c), jnp.cos(ang_c)], axis=-1)
    sa = jnp.concatenate([-jnp.sin(ang_r), zeros, -jnp.sin(ang_c), zeros], axis=-1)
    sb = jnp.concatenate([zeros, jnp.sin(ang_r), zeros, jnp.sin(ang_c)], axis=-1)
    return tuple(jnp.tile(a, (1, 4)) for a in (cos, sa, sb))


N_HEADS = 4


def _lane_head(width):
    return lax.broadcasted_iota(jnp.int32, (1, width), 1) // HEAD_DIM


def _stack_heads(q):
    head = _lane_head(q.shape[1])
    return jnp.concatenate([jnp.where(head == h, q, 0.0) for h in range(N_HEADS)], axis=0).astype(BF16)


def _stack_heads_gqa(q):
    lo = lax.broadcasted_iota(jnp.int32, (1, LANES), 1) < HEAD_DIM
    q = q.astype(F32)
    q01, q23 = q[:, :LANES], q[:, LANES:]
    blocks = [jnp.where(lo, q01, 0.0), jnp.where(lo, pltpu.roll(q01, HEAD_DIM, 1), 0.0),
              jnp.where(lo, 0.0, pltpu.roll(q23, HEAD_DIM, 1)), jnp.where(lo, 0.0, q23)]
    return jnp.concatenate(blocks, axis=0).astype(BF16)


def _spread_kv_gqa(v):
    lo = lax.broadcasted_iota(jnp.int32, (1, LANES), 1) < HEAD_DIM
    v = v.astype(F32)
    vr = pltpu.roll(v, HEAD_DIM, 1)
    return jnp.concatenate([jnp.where(lo, v, vr), jnp.where(lo, vr, v)], axis=1)


def _mha(qs, blocks, tq):
    scores = []
    for k, _, bias in blocks:
        s = _bdot_nt(qs, k)
        scores.append(s if bias is None else s + bias)
    m = functools.reduce(jnp.maximum, [jnp.max(s, axis=-1, keepdims=True) for s in scores])
    es = [jnp.exp(s - m) for s in scores]
    denom = functools.reduce(jnp.add, [jnp.sum(e, axis=-1, keepdims=True) for e in es])
    ps = [e.astype(BF16) for e in es]
    head = _lane_head(N_HEADS * HEAD_DIM)
    vals = [v.astype(BF16) for _, v, _ in blocks]
    o = None
    dall = None
    for h in range(N_HEADS):
        rows = slice(h * tq, (h + 1) * tq)
        for p, v in zip(ps, vals):
            t = jnp.dot(p[rows], jnp.where(head == h, v, jnp.zeros_like(v)), preferred_element_type=F32)
            o = t if o is None else o + t
        d = jnp.where(head == h, denom[rows], 0.0)
        dall = d if dall is None else dall + d
    return o / dall


def _ctx_attn_kernel(aq_ref, ak_ref, av_ref, bq_ref, bk_ref, bv_ref, oa_ref, ob_ref):
    tq = aq_ref.shape[0]
    oa_ref[...] = _mha(_stack_heads_gqa(aq_ref[...] * Q_SCALE),
                       [(ak_ref[...], _spread_kv_gqa(av_ref[...]), None)], tq)
    ob_ref[...] = _mha(_stack_heads(bq_ref[...] * Q_SCALE), [(bk_ref[...], bv_ref[...], None)], tq)


def _ctx_attention(z, nb, seq_len):
    def col(width, off):
        return pl.BlockSpec((seq_len, width), lambda b: (b, off // width))
    return pl.pallas_call(
        _ctx_attn_kernel,
        grid=(nb,),
        in_specs=[col(256, OFF_AQ), col(128, OFF_AK), col(128, OFF_AV),
                  col(256, OFF_BQ), col(256, OFF_BK), col(256, OFF_BV)],
        out_specs=[pl.BlockSpec((seq_len, 256), lambda b: (b, 0))] * 2,
        out_shape=[jax.ShapeDtypeStruct((nb * seq_len, 256), F32)] * 2,
        compiler_params=_cparams("parallel"),
        name="ctx_attention",
    )(z, z, z, z, z, z)


def _lat_attn_a_kernel(q_ref, kn_ref, vn_ref, kc_ref, vc_ref, o_ref):
    o_ref[...] = _mha(_stack_heads_gqa(q_ref[...] * Q_SCALE),
                      [(kc_ref[...], _spread_kv_gqa(vc_ref[...]), None),
                       (kn_ref[...], _spread_kv_gqa(vn_ref[...]), None)], q_ref.shape[0])


def _lat_attention_a(z, cache_k, cache_v, layer, tq=256):
    nq = DEC_SEQ // tq
    cache_spec = pl.BlockSpec((None, None, PAST_LEN, 128), lambda b, j: (b, layer, 0, 0))
    return pl.pallas_call(
        _lat_attn_a_kernel,
        grid=(DEC_BATCH, nq),
        in_specs=[pl.BlockSpec((tq, 256), lambda b, j: (b * nq + j, OFF_AQ // 256)),
                  pl.BlockSpec((DEC_SEQ, 128), lambda b, j: (b, OFF_AK // 128)),
                  pl.BlockSpec((DEC_SEQ, 128), lambda b, j: (b, OFF_AV // 128)),
                  cache_spec, cache_spec],
        out_specs=pl.BlockSpec((tq, 256), lambda b, j: (b * nq + j, 0)),
        out_shape=jax.ShapeDtypeStruct((DEC_BATCH * DEC_SEQ, 256), F32),
        compiler_params=_cparams("parallel", "parallel"),
        name="lat_attention_a",
    )(z, z, z, cache_k, cache_v)


NA_KEYS = NA_ROWS * GRID_W


NA_PAIRS = 2 * NA_ROWS - 2


def _na_kernel(q_ref, k_ref, v_ref, kc_ref, vc_ref, bias_ref, o_ref):
    r = pl.program_id(1)
    rows = DEC_SEQ // GRID_W
    row_start = jnp.clip(r - NA_ROWS // 2, 0, rows - NA_ROWS)
    start = pl.multiple_of(row_start * GRID_W, GRID_W)
    rel0 = row_start - r + NA_ROWS - 1
    kl = k_ref[pl.ds(start, NA_KEYS), :]
    vl = v_ref[pl.ds(start, NA_KEYS), :]
    bias = jnp.concatenate(
        [jnp.concatenate([bias_ref[h, rel0 + 2 * jp] for jp in range(NA_ROWS // 2)], axis=1)
         for h in range(B_HEADS)], axis=0)
    o_ref[...] = _mha(_stack_heads(q_ref[...] * Q_SCALE),
                      [(kl, vl, bias), (kc_ref[...], vc_ref[...], None)], GRID_W)


def _na_bias(rel_bias):
    nrel = 2 * NA_COLS - 1
    period = 2 * GRID_W
    b = rel_bias.astype(F32)
    ext = jnp.concatenate([b[..., NA_COLS - 1:],
                           jnp.zeros(b.shape[:-1] + (period - nrel,), F32),
                           b[..., :NA_COLS - 1]], axis=-1)
    flat = jnp.tile(ext, (1, 1, GRID_W))[..., :GRID_W * (period - 1)]
    toe = flat.reshape(b.shape[:-1] + (GRID_W, period - 1))[..., :GRID_W]
    col_start = np.clip(np.arange(GRID_W) - NA_COLS // 2, 0, GRID_W - NA_COLS)
    kc = np.arange(GRID_W)
    inside = (kc[None, :] >= col_start[:, None]) & (kc[None, :] < col_start[:, None] + NA_COLS)
    toe = jnp.where(jnp.asarray(inside), toe, NEG_BIG)
    return jnp.concatenate([toe[:, :-1], toe[:, 1:]], axis=-1)


def _lat_attention_b(z, cache_k, cache_v, bias, layer):
    rows = DEC_SEQ // GRID_W
    cache_spec = pl.BlockSpec((None, None, PAST_LEN, 256), lambda b, r: (b, layer, 0, 0))
    return pl.pallas_call(
        _na_kernel,
        grid=(DEC_BATCH, rows),
        in_specs=[pl.BlockSpec((GRID_W, 256), lambda b, r: (b * rows + r, OFF_BQ // 256)),
                  pl.BlockSpec((DEC_SEQ, 256), lambda b, r: (b, OFF_BK // 256)),
                  pl.BlockSpec((DEC_SEQ, 256), lambda b, r: (b, OFF_BV // 256)),
                  cache_spec, cache_spec,
                  pl.BlockSpec((B_HEADS, NA_PAIRS, GRID_W, 2 * GRID_W), lambda b, r: (0, 0, 0, 0))],
        out_specs=pl.BlockSpec((GRID_W, 256), lambda b, r: (b * rows + r, 0)),
        out_shape=jax.ShapeDtypeStruct((DEC_BATCH * DEC_SEQ, 256), F32),
        compiler_params=_cparams("parallel", "parallel"),
        name="lat_attention_b",
    )(z, z, z, cache_k, cache_v, bias)


def _retention_kernel(q_ref, g_ref, k_ref, v_ref, dec_ref, gn_ref, *rest, seq_len, tq, has_state,
                      hoist_decay):
    if has_state:
        s0_ref, o_ref, dec_scr = rest
    else:
        o_ref, st_ref, dec_scr = rest[-3:]
    head = _lane_head(C_HEADS * HEAD_DIM)
    lg = jax.nn.log_sigmoid(dec_ref[...])

    def per_lane(row0):
        out = jnp.zeros((1, C_HEADS * HEAD_DIM), F32)
        for h in range(C_HEADS):
            out = jnp.where(head == h, lg[row0 + h:row0 + h + 1, 0:1], out)
        return out

    lgf_l, lgb_l = per_lane(0), per_lane(C_HEADS)
    i0 = pl.program_id(1) * tq
    qi = (i0 + lax.broadcasted_iota(jnp.int32, (tq, 1), 0)).astype(F32)

    def fill_decay():
        kj = lax.broadcasted_iota(jnp.int32, (1, seq_len), 1).astype(F32)
        diff = qi - kj
        for h in range(C_HEADS):
            lgf = lg[h:h + 1, 0:1]
            lgb = lg[C_HEADS + h:C_HEADS + h + 1, 0:1]
            dec_scr[h * tq:(h + 1) * tq, :] = (
                jnp.where(diff >= 0, jnp.exp(lgf * jnp.maximum(diff, 0.0)), 0.0)
                + jnp.where(diff <= 0, jnp.exp(lgb * jnp.maximum(-diff, 0.0)), 0.0))

    if hoist_decay:
        pl.when(pl.program_id(0) == 0)(fill_decay)
    else:
        fill_decay()

    q = q_ref[...]
    k = k_ref[...]
    v = v_ref[...].astype(BF16)
    sc = (_bdot_nt(_stack_heads(q), k) * dec_scr[...]).astype(BF16)
    o = None
    for h in range(C_HEADS):
        t = jnp.dot(sc[h * tq:(h + 1) * tq], jnp.where(head == h, v, jnp.zeros_like(v)),
                    preferred_element_type=F32)
        o = t if o is None else o + t
    if has_state:
        o = (o + _bdot(q, s0_ref[0]) * jnp.exp(lgf_l * (qi + 1.0))
             + _bdot(q, s0_ref[1]) * jnp.exp(lgb_l * (seq_len - qi)))
    gm = _group_mean_matrix(C_HEADS * HEAD_DIM)
    dlt = o - _dot_hilo_lhs(o, gm)
    var = _dot_hilo_lhs(dlt * dlt, gm)
    o_ref[...] = dlt * lax.rsqrt(var + EPS) * gn_ref[...] * jax.nn.silu(g_ref[...])
    if not has_state:
        kpos = lax.broadcasted_iota(jnp.int32, (seq_len, 1), 0).astype(F32)
        sf = _bdot_tn(k * jnp.exp(lgf_l * (seq_len - 1.0 - kpos)), v)
        sb = _bdot_tn(k * jnp.exp(lgb_l * kpos), v)
        for h in range(C_HEADS):
            sl = slice(h * HEAD_DIM, (h + 1) * HEAD_DIM)
            st_ref[0, h] = sf[sl, sl]
            st_ref[1, h] = sb[sl, sl]


def _retention(z, cg, dec, gn, s0, layer, *, nb, seq_len, prev_state=None, tq=256):
    nq = seq_len // tq
    has_state = s0 is not None
    aliases = {}
    in_specs = [pl.BlockSpec((tq, 256), lambda b, j: (b * nq + j, OFF_CQ // 256)),
                pl.BlockSpec((tq, 256), lambda b, j: (b * nq + j, 0)),
                pl.BlockSpec((seq_len, 256), lambda b, j: (b, OFF_CK // 256)),
                pl.BlockSpec((seq_len, 256), lambda b, j: (b, OFF_CV // 256)),
                pl.BlockSpec((SUBLANES, LANES), lambda b, j: (0, 0)),
                pl.BlockSpec((1, 256), lambda b, j: (0, 0))]
    args = [z, cg, z, z, dec, gn]
    o_spec = pl.BlockSpec((tq, 256), lambda b, j: (b * nq + j, 0))
    o_shape = jax.ShapeDtypeStruct((nb * seq_len, 256), F32)
    if has_state:
        in_specs.append(pl.BlockSpec((None, None, 2, 256, 256), lambda b, j: (b, layer, 0, 0, 0)))
        args.append(s0)
        out_specs, out_shape = o_spec, o_shape
    else:
        assert nq == 1
        out_specs = [o_spec, pl.BlockSpec((None, None, 2, C_HEADS, HEAD_DIM, HEAD_DIM),
                                          lambda b, j: (b, layer, 0, 0, 0, 0))]
        out_shape = [o_shape, jax.ShapeDtypeStruct((nb, DEPTH, 2, C_HEADS, HEAD_DIM, HEAD_DIM), F32)]
        if prev_state is not None:
            aliases[len(args)] = 1
            in_specs.append(pl.BlockSpec(memory_space=pl.ANY))
            args.append(prev_state)
    return pl.pallas_call(
        functools.partial(_retention_kernel, seq_len=seq_len, tq=tq, has_state=has_state,
                          hoist_decay=nq == 1),
        grid=(nb, nq),
        in_specs=in_specs,
        out_specs=out_specs,
        out_shape=out_shape,
        scratch_shapes=[pltpu.VMEM((C_HEADS * tq, seq_len), F32)],
        input_output_aliases=aliases,
        compiler_params=_cparams("arbitrary", "arbitrary"),
        name="retention",
    )(*args)


def _s5_prep_kernel(lre_ref, lim_ref, ldt_ref, bre_ref, bim_ref, are_ref, aim_ref, bbre_ref, bbim_ref):
    lre = lre_ref[...]
    lim = lim_ref[...]
    dt = jnp.exp(ldt_ref[...])
    mag = jnp.exp(lre * dt)
    a_re = mag * jnp.cos(lim * dt)
    a_im = mag * jnp.sin(lim * dt)
    den = lre * lre + lim * lim
    r_re = ((a_re - 1.0) * lre + a_im * lim) / den
    r_im = (a_im * lre - (a_re - 1.0) * lim) / den
    are_ref[...] = a_re
    aim_ref[...] = a_im
    bbre_ref[...] = r_re * bre_ref[...] - r_im * bim_ref[...]
    bbim_ref[...] = r_re * bim_ref[...] + r_im * bre_ref[...]


def _s5_prepare(lam_re, lam_im, log_dt, b_re, b_im, c_re, c_im):
    lead = (DEPTH, 2, S5_GROUPS)
    full = lead + (S5_CH, S5_STATE)
    rows = DEPTH * 2 * S5_GROUPS * S5_CH

    def expand(t):
        return jnp.broadcast_to(t[:, :, :, None, :], full).reshape(rows, S5_STATE)

    ldt = jnp.broadcast_to(log_dt[:, :, :, None, None], full).reshape(rows, S5_STATE)
    bt = [jnp.swapaxes(t, -1, -2).reshape(rows, S5_STATE) for t in (b_re, b_im)]
    spec = pl.BlockSpec((rows, S5_STATE), lambda: (0, 0))
    a_re, a_im, bb_re, bb_im = pl.pallas_call(
        _s5_prep_kernel,
        in_specs=[spec] * 5,
        out_specs=[spec] * 4,
        out_shape=[jax.ShapeDtypeStruct((rows, S5_STATE), F32)] * 4,
        name="s5_prepare",
    )(expand(lam_re), expand(lam_im), ldt, bt[0], bt[1])
    a = jnp.stack([t.reshape(full)[:, :, :, 0, :].reshape(DEPTH, 2, S5_SP) for t in (a_re, a_im)], axis=2)
    eye = jnp.eye(S5_GROUPS, dtype=F32)

    def in_blockdiag(t):
        t = t.reshape(full)
        return (t[:, :, :, :, None, :] * eye[None, None, :, None, :, None]).reshape(DEPTH, 2, GROUP_WIDTH, S5_SP)

    def out_blockdiag(t):
        t = jnp.swapaxes(t, -1, -2)
        return (t[:, :, :, :, None, :] * eye[None, None, :, None, :, None]).reshape(DEPTH, 2, S5_SP, GROUP_WIDTH)

    bmat = jnp.concatenate([in_blockdiag(bb_re), in_blockdiag(bb_im)], axis=-1).astype(BF16)
    return a, bmat, out_blockdiag(c_re).astype(BF16), out_blockdiag(c_im).astype(BF16)


def _cmul(ar, ai, br, bi):
    return ar * br - ai * bi, ar * bi + ai * br


def _s5_kernel(u_ref, h0_ref, a_ref, bm_ref, cre_ref, cim_ref, dvec_ref, glu_ref, *rest, nseg):
    od_ref, fin_ref, x_scr, y_scr = rest[-4:]
    steps = S5_SEG
    rows = steps * SUBLANES
    chunk = 256
    nchunk = rows // chunk
    seg = lax.broadcasted_iota(jnp.int32, (SUBLANES, S5_SP), 0) % nseg

    for d in range(2):
        def xbody(c, carry):
            r0 = pl.multiple_of(c * chunk, chunk)
            x_scr[pl.ds(r0, chunk), :] = jnp.dot(u_ref[pl.ds(r0, chunk), :].astype(BF16), bm_ref[d],
                                                 preferred_element_type=F32)
            return carry
        lax.fori_loop(0, nchunk, xbody, 0)

        ar = jnp.broadcast_to(a_ref[d, 0:1, :], (SUBLANES, S5_SP))
        ai = jnp.broadcast_to(a_ref[d, 1:2, :], (SUBLANES, S5_SP))

        def scan(init, store):
            def body(t, carry):
                sr, si = carry
                tt = t if d == 0 else steps - 1 - t
                r0 = pl.multiple_of(tt * SUBLANES, SUBLANES)
                pr, pi = _cmul(ar, ai, sr, si)
                nr = pr + x_scr[pl.ds(r0, SUBLANES), 0:S5_SP]
                ni = pi + x_scr[pl.ds(r0, SUBLANES), S5_SP:]
                if store:
                    x_scr[pl.ds(r0, SUBLANES), 0:S5_SP] = nr
                    x_scr[pl.ds(r0, SUBLANES), S5_SP:] = ni
                return nr, ni
            return lax.fori_loop(0, steps, body, init, unroll=4)

        init = (h0_ref[d, :, 0:S5_SP], h0_ref[d, :, S5_SP:])
        if nseg > 1:
            zero = jnp.zeros((SUBLANES, S5_SP), F32)
            fr, fi = scan((zero, zero), store=False)
            pr, pi = ar, ai
            for _ in range(int(math.log2(steps))):
                pr, pi = _cmul(pr, pi, pr, pi)
            cr, ci = init
            shift = 1 if d == 0 else SUBLANES - 1
            order = range(1, nseg) if d == 0 else range(nseg - 2, -1, -1)
            for s in order:
                ncr, nci = pltpu.roll(cr, shift, 0), pltpu.roll(ci, shift, 0)
                nfr, nfi = pltpu.roll(fr, shift, 0), pltpu.roll(fi, shift, 0)
                qr, qi = _cmul(pr, pi, ncr, nci)
                cr = jnp.where(seg == s, qr + nfr, cr)
                ci = jnp.where(seg == s, qi + nfi, ci)
            init = (cr, ci)
        sr, si = scan(init, store=True)
        fin_ref[:, 2 * d * S5_SP:(2 * d + 1) * S5_SP] = sr
        fin_ref[:, (2 * d + 1) * S5_SP:(2 * d + 2) * S5_SP] = si

        def ybody(c, carry):
            r0 = pl.multiple_of(c * chunk, chunk)
            y = (_bdot(x_scr[pl.ds(r0, chunk), 0:S5_SP], cre_ref[d])
                 - _bdot(x_scr[pl.ds(r0, chunk), S5_SP:], cim_ref[d]))
            if d == 0:
                y_scr[pl.ds(r0, chunk), :] = y
            else:
                y_scr[pl.ds(r0, chunk), :] += y
            return carry
        lax.fori_loop(0, nchunk, ybody, 0)

    def obody(c, carry):
        r0 = pl.multiple_of(c * chunk, chunk)
        y = y_scr[pl.ds(r0, chunk), :] + dvec_ref[...] * u_ref[pl.ds(r0, chunk), :]
        zz = jax.nn.gelu(y)
        od_ref[pl.ds(r0, chunk), :] = zz * jax.nn.sigmoid(_bdot(zz, glu_ref[...]))
        return carry
    lax.fori_loop(0, nchunk, obody, 0)


def _s5(du_tm, h0, a, bmat, cre, cim, dvec, glu_bf, layer, *, nseg, fin_layer=0, fin_layers=1,
        prev_fin=None):
    nblk = du_tm.shape[0]
    rows = S5_SEG * SUBLANES
    fin_w = 4 * S5_SP
    in_specs = [pl.BlockSpec((None, rows, GROUP_WIDTH), lambda i: (i, 0, 0)),
                pl.BlockSpec((2, SUBLANES, 2 * S5_SP), lambda i: (0, 0, 0)),
                pl.BlockSpec((None, 2, 2, S5_SP), lambda i: (layer, 0, 0, 0)),
                pl.BlockSpec((None, 2, GROUP_WIDTH, 2 * S5_SP), lambda i: (layer, 0, 0, 0)),
                pl.BlockSpec((None, 2, S5_SP, GROUP_WIDTH), lambda i: (layer, 0, 0, 0)),
                pl.BlockSpec((None, 2, S5_SP, GROUP_WIDTH), lambda i: (layer, 0, 0, 0)),
                pl.BlockSpec((1, GROUP_WIDTH), lambda i: (0, 0)),
                pl.BlockSpec((None, GROUP_WIDTH, GROUP_WIDTH), lambda i: (layer, 0, 0))]
    args = [du_tm.reshape(nblk, rows, GROUP_WIDTH), h0, a, bmat, cre, cim, dvec, glu_bf]
    aliases = {}
    if prev_fin is not None:
        aliases[len(args)] = 1
        in_specs.append(pl.BlockSpec(memory_space=pl.ANY))
        args.append(prev_fin)
    od, fin = pl.pallas_call(
        functools.partial(_s5_kernel, nseg=nseg),
        grid=(nblk,),
        in_specs=in_specs,
        out_specs=[pl.BlockSpec((None, rows, GROUP_WIDTH), lambda i: (i, 0, 0)),
                   pl.BlockSpec((SUBLANES, fin_w), lambda i: (i, fin_layer))],
        out_shape=[jax.ShapeDtypeStruct((nblk, rows, GROUP_WIDTH), F32),
                   jax.ShapeDtypeStruct((nblk * SUBLANES, fin_layers * fin_w), F32)],
        scratch_shapes=[pltpu.VMEM((rows, 2 * S5_SP), F32), pltpu.VMEM((rows, GROUP_WIDTH), F32)],
        input_output_aliases=aliases,
        compiler_params=_cparams("parallel"),
        name="s5",
    )(*args)
    return od.reshape(nblk, S5_SEG, SUBLANES * GROUP_WIDTH), fin


ROUTE_GROUP = MOE_PER_GROUP


def _out_kernel(x_ref, oa_ref, ob_ref, oc_ref, od_ref, mod_ref, wo_ref, g2_ref, wr_ref, br_ref,
                xm_ref, h2_ref, route_ref, cnt_ref):
    mix = functools.reduce(jnp.add, [
        _bdot(o_ref[...], wo_ref[i * GROUP_WIDTH:(i + 1) * GROUP_WIDTH, :])
        for i, o_ref in enumerate((oa_ref, ob_ref, oc_ref, od_ref))])
    xm = x_ref[...] + mod_ref[2:3, :] * mix
    xm_ref[...] = xm
    h2 = _rms_rows(xm) * g2_ref[...] * (1.0 + mod_ref[4:5, :]) + mod_ref[3:4, :]
    h2_ref[...] = h2.astype(BF16)

    logits = _dot3(h2, wr_ref[...]) + br_ref[...]
    lane = lax.broadcasted_iota(jnp.int32, logits.shape, 1)
    big = jnp.int32(2 ** 30)
    gmask = lane < MOE_GROUPS
    gl = jnp.where(gmask, logits, -jnp.inf)
    gmax = jnp.max(gl, axis=-1, keepdims=True)
    p_top = 1.0 / jnp.sum(jnp.exp(gl - gmax), axis=-1, keepdims=True)
    g_top = jnp.min(jnp.where(gl == gmax, lane, big), axis=-1, keepdims=True)
    e_lane = lane - ROUTER_OFF
    emask = (e_lane >= 0) & (e_lane < MOE_EXPERTS) & ((e_lane // MOE_PER_GROUP) == g_top)
    el = jnp.where(emask, logits, -jnp.inf)
    m1 = jnp.max(el, axis=-1, keepdims=True)
    i1 = jnp.min(jnp.where(el == m1, lane, big), axis=-1, keepdims=True)
    el2 = jnp.where(lane == i1, -jnp.inf, el)
    m2 = jnp.max(el2, axis=-1, keepdims=True)
    i2 = jnp.min(jnp.where(el2 == m2, lane, big), axis=-1, keepdims=True)
    e2 = jnp.exp(m2 - m1)
    den = 1.0 + e2
    gates = (jnp.where(lane == i1, (1.0 / den) * p_top, 0.0)
             + jnp.where(lane == i2, (e2 / den) * p_top, 0.0))
    route = jnp.where(lane == ROUTE_GROUP + g_top, 1.0, 0.0)
    for g in range(MOE_GROUPS):
        local = pltpu.roll(gates, LANES - ROUTER_OFF - g * MOE_PER_GROUP, 1)
        route = route + jnp.where((g_top == g) & (lane < MOE_PER_GROUP), local, 0.0)
    route_ref[...] = route
    cnt_ref[...] = jnp.broadcast_to(jnp.sum(route, axis=0, keepdims=True), (SUBLANES, LANES)).astype(jnp.int32)


def _output_stage(x, mixes, mods, mod_row, mod_tokens, wo_bf, g2, wr, br, layer):
    tm = S5_SEG
    n = x.shape[0]
    row = lambda w: pl.BlockSpec((tm, w), lambda i: (i, 0))
    const = lambda shape: pl.BlockSpec(shape, lambda i: (0,) * len(shape))
    return pl.pallas_call(
        _out_kernel,
        grid=(n // tm,),
        in_specs=[row(D_MODEL), row(256), row(256), row(256), _du_spec(1),
                  _mod_spec(layer, mod_row, mod_tokens // tm, 1),
                  pl.BlockSpec((None, D_MODEL, D_MODEL), lambda i: (layer, 0, 0)), const((1, D_MODEL)),
                  const((D_MODEL, LANES)), const((1, LANES))],
        out_specs=[row(D_MODEL), row(D_MODEL), row(LANES),
                   pl.BlockSpec((None, SUBLANES, LANES), lambda i: (i, 0, 0))],
        out_shape=[jax.ShapeDtypeStruct((n, D_MODEL), F32),
                   jax.ShapeDtypeStruct((n, D_MODEL), BF16),
                   jax.ShapeDtypeStruct((n, LANES), F32),
                   jax.ShapeDtypeStruct((n // tm, SUBLANES, LANES), jnp.int32)],
        compiler_params=_cparams("parallel"),
        name="output_stage",
    )(x, *mixes, mods, wo_bf, g2, wr, br)


GROUP_HID = MOE_PER_GROUP * MOE_HIDDEN


MOE_CHUNK = 128


def _moe_kernel(cnt_ref, h2_ref, route_ref, xm_ref, mod_ref, w1_ref, w3_ref, w2_ref, fg_ref, o_ref,
                hs_scr, rs_scr, os_scr, *, final, tm):
    i = pl.program_id(0)
    off1 = cnt_ref[i, 0]
    off2 = off1 + cnt_ref[i, 1]
    off3 = off2 + cnt_ref[i, 2]
    starts = (jnp.int32(0), off1, off2, off3)
    ends = (off1, off2, off3, jnp.int32(tm))

    route = route_ref[...]
    r_hi, r_lo = _split(route)
    pick = (lax.broadcasted_iota(jnp.int32, (SUBLANES, LANES), 1)
            == ROUTE_GROUP + lax.broadcasted_iota(jnp.int32, (SUBLANES, LANES), 0))
    gt = lax.dot_general(jnp.where(pick, 1.0, 0.0).astype(BF16), r_hi, (((1,), (1,)), ((), ())),
                         preferred_element_type=F32)
    before = (lax.broadcasted_iota(jnp.int32, (tm, tm), 0)
              < lax.broadcasted_iota(jnp.int32, (tm, tm), 1))
    rank = jnp.dot(gt.astype(BF16), jnp.where(before, 1.0, 0.0).astype(BF16),
                   preferred_element_type=F32)
    gt_i = gt.astype(jnp.int32)
    rank_i = rank.astype(jnp.int32)
    pos = jnp.zeros((1, tm), jnp.int32)
    for g in range(MOE_GROUPS):
        pos = pos + gt_i[g:g + 1, :] * (rank_i[g:g + 1, :] + starts[g])
    perm = jnp.where(lax.broadcasted_iota(jnp.int32, (tm, tm), 0) == pos, 1.0, 0.0).astype(BF16)
    hs_scr[...] = jnp.dot(perm, h2_ref[...], preferred_element_type=F32).astype(BF16)
    rs_scr[...] = (jnp.dot(perm, r_hi, preferred_element_type=F32)
                   + jnp.dot(perm, r_lo, preferred_element_type=F32))

    for k in range(tm // MOE_CHUNK):
        r0 = k * MOE_CHUNK
        rows = slice(r0, r0 + MOE_CHUNK)
        first = r0
        last = r0 + MOE_CHUNK - 1
        g_lo = sum((first >= s).astype(jnp.int32) for s in starts[1:])
        g_hi = sum((last >= s).astype(jnp.int32) for s in starts[1:])
        x = hs_scr[rows, :]
        gates = rs_scr[rows, :]
        rowid = r0 + lax.broadcasted_iota(jnp.int32, (MOE_CHUNK, 1), 0)
        os_scr[rows, :] = jnp.zeros((MOE_CHUNK, D_MODEL), F32)

        def group_body(g, carry):
            lo = jnp.where(g == 0, starts[0], jnp.where(g == 1, starts[1], jnp.where(g == 2, starts[2], starts[3])))
            hi = jnp.where(g == 0, ends[0], jnp.where(g == 1, ends[1], jnp.where(g == 2, ends[2], ends[3])))
            a = jnp.dot(x, w1_ref[g], preferred_element_type=F32)
            b = jnp.dot(x, w3_ref[g], preferred_element_type=F32)
            hid = []
            for e in range(MOE_PER_GROUP):
                sl = slice(e * MOE_HIDDEN, (e + 1) * MOE_HIDDEN)
                hid.append((jax.nn.silu(a[:, sl]) * b[:, sl] * gates[:, e:e + 1]).astype(BF16))
            y = jnp.dot(jnp.concatenate(hid, axis=1), w2_ref[g], preferred_element_type=F32)
            member = (rowid >= lo) & (rowid < hi)
            os_scr[rows, :] = jnp.where(member, y, os_scr[rows, :])
            return carry

        lax.fori_loop(g_lo, g_hi + 1, group_body, 0)

    o_hi, o_lo = _split(os_scr[...])
    moe = (lax.dot_general(perm, o_hi, (((0,), (0,)), ((), ())), preferred_element_type=F32)
           + lax.dot_general(perm, o_lo, (((0,), (0,)), ((), ())), preferred_element_type=F32))
    out = xm_ref[...] + mod_ref[5:6, :] * moe
    if final:
        out = _rms_rows(out) * fg_ref[...]
    o_ref[...] = out


def _moe_weight_kernel(w1_ref, w3_ref, w2_ref, o1_ref, o3_ref, o2_ref):
    for e in range(MOE_PER_GROUP):
        sl = slice(e * MOE_HIDDEN, (e + 1) * MOE_HIDDEN)
        o1_ref[:, sl] = w1_ref[e].astype(BF16)
        o3_ref[:, sl] = w3_ref[e].astype(BF16)
        o2_ref[sl, :] = w2_ref[e].astype(BF16)


def _moe_weights(w1, w3, w2):
    up = pl.BlockSpec((None, MOE_PER_GROUP, D_MODEL, MOE_HIDDEN), lambda l, g: (l, g, 0, 0))
    down = pl.BlockSpec((None, MOE_PER_GROUP, MOE_HIDDEN, D_MODEL), lambda l, g: (l, g, 0, 0))
    out = pl.BlockSpec((None, None, D_MODEL, GROUP_HID), lambda l, g: (l, g, 0, 0))
    shape = jax.ShapeDtypeStruct((DEPTH, MOE_GROUPS, D_MODEL, GROUP_HID), BF16)
    return pl.pallas_call(
        _moe_weight_kernel,
        grid=(DEPTH, MOE_GROUPS),
        in_specs=[up, up, down],
        out_specs=[out, out, out],
        out_shape=[shape, shape, shape],
        compiler_params=_cparams("parallel", "parallel"),
        name="moe_weights",
    )(w1, w3, w2)


def _moe(h2, route, tile_counts, xm, mods, mod_row, mod_tokens, w1g, w3g, w2g, fg, layer, *, final, tm=512):
    n = h2.shape[0]
    cnt = tile_counts[:, 0, ROUTE_GROUP:ROUTE_GROUP + MOE_GROUPS].reshape(
        n // tm, tm // S5_SEG, MOE_GROUPS).sum(axis=1)
    row = lambda w: pl.BlockSpec((tm, w), lambda i, c: (i, 0))
    mod_tiles = mod_tokens // tm
    wspec = pl.BlockSpec((None, MOE_GROUPS, D_MODEL, GROUP_HID), lambda i, c: (layer, 0, 0, 0),
                         pipeline_mode=pl.Buffered(1))
    return pl.pallas_call(
        functools.partial(_moe_kernel, final=final, tm=tm),
        grid_spec=pltpu.PrefetchScalarGridSpec(
            num_scalar_prefetch=1,
            grid=(n // tm,),
            in_specs=[row(D_MODEL), row(LANES), row(D_MODEL),
                      pl.BlockSpec((None, None, 6, D_MODEL), lambda i, c: (layer, mod_row + i // mod_tiles, 0, 0)),
                      wspec, wspec, wspec,
                      pl.BlockSpec((1, D_MODEL), lambda i, c: (0, 0))],
            out_specs=row(D_MODEL),
            scratch_shapes=[pltpu.VMEM((tm, D_MODEL), BF16), pltpu.VMEM((tm, LANES), F32),
                            pltpu.VMEM((tm, D_MODEL), F32)]),
        out_shape=jax.ShapeDtypeStruct((n, D_MODEL), F32),
        compiler_params=_cparams("arbitrary"),
        name="moe",
    )(cnt, h2, route, xm, mods, w1g, w3g, w2g, fg)


def kernel(x_prompt, x_sample, cache_a_k, cache_a_v, cache_b_k, cache_b_v, state_ret, state_ssm, c, c_ctx, mod_w, mod_b, norm1_g, norm2_g, w_in, a_qn_g, a_kn_g, b_rel_bias, ret_decay, ret_gn_g, s5_lam_re, s5_lam_im, s5_log_dt, s5_b_re, s5_b_im, s5_c_re, s5_c_im, s5_d, s5_glu_w, w_out, moe_gw, moe_gb, moe_ew, moe_eb, moe_w1, moe_w3, moe_w2, final_norm_g):
    n_ctx = BATCH * SEQ
    n_lat = DEC_BATCH * DEC_SEQ
    lat_seg = DEC_SEQ // S5_SEG

    cond = jnp.zeros((SUBLANES, D_MODEL), F32).at[0].set(c_ctx).at[1:1 + DEC_BATCH].set(c)
    mods = _modulation(cond, mod_w, mod_b).reshape(DEPTH, SUBLANES, 6, D_MODEL)

    rope_tabs = _rope_tables()
    s5_a, s5_bm, s5_cre, s5_cim = _s5_prepare(s5_lam_re, s5_lam_im, s5_log_dt, s5_b_re, s5_b_im,
                                              s5_c_re, s5_c_im)
    cak = cache_a_k.reshape(DEC_BATCH, DEPTH, PAST_LEN, A_KV_HEADS * HEAD_DIM)
    cav = cache_a_v.reshape(DEC_BATCH, DEPTH, PAST_LEN, A_KV_HEADS * HEAD_DIM)
    cbk = cache_b_k.reshape(DEC_BATCH, DEPTH, PAST_LEN, B_HEADS * HEAD_DIM)
    cbv = cache_b_v.reshape(DEC_BATCH, DEPTH, PAST_LEN, B_HEADS * HEAD_DIM)

    xc = x_prompt.reshape(n_ctx, D_MODEL)
    xs = x_sample.reshape(n_lat, D_MODEL)
    w1_all, w3_all, w2_all = _moe_weights(moe_w1, moe_w3, moe_w2)
    eye_h = jnp.eye(C_HEADS, dtype=F32)
    s0_bd = (state_ret[:, :, :, :, :, None, :] * eye_h[None, None, None, :, None, :, None]).reshape(
        DEC_BATCH, DEPTH, 2, C_HEADS * HEAD_DIM, C_HEADS * HEAD_DIM)
    caches = ret_states = ssm_states = None
    h0_zero = jnp.zeros((2, SUBLANES, 2 * S5_SP), F32)
    w_in_bf = w_in.astype(BF16)
    wo_bf = w_out.astype(BF16)
    glu_bf = s5_glu_w.astype(BF16)
    for l in range(DEPTH):
        final = l == DEPTH - 1
        g1 = norm1_g[l].reshape(1, D_MODEL)
        g2 = norm2_g[l].reshape(1, D_MODEL)
        fg = final_norm_g.reshape(1, D_MODEL)
        qn = jnp.tile(a_qn_g[l], A_HEADS).reshape(1, 256)
        kn = jnp.tile(a_kn_g[l], A_KV_HEADS).reshape(1, 128)
        dec = jnp.broadcast_to(ret_decay[l].reshape(2 * C_HEADS, 1), (2 * C_HEADS, LANES))
        gn = ret_gn_g[l].reshape(1, 256)
        dvec = s5_d[l].reshape(1, GROUP_WIDTH)
        wr = jnp.zeros((D_MODEL, LANES), F32).at[:, :MOE_GROUPS].set(moe_gw[l]).at[
            :, ROUTER_OFF:ROUTER_OFF + MOE_EXPERTS].set(moe_ew[l])
        br = jnp.zeros((1, LANES), F32).at[0, :MOE_GROUPS].set(moe_gb[l]).at[
            0, ROUTER_OFF:ROUTER_OFF + MOE_EXPERTS].set(moe_eb[l])
        na_bias = _na_bias(b_rel_bias[l])

        zc, cg, du_tm, *caches = _project(xc, mods, 0, n_ctx, g1, w_in_bf, qn, kn, None, l, seq_len=SEQ,
                                          with_cache=True, prev_caches=caches)
        oa, ob = _ctx_attention(zc, BATCH, SEQ)
        oc, ret_states = _retention(zc, cg, dec, gn, None, l, nb=BATCH, seq_len=SEQ, prev_state=ret_states)
        od_tm, ssm_states = _s5(du_tm, h0_zero, s5_a, s5_bm, s5_cre, s5_cim, dvec, glu_bf, l,
                                nseg=1, fin_layer=l, fin_layers=DEPTH, prev_fin=ssm_states)
        xm, h2, route, counts = _output_stage(xc, (oa, ob, oc, od_tm), mods, 0, n_ctx, wo_bf, g2, wr, br, l)
        xc = _moe(h2, route, counts, xm, mods, 0, n_ctx, w1_all, w3_all, w2_all, fg, l, final=final)

        zs, cg, du_tm = _project(xs, mods, 1, DEC_SEQ, g1, w_in_bf, qn, kn, rope_tabs, l, seq_len=DEC_SEQ)
        oa = _lat_attention_a(zs, cak, cav, l)
        ob = _lat_attention_b(zs, cbk, cbv, na_bias, l)
        oc = _retention(zs, cg, dec, gn, s0_bd, l, nb=DEC_BATCH, seq_len=DEC_SEQ)
        h0 = state_ssm[:, l].reshape(DEC_BATCH, 2, 2 * S5_SP).transpose(1, 0, 2)
        h0_seg = jnp.zeros((2, DEC_BATCH, lat_seg, 2 * S5_SP), F32)
        h0_seg = h0_seg.at[0, :, 0].set(h0[0]).at[1, :, lat_seg - 1].set(h0[1])
        od_tm, _ = _s5(du_tm, h0_seg.reshape(2, SUBLANES, 2 * S5_SP),
                       s5_a, s5_bm, s5_cre, s5_cim, dvec, glu_bf, l, nseg=lat_seg)
        xm, h2, route, counts = _output_stage(xs, (oa, ob, oc, od_tm), mods, 1, DEC_SEQ, wo_bf, g2, wr, br, l)
        xs = _moe(h2, route, counts, xm, mods, 1, DEC_SEQ, w1_all, w3_all, w2_all, fg, l, final=final)

    new_ak, new_av, new_bk, new_bv = caches
    return (xc.reshape(BATCH, SEQ, D_MODEL), xs.reshape(DEC_BATCH, DEC_SEQ, D_MODEL),
            new_ak.reshape(BATCH, DEPTH, SEQ, A_KV_HEADS, HEAD_DIM),
            new_av.reshape(BATCH, DEPTH, SEQ, A_KV_HEADS, HEAD_DIM),
            new_bk.reshape(BATCH, DEPTH, SEQ, B_HEADS, HEAD_DIM),
            new_bv.reshape(BATCH, DEPTH, SEQ, B_HEADS, HEAD_DIM),
            ret_states,
            ssm_states.reshape(BATCH, DEPTH, 2, 2, S5_GROUPS, S5_STATE))
```

```python
import functools
import math

import numpy as np
import jax
import jax.numpy as jnp
from jax import lax
from jax.experimental import pallas as pl
from jax.experimental.pallas import tpu as pltpu

F32 = jnp.float32
BF16 = jnp.bfloat16

D_MODEL = 1024
BATCH = 32
SEQ = 256
DEPTH = 2
DEC_BATCH = 2
DEC_SEQ = 1024
PAST_LEN = 256
GRID_W = 64
HEAD_DIM = 64
GROUP_WIDTH = 256
A_HEADS = 4
A_KV_HEADS = 2
B_HEADS = 4
NA_ROWS = 8
NA_COLS = 16
C_HEADS = 4
S5_CH = 16
S5_GROUPS = 16
S5_STATE = 64
MOE_GROUPS = 4
MOE_PER_GROUP = 8
MOE_EXPERTS = 32
MOE_HIDDEN = 128
ROPE_THETA = 10000.0
EPS = 1e-6
IN_WIDTH = 2560
Q_SCALE = HEAD_DIM ** -0.5

OFF_AQ, OFF_AK, OFF_AV = 0, 256, 384
OFF_BQ, OFF_BK, OFF_BV = 512, 768, 1024
OFF_CQ, OFF_CK, OFF_CV, OFF_CG = 1280, 1536, 1792, 2048
OFF_DU = 2304

LANES = 128
SUBLANES = 8
S5_SP = S5_GROUPS * S5_STATE
S5_SEG = 256
ROUTER_OFF = 4
NEG_BIG = -1e30
VMEM_LIMIT = 56 * 1024 * 1024


def _cparams(*sem):
    return pltpu.CompilerParams(dimension_semantics=sem, vmem_limit_bytes=VMEM_LIMIT)


def _mod_spec(layer, first_row, tiles_per_row, grid_rank):
    if grid_rank == 1:
        return pl.BlockSpec((None, None, 6, D_MODEL), lambda i: (layer, first_row + i // tiles_per_row, 0, 0))
    return pl.BlockSpec((None, None, 6, D_MODEL), lambda i, g: (layer, first_row + i // tiles_per_row, 0, 0))


def _bdot(a, b):
    return jnp.dot(a.astype(BF16), b.astype(BF16), preferred_element_type=F32)


def _bdot_nt(a, b):
    return lax.dot_general(a.astype(BF16), b.astype(BF16), (((1,), (1,)), ((), ())),
                           preferred_element_type=F32)


def _bdot_tn(a, b):
    return lax.dot_general(a.astype(BF16), b.astype(BF16), (((0,), (0,)), ((), ())),
                           preferred_element_type=F32)


def _split(a):
    hi = a.astype(BF16)
    lo = (a - hi.astype(F32)).astype(BF16)
    return hi, lo


def _dot_hilo_lhs(a, b_bf16):
    hi, lo = _split(a)
    return (jnp.dot(hi, b_bf16, preferred_element_type=F32)
            + jnp.dot(lo, b_bf16, preferred_element_type=F32))


def _rms_rows(x):
    return x * lax.rsqrt(jnp.mean(x * x, axis=-1, keepdims=True) + EPS)


def _mod_kernel(cond_ref, w_ref, b_ref, o_ref):
    o_ref[...] = _bdot(jax.nn.silu(cond_ref[...]), w_ref[...]) + b_ref[...]


def _modulation(cond, mod_w, mod_b):
    tn = 1536
    return pl.pallas_call(
        _mod_kernel,
        grid=(DEPTH, 6 * D_MODEL // tn),
        in_specs=[pl.BlockSpec((SUBLANES, D_MODEL), lambda l, j: (0, 0)),
                  pl.BlockSpec((None, D_MODEL, tn), lambda l, j: (l, 0, j)),
                  pl.BlockSpec((None, 1, tn), lambda l, j: (l, 0, j))],
        out_specs=pl.BlockSpec((None, SUBLANES, tn), lambda l, j: (l, 0, j)),
        out_shape=jax.ShapeDtypeStruct((DEPTH, SUBLANES, 6 * D_MODEL), F32),
        compiler_params=_cparams("arbitrary", "arbitrary"),
        name="modulation",
    )(cond, mod_w, mod_b.reshape(DEPTH, 1, 6 * D_MODEL))


def _group_mean_matrix(w):
    ri = lax.broadcasted_iota(jnp.int32, (w, w), 0) // HEAD_DIM
    ci = lax.broadcasted_iota(jnp.int32, (w, w), 1) // HEAD_DIM
    return jnp.where(ri == ci, 1.0 / HEAD_DIM, 0.0).astype(BF16)


def _head_norm(t, g):
    ms = _dot_hilo_lhs(t * t, _group_mean_matrix(t.shape[1]))
    return t * lax.rsqrt(ms + EPS) * g


def _rope(t, cos, sa, sb):
    return (t * cos + pltpu.roll(t, LANES - 16, 1) * sa + pltpu.roll(t, 16, 1) * sb)


def _store_layer_slot(ref, slot, value):
    for s in range(ref.shape[0]):
        ref[s] = value if s == slot else jnp.zeros_like(value)


def _layer_slot_block(layer, first_call, tail):
    if first_call:
        return (None, DEPTH) + tail, (0,) * (1 + len(tail)), layer
    return (None, 1) + tail, (layer,) + (0,) * len(tail), 0


def _proj_kernel(*refs, rope, n_alias, with_cache, slot):
    x_ref, mod_ref, g1_ref, w_ref, qn_ref, kn_ref = refs[:6]
    n_in = 6
    if rope:
        cos_ref, sa_ref, sb_ref = refs[6:9]
        n_in = 9
    outs = refs[n_in + n_alias:]
    z_ref, cg_ref, du_ref = outs[:3]
    h = _rms_rows(x_ref[...]) * g1_ref[...] * (1.0 + mod_ref[1:2, :]) + mod_ref[0:1, :]
    z = jnp.dot(h.astype(BF16), w_ref[...], preferred_element_type=F32)
    aq = _head_norm(z[:, OFF_AQ:OFF_AK], qn_ref[...])
    ak = _head_norm(z[:, OFF_AK:OFF_AV], kn_ref[...])
    for j in range(3):
        t = aq[:, j * LANES:(j + 1) * LANES] if j < 2 else ak
        if rope:
            cj = 0 if j == 2 else j
            sl = slice(cj * LANES, (cj + 1) * LANES)
            t = _rope(t, cos_ref[:, sl], sa_ref[:, sl], sb_ref[:, sl])
        if j == 2:
            ak = t
        z_ref[:, j * LANES:(j + 1) * LANES] = t.astype(BF16)
    z_ref[:, OFF_AV:OFF_CK] = z[:, OFF_AV:OFF_CK].astype(BF16)
    z_ref[:, OFF_CK:OFF_CV] = (z[:, OFF_CK:OFF_CV] * Q_SCALE).astype(BF16)
    z_ref[:, OFF_CV:OFF_CG] = z[:, OFF_CV:OFF_CG].astype(BF16)
    cg_ref[...] = z[:, OFF_CG:OFF_DU]
    du_ref[...] = z[:, OFF_DU:]
    if with_cache:
        ak_ref, av_ref, bk_ref, bv_ref = outs[3:7]
        _store_layer_slot(ak_ref, slot, ak)
        _store_layer_slot(av_ref, slot, z[:, OFF_AV:OFF_BQ])
        _store_layer_slot(bk_ref, slot, z[:, OFF_BK:OFF_BV])
        _store_layer_slot(bv_ref, slot, z[:, OFF_BV:OFF_CQ])


def _du_spec(grid_rank):
    if grid_rank == 1:
        return pl.BlockSpec((None, S5_SEG, GROUP_WIDTH), lambda i: (i // SUBLANES, 0, i % SUBLANES))
    return pl.BlockSpec((None, S5_SEG, GROUP_WIDTH), lambda i, g: (i // SUBLANES, 0, i % SUBLANES))


def _project(x, mods, mod_row, mod_tokens, g1, w_in_bf, qn, kn, rope_tabs, layer, *, seq_len,
             with_cache=False, prev_caches=None):
    tm = S5_SEG
    n = x.shape[0]
    rope = rope_tabs is not None
    in_specs = [pl.BlockSpec((tm, D_MODEL), lambda i: (i, 0)),
                _mod_spec(layer, mod_row, mod_tokens // tm, 1),
                pl.BlockSpec((1, D_MODEL), lambda i: (0, 0)),
                pl.BlockSpec((None, D_MODEL, IN_WIDTH), lambda i: (layer, 0, 0)),
                pl.BlockSpec((1, 256), lambda i: (0, 0)),
                pl.BlockSpec((1, 128), lambda i: (0, 0))]
    args = [x, mods, g1, w_in_bf, qn, kn]
    if rope:
        tps = seq_len // tm
        in_specs += [pl.BlockSpec((tm, 256), lambda i: (i % tps, 0))] * 3
        args += list(rope_tabs)
    out_specs = [pl.BlockSpec((tm, OFF_CG), lambda i: (i, 0)),
                 pl.BlockSpec((tm, GROUP_WIDTH), lambda i: (i, 0)), _du_spec(1)]
    out_shape = [jax.ShapeDtypeStruct((n, OFF_CG), BF16),
                 jax.ShapeDtypeStruct((n, GROUP_WIDTH), F32),
                 jax.ShapeDtypeStruct((n // (tm * SUBLANES), S5_SEG, SUBLANES * GROUP_WIDTH), F32)]
    aliases = {}
    n_alias = 0
    slot = 0
    if with_cache:
        assert tm == seq_len
        nb = n // seq_len
        for w in (128, 128, 256, 256):
            blk, idx, slot = _layer_slot_block(layer, prev_caches is None, (seq_len, w))
            out_specs.append(pl.BlockSpec(blk, lambda i, idx=idx: (i,) + idx))
            out_shape.append(jax.ShapeDtypeStruct((nb, DEPTH, seq_len, w), F32))
        if prev_caches is not None:
            n_alias = len(prev_caches)
            for k, arr in enumerate(prev_caches):
                aliases[len(args)] = 3 + k
                in_specs.append(pl.BlockSpec(memory_space=pl.ANY))
                args.append(arr)
    return pl.pallas_call(
        functools.partial(_proj_kernel, rope=rope, n_alias=n_alias, with_cache=with_cache, slot=slot),
        grid=(n // tm,),
        in_specs=in_specs,
        out_specs=out_specs,
        out_shape=out_shape,
        input_output_aliases=aliases,
        compiler_params=_cparams("parallel"),
        name="project",
    )(*args)


def _rope_tables():
    t = jnp.arange(DEC_SEQ)
    row = (t // GRID_W).astype(F32)
    col = (t % GRID_W).astype(F32)
    nf = HEAD_DIM // 4
    inv = ROPE_THETA ** (-jnp.arange(nf, dtype=F32) / nf)
    ang_r = row[:, None] * inv[None, :]
    ang_c = col[:, None] * inv[None, :]
    zeros = jnp.zeros_like(ang_r)
    cos = jnp.concatenate([jnp.cos(ang_r), jnp.cos(ang_r), jnp.cos(ang_c), jnp.cos(ang_c)], axis=-1)
    sa = jnp.concatenate([-jnp.sin(ang_r), zeros, -jnp.sin(ang_c), zeros], axis=-1)
    sb = jnp.concatenate([zeros, jnp.sin(ang_r), zeros, jnp.sin(ang_c)], axis=-1)
    return tuple(jnp.tile(a, (1, 4)) for a in (cos, sa, sb))


N_HEADS = 4


def _lane_head(width):
    return lax.broadcasted_iota(jnp.int32, (1, width), 1) // HEAD_DIM


def _stack_heads(q):
    head = _lane_head(q.shape[1])
    return jnp.concatenate([jnp.where(head == h, q, 0.0) for h in range(N_HEADS)], axis=0).astype(BF16)


def _stack_heads_gqa(q):
    lo = lax.broadcasted_iota(jnp.int32, (1, LANES), 1) < HEAD_DIM
    q = q.astype(F32)
    q01, q23 = q[:, :LANES], q[:, LANES:]
    blocks = [jnp.where(lo, q01, 0.0), jnp.where(lo, pltpu.roll(q01, HEAD_DIM, 1), 0.0),
              jnp.where(lo, 0.0, pltpu.roll(q23, HEAD_DIM, 1)), jnp.where(lo, 0.0, q23)]
    return jnp.concatenate(blocks, axis=0).astype(BF16)


def _spread_kv_gqa(v):
    lo = lax.broadcasted_iota(jnp.int32, (1, LANES), 1) < HEAD_DIM
    v = v.astype(F32)
    vr = pltpu.roll(v, HEAD_DIM, 1)
    return jnp.concatenate([jnp.where(lo, v, vr), jnp.where(lo, vr, v)], axis=1)


def _mha(qs, blocks, tq):
    scores = []
    for k, _, bias in blocks:
        s = _bdot_nt(qs, k)
        scores.append(s if bias is None else s + bias)
    m = functools.reduce(jnp.maximum, [jnp.max(s, axis=-1, keepdims=True) for s in scores])
    es = [jnp.exp(s - m) for s in scores]
    denom = functools.reduce(jnp.add, [jnp.sum(e, axis=-1, keepdims=True) for e in es])
    ps = [e.astype(BF16) for e in es]
    head = _lane_head(N_HEADS * HEAD_DIM)
    vals = [v.astype(BF16) for _, v, _ in blocks]
    o = None
    dall = None
    for h in range(N_HEADS):
        rows = slice(h * tq, (h + 1) * tq)
        for p, v in zip(ps, vals):
            t = jnp.dot(p[rows], jnp.where(head == h, v, jnp.zeros_like(v)), preferred_element_type=F32)
            o = t if o is None else o + t
        d = jnp.where(head == h, denom[rows], 0.0)
        dall = d if dall is None else dall + d
    return (o / dall).astype(BF16)


def _ctx_attn_kernel(aq_ref, ak_ref, av_ref, bq_ref, bk_ref, bv_ref, oa_ref, ob_ref):
    tq = aq_ref.shape[0]
    oa_ref[...] = _mha(_stack_heads_gqa(aq_ref[...] * Q_SCALE),
                       [(ak_ref[...], _spread_kv_gqa(av_ref[...]), None)], tq)
    ob_ref[...] = _mha(_stack_heads(bq_ref[...] * Q_SCALE), [(bk_ref[...], bv_ref[...], None)], tq)


def _ctx_attention(z, nb, seq_len):
    def col(width, off):
        return pl.BlockSpec((seq_len, width), lambda b: (b, off // width))
    return pl.pallas_call(
        _ctx_attn_kernel,
        grid=(nb,),
        in_specs=[col(256, OFF_AQ), col(128, OFF_AK), col(128, OFF_AV),
                  col(256, OFF_BQ), col(256, OFF_BK), col(256, OFF_BV)],
        out_specs=[pl.BlockSpec((seq_len, 256), lambda b: (b, 0))] * 2,
        out_shape=[jax.ShapeDtypeStruct((nb * seq_len, 256), BF16)] * 2,
        compiler_params=_cparams("parallel"),
        name="ctx_attention",
    )(z, z, z, z, z, z)


def _lat_attn_a_kernel(q_ref, kn_ref, vn_ref, kc_ref, vc_ref, o_ref):
    o_ref[...] = _mha(_stack_heads_gqa(q_ref[...] * Q_SCALE),
                      [(kc_ref[...], _spread_kv_gqa(vc_ref[...]), None),
                       (kn_ref[...], _spread_kv_gqa(vn_ref[...]), None)], q_ref.shape[0])


def _lat_attention_a(z, cache_k, cache_v, layer, tq=256):
    nq = DEC_SEQ // tq
    cache_spec = pl.BlockSpec((None, None, PAST_LEN, 128), lambda b, j: (b, layer, 0, 0))
    return pl.pallas_call(
        _lat_attn_a_kernel,
        grid=(DEC_BATCH, nq),
        in_specs=[pl.BlockSpec((tq, 256), lambda b, j: (b * nq + j, OFF_AQ // 256)),
                  pl.BlockSpec((DEC_SEQ, 128), lambda b, j: (b, OFF_AK // 128)),
                  pl.BlockSpec((DEC_SEQ, 128), lambda b, j: (b, OFF_AV // 128)),
                  cache_spec, cache_spec],
        out_specs=pl.BlockSpec((tq, 256), lambda b, j: (b * nq + j, 0)),
        out_shape=jax.ShapeDtypeStruct((DEC_BATCH * DEC_SEQ, 256), BF16),
        compiler_params=_cparams("parallel", "parallel"),
        name="lat_attention_a",
    )(z, z, z, cache_k, cache_v)


NA_KEYS = NA_ROWS * GRID_W


NA_PAIRS = 2 * NA_ROWS - 2


def _na_kernel(q_ref, k_ref, v_ref, kc_ref, vc_ref, bias_ref, o_ref):
    r = pl.program_id(1)
    rows = DEC_SEQ // GRID_W
    row_start = jnp.clip(r - NA_ROWS // 2, 0, rows - NA_ROWS)
    start = pl.multiple_of(row_start * GRID_W, GRID_W)
    rel0 = row_start - r + NA_ROWS - 1
    kl = k_ref[pl.ds(start, NA_KEYS), :]
    vl = v_ref[pl.ds(start, NA_KEYS), :]
    bias = jnp.concatenate(
        [jnp.concatenate([bias_ref[h, rel0 + 2 * jp] for jp in range(NA_ROWS // 2)], axis=1)
         for h in range(B_HEADS)], axis=0)
    o_ref[...] = _mha(_stack_heads(q_ref[...] * Q_SCALE),
                      [(kl, vl, bias), (kc_ref[...], vc_ref[...], None)], GRID_W)


def _na_bias(rel_bias):
    nrel = 2 * NA_COLS - 1
    period = 2 * GRID_W
    b = rel_bias.astype(F32)
    ext = jnp.concatenate([b[..., NA_COLS - 1:],
                           jnp.zeros(b.shape[:-1] + (period - nrel,), F32),
                           b[..., :NA_COLS - 1]], axis=-1)
    flat = jnp.tile(ext, (1, 1, GRID_W))[..., :GRID_W * (period - 1)]
    toe = flat.reshape(b.shape[:-1] + (GRID_W, period - 1))[..., :GRID_W]
    col_start = np.clip(np.arange(GRID_W) - NA_COLS // 2, 0, GRID_W - NA_COLS)
    kc = np.arange(GRID_W)
    inside = (kc[None, :] >= col_start[:, None]) & (kc[None, :] < col_start[:, None] + NA_COLS)
    toe = jnp.where(jnp.asarray(inside), toe, NEG_BIG)
    return jnp.concatenate([toe[:, :-1], toe[:, 1:]], axis=-1)


def _lat_attention_b(z, cache_k, cache_v, bias, layer):
    rows = DEC_SEQ // GRID_W
    cache_spec = pl.BlockSpec((None, None, PAST_LEN, 256), lambda b, r: (b, layer, 0, 0))
    return pl.pallas_call(
        _na_kernel,
        grid=(DEC_BATCH, rows),
        in_specs=[pl.BlockSpec((GRID_W, 256), lambda b, r: (b * rows + r, OFF_BQ // 256)),
                  pl.BlockSpec((DEC_SEQ, 256), lambda b, r: (b, OFF_BK // 256)),
                  pl.BlockSpec((DEC_SEQ, 256), lambda b, r: (b, OFF_BV // 256)),
                  cache_spec, cache_spec,
                  pl.BlockSpec((B_HEADS, NA_PAIRS, GRID_W, 2 * GRID_W), lambda b, r: (0, 0, 0, 0))],
        out_specs=pl.BlockSpec((GRID_W, 256), lambda b, r: (b * rows + r, 0)),
        out_shape=jax.ShapeDtypeStruct((DEC_BATCH * DEC_SEQ, 256), BF16),
        compiler_params=_cparams("parallel", "parallel"),
        name="lat_attention_b",
    )(z, z, z, cache_k, cache_v, bias)


def _retention_kernel(q_ref, g_ref, k_ref, v_ref, dec_ref, gn_ref, *rest, seq_len, tq, has_state,
                      hoist_decay, slot):
    if has_state:
        s0_ref, o_ref, dec_scr = rest
    else:
        o_ref, st_ref, dec_scr = rest[-3:]
    head = _lane_head(C_HEADS * HEAD_DIM)
    lg = jax.nn.log_sigmoid(dec_ref[...])

    def per_lane(row0):
        out = jnp.zeros((1, C_HEADS * HEAD_DIM), F32)
        for h in range(C_HEADS):
            out = jnp.where(head == h, lg[row0 + h:row0 + h + 1, 0:1], out)
        return out

    lgf_l, lgb_l = per_lane(0), per_lane(C_HEADS)
    i0 = pl.program_id(1) * tq
    qi = (i0 + lax.broadcasted_iota(jnp.int32, (tq, 1), 0)).astype(F32)

    def fill_decay():
        kj = lax.broadcasted_iota(jnp.int32, (1, seq_len), 1).astype(F32)
        diff = qi - kj
        for h in range(C_HEADS):
            lgf = lg[h:h + 1, 0:1]
            lgb = lg[C_HEADS + h:C_HEADS + h + 1, 0:1]
            dec_scr[h * tq:(h + 1) * tq, :] = (
                jnp.where(diff >= 0, jnp.exp(lgf * jnp.maximum(diff, 0.0)), 0.0)
                + jnp.where(diff <= 0, jnp.exp(lgb * jnp.maximum(-diff, 0.0)), 0.0))

    if hoist_decay:
        pl.when(pl.program_id(0) == 0)(fill_decay)
    else:
        fill_decay()

    q = q_ref[...]
    k = k_ref[...]
    v = v_ref[...].astype(BF16)
    sc = (_bdot_nt(_stack_heads(q), k) * dec_scr[...]).astype(BF16)
    o = None
    for h in range(C_HEADS):
        t = jnp.dot(sc[h * tq:(h + 1) * tq], jnp.where(head == h, v, jnp.zeros_like(v)),
                    preferred_element_type=F32)
        o = t if o is None else o + t
    if has_state:
        o = (o + _bdot(q, s0_ref[0]) * jnp.exp(lgf_l * (qi + 1.0))
             + _bdot(q, s0_ref[1]) * jnp.exp(lgb_l * (seq_len - qi)))
    gm = _group_mean_matrix(C_HEADS * HEAD_DIM)
    dlt = o - _dot_hilo_lhs(o, gm)
    var = _dot_hilo_lhs(dlt * dlt, gm)
    o_ref[...] = (dlt * lax.rsqrt(var + EPS) * gn_ref[...] * jax.nn.silu(g_ref[...])).astype(BF16)
    if not has_state:
        kpos = lax.broadcasted_iota(jnp.int32, (seq_len, 1), 0).astype(F32)
        sf = _bdot_tn(k * jnp.exp(lgf_l * (seq_len - 1.0 - kpos)), v)
        sb = _bdot_tn(k * jnp.exp(lgb_l * kpos), v)
        for s in range(st_ref.shape[0]):
            for h in range(C_HEADS):
                sl = slice(h * HEAD_DIM, (h + 1) * HEAD_DIM)
                st_ref[s, 0, h] = sf[sl, sl] if s == slot else jnp.zeros((HEAD_DIM, HEAD_DIM), F32)
                st_ref[s, 1, h] = sb[sl, sl] if s == slot else jnp.zeros((HEAD_DIM, HEAD_DIM), F32)


def _retention(z, cg, dec, gn, s0, layer, *, nb, seq_len, prev_state=None, tq=256):
    nq = seq_len // tq
    has_state = s0 is not None
    aliases = {}
    slot = 0
    in_specs = [pl.BlockSpec((tq, 256), lambda b, j: (b * nq + j, OFF_CQ // 256)),
                pl.BlockSpec((tq, 256), lambda b, j: (b * nq + j, 0)),
                pl.BlockSpec((seq_len, 256), lambda b, j: (b, OFF_CK // 256)),
                pl.BlockSpec((seq_len, 256), lambda b, j: (b, OFF_CV // 256)),
                pl.BlockSpec((SUBLANES, LANES), lambda b, j: (0, 0)),
                pl.BlockSpec((1, 256), lambda b, j: (0, 0))]
    args = [z, cg, z, z, dec, gn]
    o_spec = pl.BlockSpec((tq, 256), lambda b, j: (b * nq + j, 0))
    o_shape = jax.ShapeDtypeStruct((nb * seq_len, 256), BF16)
    if has_state:
        in_specs.append(pl.BlockSpec((None, None, 2, 256, 256), lambda b, j: (b, layer, 0, 0, 0)))
        args.append(s0)
        out_specs, out_shape = o_spec, o_shape
    else:
        assert nq == 1
        blk, idx, slot = _layer_slot_block(layer, prev_state is None, (2, C_HEADS, HEAD_DIM, HEAD_DIM))
        out_specs = [o_spec, pl.BlockSpec(blk, lambda b, j: (b,) + idx)]
        out_shape = [o_shape, jax.ShapeDtypeStruct((nb, DEPTH, 2, C_HEADS, HEAD_DIM, HEAD_DIM), F32)]
        if prev_state is not None:
            aliases[len(args)] = 1
            in_specs.append(pl.BlockSpec(memory_space=pl.ANY))
            args.append(prev_state)
    return pl.pallas_call(
        functools.partial(_retention_kernel, seq_len=seq_len, tq=tq, has_state=has_state,
                          hoist_decay=nq == 1, slot=slot),
        grid=(nb, nq),
        in_specs=in_specs,
        out_specs=out_specs,
        out_shape=out_shape,
        scratch_shapes=[pltpu.VMEM((C_HEADS * tq, seq_len), F32)],
        input_output_aliases=aliases,
        compiler_params=_cparams("arbitrary", "arbitrary"),
        name="retention",
    )(*args)


def _s5_prep_kernel(lre_ref, lim_ref, ldt_ref, bre_ref, bim_ref, cre_ref, cim_ref,
                    a_ref, bm_ref, cro_ref, cio_ref, bm_scr, cr_scr, ci_scr):
    lre = lre_ref[...]
    lim = lim_ref[...]
    dt = jnp.exp(ldt_ref[...])
    mag = jnp.exp(lre * dt)
    a_re = mag * jnp.cos(lim * dt)
    a_im = mag * jnp.sin(lim * dt)
    den = lre * lre + lim * lim
    r_re = ((a_re - 1.0) * lre + a_im * lim) / den
    r_im = (a_im * lre - (a_re - 1.0) * lim) / den
    bm_scr[...] = jnp.zeros_like(bm_scr)
    cr_scr[...] = jnp.zeros_like(cr_scr)
    ci_scr[...] = jnp.zeros_like(ci_scr)
    for g in range(S5_GROUPS):
        rows = slice(g * S5_CH, (g + 1) * S5_CH)
        cols = slice(g * S5_STATE, (g + 1) * S5_STATE)
        a_ref[0:1, cols] = a_re[g:g + 1, :]
        a_ref[1:2, cols] = a_im[g:g + 1, :]
        rr, ri = r_re[g:g + 1, :], r_im[g:g + 1, :]
        br, bi = bre_ref[g], bim_ref[g]
        bm_scr[rows, cols] = rr * br - ri * bi
        bm_scr[rows, S5_SP + g * S5_STATE:S5_SP + (g + 1) * S5_STATE] = rr * bi + ri * br
        cr_scr[cols, rows] = cre_ref[g]
        ci_scr[cols, rows] = cim_ref[g]
    bm_ref[...] = bm_scr[...].astype(BF16)
    cro_ref[...] = cr_scr[...].astype(BF16)
    cio_ref[...] = ci_scr[...].astype(BF16)


def _s5_prepare(lam_re, lam_im, log_dt, b_re, b_im, c_re, c_im):
    gp = (S5_GROUPS, S5_STATE)
    ldt = jnp.broadcast_to(log_dt[..., None], (DEPTH, 2) + gp)
    bt = [jnp.swapaxes(t, -1, -2) for t in (b_re, b_im)]
    ct = [jnp.swapaxes(t, -1, -2) for t in (c_re, c_im)]

    def spec(*tail):
        return pl.BlockSpec((None, None) + tail, lambda l, d: (l, d) + (0,) * len(tail))

    return pl.pallas_call(
        _s5_prep_kernel,
        grid=(DEPTH, 2),
        in_specs=[spec(*gp)] * 3 + [spec(S5_GROUPS, S5_CH, S5_STATE)] * 2 + [spec(S5_GROUPS, S5_STATE, S5_CH)] * 2,
        out_specs=[spec(2, S5_SP), spec(GROUP_WIDTH, 2 * S5_SP), spec(S5_SP, GROUP_WIDTH), spec(S5_SP, GROUP_WIDTH)],
        out_shape=[jax.ShapeDtypeStruct((DEPTH, 2, 2, S5_SP), F32),
                   jax.ShapeDtypeStruct((DEPTH, 2, GROUP_WIDTH, 2 * S5_SP), BF16),
                   jax.ShapeDtypeStruct((DEPTH, 2, S5_SP, GROUP_WIDTH), BF16),
                   jax.ShapeDtypeStruct((DEPTH, 2, S5_SP, GROUP_WIDTH), BF16)],
        scratch_shapes=[pltpu.VMEM((GROUP_WIDTH, 2 * S5_SP), F32), pltpu.VMEM((S5_SP, GROUP_WIDTH), F32),
                        pltpu.VMEM((S5_SP, GROUP_WIDTH), F32)],
        compiler_params=_cparams("parallel", "parallel"),
        name="s5_prepare",
    )(lam_re, lam_im, ldt, bt[0], bt[1], ct[0], ct[1])


def _cmul(ar, ai, br, bi):
    return ar * br - ai * bi, ar * bi + ai * br


def _s5_kernel(u_ref, h0_ref, a_ref, bm_ref, cre_ref, cim_ref, dvec_ref, glu_ref, *rest, nseg, slot):
    od_ref, fin_ref, x_scr, y_scr = rest[-4:]
    steps = S5_SEG
    rows = steps * SUBLANES
    chunk = 256
    nchunk = rows // chunk
    seg = lax.broadcasted_iota(jnp.int32, (SUBLANES, S5_SP), 0) % nseg

    for d in range(2):
        def xbody(c, carry):
            r0 = pl.multiple_of(c * chunk, chunk)
            x_scr[pl.ds(r0, chunk), :] = jnp.dot(u_ref[pl.ds(r0, chunk), :].astype(BF16), bm_ref[d],
                                                 preferred_element_type=F32)
            return carry
        lax.fori_loop(0, nchunk, xbody, 0)

        ar = jnp.broadcast_to(a_ref[d, 0:1, :], (SUBLANES, S5_SP))
        ai = jnp.broadcast_to(a_ref[d, 1:2, :], (SUBLANES, S5_SP))

        def scan(init, store):
            def body(t, carry):
                sr, si = carry
                tt = t if d == 0 else steps - 1 - t
                r0 = pl.multiple_of(tt * SUBLANES, SUBLANES)
                pr, pi = _cmul(ar, ai, sr, si)
                nr = pr + x_scr[pl.ds(r0, SUBLANES), 0:S5_SP]
                ni = pi + x_scr[pl.ds(r0, SUBLANES), S5_SP:]
                if store:
                    x_scr[pl.ds(r0, SUBLANES), 0:S5_SP] = nr
                    x_scr[pl.ds(r0, SUBLANES), S5_SP:] = ni
                return nr, ni
            return lax.fori_loop(0, steps, body, init, unroll=4)

        init = (h0_ref[d, :, 0:S5_SP], h0_ref[d, :, S5_SP:])
        if nseg > 1:
            zero = jnp.zeros((SUBLANES, S5_SP), F32)
            fr, fi = scan((zero, zero), store=False)
            pr, pi = ar, ai
            for _ in range(int(math.log2(steps))):
                pr, pi = _cmul(pr, pi, pr, pi)
            cr, ci = init
            shift = 1 if d == 0 else SUBLANES - 1
            order = range(1, nseg) if d == 0 else range(nseg - 2, -1, -1)
            for s in order:
                ncr, nci = pltpu.roll(cr, shift, 0), pltpu.roll(ci, shift, 0)
                nfr, nfi = pltpu.roll(fr, shift, 0), pltpu.roll(fi, shift, 0)
                qr, qi = _cmul(pr, pi, ncr, nci)
                cr = jnp.where(seg == s, qr + nfr, cr)
                ci = jnp.where(seg == s, qi + nfi, ci)
            init = (cr, ci)
        sr, si = scan(init, store=True)
        for s in range(fin_ref.shape[1] // (4 * S5_SP)):
            base = (4 * s + 2 * d) * S5_SP
            fin_ref[:, base:base + S5_SP] = sr if s == slot else jnp.zeros_like(sr)
            fin_ref[:, base + S5_SP:base + 2 * S5_SP] = si if s == slot else jnp.zeros_like(si)

        def ybody(c, carry):
            r0 = pl.multiple_of(c * chunk, chunk)
            y = (_bdot(x_scr[pl.ds(r0, chunk), 0:S5_SP], cre_ref[d])
                 - _bdot(x_scr[pl.ds(r0, chunk), S5_SP:], cim_ref[d]))
            if d == 0:
                y_scr[pl.ds(r0, chunk), :] = y
            else:
                y_scr[pl.ds(r0, chunk), :] += y
            return carry
        lax.fori_loop(0, nchunk, ybody, 0)

    def obody(c, carry):
        r0 = pl.multiple_of(c * chunk, chunk)
        y = y_scr[pl.ds(r0, chunk), :] + dvec_ref[...] * u_ref[pl.ds(r0, chunk), :]
        zz = jax.nn.gelu(y)
        od_ref[pl.ds(r0, chunk), :] = (zz * jax.nn.sigmoid(_bdot(zz, glu_ref[...]))).astype(BF16)
        return carry
    lax.fori_loop(0, nchunk, obody, 0)


def _s5(du_tm, h0, a, bmat, cre, cim, dvec, glu_bf, layer, *, nseg, fin_layer=0, fin_layers=1,
        prev_fin=None):
    nblk = du_tm.shape[0]
    rows = S5_SEG * SUBLANES
    fin_w = 4 * S5_SP
    in_specs = [pl.BlockSpec((None, rows, GROUP_WIDTH), lambda i: (i, 0, 0)),
                pl.BlockSpec((2, SUBLANES, 2 * S5_SP), lambda i: (0, 0, 0)),
                pl.BlockSpec((None, 2, 2, S5_SP), lambda i: (layer, 0, 0, 0)),
                pl.BlockSpec((None, 2, GROUP_WIDTH, 2 * S5_SP), lambda i: (layer, 0, 0, 0)),
                pl.BlockSpec((None, 2, S5_SP, GROUP_WIDTH), lambda i: (layer, 0, 0, 0)),
                pl.BlockSpec((None, 2, S5_SP, GROUP_WIDTH), lambda i: (layer, 0, 0, 0)),
                pl.BlockSpec((1, GROUP_WIDTH), lambda i: (0, 0)),
                pl.BlockSpec((None, GROUP_WIDTH, GROUP_WIDTH), lambda i: (layer, 0, 0))]
    args = [du_tm.reshape(nblk, rows, GROUP_WIDTH), h0, a, bmat, cre, cim, dvec, glu_bf]
    aliases = {}
    if prev_fin is not None:
        aliases[len(args)] = 1
        in_specs.append(pl.BlockSpec(memory_space=pl.ANY))
        args.append(prev_fin)
        fin_spec, slot = pl.BlockSpec((SUBLANES, fin_w), lambda i: (i, fin_layer)), 0
    else:
        fin_spec, slot = pl.BlockSpec((SUBLANES, fin_layers * fin_w), lambda i: (i, 0)), fin_layer
    od, fin = pl.pallas_call(
        functools.partial(_s5_kernel, nseg=nseg, slot=slot),
        grid=(nblk,),
        in_specs=in_specs,
        out_specs=[pl.BlockSpec((None, rows, GROUP_WIDTH), lambda i: (i, 0, 0)), fin_spec],
        out_shape=[jax.ShapeDtypeStruct((nblk, rows, GROUP_WIDTH), BF16),
                   jax.ShapeDtypeStruct((nblk * SUBLANES, fin_layers * fin_w), F32)],
        scratch_shapes=[pltpu.VMEM((rows, 2 * S5_SP), F32), pltpu.VMEM((rows, GROUP_WIDTH), F32)],
        input_output_aliases=aliases,
        compiler_params=_cparams("parallel"),
        name="s5",
    )(*args)
    return od.reshape(nblk, S5_SEG, SUBLANES * GROUP_WIDTH), fin


ROUTE_GROUP = MOE_PER_GROUP
OUT_SEQS = 2


def _out_kernel(x_ref, oa_ref, ob_ref, oc_ref, od_ref, mod_ref, wo_ref, g2_ref, wrh_ref, wrl_ref, br_ref,
                xm_ref, h2_ref, route_ref, cnt_ref):
    od = jnp.concatenate([od_ref[:, s * GROUP_WIDTH:(s + 1) * GROUP_WIDTH] for s in range(OUT_SEQS)], axis=0)
    mix = functools.reduce(jnp.add, [
        _bdot(o, wo_ref[i * GROUP_WIDTH:(i + 1) * GROUP_WIDTH, :])
        for i, o in enumerate((oa_ref[...], ob_ref[...], oc_ref[...], od))])
    xm = x_ref[...] + mod_ref[2:3, :] * mix
    xm_ref[...] = xm
    h2 = _rms_rows(xm) * g2_ref[...] * (1.0 + mod_ref[4:5, :]) + mod_ref[3:4, :]
    h2_ref[...] = h2.astype(BF16)

    h_hi, h_lo = _split(h2)
    logits = (jnp.dot(h_hi, wrh_ref[...], preferred_element_type=F32)
              + jnp.dot(h_hi, wrl_ref[...], preferred_element_type=F32)
              + jnp.dot(h_lo, wrh_ref[...], preferred_element_type=F32)) + br_ref[...]
    lane_i = lax.broadcasted_iota(jnp.int32, logits.shape, 1)
    lane = lane_i.astype(F32)
    big = jnp.float32(2 ** 30)
    gmask = lane_i < MOE_GROUPS
    gl = jnp.where(gmask, logits, -jnp.inf)
    gmax = jnp.max(gl, axis=-1, keepdims=True)
    p_top = 1.0 / jnp.sum(jnp.exp(gl - gmax), axis=-1, keepdims=True)
    g_top = jnp.min(jnp.where(gl == gmax, lane, big), axis=-1, keepdims=True)
    e_lane = lane_i - ROUTER_OFF
    lane_group = (e_lane // MOE_PER_GROUP).astype(F32)
    emask = (e_lane >= 0) & (e_lane < MOE_EXPERTS) & (lane_group == g_top)
    el = jnp.where(emask, logits, -jnp.inf)
    m1 = jnp.max(el, axis=-1, keepdims=True)
    i1 = jnp.min(jnp.where(el == m1, lane, big), axis=-1, keepdims=True)
    el2 = jnp.where(lane == i1, -jnp.inf, el)
    m2 = jnp.max(el2, axis=-1, keepdims=True)
    i2 = jnp.min(jnp.where(el2 == m2, lane, big), axis=-1, keepdims=True)
    e2 = jnp.exp(m2 - m1)
    den = 1.0 + e2
    gates = (jnp.where(lane == i1, (1.0 / den) * p_top, 0.0)
             + jnp.where(lane == i2, (e2 / den) * p_top, 0.0))
    route = jnp.where(lane == ROUTE_GROUP + g_top, 1.0, 0.0)
    for g in range(MOE_GROUPS):
        local = pltpu.roll(gates, LANES - ROUTER_OFF - g * MOE_PER_GROUP, 1)
        route = route + jnp.where((g_top == g) & (lane_i < MOE_PER_GROUP), local, 0.0)
    route_ref[...] = route
    cnt_ref[...] = jnp.broadcast_to(jnp.sum(route, axis=0, keepdims=True), (SUBLANES, LANES)).astype(jnp.int32)


def _output_stage(x, mixes, mods, mod_row, mod_tokens, wo_bf, g2, wr_hi, wr_lo, br, layer):
    tm = OUT_SEQS * S5_SEG
    n = x.shape[0]
    row = lambda w: pl.BlockSpec((tm, w), lambda i: (i, 0))
    const = lambda shape: pl.BlockSpec(shape, lambda i: (0,) * len(shape))
    per_blk = SUBLANES // OUT_SEQS
    return pl.pallas_call(
        _out_kernel,
        grid=(n // tm,),
        in_specs=[row(D_MODEL), row(256), row(256), row(256),
                  pl.BlockSpec((None, S5_SEG, OUT_SEQS * GROUP_WIDTH), lambda i: (i // per_blk, 0, i % per_blk)),
                  _mod_spec(layer, mod_row, mod_tokens // tm, 1),
                  pl.BlockSpec((None, D_MODEL, D_MODEL), lambda i: (layer, 0, 0)), const((1, D_MODEL)),
                  const((D_MODEL, LANES)), const((D_MODEL, LANES)), const((1, LANES))],
        out_specs=[row(D_MODEL), row(D_MODEL), row(LANES),
                   pl.BlockSpec((None, SUBLANES, LANES), lambda i: (i, 0, 0))],
        out_shape=[jax.ShapeDtypeStruct((n, D_MODEL), F32),
                   jax.ShapeDtypeStruct((n, D_MODEL), BF16),
                   jax.ShapeDtypeStruct((n, LANES), F32),
                   jax.ShapeDtypeStruct((n // tm, SUBLANES, LANES), jnp.int32)],
        compiler_params=_cparams("parallel"),
        name="output_stage",
    )(x, *mixes, mods, wo_bf, g2, wr_hi, wr_lo, br)


GROUP_HID = MOE_PER_GROUP * MOE_HIDDEN


MOE_CHUNK = 128


def _moe_kernel(cnt_ref, h2_ref, route_ref, xm_ref, mod_ref, w1_ref, w3_ref, w2_ref, fg_ref, o_ref,
                hs_scr, rs_scr, os_scr, *, final, tm):
    i = pl.program_id(0)
    off1 = cnt_ref[i, 0]
    off2 = off1 + cnt_ref[i, 1]
    off3 = off2 + cnt_ref[i, 2]
    starts = (jnp.int32(0), off1, off2, off3)
    ends = (off1, off2, off3, jnp.int32(tm))

    route = route_ref[...]
    r_hi, r_lo = _split(route)
    pick = (lax.broadcasted_iota(jnp.int32, (SUBLANES, LANES), 1)
            == ROUTE_GROUP + lax.broadcasted_iota(jnp.int32, (SUBLANES, LANES), 0))
    gt = lax.dot_general(jnp.where(pick, 1.0, 0.0).astype(BF16), r_hi, (((1,), (1,)), ((), ())),
                         preferred_element_type=F32)
    before = (lax.broadcasted_iota(jnp.int32, (tm, tm), 0)
              < lax.broadcasted_iota(jnp.int32, (tm, tm), 1))
    rank = jnp.dot(gt.astype(BF16), jnp.where(before, 1.0, 0.0).astype(BF16),
                   preferred_element_type=F32)
    gt_i = gt.astype(jnp.int32)
    rank_i = rank.astype(jnp.int32)
    pos = jnp.zeros((1, tm), jnp.int32)
    for g in range(MOE_GROUPS):
        pos = pos + gt_i[g:g + 1, :] * (rank_i[g:g + 1, :] + starts[g])
    perm = jnp.where(lax.broadcasted_iota(jnp.int32, (tm, tm), 0) == pos, 1.0, 0.0).astype(BF16)
    hs_scr[...] = jnp.dot(perm, h2_ref[...], preferred_element_type=F32).astype(BF16)
    rs_scr[...] = (jnp.dot(perm, r_hi, preferred_element_type=F32)
                   + jnp.dot(perm, r_lo, preferred_element_type=F32))

    for k in range(tm // MOE_CHUNK):
        r0 = k * MOE_CHUNK
        rows = slice(r0, r0 + MOE_CHUNK)
        first = r0
        last = r0 + MOE_CHUNK - 1
        g_lo = sum((first >= s).astype(jnp.int32) for s in starts[1:])
        g_hi = sum((last >= s).astype(jnp.int32) for s in starts[1:])
        x = hs_scr[rows, :]
        gates = rs_scr[rows, :]
        rowid = r0 + lax.broadcasted_iota(jnp.int32, (MOE_CHUNK, 1), 0)
        os_scr[rows, :] = jnp.zeros((MOE_CHUNK, D_MODEL), F32)

        def group_body(g, carry):
            lo = jnp.where(g == 0, starts[0], jnp.where(g == 1, starts[1], jnp.where(g == 2, starts[2], starts[3])))
            hi = jnp.where(g == 0, ends[0], jnp.where(g == 1, ends[1], jnp.where(g == 2, ends[2], ends[3])))
            a = jnp.dot(x, w1_ref[g], preferred_element_type=F32)
            b = jnp.dot(x, w3_ref[g], preferred_element_type=F32)
            hid = []
            for e in range(MOE_PER_GROUP):
                sl = slice(e * MOE_HIDDEN, (e + 1) * MOE_HIDDEN)
                hid.append((jax.nn.silu(a[:, sl]) * b[:, sl] * gates[:, e:e + 1]).astype(BF16))
            y = jnp.dot(jnp.concatenate(hid, axis=1), w2_ref[g], preferred_element_type=F32)
            member = (rowid >= lo) & (rowid < hi)
            os_scr[rows, :] = jnp.where(member, y, os_scr[rows, :])
            return carry

        lax.fori_loop(g_lo, g_hi + 1, group_body, 0)

    o_hi, o_lo = _split(os_scr[...])
    moe = (lax.dot_general(perm, o_hi, (((0,), (0,)), ((), ())), preferred_element_type=F32)
           + lax.dot_general(perm, o_lo, (((0,), (0,)), ((), ())), preferred_element_type=F32))
    out = xm_ref[...] + mod_ref[5:6, :] * moe
    if final:
        out = _rms_rows(out) * fg_ref[...]
    o_ref[...] = out


def _moe_weight_kernel(w1_ref, w3_ref, w2_ref, o1_ref, o3_ref, o2_ref):
    for e in range(MOE_PER_GROUP):
        sl = slice(e * MOE_HIDDEN, (e + 1) * MOE_HIDDEN)
        o1_ref[:, sl] = w1_ref[e].astype(BF16)
        o3_ref[:, sl] = w3_ref[e].astype(BF16)
        o2_ref[sl, :] = w2_ref[e].astype(BF16)


def _moe_weights(w1, w3, w2):
    up = pl.BlockSpec((None, MOE_PER_GROUP, D_MODEL, MOE_HIDDEN), lambda l, g: (l, g, 0, 0))
    down = pl.BlockSpec((None, MOE_PER_GROUP, MOE_HIDDEN, D_MODEL), lambda l, g: (l, g, 0, 0))
    out = pl.BlockSpec((None, None, D_MODEL, GROUP_HID), lambda l, g: (l, g, 0, 0))
    shape = jax.ShapeDtypeStruct((DEPTH, MOE_GROUPS, D_MODEL, GROUP_HID), BF16)
    return pl.pallas_call(
        _moe_weight_kernel,
        grid=(DEPTH, MOE_GROUPS),
        in_specs=[up, up, down],
        out_specs=[out, out, out],
        out_shape=[shape, shape, shape],
        compiler_params=_cparams("parallel", "parallel"),
        name="moe_weights",
    )(w1, w3, w2)


def _moe(h2, route, tile_counts, xm, mods, mod_row, mod_tokens, w1g, w3g, w2g, fg, layer, *, final, tm=512):
    n = h2.shape[0]
    cnt = tile_counts[:, 0, ROUTE_GROUP:ROUTE_GROUP + MOE_GROUPS].reshape(
        n // tm, tm // (OUT_SEQS * S5_SEG), MOE_GROUPS).sum(axis=1)
    row = lambda w: pl.BlockSpec((tm, w), lambda i, c: (i, 0))
    mod_tiles = mod_tokens // tm
    wspec = pl.BlockSpec((None, MOE_GROUPS, D_MODEL, GROUP_HID), lambda i, c: (layer, 0, 0, 0),
                         pipeline_mode=pl.Buffered(1))
    return pl.pallas_call(
        functools.partial(_moe_kernel, final=final, tm=tm),
        grid_spec=pltpu.PrefetchScalarGridSpec(
            num_scalar_prefetch=1,
            grid=(n // tm,),
            in_specs=[row(D_MODEL), row(LANES), row(D_MODEL),
                      pl.BlockSpec((None, None, 6, D_MODEL), lambda i, c: (layer, mod_row + i // mod_tiles, 0, 0)),
                      wspec, wspec, wspec,
                      pl.BlockSpec((1, D_MODEL), lambda i, c: (0, 0))],
            out_specs=row(D_MODEL),
            scratch_shapes=[pltpu.VMEM((tm, D_MODEL), BF16), pltpu.VMEM((tm, LANES), F32),
                            pltpu.VMEM((tm, D_MODEL), F32)]),
        out_shape=jax.ShapeDtypeStruct((n, D_MODEL), F32),
        compiler_params=_cparams("arbitrary"),
        name="moe",
    )(cnt, h2, route, xm, mods, w1g, w3g, w2g, fg)


def kernel(x_prompt, x_sample, cache_a_k, cache_a_v, cache_b_k, cache_b_v, state_ret, state_ssm, c, c_ctx, mod_w, mod_b, norm1_g, norm2_g, w_in, a_qn_g, a_kn_g, b_rel_bias, ret_decay, ret_gn_g, s5_lam_re, s5_lam_im, s5_log_dt, s5_b_re, s5_b_im, s5_c_re, s5_c_im, s5_d, s5_glu_w, w_out, moe_gw, moe_gb, moe_ew, moe_eb, moe_w1, moe_w3, moe_w2, final_norm_g):
    n_ctx = BATCH * SEQ
    n_lat = DEC_BATCH * DEC_SEQ
    lat_seg = DEC_SEQ // S5_SEG

    cond = jnp.zeros((SUBLANES, D_MODEL), F32).at[0].set(c_ctx).at[1:1 + DEC_BATCH].set(c)
    mods = _modulation(cond, mod_w, mod_b).reshape(DEPTH, SUBLANES, 6, D_MODEL)

    rope_tabs = _rope_tables()
    s5_a, s5_bm, s5_cre, s5_cim = _s5_prepare(s5_lam_re, s5_lam_im, s5_log_dt, s5_b_re, s5_b_im,
                                              s5_c_re, s5_c_im)
    cak = cache_a_k.reshape(DEC_BATCH, DEPTH, PAST_LEN, A_KV_HEADS * HEAD_DIM)
    cav = cache_a_v.reshape(DEC_BATCH, DEPTH, PAST_LEN, A_KV_HEADS * HEAD_DIM)
    cbk = cache_b_k.reshape(DEC_BATCH, DEPTH, PAST_LEN, B_HEADS * HEAD_DIM)
    cbv = cache_b_v.reshape(DEC_BATCH, DEPTH, PAST_LEN, B_HEADS * HEAD_DIM)

    xc = x_prompt.reshape(n_ctx, D_MODEL)
    xs = x_sample.reshape(n_lat, D_MODEL)
    w1_all, w3_all, w2_all = _moe_weights(moe_w1, moe_w3, moe_w2)
    eye_h = jnp.eye(C_HEADS, dtype=F32)
    s0_bd = (state_ret[:, :, :, :, :, None, :] * eye_h[None, None, None, :, None, :, None]).reshape(
        DEC_BATCH, DEPTH, 2, C_HEADS * HEAD_DIM, C_HEADS * HEAD_DIM)
    caches = ret_states = ssm_states = None
    h0_zero = jnp.zeros((2, SUBLANES, 2 * S5_SP), F32)
    w_in_bf = w_in.astype(BF16)
    wo_bf = w_out.astype(BF16)
    glu_bf = s5_glu_w.astype(BF16)
    for l in range(DEPTH):
        final = l == DEPTH - 1
        g1 = norm1_g[l].reshape(1, D_MODEL)
        g2 = norm2_g[l].reshape(1, D_MODEL)
        fg = final_norm_g.reshape(1, D_MODEL)
        qn = jnp.tile(a_qn_g[l], A_HEADS).reshape(1, 256)
        kn = jnp.tile(a_kn_g[l], A_KV_HEADS).reshape(1, 128)
        dec = jnp.broadcast_to(ret_decay[l].reshape(2 * C_HEADS, 1), (2 * C_HEADS, LANES))
        gn = ret_gn_g[l].reshape(1, 256)
        dvec = s5_d[l].reshape(1, GROUP_WIDTH)
        wr = jnp.zeros((D_MODEL, LANES), F32).at[:, :MOE_GROUPS].set(moe_gw[l]).at[
            :, ROUTER_OFF:ROUTER_OFF + MOE_EXPERTS].set(moe_ew[l])
        br = jnp.zeros((1, LANES), F32).at[0, :MOE_GROUPS].set(moe_gb[l]).at[
            0, ROUTER_OFF:ROUTER_OFF + MOE_EXPERTS].set(moe_eb[l])
        wr_hi = wr.astype(BF16)
        wr_lo = (wr - wr_hi.astype(F32)).astype(BF16)
        na_bias = _na_bias(b_rel_bias[l])

        zc, cg, du_tm, *caches = _project(xc, mods, 0, n_ctx, g1, w_in_bf, qn, kn, None, l, seq_len=SEQ,
                                          with_cache=True, prev_caches=caches)
        oa, ob = _ctx_attention(zc, BATCH, SEQ)
        oc, ret_states = _retention(zc, cg, dec, gn, None, l, nb=BATCH, seq_len=SEQ, prev_state=ret_states)
        od_tm, ssm_states = _s5(du_tm, h0_zero, s5_a, s5_bm, s5_cre, s5_cim, dvec, glu_bf, l,
                                nseg=1, fin_layer=l, fin_layers=DEPTH, prev_fin=ssm_states)
        xm, h2, route, counts = _output_stage(xc, (oa, ob, oc, od_tm), mods, 0, n_ctx, wo_bf, g2,
                                              wr_hi, wr_lo, br, l)
        xc = _moe(h2, route, counts, xm, mods, 0, n_ctx, w1_all, w3_all, w2_all, fg, l, final=final)

        zs, cg, du_tm = _project(xs, mods, 1, DEC_SEQ, g1, w_in_bf, qn, kn, rope_tabs, l, seq_len=DEC_SEQ)
        oa = _lat_attention_a(zs, cak, cav, l)
        ob = _lat_attention_b(zs, cbk, cbv, na_bias, l)
        oc = _retention(zs, cg, dec, gn, s0_bd, l, nb=DEC_BATCH, seq_len=DEC_SEQ)
        h0 = state_ssm[:, l].reshape(DEC_BATCH, 2, 2 * S5_SP).transpose(1, 0, 2)
        h0_seg = jnp.zeros((2, DEC_BATCH, lat_seg, 2 * S5_SP), F32)
        h0_seg = h0_seg.at[0, :, 0].set(h0[0]).at[1, :, lat_seg - 1].set(h0[1])
        od_tm, _ = _s5(du_tm, h0_seg.reshape(2, SUBLANES, 2 * S5_SP),
                       s5_a, s5_bm, s5_cre, s5_cim, dvec, glu_bf, l, nseg=lat_seg)
        xm, h2, route, counts = _output_stage(xs, (oa, ob, oc, od_tm), mods, 1, DEC_SEQ, wo_bf, g2,
                                              wr_hi, wr_lo, br, l)
        xs = _moe(h2, route, counts, xm, mods, 1, DEC_SEQ, w1_all, w3_all, w2_all, fg, l, final=final)

    new_ak, new_av, new_bk, new_bv = caches
    return (xc.reshape(BATCH, SEQ, D_MODEL), xs.reshape(DEC_BATCH, DEC_SEQ, D_MODEL),
            new_ak.reshape(BATCH, DEPTH, SEQ, A_KV_HEADS, HEAD_DIM),
            new_av.reshape(BATCH, DEPTH, SEQ, A_KV_HEADS, HEAD_DIM),
            new_bk.reshape(BATCH, DEPTH, SEQ, B_HEADS, HEAD_DIM),
            new_bv.reshape(BATCH, DEPTH, SEQ, B_HEADS, HEAD_DIM),
            ret_states,
            ssm_states.reshape(BATCH, DEPTH, 2, 2, S5_GROUPS, S5_STATE))
```

```python
import functools
import math

import numpy as np
import jax
import jax.numpy as jnp
from jax import lax
from jax.experimental import pallas as pl
from jax.experimental.pallas import tpu as pltpu

F32 = jnp.float32
BF16 = jnp.bfloat16

D_MODEL = 1024
BATCH = 32
SEQ = 256
DEPTH = 2
DEC_BATCH = 2
DEC_SEQ = 1024
PAST_LEN = 256
GRID_W = 64
HEAD_DIM = 64
GROUP_WIDTH = 256
A_HEADS = 4
A_KV_HEADS = 2
B_HEADS = 4
NA_ROWS = 8
NA_COLS = 16
C_HEADS = 4
S5_CH = 16
S5_GROUPS = 16
S5_STATE = 64
MOE_GROUPS = 4
MOE_PER_GROUP = 8
MOE_EXPERTS = 32
MOE_HIDDEN = 128
ROPE_THETA = 10000.0
EPS = 1e-6
IN_WIDTH = 2560
Q_SCALE = HEAD_DIM ** -0.5

OFF_AQ, OFF_AK, OFF_AV = 0, 256, 384
OFF_BQ, OFF_BK, OFF_BV = 512, 768, 1024
OFF_CQ, OFF_CK, OFF_CV, OFF_CG = 1280, 1536, 1792, 2048
OFF_DU = 2304

LANES = 128
SUBLANES = 8
S5_SP = S5_GROUPS * S5_STATE
S5_SEG = 256
S5_CHUNK = 256
ROUTER_OFF = 4
NEG_BIG = -1e30
VMEM_LIMIT = 56 * 1024 * 1024


def _cparams(*sem):
    return pltpu.CompilerParams(dimension_semantics=sem, vmem_limit_bytes=VMEM_LIMIT)


def _mod_spec(layer, first_row, tiles_per_row, grid_rank):
    if grid_rank == 1:
        return pl.BlockSpec((None, None, 6, D_MODEL), lambda i: (layer, first_row + i // tiles_per_row, 0, 0))
    return pl.BlockSpec((None, None, 6, D_MODEL), lambda i, g: (layer, first_row + i // tiles_per_row, 0, 0))


def _bdot(a, b):
    return jnp.dot(a.astype(BF16), b.astype(BF16), preferred_element_type=F32)


def _bdot_nt(a, b):
    return lax.dot_general(a.astype(BF16), b.astype(BF16), (((1,), (1,)), ((), ())),
                           preferred_element_type=F32)


def _bdot_tn(a, b):
    return lax.dot_general(a.astype(BF16), b.astype(BF16), (((0,), (0,)), ((), ())),
                           preferred_element_type=F32)


def _split(a):
    hi = a.astype(BF16)
    lo = (a - hi.astype(F32)).astype(BF16)
    return hi, lo


def _dot_hilo_lhs(a, b_bf16):
    hi, lo = _split(a)
    return (jnp.dot(hi, b_bf16, preferred_element_type=F32)
            + jnp.dot(lo, b_bf16, preferred_element_type=F32))


def _rms_rows(x):
    return x * lax.rsqrt(jnp.mean(x * x, axis=-1, keepdims=True) + EPS)


def _mod_kernel(cond_ref, w_ref, b_ref, o_ref):
    o_ref[...] = _bdot(jax.nn.silu(cond_ref[...]), w_ref[...]) + b_ref[...]


def _modulation(cond, mod_w, mod_b):
    tn = 1536
    return pl.pallas_call(
        _mod_kernel,
        grid=(DEPTH, 6 * D_MODEL // tn),
        in_specs=[pl.BlockSpec((SUBLANES, D_MODEL), lambda l, j: (0, 0)),
                  pl.BlockSpec((None, D_MODEL, tn), lambda l, j: (l, 0, j)),
                  pl.BlockSpec((None, 1, tn), lambda l, j: (l, 0, j))],
        out_specs=pl.BlockSpec((None, SUBLANES, tn), lambda l, j: (l, 0, j)),
        out_shape=jax.ShapeDtypeStruct((DEPTH, SUBLANES, 6 * D_MODEL), F32),
        compiler_params=_cparams("arbitrary", "arbitrary"),
        name="modulation",
    )(cond, mod_w, mod_b.reshape(DEPTH, 1, 6 * D_MODEL))


def _group_mean_matrix(w):
    ri = lax.broadcasted_iota(jnp.int32, (w, w), 0) // HEAD_DIM
    ci = lax.broadcasted_iota(jnp.int32, (w, w), 1) // HEAD_DIM
    return jnp.where(ri == ci, 1.0 / HEAD_DIM, 0.0).astype(BF16)


def _head_norm(t, g):
    ms = _dot_hilo_lhs(t * t, _group_mean_matrix(t.shape[1]))
    return t * lax.rsqrt(ms + EPS) * g


def _rope(t, cos, sa, sb):
    return (t * cos + pltpu.roll(t, LANES - 16, 1) * sa + pltpu.roll(t, 16, 1) * sb)


def _store_layer_slot(ref, slot, value):
    for s in range(ref.shape[0]):
        ref[s] = value if s == slot else jnp.zeros_like(value)


def _layer_slot_block(layer, first_call, tail):
    if first_call:
        return (None, DEPTH) + tail, (0,) * (1 + len(tail)), layer
    return (None, 1) + tail, (layer,) + (0,) * len(tail), 0


def _proj_kernel(*refs, rope, n_alias, with_cache, slot):
    x_ref, mod_ref, g1_ref, w_ref, qn_ref, kn_ref = refs[:6]
    n_in = 6
    if rope:
        cos_ref, sa_ref, sb_ref = refs[6:9]
        n_in = 9
    outs = refs[n_in + n_alias:]
    z_ref, cg_ref, du_ref = outs[:3]
    h = _rms_rows(x_ref[...]) * g1_ref[...] * (1.0 + mod_ref[1:2, :]) + mod_ref[0:1, :]
    z = jnp.dot(h.astype(BF16), w_ref[...], preferred_element_type=F32)
    aq = _head_norm(z[:, OFF_AQ:OFF_AK], qn_ref[...])
    ak = _head_norm(z[:, OFF_AK:OFF_AV], kn_ref[...])
    for j in range(3):
        t = aq[:, j * LANES:(j + 1) * LANES] if j < 2 else ak
        if rope:
            cj = 0 if j == 2 else j
            sl = slice(cj * LANES, (cj + 1) * LANES)
            t = _rope(t, cos_ref[:, sl], sa_ref[:, sl], sb_ref[:, sl])
        if j == 2:
            ak = t
        z_ref[:, j * LANES:(j + 1) * LANES] = t.astype(BF16)
    z_ref[:, OFF_AV:OFF_CK] = z[:, OFF_AV:OFF_CK].astype(BF16)
    z_ref[:, OFF_CK:OFF_CV] = (z[:, OFF_CK:OFF_CV] * Q_SCALE).astype(BF16)
    z_ref[:, OFF_CV:OFF_CG] = z[:, OFF_CV:OFF_CG].astype(BF16)
    cg_ref[...] = z[:, OFF_CG:OFF_DU]
    du_ref[...] = z[:, OFF_DU:]
    if with_cache:
        ak_ref, av_ref, bk_ref, bv_ref = outs[3:7]
        _store_layer_slot(ak_ref, slot, ak)
        _store_layer_slot(av_ref, slot, z[:, OFF_AV:OFF_BQ])
        _store_layer_slot(bk_ref, slot, z[:, OFF_BK:OFF_BV])
        _store_layer_slot(bv_ref, slot, z[:, OFF_BV:OFF_CQ])


def _du_spec(grid_rank):
    if grid_rank == 1:
        return pl.BlockSpec((None, S5_SEG, GROUP_WIDTH), lambda i: (i // SUBLANES, 0, i % SUBLANES))
    return pl.BlockSpec((None, S5_SEG, GROUP_WIDTH), lambda i, g: (i // SUBLANES, 0, i % SUBLANES))


def _project(x, mods, mod_row, mod_tokens, g1, w_in_bf, qn, kn, rope_tabs, layer, *, seq_len,
             with_cache=False, prev_caches=None):
    tm = S5_SEG
    n = x.shape[0]
    rope = rope_tabs is not None
    in_specs = [pl.BlockSpec((tm, D_MODEL), lambda i: (i, 0)),
                _mod_spec(layer, mod_row, mod_tokens // tm, 1),
                pl.BlockSpec((1, D_MODEL), lambda i: (0, 0)),
                pl.BlockSpec((None, D_MODEL, IN_WIDTH), lambda i: (layer, 0, 0)),
                pl.BlockSpec((1, 256), lambda i: (0, 0)),
                pl.BlockSpec((1, 128), lambda i: (0, 0))]
    args = [x, mods, g1, w_in_bf, qn, kn]
    if rope:
        tps = seq_len // tm
        in_specs += [pl.BlockSpec((tm, 256), lambda i: (i % tps, 0))] * 3
        args += list(rope_tabs)
    out_specs = [pl.BlockSpec((tm, OFF_CG), lambda i: (i, 0)),
                 pl.BlockSpec((tm, GROUP_WIDTH), lambda i: (i, 0)), _du_spec(1)]
    out_shape = [jax.ShapeDtypeStruct((n, OFF_CG), BF16),
                 jax.ShapeDtypeStruct((n, GROUP_WIDTH), F32),
                 jax.ShapeDtypeStruct((n // (tm * SUBLANES), S5_SEG, SUBLANES * GROUP_WIDTH), F32)]
    aliases = {}
    n_alias = 0
    slot = 0
    if with_cache:
        assert tm == seq_len
        nb = n // seq_len
        for w in (128, 128, 256, 256):
            blk, idx, slot = _layer_slot_block(layer, prev_caches is None, (seq_len, w))
            out_specs.append(pl.BlockSpec(blk, lambda i, idx=idx: (i,) + idx))
            out_shape.append(jax.ShapeDtypeStruct((nb, DEPTH, seq_len, w), F32))
        if prev_caches is not None:
            n_alias = len(prev_caches)
            for k, arr in enumerate(prev_caches):
                aliases[len(args)] = 3 + k
                in_specs.append(pl.BlockSpec(memory_space=pl.ANY))
                args.append(arr)
    return pl.pallas_call(
        functools.partial(_proj_kernel, rope=rope, n_alias=n_alias, with_cache=with_cache, slot=slot),
        grid=(n // tm,),
        in_specs=in_specs,
        out_specs=out_specs,
        out_shape=out_shape,
        input_output_aliases=aliases,
        compiler_params=_cparams("parallel"),
        name="project",
    )(*args)


def _rope_tables():
    t = jnp.arange(DEC_SEQ)
    row = (t // GRID_W).astype(F32)
    col = (t % GRID_W).astype(F32)
    nf = HEAD_DIM // 4
    inv = ROPE_THETA ** (-jnp.arange(nf, dtype=F32) / nf)
    ang_r = row[:, None] * inv[None, :]
    ang_c = col[:, None] * inv[None, :]
    zeros = jnp.zeros_like(ang_r)
    cos = jnp.concatenate([jnp.cos(ang_r), jnp.cos(ang_r), jnp.cos(ang_c), jnp.cos(ang_c)], axis=-1)
    sa = jnp.concatenate([-jnp.sin(ang_r), zeros, -jnp.sin(ang_c), zeros], axis=-1)
    sb = jnp.concatenate([zeros, jnp.sin(ang_r), zeros, jnp.sin(ang_c)], axis=-1)
    return tuple(jnp.tile(a, (1, 4)) for a in (cos, sa, sb))


N_HEADS = 4


def _lane_head(width):
    return lax.broadcasted_iota(jnp.int32, (1, width), 1) // HEAD_DIM


def _stack_heads(q):
    head = _lane_head(q.shape[1])
    return jnp.concatenate([jnp.where(head == h, q, 0.0) for h in range(N_HEADS)], axis=0).astype(BF16)


def _stack_heads_gqa(q):
    lo = lax.broadcasted_iota(jnp.int32, (1, LANES), 1) < HEAD_DIM
    q = q.astype(F32)
    q01, q23 = q[:, :LANES], q[:, LANES:]
    blocks = [jnp.where(lo, q01, 0.0), jnp.where(lo, pltpu.roll(q01, HEAD_DIM, 1), 0.0),
              jnp.where(lo, 0.0, pltpu.roll(q23, HEAD_DIM, 1)), jnp.where(lo, 0.0, q23)]
    return jnp.concatenate(blocks, axis=0).astype(BF16)


def _spread_kv_gqa(v):
    lo = lax.broadcasted_iota(jnp.int32, (1, LANES), 1) < HEAD_DIM
    v = v.astype(F32)
    vr = pltpu.roll(v, HEAD_DIM, 1)
    return jnp.concatenate([jnp.where(lo, v, vr), jnp.where(lo, vr, v)], axis=1)


def _mha(qs, blocks, tq):
    scores = []
    for k, _, bias in blocks:
        s = _bdot_nt(qs, k)
        scores.append(s if bias is None else s + bias)
    m = functools.reduce(jnp.maximum, [jnp.max(s, axis=-1, keepdims=True) for s in scores])
    es = [jnp.exp(s - m) for s in scores]
    denom = functools.reduce(jnp.add, [jnp.sum(e, axis=-1, keepdims=True) for e in es])
    ps = [e.astype(BF16) for e in es]
    head = _lane_head(N_HEADS * HEAD_DIM)
    vals = [v.astype(BF16) for _, v, _ in blocks]
    o = None
    dall = None
    for h in range(N_HEADS):
        rows = slice(h * tq, (h + 1) * tq)
        for p, v in zip(ps, vals):
            t = jnp.dot(p[rows], jnp.where(head == h, v, jnp.zeros_like(v)), preferred_element_type=F32)
            o = t if o is None else o + t
        d = jnp.where(head == h, denom[rows], 0.0)
        dall = d if dall is None else dall + d
    return (o / dall).astype(BF16)


def _ctx_attn_kernel(aq_ref, ak_ref, av_ref, bq_ref, bk_ref, bv_ref, oa_ref, ob_ref):
    tq = aq_ref.shape[0]
    oa_ref[...] = _mha(_stack_heads_gqa(aq_ref[...] * Q_SCALE),
                       [(ak_ref[...], _spread_kv_gqa(av_ref[...]), None)], tq)
    ob_ref[...] = _mha(_stack_heads(bq_ref[...] * Q_SCALE), [(bk_ref[...], bv_ref[...], None)], tq)


def _ctx_attention(z, nb, seq_len):
    def col(width, off):
        return pl.BlockSpec((seq_len, width), lambda b: (b, off // width))
    return pl.pallas_call(
        _ctx_attn_kernel,
        grid=(nb,),
        in_specs=[col(256, OFF_AQ), col(128, OFF_AK), col(128, OFF_AV),
                  col(256, OFF_BQ), col(256, OFF_BK), col(256, OFF_BV)],
        out_specs=[pl.BlockSpec((seq_len, 256), lambda b: (b, 0))] * 2,
        out_shape=[jax.ShapeDtypeStruct((nb * seq_len, 256), BF16)] * 2,
        compiler_params=_cparams("parallel"),
        name="ctx_attention",
    )(z, z, z, z, z, z)


def _lat_attn_a_kernel(q_ref, kn_ref, vn_ref, kc_ref, vc_ref, o_ref):
    o_ref[...] = _mha(_stack_heads_gqa(q_ref[...] * Q_SCALE),
                      [(kc_ref[...], _spread_kv_gqa(vc_ref[...]), None),
                       (kn_ref[...], _spread_kv_gqa(vn_ref[...]), None)], q_ref.shape[0])


def _lat_attention_a(z, cache_k, cache_v, layer, tq=256):
    nq = DEC_SEQ // tq
    cache_spec = pl.BlockSpec((None, None, PAST_LEN, 128), lambda b, j: (b, layer, 0, 0))
    return pl.pallas_call(
        _lat_attn_a_kernel,
        grid=(DEC_BATCH, nq),
        in_specs=[pl.BlockSpec((tq, 256), lambda b, j: (b * nq + j, OFF_AQ // 256)),
                  pl.BlockSpec((DEC_SEQ, 128), lambda b, j: (b, OFF_AK // 128)),
                  pl.BlockSpec((DEC_SEQ, 128), lambda b, j: (b, OFF_AV // 128)),
                  cache_spec, cache_spec],
        out_specs=pl.BlockSpec((tq, 256), lambda b, j: (b * nq + j, 0)),
        out_shape=jax.ShapeDtypeStruct((DEC_BATCH * DEC_SEQ, 256), BF16),
        compiler_params=_cparams("parallel", "parallel"),
        name="lat_attention_a",
    )(z, z, z, cache_k, cache_v)


NA_KEYS = NA_ROWS * GRID_W


NA_PAIRS = 2 * NA_ROWS - 2


def _na_kernel(q_ref, k_ref, v_ref, kc_ref, vc_ref, bias_ref, o_ref):
    r = pl.program_id(1)
    rows = DEC_SEQ // GRID_W
    row_start = jnp.clip(r - NA_ROWS // 2, 0, rows - NA_ROWS)
    start = pl.multiple_of(row_start * GRID_W, GRID_W)
    rel0 = row_start - r + NA_ROWS - 1
    kl = k_ref[pl.ds(start, NA_KEYS), :]
    vl = v_ref[pl.ds(start, NA_KEYS), :]
    bias = jnp.concatenate(
        [jnp.concatenate([bias_ref[h, rel0 + 2 * jp] for jp in range(NA_ROWS // 2)], axis=1)
         for h in range(B_HEADS)], axis=0)
    o_ref[...] = _mha(_stack_heads(q_ref[...] * Q_SCALE),
                      [(kl, vl, bias), (kc_ref[...], vc_ref[...], None)], GRID_W)


def _na_bias(rel_bias):
    nrel = 2 * NA_COLS - 1
    period = 2 * GRID_W
    b = rel_bias.astype(F32)
    ext = jnp.concatenate([b[..., NA_COLS - 1:],
                           jnp.zeros(b.shape[:-1] + (period - nrel,), F32),
                           b[..., :NA_COLS - 1]], axis=-1)
    flat = jnp.tile(ext, (1, 1, GRID_W))[..., :GRID_W * (period - 1)]
    toe = flat.reshape(b.shape[:-1] + (GRID_W, period - 1))[..., :GRID_W]
    col_start = np.clip(np.arange(GRID_W) - NA_COLS // 2, 0, GRID_W - NA_COLS)
    kc = np.arange(GRID_W)
    inside = (kc[None, :] >= col_start[:, None]) & (kc[None, :] < col_start[:, None] + NA_COLS)
    toe = jnp.where(jnp.asarray(inside), toe, NEG_BIG)
    return jnp.concatenate([toe[:, :-1], toe[:, 1:]], axis=-1)


def _lat_attention_b(z, cache_k, cache_v, bias, layer):
    rows = DEC_SEQ // GRID_W
    cache_spec = pl.BlockSpec((None, None, PAST_LEN, 256), lambda b, r: (b, layer, 0, 0))
    return pl.pallas_call(
        _na_kernel,
        grid=(DEC_BATCH, rows),
        in_specs=[pl.BlockSpec((GRID_W, 256), lambda b, r: (b * rows + r, OFF_BQ // 256)),
                  pl.BlockSpec((DEC_SEQ, 256), lambda b, r: (b, OFF_BK // 256)),
                  pl.BlockSpec((DEC_SEQ, 256), lambda b, r: (b, OFF_BV // 256)),
                  cache_spec, cache_spec,
                  pl.BlockSpec((B_HEADS, NA_PAIRS, GRID_W, 2 * GRID_W), lambda b, r: (0, 0, 0, 0))],
        out_specs=pl.BlockSpec((GRID_W, 256), lambda b, r: (b * rows + r, 0)),
        out_shape=jax.ShapeDtypeStruct((DEC_BATCH * DEC_SEQ, 256), BF16),
        compiler_params=_cparams("parallel", "parallel"),
        name="lat_attention_b",
    )(z, z, z, cache_k, cache_v, bias)


def _retention_kernel(q_ref, g_ref, k_ref, v_ref, dec_ref, gn_ref, *rest, seq_len, tq, has_state,
                      hoist_decay, slot):
    if has_state:
        s0_ref, o_ref, dec_scr = rest
    else:
        o_ref, st_ref, dec_scr = rest[-3:]
    head = _lane_head(C_HEADS * HEAD_DIM)
    lg = jax.nn.log_sigmoid(dec_ref[...])

    def per_lane(row0):
        out = jnp.zeros((1, C_HEADS * HEAD_DIM), F32)
        for h in range(C_HEADS):
            out = jnp.where(head == h, lg[row0 + h:row0 + h + 1, 0:1], out)
        return out

    lgf_l, lgb_l = per_lane(0), per_lane(C_HEADS)
    i0 = pl.program_id(1) * tq
    qi = (i0 + lax.broadcasted_iota(jnp.int32, (tq, 1), 0)).astype(F32)

    def fill_decay():
        kj = lax.broadcasted_iota(jnp.int32, (1, seq_len), 1).astype(F32)
        diff = qi - kj
        for h in range(C_HEADS):
            lgf = lg[h:h + 1, 0:1]
            lgb = lg[C_HEADS + h:C_HEADS + h + 1, 0:1]
            dec_scr[h * tq:(h + 1) * tq, :] = (
                jnp.where(diff >= 0, jnp.exp(lgf * jnp.maximum(diff, 0.0)), 0.0)
                + jnp.where(diff <= 0, jnp.exp(lgb * jnp.maximum(-diff, 0.0)), 0.0))

    if hoist_decay:
        pl.when(pl.program_id(0) == 0)(fill_decay)
    else:
        fill_decay()

    q = q_ref[...]
    k = k_ref[...]
    v = v_ref[...].astype(BF16)
    sc = (_bdot_nt(_stack_heads(q), k) * dec_scr[...]).astype(BF16)
    o = None
    for h in range(C_HEADS):
        t = jnp.dot(sc[h * tq:(h + 1) * tq], jnp.where(head == h, v, jnp.zeros_like(v)),
                    preferred_element_type=F32)
        o = t if o is None else o + t
    if has_state:
        o = (o + _bdot(q, s0_ref[0]) * jnp.exp(lgf_l * (qi + 1.0))
             + _bdot(q, s0_ref[1]) * jnp.exp(lgb_l * (seq_len - qi)))
    gm = _group_mean_matrix(C_HEADS * HEAD_DIM)
    dlt = o - _dot_hilo_lhs(o, gm)
    var = _dot_hilo_lhs(dlt * dlt, gm)
    o_ref[...] = (dlt * lax.rsqrt(var + EPS) * gn_ref[...] * jax.nn.silu(g_ref[...])).astype(BF16)
    if not has_state:
        kpos = lax.broadcasted_iota(jnp.int32, (seq_len, 1), 0).astype(F32)
        sf = _bdot_tn(k * jnp.exp(lgf_l * (seq_len - 1.0 - kpos)), v)
        sb = _bdot_tn(k * jnp.exp(lgb_l * kpos), v)
        for s in range(st_ref.shape[0]):
            for h in range(C_HEADS):
                sl = slice(h * HEAD_DIM, (h + 1) * HEAD_DIM)
                st_ref[s, 0, h] = sf[sl, sl] if s == slot else jnp.zeros((HEAD_DIM, HEAD_DIM), F32)
                st_ref[s, 1, h] = sb[sl, sl] if s == slot else jnp.zeros((HEAD_DIM, HEAD_DIM), F32)


def _retention(z, cg, dec, gn, s0, layer, *, nb, seq_len, prev_state=None, tq=256):
    nq = seq_len // tq
    has_state = s0 is not None
    aliases = {}
    slot = 0
    in_specs = [pl.BlockSpec((tq, 256), lambda b, j: (b * nq + j, OFF_CQ // 256)),
                pl.BlockSpec((tq, 256), lambda b, j: (b * nq + j, 0)),
                pl.BlockSpec((seq_len, 256), lambda b, j: (b, OFF_CK // 256)),
                pl.BlockSpec((seq_len, 256), lambda b, j: (b, OFF_CV // 256)),
                pl.BlockSpec((SUBLANES, LANES), lambda b, j: (0, 0)),
                pl.BlockSpec((1, 256), lambda b, j: (0, 0))]
    args = [z, cg, z, z, dec, gn]
    o_spec = pl.BlockSpec((tq, 256), lambda b, j: (b * nq + j, 0))
    o_shape = jax.ShapeDtypeStruct((nb * seq_len, 256), BF16)
    if has_state:
        in_specs.append(pl.BlockSpec((None, None, 2, 256, 256), lambda b, j: (b, layer, 0, 0, 0)))
        args.append(s0)
        out_specs, out_shape = o_spec, o_shape
    else:
        assert nq == 1
        blk, idx, slot = _layer_slot_block(layer, prev_state is None, (2, C_HEADS, HEAD_DIM, HEAD_DIM))
        out_specs = [o_spec, pl.BlockSpec(blk, lambda b, j: (b,) + idx)]
        out_shape = [o_shape, jax.ShapeDtypeStruct((nb, DEPTH, 2, C_HEADS, HEAD_DIM, HEAD_DIM), F32)]
        if prev_state is not None:
            aliases[len(args)] = 1
            in_specs.append(pl.BlockSpec(memory_space=pl.ANY))
            args.append(prev_state)
    return pl.pallas_call(
        functools.partial(_retention_kernel, seq_len=seq_len, tq=tq, has_state=has_state,
                          hoist_decay=nq == 1, slot=slot),
        grid=(nb, nq),
        in_specs=in_specs,
        out_specs=out_specs,
        out_shape=out_shape,
        scratch_shapes=[pltpu.VMEM((C_HEADS * tq, seq_len), F32)],
        input_output_aliases=aliases,
        compiler_params=_cparams("arbitrary", "arbitrary"),
        name="retention",
    )(*args)


def _s5_prep_kernel(lre_ref, lim_ref, ldt_ref, bre_ref, bim_ref, cre_ref, cim_ref,
                    a_ref, bm_ref, cro_ref, cio_ref, bm_scr, cr_scr, ci_scr):
    lre = lre_ref[...]
    lim = lim_ref[...]
    dt = jnp.exp(ldt_ref[...])
    mag = jnp.exp(lre * dt)
    a_re = mag * jnp.cos(lim * dt)
    a_im = mag * jnp.sin(lim * dt)
    den = lre * lre + lim * lim
    r_re = ((a_re - 1.0) * lre + a_im * lim) / den
    r_im = (a_im * lre - (a_re - 1.0) * lim) / den
    bm_scr[...] = jnp.zeros_like(bm_scr)
    cr_scr[...] = jnp.zeros_like(cr_scr)
    ci_scr[...] = jnp.zeros_like(ci_scr)
    for g in range(S5_GROUPS):
        rows = slice(g * S5_CH, (g + 1) * S5_CH)
        cols = slice(g * S5_STATE, (g + 1) * S5_STATE)
        a_ref[0:1, cols] = a_re[g:g + 1, :]
        a_ref[1:2, cols] = a_im[g:g + 1, :]
        rr, ri = r_re[g:g + 1, :], r_im[g:g + 1, :]
        br, bi = bre_ref[g], bim_ref[g]
        bm_scr[rows, cols] = rr * br - ri * bi
        bm_scr[rows, S5_SP + g * S5_STATE:S5_SP + (g + 1) * S5_STATE] = rr * bi + ri * br
        cr_scr[cols, rows] = cre_ref[g]
        ci_scr[cols, rows] = cim_ref[g]
    bm_ref[...] = bm_scr[...].astype(BF16)
    cro_ref[...] = cr_scr[...].astype(BF16)
    cio_ref[...] = ci_scr[...].astype(BF16)


def _s5_prepare(lam_re, lam_im, log_dt, b_re, b_im, c_re, c_im):
    gp = (S5_GROUPS, S5_STATE)
    ldt = jnp.broadcast_to(log_dt[..., None], (DEPTH, 2) + gp)
    bt = [jnp.swapaxes(t, -1, -2) for t in (b_re, b_im)]
    ct = [jnp.swapaxes(t, -1, -2) for t in (c_re, c_im)]

    def spec(*tail):
        return pl.BlockSpec((None, None) + tail, lambda l, d: (l, d) + (0,) * len(tail))

    return pl.pallas_call(
        _s5_prep_kernel,
        grid=(DEPTH, 2),
        in_specs=[spec(*gp)] * 3 + [spec(S5_GROUPS, S5_CH, S5_STATE)] * 2 + [spec(S5_GROUPS, S5_STATE, S5_CH)] * 2,
        out_specs=[spec(2, S5_SP), spec(GROUP_WIDTH, 2 * S5_SP), spec(S5_SP, GROUP_WIDTH), spec(S5_SP, GROUP_WIDTH)],
        out_shape=[jax.ShapeDtypeStruct((DEPTH, 2, 2, S5_SP), F32),
                   jax.ShapeDtypeStruct((DEPTH, 2, GROUP_WIDTH, 2 * S5_SP), BF16),
                   jax.ShapeDtypeStruct((DEPTH, 2, S5_SP, GROUP_WIDTH), BF16),
                   jax.ShapeDtypeStruct((DEPTH, 2, S5_SP, GROUP_WIDTH), BF16)],
        scratch_shapes=[pltpu.VMEM((GROUP_WIDTH, 2 * S5_SP), F32), pltpu.VMEM((S5_SP, GROUP_WIDTH), F32),
                        pltpu.VMEM((S5_SP, GROUP_WIDTH), F32)],
        compiler_params=_cparams("parallel", "parallel"),
        name="s5_prepare",
    )(lam_re, lam_im, ldt, bt[0], bt[1], ct[0], ct[1])


def _cmul(ar, ai, br, bi):
    return ar * br - ai * bi, ar * bi + ai * br


def _s5_kernel(u_ref, h0_ref, a_ref, bm_ref, cre_ref, cim_ref, dvec_ref, glu_ref, *rest, nseg, slot):
    od_ref, fin_ref, x_scr, s_scr, y_scr = rest[-5:]
    steps = S5_SEG
    rows = steps * SUBLANES
    chunk = S5_CHUNK
    chunk_steps = chunk // SUBLANES
    nchunk = rows // chunk
    seg = lax.broadcasted_iota(jnp.int32, (SUBLANES, S5_SP), 0) % nseg

    for d in range(2):
        ar = jnp.broadcast_to(a_ref[d, 0:1, :], (SUBLANES, S5_SP))
        ai = jnp.broadcast_to(a_ref[d, 1:2, :], (SUBLANES, S5_SP))

        def row0(k):
            c = k if d == 0 else nchunk - 1 - k
            return c * chunk if isinstance(c, int) else pl.multiple_of(c * chunk, chunk)

        def input_part(k, buf):
            x_scr[buf] = jnp.dot(u_ref[pl.ds(row0(k), chunk), :].astype(BF16), bm_ref[d],
                                 preferred_element_type=F32)

        def scan_part(buf, carry, store):
            sr, si = carry
            for t in range(chunk_steps):
                r = (t if d == 0 else chunk_steps - 1 - t) * SUBLANES
                pr, pi = _cmul(ar, ai, sr, si)
                sr = pr + x_scr[buf, r:r + SUBLANES, 0:S5_SP]
                si = pi + x_scr[buf, r:r + SUBLANES, S5_SP:]
                if store:
                    s_scr[buf, r:r + SUBLANES, 0:S5_SP] = sr
                    s_scr[buf, r:r + SUBLANES, S5_SP:] = si
            return sr, si

        def output_part(k, buf):
            y = _bdot(s_scr[buf, :, 0:S5_SP], cre_ref[d]) - _bdot(s_scr[buf, :, S5_SP:], cim_ref[d])
            if d == 0:
                y_scr[pl.ds(row0(k), chunk), :] = y
            else:
                y_scr[pl.ds(row0(k), chunk), :] += y

        def half(k, buf, carry, store, nxt=True, prev=True):
            if nxt:
                input_part(k + 1, 1 - buf)
            carry = scan_part(buf, carry, store)
            if store and prev:
                output_part(k - 1, 1 - buf)
            return carry

        def run_pass(carry, store):
            input_part(0, 0)
            carry = half(0, 0, carry, store, prev=False)
            carry = half(1, 1, carry, store)

            def pair(j, c):
                c = half(2 * j, 0, c, store)
                return half(2 * j + 1, 1, c, store)
            carry = lax.fori_loop(1, nchunk // 2 - 1, pair, carry)
            carry = half(nchunk - 2, 0, carry, store)
            carry = half(nchunk - 1, 1, carry, store, nxt=False)
            if store:
                output_part(nchunk - 1, 1)
            return carry

        init = (h0_ref[d, :, 0:S5_SP], h0_ref[d, :, S5_SP:])
        if nseg > 1:
            zero = jnp.zeros((SUBLANES, S5_SP), F32)
            fr, fi = run_pass((zero, zero), store=False)
            pr, pi = ar, ai
            for _ in range(int(math.log2(steps))):
                pr, pi = _cmul(pr, pi, pr, pi)
            cr, ci = init
            shift = 1 if d == 0 else SUBLANES - 1
            order = range(1, nseg) if d == 0 else range(nseg - 2, -1, -1)
            for s in order:
                ncr, nci = pltpu.roll(cr, shift, 0), pltpu.roll(ci, shift, 0)
                nfr, nfi = pltpu.roll(fr, shift, 0), pltpu.roll(fi, shift, 0)
                qr, qi = _cmul(pr, pi, ncr, nci)
                cr = jnp.where(seg == s, qr + nfr, cr)
                ci = jnp.where(seg == s, qi + nfi, ci)
            init = (cr, ci)
        sr, si = run_pass(init, store=True)
        for s in range(fin_ref.shape[1] // (4 * S5_SP)):
            base = (4 * s + 2 * d) * S5_SP
            fin_ref[:, base:base + S5_SP] = sr if s == slot else jnp.zeros_like(sr)
            fin_ref[:, base + S5_SP:base + 2 * S5_SP] = si if s == slot else jnp.zeros_like(si)

    def obody(c, carry):
        r0 = pl.multiple_of(c * chunk, chunk)
        y = y_scr[pl.ds(r0, chunk), :] + dvec_ref[...] * u_ref[pl.ds(r0, chunk), :]
        zz = jax.nn.gelu(y)
        od_ref[pl.ds(r0, chunk), :] = (zz * jax.nn.sigmoid(_bdot(zz, glu_ref[...]))).astype(BF16)
        return carry
    lax.fori_loop(0, nchunk, obody, 0)


def _s5(du_tm, h0, a, bmat, cre, cim, dvec, glu_bf, layer, *, nseg, fin_layer=0, fin_layers=1,
        prev_fin=None):
    nblk = du_tm.shape[0]
    rows = S5_SEG * SUBLANES
    fin_w = 4 * S5_SP
    in_specs = [pl.BlockSpec((None, rows, GROUP_WIDTH), lambda i: (i, 0, 0)),
                pl.BlockSpec((2, SUBLANES, 2 * S5_SP), lambda i: (0, 0, 0)),
                pl.BlockSpec((None, 2, 2, S5_SP), lambda i: (layer, 0, 0, 0)),
                pl.BlockSpec((None, 2, GROUP_WIDTH, 2 * S5_SP), lambda i: (layer, 0, 0, 0)),
                pl.BlockSpec((None, 2, S5_SP, GROUP_WIDTH), lambda i: (layer, 0, 0, 0)),
                pl.BlockSpec((None, 2, S5_SP, GROUP_WIDTH), lambda i: (layer, 0, 0, 0)),
                pl.BlockSpec((1, GROUP_WIDTH), lambda i: (0, 0)),
                pl.BlockSpec((None, GROUP_WIDTH, GROUP_WIDTH), lambda i: (layer, 0, 0))]
    args = [du_tm.reshape(nblk, rows, GROUP_WIDTH), h0, a, bmat, cre, cim, dvec, glu_bf]
    aliases = {}
    if prev_fin is not None:
        aliases[len(args)] = 1
        in_specs.append(pl.BlockSpec(memory_space=pl.ANY))
        args.append(prev_fin)
        fin_spec, slot = pl.BlockSpec((SUBLANES, fin_w), lambda i: (i, fin_layer)), 0
    else:
        fin_spec, slot = pl.BlockSpec((SUBLANES, fin_layers * fin_w), lambda i: (i, 0)), fin_layer
    od, fin = pl.pallas_call(
        functools.partial(_s5_kernel, nseg=nseg, slot=slot),
        grid=(nblk,),
        in_specs=in_specs,
        out_specs=[pl.BlockSpec((None, rows, GROUP_WIDTH), lambda i: (i, 0, 0)), fin_spec],
        out_shape=[jax.ShapeDtypeStruct((nblk, rows, GROUP_WIDTH), BF16),
                   jax.ShapeDtypeStruct((nblk * SUBLANES, fin_layers * fin_w), F32)],
        scratch_shapes=[pltpu.VMEM((2, S5_CHUNK, 2 * S5_SP), F32), pltpu.VMEM((2, S5_CHUNK, 2 * S5_SP), F32),
                        pltpu.VMEM((rows, GROUP_WIDTH), F32)],
        input_output_aliases=aliases,
        compiler_params=_cparams("parallel"),
        name="s5",
    )(*args)
    return od.reshape(nblk, S5_SEG, SUBLANES * GROUP_WIDTH), fin


ROUTE_GROUP = MOE_PER_GROUP
OUT_SEQS = 2


def _out_kernel(x_ref, oa_ref, ob_ref, oc_ref, od_ref, mod_ref, wo_ref, g2_ref, wrh_ref, wrl_ref, br_ref,
                xm_ref, h2_ref, route_ref, cnt_ref):
    od = jnp.concatenate([od_ref[:, s * GROUP_WIDTH:(s + 1) * GROUP_WIDTH] for s in range(OUT_SEQS)], axis=0)
    mix = functools.reduce(jnp.add, [
        _bdot(o, wo_ref[i * GROUP_WIDTH:(i + 1) * GROUP_WIDTH, :])
        for i, o in enumerate((oa_ref[...], ob_ref[...], oc_ref[...], od))])
    xm = x_ref[...] + mod_ref[2:3, :] * mix
    xm_ref[...] = xm
    h2 = _rms_rows(xm) * g2_ref[...] * (1.0 + mod_ref[4:5, :]) + mod_ref[3:4, :]
    h2_ref[...] = h2.astype(BF16)

    h_hi, h_lo = _split(h2)
    logits = (jnp.dot(h_hi, wrh_ref[...], preferred_element_type=F32)
              + jnp.dot(h_hi, wrl_ref[...], preferred_element_type=F32)
              + jnp.dot(h_lo, wrh_ref[...], preferred_element_type=F32)) + br_ref[...]
    lane_i = lax.broadcasted_iota(jnp.int32, logits.shape, 1)
    lane = lane_i.astype(F32)
    big = jnp.float32(2 ** 30)
    gmask = lane_i < MOE_GROUPS
    gl = jnp.where(gmask, logits, -jnp.inf)
    gmax = jnp.max(gl, axis=-1, keepdims=True)
    p_top = 1.0 / jnp.sum(jnp.exp(gl - gmax), axis=-1, keepdims=True)
    g_top = jnp.min(jnp.where(gl == gmax, lane, big), axis=-1, keepdims=True)
    e_lane = lane_i - ROUTER_OFF
    lane_group = (e_lane // MOE_PER_GROUP).astype(F32)
    emask = (e_lane >= 0) & (e_lane < MOE_EXPERTS) & (lane_group == g_top)
    el = jnp.where(emask, logits, -jnp.inf)
    m1 = jnp.max(el, axis=-1, keepdims=True)
    i1 = jnp.min(jnp.where(el == m1, lane, big), axis=-1, keepdims=True)
    el2 = jnp.where(lane == i1, -jnp.inf, el)
    m2 = jnp.max(el2, axis=-1, keepdims=True)
    i2 = jnp.min(jnp.where(el2 == m2, lane, big), axis=-1, keepdims=True)
    e2 = jnp.exp(m2 - m1)
    den = 1.0 + e2
    gates = (jnp.where(lane == i1, (1.0 / den) * p_top, 0.0)
             + jnp.where(lane == i2, (e2 / den) * p_top, 0.0))
    route = jnp.where(lane == ROUTE_GROUP + g_top, 1.0, 0.0)
    for g in range(MOE_GROUPS):
        local = pltpu.roll(gates, LANES - ROUTER_OFF - g * MOE_PER_GROUP, 1)
        route = route + jnp.where((g_top == g) & (lane_i < MOE_PER_GROUP), local, 0.0)
    route_ref[...] = route
    cnt_ref[...] = jnp.broadcast_to(jnp.sum(route, axis=0, keepdims=True), (SUBLANES, LANES)).astype(jnp.int32)


def _output_stage(x, mixes, mods, mod_row, mod_tokens, wo_bf, g2, wr_hi, wr_lo, br, layer):
    tm = OUT_SEQS * S5_SEG
    n = x.shape[0]
    row = lambda w: pl.BlockSpec((tm, w), lambda i: (i, 0))
    const = lambda shape: pl.BlockSpec(shape, lambda i: (0,) * len(shape))
    per_blk = SUBLANES // OUT_SEQS
    return pl.pallas_call(
        _out_kernel,
        grid=(n // tm,),
        in_specs=[row(D_MODEL), row(256), row(256), row(256),
                  pl.BlockSpec((None, S5_SEG, OUT_SEQS * GROUP_WIDTH), lambda i: (i // per_blk, 0, i % per_blk)),
                  _mod_spec(layer, mod_row, mod_tokens // tm, 1),
                  pl.BlockSpec((None, D_MODEL, D_MODEL), lambda i: (layer, 0, 0)), const((1, D_MODEL)),
                  const((D_MODEL, LANES)), const((D_MODEL, LANES)), const((1, LANES))],
        out_specs=[row(D_MODEL), row(D_MODEL), row(LANES),
                   pl.BlockSpec((None, SUBLANES, LANES), lambda i: (i, 0, 0))],
        out_shape=[jax.ShapeDtypeStruct((n, D_MODEL), F32),
                   jax.ShapeDtypeStruct((n, D_MODEL), BF16),
                   jax.ShapeDtypeStruct((n, LANES), F32),
                   jax.ShapeDtypeStruct((n // tm, SUBLANES, LANES), jnp.int32)],
        compiler_params=_cparams("parallel"),
        name="output_stage",
    )(x, *mixes, mods, wo_bf, g2, wr_hi, wr_lo, br)


GROUP_HID = MOE_PER_GROUP * MOE_HIDDEN


MOE_CHUNK = 128


def _moe_kernel(cnt_ref, h2_ref, route_ref, xm_ref, mod_ref, w1_ref, w3_ref, w2_ref, fg_ref, o_ref,
                hs_scr, rs_scr, os_scr, *, final, tm):
    i = pl.program_id(0)
    off1 = cnt_ref[i, 0]
    off2 = off1 + cnt_ref[i, 1]
    off3 = off2 + cnt_ref[i, 2]
    starts = (jnp.int32(0), off1, off2, off3)
    ends = (off1, off2, off3, jnp.int32(tm))

    route = route_ref[...]
    r_hi, r_lo = _split(route)
    pick = (lax.broadcasted_iota(jnp.int32, (SUBLANES, LANES), 1)
            == ROUTE_GROUP + lax.broadcasted_iota(jnp.int32, (SUBLANES, LANES), 0))
    gt = lax.dot_general(jnp.where(pick, 1.0, 0.0).astype(BF16), r_hi, (((1,), (1,)), ((), ())),
                         preferred_element_type=F32)
    before = (lax.broadcasted_iota(jnp.int32, (tm, tm), 0)
              < lax.broadcasted_iota(jnp.int32, (tm, tm), 1))
    rank = jnp.dot(gt.astype(BF16), jnp.where(before, 1.0, 0.0).astype(BF16),
                   preferred_element_type=F32)
    gt_i = gt.astype(jnp.int32)
    rank_i = rank.astype(jnp.int32)
    pos = jnp.zeros((1, tm), jnp.int32)
    for g in range(MOE_GROUPS):
        pos = pos + gt_i[g:g + 1, :] * (rank_i[g:g + 1, :] + starts[g])
    perm = jnp.where(lax.broadcasted_iota(jnp.int32, (tm, tm), 0) == pos, 1.0, 0.0).astype(BF16)
    hs_scr[...] = jnp.dot(perm, h2_ref[...], preferred_element_type=F32).astype(BF16)
    rs_scr[...] = (jnp.dot(perm, r_hi, preferred_element_type=F32)
                   + jnp.dot(perm, r_lo, preferred_element_type=F32))

    for k in range(tm // MOE_CHUNK):
        r0 = k * MOE_CHUNK
        rows = slice(r0, r0 + MOE_CHUNK)
        first = r0
        last = r0 + MOE_CHUNK - 1
        g_lo = sum((first >= s).astype(jnp.int32) for s in starts[1:])
        g_hi = sum((last >= s).astype(jnp.int32) for s in starts[1:])
        x = hs_scr[rows, :]
        gates = rs_scr[rows, :]
        rowid = r0 + lax.broadcasted_iota(jnp.int32, (MOE_CHUNK, 1), 0)
        os_scr[rows, :] = jnp.zeros((MOE_CHUNK, D_MODEL), F32)

        def group_body(g, carry):
            lo = jnp.where(g == 0, starts[0], jnp.where(g == 1, starts[1], jnp.where(g == 2, starts[2], starts[3])))
            hi = jnp.where(g == 0, ends[0], jnp.where(g == 1, ends[1], jnp.where(g == 2, ends[2], ends[3])))
            a = jnp.dot(x, w1_ref[g], preferred_element_type=F32)
            b = jnp.dot(x, w3_ref[g], preferred_element_type=F32)
            hid = []
            for e in range(MOE_PER_GROUP):
                sl = slice(e * MOE_HIDDEN, (e + 1) * MOE_HIDDEN)
                hid.append((jax.nn.silu(a[:, sl]) * b[:, sl] * gates[:, e:e + 1]).astype(BF16))
            y = jnp.dot(jnp.concatenate(hid, axis=1), w2_ref[g], preferred_element_type=F32)
            member = (rowid >= lo) & (rowid < hi)
            os_scr[rows, :] = jnp.where(member, y, os_scr[rows, :])
            return carry

        lax.fori_loop(g_lo, g_hi + 1, group_body, 0)

    o_hi, o_lo = _split(os_scr[...])
    moe = (lax.dot_general(perm, o_hi, (((0,), (0,)), ((), ())), preferred_element_type=F32)
           + lax.dot_general(perm, o_lo, (((0,), (0,)), ((), ())), preferred_element_type=F32))
    out = xm_ref[...] + mod_ref[5:6, :] * moe
    if final:
        out = _rms_rows(out) * fg_ref[...]
    o_ref[...] = out


def _moe_weight_kernel(w1_ref, w3_ref, w2_ref, o1_ref, o3_ref, o2_ref):
    for e in range(MOE_PER_GROUP):
        sl = slice(e * MOE_HIDDEN, (e + 1) * MOE_HIDDEN)
        o1_ref[:, sl] = w1_ref[e].astype(BF16)
        o3_ref[:, sl] = w3_ref[e].astype(BF16)
        o2_ref[sl, :] = w2_ref[e].astype(BF16)


def _moe_weights(w1, w3, w2):
    up = pl.BlockSpec((None, MOE_PER_GROUP, D_MODEL, MOE_HIDDEN), lambda l, g: (l, g, 0, 0))
    down = pl.BlockSpec((None, MOE_PER_GROUP, MOE_HIDDEN, D_MODEL), lambda l, g: (l, g, 0, 0))
    out = pl.BlockSpec((None, None, D_MODEL, GROUP_HID), lambda l, g: (l, g, 0, 0))
    shape = jax.ShapeDtypeStruct((DEPTH, MOE_GROUPS, D_MODEL, GROUP_HID), BF16)
    return pl.pallas_call(
        _moe_weight_kernel,
        grid=(DEPTH, MOE_GROUPS),
        in_specs=[up, up, down],
        out_specs=[out, out, out],
        out_shape=[shape, shape, shape],
        compiler_params=_cparams("parallel", "parallel"),
        name="moe_weights",
    )(w1, w3, w2)


def _moe(h2, route, tile_counts, xm, mods, mod_row, mod_tokens, w1g, w3g, w2g, fg, layer, *, final, tm=512):
    n = h2.shape[0]
    cnt = tile_counts[:, 0, ROUTE_GROUP:ROUTE_GROUP + MOE_GROUPS].reshape(
        n // tm, tm // (OUT_SEQS * S5_SEG), MOE_GROUPS).sum(axis=1)
    row = lambda w: pl.BlockSpec((tm, w), lambda i, c: (i, 0))
    mod_tiles = mod_tokens // tm
    wspec = pl.BlockSpec((None, MOE_GROUPS, D_MODEL, GROUP_HID), lambda i, c: (layer, 0, 0, 0),
                         pipeline_mode=pl.Buffered(1))
    return pl.pallas_call(
        functools.partial(_moe_kernel, final=final, tm=tm),
        grid_spec=pltpu.PrefetchScalarGridSpec(
            num_scalar_prefetch=1,
            grid=(n // tm,),
            in_specs=[row(D_MODEL), row(LANES), row(D_MODEL),
                      pl.BlockSpec((None, None, 6, D_MODEL), lambda i, c: (layer, mod_row + i // mod_tiles, 0, 0)),
                      wspec, wspec, wspec,
                      pl.BlockSpec((1, D_MODEL), lambda i, c: (0, 0))],
            out_specs=row(D_MODEL),
            scratch_shapes=[pltpu.VMEM((tm, D_MODEL), BF16), pltpu.VMEM((tm, LANES), F32),
                            pltpu.VMEM((tm, D_MODEL), F32)]),
        out_shape=jax.ShapeDtypeStruct((n, D_MODEL), F32),
        compiler_params=_cparams("arbitrary"),
        name="moe",
    )(cnt, h2, route, xm, mods, w1g, w3g, w2g, fg)


def kernel(x_prompt, x_sample, cache_a_k, cache_a_v, cache_b_k, cache_b_v, state_ret, state_ssm, c, c_ctx, mod_w, mod_b, norm1_g, norm2_g, w_in, a_qn_g, a_kn_g, b_rel_bias, ret_decay, ret_gn_g, s5_lam_re, s5_lam_im, s5_log_dt, s5_b_re, s5_b_im, s5_c_re, s5_c_im, s5_d, s5_glu_w, w_out, moe_gw, moe_gb, moe_ew, moe_eb, moe_w1, moe_w3, moe_w2, final_norm_g):
    n_ctx = BATCH * SEQ
    n_lat = DEC_BATCH * DEC_SEQ
    lat_seg = DEC_SEQ // S5_SEG

    cond = jnp.zeros((SUBLANES, D_MODEL), F32).at[0].set(c_ctx).at[1:1 + DEC_BATCH].set(c)
    mods = _modulation(cond, mod_w, mod_b).reshape(DEPTH, SUBLANES, 6, D_MODEL)

    rope_tabs = _rope_tables()
    s5_a, s5_bm, s5_cre, s5_cim = _s5_prepare(s5_lam_re, s5_lam_im, s5_log_dt, s5_b_re, s5_b_im,
                                              s5_c_re, s5_c_im)
    cak = cache_a_k.reshape(DEC_BATCH, DEPTH, PAST_LEN, A_KV_HEADS * HEAD_DIM)
    cav = cache_a_v.reshape(DEC_BATCH, DEPTH, PAST_LEN, A_KV_HEADS * HEAD_DIM)
    cbk = cache_b_k.reshape(DEC_BATCH, DEPTH, PAST_LEN, B_HEADS * HEAD_DIM)
    cbv = cache_b_v.reshape(DEC_BATCH, DEPTH, PAST_LEN, B_HEADS * HEAD_DIM)

    xc = x_prompt.reshape(n_ctx, D_MODEL)
    xs = x_sample.reshape(n_lat, D_MODEL)
    w1_all, w3_all, w2_all = _moe_weights(moe_w1, moe_w3, moe_w2)
    eye_h = jnp.eye(C_HEADS, dtype=F32)
    s0_bd = (state_ret[:, :, :, :, :, None, :] * eye_h[None, None, None, :, None, :, None]).reshape(
        DEC_BATCH, DEPTH, 2, C_HEADS * HEAD_DIM, C_HEADS * HEAD_DIM)
    caches = ret_states = ssm_states = None
    h0_zero = jnp.zeros((2, SUBLANES, 2 * S5_SP), F32)
    w_in_bf = w_in.astype(BF16)
    wo_bf = w_out.astype(BF16)
    glu_bf = s5_glu_w.astype(BF16)
    for l in range(DEPTH):
        final = l == DEPTH - 1
        g1 = norm1_g[l].reshape(1, D_MODEL)
        g2 = norm2_g[l].reshape(1, D_MODEL)
        fg = final_norm_g.reshape(1, D_MODEL)
        qn = jnp.tile(a_qn_g[l], A_HEADS).reshape(1, 256)
        kn = jnp.tile(a_kn_g[l], A_KV_HEADS).reshape(1, 128)
        dec = jnp.broadcast_to(ret_decay[l].reshape(2 * C_HEADS, 1), (2 * C_HEADS, LANES))
        gn = ret_gn_g[l].reshape(1, 256)
        dvec = s5_d[l].reshape(1, GROUP_WIDTH)
        wr = jnp.zeros((D_MODEL, LANES), F32).at[:, :MOE_GROUPS].set(moe_gw[l]).at[
            :, ROUTER_OFF:ROUTER_OFF + MOE_EXPERTS].set(moe_ew[l])
        br = jnp.zeros((1, LANES), F32).at[0, :MOE_GROUPS].set(moe_gb[l]).at[
            0, ROUTER_OFF:ROUTER_OFF + MOE_EXPERTS].set(moe_eb[l])
        wr_hi = wr.astype(BF16)
        wr_lo = (wr - wr_hi.astype(F32)).astype(BF16)
        na_bias = _na_bias(b_rel_bias[l])

        zc, cg, du_tm, *caches = _project(xc, mods, 0, n_ctx, g1, w_in_bf, qn, kn, None, l, seq_len=SEQ,
                                          with_cache=True, prev_caches=caches)
        oa, ob = _ctx_attention(zc, BATCH, SEQ)
        oc, ret_states = _retention(zc, cg, dec, gn, None, l, nb=BATCH, seq_len=SEQ, prev_state=ret_states)
        od_tm, ssm_states = _s5(du_tm, h0_zero, s5_a, s5_bm, s5_cre, s5_cim, dvec, glu_bf, l,
                                nseg=1, fin_layer=l, fin_layers=DEPTH, prev_fin=ssm_states)
        xm, h2, route, counts = _output_stage(xc, (oa, ob, oc, od_tm), mods, 0, n_ctx, wo_bf, g2,
                                              wr_hi, wr_lo, br, l)
        xc = _moe(h2, route, counts, xm, mods, 0, n_ctx, w1_all, w3_all, w2_all, fg, l, final=final)

        zs, cg, du_tm = _project(xs, mods, 1, DEC_SEQ, g1, w_in_bf, qn, kn, rope_tabs, l, seq_len=DEC_SEQ)
        oa = _lat_attention_a(zs, cak, cav, l)
        ob = _lat_attention_b(zs, cbk, cbv, na_bias, l)
        oc = _retention(zs, cg, dec, gn, s0_bd, l, nb=DEC_BATCH, seq_len=DEC_SEQ)
        h0 = state_ssm[:, l].reshape(DEC_BATCH, 2, 2 * S5_SP).transpose(1, 0, 2)
        h0_seg = jnp.zeros((2, DEC_BATCH, lat_seg, 2 * S5_SP), F32)
        h0_seg = h0_seg.at[0, :, 0].set(h0[0]).at[1, :, lat_seg - 1].set(h0[1])
        od_tm, _ = _s5(du_tm, h0_seg.reshape(2, SUBLANES, 2 * S5_SP),
                       s5_a, s5_bm, s5_cre, s5_cim, dvec, glu_bf, l, nseg=lat_seg)
        xm, h2, route, counts = _output_stage(xs, (oa, ob, oc, od_tm), mods, 1, DEC_SEQ, wo_bf, g2,
                                              wr_hi, wr_lo, br, l)
        xs = _moe(h2, route, counts, xm, mods, 1, DEC_SEQ, w1_all, w3_all, w2_all, fg, l, final=final)

    new_ak, new_av, new_bk, new_bv = caches
    return (xc.reshape(BATCH, SEQ, D_MODEL), xs.reshape(DEC_BATCH, DEC_SEQ, D_MODEL),
            new_ak.reshape(BATCH, DEPTH, SEQ, A_KV_HEADS, HEAD_DIM),
            new_av.reshape(BATCH, DEPTH, SEQ, A_KV_HEADS, HEAD_DIM),
            new_bk.reshape(BATCH, DEPTH, SEQ, B_HEADS, HEAD_DIM),
            new_bv.reshape(BATCH, DEPTH, SEQ, B_HEADS, HEAD_DIM),
            ret_states,
            ssm_states.reshape(BATCH, DEPTH, 2, 2, S5_GROUPS, S5_STATE))
```

```python
import functools
import math

import numpy as np
import jax
import jax.numpy as jnp
from jax import lax
from jax.experimental import pallas as pl
from jax.experimental.pallas import tpu as pltpu

F32 = jnp.float32
BF16 = jnp.bfloat16

D_MODEL = 1024
BATCH = 32
SEQ = 256
DEPTH = 2
DEC_BATCH = 2
DEC_SEQ = 1024
PAST_LEN = 256
GRID_W = 64
HEAD_DIM = 64
GROUP_WIDTH = 256
A_HEADS = 4
A_KV_HEADS = 2
B_HEADS = 4
NA_ROWS = 8
NA_COLS = 16
C_HEADS = 4
S5_CH = 16
S5_GROUPS = 16
S5_STATE = 64
MOE_GROUPS = 4
MOE_PER_GROUP = 8
MOE_EXPERTS = 32
MOE_HIDDEN = 128
ROPE_THETA = 10000.0
EPS = 1e-6
IN_WIDTH = 2560
Q_SCALE = HEAD_DIM ** -0.5

OFF_AQ, OFF_AK, OFF_AV = 0, 256, 384
OFF_BQ, OFF_BK, OFF_BV = 512, 768, 1024
OFF_CQ, OFF_CK, OFF_CV, OFF_CG = 1280, 1536, 1792, 2048
OFF_DU = 2304

LANES = 128
SUBLANES = 8
BF16_ROWS = 16
S5_SP = S5_GROUPS * S5_STATE
S5_SEG = 256
S5_CHUNK = 256
ROUTER_OFF = 4
NEG_BIG = -1e30
VMEM_LIMIT = 56 * 1024 * 1024


def _cparams(*sem):
    return pltpu.CompilerParams(dimension_semantics=sem, vmem_limit_bytes=VMEM_LIMIT)


def _mod_spec(layer, first_row, tiles_per_row, grid_rank):
    if grid_rank == 1:
        return pl.BlockSpec((None, None, 6, D_MODEL), lambda i: (layer, first_row + i // tiles_per_row, 0, 0))
    return pl.BlockSpec((None, None, 6, D_MODEL), lambda i, g: (layer, first_row + i // tiles_per_row, 0, 0))


def _bdot(a, b):
    return jnp.dot(a.astype(BF16), b.astype(BF16), preferred_element_type=F32)


def _bdot_nt(a, b):
    return lax.dot_general(a.astype(BF16), b.astype(BF16), (((1,), (1,)), ((), ())),
                           preferred_element_type=F32)


def _bdot_tn(a, b):
    return lax.dot_general(a.astype(BF16), b.astype(BF16), (((0,), (0,)), ((), ())),
                           preferred_element_type=F32)


def _split(a):
    hi = a.astype(BF16)
    lo = (a - hi.astype(F32)).astype(BF16)
    return hi, lo


def _dot_hilo_lhs(a, b_bf16):
    hi, lo = _split(a)
    return (jnp.dot(hi, b_bf16, preferred_element_type=F32)
            + jnp.dot(lo, b_bf16, preferred_element_type=F32))


def _rms_rows(x):
    return x * lax.rsqrt(jnp.mean(x * x, axis=-1, keepdims=True) + EPS)


def _mod_kernel(cond_ref, w_ref, b_ref, o_ref):
    o_ref[...] = _bdot(jax.nn.silu(cond_ref[...]), w_ref[...]) + b_ref[...]


def _modulation(cond, mod_w, mod_b):
    tn = 1536
    return pl.pallas_call(
        _mod_kernel,
        grid=(DEPTH, 6 * D_MODEL // tn),
        in_specs=[pl.BlockSpec((SUBLANES, D_MODEL), lambda l, j: (0, 0)),
                  pl.BlockSpec((None, D_MODEL, tn), lambda l, j: (l, 0, j)),
                  pl.BlockSpec((None, 1, tn), lambda l, j: (l, 0, j))],
        out_specs=pl.BlockSpec((None, SUBLANES, tn), lambda l, j: (l, 0, j)),
        out_shape=jax.ShapeDtypeStruct((DEPTH, SUBLANES, 6 * D_MODEL), F32),
        compiler_params=_cparams("arbitrary", "arbitrary"),
        name="modulation",
    )(cond, mod_w, mod_b.reshape(DEPTH, 1, 6 * D_MODEL))


def _group_mean_matrix(w):
    ri = lax.broadcasted_iota(jnp.int32, (w, w), 0) // HEAD_DIM
    ci = lax.broadcasted_iota(jnp.int32, (w, w), 1) // HEAD_DIM
    return jnp.where(ri == ci, 1.0 / HEAD_DIM, 0.0).astype(BF16)


def _head_norm(t, g):
    ms = _dot_hilo_lhs(t * t, _group_mean_matrix(t.shape[1]))
    return t * lax.rsqrt(ms + EPS) * g


def _rope(t, cos, sa, sb):
    return (t * cos + pltpu.roll(t, LANES - 16, 1) * sa + pltpu.roll(t, 16, 1) * sb)


def _store_layer_slot(ref, slot, value):
    for s in range(ref.shape[0]):
        ref[s] = value if s == slot else jnp.zeros_like(value)


def _layer_slot_block(layer, first_call, tail):
    if first_call:
        return (None, DEPTH) + tail, (0,) * (1 + len(tail)), layer
    return (None, 1) + tail, (layer,) + (0,) * len(tail), 0


def _proj_kernel(*refs, rope, n_alias, with_cache, slot):
    x_ref, mod_ref, g1_ref, w_ref, qn_ref, kn_ref = refs[:6]
    n_in = 6
    if rope:
        cos_ref, sa_ref, sb_ref = refs[6:9]
        n_in = 9
    outs = refs[n_in + n_alias:]
    z_ref, cg_ref, du_ref = outs[:3]
    h = _rms_rows(x_ref[...]) * g1_ref[...] * (1.0 + mod_ref[1:2, :]) + mod_ref[0:1, :]
    z = jnp.dot(h.astype(BF16), w_ref[...], preferred_element_type=F32)
    aq = _head_norm(z[:, OFF_AQ:OFF_AK], qn_ref[...])
    ak = _head_norm(z[:, OFF_AK:OFF_AV], kn_ref[...])
    for j in range(3):
        t = aq[:, j * LANES:(j + 1) * LANES] if j < 2 else ak
        if rope:
            cj = 0 if j == 2 else j
            sl = slice(cj * LANES, (cj + 1) * LANES)
            t = _rope(t, cos_ref[:, sl], sa_ref[:, sl], sb_ref[:, sl])
        if j == 2:
            ak = t
        z_ref[:, j * LANES:(j + 1) * LANES] = t.astype(BF16)
    z_ref[:, OFF_AV:OFF_CK] = z[:, OFF_AV:OFF_CK].astype(BF16)
    z_ref[:, OFF_CK:OFF_CV] = (z[:, OFF_CK:OFF_CV] * Q_SCALE).astype(BF16)
    z_ref[:, OFF_CV:OFF_CG] = z[:, OFF_CV:OFF_CG].astype(BF16)
    cg_ref[...] = z[:, OFF_CG:OFF_DU]
    du_ref[...] = z[:, OFF_DU:]
    if with_cache:
        ak_ref, av_ref, bk_ref, bv_ref = outs[3:7]
        _store_layer_slot(ak_ref, slot, ak)
        _store_layer_slot(av_ref, slot, z[:, OFF_AV:OFF_BQ])
        _store_layer_slot(bk_ref, slot, z[:, OFF_BK:OFF_BV])
        _store_layer_slot(bv_ref, slot, z[:, OFF_BV:OFF_CQ])


def _du_spec(grid_rank):
    if grid_rank == 1:
        return pl.BlockSpec((None, S5_SEG, GROUP_WIDTH), lambda i: (i // SUBLANES, 0, i % SUBLANES))
    return pl.BlockSpec((None, S5_SEG, GROUP_WIDTH), lambda i, g: (i // SUBLANES, 0, i % SUBLANES))


def _project(x, mods, mod_row, mod_tokens, g1, w_in_bf, qn, kn, rope_tabs, layer, *, seq_len,
             with_cache=False, prev_caches=None):
    tm = S5_SEG
    n = x.shape[0]
    rope = rope_tabs is not None
    in_specs = [pl.BlockSpec((tm, D_MODEL), lambda i: (i, 0)),
                _mod_spec(layer, mod_row, mod_tokens // tm, 1),
                pl.BlockSpec((1, D_MODEL), lambda i: (0, 0)),
                pl.BlockSpec((None, D_MODEL, IN_WIDTH), lambda i: (layer, 0, 0)),
                pl.BlockSpec((1, 256), lambda i: (0, 0)),
                pl.BlockSpec((1, 128), lambda i: (0, 0))]
    args = [x, mods, g1, w_in_bf, qn, kn]
    if rope:
        tps = seq_len // tm
        in_specs += [pl.BlockSpec((tm, 256), lambda i: (i % tps, 0))] * 3
        args += list(rope_tabs)
    out_specs = [pl.BlockSpec((tm, OFF_CG), lambda i: (i, 0)),
                 pl.BlockSpec((tm, GROUP_WIDTH), lambda i: (i, 0)), _du_spec(1)]
    out_shape = [jax.ShapeDtypeStruct((n, OFF_CG), BF16),
                 jax.ShapeDtypeStruct((n, GROUP_WIDTH), F32),
                 jax.ShapeDtypeStruct((n // (tm * SUBLANES), S5_SEG, SUBLANES * GROUP_WIDTH), F32)]
    aliases = {}
    n_alias = 0
    slot = 0
    if with_cache:
        assert tm == seq_len
        nb = n // seq_len
        for w in (128, 128, 256, 256):
            blk, idx, slot = _layer_slot_block(layer, prev_caches is None, (seq_len, w))
            out_specs.append(pl.BlockSpec(blk, lambda i, idx=idx: (i,) + idx))
            out_shape.append(jax.ShapeDtypeStruct((nb, DEPTH, seq_len, w), F32))
        if prev_caches is not None:
            n_alias = len(prev_caches)
            for k, arr in enumerate(prev_caches):
                aliases[len(args)] = 3 + k
                in_specs.append(pl.BlockSpec(memory_space=pl.ANY))
                args.append(arr)
    return pl.pallas_call(
        functools.partial(_proj_kernel, rope=rope, n_alias=n_alias, with_cache=with_cache, slot=slot),
        grid=(n // tm,),
        in_specs=in_specs,
        out_specs=out_specs,
        out_shape=out_shape,
        input_output_aliases=aliases,
        compiler_params=_cparams("parallel"),
        name="project",
    )(*args)


def _rope_tables():
    t = jnp.arange(DEC_SEQ)
    row = (t // GRID_W).astype(F32)
    col = (t % GRID_W).astype(F32)
    nf = HEAD_DIM // 4
    inv = ROPE_THETA ** (-jnp.arange(nf, dtype=F32) / nf)
    ang_r = row[:, None] * inv[None, :]
    ang_c = col[:, None] * inv[None, :]
    zeros = jnp.zeros_like(ang_r)
    cos = jnp.concatenate([jnp.cos(ang_r), jnp.cos(ang_r), jnp.cos(ang_c), jnp.cos(ang_c)], axis=-1)
    sa = jnp.concatenate([-jnp.sin(ang_r), zeros, -jnp.sin(ang_c), zeros], axis=-1)
    sb = jnp.concatenate([zeros, jnp.sin(ang_r), zeros, jnp.sin(ang_c)], axis=-1)
    return tuple(jnp.tile(a, (1, 4)) for a in (cos, sa, sb))


N_HEADS = 4


def _lane_head(width):
    return lax.broadcasted_iota(jnp.int32, (1, width), 1) // HEAD_DIM


def _stack_heads(q):
    head = _lane_head(q.shape[1])
    return jnp.concatenate([jnp.where(head == h, q, 0.0) for h in range(N_HEADS)], axis=0).astype(BF16)


def _stack_heads_gqa(q):
    lo = lax.broadcasted_iota(jnp.int32, (1, LANES), 1) < HEAD_DIM
    q = q.astype(F32)
    q01, q23 = q[:, :LANES], q[:, LANES:]
    blocks = [jnp.where(lo, q01, 0.0), jnp.where(lo, pltpu.roll(q01, HEAD_DIM, 1), 0.0),
              jnp.where(lo, 0.0, pltpu.roll(q23, HEAD_DIM, 1)), jnp.where(lo, 0.0, q23)]
    return jnp.concatenate(blocks, axis=0).astype(BF16)


def _spread_kv_gqa(v):
    lo = lax.broadcasted_iota(jnp.int32, (1, LANES), 1) < HEAD_DIM
    v = v.astype(F32)
    vr = pltpu.roll(v, HEAD_DIM, 1)
    return jnp.concatenate([jnp.where(lo, v, vr), jnp.where(lo, vr, v)], axis=1)


def _mha(qs, blocks, tq):
    scores = []
    for k, _, bias in blocks:
        s = _bdot_nt(qs, k)
        scores.append(s if bias is None else s + bias)
    m = functools.reduce(jnp.maximum, [jnp.max(s, axis=-1, keepdims=True) for s in scores])
    es = [jnp.exp(s - m) for s in scores]
    denom = functools.reduce(jnp.add, [jnp.sum(e, axis=-1, keepdims=True) for e in es])
    ps = [e.astype(BF16) for e in es]
    head = _lane_head(N_HEADS * HEAD_DIM)
    vals = [v.astype(BF16) for _, v, _ in blocks]
    o = None
    dall = None
    for h in range(N_HEADS):
        rows = slice(h * tq, (h + 1) * tq)
        for p, v in zip(ps, vals):
            t = jnp.dot(p[rows], jnp.where(head == h, v, jnp.zeros_like(v)), preferred_element_type=F32)
            o = t if o is None else o + t
        d = jnp.where(head == h, denom[rows], 0.0)
        dall = d if dall is None else dall + d
    return (o / dall).astype(BF16)


def _ctx_attn_kernel(aq_ref, ak_ref, av_ref, bq_ref, bk_ref, bv_ref, oa_ref, ob_ref):
    tq = aq_ref.shape[0]
    oa_ref[...] = _mha(_stack_heads_gqa(aq_ref[...] * Q_SCALE),
                       [(ak_ref[...], _spread_kv_gqa(av_ref[...]), None)], tq)
    ob_ref[...] = _mha(_stack_heads(bq_ref[...] * Q_SCALE), [(bk_ref[...], bv_ref[...], None)], tq)


def _ctx_attention(z, nb, seq_len):
    def col(width, off):
        return pl.BlockSpec((seq_len, width), lambda b: (b, off // width))
    return pl.pallas_call(
        _ctx_attn_kernel,
        grid=(nb,),
        in_specs=[col(256, OFF_AQ), col(128, OFF_AK), col(128, OFF_AV),
                  col(256, OFF_BQ), col(256, OFF_BK), col(256, OFF_BV)],
        out_specs=[pl.BlockSpec((seq_len, 256), lambda b: (b, 0))] * 2,
        out_shape=[jax.ShapeDtypeStruct((nb * seq_len, 256), BF16)] * 2,
        compiler_params=_cparams("parallel"),
        name="ctx_attention",
    )(z, z, z, z, z, z)


def _lat_attn_a_kernel(q_ref, kn_ref, vn_ref, kc_ref, vc_ref, o_ref):
    o_ref[...] = _mha(_stack_heads_gqa(q_ref[...] * Q_SCALE),
                      [(kc_ref[...], _spread_kv_gqa(vc_ref[...]), None),
                       (kn_ref[...], _spread_kv_gqa(vn_ref[...]), None)], q_ref.shape[0])


def _lat_attention_a(z, cache_k, cache_v, layer, tq=256):
    nq = DEC_SEQ // tq
    cache_spec = pl.BlockSpec((None, None, PAST_LEN, 128), lambda b, j: (b, layer, 0, 0))
    return pl.pallas_call(
        _lat_attn_a_kernel,
        grid=(DEC_BATCH, nq),
        in_specs=[pl.BlockSpec((tq, 256), lambda b, j: (b * nq + j, OFF_AQ // 256)),
                  pl.BlockSpec((DEC_SEQ, 128), lambda b, j: (b, OFF_AK // 128)),
                  pl.BlockSpec((DEC_SEQ, 128), lambda b, j: (b, OFF_AV // 128)),
                  cache_spec, cache_spec],
        out_specs=pl.BlockSpec((tq, 256), lambda b, j: (b * nq + j, 0)),
        out_shape=jax.ShapeDtypeStruct((DEC_BATCH * DEC_SEQ, 256), BF16),
        compiler_params=_cparams("parallel", "parallel"),
        name="lat_attention_a",
    )(z, z, z, cache_k, cache_v)


NA_KEYS = NA_ROWS * GRID_W


NA_PAIRS = 2 * NA_ROWS - 2


NA_STEP_ROWS = 2


def _na_kernel(q_ref, k_ref, v_ref, kc_ref, vc_ref, bias_ref, o_ref):
    rows = DEC_SEQ // GRID_W
    outs = []
    for rr in range(NA_STEP_ROWS):
        r = pl.program_id(1) * NA_STEP_ROWS + rr
        row_start = jnp.clip(r - NA_ROWS // 2, 0, rows - NA_ROWS)
        start = pl.multiple_of(row_start * GRID_W, GRID_W)
        rel0 = row_start - r + NA_ROWS - 1
        kl = k_ref[pl.ds(start, NA_KEYS), :]
        vl = v_ref[pl.ds(start, NA_KEYS), :]
        bias = jnp.concatenate(
            [jnp.concatenate([bias_ref[h, rel0 + 2 * jp] for jp in range(NA_ROWS // 2)], axis=1)
             for h in range(B_HEADS)], axis=0)
        qrows = slice(rr * GRID_W, (rr + 1) * GRID_W)
        outs.append(_mha(_stack_heads(q_ref[qrows, :] * Q_SCALE),
                         [(kl, vl, bias), (kc_ref[...], vc_ref[...], None)], GRID_W))
    o_ref[...] = jnp.concatenate(outs, axis=0)


def _na_bias(rel_bias):
    nrel = 2 * NA_COLS - 1
    period = 2 * GRID_W
    b = rel_bias.astype(F32)
    ext = jnp.concatenate([b[..., NA_COLS - 1:],
                           jnp.zeros(b.shape[:-1] + (period - nrel,), F32),
                           b[..., :NA_COLS - 1]], axis=-1)
    flat = jnp.tile(ext, (1, 1, GRID_W))[..., :GRID_W * (period - 1)]
    toe = flat.reshape(b.shape[:-1] + (GRID_W, period - 1))[..., :GRID_W]
    col_start = np.clip(np.arange(GRID_W) - NA_COLS // 2, 0, GRID_W - NA_COLS)
    kc = np.arange(GRID_W)
    inside = (kc[None, :] >= col_start[:, None]) & (kc[None, :] < col_start[:, None] + NA_COLS)
    toe = jnp.where(jnp.asarray(inside), toe, NEG_BIG)
    return jnp.concatenate([toe[:, :-1], toe[:, 1:]], axis=-1)


def _lat_attention_b(z, cache_k, cache_v, bias, layer):
    rows = DEC_SEQ // GRID_W // NA_STEP_ROWS
    tq = NA_STEP_ROWS * GRID_W
    cache_spec = pl.BlockSpec((None, None, PAST_LEN, 256), lambda b, r: (b, layer, 0, 0))
    return pl.pallas_call(
        _na_kernel,
        grid=(DEC_BATCH, rows),
        in_specs=[pl.BlockSpec((tq, 256), lambda b, r: (b * rows + r, OFF_BQ // 256)),
                  pl.BlockSpec((DEC_SEQ, 256), lambda b, r: (b, OFF_BK // 256)),
                  pl.BlockSpec((DEC_SEQ, 256), lambda b, r: (b, OFF_BV // 256)),
                  cache_spec, cache_spec,
                  pl.BlockSpec((B_HEADS, NA_PAIRS, GRID_W, 2 * GRID_W), lambda b, r: (0, 0, 0, 0))],
        out_specs=pl.BlockSpec((tq, 256), lambda b, r: (b * rows + r, 0)),
        out_shape=jax.ShapeDtypeStruct((DEC_BATCH * DEC_SEQ, 256), BF16),
        compiler_params=_cparams("parallel", "parallel"),
        name="lat_attention_b",
    )(z, z, z, cache_k, cache_v, bias)


def _retention_kernel(q_ref, g_ref, k_ref, v_ref, dec_ref, gn_ref, *rest, seq_len, tq, has_state,
                      hoist_decay, slot):
    if has_state:
        s0_ref, o_ref, dec_scr = rest
    else:
        o_ref, st_ref, dec_scr = rest[-3:]
    head = _lane_head(C_HEADS * HEAD_DIM)
    lg = jax.nn.log_sigmoid(dec_ref[...])

    def per_lane(row0):
        out = jnp.zeros((1, C_HEADS * HEAD_DIM), F32)
        for h in range(C_HEADS):
            out = jnp.where(head == h, lg[row0 + h:row0 + h + 1, 0:1], out)
        return out

    lgf_l, lgb_l = per_lane(0), per_lane(C_HEADS)
    i0 = pl.program_id(1) * tq
    qi = (i0 + lax.broadcasted_iota(jnp.int32, (tq, 1), 0)).astype(F32)

    def fill_decay():
        kj = lax.broadcasted_iota(jnp.int32, (1, seq_len), 1).astype(F32)
        diff = qi - kj
        for h in range(C_HEADS):
            lgf = lg[h:h + 1, 0:1]
            lgb = lg[C_HEADS + h:C_HEADS + h + 1, 0:1]
            dec_scr[h * tq:(h + 1) * tq, :] = (
                jnp.where(diff >= 0, jnp.exp(lgf * jnp.maximum(diff, 0.0)), 0.0)
                + jnp.where(diff <= 0, jnp.exp(lgb * jnp.maximum(-diff, 0.0)), 0.0))

    if hoist_decay:
        pl.when(pl.program_id(0) == 0)(fill_decay)
    else:
        fill_decay()

    q = q_ref[...]
    k = k_ref[...]
    v = v_ref[...].astype(BF16)
    sc = (_bdot_nt(_stack_heads(q), k) * dec_scr[...]).astype(BF16)
    o = None
    for h in range(C_HEADS):
        t = jnp.dot(sc[h * tq:(h + 1) * tq], jnp.where(head == h, v, jnp.zeros_like(v)),
                    preferred_element_type=F32)
        o = t if o is None else o + t
    if has_state:
        o = (o + _bdot(q, s0_ref[0]) * jnp.exp(lgf_l * (qi + 1.0))
             + _bdot(q, s0_ref[1]) * jnp.exp(lgb_l * (seq_len - qi)))
    gm = _group_mean_matrix(C_HEADS * HEAD_DIM)
    dlt = o - _dot_hilo_lhs(o, gm)
    var = _dot_hilo_lhs(dlt * dlt, gm)
    o_ref[...] = (dlt * lax.rsqrt(var + EPS) * gn_ref[...] * jax.nn.silu(g_ref[...])).astype(BF16)
    if not has_state:
        kpos = lax.broadcasted_iota(jnp.int32, (seq_len, 1), 0).astype(F32)
        sf = _bdot_tn(k * jnp.exp(lgf_l * (seq_len - 1.0 - kpos)), v)
        sb = _bdot_tn(k * jnp.exp(lgb_l * kpos), v)
        for s in range(st_ref.shape[0]):
            for h in range(C_HEADS):
                sl = slice(h * HEAD_DIM, (h + 1) * HEAD_DIM)
                st_ref[s, 0, h] = sf[sl, sl] if s == slot else jnp.zeros((HEAD_DIM, HEAD_DIM), F32)
                st_ref[s, 1, h] = sb[sl, sl] if s == slot else jnp.zeros((HEAD_DIM, HEAD_DIM), F32)


def _retention(z, cg, dec, gn, s0, layer, *, nb, seq_len, prev_state=None, tq=256):
    nq = seq_len // tq
    has_state = s0 is not None
    aliases = {}
    slot = 0
    in_specs = [pl.BlockSpec((tq, 256), lambda b, j: (b * nq + j, OFF_CQ // 256)),
                pl.BlockSpec((tq, 256), lambda b, j: (b * nq + j, 0)),
                pl.BlockSpec((seq_len, 256), lambda b, j: (b, OFF_CK // 256)),
                pl.BlockSpec((seq_len, 256), lambda b, j: (b, OFF_CV // 256)),
                pl.BlockSpec((SUBLANES, LANES), lambda b, j: (0, 0)),
                pl.BlockSpec((1, 256), lambda b, j: (0, 0))]
    args = [z, cg, z, z, dec, gn]
    o_spec = pl.BlockSpec((tq, 256), lambda b, j: (b * nq + j, 0))
    o_shape = jax.ShapeDtypeStruct((nb * seq_len, 256), BF16)
    if has_state:
        in_specs.append(pl.BlockSpec((None, None, 2, 256, 256), lambda b, j: (b, layer, 0, 0, 0)))
        args.append(s0)
        out_specs, out_shape = o_spec, o_shape
    else:
        assert nq == 1
        blk, idx, slot = _layer_slot_block(layer, prev_state is None, (2, C_HEADS, HEAD_DIM, HEAD_DIM))
        out_specs = [o_spec, pl.BlockSpec(blk, lambda b, j: (b,) + idx)]
        out_shape = [o_shape, jax.ShapeDtypeStruct((nb, DEPTH, 2, C_HEADS, HEAD_DIM, HEAD_DIM), F32)]
        if prev_state is not None:
            aliases[len(args)] = 1
            in_specs.append(pl.BlockSpec(memory_space=pl.ANY))
            args.append(prev_state)
    return pl.pallas_call(
        functools.partial(_retention_kernel, seq_len=seq_len, tq=tq, has_state=has_state,
                          hoist_decay=nq == 1, slot=slot),
        grid=(nb, nq),
        in_specs=in_specs,
        out_specs=out_specs,
        out_shape=out_shape,
        scratch_shapes=[pltpu.VMEM((C_HEADS * tq, seq_len), F32)],
        input_output_aliases=aliases,
        compiler_params=_cparams("arbitrary", "arbitrary"),
        name="retention",
    )(*args)


def _s5_prep_kernel(lre_ref, lim_ref, ldt_ref, bre_ref, bim_ref, cre_ref, cim_ref,
                    a_ref, bm_ref, cro_ref, cio_ref, bm_scr, cr_scr, ci_scr):
    lre = lre_ref[...]
    lim = lim_ref[...]
    dt = jnp.exp(ldt_ref[...])
    mag = jnp.exp(lre * dt)
    a_re = mag * jnp.cos(lim * dt)
    a_im = mag * jnp.sin(lim * dt)
    den = lre * lre + lim * lim
    r_re = ((a_re - 1.0) * lre + a_im * lim) / den
    r_im = (a_im * lre - (a_re - 1.0) * lim) / den
    bm_scr[...] = jnp.zeros_like(bm_scr)
    cr_scr[...] = jnp.zeros_like(cr_scr)
    ci_scr[...] = jnp.zeros_like(ci_scr)
    for g in range(S5_GROUPS):
        rows = slice(g * S5_CH, (g + 1) * S5_CH)
        cols = slice(g * S5_STATE, (g + 1) * S5_STATE)
        a_ref[0:1, cols] = a_re[g:g + 1, :]
        a_ref[1:2, cols] = a_im[g:g + 1, :]
        rr, ri = r_re[g:g + 1, :], r_im[g:g + 1, :]
        br, bi = bre_ref[g], bim_ref[g]
        bm_scr[rows, cols] = rr * br - ri * bi
        bm_scr[rows, S5_SP + g * S5_STATE:S5_SP + (g + 1) * S5_STATE] = rr * bi + ri * br
        cr_scr[cols, rows] = cre_ref[g]
        ci_scr[cols, rows] = cim_ref[g]
    bm_ref[...] = bm_scr[...].astype(BF16)
    cro_ref[...] = cr_scr[...].astype(BF16)
    cio_ref[...] = ci_scr[...].astype(BF16)


def _s5_prepare(lam_re, lam_im, log_dt, b_re, b_im, c_re, c_im):
    gp = (S5_GROUPS, S5_STATE)
    ldt = jnp.broadcast_to(log_dt[..., None], (DEPTH, 2) + gp)
    bt = [jnp.swapaxes(t, -1, -2) for t in (b_re, b_im)]
    ct = [jnp.swapaxes(t, -1, -2) for t in (c_re, c_im)]

    def spec(*tail):
        return pl.BlockSpec((None, None) + tail, lambda l, d: (l, d) + (0,) * len(tail))

    return pl.pallas_call(
        _s5_prep_kernel,
        grid=(DEPTH, 2),
        in_specs=[spec(*gp)] * 3 + [spec(S5_GROUPS, S5_CH, S5_STATE)] * 2 + [spec(S5_GROUPS, S5_STATE, S5_CH)] * 2,
        out_specs=[spec(2, S5_SP), spec(GROUP_WIDTH, 2 * S5_SP), spec(S5_SP, GROUP_WIDTH), spec(S5_SP, GROUP_WIDTH)],
        out_shape=[jax.ShapeDtypeStruct((DEPTH, 2, 2, S5_SP), F32),
                   jax.ShapeDtypeStruct((DEPTH, 2, GROUP_WIDTH, 2 * S5_SP), BF16),
                   jax.ShapeDtypeStruct((DEPTH, 2, S5_SP, GROUP_WIDTH), BF16),
                   jax.ShapeDtypeStruct((DEPTH, 2, S5_SP, GROUP_WIDTH), BF16)],
        scratch_shapes=[pltpu.VMEM((GROUP_WIDTH, 2 * S5_SP), F32), pltpu.VMEM((S5_SP, GROUP_WIDTH), F32),
                        pltpu.VMEM((S5_SP, GROUP_WIDTH), F32)],
        compiler_params=_cparams("parallel", "parallel"),
        name="s5_prepare",
    )(lam_re, lam_im, ldt, bt[0], bt[1], ct[0], ct[1])


def _cmul(ar, ai, br, bi):
    return ar * br - ai * bi, ar * bi + ai * br


def _s5_kernel(u_ref, h0_ref, a_ref, bm_ref, cre_ref, cim_ref, dvec_ref, glu_ref, *rest, nseg, slot):
    od_ref, fin_ref, x_scr, s_scr, y_scr = rest[-5:]
    steps = S5_SEG
    rows = steps * SUBLANES
    chunk = S5_CHUNK
    chunk_steps = chunk // SUBLANES
    nchunk = rows // chunk
    seg = lax.broadcasted_iota(jnp.int32, (SUBLANES, S5_SP), 0) % nseg

    for d in range(2):
        ar = jnp.broadcast_to(a_ref[d, 0:1, :], (SUBLANES, S5_SP))
        ai = jnp.broadcast_to(a_ref[d, 1:2, :], (SUBLANES, S5_SP))

        def row0(k):
            c = k if d == 0 else nchunk - 1 - k
            return c * chunk if isinstance(c, int) else pl.multiple_of(c * chunk, chunk)

        def input_part(k, buf):
            x_scr[buf] = jnp.dot(u_ref[pl.ds(row0(k), chunk), :].astype(BF16), bm_ref[d],
                                 preferred_element_type=F32)

        def scan_part(buf, carry, store):
            sr, si = carry
            for t in range(chunk_steps):
                r = (t if d == 0 else chunk_steps - 1 - t) * SUBLANES
                pr, pi = _cmul(ar, ai, sr, si)
                sr = pr + x_scr[buf, r:r + SUBLANES, 0:S5_SP]
                si = pi + x_scr[buf, r:r + SUBLANES, S5_SP:]
                if store:
                    s_scr[buf, r:r + SUBLANES, 0:S5_SP] = sr
                    s_scr[buf, r:r + SUBLANES, S5_SP:] = si
            return sr, si

        def output_part(k, buf):
            y = _bdot(s_scr[buf, :, 0:S5_SP], cre_ref[d]) - _bdot(s_scr[buf, :, S5_SP:], cim_ref[d])
            if d == 0:
                y_scr[pl.ds(row0(k), chunk), :] = y
            else:
                y_scr[pl.ds(row0(k), chunk), :] += y

        def half(k, buf, carry, store, nxt=True, prev=True):
            if nxt:
                input_part(k + 1, 1 - buf)
            carry = scan_part(buf, carry, store)
            if store and prev:
                output_part(k - 1, 1 - buf)
            return carry

        def run_pass(carry, store):
            input_part(0, 0)
            carry = half(0, 0, carry, store, prev=False)
            carry = half(1, 1, carry, store)

            def pair(j, c):
                c = half(2 * j, 0, c, store)
                return half(2 * j + 1, 1, c, store)
            carry = lax.fori_loop(1, nchunk // 2 - 1, pair, carry)
            carry = half(nchunk - 2, 0, carry, store)
            carry = half(nchunk - 1, 1, carry, store, nxt=False)
            if store:
                output_part(nchunk - 1, 1)
            return carry

        init = (h0_ref[d, :, 0:S5_SP], h0_ref[d, :, S5_SP:])
        if nseg > 1:
            zero = jnp.zeros((SUBLANES, S5_SP), F32)
            fr, fi = run_pass((zero, zero), store=False)
            pr, pi = ar, ai
            for _ in range(int(math.log2(steps))):
                pr, pi = _cmul(pr, pi, pr, pi)
            cr, ci = init
            shift = 1 if d == 0 else SUBLANES - 1
            order = range(1, nseg) if d == 0 else range(nseg - 2, -1, -1)
            for s in order:
                ncr, nci = pltpu.roll(cr, shift, 0), pltpu.roll(ci, shift, 0)
                nfr, nfi = pltpu.roll(fr, shift, 0), pltpu.roll(fi, shift, 0)
                qr, qi = _cmul(pr, pi, ncr, nci)
                cr = jnp.where(seg == s, qr + nfr, cr)
                ci = jnp.where(seg == s, qi + nfi, ci)
            init = (cr, ci)
        sr, si = run_pass(init, store=True)
        for s in range(fin_ref.shape[1] // (4 * S5_SP)):
            base = (4 * s + 2 * d) * S5_SP
            fin_ref[:, base:base + S5_SP] = sr if s == slot else jnp.zeros_like(sr)
            fin_ref[:, base + S5_SP:base + 2 * S5_SP] = si if s == slot else jnp.zeros_like(si)

    def obody(c, carry):
        r0 = pl.multiple_of(c * chunk, chunk)
        y = y_scr[pl.ds(r0, chunk), :] + dvec_ref[...] * u_ref[pl.ds(r0, chunk), :]
        zz = jax.nn.gelu(y)
        od_ref[pl.ds(r0, chunk), :] = (zz * jax.nn.sigmoid(_bdot(zz, glu_ref[...]))).astype(BF16)
        return carry
    lax.fori_loop(0, nchunk, obody, 0)


def _s5(du_tm, h0, a, bmat, cre, cim, dvec, glu_bf, layer, *, nseg, fin_layer=0, fin_layers=1,
        prev_fin=None):
    nblk = du_tm.shape[0]
    rows = S5_SEG * SUBLANES
    fin_w = 4 * S5_SP
    in_specs = [pl.BlockSpec((None, rows, GROUP_WIDTH), lambda i: (i, 0, 0)),
                pl.BlockSpec((2, SUBLANES, 2 * S5_SP), lambda i: (0, 0, 0)),
                pl.BlockSpec((None, 2, 2, S5_SP), lambda i: (layer, 0, 0, 0)),
                pl.BlockSpec((None, 2, GROUP_WIDTH, 2 * S5_SP), lambda i: (layer, 0, 0, 0)),
                pl.BlockSpec((None, 2, S5_SP, GROUP_WIDTH), lambda i: (layer, 0, 0, 0)),
                pl.BlockSpec((None, 2, S5_SP, GROUP_WIDTH), lambda i: (layer, 0, 0, 0)),
                pl.BlockSpec((1, GROUP_WIDTH), lambda i: (0, 0)),
                pl.BlockSpec((None, GROUP_WIDTH, GROUP_WIDTH), lambda i: (layer, 0, 0))]
    args = [du_tm.reshape(nblk, rows, GROUP_WIDTH), h0, a, bmat, cre, cim, dvec, glu_bf]
    aliases = {}
    if prev_fin is not None:
        aliases[len(args)] = 1
        in_specs.append(pl.BlockSpec(memory_space=pl.ANY))
        args.append(prev_fin)
        fin_spec, slot = pl.BlockSpec((SUBLANES, fin_w), lambda i: (i, fin_layer)), 0
    else:
        fin_spec, slot = pl.BlockSpec((SUBLANES, fin_layers * fin_w), lambda i: (i, 0)), fin_layer
    od, fin = pl.pallas_call(
        functools.partial(_s5_kernel, nseg=nseg, slot=slot),
        grid=(nblk,),
        in_specs=in_specs,
        out_specs=[pl.BlockSpec((None, rows, GROUP_WIDTH), lambda i: (i, 0, 0)), fin_spec],
        out_shape=[jax.ShapeDtypeStruct((nblk, rows, GROUP_WIDTH), BF16),
                   jax.ShapeDtypeStruct((nblk * SUBLANES, fin_layers * fin_w), F32)],
        scratch_shapes=[pltpu.VMEM((2, S5_CHUNK, 2 * S5_SP), F32), pltpu.VMEM((2, S5_CHUNK, 2 * S5_SP), F32),
                        pltpu.VMEM((rows, GROUP_WIDTH), F32)],
        input_output_aliases=aliases,
        compiler_params=_cparams("parallel"),
        name="s5",
    )(*args)
    return od.reshape(nblk, S5_SEG, SUBLANES * GROUP_WIDTH), fin


ROUTE_GROUP = MOE_PER_GROUP
OUT_SEQS = 2


def _out_kernel(x_ref, oa_ref, ob_ref, oc_ref, od_ref, mod_ref, wo_ref, g2_ref, wrh_ref, wrl_ref, br_ref,
                xm_ref, h2_ref, route_ref, cnt_ref):
    od = jnp.concatenate([od_ref[:, s * GROUP_WIDTH:(s + 1) * GROUP_WIDTH] for s in range(OUT_SEQS)], axis=0)
    mix = functools.reduce(jnp.add, [
        _bdot(o, wo_ref[i * GROUP_WIDTH:(i + 1) * GROUP_WIDTH, :])
        for i, o in enumerate((oa_ref[...], ob_ref[...], oc_ref[...], od))])
    xm = x_ref[...] + mod_ref[2:3, :] * mix
    xm_ref[...] = xm
    h2 = _rms_rows(xm) * g2_ref[...] * (1.0 + mod_ref[4:5, :]) + mod_ref[3:4, :]
    h2_ref[...] = h2.astype(BF16)

    h_hi, h_lo = _split(h2)
    logits = (jnp.dot(h_hi, wrh_ref[...], preferred_element_type=F32)
              + jnp.dot(h_hi, wrl_ref[...], preferred_element_type=F32)
              + jnp.dot(h_lo, wrh_ref[...], preferred_element_type=F32)) + br_ref[...]
    lane_i = lax.broadcasted_iota(jnp.int32, logits.shape, 1)
    lane = lane_i.astype(F32)
    big = jnp.float32(2 ** 30)
    gmask = lane_i < MOE_GROUPS
    gl = jnp.where(gmask, logits, -jnp.inf)
    gmax = jnp.max(gl, axis=-1, keepdims=True)
    p_top = 1.0 / jnp.sum(jnp.exp(gl - gmax), axis=-1, keepdims=True)
    g_top = jnp.min(jnp.where(gl == gmax, lane, big), axis=-1, keepdims=True)
    e_lane = lane_i - ROUTER_OFF
    lane_group = (e_lane // MOE_PER_GROUP).astype(F32)
    emask = (e_lane >= 0) & (e_lane < MOE_EXPERTS) & (lane_group == g_top)
    el = jnp.where(emask, logits, -jnp.inf)
    m1 = jnp.max(el, axis=-1, keepdims=True)
    i1 = jnp.min(jnp.where(el == m1, lane, big), axis=-1, keepdims=True)
    el2 = jnp.where(lane == i1, -jnp.inf, el)
    m2 = jnp.max(el2, axis=-1, keepdims=True)
    i2 = jnp.min(jnp.where(el2 == m2, lane, big), axis=-1, keepdims=True)
    e2 = jnp.exp(m2 - m1)
    den = 1.0 + e2
    gates = (jnp.where(lane == i1, (1.0 / den) * p_top, 0.0)
             + jnp.where(lane == i2, (e2 / den) * p_top, 0.0))
    route = jnp.where(lane == ROUTE_GROUP + g_top, 1.0, 0.0)
    for g in range(MOE_GROUPS):
        local = pltpu.roll(gates, LANES - ROUTER_OFF - g * MOE_PER_GROUP, 1)
        route = route + jnp.where((g_top == g) & (lane_i < MOE_PER_GROUP), local, 0.0)
    route_ref[...] = route
    cnt_ref[...] = jnp.broadcast_to(jnp.sum(route, axis=0, keepdims=True), (SUBLANES, LANES)).astype(jnp.int32)


def _output_stage(x, mixes, mods, mod_row, mod_tokens, wo_bf, g2, wr_hi, wr_lo, br, layer):
    tm = OUT_SEQS * S5_SEG
    n = x.shape[0]
    row = lambda w: pl.BlockSpec((tm, w), lambda i: (i, 0))
    const = lambda shape: pl.BlockSpec(shape, lambda i: (0,) * len(shape))
    per_blk = SUBLANES // OUT_SEQS
    return pl.pallas_call(
        _out_kernel,
        grid=(n // tm,),
        in_specs=[row(D_MODEL), row(256), row(256), row(256),
                  pl.BlockSpec((None, S5_SEG, OUT_SEQS * GROUP_WIDTH), lambda i: (i // per_blk, 0, i % per_blk)),
                  _mod_spec(layer, mod_row, mod_tokens // tm, 1),
                  pl.BlockSpec((None, D_MODEL, D_MODEL), lambda i: (layer, 0, 0)), const((1, D_MODEL)),
                  const((D_MODEL, LANES)), const((D_MODEL, LANES)), const((1, LANES))],
        out_specs=[row(D_MODEL), row(D_MODEL), row(LANES),
                   pl.BlockSpec((None, SUBLANES, LANES), lambda i: (i, 0, 0))],
        out_shape=[jax.ShapeDtypeStruct((n, D_MODEL), F32),
                   jax.ShapeDtypeStruct((n, D_MODEL), BF16),
                   jax.ShapeDtypeStruct((n, LANES), F32),
                   jax.ShapeDtypeStruct((n // tm, SUBLANES, LANES), jnp.int32)],
        compiler_params=_cparams("parallel"),
        name="output_stage",
    )(x, *mixes, mods, wo_bf, g2, wr_hi, wr_lo, br)


GROUP_HID = MOE_PER_GROUP * MOE_HIDDEN


MOE_CHUNK = 128


def _moe_kernel(cnt_ref, h2_ref, route_ref, xm_ref, mod_ref, w1_ref, w3_ref, w2_ref, fg_ref, o_ref,
                hs_scr, rs_scr, os_scr, before_scr, *, final, tm):
    i = pl.program_id(0)
    off1 = cnt_ref[i, 0]
    off2 = off1 + cnt_ref[i, 1]
    off3 = off2 + cnt_ref[i, 2]
    starts = (jnp.int32(0), off1, off2, off3)
    ends = (off1, off2, off3, jnp.int32(tm))

    route = route_ref[...]
    r_hi, r_lo = _split(route)
    pick = (lax.broadcasted_iota(jnp.int32, (SUBLANES, LANES), 1)
            == ROUTE_GROUP + lax.broadcasted_iota(jnp.int32, (SUBLANES, LANES), 0))
    gt = lax.dot_general(jnp.where(pick, 1.0, 0.0).astype(BF16), r_hi, (((1,), (1,)), ((), ())),
                         preferred_element_type=F32)
    @pl.when(i == 0)
    def _():
        before_scr[...] = jnp.where(lax.broadcasted_iota(jnp.int32, (tm, tm), 0)
                                    < lax.broadcasted_iota(jnp.int32, (tm, tm), 1), 1.0, 0.0).astype(BF16)

    rank = jnp.dot(gt.astype(BF16), before_scr[...], preferred_element_type=F32)
    gt_i = gt.astype(jnp.int32)
    rank_i = rank.astype(jnp.int32)
    pos = jnp.zeros((1, tm), jnp.int32)
    for g in range(MOE_GROUPS):
        pos = pos + gt_i[g:g + 1, :] * (rank_i[g:g + 1, :] + starts[g])
    perm = jnp.where(lax.broadcasted_iota(jnp.int32, (tm, tm), 0) == pos, 1.0, 0.0).astype(BF16)
    hs_scr[...] = jnp.dot(perm, h2_ref[...], preferred_element_type=F32).astype(BF16)
    rs_scr[...] = (jnp.dot(perm, r_hi, preferred_element_type=F32)
                   + jnp.dot(perm, r_lo, preferred_element_type=F32))

    os_scr[...] = jnp.zeros_like(os_scr)
    for g in range(MOE_GROUPS):
        lo, hi = starts[g], ends[g]
        base = (lo // BF16_ROWS) * BF16_ROWS
        n_chunks = jnp.where(hi > lo, (hi - base + MOE_CHUNK - 1) // MOE_CHUNK, 0)

        def chunk_body(c, carry, g=g, lo=lo, hi=hi, base=base):
            r0 = pl.multiple_of(jnp.minimum(base + c * MOE_CHUNK, tm - MOE_CHUNK), BF16_ROWS)
            rows = pl.ds(r0, MOE_CHUNK)
            x = hs_scr[rows, :]
            gates = rs_scr[rows, :]
            a = jnp.dot(x, w1_ref[g], preferred_element_type=F32)
            b = jnp.dot(x, w3_ref[g], preferred_element_type=F32)
            hid = []
            for e in range(MOE_PER_GROUP):
                sl = slice(e * MOE_HIDDEN, (e + 1) * MOE_HIDDEN)
                hid.append((jax.nn.silu(a[:, sl]) * b[:, sl] * gates[:, e:e + 1]).astype(BF16))
            y = jnp.dot(jnp.concatenate(hid, axis=1), w2_ref[g], preferred_element_type=F32)
            rowid = r0 + lax.broadcasted_iota(jnp.int32, (MOE_CHUNK, 1), 0)
            member = (rowid >= lo) & (rowid < hi)
            os_scr[rows, :] = jnp.where(member, y, os_scr[rows, :])
            return carry

        lax.fori_loop(0, n_chunks, chunk_body, 0)

    o_hi, o_lo = _split(os_scr[...])
    moe = (lax.dot_general(perm, o_hi, (((0,), (0,)), ((), ())), preferred_element_type=F32)
           + lax.dot_general(perm, o_lo, (((0,), (0,)), ((), ())), preferred_element_type=F32))
    out = xm_ref[...] + mod_ref[5:6, :] * moe
    if final:
        out = _rms_rows(out) * fg_ref[...]
    o_ref[...] = out


def _moe_weight_kernel(w1_ref, w3_ref, w2_ref, o1_ref, o3_ref, o2_ref):
    for e in range(MOE_PER_GROUP):
        sl = slice(e * MOE_HIDDEN, (e + 1) * MOE_HIDDEN)
        o1_ref[:, sl] = w1_ref[e].astype(BF16)
        o3_ref[:, sl] = w3_ref[e].astype(BF16)
        o2_ref[sl, :] = w2_ref[e].astype(BF16)


def _moe_weights(w1, w3, w2):
    up = pl.BlockSpec((None, MOE_PER_GROUP, D_MODEL, MOE_HIDDEN), lambda l, g: (l, g, 0, 0))
    down = pl.BlockSpec((None, MOE_PER_GROUP, MOE_HIDDEN, D_MODEL), lambda l, g: (l, g, 0, 0))
    out = pl.BlockSpec((None, None, D_MODEL, GROUP_HID), lambda l, g: (l, g, 0, 0))
    shape = jax.ShapeDtypeStruct((DEPTH, MOE_GROUPS, D_MODEL, GROUP_HID), BF16)
    return pl.pallas_call(
        _moe_weight_kernel,
        grid=(DEPTH, MOE_GROUPS),
        in_specs=[up, up, down],
        out_specs=[out, out, out],
        out_shape=[shape, shape, shape],
        compiler_params=_cparams("parallel", "parallel"),
        name="moe_weights",
    )(w1, w3, w2)


def _moe(h2, route, tile_counts, xm, mods, mod_row, mod_tokens, w1g, w3g, w2g, fg, layer, *, final, tm=512):
    n = h2.shape[0]
    cnt = tile_counts[:, 0, ROUTE_GROUP:ROUTE_GROUP + MOE_GROUPS].reshape(
        n // tm, tm // (OUT_SEQS * S5_SEG), MOE_GROUPS).sum(axis=1)
    row = lambda w: pl.BlockSpec((tm, w), lambda i, c: (i, 0))
    mod_tiles = mod_tokens // tm
    wspec = pl.BlockSpec((None, MOE_GROUPS, D_MODEL, GROUP_HID), lambda i, c: (layer, 0, 0, 0),
                         pipeline_mode=pl.Buffered(1))
    return pl.pallas_call(
        functools.partial(_moe_kernel, final=final, tm=tm),
        grid_spec=pltpu.PrefetchScalarGridSpec(
            num_scalar_prefetch=1,
            grid=(n // tm,),
            in_specs=[row(D_MODEL), row(LANES), row(D_MODEL),
                      pl.BlockSpec((None, None, 6, D_MODEL), lambda i, c: (layer, mod_row + i // mod_tiles, 0, 0)),
                      wspec, wspec, wspec,
                      pl.BlockSpec((1, D_MODEL), lambda i, c: (0, 0))],
            out_specs=row(D_MODEL),
            scratch_shapes=[pltpu.VMEM((tm, D_MODEL), BF16), pltpu.VMEM((tm, LANES), F32),
                            pltpu.VMEM((tm, D_MODEL), F32), pltpu.VMEM((tm, tm), BF16)]),
        out_shape=jax.ShapeDtypeStruct((n, D_MODEL), F32),
        compiler_params=_cparams("arbitrary"),
        name="moe",
    )(cnt, h2, route, xm, mods, w1g, w3g, w2g, fg)


def kernel(x_prompt, x_sample, cache_a_k, cache_a_v, cache_b_k, cache_b_v, state_ret, state_ssm, c, c_ctx, mod_w, mod_b, norm1_g, norm2_g, w_in, a_qn_g, a_kn_g, b_rel_bias, ret_decay, ret_gn_g, s5_lam_re, s5_lam_im, s5_log_dt, s5_b_re, s5_b_im, s5_c_re, s5_c_im, s5_d, s5_glu_w, w_out, moe_gw, moe_gb, moe_ew, moe_eb, moe_w1, moe_w3, moe_w2, final_norm_g):
    n_ctx = BATCH * SEQ
    n_lat = DEC_BATCH * DEC_SEQ
    lat_seg = DEC_SEQ // S5_SEG

    cond = jnp.zeros((SUBLANES, D_MODEL), F32).at[0].set(c_ctx).at[1:1 + DEC_BATCH].set(c)
    mods = _modulation(cond, mod_w, mod_b).reshape(DEPTH, SUBLANES, 6, D_MODEL)

    rope_tabs = _rope_tables()
    s5_a, s5_bm, s5_cre, s5_cim = _s5_prepare(s5_lam_re, s5_lam_im, s5_log_dt, s5_b_re, s5_b_im,
                                              s5_c_re, s5_c_im)
    cak = cache_a_k.reshape(DEC_BATCH, DEPTH, PAST_LEN, A_KV_HEADS * HEAD_DIM)
    cav = cache_a_v.reshape(DEC_BATCH, DEPTH, PAST_LEN, A_KV_HEADS * HEAD_DIM)
    cbk = cache_b_k.reshape(DEC_BATCH, DEPTH, PAST_LEN, B_HEADS * HEAD_DIM)
    cbv = cache_b_v.reshape(DEC_BATCH, DEPTH, PAST_LEN, B_HEADS * HEAD_DIM)

    xc = x_prompt.reshape(n_ctx, D_MODEL)
    xs = x_sample.reshape(n_lat, D_MODEL)
    w1_all, w3_all, w2_all = _moe_weights(moe_w1, moe_w3, moe_w2)
    eye_h = jnp.eye(C_HEADS, dtype=F32)
    s0_bd = (state_ret[:, :, :, :, :, None, :] * eye_h[None, None, None, :, None, :, None]).reshape(
        DEC_BATCH, DEPTH, 2, C_HEADS * HEAD_DIM, C_HEADS * HEAD_DIM)
    caches = ret_states = ssm_states = None
    h0_zero = jnp.zeros((2, SUBLANES, 2 * S5_SP), F32)
    w_in_bf = w_in.astype(BF16)
    wo_bf = w_out.astype(BF16)
    glu_bf = s5_glu_w.astype(BF16)
    for l in range(DEPTH):
        final = l == DEPTH - 1
        g1 = norm1_g[l].reshape(1, D_MODEL)
        g2 = norm2_g[l].reshape(1, D_MODEL)
        fg = final_norm_g.reshape(1, D_MODEL)
        qn = jnp.tile(a_qn_g[l], A_HEADS).reshape(1, 256)
        kn = jnp.tile(a_kn_g[l], A_KV_HEADS).reshape(1, 128)
        dec = jnp.broadcast_to(ret_decay[l].reshape(2 * C_HEADS, 1), (2 * C_HEADS, LANES))
        gn = ret_gn_g[l].reshape(1, 256)
        dvec = s5_d[l].reshape(1, GROUP_WIDTH)
        wr = jnp.zeros((D_MODEL, LANES), F32).at[:, :MOE_GROUPS].set(moe_gw[l]).at[
            :, ROUTER_OFF:ROUTER_OFF + MOE_EXPERTS].set(moe_ew[l])
        br = jnp.zeros((1, LANES), F32).at[0, :MOE_GROUPS].set(moe_gb[l]).at[
            0, ROUTER_OFF:ROUTER_OFF + MOE_EXPERTS].set(moe_eb[l])
        wr_hi = wr.astype(BF16)
        wr_lo = (wr - wr_hi.astype(F32)).astype(BF16)
        na_bias = _na_bias(b_rel_bias[l])

        zc, cg, du_tm, *caches = _project(xc, mods, 0, n_ctx, g1, w_in_bf, qn, kn, None, l, seq_len=SEQ,
                                          with_cache=True, prev_caches=caches)
        oa, ob = _ctx_attention(zc, BATCH, SEQ)
        oc, ret_states = _retention(zc, cg, dec, gn, None, l, nb=BATCH, seq_len=SEQ, prev_state=ret_states)
        od_tm, ssm_states = _s5(du_tm, h0_zero, s5_a, s5_bm, s5_cre, s5_cim, dvec, glu_bf, l,
                                nseg=1, fin_layer=l, fin_layers=DEPTH, prev_fin=ssm_states)
        xm, h2, route, counts = _output_stage(xc, (oa, ob, oc, od_tm), mods, 0, n_ctx, wo_bf, g2,
                                              wr_hi, wr_lo, br, l)
        xc = _moe(h2, route, counts, xm, mods, 0, n_ctx, w1_all, w3_all, w2_all, fg, l, final=final)

        zs, cg, du_tm = _project(xs, mods, 1, DEC_SEQ, g1, w_in_bf, qn, kn, rope_tabs, l, seq_len=DEC_SEQ)
        oa = _lat_attention_a(zs, cak, cav, l)
        ob = _lat_attention_b(zs, cbk, cbv, na_bias, l)
        oc = _retention(zs, cg, dec, gn, s0_bd, l, nb=DEC_BATCH, seq_len=DEC_SEQ)
        h0 = state_ssm[:, l].reshape(DEC_BATCH, 2, 2 * S5_SP).transpose(1, 0, 2)
        h0_seg = jnp.zeros((2, DEC_BATCH, lat_seg, 2 * S5_SP), F32)
        h0_seg = h0_seg.at[0, :, 0].set(h0[0]).at[1, :, lat_seg - 1].set(h0[1])
        od_tm, _ = _s5(du_tm, h0_seg.reshape(2, SUBLANES, 2 * S5_SP),
                       s5_a, s5_bm, s5_cre, s5_cim, dvec, glu_bf, l, nseg=lat_seg)
        xm, h2, route, counts = _output_stage(xs, (oa, ob, oc, od_tm), mods, 1, DEC_SEQ, wo_bf, g2,
                                              wr_hi, wr_lo, br, l)
        xs = _moe(h2, route, counts, xm, mods, 1, DEC_SEQ, w1_all, w3_all, w2_all, fg, l, final=final)

    new_ak, new_av, new_bk, new_bv = caches
    return (xc.reshape(BATCH, SEQ, D_MODEL), xs.reshape(DEC_BATCH, DEC_SEQ, D_MODEL),
            new_ak.reshape(BATCH, DEPTH, SEQ, A_KV_HEADS, HEAD_DIM),
            new_av.reshape(BATCH, DEPTH, SEQ, A_KV_HEADS, HEAD_DIM),
            new_bk.reshape(BATCH, DEPTH, SEQ, B_HEADS, HEAD_DIM),
            new_bv.reshape(BATCH, DEPTH, SEQ, B_HEADS, HEAD_DIM),
            ret_states,
            ssm_states.reshape(BATCH, DEPTH, 2, 2, S5_GROUPS, S5_STATE))
```

```python
import functools
import math

import numpy as np
import jax
import jax.numpy as jnp
from jax import lax
from jax.experimental import pallas as pl
from jax.experimental.pallas import tpu as pltpu

F32 = jnp.float32
BF16 = jnp.bfloat16

D_MODEL = 1024
BATCH = 32
SEQ = 256
DEPTH = 2
DEC_BATCH = 2
DEC_SEQ = 1024
PAST_LEN = 256
GRID_W = 64
HEAD_DIM = 64
GROUP_WIDTH = 256
A_HEADS = 4
A_KV_HEADS = 2
B_HEADS = 4
NA_ROWS = 8
NA_COLS = 16
C_HEADS = 4
S5_CH = 16
S5_GROUPS = 16
S5_STATE = 64
MOE_GROUPS = 4
MOE_PER_GROUP = 8
MOE_EXPERTS = 32
MOE_HIDDEN = 128
ROPE_THETA = 10000.0
EPS = 1e-6
IN_WIDTH = 2560
Q_SCALE = HEAD_DIM ** -0.5

OFF_AQ, OFF_AK, OFF_AV = 0, 256, 384
OFF_BQ, OFF_BK, OFF_BV = 512, 768, 1024
OFF_CQ, OFF_CK, OFF_CV, OFF_CG = 1280, 1536, 1792, 2048
OFF_DU = 2304

LANES = 128
SUBLANES = 8
BF16_ROWS = 16
S5_SP = S5_GROUPS * S5_STATE
S5_SEG = 256
S5_CHUNK = 256
ROUTER_OFF = 4
NEG_BIG = -1e30
VMEM_LIMIT = 56 * 1024 * 1024


def _cparams(*sem):
    return pltpu.CompilerParams(dimension_semantics=sem, vmem_limit_bytes=VMEM_LIMIT)


def _mod_spec(layer, first_row, tiles_per_row, grid_rank):
    if grid_rank == 1:
        return pl.BlockSpec((None, None, 6, D_MODEL), lambda i: (layer, first_row + i // tiles_per_row, 0, 0))
    return pl.BlockSpec((None, None, 6, D_MODEL), lambda i, g: (layer, first_row + i // tiles_per_row, 0, 0))


def _bdot(a, b):
    return jnp.dot(a.astype(BF16), b.astype(BF16), preferred_element_type=F32)


def _bdot_nt(a, b):
    return lax.dot_general(a.astype(BF16), b.astype(BF16), (((1,), (1,)), ((), ())),
                           preferred_element_type=F32)


def _bdot_tn(a, b):
    return lax.dot_general(a.astype(BF16), b.astype(BF16), (((0,), (0,)), ((), ())),
                           preferred_element_type=F32)


def _split(a):
    hi = a.astype(BF16)
    lo = (a - hi.astype(F32)).astype(BF16)
    return hi, lo


def _dot_hilo_lhs(a, b_bf16):
    hi, lo = _split(a)
    return (jnp.dot(hi, b_bf16, preferred_element_type=F32)
            + jnp.dot(lo, b_bf16, preferred_element_type=F32))


def _rms_rows(x):
    return x * lax.rsqrt(jnp.mean(x * x, axis=-1, keepdims=True) + EPS)


def _mod_kernel(cond_ref, w_ref, b_ref, o_ref):
    o_ref[...] = _bdot(jax.nn.silu(cond_ref[...]), w_ref[...]) + b_ref[...]


def _modulation(cond, mod_w, mod_b):
    tn = 1536
    return pl.pallas_call(
        _mod_kernel,
        grid=(DEPTH, 6 * D_MODEL // tn),
        in_specs=[pl.BlockSpec((SUBLANES, D_MODEL), lambda l, j: (0, 0)),
                  pl.BlockSpec((None, D_MODEL, tn), lambda l, j: (l, 0, j)),
                  pl.BlockSpec((None, 1, tn), lambda l, j: (l, 0, j))],
        out_specs=pl.BlockSpec((None, SUBLANES, tn), lambda l, j: (l, 0, j)),
        out_shape=jax.ShapeDtypeStruct((DEPTH, SUBLANES, 6 * D_MODEL), F32),
        compiler_params=_cparams("arbitrary", "arbitrary"),
        name="modulation",
    )(cond, mod_w, mod_b.reshape(DEPTH, 1, 6 * D_MODEL))


def _group_mean_matrix(w):
    ri = lax.broadcasted_iota(jnp.int32, (w, w), 0) // HEAD_DIM
    ci = lax.broadcasted_iota(jnp.int32, (w, w), 1) // HEAD_DIM
    return jnp.where(ri == ci, 1.0 / HEAD_DIM, 0.0).astype(BF16)


def _head_norm(t, g):
    ms = _dot_hilo_lhs(t * t, _group_mean_matrix(t.shape[1]))
    return t * lax.rsqrt(ms + EPS) * g


def _rope(t, cos, sa, sb):
    return (t * cos + pltpu.roll(t, LANES - 16, 1) * sa + pltpu.roll(t, 16, 1) * sb)


def _store_layer_slot(ref, slot, value):
    for s in range(ref.shape[0]):
        ref[s] = value if s == slot else jnp.zeros_like(value)


def _layer_slot_block(layer, first_call, tail):
    if first_call:
        return (None, DEPTH) + tail, (0,) * (1 + len(tail)), layer
    return (None, 1) + tail, (layer,) + (0,) * len(tail), 0


def _proj_kernel(*refs, rope, n_alias, with_cache, slot):
    x_ref, mod_ref, g1_ref, w_ref, qn_ref, kn_ref = refs[:6]
    n_in = 6
    if rope:
        cos_ref, sa_ref, sb_ref = refs[6:9]
        n_in = 9
    outs = refs[n_in + n_alias:]
    z_ref, cg_ref, du_ref = outs[:3]
    h = _rms_rows(x_ref[...]) * g1_ref[...] * (1.0 + mod_ref[1:2, :]) + mod_ref[0:1, :]
    z = jnp.dot(h.astype(BF16), w_ref[...], preferred_element_type=F32)
    aq = _head_norm(z[:, OFF_AQ:OFF_AK], qn_ref[...])
    ak = _head_norm(z[:, OFF_AK:OFF_AV], kn_ref[...])
    for j in range(3):
        t = aq[:, j * LANES:(j + 1) * LANES] if j < 2 else ak
        if rope:
            cj = 0 if j == 2 else j
            sl = slice(cj * LANES, (cj + 1) * LANES)
            t = _rope(t, cos_ref[:, sl], sa_ref[:, sl], sb_ref[:, sl])
        if j == 2:
            ak = t
        z_ref[:, j * LANES:(j + 1) * LANES] = t.astype(BF16)
    z_ref[:, OFF_AV:OFF_CK] = z[:, OFF_AV:OFF_CK].astype(BF16)
    z_ref[:, OFF_CK:OFF_CV] = (z[:, OFF_CK:OFF_CV] * Q_SCALE).astype(BF16)
    z_ref[:, OFF_CV:OFF_CG] = z[:, OFF_CV:OFF_CG].astype(BF16)
    cg_ref[...] = z[:, OFF_CG:OFF_DU]
    du_ref[...] = z[:, OFF_DU:]
    if with_cache:
        ak_ref, av_ref, bk_ref, bv_ref = outs[3:7]
        _store_layer_slot(ak_ref, slot, ak)
        _store_layer_slot(av_ref, slot, z[:, OFF_AV:OFF_BQ])
        _store_layer_slot(bk_ref, slot, z[:, OFF_BK:OFF_BV])
        _store_layer_slot(bv_ref, slot, z[:, OFF_BV:OFF_CQ])


def _du_spec(grid_rank):
    if grid_rank == 1:
        return pl.BlockSpec((None, S5_SEG, GROUP_WIDTH), lambda i: (i // SUBLANES, 0, i % SUBLANES))
    return pl.BlockSpec((None, S5_SEG, GROUP_WIDTH), lambda i, g: (i // SUBLANES, 0, i % SUBLANES))


def _project(x, mods, mod_row, mod_tokens, g1, w_in_bf, qn, kn, rope_tabs, layer, *, seq_len,
             with_cache=False, prev_caches=None):
    tm = S5_SEG
    n = x.shape[0]
    rope = rope_tabs is not None
    in_specs = [pl.BlockSpec((tm, D_MODEL), lambda i: (i, 0)),
                _mod_spec(layer, mod_row, mod_tokens // tm, 1),
                pl.BlockSpec((1, D_MODEL), lambda i: (0, 0)),
                pl.BlockSpec((None, D_MODEL, IN_WIDTH), lambda i: (layer, 0, 0)),
                pl.BlockSpec((1, 256), lambda i: (0, 0)),
                pl.BlockSpec((1, 128), lambda i: (0, 0))]
    args = [x, mods, g1, w_in_bf, qn, kn]
    if rope:
        tps = seq_len // tm
        in_specs += [pl.BlockSpec((tm, 256), lambda i: (i % tps, 0))] * 3
        args += list(rope_tabs)
    out_specs = [pl.BlockSpec((tm, OFF_CG), lambda i: (i, 0)),
                 pl.BlockSpec((tm, GROUP_WIDTH), lambda i: (i, 0)), _du_spec(1)]
    out_shape = [jax.ShapeDtypeStruct((n, OFF_CG), BF16),
                 jax.ShapeDtypeStruct((n, GROUP_WIDTH), F32),
                 jax.ShapeDtypeStruct((n // (tm * SUBLANES), S5_SEG, SUBLANES * GROUP_WIDTH), F32)]
    aliases = {}
    n_alias = 0
    slot = 0
    if with_cache:
        assert tm == seq_len
        nb = n // seq_len
        for w in (128, 128, 256, 256):
            blk, idx, slot = _layer_slot_block(layer, prev_caches is None, (seq_len, w))
            out_specs.append(pl.BlockSpec(blk, lambda i, idx=idx: (i,) + idx))
            out_shape.append(jax.ShapeDtypeStruct((nb, DEPTH, seq_len, w), F32))
        if prev_caches is not None:
            n_alias = len(prev_caches)
            for k, arr in enumerate(prev_caches):
                aliases[len(args)] = 3 + k
                in_specs.append(pl.BlockSpec(memory_space=pl.ANY))
                args.append(arr)
    return pl.pallas_call(
        functools.partial(_proj_kernel, rope=rope, n_alias=n_alias, with_cache=with_cache, slot=slot),
        grid=(n // tm,),
        in_specs=in_specs,
        out_specs=out_specs,
        out_shape=out_shape,
        input_output_aliases=aliases,
        compiler_params=_cparams("parallel"),
        name="project",
    )(*args)


def _rope_tables():
    t = jnp.arange(DEC_SEQ)
    row = (t // GRID_W).astype(F32)
    col = (t % GRID_W).astype(F32)
    nf = HEAD_DIM // 4
    inv = ROPE_THETA ** (-jnp.arange(nf, dtype=F32) / nf)
    ang_r = row[:, None] * inv[None, :]
    ang_c = col[:, None] * inv[None, :]
    zeros = jnp.zeros_like(ang_r)
    cos = jnp.concatenate([jnp.cos(ang_r), jnp.cos(ang_r), jnp.cos(ang_c), jnp.cos(ang_c)], axis=-1)
    sa = jnp.concatenate([-jnp.sin(ang_r), zeros, -jnp.sin(ang_c), zeros], axis=-1)
    sb = jnp.concatenate([zeros, jnp.sin(ang_r), zeros, jnp.sin(ang_c)], axis=-1)
    return tuple(jnp.tile(a, (1, 4)) for a in (cos, sa, sb))


N_HEADS = 4


def _lane_head(width):
    return lax.broadcasted_iota(jnp.int32, (1, width), 1) // HEAD_DIM


def _stack_heads(q):
    head = _lane_head(q.shape[1])
    return jnp.concatenate([jnp.where(head == h, q, 0.0) for h in range(N_HEADS)], axis=0).astype(BF16)


def _stack_heads_gqa(q):
    lo = lax.broadcasted_iota(jnp.int32, (1, LANES), 1) < HEAD_DIM
    q = q.astype(F32)
    q01, q23 = q[:, :LANES], q[:, LANES:]
    blocks = [jnp.where(lo, q01, 0.0), jnp.where(lo, pltpu.roll(q01, HEAD_DIM, 1), 0.0),
              jnp.where(lo, 0.0, pltpu.roll(q23, HEAD_DIM, 1)), jnp.where(lo, 0.0, q23)]
    return jnp.concatenate(blocks, axis=0).astype(BF16)


def _spread_kv_gqa(v):
    lo = lax.broadcasted_iota(jnp.int32, (1, LANES), 1) < HEAD_DIM
    v = v.astype(F32)
    vr = pltpu.roll(v, HEAD_DIM, 1)
    return jnp.concatenate([jnp.where(lo, v, vr), jnp.where(lo, vr, v)], axis=1)


def _mha(qs, blocks, tq):
    scores = []
    for k, _, bias in blocks:
        s = _bdot_nt(qs, k)
        scores.append(s if bias is None else s + bias)
    m = functools.reduce(jnp.maximum, [jnp.max(s, axis=-1, keepdims=True) for s in scores])
    es = [jnp.exp(s - m) for s in scores]
    denom = functools.reduce(jnp.add, [jnp.sum(e, axis=-1, keepdims=True) for e in es])
    ps = [e.astype(BF16) for e in es]
    head = _lane_head(N_HEADS * HEAD_DIM)
    vals = [v.astype(BF16) for _, v, _ in blocks]
    o = None
    dall = None
    for h in range(N_HEADS):
        rows = slice(h * tq, (h + 1) * tq)
        for p, v in zip(ps, vals):
            t = jnp.dot(p[rows], jnp.where(head == h, v, jnp.zeros_like(v)), preferred_element_type=F32)
            o = t if o is None else o + t
        d = jnp.where(head == h, denom[rows], 0.0)
        dall = d if dall is None else dall + d
    return (o / dall).astype(BF16)


def _ctx_attn_kernel(aq_ref, ak_ref, av_ref, bq_ref, bk_ref, bv_ref, oa_ref, ob_ref):
    tq = aq_ref.shape[0]
    oa_ref[...] = _mha(_stack_heads_gqa(aq_ref[...] * Q_SCALE),
                       [(ak_ref[...], _spread_kv_gqa(av_ref[...]), None)], tq)
    ob_ref[...] = _mha(_stack_heads(bq_ref[...] * Q_SCALE), [(bk_ref[...], bv_ref[...], None)], tq)


def _ctx_attention(z, nb, seq_len):
    def col(width, off):
        return pl.BlockSpec((seq_len, width), lambda b: (b, off // width))
    return pl.pallas_call(
        _ctx_attn_kernel,
        grid=(nb,),
        in_specs=[col(256, OFF_AQ), col(128, OFF_AK), col(128, OFF_AV),
                  col(256, OFF_BQ), col(256, OFF_BK), col(256, OFF_BV)],
        out_specs=[pl.BlockSpec((seq_len, 256), lambda b: (b, 0))] * 2,
        out_shape=[jax.ShapeDtypeStruct((nb * seq_len, 256), BF16)] * 2,
        compiler_params=_cparams("parallel"),
        name="ctx_attention",
    )(z, z, z, z, z, z)


def _lat_attn_a_kernel(q_ref, kn_ref, vn_ref, kc_ref, vc_ref, o_ref):
    o_ref[...] = _mha(_stack_heads_gqa(q_ref[...] * Q_SCALE),
                      [(kc_ref[...], _spread_kv_gqa(vc_ref[...]), None),
                       (kn_ref[...], _spread_kv_gqa(vn_ref[...]), None)], q_ref.shape[0])


def _lat_attention_a(z, cache_k, cache_v, layer, tq=256):
    nq = DEC_SEQ // tq
    cache_spec = pl.BlockSpec((None, None, PAST_LEN, 128), lambda b, j: (b, layer, 0, 0))
    return pl.pallas_call(
        _lat_attn_a_kernel,
        grid=(DEC_BATCH, nq),
        in_specs=[pl.BlockSpec((tq, 256), lambda b, j: (b * nq + j, OFF_AQ // 256)),
                  pl.BlockSpec((DEC_SEQ, 128), lambda b, j: (b, OFF_AK // 128)),
                  pl.BlockSpec((DEC_SEQ, 128), lambda b, j: (b, OFF_AV // 128)),
                  cache_spec, cache_spec],
        out_specs=pl.BlockSpec((tq, 256), lambda b, j: (b * nq + j, 0)),
        out_shape=jax.ShapeDtypeStruct((DEC_BATCH * DEC_SEQ, 256), BF16),
        compiler_params=_cparams("parallel", "parallel"),
        name="lat_attention_a",
    )(z, z, z, cache_k, cache_v)


NA_KEYS = NA_ROWS * GRID_W


NA_PAIRS = 2 * NA_ROWS - 2


NA_STEP_ROWS = 2


def _na_kernel(q_ref, k_ref, v_ref, kc_ref, vc_ref, bias_ref, o_ref):
    rows = DEC_SEQ // GRID_W
    outs = []
    for rr in range(NA_STEP_ROWS):
        r = pl.program_id(1) * NA_STEP_ROWS + rr
        row_start = jnp.clip(r - NA_ROWS // 2, 0, rows - NA_ROWS)
        start = pl.multiple_of(row_start * GRID_W, GRID_W)
        rel0 = row_start - r + NA_ROWS - 1
        kl = k_ref[pl.ds(start, NA_KEYS), :]
        vl = v_ref[pl.ds(start, NA_KEYS), :]
        bias = jnp.concatenate(
            [jnp.concatenate([bias_ref[h, rel0 + 2 * jp] for jp in range(NA_ROWS // 2)], axis=1)
             for h in range(B_HEADS)], axis=0)
        qrows = slice(rr * GRID_W, (rr + 1) * GRID_W)
        outs.append(_mha(_stack_heads(q_ref[qrows, :] * Q_SCALE),
                         [(kl, vl, bias), (kc_ref[...], vc_ref[...], None)], GRID_W))
    o_ref[...] = jnp.concatenate(outs, axis=0)


def _na_bias(rel_bias):
    nrel = 2 * NA_COLS - 1
    period = 2 * GRID_W
    b = rel_bias.astype(F32)
    ext = jnp.concatenate([b[..., NA_COLS - 1:],
                           jnp.zeros(b.shape[:-1] + (period - nrel,), F32),
                           b[..., :NA_COLS - 1]], axis=-1)
    flat = jnp.tile(ext, (1, 1, GRID_W))[..., :GRID_W * (period - 1)]
    toe = flat.reshape(b.shape[:-1] + (GRID_W, period - 1))[..., :GRID_W]
    col_start = np.clip(np.arange(GRID_W) - NA_COLS // 2, 0, GRID_W - NA_COLS)
    kc = np.arange(GRID_W)
    inside = (kc[None, :] >= col_start[:, None]) & (kc[None, :] < col_start[:, None] + NA_COLS)
    toe = jnp.where(jnp.asarray(inside), toe, NEG_BIG)
    return jnp.concatenate([toe[:, :-1], toe[:, 1:]], axis=-1)


def _lat_attention_b(z, cache_k, cache_v, bias, layer):
    rows = DEC_SEQ // GRID_W // NA_STEP_ROWS
    tq = NA_STEP_ROWS * GRID_W
    cache_spec = pl.BlockSpec((None, None, PAST_LEN, 256), lambda b, r: (b, layer, 0, 0))
    return pl.pallas_call(
        _na_kernel,
        grid=(DEC_BATCH, rows),
        in_specs=[pl.BlockSpec((tq, 256), lambda b, r: (b * rows + r, OFF_BQ // 256)),
                  pl.BlockSpec((DEC_SEQ, 256), lambda b, r: (b, OFF_BK // 256)),
                  pl.BlockSpec((DEC_SEQ, 256), lambda b, r: (b, OFF_BV // 256)),
                  cache_spec, cache_spec,
                  pl.BlockSpec((B_HEADS, NA_PAIRS, GRID_W, 2 * GRID_W), lambda b, r: (0, 0, 0, 0))],
        out_specs=pl.BlockSpec((tq, 256), lambda b, r: (b * rows + r, 0)),
        out_shape=jax.ShapeDtypeStruct((DEC_BATCH * DEC_SEQ, 256), BF16),
        compiler_params=_cparams("parallel", "parallel"),
        name="lat_attention_b",
    )(z, z, z, cache_k, cache_v, bias)


def _retention_kernel(q_ref, g_ref, k_ref, v_ref, dec_ref, gn_ref, *rest, seq_len, tq, has_state,
                      hoist_decay, slot):
    if has_state:
        s0_ref, o_ref, dec_scr = rest
    else:
        o_ref, st_ref, dec_scr = rest[-3:]
    head = _lane_head(C_HEADS * HEAD_DIM)
    lg = jax.nn.log_sigmoid(dec_ref[...])

    def per_lane(row0):
        out = jnp.zeros((1, C_HEADS * HEAD_DIM), F32)
        for h in range(C_HEADS):
            out = jnp.where(head == h, lg[row0 + h:row0 + h + 1, 0:1], out)
        return out

    lgf_l, lgb_l = per_lane(0), per_lane(C_HEADS)
    i0 = pl.program_id(1) * tq
    qi = (i0 + lax.broadcasted_iota(jnp.int32, (tq, 1), 0)).astype(F32)

    def fill_decay():
        kj = lax.broadcasted_iota(jnp.int32, (1, seq_len), 1).astype(F32)
        diff = qi - kj
        for h in range(C_HEADS):
            lgf = lg[h:h + 1, 0:1]
            lgb = lg[C_HEADS + h:C_HEADS + h + 1, 0:1]
            dec_scr[h * tq:(h + 1) * tq, :] = (
                jnp.where(diff >= 0, jnp.exp(lgf * jnp.maximum(diff, 0.0)), 0.0)
                + jnp.where(diff <= 0, jnp.exp(lgb * jnp.maximum(-diff, 0.0)), 0.0))

    if hoist_decay:
        pl.when(pl.program_id(0) == 0)(fill_decay)
    else:
        fill_decay()

    q = q_ref[...]
    k = k_ref[...]
    v = v_ref[...].astype(BF16)
    sc = (_bdot_nt(_stack_heads(q), k) * dec_scr[...]).astype(BF16)
    o = None
    for h in range(C_HEADS):
        t = jnp.dot(sc[h * tq:(h + 1) * tq], jnp.where(head == h, v, jnp.zeros_like(v)),
                    preferred_element_type=F32)
        o = t if o is None else o + t
    if has_state:
        o = (o + _bdot(q, s0_ref[0]) * jnp.exp(lgf_l * (qi + 1.0))
             + _bdot(q, s0_ref[1]) * jnp.exp(lgb_l * (seq_len - qi)))
    gm = _group_mean_matrix(C_HEADS * HEAD_DIM)
    dlt = o - _dot_hilo_lhs(o, gm)
    var = _dot_hilo_lhs(dlt * dlt, gm)
    o_ref[...] = (dlt * lax.rsqrt(var + EPS) * gn_ref[...] * jax.nn.silu(g_ref[...])).astype(BF16)
    if not has_state:
        kpos = lax.broadcasted_iota(jnp.int32, (seq_len, 1), 0).astype(F32)
        sf = _bdot_tn(k * jnp.exp(lgf_l * (seq_len - 1.0 - kpos)), v)
        sb = _bdot_tn(k * jnp.exp(lgb_l * kpos), v)
        for s in range(st_ref.shape[0]):
            for h in range(C_HEADS):
                sl = slice(h * HEAD_DIM, (h + 1) * HEAD_DIM)
                st_ref[s, 0, h] = sf[sl, sl] if s == slot else jnp.zeros((HEAD_DIM, HEAD_DIM), F32)
                st_ref[s, 1, h] = sb[sl, sl] if s == slot else jnp.zeros((HEAD_DIM, HEAD_DIM), F32)


def _retention(z, cg, dec, gn, s0, layer, *, nb, seq_len, prev_state=None, tq=256):
    nq = seq_len // tq
    has_state = s0 is not None
    aliases = {}
    slot = 0
    in_specs = [pl.BlockSpec((tq, 256), lambda b, j: (b * nq + j, OFF_CQ // 256)),
                pl.BlockSpec((tq, 256), lambda b, j: (b * nq + j, 0)),
                pl.BlockSpec((seq_len, 256), lambda b, j: (b, OFF_CK // 256)),
                pl.BlockSpec((seq_len, 256), lambda b, j: (b, OFF_CV // 256)),
                pl.BlockSpec((SUBLANES, LANES), lambda b, j: (0, 0)),
                pl.BlockSpec((1, 256), lambda b, j: (0, 0))]
    args = [z, cg, z, z, dec, gn]
    o_spec = pl.BlockSpec((tq, 256), lambda b, j: (b * nq + j, 0))
    o_shape = jax.ShapeDtypeStruct((nb * seq_len, 256), BF16)
    if has_state:
        in_specs.append(pl.BlockSpec((None, None, 2, 256, 256), lambda b, j: (b, layer, 0, 0, 0)))
        args.append(s0)
        out_specs, out_shape = o_spec, o_shape
    else:
        assert nq == 1
        blk, idx, slot = _layer_slot_block(layer, prev_state is None, (2, C_HEADS, HEAD_DIM, HEAD_DIM))
        out_specs = [o_spec, pl.BlockSpec(blk, lambda b, j: (b,) + idx)]
        out_shape = [o_shape, jax.ShapeDtypeStruct((nb, DEPTH, 2, C_HEADS, HEAD_DIM, HEAD_DIM), F32)]
        if prev_state is not None:
            aliases[len(args)] = 1
            in_specs.append(pl.BlockSpec(memory_space=pl.ANY))
            args.append(prev_state)
    return pl.pallas_call(
        functools.partial(_retention_kernel, seq_len=seq_len, tq=tq, has_state=has_state,
                          hoist_decay=nq == 1, slot=slot),
        grid=(nb, nq),
        in_specs=in_specs,
        out_specs=out_specs,
        out_shape=out_shape,
        scratch_shapes=[pltpu.VMEM((C_HEADS * tq, seq_len), F32)],
        input_output_aliases=aliases,
        compiler_params=_cparams("arbitrary", "arbitrary"),
        name="retention",
    )(*args)


def _s5_prep_kernel(lre_ref, lim_ref, ldt_ref, bre_ref, bim_ref, cre_ref, cim_ref,
                    a_ref, bm_ref, cro_ref, cio_ref, bm_scr, cr_scr, ci_scr):
    lre = lre_ref[...]
    lim = lim_ref[...]
    dt = jnp.exp(ldt_ref[...])
    mag = jnp.exp(lre * dt)
    a_re = mag * jnp.cos(lim * dt)
    a_im = mag * jnp.sin(lim * dt)
    den = lre * lre + lim * lim
    r_re = ((a_re - 1.0) * lre + a_im * lim) / den
    r_im = (a_im * lre - (a_re - 1.0) * lim) / den
    bm_scr[...] = jnp.zeros_like(bm_scr)
    cr_scr[...] = jnp.zeros_like(cr_scr)
    ci_scr[...] = jnp.zeros_like(ci_scr)
    for g in range(S5_GROUPS):
        rows = slice(g * S5_CH, (g + 1) * S5_CH)
        cols = slice(g * S5_STATE, (g + 1) * S5_STATE)
        a_ref[0:1, cols] = a_re[g:g + 1, :]
        a_ref[1:2, cols] = a_im[g:g + 1, :]
        rr, ri = r_re[g:g + 1, :], r_im[g:g + 1, :]
        br, bi = bre_ref[g], bim_ref[g]
        bm_scr[rows, cols] = rr * br - ri * bi
        bm_scr[rows, S5_SP + g * S5_STATE:S5_SP + (g + 1) * S5_STATE] = rr * bi + ri * br
        cr_scr[cols, rows] = cre_ref[g]
        ci_scr[cols, rows] = cim_ref[g]
    bm_ref[...] = bm_scr[...].astype(BF16)
    cro_ref[...] = cr_scr[...].astype(BF16)
    cio_ref[...] = ci_scr[...].astype(BF16)


def _s5_prepare(lam_re, lam_im, log_dt, b_re, b_im, c_re, c_im):
    gp = (S5_GROUPS, S5_STATE)
    ldt = jnp.broadcast_to(log_dt[..., None], (DEPTH, 2) + gp)
    bt = [jnp.swapaxes(t, -1, -2) for t in (b_re, b_im)]
    ct = [jnp.swapaxes(t, -1, -2) for t in (c_re, c_im)]

    def spec(*tail):
        return pl.BlockSpec((None, None) + tail, lambda l, d: (l, d) + (0,) * len(tail))

    return pl.pallas_call(
        _s5_prep_kernel,
        grid=(DEPTH, 2),
        in_specs=[spec(*gp)] * 3 + [spec(S5_GROUPS, S5_CH, S5_STATE)] * 2 + [spec(S5_GROUPS, S5_STATE, S5_CH)] * 2,
        out_specs=[spec(2, S5_SP), spec(GROUP_WIDTH, 2 * S5_SP), spec(S5_SP, GROUP_WIDTH), spec(S5_SP, GROUP_WIDTH)],
        out_shape=[jax.ShapeDtypeStruct((DEPTH, 2, 2, S5_SP), F32),
                   jax.ShapeDtypeStruct((DEPTH, 2, GROUP_WIDTH, 2 * S5_SP), BF16),
                   jax.ShapeDtypeStruct((DEPTH, 2, S5_SP, GROUP_WIDTH), BF16),
                   jax.ShapeDtypeStruct((DEPTH, 2, S5_SP, GROUP_WIDTH), BF16)],
        scratch_shapes=[pltpu.VMEM((GROUP_WIDTH, 2 * S5_SP), F32), pltpu.VMEM((S5_SP, GROUP_WIDTH), F32),
                        pltpu.VMEM((S5_SP, GROUP_WIDTH), F32)],
        compiler_params=_cparams("parallel", "parallel"),
        name="s5_prepare",
    )(lam_re, lam_im, ldt, bt[0], bt[1], ct[0], ct[1])


def _cmul(ar, ai, br, bi):
    return ar * br - ai * bi, ar * bi + ai * br


def _s5_kernel(u_ref, h0_ref, a_ref, bm_ref, cre_ref, cim_ref, dvec_ref, glu_ref, *rest, nseg, slot):
    od_ref, fin_ref, x_scr, s_scr, y_scr = rest[-5:]
    steps = S5_SEG
    rows = steps * SUBLANES
    chunk = S5_CHUNK
    chunk_steps = chunk // SUBLANES
    nchunk = rows // chunk
    seg = lax.broadcasted_iota(jnp.int32, (SUBLANES, S5_SP), 0) % nseg

    for d in range(2):
        ar = jnp.broadcast_to(a_ref[d, 0:1, :], (SUBLANES, S5_SP))
        ai = jnp.broadcast_to(a_ref[d, 1:2, :], (SUBLANES, S5_SP))

        def row0(k):
            c = k if d == 0 else nchunk - 1 - k
            return c * chunk if isinstance(c, int) else pl.multiple_of(c * chunk, chunk)

        def input_part(k, buf):
            x_scr[buf] = jnp.dot(u_ref[pl.ds(row0(k), chunk), :].astype(BF16), bm_ref[d],
                                 preferred_element_type=F32)

        def scan_part(buf, carry, store):
            sr, si = carry
            for t in range(chunk_steps):
                r = (t if d == 0 else chunk_steps - 1 - t) * SUBLANES
                pr, pi = _cmul(ar, ai, sr, si)
                sr = pr + x_scr[buf, r:r + SUBLANES, 0:S5_SP]
                si = pi + x_scr[buf, r:r + SUBLANES, S5_SP:]
                if store:
                    s_scr[buf, r:r + SUBLANES, 0:S5_SP] = sr
                    s_scr[buf, r:r + SUBLANES, S5_SP:] = si
            return sr, si

        def output_part(k, buf):
            y = _bdot(s_scr[buf, :, 0:S5_SP], cre_ref[d]) - _bdot(s_scr[buf, :, S5_SP:], cim_ref[d])
            rows_k = pl.ds(row0(k), chunk)
            if d == 0:
                y_scr[rows_k, :] = y
            else:
                zz = jax.nn.gelu(y_scr[rows_k, :] + y + dvec_ref[...] * u_ref[rows_k, :])
                od_ref[rows_k, :] = (zz * jax.nn.sigmoid(_bdot(zz, glu_ref[...]))).astype(BF16)

        def half(k, buf, carry, store, nxt=True, prev=True):
            if nxt:
                input_part(k + 1, 1 - buf)
            carry = scan_part(buf, carry, store)
            if store and prev:
                output_part(k - 1, 1 - buf)
            return carry

        def run_pass(carry, store):
            input_part(0, 0)
            carry = half(0, 0, carry, store, prev=False)
            carry = half(1, 1, carry, store)

            def pair(j, c):
                c = half(2 * j, 0, c, store)
                return half(2 * j + 1, 1, c, store)
            carry = lax.fori_loop(1, nchunk // 2 - 1, pair, carry)
            carry = half(nchunk - 2, 0, carry, store)
            carry = half(nchunk - 1, 1, carry, store, nxt=False)
            if store:
                output_part(nchunk - 1, 1)
            return carry

        init = (h0_ref[d, :, 0:S5_SP], h0_ref[d, :, S5_SP:])
        if nseg > 1:
            zero = jnp.zeros((SUBLANES, S5_SP), F32)
            fr, fi = run_pass((zero, zero), store=False)
            pr, pi = ar, ai
            for _ in range(int(math.log2(steps))):
                pr, pi = _cmul(pr, pi, pr, pi)
            cr, ci = init
            shift = 1 if d == 0 else SUBLANES - 1
            order = range(1, nseg) if d == 0 else range(nseg - 2, -1, -1)
            for s in order:
                ncr, nci = pltpu.roll(cr, shift, 0), pltpu.roll(ci, shift, 0)
                nfr, nfi = pltpu.roll(fr, shift, 0), pltpu.roll(fi, shift, 0)
                qr, qi = _cmul(pr, pi, ncr, nci)
                cr = jnp.where(seg == s, qr + nfr, cr)
                ci = jnp.where(seg == s, qi + nfi, ci)
            init = (cr, ci)
        sr, si = run_pass(init, store=True)
        for s in range(fin_ref.shape[1] // (4 * S5_SP)):
            base = (4 * s + 2 * d) * S5_SP
            fin_ref[:, base:base + S5_SP] = sr if s == slot else jnp.zeros_like(sr)
            fin_ref[:, base + S5_SP:base + 2 * S5_SP] = si if s == slot else jnp.zeros_like(si)


def _s5(du_tm, h0, a, bmat, cre, cim, dvec, glu_bf, layer, *, nseg, fin_layer=0, fin_layers=1,
        prev_fin=None):
    nblk = du_tm.shape[0]
    rows = S5_SEG * SUBLANES
    fin_w = 4 * S5_SP
    in_specs = [pl.BlockSpec((None, rows, GROUP_WIDTH), lambda i: (i, 0, 0)),
                pl.BlockSpec((2, SUBLANES, 2 * S5_SP), lambda i: (0, 0, 0)),
                pl.BlockSpec((None, 2, 2, S5_SP), lambda i: (layer, 0, 0, 0)),
                pl.BlockSpec((None, 2, GROUP_WIDTH, 2 * S5_SP), lambda i: (layer, 0, 0, 0)),
                pl.BlockSpec((None, 2, S5_SP, GROUP_WIDTH), lambda i: (layer, 0, 0, 0)),
                pl.BlockSpec((None, 2, S5_SP, GROUP_WIDTH), lambda i: (layer, 0, 0, 0)),
                pl.BlockSpec((1, GROUP_WIDTH), lambda i: (0, 0)),
                pl.BlockSpec((None, GROUP_WIDTH, GROUP_WIDTH), lambda i: (layer, 0, 0))]
    args = [du_tm.reshape(nblk, rows, GROUP_WIDTH), h0, a, bmat, cre, cim, dvec, glu_bf]
    aliases = {}
    if prev_fin is not None:
        aliases[len(args)] = 1
        in_specs.append(pl.BlockSpec(memory_space=pl.ANY))
        args.append(prev_fin)
        fin_spec, slot = pl.BlockSpec((SUBLANES, fin_w), lambda i: (i, fin_layer)), 0
    else:
        fin_spec, slot = pl.BlockSpec((SUBLANES, fin_layers * fin_w), lambda i: (i, 0)), fin_layer
    od, fin = pl.pallas_call(
        functools.partial(_s5_kernel, nseg=nseg, slot=slot),
        grid=(nblk,),
        in_specs=in_specs,
        out_specs=[pl.BlockSpec((None, rows, GROUP_WIDTH), lambda i: (i, 0, 0)), fin_spec],
        out_shape=[jax.ShapeDtypeStruct((nblk, rows, GROUP_WIDTH), BF16),
                   jax.ShapeDtypeStruct((nblk * SUBLANES, fin_layers * fin_w), F32)],
        scratch_shapes=[pltpu.VMEM((2, S5_CHUNK, 2 * S5_SP), F32), pltpu.VMEM((2, S5_CHUNK, 2 * S5_SP), F32),
                        pltpu.VMEM((rows, GROUP_WIDTH), F32)],
        input_output_aliases=aliases,
        compiler_params=_cparams("parallel"),
        name="s5",
    )(*args)
    return od.reshape(nblk, S5_SEG, SUBLANES * GROUP_WIDTH), fin


ROUTE_GROUP = MOE_PER_GROUP
OUT_SEQS = 2


def _out_kernel(x_ref, oa_ref, ob_ref, oc_ref, od_ref, mod_ref, wo_ref, g2_ref, wrh_ref, wrl_ref, br_ref,
                xm_ref, h2_ref, route_ref, cnt_ref):
    od = jnp.concatenate([od_ref[:, s * GROUP_WIDTH:(s + 1) * GROUP_WIDTH] for s in range(OUT_SEQS)], axis=0)
    mix = functools.reduce(jnp.add, [
        _bdot(o, wo_ref[i * GROUP_WIDTH:(i + 1) * GROUP_WIDTH, :])
        for i, o in enumerate((oa_ref[...], ob_ref[...], oc_ref[...], od))])
    xm = x_ref[...] + mod_ref[2:3, :] * mix
    xm_ref[...] = xm
    h2 = _rms_rows(xm) * g2_ref[...] * (1.0 + mod_ref[4:5, :]) + mod_ref[3:4, :]
    h2_ref[...] = h2.astype(BF16)

    h_hi, h_lo = _split(h2)
    logits = (jnp.dot(h_hi, wrh_ref[...], preferred_element_type=F32)
              + jnp.dot(h_hi, wrl_ref[...], preferred_element_type=F32)
              + jnp.dot(h_lo, wrh_ref[...], preferred_element_type=F32)) + br_ref[...]
    lane_i = lax.broadcasted_iota(jnp.int32, logits.shape, 1)
    lane = lane_i.astype(F32)
    big = jnp.float32(2 ** 30)
    gmask = lane_i < MOE_GROUPS
    gl = jnp.where(gmask, logits, -jnp.inf)
    gmax = jnp.max(gl, axis=-1, keepdims=True)
    p_top = 1.0 / jnp.sum(jnp.exp(gl - gmax), axis=-1, keepdims=True)
    g_top = jnp.min(jnp.where(gl == gmax, lane, big), axis=-1, keepdims=True)
    e_lane = lane_i - ROUTER_OFF
    lane_group = (e_lane // MOE_PER_GROUP).astype(F32)
    emask = (e_lane >= 0) & (e_lane < MOE_EXPERTS) & (lane_group == g_top)
    el = jnp.where(emask, logits, -jnp.inf)
    m1 = jnp.max(el, axis=-1, keepdims=True)
    i1 = jnp.min(jnp.where(el == m1, lane, big), axis=-1, keepdims=True)
    el2 = jnp.where(lane == i1, -jnp.inf, el)
    m2 = jnp.max(el2, axis=-1, keepdims=True)
    i2 = jnp.min(jnp.where(el2 == m2, lane, big), axis=-1, keepdims=True)
    e2 = jnp.exp(m2 - m1)
    den = 1.0 + e2
    gates = (jnp.where(lane == i1, (1.0 / den) * p_top, 0.0)
             + jnp.where(lane == i2, (e2 / den) * p_top, 0.0))
    route = jnp.where(lane == ROUTE_GROUP + g_top, 1.0, 0.0)
    for g in range(MOE_GROUPS):
        local = pltpu.roll(gates, LANES - ROUTER_OFF - g * MOE_PER_GROUP, 1)
        route = route + jnp.where((g_top == g) & (lane_i < MOE_PER_GROUP), local, 0.0)
    route_ref[...] = route
    cnt_ref[...] = jnp.broadcast_to(jnp.sum(route, axis=0, keepdims=True), (SUBLANES, LANES)).astype(jnp.int32)


def _output_stage(x, mixes, mods, mod_row, mod_tokens, wo_bf, g2, wr_hi, wr_lo, br, layer):
    tm = OUT_SEQS * S5_SEG
    n = x.shape[0]
    row = lambda w: pl.BlockSpec((tm, w), lambda i: (i, 0))
    const = lambda shape: pl.BlockSpec(shape, lambda i: (0,) * len(shape))
    per_blk = SUBLANES // OUT_SEQS
    return pl.pallas_call(
        _out_kernel,
        grid=(n // tm,),
        in_specs=[row(D_MODEL), row(256), row(256), row(256),
                  pl.BlockSpec((None, S5_SEG, OUT_SEQS * GROUP_WIDTH), lambda i: (i // per_blk, 0, i % per_blk)),
                  _mod_spec(layer, mod_row, mod_tokens // tm, 1),
                  pl.BlockSpec((None, D_MODEL, D_MODEL), lambda i: (layer, 0, 0)), const((1, D_MODEL)),
                  const((D_MODEL, LANES)), const((D_MODEL, LANES)), const((1, LANES))],
        out_specs=[row(D_MODEL), row(D_MODEL), row(LANES),
                   pl.BlockSpec((None, SUBLANES, LANES), lambda i: (i, 0, 0))],
        out_shape=[jax.ShapeDtypeStruct((n, D_MODEL), F32),
                   jax.ShapeDtypeStruct((n, D_MODEL), BF16),
                   jax.ShapeDtypeStruct((n, LANES), F32),
                   jax.ShapeDtypeStruct((n // tm, SUBLANES, LANES), jnp.int32)],
        compiler_params=_cparams("parallel"),
        name="output_stage",
    )(x, *mixes, mods, wo_bf, g2, wr_hi, wr_lo, br)


GROUP_HID = MOE_PER_GROUP * MOE_HIDDEN


MOE_CHUNK = 128
MOE_STATIC_CHUNKS = 2


def _moe_kernel(cnt_ref, h2_ref, route_ref, xm_ref, mod_ref, w1_ref, w3_ref, w2_ref, fg_ref, o_ref,
                hs_scr, rs_scr, os_scr, before_scr, *, final, tm):
    i = pl.program_id(0)
    off1 = cnt_ref[i, 0]
    off2 = off1 + cnt_ref[i, 1]
    off3 = off2 + cnt_ref[i, 2]
    starts = (jnp.int32(0), off1, off2, off3)
    ends = (off1, off2, off3, jnp.int32(tm))

    route = route_ref[...]
    r_hi, r_lo = _split(route)
    pick = (lax.broadcasted_iota(jnp.int32, (SUBLANES, LANES), 1)
            == ROUTE_GROUP + lax.broadcasted_iota(jnp.int32, (SUBLANES, LANES), 0))
    gt = lax.dot_general(jnp.where(pick, 1.0, 0.0).astype(BF16), r_hi, (((1,), (1,)), ((), ())),
                         preferred_element_type=F32)
    @pl.when(i == 0)
    def _():
        before_scr[...] = jnp.where(lax.broadcasted_iota(jnp.int32, (tm, tm), 0)
                                    < lax.broadcasted_iota(jnp.int32, (tm, tm), 1), 1.0, 0.0).astype(BF16)

    rank = jnp.dot(gt.astype(BF16), before_scr[...], preferred_element_type=F32)
    gt_i = gt.astype(jnp.int32)
    rank_i = rank.astype(jnp.int32)
    pos = jnp.zeros((1, tm), jnp.int32)
    for g in range(MOE_GROUPS):
        pos = pos + gt_i[g:g + 1, :] * (rank_i[g:g + 1, :] + starts[g])
    perm = jnp.where(lax.broadcasted_iota(jnp.int32, (tm, tm), 0) == pos, 1.0, 0.0).astype(BF16)
    hs_scr[...] = jnp.dot(perm, h2_ref[...], preferred_element_type=F32).astype(BF16)
    rs_scr[...] = (jnp.dot(perm, r_hi, preferred_element_type=F32)
                   + jnp.dot(perm, r_lo, preferred_element_type=F32))

    os_scr[...] = jnp.zeros_like(os_scr)
    for g in range(MOE_GROUPS):
        lo, hi = starts[g], ends[g]
        base = (lo // BF16_ROWS) * BF16_ROWS
        n_chunks = jnp.where(hi > lo, (hi - base + MOE_CHUNK - 1) // MOE_CHUNK, 0)

        def chunk_body(c, carry, g=g, lo=lo, hi=hi, base=base):
            r0 = pl.multiple_of(jnp.minimum(base + c * MOE_CHUNK, tm - MOE_CHUNK), BF16_ROWS)
            rows = pl.ds(r0, MOE_CHUNK)
            x = hs_scr[rows, :]
            gates = rs_scr[rows, :]
            a = jnp.dot(x, w1_ref[g], preferred_element_type=F32)
            b = jnp.dot(x, w3_ref[g], preferred_element_type=F32)
            hid = []
            for e in range(MOE_PER_GROUP):
                sl = slice(e * MOE_HIDDEN, (e + 1) * MOE_HIDDEN)
                hid.append((jax.nn.silu(a[:, sl]) * b[:, sl] * gates[:, e:e + 1]).astype(BF16))
            y = jnp.dot(jnp.concatenate(hid, axis=1), w2_ref[g], preferred_element_type=F32)
            rowid = r0 + lax.broadcasted_iota(jnp.int32, (MOE_CHUNK, 1), 0)
            member = (rowid >= lo) & (rowid < hi)
            os_scr[rows, :] = jnp.where(member, y, os_scr[rows, :])
            return carry

        for c in range(MOE_STATIC_CHUNKS):
            chunk_body(c, 0)
        lax.fori_loop(MOE_STATIC_CHUNKS, n_chunks, chunk_body, 0)

    o_hi, o_lo = _split(os_scr[...])
    moe = (lax.dot_general(perm, o_hi, (((0,), (0,)), ((), ())), preferred_element_type=F32)
           + lax.dot_general(perm, o_lo, (((0,), (0,)), ((), ())), preferred_element_type=F32))
    out = xm_ref[...] + mod_ref[5:6, :] * moe
    if final:
        out = _rms_rows(out) * fg_ref[...]
    o_ref[...] = out


def _moe_weight_kernel(w1_ref, w3_ref, w2_ref, o1_ref, o3_ref, o2_ref):
    for e in range(MOE_PER_GROUP):
        sl = slice(e * MOE_HIDDEN, (e + 1) * MOE_HIDDEN)
        o1_ref[:, sl] = w1_ref[e].astype(BF16)
        o3_ref[:, sl] = w3_ref[e].astype(BF16)
        o2_ref[sl, :] = w2_ref[e].astype(BF16)


def _moe_weights(w1, w3, w2):
    up = pl.BlockSpec((None, MOE_PER_GROUP, D_MODEL, MOE_HIDDEN), lambda l, g: (l, g, 0, 0))
    down = pl.BlockSpec((None, MOE_PER_GROUP, MOE_HIDDEN, D_MODEL), lambda l, g: (l, g, 0, 0))
    out = pl.BlockSpec((None, None, D_MODEL, GROUP_HID), lambda l, g: (l, g, 0, 0))
    shape = jax.ShapeDtypeStruct((DEPTH, MOE_GROUPS, D_MODEL, GROUP_HID), BF16)
    return pl.pallas_call(
        _moe_weight_kernel,
        grid=(DEPTH, MOE_GROUPS),
        in_specs=[up, up, down],
        out_specs=[out, out, out],
        out_shape=[shape, shape, shape],
        compiler_params=_cparams("parallel", "parallel"),
        name="moe_weights",
    )(w1, w3, w2)


def _moe(h2, route, tile_counts, xm, mods, mod_row, mod_tokens, w1g, w3g, w2g, fg, layer, *, final, tm=512):
    n = h2.shape[0]
    cnt = tile_counts[:, 0, ROUTE_GROUP:ROUTE_GROUP + MOE_GROUPS].reshape(
        n // tm, tm // (OUT_SEQS * S5_SEG), MOE_GROUPS).sum(axis=1)
    row = lambda w: pl.BlockSpec((tm, w), lambda i, c: (i, 0))
    mod_tiles = mod_tokens // tm
    wspec = pl.BlockSpec((None, MOE_GROUPS, D_MODEL, GROUP_HID), lambda i, c: (layer, 0, 0, 0),
                         pipeline_mode=pl.Buffered(1))
    return pl.pallas_call(
        functools.partial(_moe_kernel, final=final, tm=tm),
        grid_spec=pltpu.PrefetchScalarGridSpec(
            num_scalar_prefetch=1,
            grid=(n // tm,),
            in_specs=[row(D_MODEL), row(LANES), row(D_MODEL),
                      pl.BlockSpec((None, None, 6, D_MODEL), lambda i, c: (layer, mod_row + i // mod_tiles, 0, 0)),
                      wspec, wspec, wspec,
                      pl.BlockSpec((1, D_MODEL), lambda i, c: (0, 0))],
            out_specs=row(D_MODEL),
            scratch_shapes=[pltpu.VMEM((tm, D_MODEL), BF16), pltpu.VMEM((tm, LANES), F32),
                            pltpu.VMEM((tm, D_MODEL), F32), pltpu.VMEM((tm, tm), BF16)]),
        out_shape=jax.ShapeDtypeStruct((n, D_MODEL), F32),
        compiler_params=_cparams("arbitrary"),
        name="moe",
    )(cnt, h2, route, xm, mods, w1g, w3g, w2g, fg)


def kernel(x_prompt, x_sample, cache_a_k, cache_a_v, cache_b_k, cache_b_v, state_ret, state_ssm, c, c_ctx, mod_w, mod_b, norm1_g, norm2_g, w_in, a_qn_g, a_kn_g, b_rel_bias, ret_decay, ret_gn_g, s5_lam_re, s5_lam_im, s5_log_dt, s5_b_re, s5_b_im, s5_c_re, s5_c_im, s5_d, s5_glu_w, w_out, moe_gw, moe_gb, moe_ew, moe_eb, moe_w1, moe_w3, moe_w2, final_norm_g):
    n_ctx = BATCH * SEQ
    n_lat = DEC_BATCH * DEC_SEQ
    lat_seg = DEC_SEQ // S5_SEG

    cond = jnp.zeros((SUBLANES, D_MODEL), F32).at[0].set(c_ctx).at[1:1 + DEC_BATCH].set(c)
    mods = _modulation(cond, mod_w, mod_b).reshape(DEPTH, SUBLANES, 6, D_MODEL)

    rope_tabs = _rope_tables()
    s5_a, s5_bm, s5_cre, s5_cim = _s5_prepare(s5_lam_re, s5_lam_im, s5_log_dt, s5_b_re, s5_b_im,
                                              s5_c_re, s5_c_im)
    cak = cache_a_k.reshape(DEC_BATCH, DEPTH, PAST_LEN, A_KV_HEADS * HEAD_DIM)
    cav = cache_a_v.reshape(DEC_BATCH, DEPTH, PAST_LEN, A_KV_HEADS * HEAD_DIM)
    cbk = cache_b_k.reshape(DEC_BATCH, DEPTH, PAST_LEN, B_HEADS * HEAD_DIM)
    cbv = cache_b_v.reshape(DEC_BATCH, DEPTH, PAST_LEN, B_HEADS * HEAD_DIM)

    xc = x_prompt.reshape(n_ctx, D_MODEL)
    xs = x_sample.reshape(n_lat, D_MODEL)
    w1_all, w3_all, w2_all = _moe_weights(moe_w1, moe_w3, moe_w2)
    eye_h = jnp.eye(C_HEADS, dtype=F32)
    s0_bd = (state_ret[:, :, :, :, :, None, :] * eye_h[None, None, None, :, None, :, None]).reshape(
        DEC_BATCH, DEPTH, 2, C_HEADS * HEAD_DIM, C_HEADS * HEAD_DIM)
    caches = ret_states = ssm_states = None
    h0_zero = jnp.zeros((2, SUBLANES, 2 * S5_SP), F32)
    w_in_bf = w_in.astype(BF16)
    wo_bf = w_out.astype(BF16)
    glu_bf = s5_glu_w.astype(BF16)
    for l in range(DEPTH):
        final = l == DEPTH - 1
        g1 = norm1_g[l].reshape(1, D_MODEL)
        g2 = norm2_g[l].reshape(1, D_MODEL)
        fg = final_norm_g.reshape(1, D_MODEL)
        qn = jnp.tile(a_qn_g[l], A_HEADS).reshape(1, 256)
        kn = jnp.tile(a_kn_g[l], A_KV_HEADS).reshape(1, 128)
        dec = jnp.broadcast_to(ret_decay[l].reshape(2 * C_HEADS, 1), (2 * C_HEADS, LANES))
        gn = ret_gn_g[l].reshape(1, 256)
        dvec = s5_d[l].reshape(1, GROUP_WIDTH)
        wr = jnp.zeros((D_MODEL, LANES), F32).at[:, :MOE_GROUPS].set(moe_gw[l]).at[
            :, ROUTER_OFF:ROUTER_OFF + MOE_EXPERTS].set(moe_ew[l])
        br = jnp.zeros((1, LANES), F32).at[0, :MOE_GROUPS].set(moe_gb[l]).at[
            0, ROUTER_OFF:ROUTER_OFF + MOE_EXPERTS].set(moe_eb[l])
        wr_hi = wr.astype(BF16)
        wr_lo = (wr - wr_hi.astype(F32)).astype(BF16)
        na_bias = _na_bias(b_rel_bias[l])

        zc, cg, du_tm, *caches = _project(xc, mods, 0, n_ctx, g1, w_in_bf, qn, kn, None, l, seq_len=SEQ,
                                          with_cache=True, prev_caches=caches)
        oa, ob = _ctx_attention(zc, BATCH, SEQ)
        oc, ret_states = _retention(zc, cg, dec, gn, None, l, nb=BATCH, seq_len=SEQ, prev_state=ret_states)
        od_tm, ssm_states = _s5(du_tm, h0_zero, s5_a, s5_bm, s5_cre, s5_cim, dvec, glu_bf, l,
                                nseg=1, fin_layer=l, fin_layers=DEPTH, prev_fin=ssm_states)
        xm, h2, route, counts = _output_stage(xc, (oa, ob, oc, od_tm), mods, 0, n_ctx, wo_bf, g2,
                                              wr_hi, wr_lo, br, l)
        xc = _moe(h2, route, counts, xm, mods, 0, n_ctx, w1_all, w3_all, w2_all, fg, l, final=final)

        zs, cg, du_tm = _project(xs, mods, 1, DEC_SEQ, g1, w_in_bf, qn, kn, rope_tabs, l, seq_len=DEC_SEQ)
        oa = _lat_attention_a(zs, cak, cav, l)
        ob = _lat_attention_b(zs, cbk, cbv, na_bias, l)
        oc = _retention(zs, cg, dec, gn, s0_bd, l, nb=DEC_BATCH, seq_len=DEC_SEQ)
        h0 = state_ssm[:, l].reshape(DEC_BATCH, 2, 2 * S5_SP).transpose(1, 0, 2)
        h0_seg = jnp.zeros((2, DEC_BATCH, lat_seg, 2 * S5_SP), F32)
        h0_seg = h0_seg.at[0, :, 0].set(h0[0]).at[1, :, lat_seg - 1].set(h0[1])
        od_tm, _ = _s5(du_tm, h0_seg.reshape(2, SUBLANES, 2 * S5_SP),
                       s5_a, s5_bm, s5_cre, s5_cim, dvec, glu_bf, l, nseg=lat_seg)
        xm, h2, route, counts = _output_stage(xs, (oa, ob, oc, od_tm), mods, 1, DEC_SEQ, wo_bf, g2,
                                              wr_hi, wr_lo, br, l)
        xs = _moe(h2, route, counts, xm, mods, 1, DEC_SEQ, w1_all, w3_all, w2_all, fg, l, final=final)

    new_ak, new_av, new_bk, new_bv = caches
    return (xc.reshape(BATCH, SEQ, D_MODEL), xs.reshape(DEC_BATCH, DEC_SEQ, D_MODEL),
            new_ak.reshape(BATCH, DEPTH, SEQ, A_KV_HEADS, HEAD_DIM),
            new_av.reshape(BATCH, DEPTH, SEQ, A_KV_HEADS, HEAD_DIM),
            new_bk.reshape(BATCH, DEPTH, SEQ, B_HEADS, HEAD_DIM),
            new_bv.reshape(BATCH, DEPTH, SEQ, B_HEADS, HEAD_DIM),
            ret_states,
            ssm_states.reshape(BATCH, DEPTH, 2, 2, S5_GROUPS, S5_STATE))
```

```python
import functools
import math

import numpy as np
import jax
import jax.numpy as jnp
from jax import lax
from jax.experimental import pallas as pl
from jax.experimental.pallas import tpu as pltpu

F32 = jnp.float32
BF16 = jnp.bfloat16

D_MODEL = 1024
BATCH = 32
SEQ = 256
DEPTH = 2
DEC_BATCH = 2
DEC_SEQ = 1024
PAST_LEN = 256
GRID_W = 64
HEAD_DIM = 64
GROUP_WIDTH = 256
A_HEADS = 4
A_KV_HEADS = 2
B_HEADS = 4
NA_ROWS = 8
NA_COLS = 16
C_HEADS = 4
S5_CH = 16
S5_GROUPS = 16
S5_STATE = 64
MOE_GROUPS = 4
MOE_PER_GROUP = 8
MOE_EXPERTS = 32
MOE_HIDDEN = 128
ROPE_THETA = 10000.0
EPS = 1e-6
IN_WIDTH = 2560
Q_SCALE = HEAD_DIM ** -0.5

OFF_AQ, OFF_AK, OFF_AV = 0, 256, 384
OFF_BQ, OFF_BK, OFF_BV = 512, 768, 1024
OFF_CQ, OFF_CK, OFF_CV, OFF_CG = 1280, 1536, 1792, 2048
OFF_DU = 2304

LANES = 128
SUBLANES = 8
BF16_ROWS = 16
S5_SP = S5_GROUPS * S5_STATE
S5_SEG = 256
S5_CHUNK = 256
ROUTER_OFF = 4
NEG_BIG = -1e30
VMEM_LIMIT = 56 * 1024 * 1024


def _cparams(*sem):
    return pltpu.CompilerParams(dimension_semantics=sem, vmem_limit_bytes=VMEM_LIMIT)


def _mod_spec(layer, first_row, tiles_per_row, grid_rank):
    if grid_rank == 1:
        return pl.BlockSpec((None, None, 6, D_MODEL), lambda i: (layer, first_row + i // tiles_per_row, 0, 0))
    return pl.BlockSpec((None, None, 6, D_MODEL), lambda i, g: (layer, first_row + i // tiles_per_row, 0, 0))


def _bdot(a, b):
    return jnp.dot(a.astype(BF16), b.astype(BF16), preferred_element_type=F32)


def _bdot_nt(a, b):
    return lax.dot_general(a.astype(BF16), b.astype(BF16), (((1,), (1,)), ((), ())),
                           preferred_element_type=F32)


def _bdot_tn(a, b):
    return lax.dot_general(a.astype(BF16), b.astype(BF16), (((0,), (0,)), ((), ())),
                           preferred_element_type=F32)


def _split(a):
    hi = a.astype(BF16)
    lo = (a - hi.astype(F32)).astype(BF16)
    return hi, lo


def _dot_hilo_lhs(a, b_bf16):
    hi, lo = _split(a)
    return (jnp.dot(hi, b_bf16, preferred_element_type=F32)
            + jnp.dot(lo, b_bf16, preferred_element_type=F32))


def _rms_rows(x):
    return x * lax.rsqrt(jnp.mean(x * x, axis=-1, keepdims=True) + EPS)


def _mod_kernel(cond_ref, w_ref, b_ref, o_ref):
    o_ref[...] = _bdot(jax.nn.silu(cond_ref[...]), w_ref[...]) + b_ref[...]


def _modulation(cond, mod_w, mod_b):
    tn = 1536
    return pl.pallas_call(
        _mod_kernel,
        grid=(DEPTH, 6 * D_MODEL // tn),
        in_specs=[pl.BlockSpec((SUBLANES, D_MODEL), lambda l, j: (0, 0)),
                  pl.BlockSpec((None, D_MODEL, tn), lambda l, j: (l, 0, j)),
                  pl.BlockSpec((None, 1, tn), lambda l, j: (l, 0, j))],
        out_specs=pl.BlockSpec((None, SUBLANES, tn), lambda l, j: (l, 0, j)),
        out_shape=jax.ShapeDtypeStruct((DEPTH, SUBLANES, 6 * D_MODEL), F32),
        compiler_params=_cparams("arbitrary", "arbitrary"),
        name="modulation",
    )(cond, mod_w, mod_b.reshape(DEPTH, 1, 6 * D_MODEL))


def _group_mean_matrix(w):
    ri = lax.broadcasted_iota(jnp.int32, (w, w), 0) // HEAD_DIM
    ci = lax.broadcasted_iota(jnp.int32, (w, w), 1) // HEAD_DIM
    return jnp.where(ri == ci, 1.0 / HEAD_DIM, 0.0).astype(BF16)


def _head_norm(t, g):
    ms = _dot_hilo_lhs(t * t, _group_mean_matrix(t.shape[1]))
    return t * lax.rsqrt(ms + EPS) * g


def _rope(t, cos, sa, sb):
    return (t * cos + pltpu.roll(t, LANES - 16, 1) * sa + pltpu.roll(t, 16, 1) * sb)


def _store_layer_slot(ref, slot, value):
    for s in range(ref.shape[0]):
        ref[s] = value if s == slot else jnp.zeros_like(value)


def _layer_slot_block(layer, first_call, tail):
    if first_call:
        return (None, DEPTH) + tail, (0,) * (1 + len(tail)), layer
    return (None, 1) + tail, (layer,) + (0,) * len(tail), 0


def _proj_kernel(*refs, rope, n_alias, with_cache, slot):
    x_ref, mod_ref, g1_ref, w_ref, qn_ref, kn_ref = refs[:6]
    n_in = 6
    if rope:
        cos_ref, sa_ref, sb_ref = refs[6:9]
        n_in = 9
    outs = refs[n_in + n_alias:]
    z_ref, cg_ref, du_ref = outs[:3]
    h = _rms_rows(x_ref[...]) * g1_ref[...] * (1.0 + mod_ref[1:2, :]) + mod_ref[0:1, :]
    z = jnp.dot(h.astype(BF16), w_ref[...], preferred_element_type=F32)
    aq = _head_norm(z[:, OFF_AQ:OFF_AK], qn_ref[...])
    ak = _head_norm(z[:, OFF_AK:OFF_AV], kn_ref[...])
    for j in range(3):
        t = aq[:, j * LANES:(j + 1) * LANES] if j < 2 else ak
        if rope:
            cj = 0 if j == 2 else j
            sl = slice(cj * LANES, (cj + 1) * LANES)
            t = _rope(t, cos_ref[:, sl], sa_ref[:, sl], sb_ref[:, sl])
        if j == 2:
            ak = t
        z_ref[:, j * LANES:(j + 1) * LANES] = t.astype(BF16)
    z_ref[:, OFF_AV:OFF_CK] = z[:, OFF_AV:OFF_CK].astype(BF16)
    z_ref[:, OFF_CK:OFF_CV] = (z[:, OFF_CK:OFF_CV] * Q_SCALE).astype(BF16)
    z_ref[:, OFF_CV:OFF_CG] = z[:, OFF_CV:OFF_CG].astype(BF16)
    cg_ref[...] = z[:, OFF_CG:OFF_DU]
    du_ref[...] = z[:, OFF_DU:]
    if with_cache:
        ak_ref, av_ref, bk_ref, bv_ref = outs[3:7]
        _store_layer_slot(ak_ref, slot, ak)
        _store_layer_slot(av_ref, slot, z[:, OFF_AV:OFF_BQ])
        _store_layer_slot(bk_ref, slot, z[:, OFF_BK:OFF_BV])
        _store_layer_slot(bv_ref, slot, z[:, OFF_BV:OFF_CQ])


def _du_spec(grid_rank):
    if grid_rank == 1:
        return pl.BlockSpec((None, S5_SEG, GROUP_WIDTH), lambda i: (i // SUBLANES, 0, i % SUBLANES))
    return pl.BlockSpec((None, S5_SEG, GROUP_WIDTH), lambda i, g: (i // SUBLANES, 0, i % SUBLANES))


def _project(x, mods, mod_row, mod_tokens, g1, w_in_bf, qn, kn, rope_tabs, layer, *, seq_len,
             with_cache=False, prev_caches=None):
    tm = S5_SEG
    n = x.shape[0]
    rope = rope_tabs is not None
    in_specs = [pl.BlockSpec((tm, D_MODEL), lambda i: (i, 0)),
                _mod_spec(layer, mod_row, mod_tokens // tm, 1),
                pl.BlockSpec((1, D_MODEL), lambda i: (0, 0)),
                pl.BlockSpec((None, D_MODEL, IN_WIDTH), lambda i: (layer, 0, 0)),
                pl.BlockSpec((1, 256), lambda i: (0, 0)),
                pl.BlockSpec((1, 128), lambda i: (0, 0))]
    args = [x, mods, g1, w_in_bf, qn, kn]
    if rope:
        tps = seq_len // tm
        in_specs += [pl.BlockSpec((tm, 256), lambda i: (i % tps, 0))] * 3
        args += list(rope_tabs)
    out_specs = [pl.BlockSpec((tm, OFF_CG), lambda i: (i, 0)),
                 pl.BlockSpec((tm, GROUP_WIDTH), lambda i: (i, 0)), _du_spec(1)]
    out_shape = [jax.ShapeDtypeStruct((n, OFF_CG), BF16),
                 jax.ShapeDtypeStruct((n, GROUP_WIDTH), F32),
                 jax.ShapeDtypeStruct((n // (tm * SUBLANES), S5_SEG, SUBLANES * GROUP_WIDTH), F32)]
    aliases = {}
    n_alias = 0
    slot = 0
    if with_cache:
        assert tm == seq_len
        nb = n // seq_len
        for w in (128, 128, 256, 256):
            blk, idx, slot = _layer_slot_block(layer, prev_caches is None, (seq_len, w))
            out_specs.append(pl.BlockSpec(blk, lambda i, idx=idx: (i,) + idx))
            out_shape.append(jax.ShapeDtypeStruct((nb, DEPTH, seq_len, w), F32))
        if prev_caches is not None:
            n_alias = len(prev_caches)
            for k, arr in enumerate(prev_caches):
                aliases[len(args)] = 3 + k
                in_specs.append(pl.BlockSpec(memory_space=pl.ANY))
                args.append(arr)
    return pl.pallas_call(
        functools.partial(_proj_kernel, rope=rope, n_alias=n_alias, with_cache=with_cache, slot=slot),
        grid=(n // tm,),
        in_specs=in_specs,
        out_specs=out_specs,
        out_shape=out_shape,
        input_output_aliases=aliases,
        compiler_params=_cparams("parallel"),
        name="project",
    )(*args)


def _rope_tables():
    t = jnp.arange(DEC_SEQ)
    row = (t // GRID_W).astype(F32)
    col = (t % GRID_W).astype(F32)
    nf = HEAD_DIM // 4
    inv = ROPE_THETA ** (-jnp.arange(nf, dtype=F32) / nf)
    ang_r = row[:, None] * inv[None, :]
    ang_c = col[:, None] * inv[None, :]
    zeros = jnp.zeros_like(ang_r)
    cos = jnp.concatenate([jnp.cos(ang_r), jnp.cos(ang_r), jnp.cos(ang_c), jnp.cos(ang_c)], axis=-1)
    sa = jnp.concatenate([-jnp.sin(ang_r), zeros, -jnp.sin(ang_c), zeros], axis=-1)
    sb = jnp.concatenate([zeros, jnp.sin(ang_r), zeros, jnp.sin(ang_c)], axis=-1)
    return tuple(jnp.tile(a, (1, 4)) for a in (cos, sa, sb))


N_HEADS = 4


def _lane_head(width):
    return lax.broadcasted_iota(jnp.int32, (1, width), 1) // HEAD_DIM


def _stack_heads(q):
    head = _lane_head(q.shape[1])
    return jnp.concatenate([jnp.where(head == h, q, 0.0) for h in range(N_HEADS)], axis=0).astype(BF16)


def _stack_heads_gqa(q):
    lo = lax.broadcasted_iota(jnp.int32, (1, LANES), 1) < HEAD_DIM
    q = q.astype(F32)
    q01, q23 = q[:, :LANES], q[:, LANES:]
    blocks = [jnp.where(lo, q01, 0.0), jnp.where(lo, pltpu.roll(q01, HEAD_DIM, 1), 0.0),
              jnp.where(lo, 0.0, pltpu.roll(q23, HEAD_DIM, 1)), jnp.where(lo, 0.0, q23)]
    return jnp.concatenate(blocks, axis=0).astype(BF16)


def _spread_kv_gqa(v):
    lo = lax.broadcasted_iota(jnp.int32, (1, LANES), 1) < HEAD_DIM
    v = v.astype(F32)
    vr = pltpu.roll(v, HEAD_DIM, 1)
    return jnp.concatenate([jnp.where(lo, v, vr), jnp.where(lo, vr, v)], axis=1)


def _mha(qs, blocks, tq):
    scores = []
    for k, _, bias in blocks:
        s = _bdot_nt(qs, k)
        scores.append(s if bias is None else s + bias)
    m = functools.reduce(jnp.maximum, [jnp.max(s, axis=-1, keepdims=True) for s in scores])
    es = [jnp.exp(s - m) for s in scores]
    denom = functools.reduce(jnp.add, [jnp.sum(e, axis=-1, keepdims=True) for e in es])
    ps = [e.astype(BF16) for e in es]
    head = _lane_head(N_HEADS * HEAD_DIM)
    vals = [v.astype(BF16) for _, v, _ in blocks]
    o = None
    dall = None
    for h in range(N_HEADS):
        rows = slice(h * tq, (h + 1) * tq)
        for p, v in zip(ps, vals):
            t = jnp.dot(p[rows], jnp.where(head == h, v, jnp.zeros_like(v)), preferred_element_type=F32)
            o = t if o is None else o + t
        d = jnp.where(head == h, denom[rows], 0.0)
        dall = d if dall is None else dall + d
    return (o / dall).astype(BF16)


CTX_SEQS = 2


def _ctx_attn_kernel(aq_ref, ak_ref, av_ref, bq_ref, bk_ref, bv_ref, oa_ref, ob_ref):
    tq = aq_ref.shape[0] // CTX_SEQS
    oa, ob = [], []
    for s in range(CTX_SEQS):
        rows = slice(s * tq, (s + 1) * tq)
        oa.append(_mha(_stack_heads_gqa(aq_ref[rows, :] * Q_SCALE),
                       [(ak_ref[rows, :], _spread_kv_gqa(av_ref[rows, :]), None)], tq))
        ob.append(_mha(_stack_heads(bq_ref[rows, :] * Q_SCALE), [(bk_ref[rows, :], bv_ref[rows, :], None)], tq))
    oa_ref[...] = jnp.concatenate(oa, axis=0)
    ob_ref[...] = jnp.concatenate(ob, axis=0)


def _ctx_attention(z, nb, seq_len):
    rows = CTX_SEQS * seq_len

    def col(width, off):
        return pl.BlockSpec((rows, width), lambda b: (b, off // width))
    return pl.pallas_call(
        _ctx_attn_kernel,
        grid=(nb // CTX_SEQS,),
        in_specs=[col(256, OFF_AQ), col(128, OFF_AK), col(128, OFF_AV),
                  col(256, OFF_BQ), col(256, OFF_BK), col(256, OFF_BV)],
        out_specs=[pl.BlockSpec((rows, 256), lambda b: (b, 0))] * 2,
        out_shape=[jax.ShapeDtypeStruct((nb * seq_len, 256), BF16)] * 2,
        compiler_params=_cparams("parallel"),
        name="ctx_attention",
    )(z, z, z, z, z, z)


def _lat_attn_a_kernel(q_ref, kn_ref, vn_ref, kc_ref, vc_ref, o_ref):
    o_ref[...] = _mha(_stack_heads_gqa(q_ref[...] * Q_SCALE),
                      [(kc_ref[...], _spread_kv_gqa(vc_ref[...]), None),
                       (kn_ref[...], _spread_kv_gqa(vn_ref[...]), None)], q_ref.shape[0])


def _lat_attention_a(z, cache_k, cache_v, layer, tq=256):
    nq = DEC_SEQ // tq
    cache_spec = pl.BlockSpec((None, None, PAST_LEN, 128), lambda b, j: (b, layer, 0, 0))
    return pl.pallas_call(
        _lat_attn_a_kernel,
        grid=(DEC_BATCH, nq),
        in_specs=[pl.BlockSpec((tq, 256), lambda b, j: (b * nq + j, OFF_AQ // 256)),
                  pl.BlockSpec((DEC_SEQ, 128), lambda b, j: (b, OFF_AK // 128)),
                  pl.BlockSpec((DEC_SEQ, 128), lambda b, j: (b, OFF_AV // 128)),
                  cache_spec, cache_spec],
        out_specs=pl.BlockSpec((tq, 256), lambda b, j: (b * nq + j, 0)),
        out_shape=jax.ShapeDtypeStruct((DEC_BATCH * DEC_SEQ, 256), BF16),
        compiler_params=_cparams("parallel", "parallel"),
        name="lat_attention_a",
    )(z, z, z, cache_k, cache_v)


NA_KEYS = NA_ROWS * GRID_W


NA_PAIRS = 2 * NA_ROWS - 2


NA_STEP_ROWS = 2


def _na_kernel(q_ref, k_ref, v_ref, kc_ref, vc_ref, bias_ref, o_ref):
    rows = DEC_SEQ // GRID_W
    outs = []
    for rr in range(NA_STEP_ROWS):
        r = pl.program_id(1) * NA_STEP_ROWS + rr
        row_start = jnp.clip(r - NA_ROWS // 2, 0, rows - NA_ROWS)
        start = pl.multiple_of(row_start * GRID_W, GRID_W)
        rel0 = row_start - r + NA_ROWS - 1
        kl = k_ref[pl.ds(start, NA_KEYS), :]
        vl = v_ref[pl.ds(start, NA_KEYS), :]
        bias = jnp.concatenate(
            [jnp.concatenate([bias_ref[h, rel0 + 2 * jp] for jp in range(NA_ROWS // 2)], axis=1)
             for h in range(B_HEADS)], axis=0)
        qrows = slice(rr * GRID_W, (rr + 1) * GRID_W)
        outs.append(_mha(_stack_heads(q_ref[qrows, :] * Q_SCALE),
                         [(kl, vl, bias), (kc_ref[...], vc_ref[...], None)], GRID_W))
    o_ref[...] = jnp.concatenate(outs, axis=0)


def _na_bias(rel_bias):
    nrel = 2 * NA_COLS - 1
    period = 2 * GRID_W
    b = rel_bias.astype(F32)
    ext = jnp.concatenate([b[..., NA_COLS - 1:],
                           jnp.zeros(b.shape[:-1] + (period - nrel,), F32),
                           b[..., :NA_COLS - 1]], axis=-1)
    flat = jnp.tile(ext, (1, 1, GRID_W))[..., :GRID_W * (period - 1)]
    toe = flat.reshape(b.shape[:-1] + (GRID_W, period - 1))[..., :GRID_W]
    col_start = np.clip(np.arange(GRID_W) - NA_COLS // 2, 0, GRID_W - NA_COLS)
    kc = np.arange(GRID_W)
    inside = (kc[None, :] >= col_start[:, None]) & (kc[None, :] < col_start[:, None] + NA_COLS)
    toe = jnp.where(jnp.asarray(inside), toe, NEG_BIG)
    return jnp.concatenate([toe[:, :-1], toe[:, 1:]], axis=-1)


def _lat_attention_b(z, cache_k, cache_v, bias, layer):
    rows = DEC_SEQ // GRID_W // NA_STEP_ROWS
    tq = NA_STEP_ROWS * GRID_W
    cache_spec = pl.BlockSpec((None, None, PAST_LEN, 256), lambda b, r: (b, layer, 0, 0))
    return pl.pallas_call(
        _na_kernel,
        grid=(DEC_BATCH, rows),
        in_specs=[pl.BlockSpec((tq, 256), lambda b, r: (b * rows + r, OFF_BQ // 256)),
                  pl.BlockSpec((DEC_SEQ, 256), lambda b, r: (b, OFF_BK // 256)),
                  pl.BlockSpec((DEC_SEQ, 256), lambda b, r: (b, OFF_BV // 256)),
                  cache_spec, cache_spec,
                  pl.BlockSpec((B_HEADS, NA_PAIRS, GRID_W, 2 * GRID_W), lambda b, r: (0, 0, 0, 0))],
        out_specs=pl.BlockSpec((tq, 256), lambda b, r: (b * rows + r, 0)),
        out_shape=jax.ShapeDtypeStruct((DEC_BATCH * DEC_SEQ, 256), BF16),
        compiler_params=_cparams("parallel", "parallel"),
        name="lat_attention_b",
    )(z, z, z, cache_k, cache_v, bias)


def _retention_kernel(q_ref, g_ref, k_ref, v_ref, dec_ref, gn_ref, *rest, seq_len, tq, has_state,
                      hoist_decay, slot):
    if has_state:
        s0_ref, o_ref, dec_scr = rest
    else:
        o_ref, st_ref, dec_scr = rest[-3:]
    head = _lane_head(C_HEADS * HEAD_DIM)
    lg = jax.nn.log_sigmoid(dec_ref[...])

    def per_lane(row0):
        out = jnp.zeros((1, C_HEADS * HEAD_DIM), F32)
        for h in range(C_HEADS):
            out = jnp.where(head == h, lg[row0 + h:row0 + h + 1, 0:1], out)
        return out

    lgf_l, lgb_l = per_lane(0), per_lane(C_HEADS)
    i0 = pl.program_id(1) * tq
    qi = (i0 + lax.broadcasted_iota(jnp.int32, (tq, 1), 0)).astype(F32)

    def fill_decay():
        kj = lax.broadcasted_iota(jnp.int32, (1, seq_len), 1).astype(F32)
        diff = qi - kj
        for h in range(C_HEADS):
            lgf = lg[h:h + 1, 0:1]
            lgb = lg[C_HEADS + h:C_HEADS + h + 1, 0:1]
            dec_scr[h * tq:(h + 1) * tq, :] = (
                jnp.where(diff >= 0, jnp.exp(lgf * jnp.maximum(diff, 0.0)), 0.0)
                + jnp.where(diff <= 0, jnp.exp(lgb * jnp.maximum(-diff, 0.0)), 0.0))

    if hoist_decay:
        pl.when(pl.program_id(0) == 0)(fill_decay)
    else:
        fill_decay()

    q = q_ref[...]
    k = k_ref[...]
    v = v_ref[...].astype(BF16)
    sc = (_bdot_nt(_stack_heads(q), k) * dec_scr[...]).astype(BF16)
    o = None
    for h in range(C_HEADS):
        t = jnp.dot(sc[h * tq:(h + 1) * tq], jnp.where(head == h, v, jnp.zeros_like(v)),
                    preferred_element_type=F32)
        o = t if o is None else o + t
    if has_state:
        o = (o + _bdot(q, s0_ref[0]) * jnp.exp(lgf_l * (qi + 1.0))
             + _bdot(q, s0_ref[1]) * jnp.exp(lgb_l * (seq_len - qi)))
    gm = _group_mean_matrix(C_HEADS * HEAD_DIM)
    dlt = o - _dot_hilo_lhs(o, gm)
    var = _dot_hilo_lhs(dlt * dlt, gm)
    o_ref[...] = (dlt * lax.rsqrt(var + EPS) * gn_ref[...] * jax.nn.silu(g_ref[...])).astype(BF16)
    if not has_state:
        kpos = lax.broadcasted_iota(jnp.int32, (seq_len, 1), 0).astype(F32)
        sf = _bdot_tn(k * jnp.exp(lgf_l * (seq_len - 1.0 - kpos)), v)
        sb = _bdot_tn(k * jnp.exp(lgb_l * kpos), v)
        for s in range(st_ref.shape[0]):
            for h in range(C_HEADS):
                sl = slice(h * HEAD_DIM, (h + 1) * HEAD_DIM)
                st_ref[s, 0, h] = sf[sl, sl] if s == slot else jnp.zeros((HEAD_DIM, HEAD_DIM), F32)
                st_ref[s, 1, h] = sb[sl, sl] if s == slot else jnp.zeros((HEAD_DIM, HEAD_DIM), F32)


def _retention(z, cg, dec, gn, s0, layer, *, nb, seq_len, prev_state=None, tq=256):
    nq = seq_len // tq
    has_state = s0 is not None
    aliases = {}
    slot = 0
    in_specs = [pl.BlockSpec((tq, 256), lambda b, j: (b * nq + j, OFF_CQ // 256)),
                pl.BlockSpec((tq, 256), lambda b, j: (b * nq + j, 0)),
                pl.BlockSpec((seq_len, 256), lambda b, j: (b, OFF_CK // 256)),
                pl.BlockSpec((seq_len, 256), lambda b, j: (b, OFF_CV // 256)),
                pl.BlockSpec((SUBLANES, LANES), lambda b, j: (0, 0)),
                pl.BlockSpec((1, 256), lambda b, j: (0, 0))]
    args = [z, cg, z, z, dec, gn]
    o_spec = pl.BlockSpec((tq, 256), lambda b, j: (b * nq + j, 0))
    o_shape = jax.ShapeDtypeStruct((nb * seq_len, 256), BF16)
    if has_state:
        in_specs.append(pl.BlockSpec((None, None, 2, 256, 256), lambda b, j: (b, layer, 0, 0, 0)))
        args.append(s0)
        out_specs, out_shape = o_spec, o_shape
    else:
        assert nq == 1
        blk, idx, slot = _layer_slot_block(layer, prev_state is None, (2, C_HEADS, HEAD_DIM, HEAD_DIM))
        out_specs = [o_spec, pl.BlockSpec(blk, lambda b, j: (b,) + idx)]
        out_shape = [o_shape, jax.ShapeDtypeStruct((nb, DEPTH, 2, C_HEADS, HEAD_DIM, HEAD_DIM), F32)]
        if prev_state is not None:
            aliases[len(args)] = 1
            in_specs.append(pl.BlockSpec(memory_space=pl.ANY))
            args.append(prev_state)
    return pl.pallas_call(
        functools.partial(_retention_kernel, seq_len=seq_len, tq=tq, has_state=has_state,
                          hoist_decay=nq == 1, slot=slot),
        grid=(nb, nq),
        in_specs=in_specs,
        out_specs=out_specs,
        out_shape=out_shape,
        scratch_shapes=[pltpu.VMEM((C_HEADS * tq, seq_len), F32)],
        input_output_aliases=aliases,
        compiler_params=_cparams("arbitrary", "arbitrary"),
        name="retention",
    )(*args)


def _s5_prep_kernel(lre_ref, lim_ref, ldt_ref, bre_ref, bim_ref, cre_ref, cim_ref,
                    a_ref, bm_ref, cro_ref, cio_ref, bm_scr, cr_scr, ci_scr):
    lre = lre_ref[...]
    lim = lim_ref[...]
    dt = jnp.exp(ldt_ref[...])
    mag = jnp.exp(lre * dt)
    a_re = mag * jnp.cos(lim * dt)
    a_im = mag * jnp.sin(lim * dt)
    den = lre * lre + lim * lim
    r_re = ((a_re - 1.0) * lre + a_im * lim) / den
    r_im = (a_im * lre - (a_re - 1.0) * lim) / den
    bm_scr[...] = jnp.zeros_like(bm_scr)
    cr_scr[...] = jnp.zeros_like(cr_scr)
    ci_scr[...] = jnp.zeros_like(ci_scr)
    for g in range(S5_GROUPS):
        rows = slice(g * S5_CH, (g + 1) * S5_CH)
        cols = slice(g * S5_STATE, (g + 1) * S5_STATE)
        a_ref[0:1, cols] = a_re[g:g + 1, :]
        a_ref[1:2, cols] = a_im[g:g + 1, :]
        rr, ri = r_re[g:g + 1, :], r_im[g:g + 1, :]
        br, bi = bre_ref[g], bim_ref[g]
        bm_scr[rows, cols] = rr * br - ri * bi
        bm_scr[rows, S5_SP + g * S5_STATE:S5_SP + (g + 1) * S5_STATE] = rr * bi + ri * br
        cr_scr[cols, rows] = cre_ref[g]
        ci_scr[cols, rows] = cim_ref[g]
    bm_ref[...] = bm_scr[...].astype(BF16)
    cro_ref[...] = cr_scr[...].astype(BF16)
    cio_ref[...] = ci_scr[...].astype(BF16)


def _s5_prepare(lam_re, lam_im, log_dt, b_re, b_im, c_re, c_im):
    gp = (S5_GROUPS, S5_STATE)
    ldt = jnp.broadcast_to(log_dt[..., None], (DEPTH, 2) + gp)
    bt = [jnp.swapaxes(t, -1, -2) for t in (b_re, b_im)]
    ct = [jnp.swapaxes(t, -1, -2) for t in (c_re, c_im)]

    def spec(*tail):
        return pl.BlockSpec((None, None) + tail, lambda l, d: (l, d) + (0,) * len(tail))

    return pl.pallas_call(
        _s5_prep_kernel,
        grid=(DEPTH, 2),
        in_specs=[spec(*gp)] * 3 + [spec(S5_GROUPS, S5_CH, S5_STATE)] * 2 + [spec(S5_GROUPS, S5_STATE, S5_CH)] * 2,
        out_specs=[spec(2, S5_SP), spec(GROUP_WIDTH, 2 * S5_SP), spec(S5_SP, GROUP_WIDTH), spec(S5_SP, GROUP_WIDTH)],
        out_shape=[jax.ShapeDtypeStruct((DEPTH, 2, 2, S5_SP), F32),
                   jax.ShapeDtypeStruct((DEPTH, 2, GROUP_WIDTH, 2 * S5_SP), BF16),
                   jax.ShapeDtypeStruct((DEPTH, 2, S5_SP, GROUP_WIDTH), BF16),
                   jax.ShapeDtypeStruct((DEPTH, 2, S5_SP, GROUP_WIDTH), BF16)],
        scratch_shapes=[pltpu.VMEM((GROUP_WIDTH, 2 * S5_SP), F32), pltpu.VMEM((S5_SP, GROUP_WIDTH), F32),
                        pltpu.VMEM((S5_SP, GROUP_WIDTH), F32)],
        compiler_params=_cparams("parallel", "parallel"),
        name="s5_prepare",
    )(lam_re, lam_im, ldt, bt[0], bt[1], ct[0], ct[1])


def _cmul(ar, ai, br, bi):
    return ar * br - ai * bi, ar * bi + ai * br


def _s5_kernel(u_ref, h0_ref, a_ref, bm_ref, cre_ref, cim_ref, dvec_ref, glu_ref, *rest, nseg, slot):
    od_ref, fin_ref, x_scr, s_scr, y_scr = rest[-5:]
    steps = S5_SEG
    rows = steps * SUBLANES
    chunk = S5_CHUNK
    chunk_steps = chunk // SUBLANES
    nchunk = rows // chunk
    seg = lax.broadcasted_iota(jnp.int32, (SUBLANES, S5_SP), 0) % nseg

    for d in range(2):
        ar = jnp.broadcast_to(a_ref[d, 0:1, :], (SUBLANES, S5_SP))
        ai = jnp.broadcast_to(a_ref[d, 1:2, :], (SUBLANES, S5_SP))

        def row0(k):
            c = k if d == 0 else nchunk - 1 - k
            return c * chunk if isinstance(c, int) else pl.multiple_of(c * chunk, chunk)

        def input_part(k, buf):
            x_scr[buf] = jnp.dot(u_ref[pl.ds(row0(k), chunk), :].astype(BF16), bm_ref[d],
                                 preferred_element_type=F32)

        def scan_part(buf, carry, store):
            sr, si = carry
            for t in range(chunk_steps):
                r = (t if d == 0 else chunk_steps - 1 - t) * SUBLANES
                pr, pi = _cmul(ar, ai, sr, si)
                sr = pr + x_scr[buf, r:r + SUBLANES, 0:S5_SP]
                si = pi + x_scr[buf, r:r + SUBLANES, S5_SP:]
                if store:
                    s_scr[buf, r:r + SUBLANES, 0:S5_SP] = sr
                    s_scr[buf, r:r + SUBLANES, S5_SP:] = si
            return sr, si

        def output_part(k, buf):
            y = _bdot(s_scr[buf, :, 0:S5_SP], cre_ref[d]) - _bdot(s_scr[buf, :, S5_SP:], cim_ref[d])
            rows_k = pl.ds(row0(k), chunk)
            if d == 0:
                y_scr[rows_k, :] = y
            else:
                zz = jax.nn.gelu(y_scr[rows_k, :] + y + dvec_ref[...] * u_ref[rows_k, :])
                od_ref[rows_k, :] = (zz * jax.nn.sigmoid(_bdot(zz, glu_ref[...]))).astype(BF16)

        def half(k, buf, carry, store, nxt=True, prev=True):
            if nxt:
                input_part(k + 1, 1 - buf)
            carry = scan_part(buf, carry, store)
            if store and prev:
                output_part(k - 1, 1 - buf)
            return carry

        def run_pass(carry, store):
            input_part(0, 0)
            carry = half(0, 0, carry, store, prev=False)
            carry = half(1, 1, carry, store)

            def pair(j, c):
                c = half(2 * j, 0, c, store)
                return half(2 * j + 1, 1, c, store)
            carry = lax.fori_loop(1, nchunk // 2 - 1, pair, carry)
            carry = half(nchunk - 2, 0, carry, store)
            carry = half(nchunk - 1, 1, carry, store, nxt=False)
            if store:
                output_part(nchunk - 1, 1)
            return carry

        init = (h0_ref[d, :, 0:S5_SP], h0_ref[d, :, S5_SP:])
        if nseg > 1:
            zero = jnp.zeros((SUBLANES, S5_SP), F32)
            fr, fi = run_pass((zero, zero), store=False)
            pr, pi = ar, ai
            for _ in range(int(math.log2(steps))):
                pr, pi = _cmul(pr, pi, pr, pi)
            cr, ci = init
            shift = 1 if d == 0 else SUBLANES - 1
            order = range(1, nseg) if d == 0 else range(nseg - 2, -1, -1)
            for s in order:
                ncr, nci = pltpu.roll(cr, shift, 0), pltpu.roll(ci, shift, 0)
                nfr, nfi = pltpu.roll(fr, shift, 0), pltpu.roll(fi, shift, 0)
                qr, qi = _cmul(pr, pi, ncr, nci)
                cr = jnp.where(seg == s, qr + nfr, cr)
                ci = jnp.where(seg == s, qi + nfi, ci)
            init = (cr, ci)
        sr, si = run_pass(init, store=True)
        for s in range(fin_ref.shape[1] // (4 * S5_SP)):
            base = (4 * s + 2 * d) * S5_SP
            fin_ref[:, base:base + S5_SP] = sr if s == slot else jnp.zeros_like(sr)
            fin_ref[:, base + S5_SP:base + 2 * S5_SP] = si if s == slot else jnp.zeros_like(si)


def _s5(du_tm, h0, a, bmat, cre, cim, dvec, glu_bf, layer, *, nseg, fin_layer=0, fin_layers=1,
        prev_fin=None):
    nblk = du_tm.shape[0]
    rows = S5_SEG * SUBLANES
    fin_w = 4 * S5_SP
    in_specs = [pl.BlockSpec((None, rows, GROUP_WIDTH), lambda i: (i, 0, 0)),
                pl.BlockSpec((2, SUBLANES, 2 * S5_SP), lambda i: (0, 0, 0)),
                pl.BlockSpec((None, 2, 2, S5_SP), lambda i: (layer, 0, 0, 0)),
                pl.BlockSpec((None, 2, GROUP_WIDTH, 2 * S5_SP), lambda i: (layer, 0, 0, 0)),
                pl.BlockSpec((None, 2, S5_SP, GROUP_WIDTH), lambda i: (layer, 0, 0, 0)),
                pl.BlockSpec((None, 2, S5_SP, GROUP_WIDTH), lambda i: (layer, 0, 0, 0)),
                pl.BlockSpec((1, GROUP_WIDTH), lambda i: (0, 0)),
                pl.BlockSpec((None, GROUP_WIDTH, GROUP_WIDTH), lambda i: (layer, 0, 0))]
    args = [du_tm.reshape(nblk, rows, GROUP_WIDTH), h0, a, bmat, cre, cim, dvec, glu_bf]
    aliases = {}
    if prev_fin is not None:
        aliases[len(args)] = 1
        in_specs.append(pl.BlockSpec(memory_space=pl.ANY))
        args.append(prev_fin)
        fin_spec, slot = pl.BlockSpec((SUBLANES, fin_w), lambda i: (i, fin_layer)), 0
    else:
        fin_spec, slot = pl.BlockSpec((SUBLANES, fin_layers * fin_w), lambda i: (i, 0)), fin_layer
    od, fin = pl.pallas_call(
        functools.partial(_s5_kernel, nseg=nseg, slot=slot),
        grid=(nblk,),
        in_specs=in_specs,
        out_specs=[pl.BlockSpec((None, rows, GROUP_WIDTH), lambda i: (i, 0, 0)), fin_spec],
        out_shape=[jax.ShapeDtypeStruct((nblk, rows, GROUP_WIDTH), BF16),
                   jax.ShapeDtypeStruct((nblk * SUBLANES, fin_layers * fin_w), F32)],
        scratch_shapes=[pltpu.VMEM((2, S5_CHUNK, 2 * S5_SP), F32), pltpu.VMEM((2, S5_CHUNK, 2 * S5_SP), F32),
                        pltpu.VMEM((rows, GROUP_WIDTH), F32)],
        input_output_aliases=aliases,
        compiler_params=_cparams("parallel"),
        name="s5",
    )(*args)
    return od.reshape(nblk, S5_SEG, SUBLANES * GROUP_WIDTH), fin


ROUTE_GROUP = MOE_PER_GROUP
OUT_SEQS = 2


def _out_kernel(x_ref, oa_ref, ob_ref, oc_ref, od_ref, mod_ref, wo_ref, g2_ref, wrh_ref, wrl_ref, br_ref,
                xm_ref, h2_ref, route_ref, cnt_ref):
    od = jnp.concatenate([od_ref[:, s * GROUP_WIDTH:(s + 1) * GROUP_WIDTH] for s in range(OUT_SEQS)], axis=0)
    mix = functools.reduce(jnp.add, [
        _bdot(o, wo_ref[i * GROUP_WIDTH:(i + 1) * GROUP_WIDTH, :])
        for i, o in enumerate((oa_ref[...], ob_ref[...], oc_ref[...], od))])
    xm = x_ref[...] + mod_ref[2:3, :] * mix
    xm_ref[...] = xm
    h2 = _rms_rows(xm) * g2_ref[...] * (1.0 + mod_ref[4:5, :]) + mod_ref[3:4, :]
    h2_ref[...] = h2.astype(BF16)

    h_hi, h_lo = _split(h2)
    logits = (jnp.dot(h_hi, wrh_ref[...], preferred_element_type=F32)
              + jnp.dot(h_hi, wrl_ref[...], preferred_element_type=F32)
              + jnp.dot(h_lo, wrh_ref[...], preferred_element_type=F32)) + br_ref[...]
    lane_i = lax.broadcasted_iota(jnp.int32, logits.shape, 1)
    lane = lane_i.astype(F32)
    big = jnp.float32(2 ** 30)
    gmask = lane_i < MOE_GROUPS
    gl = jnp.where(gmask, logits, -jnp.inf)
    gmax = jnp.max(gl, axis=-1, keepdims=True)
    p_top = 1.0 / jnp.sum(jnp.exp(gl - gmax), axis=-1, keepdims=True)
    g_top = jnp.min(jnp.where(gl == gmax, lane, big), axis=-1, keepdims=True)
    e_lane = lane_i - ROUTER_OFF
    lane_group = (e_lane // MOE_PER_GROUP).astype(F32)
    emask = (e_lane >= 0) & (e_lane < MOE_EXPERTS) & (lane_group == g_top)
    el = jnp.where(emask, logits, -jnp.inf)
    m1 = jnp.max(el, axis=-1, keepdims=True)
    i1 = jnp.min(jnp.where(el == m1, lane, big), axis=-1, keepdims=True)
    el2 = jnp.where(lane == i1, -jnp.inf, el)
    m2 = jnp.max(el2, axis=-1, keepdims=True)
    i2 = jnp.min(jnp.where(el2 == m2, lane, big), axis=-1, keepdims=True)
    e2 = jnp.exp(m2 - m1)
    den = 1.0 + e2
    gates = (jnp.where(lane == i1, (1.0 / den) * p_top, 0.0)
             + jnp.where(lane == i2, (e2 / den) * p_top, 0.0))
    route = jnp.where(lane == ROUTE_GROUP + g_top, 1.0, 0.0)
    for g in range(MOE_GROUPS):
        local = pltpu.roll(gates, LANES - ROUTER_OFF - g * MOE_PER_GROUP, 1)
        route = route + jnp.where((g_top == g) & (lane_i < MOE_PER_GROUP), local, 0.0)
    route_ref[...] = route
    cnt_ref[...] = jnp.broadcast_to(jnp.sum(route, axis=0, keepdims=True), (SUBLANES, LANES)).astype(jnp.int32)


def _output_stage(x, mixes, mods, mod_row, mod_tokens, wo_bf, g2, wr_hi, wr_lo, br, layer):
    tm = OUT_SEQS * S5_SEG
    n = x.shape[0]
    row = lambda w: pl.BlockSpec((tm, w), lambda i: (i, 0))
    const = lambda shape: pl.BlockSpec(shape, lambda i: (0,) * len(shape))
    per_blk = SUBLANES // OUT_SEQS
    return pl.pallas_call(
        _out_kernel,
        grid=(n // tm,),
        in_specs=[row(D_MODEL), row(256), row(256), row(256),
                  pl.BlockSpec((None, S5_SEG, OUT_SEQS * GROUP_WIDTH), lambda i: (i // per_blk, 0, i % per_blk)),
                  _mod_spec(layer, mod_row, mod_tokens // tm, 1),
                  pl.BlockSpec((None, D_MODEL, D_MODEL), lambda i: (layer, 0, 0)), const((1, D_MODEL)),
                  const((D_MODEL, LANES)), const((D_MODEL, LANES)), const((1, LANES))],
        out_specs=[row(D_MODEL), row(D_MODEL), row(LANES),
                   pl.BlockSpec((None, SUBLANES, LANES), lambda i: (i, 0, 0))],
        out_shape=[jax.ShapeDtypeStruct((n, D_MODEL), F32),
                   jax.ShapeDtypeStruct((n, D_MODEL), BF16),
                   jax.ShapeDtypeStruct((n, LANES), F32),
                   jax.ShapeDtypeStruct((n // tm, SUBLANES, LANES), jnp.int32)],
        compiler_params=_cparams("parallel"),
        name="output_stage",
    )(x, *mixes, mods, wo_bf, g2, wr_hi, wr_lo, br)


GROUP_HID = MOE_PER_GROUP * MOE_HIDDEN


MOE_CHUNK = 256


def _moe_kernel(cnt_ref, h2_ref, route_ref, xm_ref, mod_ref, w1_ref, w3_ref, w2_ref, fg_ref, o_ref,
                hs_scr, rs_scr, os_scr, before_scr, *, final, tm):
    i = pl.program_id(0)
    off1 = cnt_ref[i, 0]
    off2 = off1 + cnt_ref[i, 1]
    off3 = off2 + cnt_ref[i, 2]
    starts = (jnp.int32(0), off1, off2, off3)
    ends = (off1, off2, off3, jnp.int32(tm))

    route = route_ref[...]
    r_hi, r_lo = _split(route)
    pick = (lax.broadcasted_iota(jnp.int32, (SUBLANES, LANES), 1)
            == ROUTE_GROUP + lax.broadcasted_iota(jnp.int32, (SUBLANES, LANES), 0))
    gt = lax.dot_general(jnp.where(pick, 1.0, 0.0).astype(BF16), r_hi, (((1,), (1,)), ((), ())),
                         preferred_element_type=F32)
    @pl.when(i == 0)
    def _():
        before_scr[...] = jnp.where(lax.broadcasted_iota(jnp.int32, (tm, tm), 0)
                                    < lax.broadcasted_iota(jnp.int32, (tm, tm), 1), 1.0, 0.0).astype(BF16)

    rank = jnp.dot(gt.astype(BF16), before_scr[...], preferred_element_type=F32)
    gt_i = gt.astype(jnp.int32)
    rank_i = rank.astype(jnp.int32)
    pos = jnp.zeros((1, tm), jnp.int32)
    for g in range(MOE_GROUPS):
        pos = pos + gt_i[g:g + 1, :] * (rank_i[g:g + 1, :] + starts[g])
    perm = jnp.where(lax.broadcasted_iota(jnp.int32, (tm, tm), 0) == pos, 1.0, 0.0).astype(BF16)
    hs_scr[...] = jnp.dot(perm, h2_ref[...], preferred_element_type=F32).astype(BF16)
    rs_scr[...] = (jnp.dot(perm, r_hi, preferred_element_type=F32)
                   + jnp.dot(perm, r_lo, preferred_element_type=F32))

    os_scr[...] = jnp.zeros_like(os_scr)
    for g in range(MOE_GROUPS):
        lo, hi = starts[g], ends[g]
        base = (lo // BF16_ROWS) * BF16_ROWS
        n_chunks = jnp.where(hi > lo, (hi - base + MOE_CHUNK - 1) // MOE_CHUNK, 0)

        def chunk_body(c, carry, g=g, lo=lo, hi=hi, base=base):
            r0 = pl.multiple_of(jnp.minimum(base + c * MOE_CHUNK, tm - MOE_CHUNK), BF16_ROWS)
            rows = pl.ds(r0, MOE_CHUNK)
            x = hs_scr[rows, :]
            gates = rs_scr[rows, :]
            a = jnp.dot(x, w1_ref[g], preferred_element_type=F32)
            b = jnp.dot(x, w3_ref[g], preferred_element_type=F32)
            hid = []
            for e in range(MOE_PER_GROUP):
                sl = slice(e * MOE_HIDDEN, (e + 1) * MOE_HIDDEN)
                hid.append((jax.nn.silu(a[:, sl]) * b[:, sl] * gates[:, e:e + 1]).astype(BF16))
            y = jnp.dot(jnp.concatenate(hid, axis=1), w2_ref[g], preferred_element_type=F32)
            rowid = r0 + lax.broadcasted_iota(jnp.int32, (MOE_CHUNK, 1), 0)
            member = (rowid >= lo) & (rowid < hi)
            os_scr[rows, :] = jnp.where(member, y, os_scr[rows, :])
            return carry

        lax.fori_loop(0, n_chunks, chunk_body, 0)

    o_hi, o_lo = _split(os_scr[...])
    moe = (lax.dot_general(perm, o_hi, (((0,), (0,)), ((), ())), preferred_element_type=F32)
           + lax.dot_general(perm, o_lo, (((0,), (0,)), ((), ())), preferred_element_type=F32))
    out = xm_ref[...] + mod_ref[5:6, :] * moe
    if final:
        out = _rms_rows(out) * fg_ref[...]
    o_ref[...] = out


def _moe_weight_kernel(w1_ref, w3_ref, w2_ref, o1_ref, o3_ref, o2_ref):
    for e in range(MOE_PER_GROUP):
        sl = slice(e * MOE_HIDDEN, (e + 1) * MOE_HIDDEN)
        o1_ref[:, sl] = w1_ref[e].astype(BF16)
        o3_ref[:, sl] = w3_ref[e].astype(BF16)
        o2_ref[sl, :] = w2_ref[e].astype(BF16)


def _moe_weights(w1, w3, w2):
    up = pl.BlockSpec((None, MOE_PER_GROUP, D_MODEL, MOE_HIDDEN), lambda l, g: (l, g, 0, 0))
    down = pl.BlockSpec((None, MOE_PER_GROUP, MOE_HIDDEN, D_MODEL), lambda l, g: (l, g, 0, 0))
    out = pl.BlockSpec((None, None, D_MODEL, GROUP_HID), lambda l, g: (l, g, 0, 0))
    shape = jax.ShapeDtypeStruct((DEPTH, MOE_GROUPS, D_MODEL, GROUP_HID), BF16)
    return pl.pallas_call(
        _moe_weight_kernel,
        grid=(DEPTH, MOE_GROUPS),
        in_specs=[up, up, down],
        out_specs=[out, out, out],
        out_shape=[shape, shape, shape],
        compiler_params=_cparams("parallel", "parallel"),
        name="moe_weights",
    )(w1, w3, w2)


def _moe(h2, route, tile_counts, xm, mods, mod_row, mod_tokens, w1g, w3g, w2g, fg, layer, *, final, tm=512):
    n = h2.shape[0]
    cnt = tile_counts[:, 0, ROUTE_GROUP:ROUTE_GROUP + MOE_GROUPS].reshape(
        n // tm, tm // (OUT_SEQS * S5_SEG), MOE_GROUPS).sum(axis=1)
    row = lambda w: pl.BlockSpec((tm, w), lambda i, c: (i, 0))
    mod_tiles = mod_tokens // tm
    wspec = pl.BlockSpec((None, MOE_GROUPS, D_MODEL, GROUP_HID), lambda i, c: (layer, 0, 0, 0),
                         pipeline_mode=pl.Buffered(1))
    return pl.pallas_call(
        functools.partial(_moe_kernel, final=final, tm=tm),
        grid_spec=pltpu.PrefetchScalarGridSpec(
            num_scalar_prefetch=1,
            grid=(n // tm,),
            in_specs=[row(D_MODEL), row(LANES), row(D_MODEL),
                      pl.BlockSpec((None, None, 6, D_MODEL), lambda i, c: (layer, mod_row + i // mod_tiles, 0, 0)),
                      wspec, wspec, wspec,
                      pl.BlockSpec((1, D_MODEL), lambda i, c: (0, 0))],
            out_specs=row(D_MODEL),
            scratch_shapes=[pltpu.VMEM((tm, D_MODEL), BF16), pltpu.VMEM((tm, LANES), F32),
                            pltpu.VMEM((tm, D_MODEL), F32), pltpu.VMEM((tm, tm), BF16)]),
        out_shape=jax.ShapeDtypeStruct((n, D_MODEL), F32),
        compiler_params=_cparams("arbitrary"),
        name="moe",
    )(cnt, h2, route, xm, mods, w1g, w3g, w2g, fg)


def kernel(x_prompt, x_sample, cache_a_k, cache_a_v, cache_b_k, cache_b_v, state_ret, state_ssm, c, c_ctx, mod_w, mod_b, norm1_g, norm2_g, w_in, a_qn_g, a_kn_g, b_rel_bias, ret_decay, ret_gn_g, s5_lam_re, s5_lam_im, s5_log_dt, s5_b_re, s5_b_im, s5_c_re, s5_c_im, s5_d, s5_glu_w, w_out, moe_gw, moe_gb, moe_ew, moe_eb, moe_w1, moe_w3, moe_w2, final_norm_g):
    n_ctx = BATCH * SEQ
    n_lat = DEC_BATCH * DEC_SEQ
    lat_seg = DEC_SEQ // S5_SEG

    cond = jnp.zeros((SUBLANES, D_MODEL), F32).at[0].set(c_ctx).at[1:1 + DEC_BATCH].set(c)
    mods = _modulation(cond, mod_w, mod_b).reshape(DEPTH, SUBLANES, 6, D_MODEL)

    rope_tabs = _rope_tables()
    s5_a, s5_bm, s5_cre, s5_cim = _s5_prepare(s5_lam_re, s5_lam_im, s5_log_dt, s5_b_re, s5_b_im,
                                              s5_c_re, s5_c_im)
    cak = cache_a_k.reshape(DEC_BATCH, DEPTH, PAST_LEN, A_KV_HEADS * HEAD_DIM)
    cav = cache_a_v.reshape(DEC_BATCH, DEPTH, PAST_LEN, A_KV_HEADS * HEAD_DIM)
    cbk = cache_b_k.reshape(DEC_BATCH, DEPTH, PAST_LEN, B_HEADS * HEAD_DIM)
    cbv = cache_b_v.reshape(DEC_BATCH, DEPTH, PAST_LEN, B_HEADS * HEAD_DIM)

    xc = x_prompt.reshape(n_ctx, D_MODEL)
    xs = x_sample.reshape(n_lat, D_MODEL)
    w1_all, w3_all, w2_all = _moe_weights(moe_w1, moe_w3, moe_w2)
    eye_h = jnp.eye(C_HEADS, dtype=F32)
    s0_bd = (state_ret[:, :, :, :, :, None, :] * eye_h[None, None, None, :, None, :, None]).reshape(
        DEC_BATCH, DEPTH, 2, C_HEADS * HEAD_DIM, C_HEADS * HEAD_DIM)
    caches = ret_states = ssm_states = None
    h0_zero = jnp.zeros((2, SUBLANES, 2 * S5_SP), F32)
    w_in_bf = w_in.astype(BF16)
    wo_bf = w_out.astype(BF16)
    glu_bf = s5_glu_w.astype(BF16)
    for l in range(DEPTH):
        final = l == DEPTH - 1
        g1 = norm1_g[l].reshape(1, D_MODEL)
        g2 = norm2_g[l].reshape(1, D_MODEL)
        fg = final_norm_g.reshape(1, D_MODEL)
        qn = jnp.tile(a_qn_g[l], A_HEADS).reshape(1, 256)
        kn = jnp.tile(a_kn_g[l], A_KV_HEADS).reshape(1, 128)
        dec = jnp.broadcast_to(ret_decay[l].reshape(2 * C_HEADS, 1), (2 * C_HEADS, LANES))
        gn = ret_gn_g[l].reshape(1, 256)
        dvec = s5_d[l].reshape(1, GROUP_WIDTH)
        wr = jnp.zeros((D_MODEL, LANES), F32).at[:, :MOE_GROUPS].set(moe_gw[l]).at[
            :, ROUTER_OFF:ROUTER_OFF + MOE_EXPERTS].set(moe_ew[l])
        br = jnp.zeros((1, LANES), F32).at[0, :MOE_GROUPS].set(moe_gb[l]).at[
            0, ROUTER_OFF:ROUTER_OFF + MOE_EXPERTS].set(moe_eb[l])
        wr_hi = wr.astype(BF16)
        wr_lo = (wr - wr_hi.astype(F32)).astype(BF16)
        na_bias = _na_bias(b_rel_bias[l])

        zc, cg, du_tm, *caches = _project(xc, mods, 0, n_ctx, g1, w_in_bf, qn, kn, None, l, seq_len=SEQ,
                                          with_cache=True, prev_caches=caches)
        oa, ob = _ctx_attention(zc, BATCH, SEQ)
        oc, ret_states = _retention(zc, cg, dec, gn, None, l, nb=BATCH, seq_len=SEQ, prev_state=ret_states)
        od_tm, ssm_states = _s5(du_tm, h0_zero, s5_a, s5_bm, s5_cre, s5_cim, dvec, glu_bf, l,
                                nseg=1, fin_layer=l, fin_layers=DEPTH, prev_fin=ssm_states)
        xm, h2, route, counts = _output_stage(xc, (oa, ob, oc, od_tm), mods, 0, n_ctx, wo_bf, g2,
                                              wr_hi, wr_lo, br, l)
        xc = _moe(h2, route, counts, xm, mods, 0, n_ctx, w1_all, w3_all, w2_all, fg, l, final=final)

        zs, cg, du_tm = _project(xs, mods, 1, DEC_SEQ, g1, w_in_bf, qn, kn, rope_tabs, l, seq_len=DEC_SEQ)
        oa = _lat_attention_a(zs, cak, cav, l)
        ob = _lat_attention_b(zs, cbk, cbv, na_bias, l)
        oc = _retention(zs, cg, dec, gn, s0_bd, l, nb=DEC_BATCH, seq_len=DEC_SEQ)
        h0 = state_ssm[:, l].reshape(DEC_BATCH, 2, 2 * S5_SP).transpose(1, 0, 2)
        h0_seg = jnp.zeros((2, DEC_BATCH, lat_seg, 2 * S5_SP), F32)
        h0_seg = h0_seg.at[0, :, 0].set(h0[0]).at[1, :, lat_seg - 1].set(h0[1])
        od_tm, _ = _s5(du_tm, h0_seg.reshape(2, SUBLANES, 2 * S5_SP),
                       s5_a, s5_bm, s5_cre, s5_cim, dvec, glu_bf, l, nseg=lat_seg)
        xm, h2, route, counts = _output_stage(xs, (oa, ob, oc, od_tm), mods, 1, DEC_SEQ, wo_bf, g2,
                                              wr_hi, wr_lo, br, l)
        xs = _moe(h2, route, counts, xm, mods, 1, DEC_SEQ, w1_all, w3_all, w2_all, fg, l, final=final)

    new_ak, new_av, new_bk, new_bv = caches
    return (xc.reshape(BATCH, SEQ, D_MODEL), xs.reshape(DEC_BATCH, DEC_SEQ, D_MODEL),
            new_ak.reshape(BATCH, DEPTH, SEQ, A_KV_HEADS, HEAD_DIM),
            new_av.reshape(BATCH, DEPTH, SEQ, A_KV_HEADS, HEAD_DIM),
            new_bk.reshape(BATCH, DEPTH, SEQ, B_HEADS, HEAD_DIM),
            new_bv.reshape(BATCH, DEPTH, SEQ, B_HEADS, HEAD_DIM),
            ret_states,
            ssm_states.reshape(BATCH, DEPTH, 2, 2, S5_GROUPS, S5_STATE))
```

```python
import functools
import math

import numpy as np
import jax
import jax.numpy as jnp
from jax import lax
from jax.experimental import pallas as pl
from jax.experimental.pallas import tpu as pltpu

F32 = jnp.float32
BF16 = jnp.bfloat16

D_MODEL = 1024
BATCH = 32
SEQ = 256
DEPTH = 2
DEC_BATCH = 2
DEC_SEQ = 1024
PAST_LEN = 256
GRID_W = 64
HEAD_DIM = 64
GROUP_WIDTH = 256
A_HEADS = 4
A_KV_HEADS = 2
B_HEADS = 4
NA_ROWS = 8
NA_COLS = 16
C_HEADS = 4
S5_CH = 16
S5_GROUPS = 16
S5_STATE = 64
MOE_GROUPS = 4
MOE_PER_GROUP = 8
MOE_EXPERTS = 32
MOE_HIDDEN = 128
ROPE_THETA = 10000.0
EPS = 1e-6
IN_WIDTH = 2560
Q_SCALE = HEAD_DIM ** -0.5

OFF_AQ, OFF_AK, OFF_AV = 0, 256, 384
OFF_BQ, OFF_BK, OFF_BV = 512, 768, 1024
OFF_CQ, OFF_CK, OFF_CV, OFF_CG = 1280, 1536, 1792, 2048
OFF_DU = 2304

LANES = 128
SUBLANES = 8
BF16_ROWS = 16
S5_SP = S5_GROUPS * S5_STATE
S5_SEG = 256
S5_CHUNK = 256
ROUTER_OFF = 4
NEG_BIG = -1e30
VMEM_LIMIT = 56 * 1024 * 1024


def _cparams(*sem):
    return pltpu.CompilerParams(dimension_semantics=sem, vmem_limit_bytes=VMEM_LIMIT)


def _mod_spec(layer, first_row, tiles_per_row, grid_rank):
    if grid_rank == 1:
        return pl.BlockSpec((None, None, 6, D_MODEL), lambda i: (layer, first_row + i // tiles_per_row, 0, 0))
    return pl.BlockSpec((None, None, 6, D_MODEL), lambda i, g: (layer, first_row + i // tiles_per_row, 0, 0))


def _bdot(a, b):
    return jnp.dot(a.astype(BF16), b.astype(BF16), preferred_element_type=F32)


def _bdot_nt(a, b):
    return lax.dot_general(a.astype(BF16), b.astype(BF16), (((1,), (1,)), ((), ())),
                           preferred_element_type=F32)


def _bdot_tn(a, b):
    return lax.dot_general(a.astype(BF16), b.astype(BF16), (((0,), (0,)), ((), ())),
                           preferred_element_type=F32)


def _split(a):
    hi = a.astype(BF16)
    lo = (a - hi.astype(F32)).astype(BF16)
    return hi, lo


def _dot_hilo_lhs(a, b_bf16):
    hi, lo = _split(a)
    return (jnp.dot(hi, b_bf16, preferred_element_type=F32)
            + jnp.dot(lo, b_bf16, preferred_element_type=F32))


def _rms_rows(x):
    return x * lax.rsqrt(jnp.mean(x * x, axis=-1, keepdims=True) + EPS)


def _mod_kernel(cond_ref, w_ref, b_ref, o_ref):
    o_ref[...] = _bdot(jax.nn.silu(cond_ref[...]), w_ref[...]) + b_ref[...]


def _modulation(cond, mod_w, mod_b):
    tn = 1536
    return pl.pallas_call(
        _mod_kernel,
        grid=(DEPTH, 6 * D_MODEL // tn),
        in_specs=[pl.BlockSpec((SUBLANES, D_MODEL), lambda l, j: (0, 0)),
                  pl.BlockSpec((None, D_MODEL, tn), lambda l, j: (l, 0, j)),
                  pl.BlockSpec((None, 1, tn), lambda l, j: (l, 0, j))],
        out_specs=pl.BlockSpec((None, SUBLANES, tn), lambda l, j: (l, 0, j)),
        out_shape=jax.ShapeDtypeStruct((DEPTH, SUBLANES, 6 * D_MODEL), F32),
        compiler_params=_cparams("arbitrary", "arbitrary"),
        name="modulation",
    )(cond, mod_w, mod_b.reshape(DEPTH, 1, 6 * D_MODEL))


def _group_mean_matrix(w):
    ri = lax.broadcasted_iota(jnp.int32, (w, w), 0) // HEAD_DIM
    ci = lax.broadcasted_iota(jnp.int32, (w, w), 1) // HEAD_DIM
    return jnp.where(ri == ci, 1.0 / HEAD_DIM, 0.0).astype(BF16)


def _head_norm(t, g):
    ms = _dot_hilo_lhs(t * t, _group_mean_matrix(t.shape[1]))
    return t * lax.rsqrt(ms + EPS) * g


def _rope(t, cos, sa, sb):
    return (t * cos + pltpu.roll(t, LANES - 16, 1) * sa + pltpu.roll(t, 16, 1) * sb)


def _store_layer_slot(ref, slot, value):
    for s in range(ref.shape[0]):
        ref[s] = value if s == slot else jnp.zeros_like(value)


def _layer_slot_block(layer, first_call, tail):
    if first_call:
        return (None, DEPTH) + tail, (0,) * (1 + len(tail)), layer
    return (None, 1) + tail, (layer,) + (0,) * len(tail), 0


def _proj_kernel(x_ref, mod_ref, g1_ref, w_ref, qn_ref, kn_ref, cos_ref, sa_ref, sb_ref, z_ref, cg_ref, du_ref):
    h = _rms_rows(x_ref[...]) * g1_ref[...] * (1.0 + mod_ref[1:2, :]) + mod_ref[0:1, :]
    z = jnp.dot(h.astype(BF16), w_ref[...], preferred_element_type=F32)
    aq = _head_norm(z[:, OFF_AQ:OFF_AK], qn_ref[...])
    ak = _head_norm(z[:, OFF_AK:OFF_AV], kn_ref[...])
    for j in range(3):
        t = aq[:, j * LANES:(j + 1) * LANES] if j < 2 else ak
        sl = slice(0, LANES) if j == 2 else slice(j * LANES, (j + 1) * LANES)
        t = _rope(t, cos_ref[:, sl], sa_ref[:, sl], sb_ref[:, sl])
        z_ref[:, j * LANES:(j + 1) * LANES] = t.astype(BF16)
    z_ref[:, OFF_AV:OFF_CK] = z[:, OFF_AV:OFF_CK].astype(BF16)
    z_ref[:, OFF_CK:OFF_CV] = (z[:, OFF_CK:OFF_CV] * Q_SCALE).astype(BF16)
    z_ref[:, OFF_CV:OFF_CG] = z[:, OFF_CV:OFF_CG].astype(BF16)
    cg_ref[...] = z[:, OFF_CG:OFF_DU]
    du_ref[...] = z[:, OFF_DU:]


def _du_spec(grid_rank):
    if grid_rank == 1:
        return pl.BlockSpec((None, S5_SEG, GROUP_WIDTH), lambda i: (i // SUBLANES, 0, i % SUBLANES))
    return pl.BlockSpec((None, S5_SEG, GROUP_WIDTH), lambda i, g: (i // SUBLANES, 0, i % SUBLANES))


def _project(x, mods, mod_row, mod_tokens, g1, w_in_bf, qn, kn, rope_tabs, layer, *, seq_len):
    tm = S5_SEG
    n = x.shape[0]
    tps = seq_len // tm
    return pl.pallas_call(
        _proj_kernel,
        grid=(n // tm,),
        in_specs=[pl.BlockSpec((tm, D_MODEL), lambda i: (i, 0)),
                  _mod_spec(layer, mod_row, mod_tokens // tm, 1),
                  pl.BlockSpec((1, D_MODEL), lambda i: (0, 0)),
                  pl.BlockSpec((None, D_MODEL, IN_WIDTH), lambda i: (layer, 0, 0)),
                  pl.BlockSpec((1, 256), lambda i: (0, 0)),
                  pl.BlockSpec((1, 128), lambda i: (0, 0))]
                 + [pl.BlockSpec((tm, 256), lambda i: (i % tps, 0))] * 3,
        out_specs=[pl.BlockSpec((tm, OFF_CG), lambda i: (i, 0)),
                   pl.BlockSpec((tm, GROUP_WIDTH), lambda i: (i, 0)), _du_spec(1)],
        out_shape=[jax.ShapeDtypeStruct((n, OFF_CG), BF16),
                   jax.ShapeDtypeStruct((n, GROUP_WIDTH), F32),
                   jax.ShapeDtypeStruct((n // (tm * SUBLANES), S5_SEG, SUBLANES * GROUP_WIDTH), F32)],
        compiler_params=_cparams("parallel"),
        name="project",
    )(x, mods, g1, w_in_bf, qn, kn, *rope_tabs)


def _rope_tables():
    t = jnp.arange(DEC_SEQ)
    row = (t // GRID_W).astype(F32)
    col = (t % GRID_W).astype(F32)
    nf = HEAD_DIM // 4
    inv = ROPE_THETA ** (-jnp.arange(nf, dtype=F32) / nf)
    ang_r = row[:, None] * inv[None, :]
    ang_c = col[:, None] * inv[None, :]
    zeros = jnp.zeros_like(ang_r)
    cos = jnp.concatenate([jnp.cos(ang_r), jnp.cos(ang_r), jnp.cos(ang_c), jnp.cos(ang_c)], axis=-1)
    sa = jnp.concatenate([-jnp.sin(ang_r), zeros, -jnp.sin(ang_c), zeros], axis=-1)
    sb = jnp.concatenate([zeros, jnp.sin(ang_r), zeros, jnp.sin(ang_c)], axis=-1)
    return tuple(jnp.tile(a, (1, 4)) for a in (cos, sa, sb))


N_HEADS = 4


def _lane_head(width):
    return lax.broadcasted_iota(jnp.int32, (1, width), 1) // HEAD_DIM


def _stack_heads(q):
    head = _lane_head(q.shape[1])
    return jnp.concatenate([jnp.where(head == h, q, 0.0) for h in range(N_HEADS)], axis=0).astype(BF16)


def _stack_heads_gqa(q):
    lo = lax.broadcasted_iota(jnp.int32, (1, LANES), 1) < HEAD_DIM
    q = q.astype(F32)
    q01, q23 = q[:, :LANES], q[:, LANES:]
    blocks = [jnp.where(lo, q01, 0.0), jnp.where(lo, pltpu.roll(q01, HEAD_DIM, 1), 0.0),
              jnp.where(lo, 0.0, pltpu.roll(q23, HEAD_DIM, 1)), jnp.where(lo, 0.0, q23)]
    return jnp.concatenate(blocks, axis=0).astype(BF16)


def _spread_kv_gqa(v):
    lo = lax.broadcasted_iota(jnp.int32, (1, LANES), 1) < HEAD_DIM
    v = v.astype(F32)
    vr = pltpu.roll(v, HEAD_DIM, 1)
    return jnp.concatenate([jnp.where(lo, v, vr), jnp.where(lo, vr, v)], axis=1)


def _mha(qs, blocks, tq):
    scores = []
    for k, _, bias in blocks:
        s = _bdot_nt(qs, k)
        scores.append(s if bias is None else s + bias)
    m = functools.reduce(jnp.maximum, [jnp.max(s, axis=-1, keepdims=True) for s in scores])
    es = [jnp.exp(s - m) for s in scores]
    denom = functools.reduce(jnp.add, [jnp.sum(e, axis=-1, keepdims=True) for e in es])
    ps = [e.astype(BF16) for e in es]
    head = _lane_head(N_HEADS * HEAD_DIM)
    vals = [v.astype(BF16) for _, v, _ in blocks]
    o = None
    dall = None
    for h in range(N_HEADS):
        rows = slice(h * tq, (h + 1) * tq)
        for p, v in zip(ps, vals):
            t = jnp.dot(p[rows], jnp.where(head == h, v, jnp.zeros_like(v)), preferred_element_type=F32)
            o = t if o is None else o + t
        d = jnp.where(head == h, denom[rows], 0.0)
        dall = d if dall is None else dall + d
    return (o / dall).astype(BF16)


def _lat_attn_a_kernel(q_ref, kn_ref, vn_ref, kc_ref, vc_ref, o_ref):
    o_ref[...] = _mha(_stack_heads_gqa(q_ref[...] * Q_SCALE),
                      [(kc_ref[...], _spread_kv_gqa(vc_ref[...]), None),
                       (kn_ref[...], _spread_kv_gqa(vn_ref[...]), None)], q_ref.shape[0])


def _lat_attention_a(z, cache_k, cache_v, layer, tq=256):
    nq = DEC_SEQ // tq
    cache_spec = pl.BlockSpec((None, None, PAST_LEN, 128), lambda b, j: (b, layer, 0, 0))
    return pl.pallas_call(
        _lat_attn_a_kernel,
        grid=(DEC_BATCH, nq),
        in_specs=[pl.BlockSpec((tq, 256), lambda b, j: (b * nq + j, OFF_AQ // 256)),
                  pl.BlockSpec((DEC_SEQ, 128), lambda b, j: (b, OFF_AK // 128)),
                  pl.BlockSpec((DEC_SEQ, 128), lambda b, j: (b, OFF_AV // 128)),
                  cache_spec, cache_spec],
        out_specs=pl.BlockSpec((tq, 256), lambda b, j: (b * nq + j, 0)),
        out_shape=jax.ShapeDtypeStruct((DEC_BATCH * DEC_SEQ, 256), BF16),
        compiler_params=_cparams("parallel", "parallel"),
        name="lat_attention_a",
    )(z, z, z, cache_k, cache_v)


NA_KEYS = NA_ROWS * GRID_W


NA_PAIRS = 2 * NA_ROWS - 2


NA_STEP_ROWS = 2


def _na_kernel(q_ref, k_ref, v_ref, kc_ref, vc_ref, bias_ref, o_ref):
    rows = DEC_SEQ // GRID_W
    outs = []
    for rr in range(NA_STEP_ROWS):
        r = pl.program_id(1) * NA_STEP_ROWS + rr
        row_start = jnp.clip(r - NA_ROWS // 2, 0, rows - NA_ROWS)
        start = pl.multiple_of(row_start * GRID_W, GRID_W)
        rel0 = row_start - r + NA_ROWS - 1
        kl = k_ref[pl.ds(start, NA_KEYS), :]
        vl = v_ref[pl.ds(start, NA_KEYS), :]
        bias = jnp.concatenate(
            [jnp.concatenate([bias_ref[h, rel0 + 2 * jp] for jp in range(NA_ROWS // 2)], axis=1)
             for h in range(B_HEADS)], axis=0)
        qrows = slice(rr * GRID_W, (rr + 1) * GRID_W)
        outs.append(_mha(_stack_heads(q_ref[qrows, :] * Q_SCALE),
                         [(kl, vl, bias), (kc_ref[...], vc_ref[...], None)], GRID_W))
    o_ref[...] = jnp.concatenate(outs, axis=0)


def _na_bias(rel_bias):
    nrel = 2 * NA_COLS - 1
    period = 2 * GRID_W
    b = rel_bias.astype(F32)
    ext = jnp.concatenate([b[..., NA_COLS - 1:],
                           jnp.zeros(b.shape[:-1] + (period - nrel,), F32),
                           b[..., :NA_COLS - 1]], axis=-1)
    flat = jnp.tile(ext, (1, 1, GRID_W))[..., :GRID_W * (period - 1)]
    toe = flat.reshape(b.shape[:-1] + (GRID_W, period - 1))[..., :GRID_W]
    col_start = np.clip(np.arange(GRID_W) - NA_COLS // 2, 0, GRID_W - NA_COLS)
    kc = np.arange(GRID_W)
    inside = (kc[None, :] >= col_start[:, None]) & (kc[None, :] < col_start[:, None] + NA_COLS)
    toe = jnp.where(jnp.asarray(inside), toe, NEG_BIG)
    return jnp.concatenate([toe[:, :-1], toe[:, 1:]], axis=-1)


def _lat_attention_b(z, cache_k, cache_v, bias, layer):
    rows = DEC_SEQ // GRID_W // NA_STEP_ROWS
    tq = NA_STEP_ROWS * GRID_W
    cache_spec = pl.BlockSpec((None, None, PAST_LEN, 256), lambda b, r: (b, layer, 0, 0))
    return pl.pallas_call(
        _na_kernel,
        grid=(DEC_BATCH, rows),
        in_specs=[pl.BlockSpec((tq, 256), lambda b, r: (b * rows + r, OFF_BQ // 256)),
                  pl.BlockSpec((DEC_SEQ, 256), lambda b, r: (b, OFF_BK // 256)),
                  pl.BlockSpec((DEC_SEQ, 256), lambda b, r: (b, OFF_BV // 256)),
                  cache_spec, cache_spec,
                  pl.BlockSpec((B_HEADS, NA_PAIRS, GRID_W, 2 * GRID_W), lambda b, r: (0, 0, 0, 0))],
        out_specs=pl.BlockSpec((tq, 256), lambda b, r: (b * rows + r, 0)),
        out_shape=jax.ShapeDtypeStruct((DEC_BATCH * DEC_SEQ, 256), BF16),
        compiler_params=_cparams("parallel", "parallel"),
        name="lat_attention_b",
    )(z, z, z, cache_k, cache_v, bias)


def _retention_core(q, k, v, g, dec_ref, gn_ref, dec_scr, *, seq_len, i0, hoist_decay, s0_ref=None,
                    want_state=False):
    tq = q.shape[0]
    head = _lane_head(C_HEADS * HEAD_DIM)
    lg = jax.nn.log_sigmoid(dec_ref[...])

    def per_lane(row0):
        out = jnp.zeros((1, C_HEADS * HEAD_DIM), F32)
        for h in range(C_HEADS):
            out = jnp.where(head == h, lg[row0 + h:row0 + h + 1, 0:1], out)
        return out

    lgf_l, lgb_l = per_lane(0), per_lane(C_HEADS)
    qi = (i0 + lax.broadcasted_iota(jnp.int32, (tq, 1), 0)).astype(F32)

    def fill_decay():
        kj = lax.broadcasted_iota(jnp.int32, (1, seq_len), 1).astype(F32)
        diff = qi - kj
        for h in range(C_HEADS):
            lgf = lg[h:h + 1, 0:1]
            lgb = lg[C_HEADS + h:C_HEADS + h + 1, 0:1]
            dec_scr[h * tq:(h + 1) * tq, :] = (
                jnp.where(diff >= 0, jnp.exp(lgf * jnp.maximum(diff, 0.0)), 0.0)
                + jnp.where(diff <= 0, jnp.exp(lgb * jnp.maximum(-diff, 0.0)), 0.0))

    if hoist_decay:
        pl.when(pl.program_id(0) == 0)(fill_decay)
    else:
        fill_decay()

    v = v.astype(BF16)
    sc = (_bdot_nt(_stack_heads(q), k) * dec_scr[...]).astype(BF16)
    o = None
    for h in range(C_HEADS):
        t = jnp.dot(sc[h * tq:(h + 1) * tq], jnp.where(head == h, v, jnp.zeros_like(v)),
                    preferred_element_type=F32)
        o = t if o is None else o + t
    if s0_ref is not None:
        o = (o + _bdot(q, s0_ref[0]) * jnp.exp(lgf_l * (qi + 1.0))
             + _bdot(q, s0_ref[1]) * jnp.exp(lgb_l * (seq_len - qi)))
    gm = _group_mean_matrix(C_HEADS * HEAD_DIM)
    dlt = o - _dot_hilo_lhs(o, gm)
    var = _dot_hilo_lhs(dlt * dlt, gm)
    out = (dlt * lax.rsqrt(var + EPS) * gn_ref[...] * jax.nn.silu(g)).astype(BF16)
    if not want_state:
        return out, None
    kpos = lax.broadcasted_iota(jnp.int32, (seq_len, 1), 0).astype(F32)
    sf = _bdot_tn(k * jnp.exp(lgf_l * (seq_len - 1.0 - kpos)), v)
    sb = _bdot_tn(k * jnp.exp(lgb_l * kpos), v)
    return out, (sf, sb)


def _store_retention_state(st_ref, slot, state):
    for s in range(st_ref.shape[0]):
        for d in range(2):
            for h in range(C_HEADS):
                sl = slice(h * HEAD_DIM, (h + 1) * HEAD_DIM)
                st_ref[s, d, h] = state[d][sl, sl] if s == slot else jnp.zeros((HEAD_DIM, HEAD_DIM), F32)


def _retention_kernel(q_ref, g_ref, k_ref, v_ref, dec_ref, gn_ref, s0_ref, o_ref, dec_scr, *, seq_len, tq):
    o_ref[...], _ = _retention_core(q_ref[...], k_ref[...], v_ref[...], g_ref[...], dec_ref, gn_ref, dec_scr,
                                    seq_len=seq_len, i0=pl.program_id(1) * tq, hoist_decay=False,
                                    s0_ref=s0_ref)


def _retention(z, cg, dec, gn, s0, layer, *, nb, seq_len, tq=256):
    nq = seq_len // tq
    return pl.pallas_call(
        functools.partial(_retention_kernel, seq_len=seq_len, tq=tq),
        grid=(nb, nq),
        in_specs=[pl.BlockSpec((tq, 256), lambda b, j: (b * nq + j, OFF_CQ // 256)),
                  pl.BlockSpec((tq, 256), lambda b, j: (b * nq + j, 0)),
                  pl.BlockSpec((seq_len, 256), lambda b, j: (b, OFF_CK // 256)),
                  pl.BlockSpec((seq_len, 256), lambda b, j: (b, OFF_CV // 256)),
                  pl.BlockSpec((SUBLANES, LANES), lambda b, j: (0, 0)),
                  pl.BlockSpec((1, 256), lambda b, j: (0, 0)),
                  pl.BlockSpec((None, None, 2, 256, 256), lambda b, j: (b, layer, 0, 0, 0))],
        out_specs=pl.BlockSpec((tq, 256), lambda b, j: (b * nq + j, 0)),
        out_shape=jax.ShapeDtypeStruct((nb * seq_len, 256), BF16),
        scratch_shapes=[pltpu.VMEM((C_HEADS * tq, seq_len), F32)],
        compiler_params=_cparams("parallel", "parallel"),
        name="retention",
    )(z, cg, z, z, dec, gn, s0)


def _ctx_front_kernel(x_ref, mod_ref, g1_ref, w_ref, qn_ref, kn_ref, dec_ref, gn_ref, *rest, n_alias, slot):
    (oa_ref, ob_ref, oc_ref, du_ref, ak_ref, av_ref, bk_ref, bv_ref, st_ref, dec_scr) = rest[n_alias:]
    tq = x_ref.shape[0]
    h = _rms_rows(x_ref[...]) * g1_ref[...] * (1.0 + mod_ref[1:2, :]) + mod_ref[0:1, :]
    z = jnp.dot(h.astype(BF16), w_ref[...], preferred_element_type=F32)
    aq = _head_norm(z[:, OFF_AQ:OFF_AK], qn_ref[...])
    ak = _head_norm(z[:, OFF_AK:OFF_AV], kn_ref[...])
    av, bq, bk, bv = (z[:, OFF_AV:OFF_BQ], z[:, OFF_BQ:OFF_BK], z[:, OFF_BK:OFF_BV], z[:, OFF_BV:OFF_CQ])
    oa_ref[...] = _mha(_stack_heads_gqa(aq * Q_SCALE), [(ak, _spread_kv_gqa(av), None)], tq)
    ob_ref[...] = _mha(_stack_heads(bq * Q_SCALE), [(bk, bv, None)], tq)
    oc_ref[...], state = _retention_core(
        z[:, OFF_CQ:OFF_CK], z[:, OFF_CK:OFF_CV] * Q_SCALE, z[:, OFF_CV:OFF_CG], z[:, OFF_CG:OFF_DU],
        dec_ref, gn_ref, dec_scr, seq_len=tq, i0=0, hoist_decay=True, want_state=True)
    du_ref[...] = z[:, OFF_DU:]
    _store_layer_slot(ak_ref, slot, ak)
    _store_layer_slot(av_ref, slot, av)
    _store_layer_slot(bk_ref, slot, bk)
    _store_layer_slot(bv_ref, slot, bv)
    _store_retention_state(st_ref, slot, state)


def _ctx_front(x, mods, g1, w_in_bf, qn, kn, dec, gn, layer, prev):
    tm = SEQ
    n = x.shape[0]
    nb = n // tm
    const = lambda *shape: pl.BlockSpec(shape, lambda i: (0,) * len(shape))
    row = lambda w: pl.BlockSpec((tm, w), lambda i: (i, 0))
    in_specs = [row(D_MODEL), _mod_spec(layer, 0, nb, 1), const(1, D_MODEL),
                pl.BlockSpec((None, D_MODEL, IN_WIDTH), lambda i: (layer, 0, 0)),
                const(1, 256), const(1, 128), const(SUBLANES, LANES), const(1, 256)]
    args = [x, mods, g1, w_in_bf, qn, kn, dec, gn]
    out_specs = [row(256), row(256), row(256), _du_spec(1)]
    out_shape = [jax.ShapeDtypeStruct((n, 256), BF16)] * 3 + [
        jax.ShapeDtypeStruct((n // (tm * SUBLANES), S5_SEG, SUBLANES * GROUP_WIDTH), F32)]
    first = prev is None
    slot = 0
    for tail in ((tm, 128), (tm, 128), (tm, 256), (tm, 256), (2, C_HEADS, HEAD_DIM, HEAD_DIM)):
        blk, idx, slot = _layer_slot_block(layer, first, tail)
        out_specs.append(pl.BlockSpec(blk, lambda i, idx=idx: (i,) + idx))
        out_shape.append(jax.ShapeDtypeStruct((nb, DEPTH) + tail, F32))
    aliases = {}
    if not first:
        for k, arr in enumerate(prev):
            aliases[len(args)] = 4 + k
            in_specs.append(pl.BlockSpec(memory_space=pl.ANY))
            args.append(arr)
    outs = pl.pallas_call(
        functools.partial(_ctx_front_kernel, n_alias=len(aliases), slot=slot),
        grid=(nb,),
        in_specs=in_specs,
        out_specs=out_specs,
        out_shape=out_shape,
        scratch_shapes=[pltpu.VMEM((C_HEADS * tm, tm), F32)],
        input_output_aliases=aliases,
        compiler_params=_cparams("arbitrary"),
        name="ctx_front",
    )(*args)
    return outs[0], outs[1], outs[2], outs[3], tuple(outs[4:])


def _s5_prep_kernel(lre_ref, lim_ref, ldt_ref, bre_ref, bim_ref, cre_ref, cim_ref,
                    a_ref, bm_ref, cro_ref, cio_ref, bm_scr, cr_scr, ci_scr):
    lre = lre_ref[...]
    lim = lim_ref[...]
    dt = jnp.exp(ldt_ref[...])
    mag = jnp.exp(lre * dt)
    a_re = mag * jnp.cos(lim * dt)
    a_im = mag * jnp.sin(lim * dt)
    den = lre * lre + lim * lim
    r_re = ((a_re - 1.0) * lre + a_im * lim) / den
    r_im = (a_im * lre - (a_re - 1.0) * lim) / den
    bm_scr[...] = jnp.zeros_like(bm_scr)
    cr_scr[...] = jnp.zeros_like(cr_scr)
    ci_scr[...] = jnp.zeros_like(ci_scr)
    for g in range(S5_GROUPS):
        rows = slice(g * S5_CH, (g + 1) * S5_CH)
        cols = slice(g * S5_STATE, (g + 1) * S5_STATE)
        a_ref[0:1, cols] = a_re[g:g + 1, :]
        a_ref[1:2, cols] = a_im[g:g + 1, :]
        rr, ri = r_re[g:g + 1, :], r_im[g:g + 1, :]
        br, bi = bre_ref[g], bim_ref[g]
        bm_scr[rows, cols] = rr * br - ri * bi
        bm_scr[rows, S5_SP + g * S5_STATE:S5_SP + (g + 1) * S5_STATE] = rr * bi + ri * br
        cr_scr[cols, rows] = cre_ref[g]
        ci_scr[cols, rows] = cim_ref[g]
    bm_ref[...] = bm_scr[...].astype(BF16)
    cro_ref[...] = cr_scr[...].astype(BF16)
    cio_ref[...] = ci_scr[...].astype(BF16)


def _s5_prepare(lam_re, lam_im, log_dt, b_re, b_im, c_re, c_im):
    gp = (S5_GROUPS, S5_STATE)
    ldt = jnp.broadcast_to(log_dt[..., None], (DEPTH, 2) + gp)
    bt = [jnp.swapaxes(t, -1, -2) for t in (b_re, b_im)]
    ct = [jnp.swapaxes(t, -1, -2) for t in (c_re, c_im)]

    def spec(*tail):
        return pl.BlockSpec((None, None) + tail, lambda l, d: (l, d) + (0,) * len(tail))

    return pl.pallas_call(
        _s5_prep_kernel,
        grid=(DEPTH, 2),
        in_specs=[spec(*gp)] * 3 + [spec(S5_GROUPS, S5_CH, S5_STATE)] * 2 + [spec(S5_GROUPS, S5_STATE, S5_CH)] * 2,
        out_specs=[spec(2, S5_SP), spec(GROUP_WIDTH, 2 * S5_SP), spec(S5_SP, GROUP_WIDTH), spec(S5_SP, GROUP_WIDTH)],
        out_shape=[jax.ShapeDtypeStruct((DEPTH, 2, 2, S5_SP), F32),
                   jax.ShapeDtypeStruct((DEPTH, 2, GROUP_WIDTH, 2 * S5_SP), BF16),
                   jax.ShapeDtypeStruct((DEPTH, 2, S5_SP, GROUP_WIDTH), BF16),
                   jax.ShapeDtypeStruct((DEPTH, 2, S5_SP, GROUP_WIDTH), BF16)],
        scratch_shapes=[pltpu.VMEM((GROUP_WIDTH, 2 * S5_SP), F32), pltpu.VMEM((S5_SP, GROUP_WIDTH), F32),
                        pltpu.VMEM((S5_SP, GROUP_WIDTH), F32)],
        compiler_params=_cparams("parallel", "parallel"),
        name="s5_prepare",
    )(lam_re, lam_im, ldt, bt[0], bt[1], ct[0], ct[1])


def _cmul(ar, ai, br, bi):
    return ar * br - ai * bi, ar * bi + ai * br


def _s5_kernel(u_ref, h0_ref, a_ref, bm_ref, cre_ref, cim_ref, dvec_ref, glu_ref, *rest, nseg, slot):
    od_ref, fin_ref, x_scr, s_scr, y_scr = rest[-5:]
    steps = S5_SEG
    rows = steps * SUBLANES
    chunk = S5_CHUNK
    chunk_steps = chunk // SUBLANES
    nchunk = rows // chunk
    seg = lax.broadcasted_iota(jnp.int32, (SUBLANES, S5_SP), 0) % nseg

    for d in range(2):
        ar = jnp.broadcast_to(a_ref[d, 0:1, :], (SUBLANES, S5_SP))
        ai = jnp.broadcast_to(a_ref[d, 1:2, :], (SUBLANES, S5_SP))

        def row0(k):
            c = k if d == 0 else nchunk - 1 - k
            return c * chunk if isinstance(c, int) else pl.multiple_of(c * chunk, chunk)

        def input_part(k, buf):
            x_scr[buf] = jnp.dot(u_ref[pl.ds(row0(k), chunk), :].astype(BF16), bm_ref[d],
                                 preferred_element_type=F32)

        def scan_part(buf, carry, store):
            sr, si = carry
            for t in range(chunk_steps):
                r = (t if d == 0 else chunk_steps - 1 - t) * SUBLANES
                pr, pi = _cmul(ar, ai, sr, si)
                sr = pr + x_scr[buf, r:r + SUBLANES, 0:S5_SP]
                si = pi + x_scr[buf, r:r + SUBLANES, S5_SP:]
                if store:
                    s_scr[buf, r:r + SUBLANES, 0:S5_SP] = sr
                    s_scr[buf, r:r + SUBLANES, S5_SP:] = si
            return sr, si

        def output_part(k, buf):
            y = _bdot(s_scr[buf, :, 0:S5_SP], cre_ref[d]) - _bdot(s_scr[buf, :, S5_SP:], cim_ref[d])
            rows_k = pl.ds(row0(k), chunk)
            if d == 0:
                y_scr[rows_k, :] = y
            else:
                zz = jax.nn.gelu(y_scr[rows_k, :] + y + dvec_ref[...] * u_ref[rows_k, :])
                od_ref[rows_k, :] = (zz * jax.nn.sigmoid(_bdot(zz, glu_ref[...]))).astype(BF16)

        def half(k, buf, carry, store, nxt=True, prev=True):
            if nxt:
                input_part(k + 1, 1 - buf)
            carry = scan_part(buf, carry, store)
            if store and prev:
                output_part(k - 1, 1 - buf)
            return carry

        def run_pass(carry, store):
            input_part(0, 0)
            carry = half(0, 0, carry, store, prev=False)
            carry = half(1, 1, carry, store)

            def pair(j, c):
                c = half(2 * j, 0, c, store)
                return half(2 * j + 1, 1, c, store)
            carry = lax.fori_loop(1, nchunk // 2 - 1, pair, carry)
            carry = half(nchunk - 2, 0, carry, store)
            carry = half(nchunk - 1, 1, carry, store, nxt=False)
            if store:
                output_part(nchunk - 1, 1)
            return carry

        init = (h0_ref[d, :, 0:S5_SP], h0_ref[d, :, S5_SP:])
        if nseg > 1:
            zero = jnp.zeros((SUBLANES, S5_SP), F32)
            fr, fi = run_pass((zero, zero), store=False)
            pr, pi = ar, ai
            for _ in range(int(math.log2(steps))):
                pr, pi = _cmul(pr, pi, pr, pi)
            cr, ci = init
            shift = 1 if d == 0 else SUBLANES - 1
            order = range(1, nseg) if d == 0 else range(nseg - 2, -1, -1)
            for s in order:
                ncr, nci = pltpu.roll(cr, shift, 0), pltpu.roll(ci, shift, 0)
                nfr, nfi = pltpu.roll(fr, shift, 0), pltpu.roll(fi, shift, 0)
                qr, qi = _cmul(pr, pi, ncr, nci)
                cr = jnp.where(seg == s, qr + nfr, cr)
                ci = jnp.where(seg == s, qi + nfi, ci)
            init = (cr, ci)
        sr, si = run_pass(init, store=True)
        for s in range(fin_ref.shape[1] // (4 * S5_SP)):
            base = (4 * s + 2 * d) * S5_SP
            fin_ref[:, base:base + S5_SP] = sr if s == slot else jnp.zeros_like(sr)
            fin_ref[:, base + S5_SP:base + 2 * S5_SP] = si if s == slot else jnp.zeros_like(si)


def _s5(du_tm, h0, a, bmat, cre, cim, dvec, glu_bf, layer, *, nseg, fin_layer=0, fin_layers=1,
        prev_fin=None):
    nblk = du_tm.shape[0]
    rows = S5_SEG * SUBLANES
    fin_w = 4 * S5_SP
    in_specs = [pl.BlockSpec((None, rows, GROUP_WIDTH), lambda i: (i, 0, 0)),
                pl.BlockSpec((2, SUBLANES, 2 * S5_SP), lambda i: (0, 0, 0)),
                pl.BlockSpec((None, 2, 2, S5_SP), lambda i: (layer, 0, 0, 0)),
                pl.BlockSpec((None, 2, GROUP_WIDTH, 2 * S5_SP), lambda i: (layer, 0, 0, 0)),
                pl.BlockSpec((None, 2, S5_SP, GROUP_WIDTH), lambda i: (layer, 0, 0, 0)),
                pl.BlockSpec((None, 2, S5_SP, GROUP_WIDTH), lambda i: (layer, 0, 0, 0)),
                pl.BlockSpec((1, GROUP_WIDTH), lambda i: (0, 0)),
                pl.BlockSpec((None, GROUP_WIDTH, GROUP_WIDTH), lambda i: (layer, 0, 0))]
    args = [du_tm.reshape(nblk, rows, GROUP_WIDTH), h0, a, bmat, cre, cim, dvec, glu_bf]
    aliases = {}
    if prev_fin is not None:
        aliases[len(args)] = 1
        in_specs.append(pl.BlockSpec(memory_space=pl.ANY))
        args.append(prev_fin)
        fin_spec, slot = pl.BlockSpec((SUBLANES, fin_w), lambda i: (i, fin_layer)), 0
    else:
        fin_spec, slot = pl.BlockSpec((SUBLANES, fin_layers * fin_w), lambda i: (i, 0)), fin_layer
    od, fin = pl.pallas_call(
        functools.partial(_s5_kernel, nseg=nseg, slot=slot),
        grid=(nblk,),
        in_specs=in_specs,
        out_specs=[pl.BlockSpec((None, rows, GROUP_WIDTH), lambda i: (i, 0, 0)), fin_spec],
        out_shape=[jax.ShapeDtypeStruct((nblk, rows, GROUP_WIDTH), BF16),
                   jax.ShapeDtypeStruct((nblk * SUBLANES, fin_layers * fin_w), F32)],
        scratch_shapes=[pltpu.VMEM((2, S5_CHUNK, 2 * S5_SP), F32), pltpu.VMEM((2, S5_CHUNK, 2 * S5_SP), F32),
                        pltpu.VMEM((rows, GROUP_WIDTH), F32)],
        input_output_aliases=aliases,
        compiler_params=_cparams("parallel"),
        name="s5",
    )(*args)
    return od.reshape(nblk, S5_SEG, SUBLANES * GROUP_WIDTH), fin


ROUTE_GROUP = MOE_PER_GROUP
OUT_SEQS = 2


def _out_kernel(x_ref, oa_ref, ob_ref, oc_ref, od_ref, mod_ref, wo_ref, g2_ref, wrh_ref, wrl_ref, br_ref,
                xm_ref, h2_ref, route_ref, cnt_ref):
    od = jnp.concatenate([od_ref[:, s * GROUP_WIDTH:(s + 1) * GROUP_WIDTH] for s in range(OUT_SEQS)], axis=0)
    mix = functools.reduce(jnp.add, [
        _bdot(o, wo_ref[i * GROUP_WIDTH:(i + 1) * GROUP_WIDTH, :])
        for i, o in enumerate((oa_ref[...], ob_ref[...], oc_ref[...], od))])
    xm = x_ref[...] + mod_ref[2:3, :] * mix
    xm_ref[...] = xm
    h2 = _rms_rows(xm) * g2_ref[...] * (1.0 + mod_ref[4:5, :]) + mod_ref[3:4, :]
    h2_ref[...] = h2.astype(BF16)

    h_hi, h_lo = _split(h2)
    logits = (jnp.dot(h_hi, wrh_ref[...], preferred_element_type=F32)
              + jnp.dot(h_hi, wrl_ref[...], preferred_element_type=F32)
              + jnp.dot(h_lo, wrh_ref[...], preferred_element_type=F32)) + br_ref[...]
    lane_i = lax.broadcasted_iota(jnp.int32, logits.shape, 1)
    lane = lane_i.astype(F32)
    big = jnp.float32(2 ** 30)
    gmask = lane_i < MOE_GROUPS
    gl = jnp.where(gmask, logits, -jnp.inf)
    gmax = jnp.max(gl, axis=-1, keepdims=True)
    p_top = 1.0 / jnp.sum(jnp.exp(gl - gmax), axis=-1, keepdims=True)
    g_top = jnp.min(jnp.where(gl == gmax, lane, big), axis=-1, keepdims=True)
    e_lane = lane_i - ROUTER_OFF
    lane_group = (e_lane // MOE_PER_GROUP).astype(F32)
    emask = (e_lane >= 0) & (e_lane < MOE_EXPERTS) & (lane_group == g_top)
    el = jnp.where(emask, logits, -jnp.inf)
    m1 = jnp.max(el, axis=-1, keepdims=True)
    i1 = jnp.min(jnp.where(el == m1, lane, big), axis=-1, keepdims=True)
    el2 = jnp.where(lane == i1, -jnp.inf, el)
    m2 = jnp.max(el2, axis=-1, keepdims=True)
    i2 = jnp.min(jnp.where(el2 == m2, lane, big), axis=-1, keepdims=True)
    e2 = jnp.exp(m2 - m1)
    den = 1.0 + e2
    gates = (jnp.where(lane == i1, (1.0 / den) * p_top, 0.0)
             + jnp.where(lane == i2, (e2 / den) * p_top, 0.0))
    route = jnp.where(lane == ROUTE_GROUP + g_top, 1.0, 0.0)
    for g in range(MOE_GROUPS):
        local = pltpu.roll(gates, LANES - ROUTER_OFF - g * MOE_PER_GROUP, 1)
        route = route + jnp.where((g_top == g) & (lane_i < MOE_PER_GROUP), local, 0.0)
    route_ref[...] = route
    cnt_ref[...] = jnp.broadcast_to(jnp.sum(route, axis=0, keepdims=True), (SUBLANES, LANES)).astype(jnp.int32)


def _output_stage(x, mixes, mods, mod_row, mod_tokens, wo_bf, g2, wr_hi, wr_lo, br, layer):
    tm = OUT_SEQS * S5_SEG
    n = x.shape[0]
    row = lambda w: pl.BlockSpec((tm, w), lambda i: (i, 0))
    const = lambda shape: pl.BlockSpec(shape, lambda i: (0,) * len(shape))
    per_blk = SUBLANES // OUT_SEQS
    return pl.pallas_call(
        _out_kernel,
        grid=(n // tm,),
        in_specs=[row(D_MODEL), row(256), row(256), row(256),
                  pl.BlockSpec((None, S5_SEG, OUT_SEQS * GROUP_WIDTH), lambda i: (i // per_blk, 0, i % per_blk)),
                  _mod_spec(layer, mod_row, mod_tokens // tm, 1),
                  pl.BlockSpec((None, D_MODEL, D_MODEL), lambda i: (layer, 0, 0)), const((1, D_MODEL)),
                  const((D_MODEL, LANES)), const((D_MODEL, LANES)), const((1, LANES))],
        out_specs=[row(D_MODEL), row(D_MODEL), row(LANES),
                   pl.BlockSpec((None, SUBLANES, LANES), lambda i: (i, 0, 0))],
        out_shape=[jax.ShapeDtypeStruct((n, D_MODEL), F32),
                   jax.ShapeDtypeStruct((n, D_MODEL), BF16),
                   jax.ShapeDtypeStruct((n, LANES), F32),
                   jax.ShapeDtypeStruct((n // tm, SUBLANES, LANES), jnp.int32)],
        compiler_params=_cparams("parallel"),
        name="output_stage",
    )(x, *mixes, mods, wo_bf, g2, wr_hi, wr_lo, br)


GROUP_HID = MOE_PER_GROUP * MOE_HIDDEN


MOE_CHUNK = 128


def _moe_kernel(cnt_ref, h2_ref, route_ref, xm_ref, mod_ref, w1_ref, w3_ref, w2_ref, fg_ref, o_ref,
                hs_scr, rs_scr, os_scr, before_scr, *, final, tm):
    i = pl.program_id(0)
    off1 = cnt_ref[i, 0]
    off2 = off1 + cnt_ref[i, 1]
    off3 = off2 + cnt_ref[i, 2]
    starts = (jnp.int32(0), off1, off2, off3)
    ends = (off1, off2, off3, jnp.int32(tm))

    route = route_ref[...]
    r_hi, r_lo = _split(route)
    pick = (lax.broadcasted_iota(jnp.int32, (SUBLANES, LANES), 1)
            == ROUTE_GROUP + lax.broadcasted_iota(jnp.int32, (SUBLANES, LANES), 0))
    gt = lax.dot_general(jnp.where(pick, 1.0, 0.0).astype(BF16), r_hi, (((1,), (1,)), ((), ())),
                         preferred_element_type=F32)
    @pl.when(i == 0)
    def _():
        before_scr[...] = jnp.where(lax.broadcasted_iota(jnp.int32, (tm, tm), 0)
                                    < lax.broadcasted_iota(jnp.int32, (tm, tm), 1), 1.0, 0.0).astype(BF16)

    rank = jnp.dot(gt.astype(BF16), before_scr[...], preferred_element_type=F32)
    gt_i = gt.astype(jnp.int32)
    rank_i = rank.astype(jnp.int32)
    pos = jnp.zeros((1, tm), jnp.int32)
    for g in range(MOE_GROUPS):
        pos = pos + gt_i[g:g + 1, :] * (rank_i[g:g + 1, :] + starts[g])
    perm = jnp.where(lax.broadcasted_iota(jnp.int32, (tm, tm), 0) == pos, 1.0, 0.0).astype(BF16)
    hs_scr[...] = jnp.dot(perm, h2_ref[...], preferred_element_type=F32).astype(BF16)
    rs_scr[...] = (jnp.dot(perm, r_hi, preferred_element_type=F32)
                   + jnp.dot(perm, r_lo, preferred_element_type=F32))

    os_scr[...] = jnp.zeros_like(os_scr)
    for g in range(MOE_GROUPS):
        lo, hi = starts[g], ends[g]
        base = (lo // BF16_ROWS) * BF16_ROWS
        n_chunks = jnp.where(hi > lo, (hi - base + MOE_CHUNK - 1) // MOE_CHUNK, 0)

        def chunk_body(c, carry, g=g, lo=lo, hi=hi, base=base):
            r0 = pl.multiple_of(jnp.minimum(base + c * MOE_CHUNK, tm - MOE_CHUNK), BF16_ROWS)
            rows = pl.ds(r0, MOE_CHUNK)
            x = hs_scr[rows, :]
            gates = rs_scr[rows, :]
            a = jnp.dot(x, w1_ref[g], preferred_element_type=F32)
            b = jnp.dot(x, w3_ref[g], preferred_element_type=F32)
            hid = []
            for e in range(MOE_PER_GROUP):
                sl = slice(e * MOE_HIDDEN, (e + 1) * MOE_HIDDEN)
                hid.append((jax.nn.silu(a[:, sl]) * b[:, sl] * gates[:, e:e + 1]).astype(BF16))
            y = jnp.dot(jnp.concatenate(hid, axis=1), w2_ref[g], preferred_element_type=F32)
            rowid = r0 + lax.broadcasted_iota(jnp.int32, (MOE_CHUNK, 1), 0)
            member = (rowid >= lo) & (rowid < hi)
            os_scr[rows, :] = jnp.where(member, y, os_scr[rows, :])
            return carry

        lax.fori_loop(0, n_chunks, chunk_body, 0)

    o_hi, o_lo = _split(os_scr[...])
    moe = (lax.dot_general(perm, o_hi, (((0,), (0,)), ((), ())), preferred_element_type=F32)
           + lax.dot_general(perm, o_lo, (((0,), (0,)), ((), ())), preferred_element_type=F32))
    out = xm_ref[...] + mod_ref[5:6, :] * moe
    if final:
        out = _rms_rows(out) * fg_ref[...]
    o_ref[...] = out


def _moe_weight_kernel(w1_ref, w3_ref, w2_ref, o1_ref, o3_ref, o2_ref):
    for e in range(MOE_PER_GROUP):
        sl = slice(e * MOE_HIDDEN, (e + 1) * MOE_HIDDEN)
        o1_ref[:, sl] = w1_ref[e].astype(BF16)
        o3_ref[:, sl] = w3_ref[e].astype(BF16)
        o2_ref[sl, :] = w2_ref[e].astype(BF16)


def _moe_weights(w1, w3, w2):
    up = pl.BlockSpec((None, MOE_PER_GROUP, D_MODEL, MOE_HIDDEN), lambda l, g: (l, g, 0, 0))
    down = pl.BlockSpec((None, MOE_PER_GROUP, MOE_HIDDEN, D_MODEL), lambda l, g: (l, g, 0, 0))
    out = pl.BlockSpec((None, None, D_MODEL, GROUP_HID), lambda l, g: (l, g, 0, 0))
    shape = jax.ShapeDtypeStruct((DEPTH, MOE_GROUPS, D_MODEL, GROUP_HID), BF16)
    return pl.pallas_call(
        _moe_weight_kernel,
        grid=(DEPTH, MOE_GROUPS),
        in_specs=[up, up, down],
        out_specs=[out, out, out],
        out_shape=[shape, shape, shape],
        compiler_params=_cparams("parallel", "parallel"),
        name="moe_weights",
    )(w1, w3, w2)


def _moe(h2, route, tile_counts, xm, mods, mod_row, mod_tokens, w1g, w3g, w2g, fg, layer, *, final, tm=512):
    n = h2.shape[0]
    cnt = tile_counts[:, 0, ROUTE_GROUP:ROUTE_GROUP + MOE_GROUPS].reshape(
        n // tm, tm // (OUT_SEQS * S5_SEG), MOE_GROUPS).sum(axis=1)
    row = lambda w: pl.BlockSpec((tm, w), lambda i, c: (i, 0))
    mod_tiles = mod_tokens // tm
    wspec = pl.BlockSpec((None, MOE_GROUPS, D_MODEL, GROUP_HID), lambda i, c: (layer, 0, 0, 0),
                         pipeline_mode=pl.Buffered(1))
    return pl.pallas_call(
        functools.partial(_moe_kernel, final=final, tm=tm),
        grid_spec=pltpu.PrefetchScalarGridSpec(
            num_scalar_prefetch=1,
            grid=(n // tm,),
            in_specs=[row(D_MODEL), row(LANES), row(D_MODEL),
                      pl.BlockSpec((None, None, 6, D_MODEL), lambda i, c: (layer, mod_row + i // mod_tiles, 0, 0)),
                      wspec, wspec, wspec,
                      pl.BlockSpec((1, D_MODEL), lambda i, c: (0, 0))],
            out_specs=row(D_MODEL),
            scratch_shapes=[pltpu.VMEM((tm, D_MODEL), BF16), pltpu.VMEM((tm, LANES), F32),
                            pltpu.VMEM((tm, D_MODEL), F32), pltpu.VMEM((tm, tm), BF16)]),
        out_shape=jax.ShapeDtypeStruct((n, D_MODEL), F32),
        compiler_params=_cparams("arbitrary"),
        name="moe",
    )(cnt, h2, route, xm, mods, w1g, w3g, w2g, fg)


def kernel(x_prompt, x_sample, cache_a_k, cache_a_v, cache_b_k, cache_b_v, state_ret, state_ssm, c, c_ctx, mod_w, mod_b, norm1_g, norm2_g, w_in, a_qn_g, a_kn_g, b_rel_bias, ret_decay, ret_gn_g, s5_lam_re, s5_lam_im, s5_log_dt, s5_b_re, s5_b_im, s5_c_re, s5_c_im, s5_d, s5_glu_w, w_out, moe_gw, moe_gb, moe_ew, moe_eb, moe_w1, moe_w3, moe_w2, final_norm_g):
    n_ctx = BATCH * SEQ
    n_lat = DEC_BATCH * DEC_SEQ
    lat_seg = DEC_SEQ // S5_SEG

    cond = jnp.zeros((SUBLANES, D_MODEL), F32).at[0].set(c_ctx).at[1:1 + DEC_BATCH].set(c)
    mods = _modulation(cond, mod_w, mod_b).reshape(DEPTH, SUBLANES, 6, D_MODEL)

    rope_tabs = _rope_tables()
    s5_a, s5_bm, s5_cre, s5_cim = _s5_prepare(s5_lam_re, s5_lam_im, s5_log_dt, s5_b_re, s5_b_im,
                                              s5_c_re, s5_c_im)
    cak = cache_a_k.reshape(DEC_BATCH, DEPTH, PAST_LEN, A_KV_HEADS * HEAD_DIM)
    cav = cache_a_v.reshape(DEC_BATCH, DEPTH, PAST_LEN, A_KV_HEADS * HEAD_DIM)
    cbk = cache_b_k.reshape(DEC_BATCH, DEPTH, PAST_LEN, B_HEADS * HEAD_DIM)
    cbv = cache_b_v.reshape(DEC_BATCH, DEPTH, PAST_LEN, B_HEADS * HEAD_DIM)

    xc = x_prompt.reshape(n_ctx, D_MODEL)
    xs = x_sample.reshape(n_lat, D_MODEL)
    w1_all, w3_all, w2_all = _moe_weights(moe_w1, moe_w3, moe_w2)
    eye_h = jnp.eye(C_HEADS, dtype=F32)
    s0_bd = (state_ret[:, :, :, :, :, None, :] * eye_h[None, None, None, :, None, :, None]).reshape(
        DEC_BATCH, DEPTH, 2, C_HEADS * HEAD_DIM, C_HEADS * HEAD_DIM)
    ctx_state = ssm_states = None
    h0_zero = jnp.zeros((2, SUBLANES, 2 * S5_SP), F32)
    w_in_bf = w_in.astype(BF16)
    wo_bf = w_out.astype(BF16)
    glu_bf = s5_glu_w.astype(BF16)
    for l in range(DEPTH):
        final = l == DEPTH - 1
        g1 = norm1_g[l].reshape(1, D_MODEL)
        g2 = norm2_g[l].reshape(1, D_MODEL)
        fg = final_norm_g.reshape(1, D_MODEL)
        qn = jnp.tile(a_qn_g[l], A_HEADS).reshape(1, 256)
        kn = jnp.tile(a_kn_g[l], A_KV_HEADS).reshape(1, 128)
        dec = jnp.broadcast_to(ret_decay[l].reshape(2 * C_HEADS, 1), (2 * C_HEADS, LANES))
        gn = ret_gn_g[l].reshape(1, 256)
        dvec = s5_d[l].reshape(1, GROUP_WIDTH)
        wr = jnp.zeros((D_MODEL, LANES), F32).at[:, :MOE_GROUPS].set(moe_gw[l]).at[
            :, ROUTER_OFF:ROUTER_OFF + MOE_EXPERTS].set(moe_ew[l])
        br = jnp.zeros((1, LANES), F32).at[0, :MOE_GROUPS].set(moe_gb[l]).at[
            0, ROUTER_OFF:ROUTER_OFF + MOE_EXPERTS].set(moe_eb[l])
        wr_hi = wr.astype(BF16)
        wr_lo = (wr - wr_hi.astype(F32)).astype(BF16)
        na_bias = _na_bias(b_rel_bias[l])

        oa, ob, oc, du_tm, ctx_state = _ctx_front(xc, mods, g1, w_in_bf, qn, kn, dec, gn, l, ctx_state)
        od_tm, ssm_states = _s5(du_tm, h0_zero, s5_a, s5_bm, s5_cre, s5_cim, dvec, glu_bf, l,
                                nseg=1, fin_layer=l, fin_layers=DEPTH, prev_fin=ssm_states)
        xm, h2, route, counts = _output_stage(xc, (oa, ob, oc, od_tm), mods, 0, n_ctx, wo_bf, g2,
                                              wr_hi, wr_lo, br, l)
        xc = _moe(h2, route, counts, xm, mods, 0, n_ctx, w1_all, w3_all, w2_all, fg, l, final=final)

        zs, cg, du_tm = _project(xs, mods, 1, DEC_SEQ, g1, w_in_bf, qn, kn, rope_tabs, l, seq_len=DEC_SEQ)
        oa = _lat_attention_a(zs, cak, cav, l)
        ob = _lat_attention_b(zs, cbk, cbv, na_bias, l)
        oc = _retention(zs, cg, dec, gn, s0_bd, l, nb=DEC_BATCH, seq_len=DEC_SEQ)
        h0 = state_ssm[:, l].reshape(DEC_BATCH, 2, 2 * S5_SP).transpose(1, 0, 2)
        h0_seg = jnp.zeros((2, DEC_BATCH, lat_seg, 2 * S5_SP), F32)
        h0_seg = h0_seg.at[0, :, 0].set(h0[0]).at[1, :, lat_seg - 1].set(h0[1])
        od_tm, _ = _s5(du_tm, h0_seg.reshape(2, SUBLANES, 2 * S5_SP),
                       s5_a, s5_bm, s5_cre, s5_cim, dvec, glu_bf, l, nseg=lat_seg)
        xm, h2, route, counts = _output_stage(xs, (oa, ob, oc, od_tm), mods, 1, DEC_SEQ, wo_bf, g2,
                                              wr_hi, wr_lo, br, l)
        xs = _moe(h2, route, counts, xm, mods, 1, DEC_SEQ, w1_all, w3_all, w2_all, fg, l, final=final)

    new_ak, new_av, new_bk, new_bv, ret_states = ctx_state
    return (xc.reshape(BATCH, SEQ, D_MODEL), xs.reshape(DEC_BATCH, DEC_SEQ, D_MODEL),
            new_ak.reshape(BATCH, DEPTH, SEQ, A_KV_HEADS, HEAD_DIM),
            new_av.reshape(BATCH, DEPTH, SEQ, A_KV_HEADS, HEAD_DIM),
            new_bk.reshape(BATCH, DEPTH, SEQ, B_HEADS, HEAD_DIM),
            new_bv.reshape(BATCH, DEPTH, SEQ, B_HEADS, HEAD_DIM),
            ret_states,
            ssm_states.reshape(BATCH, DEPTH, 2, 2, S5_GROUPS, S5_STATE))
```

```python
import functools
import math

import numpy as np
import jax
import jax.numpy as jnp
from jax import lax
from jax.experimental import pallas as pl
from jax.experimental.pallas import tpu as pltpu

F32 = jnp.float32
BF16 = jnp.bfloat16

D_MODEL = 1024
BATCH = 32
SEQ = 256
DEPTH = 2
DEC_BATCH = 2
DEC_SEQ = 1024
PAST_LEN = 256
GRID_W = 64
HEAD_DIM = 64
GROUP_WIDTH = 256
A_HEADS = 4
A_KV_HEADS = 2
B_HEADS = 4
NA_ROWS = 8
NA_COLS = 16
C_HEADS = 4
S5_CH = 16
S5_GROUPS = 16
S5_STATE = 64
MOE_GROUPS = 4
MOE_PER_GROUP = 8
MOE_EXPERTS = 32
MOE_HIDDEN = 128
ROPE_THETA = 10000.0
EPS = 1e-6
IN_WIDTH = 2560
Q_SCALE = HEAD_DIM ** -0.5

OFF_AQ, OFF_AK, OFF_AV = 0, 256, 384
OFF_BQ, OFF_BK, OFF_BV = 512, 768, 1024
OFF_CQ, OFF_CK, OFF_CV, OFF_CG = 1280, 1536, 1792, 2048
OFF_DU = 2304

LANES = 128
SUBLANES = 8
BF16_ROWS = 16
S5_SP = S5_GROUPS * S5_STATE
S5_SEG = 256
S5_CHUNK = 256
ROUTER_OFF = 4
NEG_BIG = -1e30
VMEM_LIMIT = 56 * 1024 * 1024


def _cparams(*sem):
    return pltpu.CompilerParams(dimension_semantics=sem, vmem_limit_bytes=VMEM_LIMIT)


def _mod_spec(layer, first_row, tiles_per_row, grid_rank):
    if grid_rank == 1:
        return pl.BlockSpec((None, None, 6, D_MODEL), lambda i: (layer, first_row + i // tiles_per_row, 0, 0))
    return pl.BlockSpec((None, None, 6, D_MODEL), lambda i, g: (layer, first_row + i // tiles_per_row, 0, 0))


def _bdot(a, b):
    return jnp.dot(a.astype(BF16), b.astype(BF16), preferred_element_type=F32)


def _bdot_nt(a, b):
    return lax.dot_general(a.astype(BF16), b.astype(BF16), (((1,), (1,)), ((), ())),
                           preferred_element_type=F32)


def _bdot_tn(a, b):
    return lax.dot_general(a.astype(BF16), b.astype(BF16), (((0,), (0,)), ((), ())),
                           preferred_element_type=F32)


def _split(a):
    hi = a.astype(BF16)
    lo = (a - hi.astype(F32)).astype(BF16)
    return hi, lo


def _dot_hilo_lhs(a, b_bf16):
    hi, lo = _split(a)
    return (jnp.dot(hi, b_bf16, preferred_element_type=F32)
            + jnp.dot(lo, b_bf16, preferred_element_type=F32))


def _rms_rows(x):
    return x * lax.rsqrt(jnp.mean(x * x, axis=-1, keepdims=True) + EPS)


def _mod_kernel(cond_ref, w_ref, b_ref, o_ref):
    o_ref[...] = _bdot(jax.nn.silu(cond_ref[...]), w_ref[...]) + b_ref[...]


def _modulation(cond, mod_w, mod_b):
    tn = 1536
    return pl.pallas_call(
        _mod_kernel,
        grid=(DEPTH, 6 * D_MODEL // tn),
        in_specs=[pl.BlockSpec((SUBLANES, D_MODEL), lambda l, j: (0, 0)),
                  pl.BlockSpec((None, D_MODEL, tn), lambda l, j: (l, 0, j)),
                  pl.BlockSpec((None, 1, tn), lambda l, j: (l, 0, j))],
        out_specs=pl.BlockSpec((None, SUBLANES, tn), lambda l, j: (l, 0, j)),
        out_shape=jax.ShapeDtypeStruct((DEPTH, SUBLANES, 6 * D_MODEL), F32),
        compiler_params=_cparams("arbitrary", "arbitrary"),
        name="modulation",
    )(cond, mod_w, mod_b.reshape(DEPTH, 1, 6 * D_MODEL))


def _group_mean_matrix(w):
    ri = lax.broadcasted_iota(jnp.int32, (w, w), 0) // HEAD_DIM
    ci = lax.broadcasted_iota(jnp.int32, (w, w), 1) // HEAD_DIM
    return jnp.where(ri == ci, 1.0 / HEAD_DIM, 0.0).astype(BF16)


def _head_norm(t, g):
    ms = _dot_hilo_lhs(t * t, _group_mean_matrix(t.shape[1]))
    return t * lax.rsqrt(ms + EPS) * g


def _rope(t, cos, sa, sb):
    return (t * cos + pltpu.roll(t, LANES - 16, 1) * sa + pltpu.roll(t, 16, 1) * sb)


def _store_layer_slot(ref, slot, value):
    for s in range(ref.shape[0]):
        ref[s] = value if s == slot else jnp.zeros_like(value)


def _layer_slot_block(layer, first_call, tail):
    if first_call:
        return (None, DEPTH) + tail, (0,) * (1 + len(tail)), layer
    return (None, 1) + tail, (layer,) + (0,) * len(tail), 0


def _proj_kernel(x_ref, mod_ref, g1_ref, w_ref, qn_ref, kn_ref, cos_ref, sa_ref, sb_ref, z_ref, cg_ref, du_ref):
    h = _rms_rows(x_ref[...]) * g1_ref[...] * (1.0 + mod_ref[1:2, :]) + mod_ref[0:1, :]
    z = jnp.dot(h.astype(BF16), w_ref[...], preferred_element_type=F32)
    aq = _head_norm(z[:, OFF_AQ:OFF_AK], qn_ref[...])
    ak = _head_norm(z[:, OFF_AK:OFF_AV], kn_ref[...])
    for j in range(3):
        t = aq[:, j * LANES:(j + 1) * LANES] if j < 2 else ak
        sl = slice(0, LANES) if j == 2 else slice(j * LANES, (j + 1) * LANES)
        t = _rope(t, cos_ref[:, sl], sa_ref[:, sl], sb_ref[:, sl])
        z_ref[:, j * LANES:(j + 1) * LANES] = t.astype(BF16)
    z_ref[:, OFF_AV:OFF_CK] = z[:, OFF_AV:OFF_CK].astype(BF16)
    z_ref[:, OFF_CK:OFF_CV] = (z[:, OFF_CK:OFF_CV] * Q_SCALE).astype(BF16)
    z_ref[:, OFF_CV:OFF_CG] = z[:, OFF_CV:OFF_CG].astype(BF16)
    cg_ref[...] = z[:, OFF_CG:OFF_DU]
    du_ref[...] = z[:, OFF_DU:]


def _du_spec(grid_rank):
    if grid_rank == 1:
        return pl.BlockSpec((None, S5_SEG, GROUP_WIDTH), lambda i: (i // SUBLANES, 0, i % SUBLANES))
    return pl.BlockSpec((None, S5_SEG, GROUP_WIDTH), lambda i, g: (i // SUBLANES, 0, i % SUBLANES))


def _project(x, mods, mod_row, mod_tokens, g1, w_in_bf, qn, kn, rope_tabs, layer, *, seq_len):
    tm = S5_SEG
    n = x.shape[0]
    tps = seq_len // tm
    return pl.pallas_call(
        _proj_kernel,
        grid=(n // tm,),
        in_specs=[pl.BlockSpec((tm, D_MODEL), lambda i: (i, 0)),
                  _mod_spec(layer, mod_row, mod_tokens // tm, 1),
                  pl.BlockSpec((1, D_MODEL), lambda i: (0, 0)),
                  pl.BlockSpec((None, D_MODEL, IN_WIDTH), lambda i: (layer, 0, 0)),
                  pl.BlockSpec((1, 256), lambda i: (0, 0)),
                  pl.BlockSpec((1, 128), lambda i: (0, 0))]
                 + [pl.BlockSpec((tm, 256), lambda i: (i % tps, 0))] * 3,
        out_specs=[pl.BlockSpec((tm, OFF_CG), lambda i: (i, 0)),
                   pl.BlockSpec((tm, GROUP_WIDTH), lambda i: (i, 0)), _du_spec(1)],
        out_shape=[jax.ShapeDtypeStruct((n, OFF_CG), BF16),
                   jax.ShapeDtypeStruct((n, GROUP_WIDTH), F32),
                   jax.ShapeDtypeStruct((n // (tm * SUBLANES), S5_SEG, SUBLANES * GROUP_WIDTH), F32)],
        compiler_params=_cparams("parallel"),
        name="project",
    )(x, mods, g1, w_in_bf, qn, kn, *rope_tabs)


def _rope_tables():
    t = jnp.arange(DEC_SEQ)
    row = (t // GRID_W).astype(F32)
    col = (t % GRID_W).astype(F32)
    nf = HEAD_DIM // 4
    inv = ROPE_THETA ** (-jnp.arange(nf, dtype=F32) / nf)
    ang_r = row[:, None] * inv[None, :]
    ang_c = col[:, None] * inv[None, :]
    zeros = jnp.zeros_like(ang_r)
    cos = jnp.concatenate([jnp.cos(ang_r), jnp.cos(ang_r), jnp.cos(ang_c), jnp.cos(ang_c)], axis=-1)
    sa = jnp.concatenate([-jnp.sin(ang_r), zeros, -jnp.sin(ang_c), zeros], axis=-1)
    sb = jnp.concatenate([zeros, jnp.sin(ang_r), zeros, jnp.sin(ang_c)], axis=-1)
    return tuple(jnp.tile(a, (1, 4)) for a in (cos, sa, sb))


N_HEADS = 4


def _lane_head(width):
    return lax.broadcasted_iota(jnp.int32, (1, width), 1) // HEAD_DIM


def _stack_heads(q):
    head = _lane_head(q.shape[1])
    return jnp.concatenate([jnp.where(head == h, q, 0.0) for h in range(N_HEADS)], axis=0).astype(BF16)


def _stack_heads_gqa(q):
    lo = lax.broadcasted_iota(jnp.int32, (1, LANES), 1) < HEAD_DIM
    q = q.astype(F32)
    q01, q23 = q[:, :LANES], q[:, LANES:]
    blocks = [jnp.where(lo, q01, 0.0), jnp.where(lo, pltpu.roll(q01, HEAD_DIM, 1), 0.0),
              jnp.where(lo, 0.0, pltpu.roll(q23, HEAD_DIM, 1)), jnp.where(lo, 0.0, q23)]
    return jnp.concatenate(blocks, axis=0).astype(BF16)


def _spread_kv_gqa(v):
    lo = lax.broadcasted_iota(jnp.int32, (1, LANES), 1) < HEAD_DIM
    v = v.astype(F32)
    vr = pltpu.roll(v, HEAD_DIM, 1)
    return jnp.concatenate([jnp.where(lo, v, vr), jnp.where(lo, vr, v)], axis=1)


def _mha(qs, blocks, tq):
    scores = []
    for k, _, bias in blocks:
        s = _bdot_nt(qs, k)
        scores.append(s if bias is None else s + bias)
    m = functools.reduce(jnp.maximum, [jnp.max(s, axis=-1, keepdims=True) for s in scores])
    es = [jnp.exp(s - m) for s in scores]
    denom = functools.reduce(jnp.add, [jnp.sum(e, axis=-1, keepdims=True) for e in es])
    ps = [e.astype(BF16) for e in es]
    head = _lane_head(N_HEADS * HEAD_DIM)
    vals = [v.astype(BF16) for _, v, _ in blocks]
    o = None
    dall = None
    for h in range(N_HEADS):
        rows = slice(h * tq, (h + 1) * tq)
        for p, v in zip(ps, vals):
            t = jnp.dot(p[rows], jnp.where(head == h, v, jnp.zeros_like(v)), preferred_element_type=F32)
            o = t if o is None else o + t
        d = jnp.where(head == h, denom[rows], 0.0)
        dall = d if dall is None else dall + d
    return (o / dall).astype(BF16)


def _lat_attn_a_kernel(q_ref, kn_ref, vn_ref, kc_ref, vc_ref, o_ref):
    o_ref[...] = _mha(_stack_heads_gqa(q_ref[...] * Q_SCALE),
                      [(kc_ref[...], _spread_kv_gqa(vc_ref[...]), None),
                       (kn_ref[...], _spread_kv_gqa(vn_ref[...]), None)], q_ref.shape[0])


def _lat_attention_a(z, cache_k, cache_v, layer, tq=256):
    nq = DEC_SEQ // tq
    cache_spec = pl.BlockSpec((None, None, PAST_LEN, 128), lambda b, j: (b, layer, 0, 0))
    return pl.pallas_call(
        _lat_attn_a_kernel,
        grid=(DEC_BATCH, nq),
        in_specs=[pl.BlockSpec((tq, 256), lambda b, j: (b * nq + j, OFF_AQ // 256)),
                  pl.BlockSpec((DEC_SEQ, 128), lambda b, j: (b, OFF_AK // 128)),
                  pl.BlockSpec((DEC_SEQ, 128), lambda b, j: (b, OFF_AV // 128)),
                  cache_spec, cache_spec],
        out_specs=pl.BlockSpec((tq, 256), lambda b, j: (b * nq + j, 0)),
        out_shape=jax.ShapeDtypeStruct((DEC_BATCH * DEC_SEQ, 256), BF16),
        compiler_params=_cparams("parallel", "parallel"),
        name="lat_attention_a",
    )(z, z, z, cache_k, cache_v)


NA_KEYS = NA_ROWS * GRID_W


NA_PAIRS = 2 * NA_ROWS - 2


NA_STEP_ROWS = 2


def _na_kernel(q_ref, k_ref, v_ref, kc_ref, vc_ref, bias_ref, o_ref):
    rows = DEC_SEQ // GRID_W
    outs = []
    for rr in range(NA_STEP_ROWS):
        r = pl.program_id(1) * NA_STEP_ROWS + rr
        row_start = jnp.clip(r - NA_ROWS // 2, 0, rows - NA_ROWS)
        start = pl.multiple_of(row_start * GRID_W, GRID_W)
        rel0 = row_start - r + NA_ROWS - 1
        kl = k_ref[pl.ds(start, NA_KEYS), :]
        vl = v_ref[pl.ds(start, NA_KEYS), :]
        bias = jnp.concatenate(
            [jnp.concatenate([bias_ref[h, rel0 + 2 * jp] for jp in range(NA_ROWS // 2)], axis=1)
             for h in range(B_HEADS)], axis=0)
        qrows = slice(rr * GRID_W, (rr + 1) * GRID_W)
        outs.append(_mha(_stack_heads(q_ref[qrows, :] * Q_SCALE),
                         [(kl, vl, bias), (kc_ref[...], vc_ref[...], None)], GRID_W))
    o_ref[...] = jnp.concatenate(outs, axis=0)


def _na_bias(rel_bias):
    nrel = 2 * NA_COLS - 1
    period = 2 * GRID_W
    b = rel_bias.astype(F32)
    ext = jnp.concatenate([b[..., NA_COLS - 1:],
                           jnp.zeros(b.shape[:-1] + (period - nrel,), F32),
                           b[..., :NA_COLS - 1]], axis=-1)
    flat = jnp.tile(ext, (1, 1, GRID_W))[..., :GRID_W * (period - 1)]
    toe = flat.reshape(b.shape[:-1] + (GRID_W, period - 1))[..., :GRID_W]
    col_start = np.clip(np.arange(GRID_W) - NA_COLS // 2, 0, GRID_W - NA_COLS)
    kc = np.arange(GRID_W)
    inside = (kc[None, :] >= col_start[:, None]) & (kc[None, :] < col_start[:, None] + NA_COLS)
    toe = jnp.where(jnp.asarray(inside), toe, NEG_BIG)
    return jnp.concatenate([toe[:, :-1], toe[:, 1:]], axis=-1)


def _lat_attention_b(z, cache_k, cache_v, bias, layer):
    rows = DEC_SEQ // GRID_W // NA_STEP_ROWS
    tq = NA_STEP_ROWS * GRID_W
    cache_spec = pl.BlockSpec((None, None, PAST_LEN, 256), lambda b, r: (b, layer, 0, 0))
    return pl.pallas_call(
        _na_kernel,
        grid=(DEC_BATCH, rows),
        in_specs=[pl.BlockSpec((tq, 256), lambda b, r: (b * rows + r, OFF_BQ // 256)),
                  pl.BlockSpec((DEC_SEQ, 256), lambda b, r: (b, OFF_BK // 256)),
                  pl.BlockSpec((DEC_SEQ, 256), lambda b, r: (b, OFF_BV // 256)),
                  cache_spec, cache_spec,
                  pl.BlockSpec((B_HEADS, NA_PAIRS, GRID_W, 2 * GRID_W), lambda b, r: (0, 0, 0, 0))],
        out_specs=pl.BlockSpec((tq, 256), lambda b, r: (b * rows + r, 0)),
        out_shape=jax.ShapeDtypeStruct((DEC_BATCH * DEC_SEQ, 256), BF16),
        compiler_params=_cparams("parallel", "parallel"),
        name="lat_attention_b",
    )(z, z, z, cache_k, cache_v, bias)


def _retention_core(q, k, v, g, dec_ref, gn_ref, dec_scr, *, seq_len, i0, decay_fill, s0_ref=None,
                    want_state=False):
    tq = q.shape[0]
    head = _lane_head(C_HEADS * HEAD_DIM)
    lg = jax.nn.log_sigmoid(dec_ref[...])

    def per_lane(row0):
        out = jnp.zeros((1, C_HEADS * HEAD_DIM), F32)
        for h in range(C_HEADS):
            out = jnp.where(head == h, lg[row0 + h:row0 + h + 1, 0:1], out)
        return out

    lgf_l, lgb_l = per_lane(0), per_lane(C_HEADS)
    qi = (i0 + lax.broadcasted_iota(jnp.int32, (tq, 1), 0)).astype(F32)

    def fill_decay():
        kj = lax.broadcasted_iota(jnp.int32, (1, seq_len), 1).astype(F32)
        diff = qi - kj
        for h in range(C_HEADS):
            lgf = lg[h:h + 1, 0:1]
            lgb = lg[C_HEADS + h:C_HEADS + h + 1, 0:1]
            dec_scr[h * tq:(h + 1) * tq, :] = (
                jnp.where(diff >= 0, jnp.exp(lgf * jnp.maximum(diff, 0.0)), 0.0)
                + jnp.where(diff <= 0, jnp.exp(lgb * jnp.maximum(-diff, 0.0)), 0.0))

    if decay_fill == "first_step":
        pl.when(pl.program_id(0) == 0)(fill_decay)
    elif decay_fill == "every_step":
        fill_decay()
    else:
        assert decay_fill == "filled"

    v = v.astype(BF16)
    sc = (_bdot_nt(_stack_heads(q), k) * dec_scr[...]).astype(BF16)
    o = None
    for h in range(C_HEADS):
        t = jnp.dot(sc[h * tq:(h + 1) * tq], jnp.where(head == h, v, jnp.zeros_like(v)),
                    preferred_element_type=F32)
        o = t if o is None else o + t
    if s0_ref is not None:
        o = (o + _bdot(q, s0_ref[0]) * jnp.exp(lgf_l * (qi + 1.0))
             + _bdot(q, s0_ref[1]) * jnp.exp(lgb_l * (seq_len - qi)))
    gm = _group_mean_matrix(C_HEADS * HEAD_DIM)
    dlt = o - _dot_hilo_lhs(o, gm)
    var = _dot_hilo_lhs(dlt * dlt, gm)
    out = (dlt * lax.rsqrt(var + EPS) * gn_ref[...] * jax.nn.silu(g)).astype(BF16)
    if not want_state:
        return out, None
    kpos = lax.broadcasted_iota(jnp.int32, (seq_len, 1), 0).astype(F32)
    sf = _bdot_tn(k * jnp.exp(lgf_l * (seq_len - 1.0 - kpos)), v)
    sb = _bdot_tn(k * jnp.exp(lgb_l * kpos), v)
    return out, (sf, sb)


def _store_retention_state(st_ref, slot, state):
    for s in range(st_ref.shape[0]):
        for d in range(2):
            for h in range(C_HEADS):
                sl = slice(h * HEAD_DIM, (h + 1) * HEAD_DIM)
                st_ref[s, d, h] = state[d][sl, sl] if s == slot else jnp.zeros((HEAD_DIM, HEAD_DIM), F32)


def _retention_kernel(q_ref, g_ref, k_ref, v_ref, dec_ref, gn_ref, s0_ref, o_ref, dec_scr, *, seq_len, tq):
    o_ref[...], _ = _retention_core(q_ref[...], k_ref[...], v_ref[...], g_ref[...], dec_ref, gn_ref, dec_scr,
                                    seq_len=seq_len, i0=pl.program_id(1) * tq, decay_fill="every_step",
                                    s0_ref=s0_ref)


def _retention(z, cg, dec, gn, s0, layer, *, nb, seq_len, tq=256):
    nq = seq_len // tq
    return pl.pallas_call(
        functools.partial(_retention_kernel, seq_len=seq_len, tq=tq),
        grid=(nb, nq),
        in_specs=[pl.BlockSpec((tq, 256), lambda b, j: (b * nq + j, OFF_CQ // 256)),
                  pl.BlockSpec((tq, 256), lambda b, j: (b * nq + j, 0)),
                  pl.BlockSpec((seq_len, 256), lambda b, j: (b, OFF_CK // 256)),
                  pl.BlockSpec((seq_len, 256), lambda b, j: (b, OFF_CV // 256)),
                  pl.BlockSpec((SUBLANES, LANES), lambda b, j: (0, 0)),
                  pl.BlockSpec((1, 256), lambda b, j: (0, 0)),
                  pl.BlockSpec((None, None, 2, 256, 256), lambda b, j: (b, layer, 0, 0, 0))],
        out_specs=pl.BlockSpec((tq, 256), lambda b, j: (b * nq + j, 0)),
        out_shape=jax.ShapeDtypeStruct((nb * seq_len, 256), BF16),
        scratch_shapes=[pltpu.VMEM((C_HEADS * tq, seq_len), F32)],
        compiler_params=_cparams("parallel", "parallel"),
        name="retention",
    )(z, cg, z, z, dec, gn, s0)


CTX_SEQS = 2


def _ctx_front_kernel(x_ref, mod_ref, g1_ref, w_ref, qn_ref, kn_ref, dec_ref, gn_ref, *rest, n_alias, slot):
    (oa_ref, ob_ref, oc_ref, du_ref, ak_ref, av_ref, bk_ref, bv_ref, st_ref, dec_scr) = rest[n_alias:]
    tq = x_ref.shape[0] // CTX_SEQS
    h = _rms_rows(x_ref[...]) * g1_ref[...] * (1.0 + mod_ref[1:2, :]) + mod_ref[0:1, :]
    zz = jnp.dot(h.astype(BF16), w_ref[...], preferred_element_type=F32)
    for s in range(CTX_SEQS):
        rows = slice(s * tq, (s + 1) * tq)
        z = zz[rows, :]
        aq = _head_norm(z[:, OFF_AQ:OFF_AK], qn_ref[...])
        ak = _head_norm(z[:, OFF_AK:OFF_AV], kn_ref[...])
        av, bq, bk, bv = (z[:, OFF_AV:OFF_BQ], z[:, OFF_BQ:OFF_BK], z[:, OFF_BK:OFF_BV], z[:, OFF_BV:OFF_CQ])
        oa_ref[rows, :] = _mha(_stack_heads_gqa(aq * Q_SCALE), [(ak, _spread_kv_gqa(av), None)], tq)
        ob_ref[rows, :] = _mha(_stack_heads(bq * Q_SCALE), [(bk, bv, None)], tq)
        oc_ref[rows, :], state = _retention_core(
            z[:, OFF_CQ:OFF_CK], z[:, OFF_CK:OFF_CV] * Q_SCALE, z[:, OFF_CV:OFF_CG], z[:, OFF_CG:OFF_DU],
            dec_ref, gn_ref, dec_scr, seq_len=tq, i0=0, decay_fill="first_step" if s == 0 else "filled",
            want_state=True)
        du_ref[:, s * GROUP_WIDTH:(s + 1) * GROUP_WIDTH] = z[:, OFF_DU:]
        _store_layer_slot(ak_ref.at[s], slot, ak)
        _store_layer_slot(av_ref.at[s], slot, av)
        _store_layer_slot(bk_ref.at[s], slot, bk)
        _store_layer_slot(bv_ref.at[s], slot, bv)
        _store_retention_state(st_ref.at[s], slot, state)


def _ctx_front(x, mods, g1, w_in_bf, qn, kn, dec, gn, layer, prev):
    assert SEQ == S5_SEG
    tm = CTX_SEQS * SEQ
    n = x.shape[0]
    nb = n // SEQ
    steps = n // tm
    per_blk = SUBLANES // CTX_SEQS
    const = lambda *shape: pl.BlockSpec(shape, lambda i: (0,) * len(shape))
    row = lambda w: pl.BlockSpec((tm, w), lambda i: (i, 0))
    in_specs = [row(D_MODEL), _mod_spec(layer, 0, steps, 1), const(1, D_MODEL),
                pl.BlockSpec((None, D_MODEL, IN_WIDTH), lambda i: (layer, 0, 0)),
                const(1, 256), const(1, 128), const(SUBLANES, LANES), const(1, 256)]
    args = [x, mods, g1, w_in_bf, qn, kn, dec, gn]
    out_specs = [row(256), row(256), row(256),
                 pl.BlockSpec((None, S5_SEG, CTX_SEQS * GROUP_WIDTH), lambda i: (i // per_blk, 0, i % per_blk))]
    out_shape = [jax.ShapeDtypeStruct((n, 256), BF16)] * 3 + [
        jax.ShapeDtypeStruct((nb // SUBLANES, S5_SEG, SUBLANES * GROUP_WIDTH), F32)]
    first = prev is None
    slot = 0
    for tail in ((SEQ, 128), (SEQ, 128), (SEQ, 256), (SEQ, 256), (2, C_HEADS, HEAD_DIM, HEAD_DIM)):
        blk, idx, slot = _layer_slot_block(layer, first, tail)
        out_specs.append(pl.BlockSpec((CTX_SEQS,) + blk[1:], lambda i, idx=idx: (i,) + idx))
        out_shape.append(jax.ShapeDtypeStruct((nb, DEPTH) + tail, F32))
    aliases = {}
    if not first:
        for k, arr in enumerate(prev):
            aliases[len(args)] = 4 + k
            in_specs.append(pl.BlockSpec(memory_space=pl.ANY))
            args.append(arr)
    outs = pl.pallas_call(
        functools.partial(_ctx_front_kernel, n_alias=len(aliases), slot=slot),
        grid=(steps,),
        in_specs=in_specs,
        out_specs=out_specs,
        out_shape=out_shape,
        scratch_shapes=[pltpu.VMEM((C_HEADS * SEQ, SEQ), F32)],
        input_output_aliases=aliases,
        compiler_params=_cparams("arbitrary"),
        name="ctx_front",
    )(*args)
    return outs[0], outs[1], outs[2], outs[3], tuple(outs[4:])


def _s5_prep_kernel(lre_ref, lim_ref, ldt_ref, bre_ref, bim_ref, cre_ref, cim_ref,
                    a_ref, bm_ref, cro_ref, cio_ref, bm_scr, cr_scr, ci_scr):
    lre = lre_ref[...]
    lim = lim_ref[...]
    dt = jnp.exp(ldt_ref[...])
    mag = jnp.exp(lre * dt)
    a_re = mag * jnp.cos(lim * dt)
    a_im = mag * jnp.sin(lim * dt)
    den = lre * lre + lim * lim
    r_re = ((a_re - 1.0) * lre + a_im * lim) / den
    r_im = (a_im * lre - (a_re - 1.0) * lim) / den
    bm_scr[...] = jnp.zeros_like(bm_scr)
    cr_scr[...] = jnp.zeros_like(cr_scr)
    ci_scr[...] = jnp.zeros_like(ci_scr)
    for g in range(S5_GROUPS):
        rows = slice(g * S5_CH, (g + 1) * S5_CH)
        cols = slice(g * S5_STATE, (g + 1) * S5_STATE)
        a_ref[0:1, cols] = a_re[g:g + 1, :]
        a_ref[1:2, cols] = a_im[g:g + 1, :]
        rr, ri = r_re[g:g + 1, :], r_im[g:g + 1, :]
        br, bi = bre_ref[g], bim_ref[g]
        bm_scr[rows, cols] = rr * br - ri * bi
        bm_scr[rows, S5_SP + g * S5_STATE:S5_SP + (g + 1) * S5_STATE] = rr * bi + ri * br
        cr_scr[cols, rows] = cre_ref[g]
        ci_scr[cols, rows] = cim_ref[g]
    bm_ref[...] = bm_scr[...].astype(BF16)
    cro_ref[...] = cr_scr[...].astype(BF16)
    cio_ref[...] = ci_scr[...].astype(BF16)


def _s5_prepare(lam_re, lam_im, log_dt, b_re, b_im, c_re, c_im):
    gp = (S5_GROUPS, S5_STATE)
    ldt = jnp.broadcast_to(log_dt[..., None], (DEPTH, 2) + gp)
    bt = [jnp.swapaxes(t, -1, -2) for t in (b_re, b_im)]
    ct = [jnp.swapaxes(t, -1, -2) for t in (c_re, c_im)]

    def spec(*tail):
        return pl.BlockSpec((None, None) + tail, lambda l, d: (l, d) + (0,) * len(tail))

    return pl.pallas_call(
        _s5_prep_kernel,
        grid=(DEPTH, 2),
        in_specs=[spec(*gp)] * 3 + [spec(S5_GROUPS, S5_CH, S5_STATE)] * 2 + [spec(S5_GROUPS, S5_STATE, S5_CH)] * 2,
        out_specs=[spec(2, S5_SP), spec(GROUP_WIDTH, 2 * S5_SP), spec(S5_SP, GROUP_WIDTH), spec(S5_SP, GROUP_WIDTH)],
        out_shape=[jax.ShapeDtypeStruct((DEPTH, 2, 2, S5_SP), F32),
                   jax.ShapeDtypeStruct((DEPTH, 2, GROUP_WIDTH, 2 * S5_SP), BF16),
                   jax.ShapeDtypeStruct((DEPTH, 2, S5_SP, GROUP_WIDTH), BF16),
                   jax.ShapeDtypeStruct((DEPTH, 2, S5_SP, GROUP_WIDTH), BF16)],
        scratch_shapes=[pltpu.VMEM((GROUP_WIDTH, 2 * S5_SP), F32), pltpu.VMEM((S5_SP, GROUP_WIDTH), F32),
                        pltpu.VMEM((S5_SP, GROUP_WIDTH), F32)],
        compiler_params=_cparams("parallel", "parallel"),
        name="s5_prepare",
    )(lam_re, lam_im, ldt, bt[0], bt[1], ct[0], ct[1])


def _cmul(ar, ai, br, bi):
    return ar * br - ai * bi, ar * bi + ai * br


def _s5_kernel(u_ref, h0_ref, a_ref, bm_ref, cre_ref, cim_ref, dvec_ref, glu_ref, *rest, nseg, slot):
    od_ref, fin_ref, x_scr, s_scr, y_scr = rest[-5:]
    steps = S5_SEG
    rows = steps * SUBLANES
    chunk = S5_CHUNK
    chunk_steps = chunk // SUBLANES
    nchunk = rows // chunk
    seg = lax.broadcasted_iota(jnp.int32, (SUBLANES, S5_SP), 0) % nseg

    for d in range(2):
        ar = jnp.broadcast_to(a_ref[d, 0:1, :], (SUBLANES, S5_SP))
        ai = jnp.broadcast_to(a_ref[d, 1:2, :], (SUBLANES, S5_SP))

        def row0(k):
            c = k if d == 0 else nchunk - 1 - k
            return c * chunk if isinstance(c, int) else pl.multiple_of(c * chunk, chunk)

        def input_part(k, buf):
            x_scr[buf] = jnp.dot(u_ref[pl.ds(row0(k), chunk), :].astype(BF16), bm_ref[d],
                                 preferred_element_type=F32)

        def scan_part(buf, carry, store):
            sr, si = carry
            for t in range(chunk_steps):
                r = (t if d == 0 else chunk_steps - 1 - t) * SUBLANES
                pr, pi = _cmul(ar, ai, sr, si)
                sr = pr + x_scr[buf, r:r + SUBLANES, 0:S5_SP]
                si = pi + x_scr[buf, r:r + SUBLANES, S5_SP:]
                if store:
                    s_scr[buf, r:r + SUBLANES, 0:S5_SP] = sr
                    s_scr[buf, r:r + SUBLANES, S5_SP:] = si
            return sr, si

        def output_part(k, buf):
            y = _bdot(s_scr[buf, :, 0:S5_SP], cre_ref[d]) - _bdot(s_scr[buf, :, S5_SP:], cim_ref[d])
            rows_k = pl.ds(row0(k), chunk)
            if d == 0:
                y_scr[rows_k, :] = y
            else:
                zz = jax.nn.gelu(y_scr[rows_k, :] + y + dvec_ref[...] * u_ref[rows_k, :])
                od_ref[rows_k, :] = (zz * jax.nn.sigmoid(_bdot(zz, glu_ref[...]))).astype(BF16)

        def half(k, buf, carry, store, nxt=True, prev=True):
            if nxt:
                input_part(k + 1, 1 - buf)
            carry = scan_part(buf, carry, store)
            if store and prev:
                output_part(k - 1, 1 - buf)
            return carry

        def run_pass(carry, store):
            input_part(0, 0)
            carry = half(0, 0, carry, store, prev=False)
            carry = half(1, 1, carry, store)

            def pair(j, c):
                c = half(2 * j, 0, c, store)
                return half(2 * j + 1, 1, c, store)
            carry = lax.fori_loop(1, nchunk // 2 - 1, pair, carry)
            carry = half(nchunk - 2, 0, carry, store)
            carry = half(nchunk - 1, 1, carry, store, nxt=False)
            if store:
                output_part(nchunk - 1, 1)
            return carry

        init = (h0_ref[d, :, 0:S5_SP], h0_ref[d, :, S5_SP:])
        if nseg > 1:
            zero = jnp.zeros((SUBLANES, S5_SP), F32)
            fr, fi = run_pass((zero, zero), store=False)
            pr, pi = ar, ai
            for _ in range(int(math.log2(steps))):
                pr, pi = _cmul(pr, pi, pr, pi)
            cr, ci = init
            shift = 1 if d == 0 else SUBLANES - 1
            order = range(1, nseg) if d == 0 else range(nseg - 2, -1, -1)
            for s in order:
                ncr, nci = pltpu.roll(cr, shift, 0), pltpu.roll(ci, shift, 0)
                nfr, nfi = pltpu.roll(fr, shift, 0), pltpu.roll(fi, shift, 0)
                qr, qi = _cmul(pr, pi, ncr, nci)
                cr = jnp.where(seg == s, qr + nfr, cr)
                ci = jnp.where(seg == s, qi + nfi, ci)
            init = (cr, ci)
        sr, si = run_pass(init, store=True)
        for s in range(fin_ref.shape[1] // (4 * S5_SP)):
            base = (4 * s + 2 * d) * S5_SP
            fin_ref[:, base:base + S5_SP] = sr if s == slot else jnp.zeros_like(sr)
            fin_ref[:, base + S5_SP:base + 2 * S5_SP] = si if s == slot else jnp.zeros_like(si)


def _s5(du_tm, h0, a, bmat, cre, cim, dvec, glu_bf, layer, *, nseg, fin_layer=0, fin_layers=1,
        prev_fin=None):
    nblk = du_tm.shape[0]
    rows = S5_SEG * SUBLANES
    fin_w = 4 * S5_SP
    in_specs = [pl.BlockSpec((None, rows, GROUP_WIDTH), lambda i: (i, 0, 0)),
                pl.BlockSpec((2, SUBLANES, 2 * S5_SP), lambda i: (0, 0, 0)),
                pl.BlockSpec((None, 2, 2, S5_SP), lambda i: (layer, 0, 0, 0)),
                pl.BlockSpec((None, 2, GROUP_WIDTH, 2 * S5_SP), lambda i: (layer, 0, 0, 0)),
                pl.BlockSpec((None, 2, S5_SP, GROUP_WIDTH), lambda i: (layer, 0, 0, 0)),
                pl.BlockSpec((None, 2, S5_SP, GROUP_WIDTH), lambda i: (layer, 0, 0, 0)),
                pl.BlockSpec((1, GROUP_WIDTH), lambda i: (0, 0)),
                pl.BlockSpec((None, GROUP_WIDTH, GROUP_WIDTH), lambda i: (layer, 0, 0))]
    args = [du_tm.reshape(nblk, rows, GROUP_WIDTH), h0, a, bmat, cre, cim, dvec, glu_bf]
    aliases = {}
    if prev_fin is not None:
        aliases[len(args)] = 1
        in_specs.append(pl.BlockSpec(memory_space=pl.ANY))
        args.append(prev_fin)
        fin_spec, slot = pl.BlockSpec((SUBLANES, fin_w), lambda i: (i, fin_layer)), 0
    else:
        fin_spec, slot = pl.BlockSpec((SUBLANES, fin_layers * fin_w), lambda i: (i, 0)), fin_layer
    od, fin = pl.pallas_call(
        functools.partial(_s5_kernel, nseg=nseg, slot=slot),
        grid=(nblk,),
        in_specs=in_specs,
        out_specs=[pl.BlockSpec((None, rows, GROUP_WIDTH), lambda i: (i, 0, 0)), fin_spec],
        out_shape=[jax.ShapeDtypeStruct((nblk, rows, GROUP_WIDTH), BF16),
                   jax.ShapeDtypeStruct((nblk * SUBLANES, fin_layers * fin_w), F32)],
        scratch_shapes=[pltpu.VMEM((2, S5_CHUNK, 2 * S5_SP), F32), pltpu.VMEM((2, S5_CHUNK, 2 * S5_SP), F32),
                        pltpu.VMEM((rows, GROUP_WIDTH), F32)],
        input_output_aliases=aliases,
        compiler_params=_cparams("parallel"),
        name="s5",
    )(*args)
    return od.reshape(nblk, S5_SEG, SUBLANES * GROUP_WIDTH), fin


ROUTE_GROUP = MOE_PER_GROUP
OUT_SEQS = 2


def _out_kernel(x_ref, oa_ref, ob_ref, oc_ref, od_ref, mod_ref, wo_ref, g2_ref, wrh_ref, wrl_ref, br_ref,
                xm_ref, h2_ref, route_ref, cnt_ref):
    od = jnp.concatenate([od_ref[:, s * GROUP_WIDTH:(s + 1) * GROUP_WIDTH] for s in range(OUT_SEQS)], axis=0)
    mix = functools.reduce(jnp.add, [
        _bdot(o, wo_ref[i * GROUP_WIDTH:(i + 1) * GROUP_WIDTH, :])
        for i, o in enumerate((oa_ref[...], ob_ref[...], oc_ref[...], od))])
    xm = x_ref[...] + mod_ref[2:3, :] * mix
    xm_ref[...] = xm
    h2 = _rms_rows(xm) * g2_ref[...] * (1.0 + mod_ref[4:5, :]) + mod_ref[3:4, :]
    h2_ref[...] = h2.astype(BF16)

    h_hi, h_lo = _split(h2)
    logits = (jnp.dot(h_hi, wrh_ref[...], preferred_element_type=F32)
              + jnp.dot(h_hi, wrl_ref[...], preferred_element_type=F32)
              + jnp.dot(h_lo, wrh_ref[...], preferred_element_type=F32)) + br_ref[...]
    lane_i = lax.broadcasted_iota(jnp.int32, logits.shape, 1)
    lane = lane_i.astype(F32)
    big = jnp.float32(2 ** 30)
    gmask = lane_i < MOE_GROUPS
    gl = jnp.where(gmask, logits, -jnp.inf)
    gmax = jnp.max(gl, axis=-1, keepdims=True)
    p_top = 1.0 / jnp.sum(jnp.exp(gl - gmax), axis=-1, keepdims=True)
    g_top = jnp.min(jnp.where(gl == gmax, lane, big), axis=-1, keepdims=True)
    e_lane = lane_i - ROUTER_OFF
    lane_group = (e_lane // MOE_PER_GROUP).astype(F32)
    emask = (e_lane >= 0) & (e_lane < MOE_EXPERTS) & (lane_group == g_top)
    el = jnp.where(emask, logits, -jnp.inf)
    m1 = jnp.max(el, axis=-1, keepdims=True)
    i1 = jnp.min(jnp.where(el == m1, lane, big), axis=-1, keepdims=True)
    el2 = jnp.where(lane == i1, -jnp.inf, el)
    m2 = jnp.max(el2, axis=-1, keepdims=True)
    i2 = jnp.min(jnp.where(el2 == m2, lane, big), axis=-1, keepdims=True)
    e2 = jnp.exp(m2 - m1)
    den = 1.0 + e2
    gates = (jnp.where(lane == i1, (1.0 / den) * p_top, 0.0)
             + jnp.where(lane == i2, (e2 / den) * p_top, 0.0))
    route = jnp.where(lane == ROUTE_GROUP + g_top, 1.0, 0.0)
    for g in range(MOE_GROUPS):
        local = pltpu.roll(gates, LANES - ROUTER_OFF - g * MOE_PER_GROUP, 1)
        route = route + jnp.where((g_top == g) & (lane_i < MOE_PER_GROUP), local, 0.0)
    route_ref[...] = route
    cnt_ref[...] = jnp.broadcast_to(jnp.sum(route, axis=0, keepdims=True), (SUBLANES, LANES)).astype(jnp.int32)


def _output_stage(x, mixes, mods, mod_row, mod_tokens, wo_bf, g2, wr_hi, wr_lo, br, layer):
    tm = OUT_SEQS * S5_SEG
    n = x.shape[0]
    row = lambda w: pl.BlockSpec((tm, w), lambda i: (i, 0))
    const = lambda shape: pl.BlockSpec(shape, lambda i: (0,) * len(shape))
    per_blk = SUBLANES // OUT_SEQS
    return pl.pallas_call(
        _out_kernel,
        grid=(n // tm,),
        in_specs=[row(D_MODEL), row(256), row(256), row(256),
                  pl.BlockSpec((None, S5_SEG, OUT_SEQS * GROUP_WIDTH), lambda i: (i // per_blk, 0, i % per_blk)),
                  _mod_spec(layer, mod_row, mod_tokens // tm, 1),
                  pl.BlockSpec((None, D_MODEL, D_MODEL), lambda i: (layer, 0, 0)), const((1, D_MODEL)),
                  const((D_MODEL, LANES)), const((D_MODEL, LANES)), const((1, LANES))],
        out_specs=[row(D_MODEL), row(D_MODEL), row(LANES),
                   pl.BlockSpec((None, SUBLANES, LANES), lambda i: (i, 0, 0))],
        out_shape=[jax.ShapeDtypeStruct((n, D_MODEL), F32),
                   jax.ShapeDtypeStruct((n, D_MODEL), BF16),
                   jax.ShapeDtypeStruct((n, LANES), F32),
                   jax.ShapeDtypeStruct((n // tm, SUBLANES, LANES), jnp.int32)],
        compiler_params=_cparams("parallel"),
        name="output_stage",
    )(x, *mixes, mods, wo_bf, g2, wr_hi, wr_lo, br)


GROUP_HID = MOE_PER_GROUP * MOE_HIDDEN


MOE_CHUNK = 128


def _moe_kernel(cnt_ref, h2_ref, route_ref, xm_ref, mod_ref, w1_ref, w3_ref, w2_ref, fg_ref, o_ref,
                hs_scr, rs_scr, os_scr, before_scr, *, final, tm):
    i = pl.program_id(0)
    off1 = cnt_ref[i, 0]
    off2 = off1 + cnt_ref[i, 1]
    off3 = off2 + cnt_ref[i, 2]
    starts = (jnp.int32(0), off1, off2, off3)
    ends = (off1, off2, off3, jnp.int32(tm))

    route = route_ref[...]
    r_hi, r_lo = _split(route)
    pick = (lax.broadcasted_iota(jnp.int32, (SUBLANES, LANES), 1)
            == ROUTE_GROUP + lax.broadcasted_iota(jnp.int32, (SUBLANES, LANES), 0))
    gt = lax.dot_general(jnp.where(pick, 1.0, 0.0).astype(BF16), r_hi, (((1,), (1,)), ((), ())),
                         preferred_element_type=F32)
    @pl.when(i == 0)
    def _():
        before_scr[...] = jnp.where(lax.broadcasted_iota(jnp.int32, (tm, tm), 0)
                                    < lax.broadcasted_iota(jnp.int32, (tm, tm), 1), 1.0, 0.0).astype(BF16)

    rank = jnp.dot(gt.astype(BF16), before_scr[...], preferred_element_type=F32)
    gt_i = gt.astype(jnp.int32)
    rank_i = rank.astype(jnp.int32)
    pos = jnp.zeros((1, tm), jnp.int32)
    for g in range(MOE_GROUPS):
        pos = pos + gt_i[g:g + 1, :] * (rank_i[g:g + 1, :] + starts[g])
    perm = jnp.where(lax.broadcasted_iota(jnp.int32, (tm, tm), 0) == pos, 1.0, 0.0).astype(BF16)
    hs_scr[...] = jnp.dot(perm, h2_ref[...], preferred_element_type=F32).astype(BF16)
    rs_scr[...] = (jnp.dot(perm, r_hi, preferred_element_type=F32)
                   + jnp.dot(perm, r_lo, preferred_element_type=F32))

    os_scr[...] = jnp.zeros_like(os_scr)
    for g in range(MOE_GROUPS):
        lo, hi = starts[g], ends[g]
        base = (lo // BF16_ROWS) * BF16_ROWS
        n_chunks = jnp.where(hi > lo, (hi - base + MOE_CHUNK - 1) // MOE_CHUNK, 0)

        def chunk_body(c, carry, g=g, lo=lo, hi=hi, base=base):
            r0 = pl.multiple_of(jnp.minimum(base + c * MOE_CHUNK, tm - MOE_CHUNK), BF16_ROWS)
            rows = pl.ds(r0, MOE_CHUNK)
            x = hs_scr[rows, :]
            gates = rs_scr[rows, :]
            a = jnp.dot(x, w1_ref[g], preferred_element_type=F32)
            b = jnp.dot(x, w3_ref[g], preferred_element_type=F32)
            hid = []
            for e in range(MOE_PER_GROUP):
                sl = slice(e * MOE_HIDDEN, (e + 1) * MOE_HIDDEN)
                hid.append((jax.nn.silu(a[:, sl]) * b[:, sl] * gates[:, e:e + 1]).astype(BF16))
            y = jnp.dot(jnp.concatenate(hid, axis=1), w2_ref[g], preferred_element_type=F32)
            rowid = r0 + lax.broadcasted_iota(jnp.int32, (MOE_CHUNK, 1), 0)
            member = (rowid >= lo) & (rowid < hi)
            os_scr[rows, :] = jnp.where(member, y, os_scr[rows, :])
            return carry

        lax.fori_loop(0, n_chunks, chunk_body, 0)

    o_hi, o_lo = _split(os_scr[...])
    moe = (lax.dot_general(perm, o_hi, (((0,), (0,)), ((), ())), preferred_element_type=F32)
           + lax.dot_general(perm, o_lo, (((0,), (0,)), ((), ())), preferred_element_type=F32))
    out = xm_ref[...] + mod_ref[5:6, :] * moe
    if final:
        out = _rms_rows(out) * fg_ref[...]
    o_ref[...] = out


def _moe_weight_kernel(w1_ref, w3_ref, w2_ref, o1_ref, o3_ref, o2_ref):
    for e in range(MOE_PER_GROUP):
        sl = slice(e * MOE_HIDDEN, (e + 1) * MOE_HIDDEN)
        o1_ref[:, sl] = w1_ref[e].astype(BF16)
        o3_ref[:, sl] = w3_ref[e].astype(BF16)
        o2_ref[sl, :] = w2_ref[e].astype(BF16)


def _moe_weights(w1, w3, w2):
    up = pl.BlockSpec((None, MOE_PER_GROUP, D_MODEL, MOE_HIDDEN), lambda l, g: (l, g, 0, 0))
    down = pl.BlockSpec((None, MOE_PER_GROUP, MOE_HIDDEN, D_MODEL), lambda l, g: (l, g, 0, 0))
    out = pl.BlockSpec((None, None, D_MODEL, GROUP_HID), lambda l, g: (l, g, 0, 0))
    shape = jax.ShapeDtypeStruct((DEPTH, MOE_GROUPS, D_MODEL, GROUP_HID), BF16)
    return pl.pallas_call(
        _moe_weight_kernel,
        grid=(DEPTH, MOE_GROUPS),
        in_specs=[up, up, down],
        out_specs=[out, out, out],
        out_shape=[shape, shape, shape],
        compiler_params=_cparams("parallel", "parallel"),
        name="moe_weights",
    )(w1, w3, w2)


def _moe(h2, route, tile_counts, xm, mods, mod_row, mod_tokens, w1g, w3g, w2g, fg, layer, *, final, tm=512):
    n = h2.shape[0]
    cnt = tile_counts[:, 0, ROUTE_GROUP:ROUTE_GROUP + MOE_GROUPS].reshape(
        n // tm, tm // (OUT_SEQS * S5_SEG), MOE_GROUPS).sum(axis=1)
    row = lambda w: pl.BlockSpec((tm, w), lambda i, c: (i, 0))
    mod_tiles = mod_tokens // tm
    wspec = pl.BlockSpec((None, MOE_GROUPS, D_MODEL, GROUP_HID), lambda i, c: (layer, 0, 0, 0),
                         pipeline_mode=pl.Buffered(1))
    return pl.pallas_call(
        functools.partial(_moe_kernel, final=final, tm=tm),
        grid_spec=pltpu.PrefetchScalarGridSpec(
            num_scalar_prefetch=1,
            grid=(n // tm,),
            in_specs=[row(D_MODEL), row(LANES), row(D_MODEL),
                      pl.BlockSpec((None, None, 6, D_MODEL), lambda i, c: (layer, mod_row + i // mod_tiles, 0, 0)),
                      wspec, wspec, wspec,
                      pl.BlockSpec((1, D_MODEL), lambda i, c: (0, 0))],
            out_specs=row(D_MODEL),
            scratch_shapes=[pltpu.VMEM((tm, D_MODEL), BF16), pltpu.VMEM((tm, LANES), F32),
                            pltpu.VMEM((tm, D_MODEL), F32), pltpu.VMEM((tm, tm), BF16)]),
        out_shape=jax.ShapeDtypeStruct((n, D_MODEL), F32),
        compiler_params=_cparams("arbitrary"),
        name="moe",
    )(cnt, h2, route, xm, mods, w1g, w3g, w2g, fg)


def kernel(x_prompt, x_sample, cache_a_k, cache_a_v, cache_b_k, cache_b_v, state_ret, state_ssm, c, c_ctx, mod_w, mod_b, norm1_g, norm2_g, w_in, a_qn_g, a_kn_g, b_rel_bias, ret_decay, ret_gn_g, s5_lam_re, s5_lam_im, s5_log_dt, s5_b_re, s5_b_im, s5_c_re, s5_c_im, s5_d, s5_glu_w, w_out, moe_gw, moe_gb, moe_ew, moe_eb, moe_w1, moe_w3, moe_w2, final_norm_g):
    n_ctx = BATCH * SEQ
    n_lat = DEC_BATCH * DEC_SEQ
    lat_seg = DEC_SEQ // S5_SEG

    cond = jnp.zeros((SUBLANES, D_MODEL), F32).at[0].set(c_ctx).at[1:1 + DEC_BATCH].set(c)
    mods = _modulation(cond, mod_w, mod_b).reshape(DEPTH, SUBLANES, 6, D_MODEL)

    rope_tabs = _rope_tables()
    s5_a, s5_bm, s5_cre, s5_cim = _s5_prepare(s5_lam_re, s5_lam_im, s5_log_dt, s5_b_re, s5_b_im,
                                              s5_c_re, s5_c_im)
    cak = cache_a_k.reshape(DEC_BATCH, DEPTH, PAST_LEN, A_KV_HEADS * HEAD_DIM)
    cav = cache_a_v.reshape(DEC_BATCH, DEPTH, PAST_LEN, A_KV_HEADS * HEAD_DIM)
    cbk = cache_b_k.reshape(DEC_BATCH, DEPTH, PAST_LEN, B_HEADS * HEAD_DIM)
    cbv = cache_b_v.reshape(DEC_BATCH, DEPTH, PAST_LEN, B_HEADS * HEAD_DIM)

    xc = x_prompt.reshape(n_ctx, D_MODEL)
    xs = x_sample.reshape(n_lat, D_MODEL)
    w1_all, w3_all, w2_all = _moe_weights(moe_w1, moe_w3, moe_w2)
    eye_h = jnp.eye(C_HEADS, dtype=F32)
    s0_bd = (state_ret[:, :, :, :, :, None, :] * eye_h[None, None, None, :, None, :, None]).reshape(
        DEC_BATCH, DEPTH, 2, C_HEADS * HEAD_DIM, C_HEADS * HEAD_DIM)
    ctx_state = ssm_states = None
    h0_zero = jnp.zeros((2, SUBLANES, 2 * S5_SP), F32)
    w_in_bf = w_in.astype(BF16)
    wo_bf = w_out.astype(BF16)
    glu_bf = s5_glu_w.astype(BF16)
    for l in range(DEPTH):
        final = l == DEPTH - 1
        g1 = norm1_g[l].reshape(1, D_MODEL)
        g2 = norm2_g[l].reshape(1, D_MODEL)
        fg = final_norm_g.reshape(1, D_MODEL)
        qn = jnp.tile(a_qn_g[l], A_HEADS).reshape(1, 256)
        kn = jnp.tile(a_kn_g[l], A_KV_HEADS).reshape(1, 128)
        dec = jnp.broadcast_to(ret_decay[l].reshape(2 * C_HEADS, 1), (2 * C_HEADS, LANES))
        gn = ret_gn_g[l].reshape(1, 256)
        dvec = s5_d[l].reshape(1, GROUP_WIDTH)
        wr = jnp.zeros((D_MODEL, LANES), F32).at[:, :MOE_GROUPS].set(moe_gw[l]).at[
            :, ROUTER_OFF:ROUTER_OFF + MOE_EXPERTS].set(moe_ew[l])
        br = jnp.zeros((1, LANES), F32).at[0, :MOE_GROUPS].set(moe_gb[l]).at[
            0, ROUTER_OFF:ROUTER_OFF + MOE_EXPERTS].set(moe_eb[l])
        wr_hi = wr.astype(BF16)
        wr_lo = (wr - wr_hi.astype(F32)).astype(BF16)
        na_bias = _na_bias(b_rel_bias[l])

        oa, ob, oc, du_tm, ctx_state = _ctx_front(xc, mods, g1, w_in_bf, qn, kn, dec, gn, l, ctx_state)
        od_tm, ssm_states = _s5(du_tm, h0_zero, s5_a, s5_bm, s5_cre, s5_cim, dvec, glu_bf, l,
                                nseg=1, fin_layer=l, fin_layers=DEPTH, prev_fin=ssm_states)
        xm, h2, route, counts = _output_stage(xc, (oa, ob, oc, od_tm), mods, 0, n_ctx, wo_bf, g2,
                                              wr_hi, wr_lo, br, l)
        xc = _moe(h2, route, counts, xm, mods, 0, n_ctx, w1_all, w3_all, w2_all, fg, l, final=final)

        zs, cg, du_tm = _project(xs, mods, 1, DEC_SEQ, g1, w_in_bf, qn, kn, rope_tabs, l, seq_len=DEC_SEQ)
        oa = _lat_attention_a(zs, cak, cav, l)
        ob = _lat_attention_b(zs, cbk, cbv, na_bias, l)
        oc = _retention(zs, cg, dec, gn, s0_bd, l, nb=DEC_BATCH, seq_len=DEC_SEQ)
        h0 = state_ssm[:, l].reshape(DEC_BATCH, 2, 2 * S5_SP).transpose(1, 0, 2)
        h0_seg = jnp.zeros((2, DEC_BATCH, lat_seg, 2 * S5_SP), F32)
        h0_seg = h0_seg.at[0, :, 0].set(h0[0]).at[1, :, lat_seg - 1].set(h0[1])
        od_tm, _ = _s5(du_tm, h0_seg.reshape(2, SUBLANES, 2 * S5_SP),
                       s5_a, s5_bm, s5_cre, s5_cim, dvec, glu_bf, l, nseg=lat_seg)
        xm, h2, route, counts = _output_stage(xs, (oa, ob, oc, od_tm), mods, 1, DEC_SEQ, wo_bf, g2,
                                              wr_hi, wr_lo, br, l)
        xs = _moe(h2, route, counts, xm, mods, 1, DEC_SEQ, w1_all, w3_all, w2_all, fg, l, final=final)

    new_ak, new_av, new_bk, new_bv, ret_states = ctx_state
    return (xc.reshape(BATCH, SEQ, D_MODEL), xs.reshape(DEC_BATCH, DEC_SEQ, D_MODEL),
            new_ak.reshape(BATCH, DEPTH, SEQ, A_KV_HEADS, HEAD_DIM),
            new_av.reshape(BATCH, DEPTH, SEQ, A_KV_HEADS, HEAD_DIM),
            new_bk.reshape(BATCH, DEPTH, SEQ, B_HEADS, HEAD_DIM),
            new_bv.reshape(BATCH, DEPTH, SEQ, B_HEADS, HEAD_DIM),
            ret_states,
            ssm_states.reshape(BATCH, DEPTH, 2, 2, S5_GROUPS, S5_STATE))
```

```python
import functools
import math

import numpy as np
import jax
import jax.numpy as jnp
from jax import lax
from jax.experimental import pallas as pl
from jax.experimental.pallas import tpu as pltpu

F32 = jnp.float32
BF16 = jnp.bfloat16

D_MODEL = 1024
BATCH = 32
SEQ = 256
DEPTH = 2
DEC_BATCH = 2
DEC_SEQ = 1024
PAST_LEN = 256
GRID_W = 64
HEAD_DIM = 64
GROUP_WIDTH = 256
A_HEADS = 4
A_KV_HEADS = 2
B_HEADS = 4
NA_ROWS = 8
NA_COLS = 16
C_HEADS = 4
S5_CH = 16
S5_GROUPS = 16
S5_STATE = 64
MOE_GROUPS = 4
MOE_PER_GROUP = 8
MOE_EXPERTS = 32
MOE_HIDDEN = 128
ROPE_THETA = 10000.0
EPS = 1e-6
IN_WIDTH = 2560
Q_SCALE = HEAD_DIM ** -0.5

OFF_AQ, OFF_AK, OFF_AV = 0, 256, 384
OFF_BQ, OFF_BK, OFF_BV = 512, 768, 1024
OFF_CQ, OFF_CK, OFF_CV, OFF_CG = 1280, 1536, 1792, 2048
OFF_DU = 2304

LANES = 128
SUBLANES = 8
BF16_ROWS = 16
S5_SP = S5_GROUPS * S5_STATE
S5_SEG = 256
S5_CHUNK = 256
ROUTER_OFF = 4
NEG_BIG = -1e30
VMEM_LIMIT = 56 * 1024 * 1024


def _cparams(*sem):
    return pltpu.CompilerParams(dimension_semantics=sem, vmem_limit_bytes=VMEM_LIMIT)


def _mod_spec(layer, first_row, tiles_per_row, grid_rank):
    if grid_rank == 1:
        return pl.BlockSpec((None, None, 6, D_MODEL), lambda i: (layer, first_row + i // tiles_per_row, 0, 0))
    return pl.BlockSpec((None, None, 6, D_MODEL), lambda i, g: (layer, first_row + i // tiles_per_row, 0, 0))


def _bdot(a, b):
    return jnp.dot(a.astype(BF16), b.astype(BF16), preferred_element_type=F32)


def _bdot_nt(a, b):
    return lax.dot_general(a.astype(BF16), b.astype(BF16), (((1,), (1,)), ((), ())),
                           preferred_element_type=F32)


def _bdot_tn(a, b):
    return lax.dot_general(a.astype(BF16), b.astype(BF16), (((0,), (0,)), ((), ())),
                           preferred_element_type=F32)


def _split(a):
    hi = a.astype(BF16)
    lo = (a - hi.astype(F32)).astype(BF16)
    return hi, lo


def _dot_hilo_lhs(a, b_bf16):
    hi, lo = _split(a)
    return (jnp.dot(hi, b_bf16, preferred_element_type=F32)
            + jnp.dot(lo, b_bf16, preferred_element_type=F32))


def _rms_rows(x):
    return x * lax.rsqrt(jnp.mean(x * x, axis=-1, keepdims=True) + EPS)


def _mod_kernel(cond_ref, w_ref, b_ref, o_ref):
    o_ref[...] = _bdot(jax.nn.silu(cond_ref[...]), w_ref[...]) + b_ref[...]


def _modulation(cond, mod_w, mod_b):
    tn = 1536
    return pl.pallas_call(
        _mod_kernel,
        grid=(DEPTH, 6 * D_MODEL // tn),
        in_specs=[pl.BlockSpec((SUBLANES, D_MODEL), lambda l, j: (0, 0)),
                  pl.BlockSpec((None, D_MODEL, tn), lambda l, j: (l, 0, j)),
                  pl.BlockSpec((None, 1, tn), lambda l, j: (l, 0, j))],
        out_specs=pl.BlockSpec((None, SUBLANES, tn), lambda l, j: (l, 0, j)),
        out_shape=jax.ShapeDtypeStruct((DEPTH, SUBLANES, 6 * D_MODEL), F32),
        compiler_params=_cparams("arbitrary", "arbitrary"),
        name="modulation",
    )(cond, mod_w, mod_b.reshape(DEPTH, 1, 6 * D_MODEL))


def _group_mean_matrix(w):
    ri = lax.broadcasted_iota(jnp.int32, (w, w), 0) // HEAD_DIM
    ci = lax.broadcasted_iota(jnp.int32, (w, w), 1) // HEAD_DIM
    return jnp.where(ri == ci, 1.0 / HEAD_DIM, 0.0).astype(BF16)


def _head_norm(t, g):
    ms = _dot_hilo_lhs(t * t, _group_mean_matrix(t.shape[1]))
    return t * lax.rsqrt(ms + EPS) * g


def _rope(t, cos, sa, sb):
    return (t * cos + pltpu.roll(t, LANES - 16, 1) * sa + pltpu.roll(t, 16, 1) * sb)


def _store_layer_slot(ref, slot, value):
    for s in range(ref.shape[0]):
        ref[s] = value if s == slot else jnp.zeros_like(value)


def _layer_slot_block(layer, first_call, tail):
    if first_call:
        return (None, DEPTH) + tail, (0,) * (1 + len(tail)), layer
    return (None, 1) + tail, (layer,) + (0,) * len(tail), 0


def _proj_kernel(x_ref, mod_ref, g1_ref, w_ref, qn_ref, kn_ref, cos_ref, sa_ref, sb_ref, z_ref, cg_ref, du_ref):
    h = _rms_rows(x_ref[...]) * g1_ref[...] * (1.0 + mod_ref[1:2, :]) + mod_ref[0:1, :]
    z = jnp.dot(h.astype(BF16), w_ref[...], preferred_element_type=F32)
    aq = _head_norm(z[:, OFF_AQ:OFF_AK], qn_ref[...])
    ak = _head_norm(z[:, OFF_AK:OFF_AV], kn_ref[...])
    for j in range(3):
        t = aq[:, j * LANES:(j + 1) * LANES] if j < 2 else ak
        sl = slice(0, LANES) if j == 2 else slice(j * LANES, (j + 1) * LANES)
        t = _rope(t, cos_ref[:, sl], sa_ref[:, sl], sb_ref[:, sl])
        z_ref[:, j * LANES:(j + 1) * LANES] = t.astype(BF16)
    z_ref[:, OFF_AV:OFF_CK] = z[:, OFF_AV:OFF_CK].astype(BF16)
    z_ref[:, OFF_CK:OFF_CV] = (z[:, OFF_CK:OFF_CV] * Q_SCALE).astype(BF16)
    z_ref[:, OFF_CV:OFF_CG] = z[:, OFF_CV:OFF_CG].astype(BF16)
    cg_ref[...] = z[:, OFF_CG:OFF_DU]
    du_ref[...] = z[:, OFF_DU:]


def _du_spec(grid_rank):
    if grid_rank == 1:
        return pl.BlockSpec((None, S5_SEG, GROUP_WIDTH), lambda i: (i // SUBLANES, 0, i % SUBLANES))
    return pl.BlockSpec((None, S5_SEG, GROUP_WIDTH), lambda i, g: (i // SUBLANES, 0, i % SUBLANES))


def _project(x, mods, mod_row, mod_tokens, g1, w_in_bf, qn, kn, rope_tabs, layer, *, seq_len):
    tm = S5_SEG
    n = x.shape[0]
    tps = seq_len // tm
    return pl.pallas_call(
        _proj_kernel,
        grid=(n // tm,),
        in_specs=[pl.BlockSpec((tm, D_MODEL), lambda i: (i, 0)),
                  _mod_spec(layer, mod_row, mod_tokens // tm, 1),
                  pl.BlockSpec((1, D_MODEL), lambda i: (0, 0)),
                  pl.BlockSpec((None, D_MODEL, IN_WIDTH), lambda i: (layer, 0, 0)),
                  pl.BlockSpec((1, 256), lambda i: (0, 0)),
                  pl.BlockSpec((1, 128), lambda i: (0, 0))]
                 + [pl.BlockSpec((tm, 256), lambda i: (i % tps, 0))] * 3,
        out_specs=[pl.BlockSpec((tm, OFF_CG), lambda i: (i, 0)),
                   pl.BlockSpec((tm, GROUP_WIDTH), lambda i: (i, 0)), _du_spec(1)],
        out_shape=[jax.ShapeDtypeStruct((n, OFF_CG), BF16),
                   jax.ShapeDtypeStruct((n, GROUP_WIDTH), F32),
                   jax.ShapeDtypeStruct((n // (tm * SUBLANES), S5_SEG, SUBLANES * GROUP_WIDTH), F32)],
        compiler_params=_cparams("parallel"),
        name="project",
    )(x, mods, g1, w_in_bf, qn, kn, *rope_tabs)


def _rope_tables():
    t = jnp.arange(DEC_SEQ)
    row = (t // GRID_W).astype(F32)
    col = (t % GRID_W).astype(F32)
    nf = HEAD_DIM // 4
    inv = ROPE_THETA ** (-jnp.arange(nf, dtype=F32) / nf)
    ang_r = row[:, None] * inv[None, :]
    ang_c = col[:, None] * inv[None, :]
    zeros = jnp.zeros_like(ang_r)
    cos = jnp.concatenate([jnp.cos(ang_r), jnp.cos(ang_r), jnp.cos(ang_c), jnp.cos(ang_c)], axis=-1)
    sa = jnp.concatenate([-jnp.sin(ang_r), zeros, -jnp.sin(ang_c), zeros], axis=-1)
    sb = jnp.concatenate([zeros, jnp.sin(ang_r), zeros, jnp.sin(ang_c)], axis=-1)
    return tuple(jnp.tile(a, (1, 4)) for a in (cos, sa, sb))


N_HEADS = 4


def _lane_head(width):
    return lax.broadcasted_iota(jnp.int32, (1, width), 1) // HEAD_DIM


def _stack_heads(q):
    head = _lane_head(q.shape[1])
    return jnp.concatenate([jnp.where(head == h, q, 0.0) for h in range(N_HEADS)], axis=0).astype(BF16)


def _stack_heads_gqa(q):
    lo = lax.broadcasted_iota(jnp.int32, (1, LANES), 1) < HEAD_DIM
    q = q.astype(F32)
    q01, q23 = q[:, :LANES], q[:, LANES:]
    blocks = [jnp.where(lo, q01, 0.0), jnp.where(lo, pltpu.roll(q01, HEAD_DIM, 1), 0.0),
              jnp.where(lo, 0.0, pltpu.roll(q23, HEAD_DIM, 1)), jnp.where(lo, 0.0, q23)]
    return jnp.concatenate(blocks, axis=0).astype(BF16)


def _spread_kv_gqa(v):
    lo = lax.broadcasted_iota(jnp.int32, (1, LANES), 1) < HEAD_DIM
    v = v.astype(F32)
    vr = pltpu.roll(v, HEAD_DIM, 1)
    return jnp.concatenate([jnp.where(lo, v, vr), jnp.where(lo, vr, v)], axis=1)


def _mha(qs, blocks, tq):
    scores = []
    for k, _, bias in blocks:
        s = _bdot_nt(qs, k)
        scores.append(s if bias is None else s + bias)
    m = functools.reduce(jnp.maximum, [jnp.max(s, axis=-1, keepdims=True) for s in scores])
    es = [jnp.exp(s - m) for s in scores]
    denom = functools.reduce(jnp.add, [jnp.sum(e, axis=-1, keepdims=True) for e in es])
    ps = [e.astype(BF16) for e in es]
    head = _lane_head(N_HEADS * HEAD_DIM)
    vals = [v.astype(BF16) for _, v, _ in blocks]
    o = None
    dall = None
    for h in range(N_HEADS):
        rows = slice(h * tq, (h + 1) * tq)
        for p, v in zip(ps, vals):
            t = jnp.dot(p[rows], jnp.where(head == h, v, jnp.zeros_like(v)), preferred_element_type=F32)
            o = t if o is None else o + t
        d = jnp.where(head == h, denom[rows], 0.0)
        dall = d if dall is None else dall + d
    return (o / dall).astype(BF16)


def _lat_attn_a_kernel(q_ref, kn_ref, vn_ref, kc_ref, vc_ref, o_ref):
    o_ref[...] = _mha(_stack_heads_gqa(q_ref[...] * Q_SCALE),
                      [(kc_ref[...], _spread_kv_gqa(vc_ref[...]), None),
                       (kn_ref[...], _spread_kv_gqa(vn_ref[...]), None)], q_ref.shape[0])


def _lat_attention_a(z, cache_k, cache_v, layer, tq=256):
    nq = DEC_SEQ // tq
    cache_spec = pl.BlockSpec((None, None, PAST_LEN, 128), lambda b, j: (b, layer, 0, 0))
    return pl.pallas_call(
        _lat_attn_a_kernel,
        grid=(DEC_BATCH, nq),
        in_specs=[pl.BlockSpec((tq, 256), lambda b, j: (b * nq + j, OFF_AQ // 256)),
                  pl.BlockSpec((DEC_SEQ, 128), lambda b, j: (b, OFF_AK // 128)),
                  pl.BlockSpec((DEC_SEQ, 128), lambda b, j: (b, OFF_AV // 128)),
                  cache_spec, cache_spec],
        out_specs=pl.BlockSpec((tq, 256), lambda b, j: (b * nq + j, 0)),
        out_shape=jax.ShapeDtypeStruct((DEC_BATCH * DEC_SEQ, 256), BF16),
        compiler_params=_cparams("parallel", "parallel"),
        name="lat_attention_a",
    )(z, z, z, cache_k, cache_v)


NA_KEYS = NA_ROWS * GRID_W


NA_PAIRS = 2 * NA_ROWS - 2


NA_STEP_ROWS = 2


def _na_kernel(q_ref, k_ref, v_ref, kc_ref, vc_ref, bias_ref, o_ref):
    rows = DEC_SEQ // GRID_W
    outs = []
    for rr in range(NA_STEP_ROWS):
        r = pl.program_id(1) * NA_STEP_ROWS + rr
        row_start = jnp.clip(r - NA_ROWS // 2, 0, rows - NA_ROWS)
        start = pl.multiple_of(row_start * GRID_W, GRID_W)
        rel0 = row_start - r + NA_ROWS - 1
        kl = k_ref[pl.ds(start, NA_KEYS), :]
        vl = v_ref[pl.ds(start, NA_KEYS), :]
        bias = jnp.concatenate(
            [jnp.concatenate([bias_ref[h, rel0 + 2 * jp] for jp in range(NA_ROWS // 2)], axis=1)
             for h in range(B_HEADS)], axis=0)
        qrows = slice(rr * GRID_W, (rr + 1) * GRID_W)
        outs.append(_mha(_stack_heads(q_ref[qrows, :] * Q_SCALE),
                         [(kl, vl, bias), (kc_ref[...], vc_ref[...], None)], GRID_W))
    o_ref[...] = jnp.concatenate(outs, axis=0)


def _na_bias(rel_bias):
    nrel = 2 * NA_COLS - 1
    period = 2 * GRID_W
    b = rel_bias.astype(F32)
    ext = jnp.concatenate([b[..., NA_COLS - 1:],
                           jnp.zeros(b.shape[:-1] + (period - nrel,), F32),
                           b[..., :NA_COLS - 1]], axis=-1)
    flat = jnp.tile(ext, (1, 1, GRID_W))[..., :GRID_W * (period - 1)]
    toe = flat.reshape(b.shape[:-1] + (GRID_W, period - 1))[..., :GRID_W]
    col_start = np.clip(np.arange(GRID_W) - NA_COLS // 2, 0, GRID_W - NA_COLS)
    kc = np.arange(GRID_W)
    inside = (kc[None, :] >= col_start[:, None]) & (kc[None, :] < col_start[:, None] + NA_COLS)
    toe = jnp.where(jnp.asarray(inside), toe, NEG_BIG)
    return jnp.concatenate([toe[:, :-1], toe[:, 1:]], axis=-1)


def _lat_attention_b(z, cache_k, cache_v, bias, layer):
    rows = DEC_SEQ // GRID_W // NA_STEP_ROWS
    tq = NA_STEP_ROWS * GRID_W
    cache_spec = pl.BlockSpec((None, None, PAST_LEN, 256), lambda b, r: (b, layer, 0, 0))
    return pl.pallas_call(
        _na_kernel,
        grid=(DEC_BATCH, rows),
        in_specs=[pl.BlockSpec((tq, 256), lambda b, r: (b * rows + r, OFF_BQ // 256)),
                  pl.BlockSpec((DEC_SEQ, 256), lambda b, r: (b, OFF_BK // 256)),
                  pl.BlockSpec((DEC_SEQ, 256), lambda b, r: (b, OFF_BV // 256)),
                  cache_spec, cache_spec,
                  pl.BlockSpec((B_HEADS, NA_PAIRS, GRID_W, 2 * GRID_W), lambda b, r: (0, 0, 0, 0))],
        out_specs=pl.BlockSpec((tq, 256), lambda b, r: (b * rows + r, 0)),
        out_shape=jax.ShapeDtypeStruct((DEC_BATCH * DEC_SEQ, 256), BF16),
        compiler_params=_cparams("parallel", "parallel"),
        name="lat_attention_b",
    )(z, z, z, cache_k, cache_v, bias)


def _retention_core(q, k, v, g, dec_ref, gn_ref, dec_scr, *, seq_len, i0, decay_fill, s0_ref=None,
                    want_state=False):
    tq = q.shape[0]
    head = _lane_head(C_HEADS * HEAD_DIM)
    lg = jax.nn.log_sigmoid(dec_ref[...])

    def per_lane(row0):
        out = jnp.zeros((1, C_HEADS * HEAD_DIM), F32)
        for h in range(C_HEADS):
            out = jnp.where(head == h, lg[row0 + h:row0 + h + 1, 0:1], out)
        return out

    lgf_l, lgb_l = per_lane(0), per_lane(C_HEADS)
    qi = (i0 + lax.broadcasted_iota(jnp.int32, (tq, 1), 0)).astype(F32)

    def fill_decay():
        kj = lax.broadcasted_iota(jnp.int32, (1, seq_len), 1).astype(F32)
        diff = qi - kj
        for h in range(C_HEADS):
            lgf = lg[h:h + 1, 0:1]
            lgb = lg[C_HEADS + h:C_HEADS + h + 1, 0:1]
            dec_scr[h * tq:(h + 1) * tq, :] = (
                jnp.where(diff >= 0, jnp.exp(lgf * jnp.maximum(diff, 0.0)), 0.0)
                + jnp.where(diff <= 0, jnp.exp(lgb * jnp.maximum(-diff, 0.0)), 0.0))

    if decay_fill == "first_step":
        pl.when(pl.program_id(0) == 0)(fill_decay)
    elif decay_fill == "every_step":
        fill_decay()
    else:
        assert decay_fill == "filled"

    v = v.astype(BF16)
    sc = (_bdot_nt(_stack_heads(q), k) * dec_scr[...]).astype(BF16)
    o = None
    for h in range(C_HEADS):
        t = jnp.dot(sc[h * tq:(h + 1) * tq], jnp.where(head == h, v, jnp.zeros_like(v)),
                    preferred_element_type=F32)
        o = t if o is None else o + t
    if s0_ref is not None:
        o = (o + _bdot(q, s0_ref[0]) * jnp.exp(lgf_l * (qi + 1.0))
             + _bdot(q, s0_ref[1]) * jnp.exp(lgb_l * (seq_len - qi)))
    gm = _group_mean_matrix(C_HEADS * HEAD_DIM)
    dlt = o - _dot_hilo_lhs(o, gm)
    var = _dot_hilo_lhs(dlt * dlt, gm)
    out = (dlt * lax.rsqrt(var + EPS) * gn_ref[...] * jax.nn.silu(g)).astype(BF16)
    if not want_state:
        return out, None
    kpos = lax.broadcasted_iota(jnp.int32, (seq_len, 1), 0).astype(F32)
    sf = _bdot_tn(k * jnp.exp(lgf_l * (seq_len - 1.0 - kpos)), v)
    sb = _bdot_tn(k * jnp.exp(lgb_l * kpos), v)
    return out, (sf, sb)


def _store_retention_state(st_ref, slot, state):
    for s in range(st_ref.shape[0]):
        for d in range(2):
            for h in range(C_HEADS):
                sl = slice(h * HEAD_DIM, (h + 1) * HEAD_DIM)
                st_ref[s, d, h] = state[d][sl, sl] if s == slot else jnp.zeros((HEAD_DIM, HEAD_DIM), F32)


def _retention_kernel(q_ref, g_ref, k_ref, v_ref, dec_ref, gn_ref, s0_ref, o_ref, dec_scr, *, seq_len, tq):
    o_ref[...], _ = _retention_core(q_ref[...], k_ref[...], v_ref[...], g_ref[...], dec_ref, gn_ref, dec_scr,
                                    seq_len=seq_len, i0=pl.program_id(1) * tq, decay_fill="every_step",
                                    s0_ref=s0_ref)


def _retention(z, cg, dec, gn, s0, layer, *, nb, seq_len, tq=256):
    nq = seq_len // tq
    return pl.pallas_call(
        functools.partial(_retention_kernel, seq_len=seq_len, tq=tq),
        grid=(nb, nq),
        in_specs=[pl.BlockSpec((tq, 256), lambda b, j: (b * nq + j, OFF_CQ // 256)),
                  pl.BlockSpec((tq, 256), lambda b, j: (b * nq + j, 0)),
                  pl.BlockSpec((seq_len, 256), lambda b, j: (b, OFF_CK // 256)),
                  pl.BlockSpec((seq_len, 256), lambda b, j: (b, OFF_CV // 256)),
                  pl.BlockSpec((SUBLANES, LANES), lambda b, j: (0, 0)),
                  pl.BlockSpec((1, 256), lambda b, j: (0, 0)),
                  pl.BlockSpec((None, None, 2, 256, 256), lambda b, j: (b, layer, 0, 0, 0))],
        out_specs=pl.BlockSpec((tq, 256), lambda b, j: (b * nq + j, 0)),
        out_shape=jax.ShapeDtypeStruct((nb * seq_len, 256), BF16),
        scratch_shapes=[pltpu.VMEM((C_HEADS * tq, seq_len), F32)],
        compiler_params=_cparams("parallel", "parallel"),
        name="retention",
    )(z, cg, z, z, dec, gn, s0)


CTX_SEQS = 2


def _ctx_front_kernel(x_ref, mod_ref, g1_ref, w_ref, qn_ref, kn_ref, dec_ref, gn_ref, *rest, n_alias, slot):
    (oa_ref, ob_ref, oc_ref, du_ref, ak_ref, av_ref, bk_ref, bv_ref, st_ref, dec_scr) = rest[n_alias:]
    tq = x_ref.shape[0] // CTX_SEQS
    h = _rms_rows(x_ref[...]) * g1_ref[...] * (1.0 + mod_ref[1:2, :]) + mod_ref[0:1, :]
    zz = jnp.dot(h.astype(BF16), w_ref[...], preferred_element_type=F32)
    for s in range(CTX_SEQS):
        rows = slice(s * tq, (s + 1) * tq)
        z = zz[rows, :]
        aq = _head_norm(z[:, OFF_AQ:OFF_AK], qn_ref[...])
        ak = _head_norm(z[:, OFF_AK:OFF_AV], kn_ref[...])
        av, bq, bk, bv = (z[:, OFF_AV:OFF_BQ], z[:, OFF_BQ:OFF_BK], z[:, OFF_BK:OFF_BV], z[:, OFF_BV:OFF_CQ])
        oa_ref[rows, :] = _mha(_stack_heads_gqa(aq * Q_SCALE), [(ak, _spread_kv_gqa(av), None)], tq)
        ob_ref[rows, :] = _mha(_stack_heads(bq * Q_SCALE), [(bk, bv, None)], tq)
        oc_ref[rows, :], state = _retention_core(
            z[:, OFF_CQ:OFF_CK], z[:, OFF_CK:OFF_CV] * Q_SCALE, z[:, OFF_CV:OFF_CG], z[:, OFF_CG:OFF_DU],
            dec_ref, gn_ref, dec_scr, seq_len=tq, i0=0, decay_fill="first_step" if s == 0 else "filled",
            want_state=True)
        du_ref[:, s * GROUP_WIDTH:(s + 1) * GROUP_WIDTH] = z[:, OFF_DU:]
        _store_layer_slot(ak_ref.at[s], slot, ak)
        _store_layer_slot(av_ref.at[s], slot, av)
        _store_layer_slot(bk_ref.at[s], slot, bk)
        _store_layer_slot(bv_ref.at[s], slot, bv)
        _store_retention_state(st_ref.at[s], slot, state)


def _ctx_front(x, mods, g1, w_in_bf, qn, kn, dec, gn, layer, prev):
    assert SEQ == S5_SEG
    tm = CTX_SEQS * SEQ
    n = x.shape[0]
    nb = n // SEQ
    steps = n // tm
    per_blk = SUBLANES // CTX_SEQS
    const = lambda *shape: pl.BlockSpec(shape, lambda i: (0,) * len(shape))
    row = lambda w: pl.BlockSpec((tm, w), lambda i: (i, 0))
    in_specs = [row(D_MODEL), _mod_spec(layer, 0, steps, 1), const(1, D_MODEL),
                pl.BlockSpec((None, D_MODEL, IN_WIDTH), lambda i: (layer, 0, 0)),
                const(1, 256), const(1, 128), const(SUBLANES, LANES), const(1, 256)]
    args = [x, mods, g1, w_in_bf, qn, kn, dec, gn]
    out_specs = [row(256), row(256), row(256),
                 pl.BlockSpec((None, S5_SEG, CTX_SEQS * GROUP_WIDTH), lambda i: (i // per_blk, 0, i % per_blk))]
    out_shape = [jax.ShapeDtypeStruct((n, 256), BF16)] * 3 + [
        jax.ShapeDtypeStruct((nb // SUBLANES, S5_SEG, SUBLANES * GROUP_WIDTH), F32)]
    first = prev is None
    slot = 0
    for tail in ((SEQ, 128), (SEQ, 128), (SEQ, 256), (SEQ, 256), (2, C_HEADS, HEAD_DIM, HEAD_DIM)):
        blk, idx, slot = _layer_slot_block(layer, first, tail)
        out_specs.append(pl.BlockSpec((CTX_SEQS,) + blk[1:], lambda i, idx=idx: (i,) + idx))
        out_shape.append(jax.ShapeDtypeStruct((nb, DEPTH) + tail, F32))
    aliases = {}
    if not first:
        for k, arr in enumerate(prev):
            aliases[len(args)] = 4 + k
            in_specs.append(pl.BlockSpec(memory_space=pl.ANY))
            args.append(arr)
    outs = pl.pallas_call(
        functools.partial(_ctx_front_kernel, n_alias=len(aliases), slot=slot),
        grid=(steps,),
        in_specs=in_specs,
        out_specs=out_specs,
        out_shape=out_shape,
        scratch_shapes=[pltpu.VMEM((C_HEADS * SEQ, SEQ), F32)],
        input_output_aliases=aliases,
        compiler_params=_cparams("arbitrary"),
        name="ctx_front",
    )(*args)
    return outs[0], outs[1], outs[2], outs[3], tuple(outs[4:])


def _s5_prep_kernel(lre_ref, lim_ref, ldt_ref, bre_ref, bim_ref, cre_ref, cim_ref,
                    a_ref, bm_ref, cro_ref, cio_ref, bm_scr, cr_scr, ci_scr):
    lre = lre_ref[...]
    lim = lim_ref[...]
    dt = jnp.exp(ldt_ref[...])
    mag = jnp.exp(lre * dt)
    a_re = mag * jnp.cos(lim * dt)
    a_im = mag * jnp.sin(lim * dt)
    den = lre * lre + lim * lim
    r_re = ((a_re - 1.0) * lre + a_im * lim) / den
    r_im = (a_im * lre - (a_re - 1.0) * lim) / den
    bm_scr[...] = jnp.zeros_like(bm_scr)
    cr_scr[...] = jnp.zeros_like(cr_scr)
    ci_scr[...] = jnp.zeros_like(ci_scr)
    for g in range(S5_GROUPS):
        rows = slice(g * S5_CH, (g + 1) * S5_CH)
        cols = slice(g * S5_STATE, (g + 1) * S5_STATE)
        a_ref[0:1, cols] = a_re[g:g + 1, :]
        a_ref[1:2, cols] = a_im[g:g + 1, :]
        rr, ri = r_re[g:g + 1, :], r_im[g:g + 1, :]
        br, bi = bre_ref[g], bim_ref[g]
        bm_scr[rows, cols] = rr * br - ri * bi
        bm_scr[rows, S5_SP + g * S5_STATE:S5_SP + (g + 1) * S5_STATE] = rr * bi + ri * br
        cr_scr[cols, rows] = cre_ref[g]
        ci_scr[cols, rows] = cim_ref[g]
    bm_ref[...] = bm_scr[...].astype(BF16)
    cro_ref[...] = cr_scr[...].astype(BF16)
    cio_ref[...] = ci_scr[...].astype(BF16)


def _s5_prepare(lam_re, lam_im, log_dt, b_re, b_im, c_re, c_im):
    gp = (S5_GROUPS, S5_STATE)
    ldt = jnp.broadcast_to(log_dt[..., None], (DEPTH, 2) + gp)
    bt = [jnp.swapaxes(t, -1, -2) for t in (b_re, b_im)]
    ct = [jnp.swapaxes(t, -1, -2) for t in (c_re, c_im)]

    def spec(*tail):
        return pl.BlockSpec((None, None) + tail, lambda l, d: (l, d) + (0,) * len(tail))

    return pl.pallas_call(
        _s5_prep_kernel,
        grid=(DEPTH, 2),
        in_specs=[spec(*gp)] * 3 + [spec(S5_GROUPS, S5_CH, S5_STATE)] * 2 + [spec(S5_GROUPS, S5_STATE, S5_CH)] * 2,
        out_specs=[spec(2, S5_SP), spec(GROUP_WIDTH, 2 * S5_SP), spec(S5_SP, GROUP_WIDTH), spec(S5_SP, GROUP_WIDTH)],
        out_shape=[jax.ShapeDtypeStruct((DEPTH, 2, 2, S5_SP), F32),
                   jax.ShapeDtypeStruct((DEPTH, 2, GROUP_WIDTH, 2 * S5_SP), BF16),
                   jax.ShapeDtypeStruct((DEPTH, 2, S5_SP, GROUP_WIDTH), BF16),
                   jax.ShapeDtypeStruct((DEPTH, 2, S5_SP, GROUP_WIDTH), BF16)],
        scratch_shapes=[pltpu.VMEM((GROUP_WIDTH, 2 * S5_SP), F32), pltpu.VMEM((S5_SP, GROUP_WIDTH), F32),
                        pltpu.VMEM((S5_SP, GROUP_WIDTH), F32)],
        compiler_params=_cparams("parallel", "parallel"),
        name="s5_prepare",
    )(lam_re, lam_im, ldt, bt[0], bt[1], ct[0], ct[1])


def _cmul(ar, ai, br, bi):
    return ar * br - ai * bi, ar * bi + ai * br


def _s5_kernel(u_ref, h0_ref, a_ref, bm_ref, cre_ref, cim_ref, dvec_ref, glu_ref, *rest, nseg, slot):
    od_ref, fin_ref, x_scr, s_scr, y_scr = rest[-5:]
    steps = S5_SEG
    rows = steps * SUBLANES
    chunk = S5_CHUNK
    chunk_steps = chunk // SUBLANES
    nchunk = rows // chunk
    seg = lax.broadcasted_iota(jnp.int32, (SUBLANES, S5_SP), 0) % nseg

    for d in range(2):
        ar = jnp.broadcast_to(a_ref[d, 0:1, :], (SUBLANES, S5_SP))
        ai = jnp.broadcast_to(a_ref[d, 1:2, :], (SUBLANES, S5_SP))

        def row0(k):
            c = k if d == 0 else nchunk - 1 - k
            return c * chunk if isinstance(c, int) else pl.multiple_of(c * chunk, chunk)

        def input_part(k, buf):
            x_scr[buf] = jnp.dot(u_ref[pl.ds(row0(k), chunk), :].astype(BF16), bm_ref[d],
                                 preferred_element_type=F32)

        def scan_part(buf, carry, store):
            sr, si = carry
            for t in range(chunk_steps):
                r = (t if d == 0 else chunk_steps - 1 - t) * SUBLANES
                pr, pi = _cmul(ar, ai, sr, si)
                sr = pr + x_scr[buf, r:r + SUBLANES, 0:S5_SP]
                si = pi + x_scr[buf, r:r + SUBLANES, S5_SP:]
                if store:
                    s_scr[buf, r:r + SUBLANES, 0:S5_SP] = sr
                    s_scr[buf, r:r + SUBLANES, S5_SP:] = si
            return sr, si

        def output_part(k, buf):
            y = _bdot(s_scr[buf, :, 0:S5_SP], cre_ref[d]) - _bdot(s_scr[buf, :, S5_SP:], cim_ref[d])
            rows_k = pl.ds(row0(k), chunk)
            if d == 0:
                y_scr[rows_k, :] = y
            else:
                zz = jax.nn.gelu(y_scr[rows_k, :] + y + dvec_ref[...] * u_ref[rows_k, :])
                od_ref[rows_k, :] = (zz * jax.nn.sigmoid(_bdot(zz, glu_ref[...]))).astype(BF16)

        def half(k, buf, carry, store, nxt=True, prev=True):
            if nxt:
                input_part(k + 1, 1 - buf)
            carry = scan_part(buf, carry, store)
            if store and prev:
                output_part(k - 1, 1 - buf)
            return carry

        def run_pass(carry, store):
            input_part(0, 0)
            carry = half(0, 0, carry, store, prev=False)
            carry = half(1, 1, carry, store)

            def pair(j, c):
                c = half(2 * j, 0, c, store)
                return half(2 * j + 1, 1, c, store)
            carry = lax.fori_loop(1, nchunk // 2 - 1, pair, carry)
            carry = half(nchunk - 2, 0, carry, store)
            carry = half(nchunk - 1, 1, carry, store, nxt=False)
            if store:
                output_part(nchunk - 1, 1)
            return carry

        init = (h0_ref[d, :, 0:S5_SP], h0_ref[d, :, S5_SP:])
        if nseg > 1:
            zero = jnp.zeros((SUBLANES, S5_SP), F32)
            fr, fi = run_pass((zero, zero), store=False)
            pr, pi = ar, ai
            for _ in range(int(math.log2(steps))):
                pr, pi = _cmul(pr, pi, pr, pi)
            cr, ci = init
            shift = 1 if d == 0 else SUBLANES - 1
            order = range(1, nseg) if d == 0 else range(nseg - 2, -1, -1)
            for s in order:
                ncr, nci = pltpu.roll(cr, shift, 0), pltpu.roll(ci, shift, 0)
                nfr, nfi = pltpu.roll(fr, shift, 0), pltpu.roll(fi, shift, 0)
                qr, qi = _cmul(pr, pi, ncr, nci)
                cr = jnp.where(seg == s, qr + nfr, cr)
                ci = jnp.where(seg == s, qi + nfi, ci)
            init = (cr, ci)
        sr, si = run_pass(init, store=True)
        for s in range(fin_ref.shape[1] // (4 * S5_SP)):
            base = (4 * s + 2 * d) * S5_SP
            fin_ref[:, base:base + S5_SP] = sr if s == slot else jnp.zeros_like(sr)
            fin_ref[:, base + S5_SP:base + 2 * S5_SP] = si if s == slot else jnp.zeros_like(si)


def _s5(du_tm, h0, a, bmat, cre, cim, dvec, glu_bf, layer, *, nseg, fin_layer=0, fin_layers=1,
        prev_fin=None):
    nblk = du_tm.shape[0]
    rows = S5_SEG * SUBLANES
    fin_w = 4 * S5_SP
    in_specs = [pl.BlockSpec((None, rows, GROUP_WIDTH), lambda i: (i, 0, 0)),
                pl.BlockSpec((2, SUBLANES, 2 * S5_SP), lambda i: (0, 0, 0)),
                pl.BlockSpec((None, 2, 2, S5_SP), lambda i: (layer, 0, 0, 0)),
                pl.BlockSpec((None, 2, GROUP_WIDTH, 2 * S5_SP), lambda i: (layer, 0, 0, 0)),
                pl.BlockSpec((None, 2, S5_SP, GROUP_WIDTH), lambda i: (layer, 0, 0, 0)),
                pl.BlockSpec((None, 2, S5_SP, GROUP_WIDTH), lambda i: (layer, 0, 0, 0)),
                pl.BlockSpec((1, GROUP_WIDTH), lambda i: (0, 0)),
                pl.BlockSpec((None, GROUP_WIDTH, GROUP_WIDTH), lambda i: (layer, 0, 0))]
    args = [du_tm.reshape(nblk, rows, GROUP_WIDTH), h0, a, bmat, cre, cim, dvec, glu_bf]
    aliases = {}
    if prev_fin is not None:
        aliases[len(args)] = 1
        in_specs.append(pl.BlockSpec(memory_space=pl.ANY))
        args.append(prev_fin)
        fin_spec, slot = pl.BlockSpec((SUBLANES, fin_w), lambda i: (i, fin_layer)), 0
    else:
        fin_spec, slot = pl.BlockSpec((SUBLANES, fin_layers * fin_w), lambda i: (i, 0)), fin_layer
    od, fin = pl.pallas_call(
        functools.partial(_s5_kernel, nseg=nseg, slot=slot),
        grid=(nblk,),
        in_specs=in_specs,
        out_specs=[pl.BlockSpec((None, rows, GROUP_WIDTH), lambda i: (i, 0, 0)), fin_spec],
        out_shape=[jax.ShapeDtypeStruct((nblk, rows, GROUP_WIDTH), BF16),
                   jax.ShapeDtypeStruct((nblk * SUBLANES, fin_layers * fin_w), F32)],
        scratch_shapes=[pltpu.VMEM((2, S5_CHUNK, 2 * S5_SP), F32), pltpu.VMEM((2, S5_CHUNK, 2 * S5_SP), F32),
                        pltpu.VMEM((rows, GROUP_WIDTH), F32)],
        input_output_aliases=aliases,
        compiler_params=_cparams("parallel"),
        name="s5",
    )(*args)
    return od.reshape(nblk, S5_SEG, SUBLANES * GROUP_WIDTH), fin


ROUTE_GROUP = MOE_PER_GROUP
OUT_SEQS = 2


def _out_kernel(x_ref, oa_ref, ob_ref, oc_ref, od_ref, mod_ref, wo_ref, g2_ref, wrh_ref, wrl_ref, br_ref,
                xm_ref, h2_ref, route_ref, cnt_ref):
    od = jnp.concatenate([od_ref[:, s * GROUP_WIDTH:(s + 1) * GROUP_WIDTH] for s in range(OUT_SEQS)], axis=0)
    mix = functools.reduce(jnp.add, [
        _bdot(o, wo_ref[i * GROUP_WIDTH:(i + 1) * GROUP_WIDTH, :])
        for i, o in enumerate((oa_ref[...], ob_ref[...], oc_ref[...], od))])
    xm = x_ref[...] + mod_ref[2:3, :] * mix
    xm_ref[...] = xm
    h2 = _rms_rows(xm) * g2_ref[...] * (1.0 + mod_ref[4:5, :]) + mod_ref[3:4, :]
    h2_ref[...] = h2.astype(BF16)

    h_hi, h_lo = _split(h2)
    logits = (jnp.dot(h_hi, wrh_ref[...], preferred_element_type=F32)
              + jnp.dot(h_hi, wrl_ref[...], preferred_element_type=F32)
              + jnp.dot(h_lo, wrh_ref[...], preferred_element_type=F32)) + br_ref[...]
    lane_i = lax.broadcasted_iota(jnp.int32, logits.shape, 1)
    lane = lane_i.astype(F32)
    big = jnp.float32(2 ** 30)
    gmask = lane_i < MOE_GROUPS
    gl = jnp.where(gmask, logits, -jnp.inf)
    gmax = jnp.max(gl, axis=-1, keepdims=True)
    p_top = 1.0 / jnp.sum(jnp.exp(gl - gmax), axis=-1, keepdims=True)
    g_top = jnp.min(jnp.where(gl == gmax, lane, big), axis=-1, keepdims=True)
    e_lane = lane_i - ROUTER_OFF
    lane_group = (e_lane // MOE_PER_GROUP).astype(F32)
    emask = (e_lane >= 0) & (e_lane < MOE_EXPERTS) & (lane_group == g_top)
    el = jnp.where(emask, logits, -jnp.inf)
    m1 = jnp.max(el, axis=-1, keepdims=True)
    i1 = jnp.min(jnp.where(el == m1, lane, big), axis=-1, keepdims=True)
    el2 = jnp.where(lane == i1, -jnp.inf, el)
    m2 = jnp.max(el2, axis=-1, keepdims=True)
    i2 = jnp.min(jnp.where(el2 == m2, lane, big), axis=-1, keepdims=True)
    e2 = jnp.exp(m2 - m1)
    den = 1.0 + e2
    gates = (jnp.where(lane == i1, (1.0 / den) * p_top, 0.0)
             + jnp.where(lane == i2, (e2 / den) * p_top, 0.0))
    route = jnp.where(lane == ROUTE_GROUP + g_top, 1.0, 0.0)
    for g in range(MOE_GROUPS):
        local = pltpu.roll(gates, LANES - ROUTER_OFF - g * MOE_PER_GROUP, 1)
        route = route + jnp.where((g_top == g) & (lane_i < MOE_PER_GROUP), local, 0.0)
    route_ref[...] = route
    cnt_ref[...] = jnp.broadcast_to(jnp.sum(route, axis=0, keepdims=True), (SUBLANES, LANES)).astype(jnp.int32)


def _output_stage(x, mixes, mods, mod_row, mod_tokens, wo_bf, g2, wr_hi, wr_lo, br, layer):
    tm = OUT_SEQS * S5_SEG
    n = x.shape[0]
    row = lambda w: pl.BlockSpec((tm, w), lambda i: (i, 0))
    const = lambda shape: pl.BlockSpec(shape, lambda i: (0,) * len(shape))
    per_blk = SUBLANES // OUT_SEQS
    return pl.pallas_call(
        _out_kernel,
        grid=(n // tm,),
        in_specs=[row(D_MODEL), row(256), row(256), row(256),
                  pl.BlockSpec((None, S5_SEG, OUT_SEQS * GROUP_WIDTH), lambda i: (i // per_blk, 0, i % per_blk)),
                  _mod_spec(layer, mod_row, mod_tokens // tm, 1),
                  pl.BlockSpec((None, D_MODEL, D_MODEL), lambda i: (layer, 0, 0)), const((1, D_MODEL)),
                  const((D_MODEL, LANES)), const((D_MODEL, LANES)), const((1, LANES))],
        out_specs=[row(D_MODEL), row(D_MODEL), row(LANES),
                   pl.BlockSpec((None, SUBLANES, LANES), lambda i: (i, 0, 0))],
        out_shape=[jax.ShapeDtypeStruct((n, D_MODEL), F32),
                   jax.ShapeDtypeStruct((n, D_MODEL), BF16),
                   jax.ShapeDtypeStruct((n, LANES), F32),
                   jax.ShapeDtypeStruct((n // tm, SUBLANES, LANES), jnp.int32)],
        compiler_params=_cparams("parallel"),
        name="output_stage",
    )(x, *mixes, mods, wo_bf, g2, wr_hi, wr_lo, br)


GROUP_HID = MOE_PER_GROUP * MOE_HIDDEN


MOE_CHUNK = 160


def _moe_kernel(cnt_ref, h2_ref, route_ref, xm_ref, mod_ref, w1_ref, w3_ref, w2_ref, fg_ref, o_ref,
                hs_scr, rs_scr, os_scr, before_scr, *, final, tm):
    i = pl.program_id(0)
    off1 = cnt_ref[i, 0]
    off2 = off1 + cnt_ref[i, 1]
    off3 = off2 + cnt_ref[i, 2]
    starts = (jnp.int32(0), off1, off2, off3)
    ends = (off1, off2, off3, jnp.int32(tm))

    route = route_ref[...]
    r_hi, r_lo = _split(route)
    pick = (lax.broadcasted_iota(jnp.int32, (SUBLANES, LANES), 1)
            == ROUTE_GROUP + lax.broadcasted_iota(jnp.int32, (SUBLANES, LANES), 0))
    gt = lax.dot_general(jnp.where(pick, 1.0, 0.0).astype(BF16), r_hi, (((1,), (1,)), ((), ())),
                         preferred_element_type=F32)
    @pl.when(i == 0)
    def _():
        before_scr[...] = jnp.where(lax.broadcasted_iota(jnp.int32, (tm, tm), 0)
                                    < lax.broadcasted_iota(jnp.int32, (tm, tm), 1), 1.0, 0.0).astype(BF16)

    rank = jnp.dot(gt.astype(BF16), before_scr[...], preferred_element_type=F32)
    gt_i = gt.astype(jnp.int32)
    rank_i = rank.astype(jnp.int32)
    pos = jnp.zeros((1, tm), jnp.int32)
    for g in range(MOE_GROUPS):
        pos = pos + gt_i[g:g + 1, :] * (rank_i[g:g + 1, :] + starts[g])
    perm = jnp.where(lax.broadcasted_iota(jnp.int32, (tm, tm), 0) == pos, 1.0, 0.0).astype(BF16)
    hs_scr[...] = jnp.dot(perm, h2_ref[...], preferred_element_type=F32).astype(BF16)
    rs_scr[...] = (jnp.dot(perm, r_hi, preferred_element_type=F32)
                   + jnp.dot(perm, r_lo, preferred_element_type=F32))

    os_scr[...] = jnp.zeros_like(os_scr)
    for g in range(MOE_GROUPS):
        lo, hi = starts[g], ends[g]
        base = (lo // BF16_ROWS) * BF16_ROWS
        n_chunks = jnp.where(hi > lo, (hi - base + MOE_CHUNK - 1) // MOE_CHUNK, 0)

        def chunk_body(c, carry, g=g, lo=lo, hi=hi, base=base):
            r0 = pl.multiple_of(jnp.minimum(base + c * MOE_CHUNK, tm - MOE_CHUNK), BF16_ROWS)
            rows = pl.ds(r0, MOE_CHUNK)
            x = hs_scr[rows, :]
            gates = rs_scr[rows, :]
            a = jnp.dot(x, w1_ref[g], preferred_element_type=F32)
            b = jnp.dot(x, w3_ref[g], preferred_element_type=F32)
            hid = []
            for e in range(MOE_PER_GROUP):
                sl = slice(e * MOE_HIDDEN, (e + 1) * MOE_HIDDEN)
                hid.append((jax.nn.silu(a[:, sl]) * b[:, sl] * gates[:, e:e + 1]).astype(BF16))
            y = jnp.dot(jnp.concatenate(hid, axis=1), w2_ref[g], preferred_element_type=F32)
            rowid = r0 + lax.broadcasted_iota(jnp.int32, (MOE_CHUNK, 1), 0)
            member = (rowid >= lo) & (rowid < hi)
            os_scr[rows, :] = jnp.where(member, y, os_scr[rows, :])
            return carry

        lax.fori_loop(0, n_chunks, chunk_body, 0)

    o_hi, o_lo = _split(os_scr[...])
    moe = (lax.dot_general(perm, o_hi, (((0,), (0,)), ((), ())), preferred_element_type=F32)
           + lax.dot_general(perm, o_lo, (((0,), (0,)), ((), ())), preferred_element_type=F32))
    out = xm_ref[...] + mod_ref[5:6, :] * moe
    if final:
        out = _rms_rows(out) * fg_ref[...]
    o_ref[...] = out


def _moe_weight_kernel(w1_ref, w3_ref, w2_ref, o1_ref, o3_ref, o2_ref):
    for e in range(MOE_PER_GROUP):
        sl = slice(e * MOE_HIDDEN, (e + 1) * MOE_HIDDEN)
        o1_ref[:, sl] = w1_ref[e].astype(BF16)
        o3_ref[:, sl] = w3_ref[e].astype(BF16)
        o2_ref[sl, :] = w2_ref[e].astype(BF16)


def _moe_weights(w1, w3, w2):
    up = pl.BlockSpec((None, MOE_PER_GROUP, D_MODEL, MOE_HIDDEN), lambda l, g: (l, g, 0, 0))
    down = pl.BlockSpec((None, MOE_PER_GROUP, MOE_HIDDEN, D_MODEL), lambda l, g: (l, g, 0, 0))
    out = pl.BlockSpec((None, None, D_MODEL, GROUP_HID), lambda l, g: (l, g, 0, 0))
    shape = jax.ShapeDtypeStruct((DEPTH, MOE_GROUPS, D_MODEL, GROUP_HID), BF16)
    return pl.pallas_call(
        _moe_weight_kernel,
        grid=(DEPTH, MOE_GROUPS),
        in_specs=[up, up, down],
        out_specs=[out, out, out],
        out_shape=[shape, shape, shape],
        compiler_params=_cparams("parallel", "parallel"),
        name="moe_weights",
    )(w1, w3, w2)


def _moe(h2, route, tile_counts, xm, mods, mod_row, mod_tokens, w1g, w3g, w2g, fg, layer, *, final, tm=512):
    n = h2.shape[0]
    cnt = tile_counts[:, 0, ROUTE_GROUP:ROUTE_GROUP + MOE_GROUPS].reshape(
        n // tm, tm // (OUT_SEQS * S5_SEG), MOE_GROUPS).sum(axis=1)
    row = lambda w: pl.BlockSpec((tm, w), lambda i, c: (i, 0))
    mod_tiles = mod_tokens // tm
    wspec = pl.BlockSpec((None, MOE_GROUPS, D_MODEL, GROUP_HID), lambda i, c: (layer, 0, 0, 0),
                         pipeline_mode=pl.Buffered(1))
    return pl.pallas_call(
        functools.partial(_moe_kernel, final=final, tm=tm),
        grid_spec=pltpu.PrefetchScalarGridSpec(
            num_scalar_prefetch=1,
            grid=(n // tm,),
            in_specs=[row(D_MODEL), row(LANES), row(D_MODEL),
                      pl.BlockSpec((None, None, 6, D_MODEL), lambda i, c: (layer, mod_row + i // mod_tiles, 0, 0)),
                      wspec, wspec, wspec,
                      pl.BlockSpec((1, D_MODEL), lambda i, c: (0, 0))],
            out_specs=row(D_MODEL),
            scratch_shapes=[pltpu.VMEM((tm, D_MODEL), BF16), pltpu.VMEM((tm, LANES), F32),
                            pltpu.VMEM((tm, D_MODEL), F32), pltpu.VMEM((tm, tm), BF16)]),
        out_shape=jax.ShapeDtypeStruct((n, D_MODEL), F32),
        compiler_params=_cparams("arbitrary"),
        name="moe",
    )(cnt, h2, route, xm, mods, w1g, w3g, w2g, fg)


def kernel(x_prompt, x_sample, cache_a_k, cache_a_v, cache_b_k, cache_b_v, state_ret, state_ssm, c, c_ctx, mod_w, mod_b, norm1_g, norm2_g, w_in, a_qn_g, a_kn_g, b_rel_bias, ret_decay, ret_gn_g, s5_lam_re, s5_lam_im, s5_log_dt, s5_b_re, s5_b_im, s5_c_re, s5_c_im, s5_d, s5_glu_w, w_out, moe_gw, moe_gb, moe_ew, moe_eb, moe_w1, moe_w3, moe_w2, final_norm_g):
    n_ctx = BATCH * SEQ
    n_lat = DEC_BATCH * DEC_SEQ
    lat_seg = DEC_SEQ // S5_SEG

    cond = jnp.zeros((SUBLANES, D_MODEL), F32).at[0].set(c_ctx).at[1:1 + DEC_BATCH].set(c)
    mods = _modulation(cond, mod_w, mod_b).reshape(DEPTH, SUBLANES, 6, D_MODEL)

    rope_tabs = _rope_tables()
    s5_a, s5_bm, s5_cre, s5_cim = _s5_prepare(s5_lam_re, s5_lam_im, s5_log_dt, s5_b_re, s5_b_im,
                                              s5_c_re, s5_c_im)
    cak = cache_a_k.reshape(DEC_BATCH, DEPTH, PAST_LEN, A_KV_HEADS * HEAD_DIM)
    cav = cache_a_v.reshape(DEC_BATCH, DEPTH, PAST_LEN, A_KV_HEADS * HEAD_DIM)
    cbk = cache_b_k.reshape(DEC_BATCH, DEPTH, PAST_LEN, B_HEADS * HEAD_DIM)
    cbv = cache_b_v.reshape(DEC_BATCH, DEPTH, PAST_LEN, B_HEADS * HEAD_DIM)

    xc = x_prompt.reshape(n_ctx, D_MODEL)
    xs = x_sample.reshape(n_lat, D_MODEL)
    w1_all, w3_all, w2_all = _moe_weights(moe_w1, moe_w3, moe_w2)
    eye_h = jnp.eye(C_HEADS, dtype=F32)
    s0_bd = (state_ret[:, :, :, :, :, None, :] * eye_h[None, None, None, :, None, :, None]).reshape(
        DEC_BATCH, DEPTH, 2, C_HEADS * HEAD_DIM, C_HEADS * HEAD_DIM)
    ctx_state = ssm_states = None
    h0_zero = jnp.zeros((2, SUBLANES, 2 * S5_SP), F32)
    w_in_bf = w_in.astype(BF16)
    wo_bf = w_out.astype(BF16)
    glu_bf = s5_glu_w.astype(BF16)
    for l in range(DEPTH):
        final = l == DEPTH - 1
        g1 = norm1_g[l].reshape(1, D_MODEL)
        g2 = norm2_g[l].reshape(1, D_MODEL)
        fg = final_norm_g.reshape(1, D_MODEL)
        qn = jnp.tile(a_qn_g[l], A_HEADS).reshape(1, 256)
        kn = jnp.tile(a_kn_g[l], A_KV_HEADS).reshape(1, 128)
        dec = jnp.broadcast_to(ret_decay[l].reshape(2 * C_HEADS, 1), (2 * C_HEADS, LANES))
        gn = ret_gn_g[l].reshape(1, 256)
        dvec = s5_d[l].reshape(1, GROUP_WIDTH)
        wr = jnp.zeros((D_MODEL, LANES), F32).at[:, :MOE_GROUPS].set(moe_gw[l]).at[
            :, ROUTER_OFF:ROUTER_OFF + MOE_EXPERTS].set(moe_ew[l])
        br = jnp.zeros((1, LANES), F32).at[0, :MOE_GROUPS].set(moe_gb[l]).at[
            0, ROUTER_OFF:ROUTER_OFF + MOE_EXPERTS].set(moe_eb[l])
        wr_hi = wr.astype(BF16)
        wr_lo = (wr - wr_hi.astype(F32)).astype(BF16)
        na_bias = _na_bias(b_rel_bias[l])

        oa, ob, oc, du_tm, ctx_state = _ctx_front(xc, mods, g1, w_in_bf, qn, kn, dec, gn, l, ctx_state)
        od_tm, ssm_states = _s5(du_tm, h0_zero, s5_a, s5_bm, s5_cre, s5_cim, dvec, glu_bf, l,
                                nseg=1, fin_layer=l, fin_layers=DEPTH, prev_fin=ssm_states)
        xm, h2, route, counts = _output_stage(xc, (oa, ob, oc, od_tm), mods, 0, n_ctx, wo_bf, g2,
                                              wr_hi, wr_lo, br, l)
        xc = _moe(h2, route, counts, xm, mods, 0, n_ctx, w1_all, w3_all, w2_all, fg, l, final=final)

        zs, cg, du_tm = _project(xs, mods, 1, DEC_SEQ, g1, w_in_bf, qn, kn, rope_tabs, l, seq_len=DEC_SEQ)
        oa = _lat_attention_a(zs, cak, cav, l)
        ob = _lat_attention_b(zs, cbk, cbv, na_bias, l)
        oc = _retention(zs, cg, dec, gn, s0_bd, l, nb=DEC_BATCH, seq_len=DEC_SEQ)
        h0 = state_ssm[:, l].reshape(DEC_BATCH, 2, 2 * S5_SP).transpose(1, 0, 2)
        h0_seg = jnp.zeros((2, DEC_BATCH, lat_seg, 2 * S5_SP), F32)
        h0_seg = h0_seg.at[0, :, 0].set(h0[0]).at[1, :, lat_seg - 1].set(h0[1])
        od_tm, _ = _s5(du_tm, h0_seg.reshape(2, SUBLANES, 2 * S5_SP),
                       s5_a, s5_bm, s5_cre, s5_cim, dvec, glu_bf, l, nseg=lat_seg)
        xm, h2, route, counts = _output_stage(xs, (oa, ob, oc, od_tm), mods, 1, DEC_SEQ, wo_bf, g2,
                                              wr_hi, wr_lo, br, l)
        xs = _moe(h2, route, counts, xm, mods, 1, DEC_SEQ, w1_all, w3_all, w2_all, fg, l, final=final)

    new_ak, new_av, new_bk, new_bv, ret_states = ctx_state
    return (xc.reshape(BATCH, SEQ, D_MODEL), xs.reshape(DEC_BATCH, DEC_SEQ, D_MODEL),
            new_ak.reshape(BATCH, DEPTH, SEQ, A_KV_HEADS, HEAD_DIM),
            new_av.reshape(BATCH, DEPTH, SEQ, A_KV_HEADS, HEAD_DIM),
            new_bk.reshape(BATCH, DEPTH, SEQ, B_HEADS, HEAD_DIM),
            new_bv.reshape(BATCH, DEPTH, SEQ, B_HEADS, HEAD_DIM),
            ret_states,
            ssm_states.reshape(BATCH, DEPTH, 2, 2, S5_GROUPS, S5_STATE))
```

```python
import functools
import math

import numpy as np
import jax
import jax.numpy as jnp
from jax import lax
from jax.experimental import pallas as pl
from jax.experimental.pallas import tpu as pltpu

F32 = jnp.float32
BF16 = jnp.bfloat16

D_MODEL = 1024
BATCH = 32
SEQ = 256
DEPTH = 2
DEC_BATCH = 2
DEC_SEQ = 1024
PAST_LEN = 256
GRID_W = 64
HEAD_DIM = 64
GROUP_WIDTH = 256
A_HEADS = 4
A_KV_HEADS = 2
B_HEADS = 4
NA_ROWS = 8
NA_COLS = 16
C_HEADS = 4
S5_CH = 16
S5_GROUPS = 16
S5_STATE = 64
MOE_GROUPS = 4
MOE_PER_GROUP = 8
MOE_EXPERTS = 32
MOE_HIDDEN = 128
ROPE_THETA = 10000.0
EPS = 1e-6
IN_WIDTH = 2560
Q_SCALE = HEAD_DIM ** -0.5

OFF_AQ, OFF_AK, OFF_AV = 0, 256, 384
OFF_BQ, OFF_BK, OFF_BV = 512, 768, 1024
OFF_CQ, OFF_CK, OFF_CV, OFF_CG = 1280, 1536, 1792, 2048
OFF_DU = 2304

LANES = 128
SUBLANES = 8
BF16_ROWS = 16
S5_SP = S5_GROUPS * S5_STATE
S5_SEG = 256
S5_CHUNK = 256
ROUTER_OFF = 4
NEG_BIG = -1e30
VMEM_LIMIT = 56 * 1024 * 1024


def _cparams(*sem):
    return pltpu.CompilerParams(dimension_semantics=sem, vmem_limit_bytes=VMEM_LIMIT)


def _mod_spec(layer, first_row, tiles_per_row, grid_rank):
    if grid_rank == 1:
        return pl.BlockSpec((None, None, 6, D_MODEL), lambda i: (layer, first_row + i // tiles_per_row, 0, 0))
    return pl.BlockSpec((None, None, 6, D_MODEL), lambda i, g: (layer, first_row + i // tiles_per_row, 0, 0))


def _bdot(a, b):
    return jnp.dot(a.astype(BF16), b.astype(BF16), preferred_element_type=F32)


def _bdot_nt(a, b):
    return lax.dot_general(a.astype(BF16), b.astype(BF16), (((1,), (1,)), ((), ())),
                           preferred_element_type=F32)


def _bdot_tn(a, b):
    return lax.dot_general(a.astype(BF16), b.astype(BF16), (((0,), (0,)), ((), ())),
                           preferred_element_type=F32)


def _split(a):
    hi = a.astype(BF16)
    lo = (a - hi.astype(F32)).astype(BF16)
    return hi, lo


def _dot_hilo_lhs(a, b_bf16):
    hi, lo = _split(a)
    return (jnp.dot(hi, b_bf16, preferred_element_type=F32)
            + jnp.dot(lo, b_bf16, preferred_element_type=F32))


def _rms_rows(x):
    return x * lax.rsqrt(jnp.mean(x * x, axis=-1, keepdims=True) + EPS)


def _mod_kernel(cond_ref, w_ref, b_ref, o_ref):
    o_ref[...] = _bdot(jax.nn.silu(cond_ref[...]), w_ref[...]) + b_ref[...]


def _modulation(cond, mod_w, mod_b):
    tn = 1536
    return pl.pallas_call(
        _mod_kernel,
        grid=(DEPTH, 6 * D_MODEL // tn),
        in_specs=[pl.BlockSpec((SUBLANES, D_MODEL), lambda l, j: (0, 0)),
                  pl.BlockSpec((None, D_MODEL, tn), lambda l, j: (l, 0, j)),
                  pl.BlockSpec((None, 1, tn), lambda l, j: (l, 0, j))],
        out_specs=pl.BlockSpec((None, SUBLANES, tn), lambda l, j: (l, 0, j)),
        out_shape=jax.ShapeDtypeStruct((DEPTH, SUBLANES, 6 * D_MODEL), F32),
        compiler_params=_cparams("arbitrary", "arbitrary"),
        name="modulation",
    )(cond, mod_w, mod_b.reshape(DEPTH, 1, 6 * D_MODEL))


def _group_mean_matrix(w):
    ri = lax.broadcasted_iota(jnp.int32, (w, w), 0) // HEAD_DIM
    ci = lax.broadcasted_iota(jnp.int32, (w, w), 1) // HEAD_DIM
    return jnp.where(ri == ci, 1.0 / HEAD_DIM, 0.0).astype(BF16)


def _head_norm(t, g):
    ms = _dot_hilo_lhs(t * t, _group_mean_matrix(t.shape[1]))
    return t * lax.rsqrt(ms + EPS) * g


def _rope(t, cos, sa, sb):
    return (t * cos + pltpu.roll(t, LANES - 16, 1) * sa + pltpu.roll(t, 16, 1) * sb)


def _store_layer_slot(ref, slot, value):
    for s in range(ref.shape[0]):
        ref[s] = value if s == slot else jnp.zeros_like(value)


def _layer_slot_block(layer, first_call, tail):
    if first_call:
        return (None, DEPTH) + tail, (0,) * (1 + len(tail)), layer
    return (None, 1) + tail, (layer,) + (0,) * len(tail), 0


def _proj_kernel(x_ref, mod_ref, g1_ref, w_ref, qn_ref, kn_ref, cos_ref, sa_ref, sb_ref, z_ref, cg_ref, du_ref):
    h = _rms_rows(x_ref[...]) * g1_ref[...] * (1.0 + mod_ref[1:2, :]) + mod_ref[0:1, :]
    z = jnp.dot(h.astype(BF16), w_ref[...], preferred_element_type=F32)
    aq = _head_norm(z[:, OFF_AQ:OFF_AK], qn_ref[...])
    ak = _head_norm(z[:, OFF_AK:OFF_AV], kn_ref[...])
    for j in range(3):
        t = aq[:, j * LANES:(j + 1) * LANES] if j < 2 else ak
        sl = slice(0, LANES) if j == 2 else slice(j * LANES, (j + 1) * LANES)
        t = _rope(t, cos_ref[:, sl], sa_ref[:, sl], sb_ref[:, sl])
        z_ref[:, j * LANES:(j + 1) * LANES] = t.astype(BF16)
    z_ref[:, OFF_AV:OFF_CK] = z[:, OFF_AV:OFF_CK].astype(BF16)
    z_ref[:, OFF_CK:OFF_CV] = (z[:, OFF_CK:OFF_CV] * Q_SCALE).astype(BF16)
    z_ref[:, OFF_CV:OFF_CG] = z[:, OFF_CV:OFF_CG].astype(BF16)
    cg_ref[...] = z[:, OFF_CG:OFF_DU]
    du_ref[...] = z[:, OFF_DU:]


def _du_spec(grid_rank):
    if grid_rank == 1:
        return pl.BlockSpec((None, S5_SEG, GROUP_WIDTH), lambda i: (i // SUBLANES, 0, i % SUBLANES))
    return pl.BlockSpec((None, S5_SEG, GROUP_WIDTH), lambda i, g: (i // SUBLANES, 0, i % SUBLANES))


def _project(x, mods, mod_row, mod_tokens, g1, w_in_bf, qn, kn, rope_tabs, layer, *, seq_len):
    tm = S5_SEG
    n = x.shape[0]
    tps = seq_len // tm
    return pl.pallas_call(
        _proj_kernel,
        grid=(n // tm,),
        in_specs=[pl.BlockSpec((tm, D_MODEL), lambda i: (i, 0)),
                  _mod_spec(layer, mod_row, mod_tokens // tm, 1),
                  pl.BlockSpec((1, D_MODEL), lambda i: (0, 0)),
                  pl.BlockSpec((None, D_MODEL, IN_WIDTH), lambda i: (layer, 0, 0)),
                  pl.BlockSpec((1, 256), lambda i: (0, 0)),
                  pl.BlockSpec((1, 128), lambda i: (0, 0))]
                 + [pl.BlockSpec((tm, 256), lambda i: (i % tps, 0))] * 3,
        out_specs=[pl.BlockSpec((tm, OFF_CG), lambda i: (i, 0)),
                   pl.BlockSpec((tm, GROUP_WIDTH), lambda i: (i, 0)), _du_spec(1)],
        out_shape=[jax.ShapeDtypeStruct((n, OFF_CG), BF16),
                   jax.ShapeDtypeStruct((n, GROUP_WIDTH), F32),
                   jax.ShapeDtypeStruct((n // (tm * SUBLANES), S5_SEG, SUBLANES * GROUP_WIDTH), F32)],
        compiler_params=_cparams("parallel"),
        name="project",
    )(x, mods, g1, w_in_bf, qn, kn, *rope_tabs)


def _rope_tables():
    t = jnp.arange(DEC_SEQ)
    row = (t // GRID_W).astype(F32)
    col = (t % GRID_W).astype(F32)
    nf = HEAD_DIM // 4
    inv = ROPE_THETA ** (-jnp.arange(nf, dtype=F32) / nf)
    ang_r = row[:, None] * inv[None, :]
    ang_c = col[:, None] * inv[None, :]
    zeros = jnp.zeros_like(ang_r)
    cos = jnp.concatenate([jnp.cos(ang_r), jnp.cos(ang_r), jnp.cos(ang_c), jnp.cos(ang_c)], axis=-1)
    sa = jnp.concatenate([-jnp.sin(ang_r), zeros, -jnp.sin(ang_c), zeros], axis=-1)
    sb = jnp.concatenate([zeros, jnp.sin(ang_r), zeros, jnp.sin(ang_c)], axis=-1)
    return tuple(jnp.tile(a, (1, 4)) for a in (cos, sa, sb))


N_HEADS = 4


def _lane_head(width):
    return lax.broadcasted_iota(jnp.int32, (1, width), 1) // HEAD_DIM


def _stack_heads(q):
    head = _lane_head(q.shape[1])
    return jnp.concatenate([jnp.where(head == h, q, 0.0) for h in range(N_HEADS)], axis=0).astype(BF16)


def _stack_heads_gqa(q):
    lo = lax.broadcasted_iota(jnp.int32, (1, LANES), 1) < HEAD_DIM
    q = q.astype(F32)
    q01, q23 = q[:, :LANES], q[:, LANES:]
    blocks = [jnp.where(lo, q01, 0.0), jnp.where(lo, pltpu.roll(q01, HEAD_DIM, 1), 0.0),
              jnp.where(lo, 0.0, pltpu.roll(q23, HEAD_DIM, 1)), jnp.where(lo, 0.0, q23)]
    return jnp.concatenate(blocks, axis=0).astype(BF16)


def _spread_kv_gqa(v):
    lo = lax.broadcasted_iota(jnp.int32, (1, LANES), 1) < HEAD_DIM
    v = v.astype(F32)
    vr = pltpu.roll(v, HEAD_DIM, 1)
    return jnp.concatenate([jnp.where(lo, v, vr), jnp.where(lo, vr, v)], axis=1)


def _mha(qs, blocks, tq):
    scores = []
    for k, _, bias in blocks:
        s = _bdot_nt(qs, k)
        scores.append(s if bias is None else s + bias)
    m = functools.reduce(jnp.maximum, [jnp.max(s, axis=-1, keepdims=True) for s in scores])
    es = [jnp.exp(s - m) for s in scores]
    denom = functools.reduce(jnp.add, [jnp.sum(e, axis=-1, keepdims=True) for e in es])
    ps = [e.astype(BF16) for e in es]
    head = _lane_head(N_HEADS * HEAD_DIM)
    vals = [v.astype(BF16) for _, v, _ in blocks]
    o = None
    dall = None
    for h in range(N_HEADS):
        rows = slice(h * tq, (h + 1) * tq)
        for p, v in zip(ps, vals):
            t = jnp.dot(p[rows], jnp.where(head == h, v, jnp.zeros_like(v)), preferred_element_type=F32)
            o = t if o is None else o + t
        d = jnp.where(head == h, denom[rows], 0.0)
        dall = d if dall is None else dall + d
    return (o / dall).astype(BF16)


def _lat_attn_a_kernel(q_ref, kn_ref, vn_ref, kc_ref, vc_ref, o_ref):
    for b in range(DEC_BATCH):
        o_ref[b] = _mha(_stack_heads_gqa(q_ref[b] * Q_SCALE),
                        [(kc_ref[b], _spread_kv_gqa(vc_ref[b]), None),
                         (kn_ref[b], _spread_kv_gqa(vn_ref[b]), None)], q_ref.shape[1])


def _lat_attention_a(z, cache_k, cache_v, layer, tq=256):
    cache_spec = pl.BlockSpec((DEC_BATCH, None, PAST_LEN, 128), lambda j: (0, layer, 0, 0))
    return pl.pallas_call(
        _lat_attn_a_kernel,
        grid=(DEC_SEQ // tq,),
        in_specs=[pl.BlockSpec((DEC_BATCH, tq, 256), lambda j: (0, j, OFF_AQ // 256)),
                  pl.BlockSpec((DEC_BATCH, DEC_SEQ, 128), lambda j: (0, 0, OFF_AK // 128)),
                  pl.BlockSpec((DEC_BATCH, DEC_SEQ, 128), lambda j: (0, 0, OFF_AV // 128)),
                  cache_spec, cache_spec],
        out_specs=pl.BlockSpec((DEC_BATCH, tq, 256), lambda j: (0, j, 0)),
        out_shape=jax.ShapeDtypeStruct((DEC_BATCH, DEC_SEQ, 256), BF16),
        compiler_params=_cparams("parallel"),
        name="lat_attention_a",
    )(z, z, z, cache_k, cache_v)


NA_KEYS = NA_ROWS * GRID_W


NA_PAIRS = 2 * NA_ROWS - 2


NA_STEP_ROWS = 2


def _na_kernel(q_ref, k_ref, v_ref, kc_ref, vc_ref, bias_ref, o_ref):
    rows = DEC_SEQ // GRID_W
    outs = []
    for rr in range(NA_STEP_ROWS):
        r = pl.program_id(1) * NA_STEP_ROWS + rr
        row_start = jnp.clip(r - NA_ROWS // 2, 0, rows - NA_ROWS)
        start = pl.multiple_of(row_start * GRID_W, GRID_W)
        rel0 = row_start - r + NA_ROWS - 1
        kl = k_ref[pl.ds(start, NA_KEYS), :]
        vl = v_ref[pl.ds(start, NA_KEYS), :]
        bias = jnp.concatenate(
            [jnp.concatenate([bias_ref[h, rel0 + 2 * jp] for jp in range(NA_ROWS // 2)], axis=1)
             for h in range(B_HEADS)], axis=0)
        qrows = slice(rr * GRID_W, (rr + 1) * GRID_W)
        outs.append(_mha(_stack_heads(q_ref[qrows, :] * Q_SCALE),
                         [(kl, vl, bias), (kc_ref[...], vc_ref[...], None)], GRID_W))
    o_ref[...] = jnp.concatenate(outs, axis=0)


def _na_bias(rel_bias):
    nrel = 2 * NA_COLS - 1
    period = 2 * GRID_W
    b = rel_bias.astype(F32)
    ext = jnp.concatenate([b[..., NA_COLS - 1:],
                           jnp.zeros(b.shape[:-1] + (period - nrel,), F32),
                           b[..., :NA_COLS - 1]], axis=-1)
    flat = jnp.tile(ext, (1, 1, GRID_W))[..., :GRID_W * (period - 1)]
    toe = flat.reshape(b.shape[:-1] + (GRID_W, period - 1))[..., :GRID_W]
    col_start = np.clip(np.arange(GRID_W) - NA_COLS // 2, 0, GRID_W - NA_COLS)
    kc = np.arange(GRID_W)
    inside = (kc[None, :] >= col_start[:, None]) & (kc[None, :] < col_start[:, None] + NA_COLS)
    toe = jnp.where(jnp.asarray(inside), toe, NEG_BIG)
    return jnp.concatenate([toe[:, :-1], toe[:, 1:]], axis=-1)


def _lat_attention_b(z, cache_k, cache_v, bias, layer):
    rows = DEC_SEQ // GRID_W // NA_STEP_ROWS
    tq = NA_STEP_ROWS * GRID_W
    cache_spec = pl.BlockSpec((None, None, PAST_LEN, 256), lambda b, r: (b, layer, 0, 0))
    return pl.pallas_call(
        _na_kernel,
        grid=(DEC_BATCH, rows),
        in_specs=[pl.BlockSpec((tq, 256), lambda b, r: (b * rows + r, OFF_BQ // 256)),
                  pl.BlockSpec((DEC_SEQ, 256), lambda b, r: (b, OFF_BK // 256)),
                  pl.BlockSpec((DEC_SEQ, 256), lambda b, r: (b, OFF_BV // 256)),
                  cache_spec, cache_spec,
                  pl.BlockSpec((B_HEADS, NA_PAIRS, GRID_W, 2 * GRID_W), lambda b, r: (0, 0, 0, 0))],
        out_specs=pl.BlockSpec((tq, 256), lambda b, r: (b * rows + r, 0)),
        out_shape=jax.ShapeDtypeStruct((DEC_BATCH * DEC_SEQ, 256), BF16),
        compiler_params=_cparams("parallel", "parallel"),
        name="lat_attention_b",
    )(z, z, z, cache_k, cache_v, bias)


def _retention_core(q, k, v, g, dec_ref, gn_ref, dec_scr, *, seq_len, i0, decay_fill, s0_ref=None,
                    want_state=False):
    tq = q.shape[0]
    head = _lane_head(C_HEADS * HEAD_DIM)
    lg = jax.nn.log_sigmoid(dec_ref[...])

    def per_lane(row0):
        out = jnp.zeros((1, C_HEADS * HEAD_DIM), F32)
        for h in range(C_HEADS):
            out = jnp.where(head == h, lg[row0 + h:row0 + h + 1, 0:1], out)
        return out

    lgf_l, lgb_l = per_lane(0), per_lane(C_HEADS)
    qi = (i0 + lax.broadcasted_iota(jnp.int32, (tq, 1), 0)).astype(F32)

    def fill_decay():
        kj = lax.broadcasted_iota(jnp.int32, (1, seq_len), 1).astype(F32)
        diff = qi - kj
        for h in range(C_HEADS):
            lgf = lg[h:h + 1, 0:1]
            lgb = lg[C_HEADS + h:C_HEADS + h + 1, 0:1]
            dec_scr[h * tq:(h + 1) * tq, :] = (
                jnp.where(diff >= 0, jnp.exp(lgf * jnp.maximum(diff, 0.0)), 0.0)
                + jnp.where(diff <= 0, jnp.exp(lgb * jnp.maximum(-diff, 0.0)), 0.0))

    if decay_fill == "first_step":
        pl.when(pl.program_id(0) == 0)(fill_decay)
    elif decay_fill == "every_step":
        fill_decay()
    else:
        assert decay_fill == "filled"

    v = v.astype(BF16)
    sc = (_bdot_nt(_stack_heads(q), k) * dec_scr[...]).astype(BF16)
    o = None
    for h in range(C_HEADS):
        t = jnp.dot(sc[h * tq:(h + 1) * tq], jnp.where(head == h, v, jnp.zeros_like(v)),
                    preferred_element_type=F32)
        o = t if o is None else o + t
    if s0_ref is not None:
        o = (o + _bdot(q, s0_ref[0]) * jnp.exp(lgf_l * (qi + 1.0))
             + _bdot(q, s0_ref[1]) * jnp.exp(lgb_l * (seq_len - qi)))
    gm = _group_mean_matrix(C_HEADS * HEAD_DIM)
    dlt = o - _dot_hilo_lhs(o, gm)
    var = _dot_hilo_lhs(dlt * dlt, gm)
    out = (dlt * lax.rsqrt(var + EPS) * gn_ref[...] * jax.nn.silu(g)).astype(BF16)
    if not want_state:
        return out, None
    kpos = lax.broadcasted_iota(jnp.int32, (seq_len, 1), 0).astype(F32)
    sf = _bdot_tn(k * jnp.exp(lgf_l * (seq_len - 1.0 - kpos)), v)
    sb = _bdot_tn(k * jnp.exp(lgb_l * kpos), v)
    return out, (sf, sb)


def _store_retention_state(st_ref, slot, state):
    for s in range(st_ref.shape[0]):
        for d in range(2):
            for h in range(C_HEADS):
                sl = slice(h * HEAD_DIM, (h + 1) * HEAD_DIM)
                st_ref[s, d, h] = state[d][sl, sl] if s == slot else jnp.zeros((HEAD_DIM, HEAD_DIM), F32)


def _retention_kernel(q_ref, g_ref, k_ref, v_ref, dec_ref, gn_ref, s0_ref, o_ref, dec_scr, *, seq_len, tq):
    for b in range(q_ref.shape[0]):
        o_ref[b], _ = _retention_core(q_ref[b], k_ref[b], v_ref[b], g_ref[b], dec_ref, gn_ref, dec_scr,
                                      seq_len=seq_len, i0=pl.program_id(0) * tq,
                                      decay_fill="every_step" if b == 0 else "filled", s0_ref=s0_ref.at[b])


def _retention(z, cg, dec, gn, s0, layer, *, tq=256):
    nb, seq_len = z.shape[:2]
    return pl.pallas_call(
        functools.partial(_retention_kernel, seq_len=seq_len, tq=tq),
        grid=(seq_len // tq,),
        in_specs=[pl.BlockSpec((nb, tq, 256), lambda j: (0, j, OFF_CQ // 256)),
                  pl.BlockSpec((nb, tq, 256), lambda j: (0, j, 0)),
                  pl.BlockSpec((nb, seq_len, 256), lambda j: (0, 0, OFF_CK // 256)),
                  pl.BlockSpec((nb, seq_len, 256), lambda j: (0, 0, OFF_CV // 256)),
                  pl.BlockSpec((SUBLANES, LANES), lambda j: (0, 0)),
                  pl.BlockSpec((1, 256), lambda j: (0, 0)),
                  pl.BlockSpec((nb, None, 2, 256, 256), lambda j: (0, layer, 0, 0, 0))],
        out_specs=pl.BlockSpec((nb, tq, 256), lambda j: (0, j, 0)),
        out_shape=jax.ShapeDtypeStruct((nb, seq_len, 256), BF16),
        scratch_shapes=[pltpu.VMEM((C_HEADS * tq, seq_len), F32)],
        compiler_params=_cparams("arbitrary"),
        name="retention",
    )(z, cg, z, z, dec, gn, s0)


CTX_SEQS = 2


def _ctx_front_kernel(x_ref, mod_ref, g1_ref, w_ref, qn_ref, kn_ref, dec_ref, gn_ref, *rest, n_alias, slot):
    (oa_ref, ob_ref, oc_ref, du_ref, ak_ref, av_ref, bk_ref, bv_ref, st_ref, dec_scr) = rest[n_alias:]
    tq = x_ref.shape[0] // CTX_SEQS
    h = _rms_rows(x_ref[...]) * g1_ref[...] * (1.0 + mod_ref[1:2, :]) + mod_ref[0:1, :]
    zz = jnp.dot(h.astype(BF16), w_ref[...], preferred_element_type=F32)
    for s in range(CTX_SEQS):
        rows = slice(s * tq, (s + 1) * tq)
        z = zz[rows, :]
        aq = _head_norm(z[:, OFF_AQ:OFF_AK], qn_ref[...])
        ak = _head_norm(z[:, OFF_AK:OFF_AV], kn_ref[...])
        av, bq, bk, bv = (z[:, OFF_AV:OFF_BQ], z[:, OFF_BQ:OFF_BK], z[:, OFF_BK:OFF_BV], z[:, OFF_BV:OFF_CQ])
        oa_ref[rows, :] = _mha(_stack_heads_gqa(aq * Q_SCALE), [(ak, _spread_kv_gqa(av), None)], tq)
        ob_ref[rows, :] = _mha(_stack_heads(bq * Q_SCALE), [(bk, bv, None)], tq)
        oc_ref[rows, :], state = _retention_core(
            z[:, OFF_CQ:OFF_CK], z[:, OFF_CK:OFF_CV] * Q_SCALE, z[:, OFF_CV:OFF_CG], z[:, OFF_CG:OFF_DU],
            dec_ref, gn_ref, dec_scr, seq_len=tq, i0=0, decay_fill="first_step" if s == 0 else "filled",
            want_state=True)
        du_ref[:, s * GROUP_WIDTH:(s + 1) * GROUP_WIDTH] = z[:, OFF_DU:]
        _store_layer_slot(ak_ref.at[s], slot, ak)
        _store_layer_slot(av_ref.at[s], slot, av)
        _store_layer_slot(bk_ref.at[s], slot, bk)
        _store_layer_slot(bv_ref.at[s], slot, bv)
        _store_retention_state(st_ref.at[s], slot, state)


def _ctx_front(x, mods, g1, w_in_bf, qn, kn, dec, gn, layer, prev):
    assert SEQ == S5_SEG
    tm = CTX_SEQS * SEQ
    n = x.shape[0]
    nb = n // SEQ
    steps = n // tm
    per_blk = SUBLANES // CTX_SEQS
    const = lambda *shape: pl.BlockSpec(shape, lambda i: (0,) * len(shape))
    row = lambda w: pl.BlockSpec((tm, w), lambda i: (i, 0))
    in_specs = [row(D_MODEL), _mod_spec(layer, 0, steps, 1), const(1, D_MODEL),
                pl.BlockSpec((None, D_MODEL, IN_WIDTH), lambda i: (layer, 0, 0)),
                const(1, 256), const(1, 128), const(SUBLANES, LANES), const(1, 256)]
    args = [x, mods, g1, w_in_bf, qn, kn, dec, gn]
    out_specs = [row(256), row(256), row(256),
                 pl.BlockSpec((None, S5_SEG, CTX_SEQS * GROUP_WIDTH), lambda i: (i // per_blk, 0, i % per_blk))]
    out_shape = [jax.ShapeDtypeStruct((n, 256), BF16)] * 3 + [
        jax.ShapeDtypeStruct((nb // SUBLANES, S5_SEG, SUBLANES * GROUP_WIDTH), F32)]
    first = prev is None
    slot = 0
    for tail in ((SEQ, 128), (SEQ, 128), (SEQ, 256), (SEQ, 256), (2, C_HEADS, HEAD_DIM, HEAD_DIM)):
        blk, idx, slot = _layer_slot_block(layer, first, tail)
        out_specs.append(pl.BlockSpec((CTX_SEQS,) + blk[1:], lambda i, idx=idx: (i,) + idx))
        out_shape.append(jax.ShapeDtypeStruct((nb, DEPTH) + tail, F32))
    aliases = {}
    if not first:
        for k, arr in enumerate(prev):
            aliases[len(args)] = 4 + k
            in_specs.append(pl.BlockSpec(memory_space=pl.ANY))
            args.append(arr)
    outs = pl.pallas_call(
        functools.partial(_ctx_front_kernel, n_alias=len(aliases), slot=slot),
        grid=(steps,),
        in_specs=in_specs,
        out_specs=out_specs,
        out_shape=out_shape,
        scratch_shapes=[pltpu.VMEM((C_HEADS * SEQ, SEQ), F32)],
        input_output_aliases=aliases,
        compiler_params=_cparams("arbitrary"),
        name="ctx_front",
    )(*args)
    return outs[0], outs[1], outs[2], outs[3], tuple(outs[4:])


def _s5_prep_kernel(lre_ref, lim_ref, ldt_ref, bre_ref, bim_ref, cre_ref, cim_ref,
                    a_ref, bm_ref, cro_ref, cio_ref, bm_scr, cr_scr, ci_scr):
    lre = lre_ref[...]
    lim = lim_ref[...]
    dt = jnp.exp(ldt_ref[...])
    mag = jnp.exp(lre * dt)
    a_re = mag * jnp.cos(lim * dt)
    a_im = mag * jnp.sin(lim * dt)
    den = lre * lre + lim * lim
    r_re = ((a_re - 1.0) * lre + a_im * lim) / den
    r_im = (a_im * lre - (a_re - 1.0) * lim) / den
    bm_scr[...] = jnp.zeros_like(bm_scr)
    cr_scr[...] = jnp.zeros_like(cr_scr)
    ci_scr[...] = jnp.zeros_like(ci_scr)
    for g in range(S5_GROUPS):
        rows = slice(g * S5_CH, (g + 1) * S5_CH)
        cols = slice(g * S5_STATE, (g + 1) * S5_STATE)
        a_ref[0:1, cols] = a_re[g:g + 1, :]
        a_ref[1:2, cols] = a_im[g:g + 1, :]
        rr, ri = r_re[g:g + 1, :], r_im[g:g + 1, :]
        br, bi = bre_ref[g], bim_ref[g]
        bm_scr[rows, cols] = rr * br - ri * bi
        bm_scr[rows, S5_SP + g * S5_STATE:S5_SP + (g + 1) * S5_STATE] = rr * bi + ri * br
        cr_scr[cols, rows] = cre_ref[g]
        ci_scr[cols, rows] = cim_ref[g]
    bm_ref[...] = bm_scr[...].astype(BF16)
    cro_ref[...] = cr_scr[...].astype(BF16)
    cio_ref[...] = ci_scr[...].astype(BF16)


def _s5_prepare(lam_re, lam_im, log_dt, b_re, b_im, c_re, c_im):
    gp = (S5_GROUPS, S5_STATE)
    ldt = jnp.broadcast_to(log_dt[..., None], (DEPTH, 2) + gp)
    bt = [jnp.swapaxes(t, -1, -2) for t in (b_re, b_im)]
    ct = [jnp.swapaxes(t, -1, -2) for t in (c_re, c_im)]

    def spec(*tail):
        return pl.BlockSpec((None, None) + tail, lambda l, d: (l, d) + (0,) * len(tail))

    return pl.pallas_call(
        _s5_prep_kernel,
        grid=(DEPTH, 2),
        in_specs=[spec(*gp)] * 3 + [spec(S5_GROUPS, S5_CH, S5_STATE)] * 2 + [spec(S5_GROUPS, S5_STATE, S5_CH)] * 2,
        out_specs=[spec(2, S5_SP), spec(GROUP_WIDTH, 2 * S5_SP), spec(S5_SP, GROUP_WIDTH), spec(S5_SP, GROUP_WIDTH)],
        out_shape=[jax.ShapeDtypeStruct((DEPTH, 2, 2, S5_SP), F32),
                   jax.ShapeDtypeStruct((DEPTH, 2, GROUP_WIDTH, 2 * S5_SP), BF16),
                   jax.ShapeDtypeStruct((DEPTH, 2, S5_SP, GROUP_WIDTH), BF16),
                   jax.ShapeDtypeStruct((DEPTH, 2, S5_SP, GROUP_WIDTH), BF16)],
        scratch_shapes=[pltpu.VMEM((GROUP_WIDTH, 2 * S5_SP), F32), pltpu.VMEM((S5_SP, GROUP_WIDTH), F32),
                        pltpu.VMEM((S5_SP, GROUP_WIDTH), F32)],
        compiler_params=_cparams("parallel", "parallel"),
        name="s5_prepare",
    )(lam_re, lam_im, ldt, bt[0], bt[1], ct[0], ct[1])


def _cmul(ar, ai, br, bi):
    return ar * br - ai * bi, ar * bi + ai * br


def _s5_kernel(u_ref, h0_ref, a_ref, bm_ref, cre_ref, cim_ref, dvec_ref, glu_ref, *rest, nseg, slot):
    od_ref, fin_ref, x_scr, s_scr, y_scr = rest[-5:]
    steps = S5_SEG
    rows = steps * SUBLANES
    chunk = S5_CHUNK
    chunk_steps = chunk // SUBLANES
    nchunk = rows // chunk
    seg = lax.broadcasted_iota(jnp.int32, (SUBLANES, S5_SP), 0) % nseg

    for d in range(2):
        ar = jnp.broadcast_to(a_ref[d, 0:1, :], (SUBLANES, S5_SP))
        ai = jnp.broadcast_to(a_ref[d, 1:2, :], (SUBLANES, S5_SP))

        def row0(k):
            c = k if d == 0 else nchunk - 1 - k
            return c * chunk if isinstance(c, int) else pl.multiple_of(c * chunk, chunk)

        def input_part(k, buf):
            x_scr[buf] = jnp.dot(u_ref[pl.ds(row0(k), chunk), :].astype(BF16), bm_ref[d],
                                 preferred_element_type=F32)

        def scan_part(buf, carry, store):
            sr, si = carry
            for t in range(chunk_steps):
                r = (t if d == 0 else chunk_steps - 1 - t) * SUBLANES
                pr, pi = _cmul(ar, ai, sr, si)
                sr = pr + x_scr[buf, r:r + SUBLANES, 0:S5_SP]
                si = pi + x_scr[buf, r:r + SUBLANES, S5_SP:]
                if store:
                    s_scr[buf, r:r + SUBLANES, 0:S5_SP] = sr
                    s_scr[buf, r:r + SUBLANES, S5_SP:] = si
            return sr, si

        def output_part(k, buf):
            y = _bdot(s_scr[buf, :, 0:S5_SP], cre_ref[d]) - _bdot(s_scr[buf, :, S5_SP:], cim_ref[d])
            rows_k = pl.ds(row0(k), chunk)
            if d == 0:
                y_scr[rows_k, :] = y
            else:
                zz = jax.nn.gelu(y_scr[rows_k, :] + y + dvec_ref[...] * u_ref[rows_k, :])
                od_ref[rows_k, :] = (zz * jax.nn.sigmoid(_bdot(zz, glu_ref[...]))).astype(BF16)

        def half(k, buf, carry, store, nxt=True, prev=True):
            if nxt:
                input_part(k + 1, 1 - buf)
            carry = scan_part(buf, carry, store)
            if store and prev:
                output_part(k - 1, 1 - buf)
            return carry

        def run_pass(carry, store):
            input_part(0, 0)
            carry = half(0, 0, carry, store, prev=False)
            carry = half(1, 1, carry, store)

            def pair(j, c):
                c = half(2 * j, 0, c, store)
                return half(2 * j + 1, 1, c, store)
            carry = lax.fori_loop(1, nchunk // 2 - 1, pair, carry)
            carry = half(nchunk - 2, 0, carry, store)
            carry = half(nchunk - 1, 1, carry, store, nxt=False)
            if store:
                output_part(nchunk - 1, 1)
            return carry

        init = (h0_ref[d, :, 0:S5_SP], h0_ref[d, :, S5_SP:])
        if nseg > 1:
            zero = jnp.zeros((SUBLANES, S5_SP), F32)
            fr, fi = run_pass((zero, zero), store=False)
            pr, pi = ar, ai
            for _ in range(int(math.log2(steps))):
                pr, pi = _cmul(pr, pi, pr, pi)
            cr, ci = init
            shift = 1 if d == 0 else SUBLANES - 1
            order = range(1, nseg) if d == 0 else range(nseg - 2, -1, -1)
            for s in order:
                ncr, nci = pltpu.roll(cr, shift, 0), pltpu.roll(ci, shift, 0)
                nfr, nfi = pltpu.roll(fr, shift, 0), pltpu.roll(fi, shift, 0)
                qr, qi = _cmul(pr, pi, ncr, nci)
                cr = jnp.where(seg == s, qr + nfr, cr)
                ci = jnp.where(seg == s, qi + nfi, ci)
            init = (cr, ci)
        sr, si = run_pass(init, store=True)
        for s in range(fin_ref.shape[1] // (4 * S5_SP)):
            base = (4 * s + 2 * d) * S5_SP
            fin_ref[:, base:base + S5_SP] = sr if s == slot else jnp.zeros_like(sr)
            fin_ref[:, base + S5_SP:base + 2 * S5_SP] = si if s == slot else jnp.zeros_like(si)


def _s5(du_tm, h0, a, bmat, cre, cim, dvec, glu_bf, layer, *, nseg, fin_layer=0, fin_layers=1,
        prev_fin=None):
    nblk = du_tm.shape[0]
    rows = S5_SEG * SUBLANES
    fin_w = 4 * S5_SP
    in_specs = [pl.BlockSpec((None, rows, GROUP_WIDTH), lambda i: (i, 0, 0)),
                pl.BlockSpec((2, SUBLANES, 2 * S5_SP), lambda i: (0, 0, 0)),
                pl.BlockSpec((None, 2, 2, S5_SP), lambda i: (layer, 0, 0, 0)),
                pl.BlockSpec((None, 2, GROUP_WIDTH, 2 * S5_SP), lambda i: (layer, 0, 0, 0)),
                pl.BlockSpec((None, 2, S5_SP, GROUP_WIDTH), lambda i: (layer, 0, 0, 0)),
                pl.BlockSpec((None, 2, S5_SP, GROUP_WIDTH), lambda i: (layer, 0, 0, 0)),
                pl.BlockSpec((1, GROUP_WIDTH), lambda i: (0, 0)),
                pl.BlockSpec((None, GROUP_WIDTH, GROUP_WIDTH), lambda i: (layer, 0, 0))]
    args = [du_tm.reshape(nblk, rows, GROUP_WIDTH), h0, a, bmat, cre, cim, dvec, glu_bf]
    aliases = {}
    if prev_fin is not None:
        aliases[len(args)] = 1
        in_specs.append(pl.BlockSpec(memory_space=pl.ANY))
        args.append(prev_fin)
        fin_spec, slot = pl.BlockSpec((SUBLANES, fin_w), lambda i: (i, fin_layer)), 0
    else:
        fin_spec, slot = pl.BlockSpec((SUBLANES, fin_layers * fin_w), lambda i: (i, 0)), fin_layer
    od, fin = pl.pallas_call(
        functools.partial(_s5_kernel, nseg=nseg, slot=slot),
        grid=(nblk,),
        in_specs=in_specs,
        out_specs=[pl.BlockSpec((None, rows, GROUP_WIDTH), lambda i: (i, 0, 0)), fin_spec],
        out_shape=[jax.ShapeDtypeStruct((nblk, rows, GROUP_WIDTH), BF16),
                   jax.ShapeDtypeStruct((nblk * SUBLANES, fin_layers * fin_w), F32)],
        scratch_shapes=[pltpu.VMEM((2, S5_CHUNK, 2 * S5_SP), F32), pltpu.VMEM((2, S5_CHUNK, 2 * S5_SP), F32),
                        pltpu.VMEM((rows, GROUP_WIDTH), F32)],
        input_output_aliases=aliases,
        compiler_params=_cparams("parallel"),
        name="s5",
    )(*args)
    return od.reshape(nblk, S5_SEG, SUBLANES * GROUP_WIDTH), fin


ROUTE_GROUP = MOE_PER_GROUP
OUT_SEQS = 2


def _out_kernel(x_ref, oa_ref, ob_ref, oc_ref, od_ref, mod_ref, wo_ref, g2_ref, wrh_ref, wrl_ref, br_ref,
                xm_ref, h2_ref, route_ref, cnt_ref):
    od = jnp.concatenate([od_ref[:, s * GROUP_WIDTH:(s + 1) * GROUP_WIDTH] for s in range(OUT_SEQS)], axis=0)
    mix = functools.reduce(jnp.add, [
        _bdot(o, wo_ref[i * GROUP_WIDTH:(i + 1) * GROUP_WIDTH, :])
        for i, o in enumerate((oa_ref[...], ob_ref[...], oc_ref[...], od))])
    xm = x_ref[...] + mod_ref[2:3, :] * mix
    xm_ref[...] = xm
    h2 = _rms_rows(xm) * g2_ref[...] * (1.0 + mod_ref[4:5, :]) + mod_ref[3:4, :]
    h2_ref[...] = h2.astype(BF16)

    h_hi, h_lo = _split(h2)
    logits = (jnp.dot(h_hi, wrh_ref[...], preferred_element_type=F32)
              + jnp.dot(h_hi, wrl_ref[...], preferred_element_type=F32)
              + jnp.dot(h_lo, wrh_ref[...], preferred_element_type=F32)) + br_ref[...]
    lane_i = lax.broadcasted_iota(jnp.int32, logits.shape, 1)
    lane = lane_i.astype(F32)
    big = jnp.float32(2 ** 30)
    gmask = lane_i < MOE_GROUPS
    gl = jnp.where(gmask, logits, -jnp.inf)
    gmax = jnp.max(gl, axis=-1, keepdims=True)
    p_top = 1.0 / jnp.sum(jnp.exp(gl - gmax), axis=-1, keepdims=True)
    g_top = jnp.min(jnp.where(gl == gmax, lane, big), axis=-1, keepdims=True)
    e_lane = lane_i - ROUTER_OFF
    lane_group = (e_lane // MOE_PER_GROUP).astype(F32)
    emask = (e_lane >= 0) & (e_lane < MOE_EXPERTS) & (lane_group == g_top)
    el = jnp.where(emask, logits, -jnp.inf)
    m1 = jnp.max(el, axis=-1, keepdims=True)
    i1 = jnp.min(jnp.where(el == m1, lane, big), axis=-1, keepdims=True)
    el2 = jnp.where(lane == i1, -jnp.inf, el)
    m2 = jnp.max(el2, axis=-1, keepdims=True)
    i2 = jnp.min(jnp.where(el2 == m2, lane, big), axis=-1, keepdims=True)
    e2 = jnp.exp(m2 - m1)
    den = 1.0 + e2
    gates = (jnp.where(lane == i1, (1.0 / den) * p_top, 0.0)
             + jnp.where(lane == i2, (e2 / den) * p_top, 0.0))
    route = jnp.where(lane == ROUTE_GROUP + g_top, 1.0, 0.0)
    for g in range(MOE_GROUPS):
        local = pltpu.roll(gates, LANES - ROUTER_OFF - g * MOE_PER_GROUP, 1)
        route = route + jnp.where((g_top == g) & (lane_i < MOE_PER_GROUP), local, 0.0)
    route_ref[...] = route
    cnt_ref[...] = jnp.broadcast_to(jnp.sum(route, axis=0, keepdims=True), (SUBLANES, LANES)).astype(jnp.int32)


def _output_stage(x, mixes, mods, mod_row, mod_tokens, wo_bf, g2, wr_hi, wr_lo, br, layer):
    tm = OUT_SEQS * S5_SEG
    n = x.shape[0]
    row = lambda w: pl.BlockSpec((tm, w), lambda i: (i, 0))
    const = lambda shape: pl.BlockSpec(shape, lambda i: (0,) * len(shape))
    per_blk = SUBLANES // OUT_SEQS
    return pl.pallas_call(
        _out_kernel,
        grid=(n // tm,),
        in_specs=[row(D_MODEL), row(256), row(256), row(256),
                  pl.BlockSpec((None, S5_SEG, OUT_SEQS * GROUP_WIDTH), lambda i: (i // per_blk, 0, i % per_blk)),
                  _mod_spec(layer, mod_row, mod_tokens // tm, 1),
                  pl.BlockSpec((None, D_MODEL, D_MODEL), lambda i: (layer, 0, 0)), const((1, D_MODEL)),
                  const((D_MODEL, LANES)), const((D_MODEL, LANES)), const((1, LANES))],
        out_specs=[row(D_MODEL), row(D_MODEL), row(LANES),
                   pl.BlockSpec((None, SUBLANES, LANES), lambda i: (i, 0, 0))],
        out_shape=[jax.ShapeDtypeStruct((n, D_MODEL), F32),
                   jax.ShapeDtypeStruct((n, D_MODEL), BF16),
                   jax.ShapeDtypeStruct((n, LANES), F32),
                   jax.ShapeDtypeStruct((n // tm, SUBLANES, LANES), jnp.int32)],
        compiler_params=_cparams("parallel"),
        name="output_stage",
    )(x, *mixes, mods, wo_bf, g2, wr_hi, wr_lo, br)


GROUP_HID = MOE_PER_GROUP * MOE_HIDDEN


MOE_CHUNK = 160


def _moe_kernel(cnt_ref, h2_ref, route_ref, xm_ref, mod_ref, w1_ref, w3_ref, w2_ref, fg_ref, o_ref,
                hs_scr, rs_scr, os_scr, before_scr, *, final, tm):
    i = pl.program_id(0)
    off1 = cnt_ref[i, 0]
    off2 = off1 + cnt_ref[i, 1]
    off3 = off2 + cnt_ref[i, 2]
    starts = (jnp.int32(0), off1, off2, off3)
    ends = (off1, off2, off3, jnp.int32(tm))

    route = route_ref[...]
    r_hi, r_lo = _split(route)
    pick = (lax.broadcasted_iota(jnp.int32, (SUBLANES, LANES), 1)
            == ROUTE_GROUP + lax.broadcasted_iota(jnp.int32, (SUBLANES, LANES), 0))
    gt = lax.dot_general(jnp.where(pick, 1.0, 0.0).astype(BF16), r_hi, (((1,), (1,)), ((), ())),
                         preferred_element_type=F32)
    @pl.when(i == 0)
    def _():
        before_scr[...] = jnp.where(lax.broadcasted_iota(jnp.int32, (tm, tm), 0)
                                    < lax.broadcasted_iota(jnp.int32, (tm, tm), 1), 1.0, 0.0).astype(BF16)

    rank = jnp.dot(gt.astype(BF16), before_scr[...], preferred_element_type=F32)
    gt_i = gt.astype(jnp.int32)
    rank_i = rank.astype(jnp.int32)
    pos = jnp.zeros((1, tm), jnp.int32)
    for g in range(MOE_GROUPS):
        pos = pos + gt_i[g:g + 1, :] * (rank_i[g:g + 1, :] + starts[g])
    perm = jnp.where(lax.broadcasted_iota(jnp.int32, (tm, tm), 0) == pos, 1.0, 0.0).astype(BF16)
    hs_scr[...] = jnp.dot(perm, h2_ref[...], preferred_element_type=F32).astype(BF16)
    rs_scr[...] = (jnp.dot(perm, r_hi, preferred_element_type=F32)
                   + jnp.dot(perm, r_lo, preferred_element_type=F32))

    os_scr[...] = jnp.zeros_like(os_scr)
    for g in range(MOE_GROUPS):
        lo, hi = starts[g], ends[g]
        base = (lo // BF16_ROWS) * BF16_ROWS
        n_chunks = jnp.where(hi > lo, (hi - base + MOE_CHUNK - 1) // MOE_CHUNK, 0)

        def chunk_body(c, carry, g=g, lo=lo, hi=hi, base=base):
            r0 = pl.multiple_of(jnp.minimum(base + c * MOE_CHUNK, tm - MOE_CHUNK), BF16_ROWS)
            rows = pl.ds(r0, MOE_CHUNK)
            x = hs_scr[rows, :]
            gates = rs_scr[rows, :]
            a = jnp.dot(x, w1_ref[g], preferred_element_type=F32)
            b = jnp.dot(x, w3_ref[g], preferred_element_type=F32)
            hid = []
            for e in range(MOE_PER_GROUP):
                sl = slice(e * MOE_HIDDEN, (e + 1) * MOE_HIDDEN)
                hid.append((jax.nn.silu(a[:, sl]) * b[:, sl] * gates[:, e:e + 1]).astype(BF16))
            y = jnp.dot(jnp.concatenate(hid, axis=1), w2_ref[g], preferred_element_type=F32)
            rowid = r0 + lax.broadcasted_iota(jnp.int32, (MOE_CHUNK, 1), 0)
            member = (rowid >= lo) & (rowid < hi)
            os_scr[rows, :] = jnp.where(member, y, os_scr[rows, :])
            return carry

        lax.fori_loop(0, n_chunks, chunk_body, 0)

    o_hi, o_lo = _split(os_scr[...])
    moe = (lax.dot_general(perm, o_hi, (((0,), (0,)), ((), ())), preferred_element_type=F32)
           + lax.dot_general(perm, o_lo, (((0,), (0,)), ((), ())), preferred_element_type=F32))
    out = xm_ref[...] + mod_ref[5:6, :] * moe
    if final:
        out = _rms_rows(out) * fg_ref[...]
    o_ref[...] = out


def _moe_weight_kernel(w1_ref, w3_ref, w2_ref, o1_ref, o3_ref, o2_ref):
    for e in range(MOE_PER_GROUP):
        sl = slice(e * MOE_HIDDEN, (e + 1) * MOE_HIDDEN)
        o1_ref[:, sl] = w1_ref[e].astype(BF16)
        o3_ref[:, sl] = w3_ref[e].astype(BF16)
        o2_ref[sl, :] = w2_ref[e].astype(BF16)


def _moe_weights(w1, w3, w2):
    up = pl.BlockSpec((None, MOE_PER_GROUP, D_MODEL, MOE_HIDDEN), lambda l, g: (l, g, 0, 0))
    down = pl.BlockSpec((None, MOE_PER_GROUP, MOE_HIDDEN, D_MODEL), lambda l, g: (l, g, 0, 0))
    out = pl.BlockSpec((None, None, D_MODEL, GROUP_HID), lambda l, g: (l, g, 0, 0))
    shape = jax.ShapeDtypeStruct((DEPTH, MOE_GROUPS, D_MODEL, GROUP_HID), BF16)
    return pl.pallas_call(
        _moe_weight_kernel,
        grid=(DEPTH, MOE_GROUPS),
        in_specs=[up, up, down],
        out_specs=[out, out, out],
        out_shape=[shape, shape, shape],
        compiler_params=_cparams("parallel", "parallel"),
        name="moe_weights",
    )(w1, w3, w2)


def _moe(h2, route, tile_counts, xm, mods, mod_row, mod_tokens, w1g, w3g, w2g, fg, layer, *, final, tm=512):
    n = h2.shape[0]
    cnt = tile_counts[:, 0, ROUTE_GROUP:ROUTE_GROUP + MOE_GROUPS].reshape(
        n // tm, tm // (OUT_SEQS * S5_SEG), MOE_GROUPS).sum(axis=1)
    row = lambda w: pl.BlockSpec((tm, w), lambda i, c: (i, 0))
    mod_tiles = mod_tokens // tm
    wspec = pl.BlockSpec((None, MOE_GROUPS, D_MODEL, GROUP_HID), lambda i, c: (layer, 0, 0, 0),
                         pipeline_mode=pl.Buffered(1))
    return pl.pallas_call(
        functools.partial(_moe_kernel, final=final, tm=tm),
        grid_spec=pltpu.PrefetchScalarGridSpec(
            num_scalar_prefetch=1,
            grid=(n // tm,),
            in_specs=[row(D_MODEL), row(LANES), row(D_MODEL),
                      pl.BlockSpec((None, None, 6, D_MODEL), lambda i, c: (layer, mod_row + i // mod_tiles, 0, 0)),
                      wspec, wspec, wspec,
                      pl.BlockSpec((1, D_MODEL), lambda i, c: (0, 0))],
            out_specs=row(D_MODEL),
            scratch_shapes=[pltpu.VMEM((tm, D_MODEL), BF16), pltpu.VMEM((tm, LANES), F32),
                            pltpu.VMEM((tm, D_MODEL), F32), pltpu.VMEM((tm, tm), BF16)]),
        out_shape=jax.ShapeDtypeStruct((n, D_MODEL), F32),
        compiler_params=_cparams("arbitrary"),
        name="moe",
    )(cnt, h2, route, xm, mods, w1g, w3g, w2g, fg)


def kernel(x_prompt, x_sample, cache_a_k, cache_a_v, cache_b_k, cache_b_v, state_ret, state_ssm, c, c_ctx, mod_w, mod_b, norm1_g, norm2_g, w_in, a_qn_g, a_kn_g, b_rel_bias, ret_decay, ret_gn_g, s5_lam_re, s5_lam_im, s5_log_dt, s5_b_re, s5_b_im, s5_c_re, s5_c_im, s5_d, s5_glu_w, w_out, moe_gw, moe_gb, moe_ew, moe_eb, moe_w1, moe_w3, moe_w2, final_norm_g):
    n_ctx = BATCH * SEQ
    n_lat = DEC_BATCH * DEC_SEQ
    lat_seg = DEC_SEQ // S5_SEG

    cond = jnp.zeros((SUBLANES, D_MODEL), F32).at[0].set(c_ctx).at[1:1 + DEC_BATCH].set(c)
    mods = _modulation(cond, mod_w, mod_b).reshape(DEPTH, SUBLANES, 6, D_MODEL)

    rope_tabs = _rope_tables()
    s5_a, s5_bm, s5_cre, s5_cim = _s5_prepare(s5_lam_re, s5_lam_im, s5_log_dt, s5_b_re, s5_b_im,
                                              s5_c_re, s5_c_im)
    cak = cache_a_k.reshape(DEC_BATCH, DEPTH, PAST_LEN, A_KV_HEADS * HEAD_DIM)
    cav = cache_a_v.reshape(DEC_BATCH, DEPTH, PAST_LEN, A_KV_HEADS * HEAD_DIM)
    cbk = cache_b_k.reshape(DEC_BATCH, DEPTH, PAST_LEN, B_HEADS * HEAD_DIM)
    cbv = cache_b_v.reshape(DEC_BATCH, DEPTH, PAST_LEN, B_HEADS * HEAD_DIM)

    xc = x_prompt.reshape(n_ctx, D_MODEL)
    xs = x_sample.reshape(n_lat, D_MODEL)
    w1_all, w3_all, w2_all = _moe_weights(moe_w1, moe_w3, moe_w2)
    eye_h = jnp.eye(C_HEADS, dtype=F32)
    s0_bd = (state_ret[:, :, :, :, :, None, :] * eye_h[None, None, None, :, None, :, None]).reshape(
        DEC_BATCH, DEPTH, 2, C_HEADS * HEAD_DIM, C_HEADS * HEAD_DIM)
    ctx_state = ssm_states = None
    h0_zero = jnp.zeros((2, SUBLANES, 2 * S5_SP), F32)
    w_in_bf = w_in.astype(BF16)
    wo_bf = w_out.astype(BF16)
    glu_bf = s5_glu_w.astype(BF16)
    for l in range(DEPTH):
        final = l == DEPTH - 1
        g1 = norm1_g[l].reshape(1, D_MODEL)
        g2 = norm2_g[l].reshape(1, D_MODEL)
        fg = final_norm_g.reshape(1, D_MODEL)
        qn = jnp.tile(a_qn_g[l], A_HEADS).reshape(1, 256)
        kn = jnp.tile(a_kn_g[l], A_KV_HEADS).reshape(1, 128)
        dec = jnp.broadcast_to(ret_decay[l].reshape(2 * C_HEADS, 1), (2 * C_HEADS, LANES))
        gn = ret_gn_g[l].reshape(1, 256)
        dvec = s5_d[l].reshape(1, GROUP_WIDTH)
        wr = jnp.zeros((D_MODEL, LANES), F32).at[:, :MOE_GROUPS].set(moe_gw[l]).at[
            :, ROUTER_OFF:ROUTER_OFF + MOE_EXPERTS].set(moe_ew[l])
        br = jnp.zeros((1, LANES), F32).at[0, :MOE_GROUPS].set(moe_gb[l]).at[
            0, ROUTER_OFF:ROUTER_OFF + MOE_EXPERTS].set(moe_eb[l])
        wr_hi = wr.astype(BF16)
        wr_lo = (wr - wr_hi.astype(F32)).astype(BF16)
        na_bias = _na_bias(b_rel_bias[l])

        oa, ob, oc, du_tm, ctx_state = _ctx_front(xc, mods, g1, w_in_bf, qn, kn, dec, gn, l, ctx_state)
        od_tm, ssm_states = _s5(du_tm, h0_zero, s5_a, s5_bm, s5_cre, s5_cim, dvec, glu_bf, l,
                                nseg=1, fin_layer=l, fin_layers=DEPTH, prev_fin=ssm_states)
        xm, h2, route, counts = _output_stage(xc, (oa, ob, oc, od_tm), mods, 0, n_ctx, wo_bf, g2,
                                              wr_hi, wr_lo, br, l)
        xc = _moe(h2, route, counts, xm, mods, 0, n_ctx, w1_all, w3_all, w2_all, fg, l, final=final)

        zs, cg, du_tm = _project(xs, mods, 1, DEC_SEQ, g1, w_in_bf, qn, kn, rope_tabs, l, seq_len=DEC_SEQ)
        zs3 = zs.reshape(DEC_BATCH, DEC_SEQ, OFF_CG)
        oa = _lat_attention_a(zs3, cak, cav, l).reshape(n_lat, 256)
        ob = _lat_attention_b(zs, cbk, cbv, na_bias, l)
        oc = _retention(zs3, cg.reshape(DEC_BATCH, DEC_SEQ, 256), dec, gn, s0_bd, l).reshape(n_lat, 256)
        h0 = state_ssm[:, l].reshape(DEC_BATCH, 2, 2 * S5_SP).transpose(1, 0, 2)
        h0_seg = jnp.zeros((2, DEC_BATCH, lat_seg, 2 * S5_SP), F32)
        h0_seg = h0_seg.at[0, :, 0].set(h0[0]).at[1, :, lat_seg - 1].set(h0[1])
        od_tm, _ = _s5(du_tm, h0_seg.reshape(2, SUBLANES, 2 * S5_SP),
                       s5_a, s5_bm, s5_cre, s5_cim, dvec, glu_bf, l, nseg=lat_seg)
        xm, h2, route, counts = _output_stage(xs, (oa, ob, oc, od_tm), mods, 1, DEC_SEQ, wo_bf, g2,
                                              wr_hi, wr_lo, br, l)
        xs = _moe(h2, route, counts, xm, mods, 1, DEC_SEQ, w1_all, w3_all, w2_all, fg, l, final=final)

    new_ak, new_av, new_bk, new_bv, ret_states = ctx_state
    return (xc.reshape(BATCH, SEQ, D_MODEL), xs.reshape(DEC_BATCH, DEC_SEQ, D_MODEL),
            new_ak.reshape(BATCH, DEPTH, SEQ, A_KV_HEADS, HEAD_DIM),
            new_av.reshape(BATCH, DEPTH, SEQ, A_KV_HEADS, HEAD_DIM),
            new_bk.reshape(BATCH, DEPTH, SEQ, B_HEADS, HEAD_DIM),
            new_bv.reshape(BATCH, DEPTH, SEQ, B_HEADS, HEAD_DIM),
            ret_states,
            ssm_states.reshape(BATCH, DEPTH, 2, 2, S5_GROUPS, S5_STATE))
```

```python
import functools
import math

import numpy as np
import jax
import jax.numpy as jnp
from jax import lax
from jax.experimental import pallas as pl
from jax.experimental.pallas import tpu as pltpu

F32 = jnp.float32
BF16 = jnp.bfloat16

D_MODEL = 1024
BATCH = 32
SEQ = 256
DEPTH = 2
DEC_BATCH = 2
DEC_SEQ = 1024
PAST_LEN = 256
GRID_W = 64
HEAD_DIM = 64
GROUP_WIDTH = 256
A_HEADS = 4
A_KV_HEADS = 2
B_HEADS = 4
NA_ROWS = 8
NA_COLS = 16
C_HEADS = 4
S5_CH = 16
S5_GROUPS = 16
S5_STATE = 64
MOE_GROUPS = 4
MOE_PER_GROUP = 8
MOE_EXPERTS = 32
MOE_HIDDEN = 128
ROPE_THETA = 10000.0
EPS = 1e-6
IN_WIDTH = 2560
Q_SCALE = HEAD_DIM ** -0.5

OFF_AQ, OFF_AK, OFF_AV = 0, 256, 384
OFF_BQ, OFF_BK, OFF_BV = 512, 768, 1024
OFF_CQ, OFF_CK, OFF_CV, OFF_CG = 1280, 1536, 1792, 2048
OFF_DU = 2304

LANES = 128
SUBLANES = 8
BF16_ROWS = 16
S5_SP = S5_GROUPS * S5_STATE
S5_SEG = 256
S5_CHUNK = 256
ROUTER_OFF = 4
NEG_BIG = -1e30
VMEM_LIMIT = 56 * 1024 * 1024


def _cparams(*sem):
    return pltpu.CompilerParams(dimension_semantics=sem, vmem_limit_bytes=VMEM_LIMIT)


def _mod_spec(layer, first_row, tiles_per_row, grid_rank):
    if grid_rank == 1:
        return pl.BlockSpec((None, None, 6, D_MODEL), lambda i: (layer, first_row + i // tiles_per_row, 0, 0))
    return pl.BlockSpec((None, None, 6, D_MODEL), lambda i, g: (layer, first_row + i // tiles_per_row, 0, 0))


def _bdot(a, b):
    return jnp.dot(a.astype(BF16), b.astype(BF16), preferred_element_type=F32)


def _bdot_nt(a, b):
    return lax.dot_general(a.astype(BF16), b.astype(BF16), (((1,), (1,)), ((), ())),
                           preferred_element_type=F32)


def _bdot_tn(a, b):
    return lax.dot_general(a.astype(BF16), b.astype(BF16), (((0,), (0,)), ((), ())),
                           preferred_element_type=F32)


def _split(a):
    hi = a.astype(BF16)
    lo = (a - hi.astype(F32)).astype(BF16)
    return hi, lo


def _dot_hilo_lhs(a, b_bf16):
    hi, lo = _split(a)
    return (jnp.dot(hi, b_bf16, preferred_element_type=F32)
            + jnp.dot(lo, b_bf16, preferred_element_type=F32))


def _rms_rows(x):
    return x * lax.rsqrt(jnp.mean(x * x, axis=-1, keepdims=True) + EPS)


def _mod_kernel(cond_ref, w_ref, b_ref, o_ref):
    o_ref[...] = _bdot(jax.nn.silu(cond_ref[...]), w_ref[...]) + b_ref[...]


def _modulation(cond, mod_w, mod_b):
    tn = 1536
    return pl.pallas_call(
        _mod_kernel,
        grid=(DEPTH, 6 * D_MODEL // tn),
        in_specs=[pl.BlockSpec((SUBLANES, D_MODEL), lambda l, j: (0, 0)),
                  pl.BlockSpec((None, D_MODEL, tn), lambda l, j: (l, 0, j)),
                  pl.BlockSpec((None, 1, tn), lambda l, j: (l, 0, j))],
        out_specs=pl.BlockSpec((None, SUBLANES, tn), lambda l, j: (l, 0, j)),
        out_shape=jax.ShapeDtypeStruct((DEPTH, SUBLANES, 6 * D_MODEL), F32),
        compiler_params=_cparams("arbitrary", "arbitrary"),
        name="modulation",
    )(cond, mod_w, mod_b.reshape(DEPTH, 1, 6 * D_MODEL))


def _group_mean_matrix(w):
    ri = lax.broadcasted_iota(jnp.int32, (w, w), 0) // HEAD_DIM
    ci = lax.broadcasted_iota(jnp.int32, (w, w), 1) // HEAD_DIM
    return jnp.where(ri == ci, 1.0 / HEAD_DIM, 0.0).astype(BF16)


def _head_norm(t, g):
    ms = _dot_hilo_lhs(t * t, _group_mean_matrix(t.shape[1]))
    return t * lax.rsqrt(ms + EPS) * g


def _rope(t, cos, sa, sb):
    return (t * cos + pltpu.roll(t, LANES - 16, 1) * sa + pltpu.roll(t, 16, 1) * sb)


def _store_layer_slot(ref, slot, value):
    for s in range(ref.shape[0]):
        ref[s] = value if s == slot else jnp.zeros_like(value)


def _layer_slot_block(layer, first_call, tail):
    if first_call:
        return (None, DEPTH) + tail, (0,) * (1 + len(tail)), layer
    return (None, 1) + tail, (layer,) + (0,) * len(tail), 0


def _proj_kernel(x_ref, mod_ref, g1_ref, w_ref, qn_ref, kn_ref, cos_ref, sa_ref, sb_ref, z_ref, cg_ref, du_ref):
    h = _rms_rows(x_ref[...]) * g1_ref[...] * (1.0 + mod_ref[1:2, :]) + mod_ref[0:1, :]
    z = jnp.dot(h.astype(BF16), w_ref[...], preferred_element_type=F32)
    aq = _head_norm(z[:, OFF_AQ:OFF_AK], qn_ref[...])
    ak = _head_norm(z[:, OFF_AK:OFF_AV], kn_ref[...])
    for j in range(3):
        t = aq[:, j * LANES:(j + 1) * LANES] if j < 2 else ak
        sl = slice(0, LANES) if j == 2 else slice(j * LANES, (j + 1) * LANES)
        t = _rope(t, cos_ref[:, sl], sa_ref[:, sl], sb_ref[:, sl])
        z_ref[:, j * LANES:(j + 1) * LANES] = t.astype(BF16)
    z_ref[:, OFF_AV:OFF_CK] = z[:, OFF_AV:OFF_CK].astype(BF16)
    z_ref[:, OFF_CK:OFF_CV] = (z[:, OFF_CK:OFF_CV] * Q_SCALE).astype(BF16)
    z_ref[:, OFF_CV:OFF_CG] = z[:, OFF_CV:OFF_CG].astype(BF16)
    cg_ref[...] = z[:, OFF_CG:OFF_DU]
    du_ref[...] = z[:, OFF_DU:]


def _du_spec(grid_rank):
    if grid_rank == 1:
        return pl.BlockSpec((None, S5_SEG, GROUP_WIDTH), lambda i: (i // SUBLANES, 0, i % SUBLANES))
    return pl.BlockSpec((None, S5_SEG, GROUP_WIDTH), lambda i, g: (i // SUBLANES, 0, i % SUBLANES))


def _project(x, mods, mod_row, mod_tokens, g1, w_in_bf, qn, kn, rope_tabs, layer, *, seq_len):
    tm = S5_SEG
    n = x.shape[0]
    tps = seq_len // tm
    return pl.pallas_call(
        _proj_kernel,
        grid=(n // tm,),
        in_specs=[pl.BlockSpec((tm, D_MODEL), lambda i: (i, 0)),
                  _mod_spec(layer, mod_row, mod_tokens // tm, 1),
                  pl.BlockSpec((1, D_MODEL), lambda i: (0, 0)),
                  pl.BlockSpec((None, D_MODEL, IN_WIDTH), lambda i: (layer, 0, 0)),
                  pl.BlockSpec((1, 256), lambda i: (0, 0)),
                  pl.BlockSpec((1, 128), lambda i: (0, 0))]
                 + [pl.BlockSpec((tm, 256), lambda i: (i % tps, 0))] * 3,
        out_specs=[pl.BlockSpec((tm, OFF_CG), lambda i: (i, 0)),
                   pl.BlockSpec((tm, GROUP_WIDTH), lambda i: (i, 0)), _du_spec(1)],
        out_shape=[jax.ShapeDtypeStruct((n, OFF_CG), BF16),
                   jax.ShapeDtypeStruct((n, GROUP_WIDTH), F32),
                   jax.ShapeDtypeStruct((n // (tm * SUBLANES), S5_SEG, SUBLANES * GROUP_WIDTH), F32)],
        compiler_params=_cparams("parallel"),
        name="project",
    )(x, mods, g1, w_in_bf, qn, kn, *rope_tabs)


def _rope_tables():
    t = jnp.arange(DEC_SEQ)
    row = (t // GRID_W).astype(F32)
    col = (t % GRID_W).astype(F32)
    nf = HEAD_DIM // 4
    inv = ROPE_THETA ** (-jnp.arange(nf, dtype=F32) / nf)
    ang_r = row[:, None] * inv[None, :]
    ang_c = col[:, None] * inv[None, :]
    zeros = jnp.zeros_like(ang_r)
    cos = jnp.concatenate([jnp.cos(ang_r), jnp.cos(ang_r), jnp.cos(ang_c), jnp.cos(ang_c)], axis=-1)
    sa = jnp.concatenate([-jnp.sin(ang_r), zeros, -jnp.sin(ang_c), zeros], axis=-1)
    sb = jnp.concatenate([zeros, jnp.sin(ang_r), zeros, jnp.sin(ang_c)], axis=-1)
    return tuple(jnp.tile(a, (1, 4)) for a in (cos, sa, sb))


N_HEADS = 4


def _lane_head(width):
    return lax.broadcasted_iota(jnp.int32, (1, width), 1) // HEAD_DIM


def _stack_heads(q):
    head = _lane_head(q.shape[1])
    return jnp.concatenate([jnp.where(head == h, q, 0.0) for h in range(N_HEADS)], axis=0).astype(BF16)


def _stack_heads_gqa(q):
    lo = lax.broadcasted_iota(jnp.int32, (1, LANES), 1) < HEAD_DIM
    q = q.astype(F32)
    q01, q23 = q[:, :LANES], q[:, LANES:]
    blocks = [jnp.where(lo, q01, 0.0), jnp.where(lo, pltpu.roll(q01, HEAD_DIM, 1), 0.0),
              jnp.where(lo, 0.0, pltpu.roll(q23, HEAD_DIM, 1)), jnp.where(lo, 0.0, q23)]
    return jnp.concatenate(blocks, axis=0).astype(BF16)


def _spread_kv_gqa(v):
    lo = lax.broadcasted_iota(jnp.int32, (1, LANES), 1) < HEAD_DIM
    v = v.astype(F32)
    vr = pltpu.roll(v, HEAD_DIM, 1)
    return jnp.concatenate([jnp.where(lo, v, vr), jnp.where(lo, vr, v)], axis=1)


def _mha(qs, blocks, tq):
    scores = []
    for k, _, bias in blocks:
        s = _bdot_nt(qs, k)
        scores.append(s if bias is None else s + bias)
    m = functools.reduce(jnp.maximum, [jnp.max(s, axis=-1, keepdims=True) for s in scores])
    es = [jnp.exp(s - m) for s in scores]
    denom = functools.reduce(jnp.add, [jnp.sum(e, axis=-1, keepdims=True) for e in es])
    ps = [e.astype(BF16) for e in es]
    head = _lane_head(N_HEADS * HEAD_DIM)
    vals = [v.astype(BF16) for _, v, _ in blocks]
    o = None
    dall = None
    for h in range(N_HEADS):
        rows = slice(h * tq, (h + 1) * tq)
        for p, v in zip(ps, vals):
            t = jnp.dot(p[rows], jnp.where(head == h, v, jnp.zeros_like(v)), preferred_element_type=F32)
            o = t if o is None else o + t
        d = jnp.where(head == h, denom[rows], 0.0)
        dall = d if dall is None else dall + d
    return (o / dall).astype(BF16)


def _lat_attn_a_kernel(q_ref, kn_ref, vn_ref, kc_ref, vc_ref, o_ref):
    for b in range(DEC_BATCH):
        o_ref[b] = _mha(_stack_heads_gqa(q_ref[b] * Q_SCALE),
                        [(kc_ref[b], _spread_kv_gqa(vc_ref[b]), None),
                         (kn_ref[b], _spread_kv_gqa(vn_ref[b]), None)], q_ref.shape[1])


def _lat_attention_a(z, cache_k, cache_v, layer, tq=256):
    cache_spec = pl.BlockSpec((DEC_BATCH, None, PAST_LEN, 128), lambda j: (0, layer, 0, 0))
    return pl.pallas_call(
        _lat_attn_a_kernel,
        grid=(DEC_SEQ // tq,),
        in_specs=[pl.BlockSpec((DEC_BATCH, tq, 256), lambda j: (0, j, OFF_AQ // 256)),
                  pl.BlockSpec((DEC_BATCH, DEC_SEQ, 128), lambda j: (0, 0, OFF_AK // 128)),
                  pl.BlockSpec((DEC_BATCH, DEC_SEQ, 128), lambda j: (0, 0, OFF_AV // 128)),
                  cache_spec, cache_spec],
        out_specs=pl.BlockSpec((DEC_BATCH, tq, 256), lambda j: (0, j, 0)),
        out_shape=jax.ShapeDtypeStruct((DEC_BATCH, DEC_SEQ, 256), BF16),
        compiler_params=_cparams("parallel"),
        name="lat_attention_a",
    )(z, z, z, cache_k, cache_v)


NA_KEYS = NA_ROWS * GRID_W


NA_PAIRS = 2 * NA_ROWS - 2


NA_STEP_ROWS = 2


def _na_kernel(q_ref, k_ref, v_ref, kc_ref, vc_ref, bias_ref, o_ref):
    rows = DEC_SEQ // GRID_W
    for b in range(q_ref.shape[0]):
        outs = []
        for rr in range(NA_STEP_ROWS):
            r = pl.program_id(0) * NA_STEP_ROWS + rr
            row_start = jnp.clip(r - NA_ROWS // 2, 0, rows - NA_ROWS)
            start = pl.multiple_of(row_start * GRID_W, GRID_W)
            rel0 = row_start - r + NA_ROWS - 1
            kl = k_ref[b, pl.ds(start, NA_KEYS), :]
            vl = v_ref[b, pl.ds(start, NA_KEYS), :]
            bias = jnp.concatenate(
                [jnp.concatenate([bias_ref[h, rel0 + 2 * jp] for jp in range(NA_ROWS // 2)], axis=1)
                 for h in range(B_HEADS)], axis=0)
            qrows = slice(rr * GRID_W, (rr + 1) * GRID_W)
            outs.append(_mha(_stack_heads(q_ref[b, qrows, :] * Q_SCALE),
                             [(kl, vl, bias), (kc_ref[b], vc_ref[b], None)], GRID_W))
        o_ref[b] = jnp.concatenate(outs, axis=0)


def _na_bias(rel_bias):
    nrel = 2 * NA_COLS - 1
    period = 2 * GRID_W
    b = rel_bias.astype(F32)
    ext = jnp.concatenate([b[..., NA_COLS - 1:],
                           jnp.zeros(b.shape[:-1] + (period - nrel,), F32),
                           b[..., :NA_COLS - 1]], axis=-1)
    flat = jnp.tile(ext, (1, 1, GRID_W))[..., :GRID_W * (period - 1)]
    toe = flat.reshape(b.shape[:-1] + (GRID_W, period - 1))[..., :GRID_W]
    col_start = np.clip(np.arange(GRID_W) - NA_COLS // 2, 0, GRID_W - NA_COLS)
    kc = np.arange(GRID_W)
    inside = (kc[None, :] >= col_start[:, None]) & (kc[None, :] < col_start[:, None] + NA_COLS)
    toe = jnp.where(jnp.asarray(inside), toe, NEG_BIG)
    return jnp.concatenate([toe[:, :-1], toe[:, 1:]], axis=-1)


def _lat_attention_b(z, cache_k, cache_v, bias, layer):
    tq = NA_STEP_ROWS * GRID_W
    cache_spec = pl.BlockSpec((DEC_BATCH, None, PAST_LEN, 256), lambda r: (0, layer, 0, 0))
    return pl.pallas_call(
        _na_kernel,
        grid=(DEC_SEQ // tq,),
        in_specs=[pl.BlockSpec((DEC_BATCH, tq, 256), lambda r: (0, r, OFF_BQ // 256)),
                  pl.BlockSpec((DEC_BATCH, DEC_SEQ, 256), lambda r: (0, 0, OFF_BK // 256)),
                  pl.BlockSpec((DEC_BATCH, DEC_SEQ, 256), lambda r: (0, 0, OFF_BV // 256)),
                  cache_spec, cache_spec,
                  pl.BlockSpec((B_HEADS, NA_PAIRS, GRID_W, 2 * GRID_W), lambda r: (0, 0, 0, 0))],
        out_specs=pl.BlockSpec((DEC_BATCH, tq, 256), lambda r: (0, r, 0)),
        out_shape=jax.ShapeDtypeStruct((DEC_BATCH, DEC_SEQ, 256), BF16),
        compiler_params=_cparams("parallel"),
        name="lat_attention_b",
    )(z, z, z, cache_k, cache_v, bias)


def _retention_core(q, k, v, g, dec_ref, gn_ref, dec_scr, *, seq_len, i0, decay_fill, s0_ref=None,
                    want_state=False):
    tq = q.shape[0]
    head = _lane_head(C_HEADS * HEAD_DIM)
    lg = jax.nn.log_sigmoid(dec_ref[...])

    def per_lane(row0):
        out = jnp.zeros((1, C_HEADS * HEAD_DIM), F32)
        for h in range(C_HEADS):
            out = jnp.where(head == h, lg[row0 + h:row0 + h + 1, 0:1], out)
        return out

    lgf_l, lgb_l = per_lane(0), per_lane(C_HEADS)
    qi = (i0 + lax.broadcasted_iota(jnp.int32, (tq, 1), 0)).astype(F32)

    def fill_decay():
        kj = lax.broadcasted_iota(jnp.int32, (1, seq_len), 1).astype(F32)
        diff = qi - kj
        for h in range(C_HEADS):
            lgf = lg[h:h + 1, 0:1]
            lgb = lg[C_HEADS + h:C_HEADS + h + 1, 0:1]
            dec_scr[h * tq:(h + 1) * tq, :] = (
                jnp.where(diff >= 0, jnp.exp(lgf * jnp.maximum(diff, 0.0)), 0.0)
                + jnp.where(diff <= 0, jnp.exp(lgb * jnp.maximum(-diff, 0.0)), 0.0))

    if decay_fill == "first_step":
        pl.when(pl.program_id(0) == 0)(fill_decay)
    elif decay_fill == "every_step":
        fill_decay()
    else:
        assert decay_fill == "filled"

    v = v.astype(BF16)
    sc = (_bdot_nt(_stack_heads(q), k) * dec_scr[...]).astype(BF16)
    o = None
    for h in range(C_HEADS):
        t = jnp.dot(sc[h * tq:(h + 1) * tq], jnp.where(head == h, v, jnp.zeros_like(v)),
                    preferred_element_type=F32)
        o = t if o is None else o + t
    if s0_ref is not None:
        o = (o + _bdot(q, s0_ref[0]) * jnp.exp(lgf_l * (qi + 1.0))
             + _bdot(q, s0_ref[1]) * jnp.exp(lgb_l * (seq_len - qi)))
    gm = _group_mean_matrix(C_HEADS * HEAD_DIM)
    dlt = o - _dot_hilo_lhs(o, gm)
    var = _dot_hilo_lhs(dlt * dlt, gm)
    out = (dlt * lax.rsqrt(var + EPS) * gn_ref[...] * jax.nn.silu(g)).astype(BF16)
    if not want_state:
        return out, None
    kpos = lax.broadcasted_iota(jnp.int32, (seq_len, 1), 0).astype(F32)
    sf = _bdot_tn(k * jnp.exp(lgf_l * (seq_len - 1.0 - kpos)), v)
    sb = _bdot_tn(k * jnp.exp(lgb_l * kpos), v)
    return out, (sf, sb)


def _store_retention_state(st_ref, slot, state):
    for s in range(st_ref.shape[0]):
        for d in range(2):
            for h in range(C_HEADS):
                sl = slice(h * HEAD_DIM, (h + 1) * HEAD_DIM)
                st_ref[s, d, h] = state[d][sl, sl] if s == slot else jnp.zeros((HEAD_DIM, HEAD_DIM), F32)


def _retention_kernel(q_ref, g_ref, k_ref, v_ref, dec_ref, gn_ref, s0_ref, o_ref, dec_scr, *, seq_len, tq):
    for b in range(q_ref.shape[0]):
        o_ref[b], _ = _retention_core(q_ref[b], k_ref[b], v_ref[b], g_ref[b], dec_ref, gn_ref, dec_scr,
                                      seq_len=seq_len, i0=pl.program_id(0) * tq,
                                      decay_fill="every_step" if b == 0 else "filled", s0_ref=s0_ref.at[b])


def _retention(z, cg, dec, gn, s0, layer, *, tq=256):
    nb, seq_len = z.shape[:2]
    return pl.pallas_call(
        functools.partial(_retention_kernel, seq_len=seq_len, tq=tq),
        grid=(seq_len // tq,),
        in_specs=[pl.BlockSpec((nb, tq, 256), lambda j: (0, j, OFF_CQ // 256)),
                  pl.BlockSpec((nb, tq, 256), lambda j: (0, j, 0)),
                  pl.BlockSpec((nb, seq_len, 256), lambda j: (0, 0, OFF_CK // 256)),
                  pl.BlockSpec((nb, seq_len, 256), lambda j: (0, 0, OFF_CV // 256)),
                  pl.BlockSpec((SUBLANES, LANES), lambda j: (0, 0)),
                  pl.BlockSpec((1, 256), lambda j: (0, 0)),
                  pl.BlockSpec((nb, None, 2, 256, 256), lambda j: (0, layer, 0, 0, 0))],
        out_specs=pl.BlockSpec((nb, tq, 256), lambda j: (0, j, 0)),
        out_shape=jax.ShapeDtypeStruct((nb, seq_len, 256), BF16),
        scratch_shapes=[pltpu.VMEM((C_HEADS * tq, seq_len), F32)],
        compiler_params=_cparams("arbitrary"),
        name="retention",
    )(z, cg, z, z, dec, gn, s0)


CTX_SEQS = 4


def _ctx_front_kernel(x_ref, mod_ref, g1_ref, w_ref, qn_ref, kn_ref, dec_ref, gn_ref, *rest, n_alias, slot):
    (oa_ref, ob_ref, oc_ref, du_ref, ak_ref, av_ref, bk_ref, bv_ref, st_ref, dec_scr) = rest[n_alias:]
    tq = x_ref.shape[0] // CTX_SEQS
    h = _rms_rows(x_ref[...]) * g1_ref[...] * (1.0 + mod_ref[1:2, :]) + mod_ref[0:1, :]
    zz = jnp.dot(h.astype(BF16), w_ref[...], preferred_element_type=F32)
    for s in range(CTX_SEQS):
        rows = slice(s * tq, (s + 1) * tq)
        z = zz[rows, :]
        aq = _head_norm(z[:, OFF_AQ:OFF_AK], qn_ref[...])
        ak = _head_norm(z[:, OFF_AK:OFF_AV], kn_ref[...])
        av, bq, bk, bv = (z[:, OFF_AV:OFF_BQ], z[:, OFF_BQ:OFF_BK], z[:, OFF_BK:OFF_BV], z[:, OFF_BV:OFF_CQ])
        oa_ref[rows, :] = _mha(_stack_heads_gqa(aq * Q_SCALE), [(ak, _spread_kv_gqa(av), None)], tq)
        ob_ref[rows, :] = _mha(_stack_heads(bq * Q_SCALE), [(bk, bv, None)], tq)
        oc_ref[rows, :], state = _retention_core(
            z[:, OFF_CQ:OFF_CK], z[:, OFF_CK:OFF_CV] * Q_SCALE, z[:, OFF_CV:OFF_CG], z[:, OFF_CG:OFF_DU],
            dec_ref, gn_ref, dec_scr, seq_len=tq, i0=0, decay_fill="first_step" if s == 0 else "filled",
            want_state=True)
        du_ref[:, s * GROUP_WIDTH:(s + 1) * GROUP_WIDTH] = z[:, OFF_DU:]
        _store_layer_slot(ak_ref.at[s], slot, ak)
        _store_layer_slot(av_ref.at[s], slot, av)
        _store_layer_slot(bk_ref.at[s], slot, bk)
        _store_layer_slot(bv_ref.at[s], slot, bv)
        _store_retention_state(st_ref.at[s], slot, state)


def _ctx_front(x, mods, g1, w_in_bf, qn, kn, dec, gn, layer, prev):
    assert SEQ == S5_SEG
    tm = CTX_SEQS * SEQ
    n = x.shape[0]
    nb = n // SEQ
    steps = n // tm
    per_blk = SUBLANES // CTX_SEQS
    const = lambda *shape: pl.BlockSpec(shape, lambda i: (0,) * len(shape))
    row = lambda w: pl.BlockSpec((tm, w), lambda i: (i, 0))
    in_specs = [row(D_MODEL), _mod_spec(layer, 0, steps, 1), const(1, D_MODEL),
                pl.BlockSpec((None, D_MODEL, IN_WIDTH), lambda i: (layer, 0, 0)),
                const(1, 256), const(1, 128), const(SUBLANES, LANES), const(1, 256)]
    args = [x, mods, g1, w_in_bf, qn, kn, dec, gn]
    out_specs = [row(256), row(256), row(256),
                 pl.BlockSpec((None, S5_SEG, CTX_SEQS * GROUP_WIDTH), lambda i: (i // per_blk, 0, i % per_blk))]
    out_shape = [jax.ShapeDtypeStruct((n, 256), BF16)] * 3 + [
        jax.ShapeDtypeStruct((nb // SUBLANES, S5_SEG, SUBLANES * GROUP_WIDTH), F32)]
    first = prev is None
    slot = 0
    for tail in ((SEQ, 128), (SEQ, 128), (SEQ, 256), (SEQ, 256), (2, C_HEADS, HEAD_DIM, HEAD_DIM)):
        blk, idx, slot = _layer_slot_block(layer, first, tail)
        out_specs.append(pl.BlockSpec((CTX_SEQS,) + blk[1:], lambda i, idx=idx: (i,) + idx))
        out_shape.append(jax.ShapeDtypeStruct((nb, DEPTH) + tail, F32))
    aliases = {}
    if not first:
        for k, arr in enumerate(prev):
            aliases[len(args)] = 4 + k
            in_specs.append(pl.BlockSpec(memory_space=pl.ANY))
            args.append(arr)
    outs = pl.pallas_call(
        functools.partial(_ctx_front_kernel, n_alias=len(aliases), slot=slot),
        grid=(steps,),
        in_specs=in_specs,
        out_specs=out_specs,
        out_shape=out_shape,
        scratch_shapes=[pltpu.VMEM((C_HEADS * SEQ, SEQ), F32)],
        input_output_aliases=aliases,
        compiler_params=_cparams("arbitrary"),
        name="ctx_front",
    )(*args)
    return outs[0], outs[1], outs[2], outs[3], tuple(outs[4:])


def _s5_prep_kernel(lre_ref, lim_ref, ldt_ref, bre_ref, bim_ref, cre_ref, cim_ref,
                    a_ref, bm_ref, cro_ref, cio_ref, bm_scr, cr_scr, ci_scr):
    lre = lre_ref[...]
    lim = lim_ref[...]
    dt = jnp.exp(ldt_ref[...])
    mag = jnp.exp(lre * dt)
    a_re = mag * jnp.cos(lim * dt)
    a_im = mag * jnp.sin(lim * dt)
    den = lre * lre + lim * lim
    r_re = ((a_re - 1.0) * lre + a_im * lim) / den
    r_im = (a_im * lre - (a_re - 1.0) * lim) / den
    bm_scr[...] = jnp.zeros_like(bm_scr)
    cr_scr[...] = jnp.zeros_like(cr_scr)
    ci_scr[...] = jnp.zeros_like(ci_scr)
    for g in range(S5_GROUPS):
        rows = slice(g * S5_CH, (g + 1) * S5_CH)
        cols = slice(g * S5_STATE, (g + 1) * S5_STATE)
        a_ref[0:1, cols] = a_re[g:g + 1, :]
        a_ref[1:2, cols] = a_im[g:g + 1, :]
        rr, ri = r_re[g:g + 1, :], r_im[g:g + 1, :]
        br, bi = bre_ref[g], bim_ref[g]
        bm_scr[rows, cols] = rr * br - ri * bi
        bm_scr[rows, S5_SP + g * S5_STATE:S5_SP + (g + 1) * S5_STATE] = rr * bi + ri * br
        cr_scr[cols, rows] = cre_ref[g]
        ci_scr[cols, rows] = cim_ref[g]
    bm_ref[...] = bm_scr[...].astype(BF16)
    cro_ref[...] = cr_scr[...].astype(BF16)
    cio_ref[...] = ci_scr[...].astype(BF16)


def _s5_prepare(lam_re, lam_im, log_dt, b_re, b_im, c_re, c_im):
    gp = (S5_GROUPS, S5_STATE)
    ldt = jnp.broadcast_to(log_dt[..., None], (DEPTH, 2) + gp)
    bt = [jnp.swapaxes(t, -1, -2) for t in (b_re, b_im)]
    ct = [jnp.swapaxes(t, -1, -2) for t in (c_re, c_im)]

    def spec(*tail):
        return pl.BlockSpec((None, None) + tail, lambda l, d: (l, d) + (0,) * len(tail))

    return pl.pallas_call(
        _s5_prep_kernel,
        grid=(DEPTH, 2),
        in_specs=[spec(*gp)] * 3 + [spec(S5_GROUPS, S5_CH, S5_STATE)] * 2 + [spec(S5_GROUPS, S5_STATE, S5_CH)] * 2,
        out_specs=[spec(2, S5_SP), spec(GROUP_WIDTH, 2 * S5_SP), spec(S5_SP, GROUP_WIDTH), spec(S5_SP, GROUP_WIDTH)],
        out_shape=[jax.ShapeDtypeStruct((DEPTH, 2, 2, S5_SP), F32),
                   jax.ShapeDtypeStruct((DEPTH, 2, GROUP_WIDTH, 2 * S5_SP), BF16),
                   jax.ShapeDtypeStruct((DEPTH, 2, S5_SP, GROUP_WIDTH), BF16),
                   jax.ShapeDtypeStruct((DEPTH, 2, S5_SP, GROUP_WIDTH), BF16)],
        scratch_shapes=[pltpu.VMEM((GROUP_WIDTH, 2 * S5_SP), F32), pltpu.VMEM((S5_SP, GROUP_WIDTH), F32),
                        pltpu.VMEM((S5_SP, GROUP_WIDTH), F32)],
        compiler_params=_cparams("parallel", "parallel"),
        name="s5_prepare",
    )(lam_re, lam_im, ldt, bt[0], bt[1], ct[0], ct[1])


def _cmul(ar, ai, br, bi):
    return ar * br - ai * bi, ar * bi + ai * br


def _s5_kernel(u_ref, h0_ref, a_ref, bm_ref, cre_ref, cim_ref, dvec_ref, glu_ref, *rest, nseg, slot):
    od_ref, fin_ref, x_scr, s_scr, y_scr = rest[-5:]
    steps = S5_SEG
    rows = steps * SUBLANES
    chunk = S5_CHUNK
    chunk_steps = chunk // SUBLANES
    nchunk = rows // chunk
    seg = lax.broadcasted_iota(jnp.int32, (SUBLANES, S5_SP), 0) % nseg

    for d in range(2):
        ar = jnp.broadcast_to(a_ref[d, 0:1, :], (SUBLANES, S5_SP))
        ai = jnp.broadcast_to(a_ref[d, 1:2, :], (SUBLANES, S5_SP))

        def row0(k):
            c = k if d == 0 else nchunk - 1 - k
            return c * chunk if isinstance(c, int) else pl.multiple_of(c * chunk, chunk)

        def input_part(k, buf):
            x_scr[buf] = jnp.dot(u_ref[pl.ds(row0(k), chunk), :].astype(BF16), bm_ref[d],
                                 preferred_element_type=F32)

        def scan_part(buf, carry, store):
            sr, si = carry
            for t in range(chunk_steps):
                r = (t if d == 0 else chunk_steps - 1 - t) * SUBLANES
                pr, pi = _cmul(ar, ai, sr, si)
                sr = pr + x_scr[buf, r:r + SUBLANES, 0:S5_SP]
                si = pi + x_scr[buf, r:r + SUBLANES, S5_SP:]
                if store:
                    s_scr[buf, r:r + SUBLANES, 0:S5_SP] = sr
                    s_scr[buf, r:r + SUBLANES, S5_SP:] = si
            return sr, si

        def output_part(k, buf):
            y = _bdot(s_scr[buf, :, 0:S5_SP], cre_ref[d]) - _bdot(s_scr[buf, :, S5_SP:], cim_ref[d])
            rows_k = pl.ds(row0(k), chunk)
            if d == 0:
                y_scr[rows_k, :] = y
            else:
                zz = jax.nn.gelu(y_scr[rows_k, :] + y + dvec_ref[...] * u_ref[rows_k, :])
                od_ref[rows_k, :] = (zz * jax.nn.sigmoid(_bdot(zz, glu_ref[...]))).astype(BF16)

        def half(k, buf, carry, store, nxt=True, prev=True):
            if nxt:
                input_part(k + 1, 1 - buf)
            carry = scan_part(buf, carry, store)
            if store and prev:
                output_part(k - 1, 1 - buf)
            return carry

        def run_pass(carry, store):
            input_part(0, 0)
            carry = half(0, 0, carry, store, prev=False)
            carry = half(1, 1, carry, store)

            def pair(j, c):
                c = half(2 * j, 0, c, store)
                return half(2 * j + 1, 1, c, store)
            carry = lax.fori_loop(1, nchunk // 2 - 1, pair, carry)
            carry = half(nchunk - 2, 0, carry, store)
            carry = half(nchunk - 1, 1, carry, store, nxt=False)
            if store:
                output_part(nchunk - 1, 1)
            return carry

        init = (h0_ref[d, :, 0:S5_SP], h0_ref[d, :, S5_SP:])
        if nseg > 1:
            zero = jnp.zeros((SUBLANES, S5_SP), F32)
            fr, fi = run_pass((zero, zero), store=False)
            pr, pi = ar, ai
            for _ in range(int(math.log2(steps))):
                pr, pi = _cmul(pr, pi, pr, pi)
            cr, ci = init
            shift = 1 if d == 0 else SUBLANES - 1
            order = range(1, nseg) if d == 0 else range(nseg - 2, -1, -1)
            for s in order:
                ncr, nci = pltpu.roll(cr, shift, 0), pltpu.roll(ci, shift, 0)
                nfr, nfi = pltpu.roll(fr, shift, 0), pltpu.roll(fi, shift, 0)
                qr, qi = _cmul(pr, pi, ncr, nci)
                cr = jnp.where(seg == s, qr + nfr, cr)
                ci = jnp.where(seg == s, qi + nfi, ci)
            init = (cr, ci)
        sr, si = run_pass(init, store=True)
        for s in range(fin_ref.shape[1] // (4 * S5_SP)):
            base = (4 * s + 2 * d) * S5_SP
            fin_ref[:, base:base + S5_SP] = sr if s == slot else jnp.zeros_like(sr)
            fin_ref[:, base + S5_SP:base + 2 * S5_SP] = si if s == slot else jnp.zeros_like(si)


def _s5(du_tm, h0, a, bmat, cre, cim, dvec, glu_bf, layer, *, nseg, fin_layer=0, fin_layers=1,
        prev_fin=None):
    nblk = du_tm.shape[0]
    rows = S5_SEG * SUBLANES
    fin_w = 4 * S5_SP
    in_specs = [pl.BlockSpec((None, rows, GROUP_WIDTH), lambda i: (i, 0, 0)),
                pl.BlockSpec((2, SUBLANES, 2 * S5_SP), lambda i: (0, 0, 0)),
                pl.BlockSpec((None, 2, 2, S5_SP), lambda i: (layer, 0, 0, 0)),
                pl.BlockSpec((None, 2, GROUP_WIDTH, 2 * S5_SP), lambda i: (layer, 0, 0, 0)),
                pl.BlockSpec((None, 2, S5_SP, GROUP_WIDTH), lambda i: (layer, 0, 0, 0)),
                pl.BlockSpec((None, 2, S5_SP, GROUP_WIDTH), lambda i: (layer, 0, 0, 0)),
                pl.BlockSpec((1, GROUP_WIDTH), lambda i: (0, 0)),
                pl.BlockSpec((None, GROUP_WIDTH, GROUP_WIDTH), lambda i: (layer, 0, 0))]
    args = [du_tm.reshape(nblk, rows, GROUP_WIDTH), h0, a, bmat, cre, cim, dvec, glu_bf]
    aliases = {}
    if prev_fin is not None:
        aliases[len(args)] = 1
        in_specs.append(pl.BlockSpec(memory_space=pl.ANY))
        args.append(prev_fin)
        fin_spec, slot = pl.BlockSpec((SUBLANES, fin_w), lambda i: (i, fin_layer)), 0
    else:
        fin_spec, slot = pl.BlockSpec((SUBLANES, fin_layers * fin_w), lambda i: (i, 0)), fin_layer
    od, fin = pl.pallas_call(
        functools.partial(_s5_kernel, nseg=nseg, slot=slot),
        grid=(nblk,),
        in_specs=in_specs,
        out_specs=[pl.BlockSpec((None, rows, GROUP_WIDTH), lambda i: (i, 0, 0)), fin_spec],
        out_shape=[jax.ShapeDtypeStruct((nblk, rows, GROUP_WIDTH), BF16),
                   jax.ShapeDtypeStruct((nblk * SUBLANES, fin_layers * fin_w), F32)],
        scratch_shapes=[pltpu.VMEM((2, S5_CHUNK, 2 * S5_SP), F32), pltpu.VMEM((2, S5_CHUNK, 2 * S5_SP), F32),
                        pltpu.VMEM((rows, GROUP_WIDTH), F32)],
        input_output_aliases=aliases,
        compiler_params=_cparams("parallel"),
        name="s5",
    )(*args)
    return od.reshape(nblk, S5_SEG, SUBLANES * GROUP_WIDTH), fin


ROUTE_GROUP = MOE_PER_GROUP
OUT_SEQS = 2
MOE_TILE = 512


def _out_kernel(x_ref, oa_ref, ob_ref, oc_ref, od_ref, mod_ref, wo_ref, g2_ref, wrh_ref, wrl_ref, br_ref,
                xm_ref, h2_ref, route_ref, cnt_ref):
    od = jnp.concatenate([od_ref[:, s * GROUP_WIDTH:(s + 1) * GROUP_WIDTH] for s in range(OUT_SEQS)], axis=0)
    mix = functools.reduce(jnp.add, [
        _bdot(o, wo_ref[i * GROUP_WIDTH:(i + 1) * GROUP_WIDTH, :])
        for i, o in enumerate((oa_ref[...], ob_ref[...], oc_ref[...], od))])
    xm = x_ref[...] + mod_ref[2:3, :] * mix
    xm_ref[...] = xm
    h2 = _rms_rows(xm) * g2_ref[...] * (1.0 + mod_ref[4:5, :]) + mod_ref[3:4, :]
    h2_ref[...] = h2.astype(BF16)

    h_hi, h_lo = _split(h2)
    logits = (jnp.dot(h_hi, wrh_ref[...], preferred_element_type=F32)
              + jnp.dot(h_hi, wrl_ref[...], preferred_element_type=F32)
              + jnp.dot(h_lo, wrh_ref[...], preferred_element_type=F32)) + br_ref[...]
    lane_i = lax.broadcasted_iota(jnp.int32, logits.shape, 1)
    lane = lane_i.astype(F32)
    big = jnp.float32(2 ** 30)
    gmask = lane_i < MOE_GROUPS
    gl = jnp.where(gmask, logits, -jnp.inf)
    gmax = jnp.max(gl, axis=-1, keepdims=True)
    p_top = 1.0 / jnp.sum(jnp.exp(gl - gmax), axis=-1, keepdims=True)
    g_top = jnp.min(jnp.where(gl == gmax, lane, big), axis=-1, keepdims=True)
    e_lane = lane_i - ROUTER_OFF
    lane_group = (e_lane // MOE_PER_GROUP).astype(F32)
    emask = (e_lane >= 0) & (e_lane < MOE_EXPERTS) & (lane_group == g_top)
    el = jnp.where(emask, logits, -jnp.inf)
    m1 = jnp.max(el, axis=-1, keepdims=True)
    i1 = jnp.min(jnp.where(el == m1, lane, big), axis=-1, keepdims=True)
    el2 = jnp.where(lane == i1, -jnp.inf, el)
    m2 = jnp.max(el2, axis=-1, keepdims=True)
    i2 = jnp.min(jnp.where(el2 == m2, lane, big), axis=-1, keepdims=True)
    e2 = jnp.exp(m2 - m1)
    den = 1.0 + e2
    gates = (jnp.where(lane == i1, (1.0 / den) * p_top, 0.0)
             + jnp.where(lane == i2, (e2 / den) * p_top, 0.0))
    route = jnp.where(lane == ROUTE_GROUP + g_top, 1.0, 0.0)
    for g in range(MOE_GROUPS):
        local = pltpu.roll(gates, LANES - ROUTER_OFF - g * MOE_PER_GROUP, 1)
        route = route + jnp.where((g_top == g) & (lane_i < MOE_PER_GROUP), local, 0.0)
    route_ref[...] = route
    for t in range(cnt_ref.shape[0]):
        part = jnp.sum(route[t * MOE_TILE:(t + 1) * MOE_TILE], axis=0, keepdims=True)
        cnt_ref[t] = jnp.broadcast_to(part, (SUBLANES, LANES)).astype(jnp.int32)


def _output_stage(x, mixes, mods, mod_row, mod_tokens, wo_bf, g2, wr_hi, wr_lo, br, layer):
    tm = OUT_SEQS * S5_SEG
    n = x.shape[0]
    row = lambda w: pl.BlockSpec((tm, w), lambda i: (i, 0))
    const = lambda shape: pl.BlockSpec(shape, lambda i: (0,) * len(shape))
    per_blk = SUBLANES // OUT_SEQS
    return pl.pallas_call(
        _out_kernel,
        grid=(n // tm,),
        in_specs=[row(D_MODEL), row(256), row(256), row(256),
                  pl.BlockSpec((None, S5_SEG, OUT_SEQS * GROUP_WIDTH), lambda i: (i // per_blk, 0, i % per_blk)),
                  _mod_spec(layer, mod_row, mod_tokens // tm, 1),
                  pl.BlockSpec((None, D_MODEL, D_MODEL), lambda i: (layer, 0, 0)), const((1, D_MODEL)),
                  const((D_MODEL, LANES)), const((D_MODEL, LANES)), const((1, LANES))],
        out_specs=[row(D_MODEL), row(D_MODEL), row(LANES),
                   pl.BlockSpec((tm // MOE_TILE, SUBLANES, LANES), lambda i: (i, 0, 0))],
        out_shape=[jax.ShapeDtypeStruct((n, D_MODEL), F32),
                   jax.ShapeDtypeStruct((n, D_MODEL), BF16),
                   jax.ShapeDtypeStruct((n, LANES), F32),
                   jax.ShapeDtypeStruct((n // MOE_TILE, SUBLANES, LANES), jnp.int32)],
        compiler_params=_cparams("parallel"),
        name="output_stage",
    )(x, *mixes, mods, wo_bf, g2, wr_hi, wr_lo, br)


GROUP_HID = MOE_PER_GROUP * MOE_HIDDEN


MOE_CHUNK = 160


def _moe_kernel(cnt_ref, h2_ref, route_ref, xm_ref, mod_ref, w1_ref, w3_ref, w2_ref, fg_ref, o_ref,
                hs_scr, rs_scr, os_scr, before_scr, *, final, tm):
    i = pl.program_id(0)
    off1 = cnt_ref[i, 0]
    off2 = off1 + cnt_ref[i, 1]
    off3 = off2 + cnt_ref[i, 2]
    starts = (jnp.int32(0), off1, off2, off3)
    ends = (off1, off2, off3, jnp.int32(tm))

    route = route_ref[...]
    r_hi, r_lo = _split(route)
    pick = (lax.broadcasted_iota(jnp.int32, (SUBLANES, LANES), 1)
            == ROUTE_GROUP + lax.broadcasted_iota(jnp.int32, (SUBLANES, LANES), 0))
    gt = lax.dot_general(jnp.where(pick, 1.0, 0.0).astype(BF16), r_hi, (((1,), (1,)), ((), ())),
                         preferred_element_type=F32)
    @pl.when(i == 0)
    def _():
        before_scr[...] = jnp.where(lax.broadcasted_iota(jnp.int32, (tm, tm), 0)
                                    < lax.broadcasted_iota(jnp.int32, (tm, tm), 1), 1.0, 0.0).astype(BF16)

    rank = jnp.dot(gt.astype(BF16), before_scr[...], preferred_element_type=F32)
    gt_i = gt.astype(jnp.int32)
    rank_i = rank.astype(jnp.int32)
    pos = jnp.zeros((1, tm), jnp.int32)
    for g in range(MOE_GROUPS):
        pos = pos + gt_i[g:g + 1, :] * (rank_i[g:g + 1, :] + starts[g])
    perm = jnp.where(lax.broadcasted_iota(jnp.int32, (tm, tm), 0) == pos, 1.0, 0.0).astype(BF16)
    hs_scr[...] = jnp.dot(perm, h2_ref[...], preferred_element_type=F32).astype(BF16)
    rs_scr[...] = (jnp.dot(perm, r_hi, preferred_element_type=F32)
                   + jnp.dot(perm, r_lo, preferred_element_type=F32))

    os_scr[...] = jnp.zeros_like(os_scr)
    for g in range(MOE_GROUPS):
        lo, hi = starts[g], ends[g]
        base = (lo // BF16_ROWS) * BF16_ROWS
        n_chunks = jnp.where(hi > lo, (hi - base + MOE_CHUNK - 1) // MOE_CHUNK, 0)

        def chunk_body(c, carry, g=g, lo=lo, hi=hi, base=base):
            r0 = pl.multiple_of(jnp.minimum(base + c * MOE_CHUNK, tm - MOE_CHUNK), BF16_ROWS)
            rows = pl.ds(r0, MOE_CHUNK)
            x = hs_scr[rows, :]
            gates = rs_scr[rows, :]
            a = jnp.dot(x, w1_ref[g], preferred_element_type=F32)
            b = jnp.dot(x, w3_ref[g], preferred_element_type=F32)
            hid = []
            for e in range(MOE_PER_GROUP):
                sl = slice(e * MOE_HIDDEN, (e + 1) * MOE_HIDDEN)
                hid.append((jax.nn.silu(a[:, sl]) * b[:, sl] * gates[:, e:e + 1]).astype(BF16))
            y = jnp.dot(jnp.concatenate(hid, axis=1), w2_ref[g], preferred_element_type=F32)
            rowid = r0 + lax.broadcasted_iota(jnp.int32, (MOE_CHUNK, 1), 0)
            member = (rowid >= lo) & (rowid < hi)
            os_scr[rows, :] = jnp.where(member, y, os_scr[rows, :])
            return carry

        lax.fori_loop(0, n_chunks, chunk_body, 0)

    o_hi, o_lo = _split(os_scr[...])
    moe = (lax.dot_general(perm, o_hi, (((0,), (0,)), ((), ())), preferred_element_type=F32)
           + lax.dot_general(perm, o_lo, (((0,), (0,)), ((), ())), preferred_element_type=F32))
    out = xm_ref[...] + mod_ref[5:6, :] * moe
    if final:
        out = _rms_rows(out) * fg_ref[...]
    o_ref[...] = out


def _moe_weight_kernel(w1_ref, w3_ref, w2_ref, o1_ref, o3_ref, o2_ref):
    for e in range(MOE_PER_GROUP):
        sl = slice(e * MOE_HIDDEN, (e + 1) * MOE_HIDDEN)
        o1_ref[:, sl] = w1_ref[e].astype(BF16)
        o3_ref[:, sl] = w3_ref[e].astype(BF16)
        o2_ref[sl, :] = w2_ref[e].astype(BF16)


def _moe_weights(w1, w3, w2):
    up = pl.BlockSpec((None, MOE_PER_GROUP, D_MODEL, MOE_HIDDEN), lambda l, g: (l, g, 0, 0))
    down = pl.BlockSpec((None, MOE_PER_GROUP, MOE_HIDDEN, D_MODEL), lambda l, g: (l, g, 0, 0))
    out = pl.BlockSpec((None, None, D_MODEL, GROUP_HID), lambda l, g: (l, g, 0, 0))
    shape = jax.ShapeDtypeStruct((DEPTH, MOE_GROUPS, D_MODEL, GROUP_HID), BF16)
    return pl.pallas_call(
        _moe_weight_kernel,
        grid=(DEPTH, MOE_GROUPS),
        in_specs=[up, up, down],
        out_specs=[out, out, out],
        out_shape=[shape, shape, shape],
        compiler_params=_cparams("parallel", "parallel"),
        name="moe_weights",
    )(w1, w3, w2)


def _moe(h2, route, tile_counts, xm, mods, mod_row, mod_tokens, w1g, w3g, w2g, fg, layer, *, final):
    tm = MOE_TILE
    n = h2.shape[0]
    cnt = tile_counts[:, 0, ROUTE_GROUP:ROUTE_GROUP + MOE_GROUPS]
    row = lambda w: pl.BlockSpec((tm, w), lambda i, c: (i, 0))
    mod_tiles = mod_tokens // tm
    wspec = pl.BlockSpec((None, MOE_GROUPS, D_MODEL, GROUP_HID), lambda i, c: (layer, 0, 0, 0),
                         pipeline_mode=pl.Buffered(1))
    return pl.pallas_call(
        functools.partial(_moe_kernel, final=final, tm=tm),
        grid_spec=pltpu.PrefetchScalarGridSpec(
            num_scalar_prefetch=1,
            grid=(n // tm,),
            in_specs=[row(D_MODEL), row(LANES), row(D_MODEL),
                      pl.BlockSpec((None, None, 6, D_MODEL), lambda i, c: (layer, mod_row + i // mod_tiles, 0, 0)),
                      wspec, wspec, wspec,
                      pl.BlockSpec((1, D_MODEL), lambda i, c: (0, 0))],
            out_specs=row(D_MODEL),
            scratch_shapes=[pltpu.VMEM((tm, D_MODEL), BF16), pltpu.VMEM((tm, LANES), F32),
                            pltpu.VMEM((tm, D_MODEL), F32), pltpu.VMEM((tm, tm), BF16)]),
        out_shape=jax.ShapeDtypeStruct((n, D_MODEL), F32),
        compiler_params=_cparams("arbitrary"),
        name="moe",
    )(cnt, h2, route, xm, mods, w1g, w3g, w2g, fg)


def kernel(x_prompt, x_sample, cache_a_k, cache_a_v, cache_b_k, cache_b_v, state_ret, state_ssm, c, c_ctx, mod_w, mod_b, norm1_g, norm2_g, w_in, a_qn_g, a_kn_g, b_rel_bias, ret_decay, ret_gn_g, s5_lam_re, s5_lam_im, s5_log_dt, s5_b_re, s5_b_im, s5_c_re, s5_c_im, s5_d, s5_glu_w, w_out, moe_gw, moe_gb, moe_ew, moe_eb, moe_w1, moe_w3, moe_w2, final_norm_g):
    n_ctx = BATCH * SEQ
    n_lat = DEC_BATCH * DEC_SEQ
    lat_seg = DEC_SEQ // S5_SEG

    cond = jnp.zeros((SUBLANES, D_MODEL), F32).at[0].set(c_ctx).at[1:1 + DEC_BATCH].set(c)
    mods = _modulation(cond, mod_w, mod_b).reshape(DEPTH, SUBLANES, 6, D_MODEL)

    rope_tabs = _rope_tables()
    s5_a, s5_bm, s5_cre, s5_cim = _s5_prepare(s5_lam_re, s5_lam_im, s5_log_dt, s5_b_re, s5_b_im,
                                              s5_c_re, s5_c_im)
    cak = cache_a_k.reshape(DEC_BATCH, DEPTH, PAST_LEN, A_KV_HEADS * HEAD_DIM)
    cav = cache_a_v.reshape(DEC_BATCH, DEPTH, PAST_LEN, A_KV_HEADS * HEAD_DIM)
    cbk = cache_b_k.reshape(DEC_BATCH, DEPTH, PAST_LEN, B_HEADS * HEAD_DIM)
    cbv = cache_b_v.reshape(DEC_BATCH, DEPTH, PAST_LEN, B_HEADS * HEAD_DIM)

    xc = x_prompt.reshape(n_ctx, D_MODEL)
    xs = x_sample.reshape(n_lat, D_MODEL)
    w1_all, w3_all, w2_all = _moe_weights(moe_w1, moe_w3, moe_w2)
    eye_h = jnp.eye(C_HEADS, dtype=F32)
    s0_bd = (state_ret[:, :, :, :, :, None, :] * eye_h[None, None, None, :, None, :, None]).reshape(
        DEC_BATCH, DEPTH, 2, C_HEADS * HEAD_DIM, C_HEADS * HEAD_DIM)
    ctx_state = ssm_states = None
    h0_zero = jnp.zeros((2, SUBLANES, 2 * S5_SP), F32)
    w_in_bf = w_in.astype(BF16)
    wo_bf = w_out.astype(BF16)
    glu_bf = s5_glu_w.astype(BF16)
    for l in range(DEPTH):
        final = l == DEPTH - 1
        g1 = norm1_g[l].reshape(1, D_MODEL)
        g2 = norm2_g[l].reshape(1, D_MODEL)
        fg = final_norm_g.reshape(1, D_MODEL)
        qn = jnp.tile(a_qn_g[l], A_HEADS).reshape(1, 256)
        kn = jnp.tile(a_kn_g[l], A_KV_HEADS).reshape(1, 128)
        dec = jnp.broadcast_to(ret_decay[l].reshape(2 * C_HEADS, 1), (2 * C_HEADS, LANES))
        gn = ret_gn_g[l].reshape(1, 256)
        dvec = s5_d[l].reshape(1, GROUP_WIDTH)
        wr = jnp.zeros((D_MODEL, LANES), F32).at[:, :MOE_GROUPS].set(moe_gw[l]).at[
            :, ROUTER_OFF:ROUTER_OFF + MOE_EXPERTS].set(moe_ew[l])
        br = jnp.zeros((1, LANES), F32).at[0, :MOE_GROUPS].set(moe_gb[l]).at[
            0, ROUTER_OFF:ROUTER_OFF + MOE_EXPERTS].set(moe_eb[l])
        wr_hi = wr.astype(BF16)
        wr_lo = (wr - wr_hi.astype(F32)).astype(BF16)
        na_bias = _na_bias(b_rel_bias[l])

        oa, ob, oc, du_tm, ctx_state = _ctx_front(xc, mods, g1, w_in_bf, qn, kn, dec, gn, l, ctx_state)
        od_tm, ssm_states = _s5(du_tm, h0_zero, s5_a, s5_bm, s5_cre, s5_cim, dvec, glu_bf, l,
                                nseg=1, fin_layer=l, fin_layers=DEPTH, prev_fin=ssm_states)
        xm, h2, route, counts = _output_stage(xc, (oa, ob, oc, od_tm), mods, 0, n_ctx, wo_bf, g2,
                                              wr_hi, wr_lo, br, l)
        xc = _moe(h2, route, counts, xm, mods, 0, n_ctx, w1_all, w3_all, w2_all, fg, l, final=final)

        zs, cg, du_tm = _project(xs, mods, 1, DEC_SEQ, g1, w_in_bf, qn, kn, rope_tabs, l, seq_len=DEC_SEQ)
        zs3 = zs.reshape(DEC_BATCH, DEC_SEQ, OFF_CG)
        oa = _lat_attention_a(zs3, cak, cav, l).reshape(n_lat, 256)
        ob = _lat_attention_b(zs3, cbk, cbv, na_bias, l).reshape(n_lat, 256)
        oc = _retention(zs3, cg.reshape(DEC_BATCH, DEC_SEQ, 256), dec, gn, s0_bd, l).reshape(n_lat, 256)
        h0 = state_ssm[:, l].reshape(DEC_BATCH, 2, 2 * S5_SP).transpose(1, 0, 2)
        h0_seg = jnp.zeros((2, DEC_BATCH, lat_seg, 2 * S5_SP), F32)
        h0_seg = h0_seg.at[0, :, 0].set(h0[0]).at[1, :, lat_seg - 1].set(h0[1])
        od_tm, _ = _s5(du_tm, h0_seg.reshape(2, SUBLANES, 2 * S5_SP),
                       s5_a, s5_bm, s5_cre, s5_cim, dvec, glu_bf, l, nseg=lat_seg)
        xm, h2, route, counts = _output_stage(xs, (oa, ob, oc, od_tm), mods, 1, DEC_SEQ, wo_bf, g2,
                                              wr_hi, wr_lo, br, l)
        xs = _moe(h2, route, counts, xm, mods, 1, DEC_SEQ, w1_all, w3_all, w2_all, fg, l, final=final)

    new_ak, new_av, new_bk, new_bv, ret_states = ctx_state
    return (xc.reshape(BATCH, SEQ, D_MODEL), xs.reshape(DEC_BATCH, DEC_SEQ, D_MODEL),
            new_ak.reshape(BATCH, DEPTH, SEQ, A_KV_HEADS, HEAD_DIM),
            new_av.reshape(BATCH, DEPTH, SEQ, A_KV_HEADS, HEAD_DIM),
            new_bk.reshape(BATCH, DEPTH, SEQ, B_HEADS, HEAD_DIM),
            new_bv.reshape(BATCH, DEPTH, SEQ, B_HEADS, HEAD_DIM),
            ret_states,
            ssm_states.reshape(BATCH, DEPTH, 2, 2, S5_GROUPS, S5_STATE))
```

```python
import functools
import math

import numpy as np
import jax
import jax.numpy as jnp
from jax import lax
from jax.experimental import pallas as pl
from jax.experimental.pallas import tpu as pltpu

F32 = jnp.float32
BF16 = jnp.bfloat16

D_MODEL = 1024
BATCH = 32
SEQ = 256
DEPTH = 2
DEC_BATCH = 2
DEC_SEQ = 1024
PAST_LEN = 256
GRID_W = 64
HEAD_DIM = 64
GROUP_WIDTH = 256
A_HEADS = 4
A_KV_HEADS = 2
B_HEADS = 4
NA_ROWS = 8
NA_COLS = 16
C_HEADS = 4
S5_CH = 16
S5_GROUPS = 16
S5_STATE = 64
MOE_GROUPS = 4
MOE_PER_GROUP = 8
MOE_EXPERTS = 32
MOE_HIDDEN = 128
ROPE_THETA = 10000.0
EPS = 1e-6
IN_WIDTH = 2560
Q_SCALE = HEAD_DIM ** -0.5
KV_WIDTH = A_KV_HEADS * HEAD_DIM
N_MOD = 6
ROPE_PAIR = HEAD_DIM // 4
LAT_TQ = 256
MOD_TILE = 1536

OFF_AQ, OFF_AK, OFF_AV = 0, 256, 384
OFF_BQ, OFF_BK, OFF_BV = 512, 768, 1024
OFF_CQ, OFF_CK, OFF_CV, OFF_CG = 1280, 1536, 1792, 2048
OFF_DU = 2304

LANES = 128
SUBLANES = 8
BF16_ROWS = 16
S5_SP = S5_GROUPS * S5_STATE
S5_SEG = 256
S5_CHUNK = 256
ROUTER_OFF = 4
NEG_BIG = -1e30
VMEM_LIMIT = 56 * 1024 * 1024


def _cparams(*sem):
    return pltpu.CompilerParams(dimension_semantics=sem, vmem_limit_bytes=VMEM_LIMIT)


def _mod_spec(layer, first_row, tiles_per_row):
    return pl.BlockSpec((None, None, N_MOD, D_MODEL), lambda i: (layer, first_row + i // tiles_per_row, 0, 0))


def _bdot(a, b):
    return jnp.dot(a.astype(BF16), b.astype(BF16), preferred_element_type=F32)


def _bdot_nt(a, b):
    return lax.dot_general(a.astype(BF16), b.astype(BF16), (((1,), (1,)), ((), ())),
                           preferred_element_type=F32)


def _bdot_tn(a, b):
    return lax.dot_general(a.astype(BF16), b.astype(BF16), (((0,), (0,)), ((), ())),
                           preferred_element_type=F32)


def _split(a):
    hi = a.astype(BF16)
    lo = (a - hi.astype(F32)).astype(BF16)
    return hi, lo


def _dot_hilo_lhs(a, b_bf16):
    hi, lo = _split(a)
    return (jnp.dot(hi, b_bf16, preferred_element_type=F32)
            + jnp.dot(lo, b_bf16, preferred_element_type=F32))


def _rms_rows(x):
    return x * lax.rsqrt(jnp.mean(x * x, axis=-1, keepdims=True) + EPS)


def _mod_kernel(cond_ref, w_ref, b_ref, o_ref):
    o_ref[...] = _bdot(jax.nn.silu(cond_ref[...]), w_ref[...]) + b_ref[...]


def _modulation(cond, mod_w, mod_b):
    tn = MOD_TILE
    width = N_MOD * D_MODEL
    return pl.pallas_call(
        _mod_kernel,
        grid=(DEPTH, width // tn),
        in_specs=[pl.BlockSpec((SUBLANES, D_MODEL), lambda l, j: (0, 0)),
                  pl.BlockSpec((None, D_MODEL, tn), lambda l, j: (l, 0, j)),
                  pl.BlockSpec((None, 1, tn), lambda l, j: (l, 0, j))],
        out_specs=pl.BlockSpec((None, SUBLANES, tn), lambda l, j: (l, 0, j)),
        out_shape=jax.ShapeDtypeStruct((DEPTH, SUBLANES, width), F32),
        compiler_params=_cparams("arbitrary", "arbitrary"),
        name="modulation",
    )(cond, mod_w, mod_b.reshape(DEPTH, 1, width))


def _group_mean_matrix(w):
    ri = lax.broadcasted_iota(jnp.int32, (w, w), 0) // HEAD_DIM
    ci = lax.broadcasted_iota(jnp.int32, (w, w), 1) // HEAD_DIM
    return jnp.where(ri == ci, 1.0 / HEAD_DIM, 0.0).astype(BF16)


def _head_norm(t, g):
    ms = _dot_hilo_lhs(t * t, _group_mean_matrix(t.shape[1]))
    return t * lax.rsqrt(ms + EPS) * g


def _rope(t, cos, sa, sb):
    return (t * cos + pltpu.roll(t, LANES - ROPE_PAIR, 1) * sa + pltpu.roll(t, ROPE_PAIR, 1) * sb)


def _store_layer_slot(ref, slot, value):
    for s in range(ref.shape[0]):
        ref[s] = value if s == slot else jnp.zeros_like(value)


def _layer_slot_block(layer, first_call, tail):
    if first_call:
        return (None, DEPTH) + tail, (0,) * (1 + len(tail)), layer
    return (None, 1) + tail, (layer,) + (0,) * len(tail), 0


def _proj_kernel(x_ref, mod_ref, g1_ref, w_ref, qn_ref, kn_ref, cos_ref, sa_ref, sb_ref, z_ref, cg_ref, du_ref):
    h = _rms_rows(x_ref[...]) * g1_ref[...] * (1.0 + mod_ref[1:2, :]) + mod_ref[0:1, :]
    z = jnp.dot(h.astype(BF16), w_ref[...], preferred_element_type=F32)
    aq = _head_norm(z[:, OFF_AQ:OFF_AK], qn_ref[...])
    ak = _head_norm(z[:, OFF_AK:OFF_AV], kn_ref[...])
    for j in range(3):
        t = aq[:, j * LANES:(j + 1) * LANES] if j < 2 else ak
        sl = slice(0, LANES) if j == 2 else slice(j * LANES, (j + 1) * LANES)
        t = _rope(t, cos_ref[:, sl], sa_ref[:, sl], sb_ref[:, sl])
        z_ref[:, j * LANES:(j + 1) * LANES] = t.astype(BF16)
    z_ref[:, OFF_AV:OFF_CK] = z[:, OFF_AV:OFF_CK].astype(BF16)
    z_ref[:, OFF_CK:OFF_CV] = (z[:, OFF_CK:OFF_CV] * Q_SCALE).astype(BF16)
    z_ref[:, OFF_CV:OFF_CG] = z[:, OFF_CV:OFF_CG].astype(BF16)
    cg_ref[...] = z[:, OFF_CG:OFF_DU]
    du_ref[...] = z[:, OFF_DU:]


def _du_spec():
    return pl.BlockSpec((None, S5_SEG, GROUP_WIDTH), lambda i: (i // SUBLANES, 0, i % SUBLANES))


def _project(x, mods, mod_row, mod_tokens, g1, w_in_bf, qn, kn, rope_tabs, layer, *, seq_len):
    tm = S5_SEG
    n = x.shape[0]
    tps = seq_len // tm
    return pl.pallas_call(
        _proj_kernel,
        grid=(n // tm,),
        in_specs=[pl.BlockSpec((tm, D_MODEL), lambda i: (i, 0)),
                  _mod_spec(layer, mod_row, mod_tokens // tm),
                  pl.BlockSpec((1, D_MODEL), lambda i: (0, 0)),
                  pl.BlockSpec((None, D_MODEL, IN_WIDTH), lambda i: (layer, 0, 0)),
                  pl.BlockSpec((1, GROUP_WIDTH), lambda i: (0, 0)),
                  pl.BlockSpec((1, KV_WIDTH), lambda i: (0, 0))]
                 + [pl.BlockSpec((tm, GROUP_WIDTH), lambda i: (i % tps, 0))] * 3,
        out_specs=[pl.BlockSpec((tm, OFF_CG), lambda i: (i, 0)),
                   pl.BlockSpec((tm, GROUP_WIDTH), lambda i: (i, 0)), _du_spec()],
        out_shape=[jax.ShapeDtypeStruct((n, OFF_CG), BF16),
                   jax.ShapeDtypeStruct((n, GROUP_WIDTH), F32),
                   jax.ShapeDtypeStruct((n // (tm * SUBLANES), S5_SEG, SUBLANES * GROUP_WIDTH), F32)],
        compiler_params=_cparams("parallel"),
        name="project",
    )(x, mods, g1, w_in_bf, qn, kn, *rope_tabs)


def _rope_tables():
    t = jnp.arange(DEC_SEQ)
    row = (t // GRID_W).astype(F32)
    col = (t % GRID_W).astype(F32)
    nf = HEAD_DIM // 4
    inv = ROPE_THETA ** (-jnp.arange(nf, dtype=F32) / nf)
    ang_r = row[:, None] * inv[None, :]
    ang_c = col[:, None] * inv[None, :]
    zeros = jnp.zeros_like(ang_r)
    cos = jnp.concatenate([jnp.cos(ang_r), jnp.cos(ang_r), jnp.cos(ang_c), jnp.cos(ang_c)], axis=-1)
    sa = jnp.concatenate([-jnp.sin(ang_r), zeros, -jnp.sin(ang_c), zeros], axis=-1)
    sb = jnp.concatenate([zeros, jnp.sin(ang_r), zeros, jnp.sin(ang_c)], axis=-1)
    return tuple(jnp.tile(a, (1, 4)) for a in (cos, sa, sb))


N_HEADS = 4


def _lane_head(width):
    return lax.broadcasted_iota(jnp.int32, (1, width), 1) // HEAD_DIM


def _stack_heads(q):
    head = _lane_head(q.shape[1])
    return jnp.concatenate([jnp.where(head == h, q, 0.0) for h in range(N_HEADS)], axis=0).astype(BF16)


def _stack_heads_gqa(q):
    lo = lax.broadcasted_iota(jnp.int32, (1, LANES), 1) < HEAD_DIM
    q = q.astype(F32)
    q01, q23 = q[:, :LANES], q[:, LANES:]
    blocks = [jnp.where(lo, q01, 0.0), jnp.where(lo, pltpu.roll(q01, HEAD_DIM, 1), 0.0),
              jnp.where(lo, 0.0, pltpu.roll(q23, HEAD_DIM, 1)), jnp.where(lo, 0.0, q23)]
    return jnp.concatenate(blocks, axis=0).astype(BF16)


def _spread_kv_gqa(v):
    lo = lax.broadcasted_iota(jnp.int32, (1, LANES), 1) < HEAD_DIM
    v = v.astype(F32)
    vr = pltpu.roll(v, HEAD_DIM, 1)
    return jnp.concatenate([jnp.where(lo, v, vr), jnp.where(lo, vr, v)], axis=1)


def _mha(qs, blocks, tq):
    scores = []
    for k, _, bias in blocks:
        s = _bdot_nt(qs, k)
        scores.append(s if bias is None else s + bias)
    m = functools.reduce(jnp.maximum, [jnp.max(s, axis=-1, keepdims=True) for s in scores])
    es = [jnp.exp(s - m) for s in scores]
    denom = functools.reduce(jnp.add, [jnp.sum(e, axis=-1, keepdims=True) for e in es])
    ps = [e.astype(BF16) for e in es]
    head = _lane_head(N_HEADS * HEAD_DIM)
    vals = [v.astype(BF16) for _, v, _ in blocks]
    o = None
    dall = None
    for h in range(N_HEADS):
        rows = slice(h * tq, (h + 1) * tq)
        for p, v in zip(ps, vals):
            t = jnp.dot(p[rows], jnp.where(head == h, v, jnp.zeros_like(v)), preferred_element_type=F32)
            o = t if o is None else o + t
        d = jnp.where(head == h, denom[rows], 0.0)
        dall = d if dall is None else dall + d
    return (o / dall).astype(BF16)


def _lat_attn_a_kernel(q_ref, kn_ref, vn_ref, kc_ref, vc_ref, o_ref):
    for b in range(DEC_BATCH):
        o_ref[b] = _mha(_stack_heads_gqa(q_ref[b] * Q_SCALE),
                        [(kc_ref[b], _spread_kv_gqa(vc_ref[b]), None),
                         (kn_ref[b], _spread_kv_gqa(vn_ref[b]), None)], q_ref.shape[1])


def _lat_attention_a(z, cache_k, cache_v, layer, tq=LAT_TQ):
    cache_spec = pl.BlockSpec((DEC_BATCH, None, PAST_LEN, KV_WIDTH), lambda j: (0, layer, 0, 0))
    return pl.pallas_call(
        _lat_attn_a_kernel,
        grid=(DEC_SEQ // tq,),
        in_specs=[pl.BlockSpec((DEC_BATCH, tq, GROUP_WIDTH), lambda j: (0, j, OFF_AQ // GROUP_WIDTH)),
                  pl.BlockSpec((DEC_BATCH, DEC_SEQ, KV_WIDTH), lambda j: (0, 0, OFF_AK // KV_WIDTH)),
                  pl.BlockSpec((DEC_BATCH, DEC_SEQ, KV_WIDTH), lambda j: (0, 0, OFF_AV // KV_WIDTH)),
                  cache_spec, cache_spec],
        out_specs=pl.BlockSpec((DEC_BATCH, tq, GROUP_WIDTH), lambda j: (0, j, 0)),
        out_shape=jax.ShapeDtypeStruct((DEC_BATCH, DEC_SEQ, GROUP_WIDTH), BF16),
        compiler_params=_cparams("parallel"),
        name="lat_attention_a",
    )(z, z, z, cache_k, cache_v)


NA_KEYS = NA_ROWS * GRID_W


NA_PAIRS = 2 * NA_ROWS - 2


NA_STEP_ROWS = 2


def _na_kernel(q_ref, k_ref, v_ref, kc_ref, vc_ref, bias_ref, o_ref):
    rows = DEC_SEQ // GRID_W
    for b in range(q_ref.shape[0]):
        outs = []
        for rr in range(NA_STEP_ROWS):
            r = pl.program_id(0) * NA_STEP_ROWS + rr
            row_start = jnp.clip(r - NA_ROWS // 2, 0, rows - NA_ROWS)
            start = pl.multiple_of(row_start * GRID_W, GRID_W)
            rel0 = row_start - r + NA_ROWS - 1
            kl = k_ref[b, pl.ds(start, NA_KEYS), :]
            vl = v_ref[b, pl.ds(start, NA_KEYS), :]
            bias = jnp.concatenate(
                [jnp.concatenate([bias_ref[h, rel0 + 2 * jp] for jp in range(NA_ROWS // 2)], axis=1)
                 for h in range(B_HEADS)], axis=0)
            qrows = slice(rr * GRID_W, (rr + 1) * GRID_W)
            outs.append(_mha(_stack_heads(q_ref[b, qrows, :] * Q_SCALE),
                             [(kl, vl, bias), (kc_ref[b], vc_ref[b], None)], GRID_W))
        o_ref[b] = jnp.concatenate(outs, axis=0)


def _na_bias(rel_bias):
    nrel = 2 * NA_COLS - 1
    period = 2 * GRID_W
    b = rel_bias.astype(F32)
    ext = jnp.concatenate([b[..., NA_COLS - 1:],
                           jnp.zeros(b.shape[:-1] + (period - nrel,), F32),
                           b[..., :NA_COLS - 1]], axis=-1)
    flat = jnp.tile(ext, (1, 1, GRID_W))[..., :GRID_W * (period - 1)]
    toe = flat.reshape(b.shape[:-1] + (GRID_W, period - 1))[..., :GRID_W]
    col_start = np.clip(np.arange(GRID_W) - NA_COLS // 2, 0, GRID_W - NA_COLS)
    kc = np.arange(GRID_W)
    inside = (kc[None, :] >= col_start[:, None]) & (kc[None, :] < col_start[:, None] + NA_COLS)
    toe = jnp.where(jnp.asarray(inside), toe, NEG_BIG)
    return jnp.concatenate([toe[:, :-1], toe[:, 1:]], axis=-1)


def _lat_attention_b(z, cache_k, cache_v, bias, layer):
    tq = NA_STEP_ROWS * GRID_W
    cache_spec = pl.BlockSpec((DEC_BATCH, None, PAST_LEN, GROUP_WIDTH), lambda r: (0, layer, 0, 0))
    return pl.pallas_call(
        _na_kernel,
        grid=(DEC_SEQ // tq,),
        in_specs=[pl.BlockSpec((DEC_BATCH, tq, GROUP_WIDTH), lambda r: (0, r, OFF_BQ // GROUP_WIDTH)),
                  pl.BlockSpec((DEC_BATCH, DEC_SEQ, GROUP_WIDTH), lambda r: (0, 0, OFF_BK // GROUP_WIDTH)),
                  pl.BlockSpec((DEC_BATCH, DEC_SEQ, GROUP_WIDTH), lambda r: (0, 0, OFF_BV // GROUP_WIDTH)),
                  cache_spec, cache_spec,
                  pl.BlockSpec((B_HEADS, NA_PAIRS, GRID_W, 2 * GRID_W), lambda r: (0, 0, 0, 0))],
        out_specs=pl.BlockSpec((DEC_BATCH, tq, GROUP_WIDTH), lambda r: (0, r, 0)),
        out_shape=jax.ShapeDtypeStruct((DEC_BATCH, DEC_SEQ, GROUP_WIDTH), BF16),
        compiler_params=_cparams("parallel"),
        name="lat_attention_b",
    )(z, z, z, cache_k, cache_v, bias)


def _retention_core(q, k, v, g, dec_ref, gn_ref, dec_scr, *, seq_len, i0, decay_fill, s0_ref=None,
                    want_state=False):
    tq = q.shape[0]
    head = _lane_head(C_HEADS * HEAD_DIM)
    lg = jax.nn.log_sigmoid(dec_ref[...])

    def per_lane(row0):
        out = jnp.zeros((1, C_HEADS * HEAD_DIM), F32)
        for h in range(C_HEADS):
            out = jnp.where(head == h, lg[row0 + h:row0 + h + 1, 0:1], out)
        return out

    lgf_l, lgb_l = per_lane(0), per_lane(C_HEADS)
    qi = (i0 + lax.broadcasted_iota(jnp.int32, (tq, 1), 0)).astype(F32)

    def fill_decay():
        kj = lax.broadcasted_iota(jnp.int32, (1, seq_len), 1).astype(F32)
        diff = qi - kj
        for h in range(C_HEADS):
            lgf = lg[h:h + 1, 0:1]
            lgb = lg[C_HEADS + h:C_HEADS + h + 1, 0:1]
            dec_scr[h * tq:(h + 1) * tq, :] = (
                jnp.where(diff >= 0, jnp.exp(lgf * jnp.maximum(diff, 0.0)), 0.0)
                + jnp.where(diff <= 0, jnp.exp(lgb * jnp.maximum(-diff, 0.0)), 0.0))

    if decay_fill == "first_step":
        pl.when(pl.program_id(0) == 0)(fill_decay)
    elif decay_fill == "every_step":
        fill_decay()
    else:
        assert decay_fill == "filled"

    v = v.astype(BF16)
    sc = (_bdot_nt(_stack_heads(q), k) * dec_scr[...]).astype(BF16)
    o = None
    for h in range(C_HEADS):
        t = jnp.dot(sc[h * tq:(h + 1) * tq], jnp.where(head == h, v, jnp.zeros_like(v)),
                    preferred_element_type=F32)
        o = t if o is None else o + t
    if s0_ref is not None:
        o = (o + _bdot(q, s0_ref[0]) * jnp.exp(lgf_l * (qi + 1.0))
             + _bdot(q, s0_ref[1]) * jnp.exp(lgb_l * (seq_len - qi)))
    gm = _group_mean_matrix(C_HEADS * HEAD_DIM)
    dlt = o - _dot_hilo_lhs(o, gm)
    var = _dot_hilo_lhs(dlt * dlt, gm)
    out = (dlt * lax.rsqrt(var + EPS) * gn_ref[...] * jax.nn.silu(g)).astype(BF16)
    if not want_state:
        return out, None
    kpos = lax.broadcasted_iota(jnp.int32, (seq_len, 1), 0).astype(F32)
    sf = _bdot_tn(k * jnp.exp(lgf_l * (seq_len - 1.0 - kpos)), v)
    sb = _bdot_tn(k * jnp.exp(lgb_l * kpos), v)
    return out, (sf, sb)


def _store_retention_state(st_ref, slot, state):
    for s in range(st_ref.shape[0]):
        for d in range(2):
            for h in range(C_HEADS):
                sl = slice(h * HEAD_DIM, (h + 1) * HEAD_DIM)
                st_ref[s, d, h] = state[d][sl, sl] if s == slot else jnp.zeros((HEAD_DIM, HEAD_DIM), F32)


def _retention_kernel(q_ref, g_ref, k_ref, v_ref, dec_ref, gn_ref, s0_ref, o_ref, dec_scr, *, seq_len, tq):
    for b in range(q_ref.shape[0]):
        o_ref[b], _ = _retention_core(q_ref[b], k_ref[b], v_ref[b], g_ref[b], dec_ref, gn_ref, dec_scr,
                                      seq_len=seq_len, i0=pl.program_id(0) * tq,
                                      decay_fill="every_step" if b == 0 else "filled", s0_ref=s0_ref.at[b])


def _retention(z, cg, dec, gn, s0, layer, *, tq=LAT_TQ):
    nb, seq_len = z.shape[:2]
    return pl.pallas_call(
        functools.partial(_retention_kernel, seq_len=seq_len, tq=tq),
        grid=(seq_len // tq,),
        in_specs=[pl.BlockSpec((nb, tq, GROUP_WIDTH), lambda j: (0, j, OFF_CQ // GROUP_WIDTH)),
                  pl.BlockSpec((nb, tq, GROUP_WIDTH), lambda j: (0, j, 0)),
                  pl.BlockSpec((nb, seq_len, GROUP_WIDTH), lambda j: (0, 0, OFF_CK // GROUP_WIDTH)),
                  pl.BlockSpec((nb, seq_len, GROUP_WIDTH), lambda j: (0, 0, OFF_CV // GROUP_WIDTH)),
                  pl.BlockSpec((SUBLANES, LANES), lambda j: (0, 0)),
                  pl.BlockSpec((1, GROUP_WIDTH), lambda j: (0, 0)),
                  pl.BlockSpec((nb, None, 2, GROUP_WIDTH, GROUP_WIDTH), lambda j: (0, layer, 0, 0, 0))],
        out_specs=pl.BlockSpec((nb, tq, GROUP_WIDTH), lambda j: (0, j, 0)),
        out_shape=jax.ShapeDtypeStruct((nb, seq_len, GROUP_WIDTH), BF16),
        scratch_shapes=[pltpu.VMEM((C_HEADS * tq, seq_len), F32)],
        compiler_params=_cparams("arbitrary"),
        name="retention",
    )(z, cg, z, z, dec, gn, s0)


CTX_SEQS = 4


def _ctx_front_kernel(x_ref, mod_ref, g1_ref, w_ref, qn_ref, kn_ref, dec_ref, gn_ref, *rest, n_alias, slot):
    (oa_ref, ob_ref, oc_ref, du_ref, ak_ref, av_ref, bk_ref, bv_ref, st_ref, dec_scr) = rest[n_alias:]
    tq = x_ref.shape[0] // CTX_SEQS
    h = _rms_rows(x_ref[...]) * g1_ref[...] * (1.0 + mod_ref[1:2, :]) + mod_ref[0:1, :]
    zz = jnp.dot(h.astype(BF16), w_ref[...], preferred_element_type=F32)
    for s in range(CTX_SEQS):
        rows = slice(s * tq, (s + 1) * tq)
        z = zz[rows, :]
        aq = _head_norm(z[:, OFF_AQ:OFF_AK], qn_ref[...])
        ak = _head_norm(z[:, OFF_AK:OFF_AV], kn_ref[...])
        av, bq, bk, bv = (z[:, OFF_AV:OFF_BQ], z[:, OFF_BQ:OFF_BK], z[:, OFF_BK:OFF_BV], z[:, OFF_BV:OFF_CQ])
        oa_ref[rows, :] = _mha(_stack_heads_gqa(aq * Q_SCALE), [(ak, _spread_kv_gqa(av), None)], tq)
        ob_ref[rows, :] = _mha(_stack_heads(bq * Q_SCALE), [(bk, bv, None)], tq)
        oc_ref[rows, :], state = _retention_core(
            z[:, OFF_CQ:OFF_CK], z[:, OFF_CK:OFF_CV] * Q_SCALE, z[:, OFF_CV:OFF_CG], z[:, OFF_CG:OFF_DU],
            dec_ref, gn_ref, dec_scr, seq_len=tq, i0=0, decay_fill="first_step" if s == 0 else "filled",
            want_state=True)
        du_ref[:, s * GROUP_WIDTH:(s + 1) * GROUP_WIDTH] = z[:, OFF_DU:]
        _store_layer_slot(ak_ref.at[s], slot, ak)
        _store_layer_slot(av_ref.at[s], slot, av)
        _store_layer_slot(bk_ref.at[s], slot, bk)
        _store_layer_slot(bv_ref.at[s], slot, bv)
        _store_retention_state(st_ref.at[s], slot, state)


def _ctx_front(x, mods, g1, w_in_bf, qn, kn, dec, gn, layer, prev):
    assert SEQ == S5_SEG
    tm = CTX_SEQS * SEQ
    n = x.shape[0]
    nb = n // SEQ
    steps = n // tm
    per_blk = SUBLANES // CTX_SEQS
    const = lambda *shape: pl.BlockSpec(shape, lambda i: (0,) * len(shape))
    row = lambda w: pl.BlockSpec((tm, w), lambda i: (i, 0))
    in_specs = [row(D_MODEL), _mod_spec(layer, 0, steps), const(1, D_MODEL),
                pl.BlockSpec((None, D_MODEL, IN_WIDTH), lambda i: (layer, 0, 0)),
                const(1, GROUP_WIDTH), const(1, KV_WIDTH), const(SUBLANES, LANES), const(1, GROUP_WIDTH)]
    args = [x, mods, g1, w_in_bf, qn, kn, dec, gn]
    out_specs = [row(GROUP_WIDTH), row(GROUP_WIDTH), row(GROUP_WIDTH),
                 pl.BlockSpec((None, S5_SEG, CTX_SEQS * GROUP_WIDTH), lambda i: (i // per_blk, 0, i % per_blk))]
    out_shape = [jax.ShapeDtypeStruct((n, GROUP_WIDTH), BF16)] * 3 + [
        jax.ShapeDtypeStruct((nb // SUBLANES, S5_SEG, SUBLANES * GROUP_WIDTH), F32)]
    first = prev is None
    slot = 0
    for tail in ((SEQ, KV_WIDTH), (SEQ, KV_WIDTH), (SEQ, GROUP_WIDTH), (SEQ, GROUP_WIDTH),
                 (2, C_HEADS, HEAD_DIM, HEAD_DIM)):
        blk, idx, slot = _layer_slot_block(layer, first, tail)
        out_specs.append(pl.BlockSpec((CTX_SEQS,) + blk[1:], lambda i, idx=idx: (i,) + idx))
        out_shape.append(jax.ShapeDtypeStruct((nb, DEPTH) + tail, F32))
    aliases = {}
    if not first:
        for k, arr in enumerate(prev):
            aliases[len(args)] = 4 + k
            in_specs.append(pl.BlockSpec(memory_space=pl.ANY))
            args.append(arr)
    outs = pl.pallas_call(
        functools.partial(_ctx_front_kernel, n_alias=len(aliases), slot=slot),
        grid=(steps,),
        in_specs=in_specs,
        out_specs=out_specs,
        out_shape=out_shape,
        scratch_shapes=[pltpu.VMEM((C_HEADS * SEQ, SEQ), F32)],
        input_output_aliases=aliases,
        compiler_params=_cparams("arbitrary"),
        name="ctx_front",
    )(*args)
    return outs[0], outs[1], outs[2], outs[3], tuple(outs[4:])


def _s5_prep_kernel(lre_ref, lim_ref, ldt_ref, bre_ref, bim_ref, cre_ref, cim_ref,
                    a_ref, bm_ref, cro_ref, cio_ref, bm_scr, cr_scr, ci_scr):
    lre = lre_ref[...]
    lim = lim_ref[...]
    dt = jnp.exp(ldt_ref[...])
    mag = jnp.exp(lre * dt)
    a_re = mag * jnp.cos(lim * dt)
    a_im = mag * jnp.sin(lim * dt)
    den = lre * lre + lim * lim
    r_re = ((a_re - 1.0) * lre + a_im * lim) / den
    r_im = (a_im * lre - (a_re - 1.0) * lim) / den
    bm_scr[...] = jnp.zeros_like(bm_scr)
    cr_scr[...] = jnp.zeros_like(cr_scr)
    ci_scr[...] = jnp.zeros_like(ci_scr)
    for g in range(S5_GROUPS):
        rows = slice(g * S5_CH, (g + 1) * S5_CH)
        cols = slice(g * S5_STATE, (g + 1) * S5_STATE)
        a_ref[0:1, cols] = a_re[g:g + 1, :]
        a_ref[1:2, cols] = a_im[g:g + 1, :]
        rr, ri = r_re[g:g + 1, :], r_im[g:g + 1, :]
        br, bi = bre_ref[g], bim_ref[g]
        bm_scr[rows, cols] = rr * br - ri * bi
        bm_scr[rows, S5_SP + g * S5_STATE:S5_SP + (g + 1) * S5_STATE] = rr * bi + ri * br
        cr_scr[cols, rows] = cre_ref[g]
        ci_scr[cols, rows] = cim_ref[g]
    bm_ref[...] = bm_scr[...].astype(BF16)
    cro_ref[...] = cr_scr[...].astype(BF16)
    cio_ref[...] = ci_scr[...].astype(BF16)


def _s5_prepare(lam_re, lam_im, log_dt, b_re, b_im, c_re, c_im):
    gp = (S5_GROUPS, S5_STATE)
    ldt = jnp.broadcast_to(log_dt[..., None], (DEPTH, 2) + gp)
    bt = [jnp.swapaxes(t, -1, -2) for t in (b_re, b_im)]
    ct = [jnp.swapaxes(t, -1, -2) for t in (c_re, c_im)]

    def spec(*tail):
        return pl.BlockSpec((None, None) + tail, lambda l, d: (l, d) + (0,) * len(tail))

    return pl.pallas_call(
        _s5_prep_kernel,
        grid=(DEPTH, 2),
        in_specs=[spec(*gp)] * 3 + [spec(S5_GROUPS, S5_CH, S5_STATE)] * 2 + [spec(S5_GROUPS, S5_STATE, S5_CH)] * 2,
        out_specs=[spec(2, S5_SP), spec(GROUP_WIDTH, 2 * S5_SP), spec(S5_SP, GROUP_WIDTH), spec(S5_SP, GROUP_WIDTH)],
        out_shape=[jax.ShapeDtypeStruct((DEPTH, 2, 2, S5_SP), F32),
                   jax.ShapeDtypeStruct((DEPTH, 2, GROUP_WIDTH, 2 * S5_SP), BF16),
                   jax.ShapeDtypeStruct((DEPTH, 2, S5_SP, GROUP_WIDTH), BF16),
                   jax.ShapeDtypeStruct((DEPTH, 2, S5_SP, GROUP_WIDTH), BF16)],
        scratch_shapes=[pltpu.VMEM((GROUP_WIDTH, 2 * S5_SP), F32), pltpu.VMEM((S5_SP, GROUP_WIDTH), F32),
                        pltpu.VMEM((S5_SP, GROUP_WIDTH), F32)],
        compiler_params=_cparams("parallel", "parallel"),
        name="s5_prepare",
    )(lam_re, lam_im, ldt, bt[0], bt[1], ct[0], ct[1])


def _cmul(ar, ai, br, bi):
    return ar * br - ai * bi, ar * bi + ai * br


def _s5_kernel(u_ref, h0_ref, a_ref, bm_ref, cre_ref, cim_ref, dvec_ref, glu_ref, *rest, nseg, slot):
    od_ref, fin_ref, x_scr, s_scr, y_scr = rest[-5:]
    steps = S5_SEG
    rows = steps * SUBLANES
    chunk = S5_CHUNK
    chunk_steps = chunk // SUBLANES
    nchunk = rows // chunk
    seg = lax.broadcasted_iota(jnp.int32, (SUBLANES, S5_SP), 0) % nseg

    for d in range(2):
        ar = jnp.broadcast_to(a_ref[d, 0:1, :], (SUBLANES, S5_SP))
        ai = jnp.broadcast_to(a_ref[d, 1:2, :], (SUBLANES, S5_SP))

        def row0(k):
            c = k if d == 0 else nchunk - 1 - k
            return c * chunk if isinstance(c, int) else pl.multiple_of(c * chunk, chunk)

        def input_part(k, buf):
            x_scr[buf] = jnp.dot(u_ref[pl.ds(row0(k), chunk), :].astype(BF16), bm_ref[d],
                                 preferred_element_type=F32)

        def scan_part(buf, carry, store):
            sr, si = carry
            for t in range(chunk_steps):
                r = (t if d == 0 else chunk_steps - 1 - t) * SUBLANES
                pr, pi = _cmul(ar, ai, sr, si)
                sr = pr + x_scr[buf, r:r + SUBLANES, 0:S5_SP]
                si = pi + x_scr[buf, r:r + SUBLANES, S5_SP:]
                if store:
                    s_scr[buf, r:r + SUBLANES, 0:S5_SP] = sr
                    s_scr[buf, r:r + SUBLANES, S5_SP:] = si
            return sr, si

        def output_part(k, buf):
            y = _bdot(s_scr[buf, :, 0:S5_SP], cre_ref[d]) - _bdot(s_scr[buf, :, S5_SP:], cim_ref[d])
            rows_k = pl.ds(row0(k), chunk)
            if d == 0:
                y_scr[rows_k, :] = y
            else:
                zz = jax.nn.gelu(y_scr[rows_k, :] + y + dvec_ref[...] * u_ref[rows_k, :])
                od_ref[rows_k, :] = (zz * jax.nn.sigmoid(_bdot(zz, glu_ref[...]))).astype(BF16)

        def half(k, buf, carry, store, nxt=True, prev=True):
            if nxt:
                input_part(k + 1, 1 - buf)
            carry = scan_part(buf, carry, store)
            if store and prev:
                output_part(k - 1, 1 - buf)
            return carry

        def run_pass(carry, store):
            input_part(0, 0)
            carry = half(0, 0, carry, store, prev=False)
            carry = half(1, 1, carry, store)

            def pair(j, c):
                c = half(2 * j, 0, c, store)
                return half(2 * j + 1, 1, c, store)
            carry = lax.fori_loop(1, nchunk // 2 - 1, pair, carry)
            carry = half(nchunk - 2, 0, carry, store)
            carry = half(nchunk - 1, 1, carry, store, nxt=False)
            if store:
                output_part(nchunk - 1, 1)
            return carry

        init = (h0_ref[d, :, 0:S5_SP], h0_ref[d, :, S5_SP:])
        if nseg > 1:
            zero = jnp.zeros((SUBLANES, S5_SP), F32)
            fr, fi = run_pass((zero, zero), store=False)
            pr, pi = ar, ai
            for _ in range(int(math.log2(steps))):
                pr, pi = _cmul(pr, pi, pr, pi)
            cr, ci = init
            shift = 1 if d == 0 else SUBLANES - 1
            order = range(1, nseg) if d == 0 else range(nseg - 2, -1, -1)
            for s in order:
                ncr, nci = pltpu.roll(cr, shift, 0), pltpu.roll(ci, shift, 0)
                nfr, nfi = pltpu.roll(fr, shift, 0), pltpu.roll(fi, shift, 0)
                qr, qi = _cmul(pr, pi, ncr, nci)
                cr = jnp.where(seg == s, qr + nfr, cr)
                ci = jnp.where(seg == s, qi + nfi, ci)
            init = (cr, ci)
        sr, si = run_pass(init, store=True)
        for s in range(fin_ref.shape[1] // (4 * S5_SP)):
            base = (4 * s + 2 * d) * S5_SP
            fin_ref[:, base:base + S5_SP] = sr if s == slot else jnp.zeros_like(sr)
            fin_ref[:, base + S5_SP:base + 2 * S5_SP] = si if s == slot else jnp.zeros_like(si)


def _s5(du_tm, h0, a, bmat, cre, cim, dvec, glu_bf, layer, *, nseg, fin_layer=0, fin_layers=1,
        prev_fin=None):
    nblk = du_tm.shape[0]
    rows = S5_SEG * SUBLANES
    fin_w = 4 * S5_SP
    in_specs = [pl.BlockSpec((None, rows, GROUP_WIDTH), lambda i: (i, 0, 0)),
                pl.BlockSpec((2, SUBLANES, 2 * S5_SP), lambda i: (0, 0, 0)),
                pl.BlockSpec((None, 2, 2, S5_SP), lambda i: (layer, 0, 0, 0)),
                pl.BlockSpec((None, 2, GROUP_WIDTH, 2 * S5_SP), lambda i: (layer, 0, 0, 0)),
                pl.BlockSpec((None, 2, S5_SP, GROUP_WIDTH), lambda i: (layer, 0, 0, 0)),
                pl.BlockSpec((None, 2, S5_SP, GROUP_WIDTH), lambda i: (layer, 0, 0, 0)),
                pl.BlockSpec((1, GROUP_WIDTH), lambda i: (0, 0)),
                pl.BlockSpec((None, GROUP_WIDTH, GROUP_WIDTH), lambda i: (layer, 0, 0))]
    args = [du_tm.reshape(nblk, rows, GROUP_WIDTH), h0, a, bmat, cre, cim, dvec, glu_bf]
    aliases = {}
    if prev_fin is not None:
        aliases[len(args)] = 1
        in_specs.append(pl.BlockSpec(memory_space=pl.ANY))
        args.append(prev_fin)
        fin_spec, slot = pl.BlockSpec((SUBLANES, fin_w), lambda i: (i, fin_layer)), 0
    else:
        fin_spec, slot = pl.BlockSpec((SUBLANES, fin_layers * fin_w), lambda i: (i, 0)), fin_layer
    od, fin = pl.pallas_call(
        functools.partial(_s5_kernel, nseg=nseg, slot=slot),
        grid=(nblk,),
        in_specs=in_specs,
        out_specs=[pl.BlockSpec((None, rows, GROUP_WIDTH), lambda i: (i, 0, 0)), fin_spec],
        out_shape=[jax.ShapeDtypeStruct((nblk, rows, GROUP_WIDTH), BF16),
                   jax.ShapeDtypeStruct((nblk * SUBLANES, fin_layers * fin_w), F32)],
        scratch_shapes=[pltpu.VMEM((2, S5_CHUNK, 2 * S5_SP), F32), pltpu.VMEM((2, S5_CHUNK, 2 * S5_SP), F32),
                        pltpu.VMEM((rows, GROUP_WIDTH), F32)],
        input_output_aliases=aliases,
        compiler_params=_cparams("parallel"),
        name="s5",
    )(*args)
    return od.reshape(nblk, S5_SEG, SUBLANES * GROUP_WIDTH), fin


ROUTE_GROUP = MOE_PER_GROUP
OUT_SEQS = 2
MOE_TILE = 512


def _out_kernel(x_ref, oa_ref, ob_ref, oc_ref, od_ref, mod_ref, wo_ref, g2_ref, wrh_ref, wrl_ref, br_ref,
                xm_ref, h2_ref, route_ref, cnt_ref):
    od = jnp.concatenate([od_ref[:, s * GROUP_WIDTH:(s + 1) * GROUP_WIDTH] for s in range(OUT_SEQS)], axis=0)
    mix = functools.reduce(jnp.add, [
        _bdot(o, wo_ref[i * GROUP_WIDTH:(i + 1) * GROUP_WIDTH, :])
        for i, o in enumerate((oa_ref[...], ob_ref[...], oc_ref[...], od))])
    xm = x_ref[...] + mod_ref[2:3, :] * mix
    xm_ref[...] = xm
    h2 = _rms_rows(xm) * g2_ref[...] * (1.0 + mod_ref[4:5, :]) + mod_ref[3:4, :]
    h2_ref[...] = h2.astype(BF16)

    h_hi, h_lo = _split(h2)
    logits = (jnp.dot(h_hi, wrh_ref[...], preferred_element_type=F32)
              + jnp.dot(h_hi, wrl_ref[...], preferred_element_type=F32)
              + jnp.dot(h_lo, wrh_ref[...], preferred_element_type=F32)) + br_ref[...]
    lane_i = lax.broadcasted_iota(jnp.int32, logits.shape, 1)
    lane = lane_i.astype(F32)
    big = jnp.float32(2 ** 30)
    gmask = lane_i < MOE_GROUPS
    gl = jnp.where(gmask, logits, -jnp.inf)
    gmax = jnp.max(gl, axis=-1, keepdims=True)
    p_top = 1.0 / jnp.sum(jnp.exp(gl - gmax), axis=-1, keepdims=True)
    g_top = jnp.min(jnp.where(gl == gmax, lane, big), axis=-1, keepdims=True)
    e_lane = lane_i - ROUTER_OFF
    lane_group = (e_lane // MOE_PER_GROUP).astype(F32)
    emask = (e_lane >= 0) & (e_lane < MOE_EXPERTS) & (lane_group == g_top)
    el = jnp.where(emask, logits, -jnp.inf)
    m1 = jnp.max(el, axis=-1, keepdims=True)
    i1 = jnp.min(jnp.where(el == m1, lane, big), axis=-1, keepdims=True)
    el2 = jnp.where(lane == i1, -jnp.inf, el)
    m2 = jnp.max(el2, axis=-1, keepdims=True)
    i2 = jnp.min(jnp.where(el2 == m2, lane, big), axis=-1, keepdims=True)
    e2 = jnp.exp(m2 - m1)
    den = 1.0 + e2
    gates = (jnp.where(lane == i1, (1.0 / den) * p_top, 0.0)
             + jnp.where(lane == i2, (e2 / den) * p_top, 0.0))
    route = jnp.where(lane == ROUTE_GROUP + g_top, 1.0, 0.0)
    for g in range(MOE_GROUPS):
        local = pltpu.roll(gates, LANES - ROUTER_OFF - g * MOE_PER_GROUP, 1)
        route = route + jnp.where((g_top == g) & (lane_i < MOE_PER_GROUP), local, 0.0)
    route_ref[...] = route
    for t in range(cnt_ref.shape[0]):
        part = jnp.sum(route[t * MOE_TILE:(t + 1) * MOE_TILE], axis=0, keepdims=True)
        cnt_ref[t] = jnp.broadcast_to(part, (SUBLANES, LANES)).astype(jnp.int32)


def _output_stage(x, mixes, mods, mod_row, mod_tokens, wo_bf, g2, wr_hi, wr_lo, br, layer):
    tm = OUT_SEQS * S5_SEG
    n = x.shape[0]
    row = lambda w: pl.BlockSpec((tm, w), lambda i: (i, 0))
    const = lambda shape: pl.BlockSpec(shape, lambda i: (0,) * len(shape))
    per_blk = SUBLANES // OUT_SEQS
    return pl.pallas_call(
        _out_kernel,
        grid=(n // tm,),
        in_specs=[row(D_MODEL), row(GROUP_WIDTH), row(GROUP_WIDTH), row(GROUP_WIDTH),
                  pl.BlockSpec((None, S5_SEG, OUT_SEQS * GROUP_WIDTH), lambda i: (i // per_blk, 0, i % per_blk)),
                  _mod_spec(layer, mod_row, mod_tokens // tm),
                  pl.BlockSpec((None, D_MODEL, D_MODEL), lambda i: (layer, 0, 0)), const((1, D_MODEL)),
                  const((D_MODEL, LANES)), const((D_MODEL, LANES)), const((1, LANES))],
        out_specs=[row(D_MODEL), row(D_MODEL), row(LANES),
                   pl.BlockSpec((tm // MOE_TILE, SUBLANES, LANES), lambda i: (i, 0, 0))],
        out_shape=[jax.ShapeDtypeStruct((n, D_MODEL), F32),
                   jax.ShapeDtypeStruct((n, D_MODEL), BF16),
                   jax.ShapeDtypeStruct((n, LANES), F32),
                   jax.ShapeDtypeStruct((n // MOE_TILE, SUBLANES, LANES), jnp.int32)],
        compiler_params=_cparams("parallel"),
        name="output_stage",
    )(x, *mixes, mods, wo_bf, g2, wr_hi, wr_lo, br)


GROUP_HID = MOE_PER_GROUP * MOE_HIDDEN


MOE_CHUNK = 160


def _moe_kernel(cnt_ref, h2_ref, route_ref, xm_ref, mod_ref, w1_ref, w3_ref, w2_ref, fg_ref, o_ref,
                hs_scr, rs_scr, os_scr, before_scr, *, final, tm):
    i = pl.program_id(0)
    off1 = cnt_ref[i, 0]
    off2 = off1 + cnt_ref[i, 1]
    off3 = off2 + cnt_ref[i, 2]
    starts = (jnp.int32(0), off1, off2, off3)
    ends = (off1, off2, off3, jnp.int32(tm))

    route = route_ref[...]
    r_hi, r_lo = _split(route)
    pick = (lax.broadcasted_iota(jnp.int32, (SUBLANES, LANES), 1)
            == ROUTE_GROUP + lax.broadcasted_iota(jnp.int32, (SUBLANES, LANES), 0))
    gt = lax.dot_general(jnp.where(pick, 1.0, 0.0).astype(BF16), r_hi, (((1,), (1,)), ((), ())),
                         preferred_element_type=F32)
    @pl.when(i == 0)
    def _():
        before_scr[...] = jnp.where(lax.broadcasted_iota(jnp.int32, (tm, tm), 0)
                                    < lax.broadcasted_iota(jnp.int32, (tm, tm), 1), 1.0, 0.0).astype(BF16)

    rank = jnp.dot(gt.astype(BF16), before_scr[...], preferred_element_type=F32)
    gt_i = gt.astype(jnp.int32)
    rank_i = rank.astype(jnp.int32)
    pos = jnp.zeros((1, tm), jnp.int32)
    for g in range(MOE_GROUPS):
        pos = pos + gt_i[g:g + 1, :] * (rank_i[g:g + 1, :] + starts[g])
    perm = jnp.where(lax.broadcasted_iota(jnp.int32, (tm, tm), 0) == pos, 1.0, 0.0).astype(BF16)
    hs_scr[...] = jnp.dot(perm, h2_ref[...], preferred_element_type=F32).astype(BF16)
    rs_scr[...] = (jnp.dot(perm, r_hi, preferred_element_type=F32)
                   + jnp.dot(perm, r_lo, preferred_element_type=F32))

    os_scr[...] = jnp.zeros_like(os_scr)
    for g in range(MOE_GROUPS):
        lo, hi = starts[g], ends[g]
        base = (lo // BF16_ROWS) * BF16_ROWS
        n_chunks = jnp.where(hi > lo, (hi - base + MOE_CHUNK - 1) // MOE_CHUNK, 0)

        def chunk_body(c, carry, g=g, lo=lo, hi=hi, base=base):
            r0 = pl.multiple_of(jnp.minimum(base + c * MOE_CHUNK, tm - MOE_CHUNK), BF16_ROWS)
            rows = pl.ds(r0, MOE_CHUNK)
            x = hs_scr[rows, :]
            gates = rs_scr[rows, :]
            a = jnp.dot(x, w1_ref[g], preferred_element_type=F32)
            b = jnp.dot(x, w3_ref[g], preferred_element_type=F32)
            hid = []
            for e in range(MOE_PER_GROUP):
                sl = slice(e * MOE_HIDDEN, (e + 1) * MOE_HIDDEN)
                hid.append((jax.nn.silu(a[:, sl]) * b[:, sl] * gates[:, e:e + 1]).astype(BF16))
            y = jnp.dot(jnp.concatenate(hid, axis=1), w2_ref[g], preferred_element_type=F32)
            rowid = r0 + lax.broadcasted_iota(jnp.int32, (MOE_CHUNK, 1), 0)
            member = (rowid >= lo) & (rowid < hi)
            os_scr[rows, :] = jnp.where(member, y, os_scr[rows, :])
            return carry

        lax.fori_loop(0, n_chunks, chunk_body, 0)

    o_hi, o_lo = _split(os_scr[...])
    moe = (lax.dot_general(perm, o_hi, (((0,), (0,)), ((), ())), preferred_element_type=F32)
           + lax.dot_general(perm, o_lo, (((0,), (0,)), ((), ())), preferred_element_type=F32))
    out = xm_ref[...] + mod_ref[5:6, :] * moe
    if final:
        out = _rms_rows(out) * fg_ref[...]
    o_ref[...] = out


def _moe_weight_kernel(w1_ref, w3_ref, w2_ref, o1_ref, o3_ref, o2_ref):
    for e in range(MOE_PER_GROUP):
        sl = slice(e * MOE_HIDDEN, (e + 1) * MOE_HIDDEN)
        o1_ref[:, sl] = w1_ref[e].astype(BF16)
        o3_ref[:, sl] = w3_ref[e].astype(BF16)
        o2_ref[sl, :] = w2_ref[e].astype(BF16)


def _moe_weights(w1, w3, w2):
    up = pl.BlockSpec((None, MOE_PER_GROUP, D_MODEL, MOE_HIDDEN), lambda l, g: (l, g, 0, 0))
    down = pl.BlockSpec((None, MOE_PER_GROUP, MOE_HIDDEN, D_MODEL), lambda l, g: (l, g, 0, 0))
    out = pl.BlockSpec((None, None, D_MODEL, GROUP_HID), lambda l, g: (l, g, 0, 0))
    shape = jax.ShapeDtypeStruct((DEPTH, MOE_GROUPS, D_MODEL, GROUP_HID), BF16)
    return pl.pallas_call(
        _moe_weight_kernel,
        grid=(DEPTH, MOE_GROUPS),
        in_specs=[up, up, down],
        out_specs=[out, out, out],
        out_shape=[shape, shape, shape],
        compiler_params=_cparams("parallel", "parallel"),
        name="moe_weights",
    )(w1, w3, w2)


def _moe(h2, route, tile_counts, xm, mods, mod_row, mod_tokens, w1g, w3g, w2g, fg, layer, *, final):
    tm = MOE_TILE
    n = h2.shape[0]
    cnt = tile_counts[:, 0, ROUTE_GROUP:ROUTE_GROUP + MOE_GROUPS]
    row = lambda w: pl.BlockSpec((tm, w), lambda i, c: (i, 0))
    mod_tiles = mod_tokens // tm
    wspec = pl.BlockSpec((None, MOE_GROUPS, D_MODEL, GROUP_HID), lambda i, c: (layer, 0, 0, 0),
                         pipeline_mode=pl.Buffered(1))
    return pl.pallas_call(
        functools.partial(_moe_kernel, final=final, tm=tm),
        grid_spec=pltpu.PrefetchScalarGridSpec(
            num_scalar_prefetch=1,
            grid=(n // tm,),
            in_specs=[row(D_MODEL), row(LANES), row(D_MODEL),
                      pl.BlockSpec((None, None, N_MOD, D_MODEL),
                                   lambda i, c: (layer, mod_row + i // mod_tiles, 0, 0)),
                      wspec, wspec, wspec,
                      pl.BlockSpec((1, D_MODEL), lambda i, c: (0, 0))],
            out_specs=row(D_MODEL),
            scratch_shapes=[pltpu.VMEM((tm, D_MODEL), BF16), pltpu.VMEM((tm, LANES), F32),
                            pltpu.VMEM((tm, D_MODEL), F32), pltpu.VMEM((tm, tm), BF16)]),
        out_shape=jax.ShapeDtypeStruct((n, D_MODEL), F32),
        compiler_params=_cparams("arbitrary"),
        name="moe",
    )(cnt, h2, route, xm, mods, w1g, w3g, w2g, fg)


def kernel(x_prompt, x_sample, cache_a_k, cache_a_v, cache_b_k, cache_b_v, state_ret, state_ssm, c, c_ctx, mod_w, mod_b, norm1_g, norm2_g, w_in, a_qn_g, a_kn_g, b_rel_bias, ret_decay, ret_gn_g, s5_lam_re, s5_lam_im, s5_log_dt, s5_b_re, s5_b_im, s5_c_re, s5_c_im, s5_d, s5_glu_w, w_out, moe_gw, moe_gb, moe_ew, moe_eb, moe_w1, moe_w3, moe_w2, final_norm_g):
    n_ctx = BATCH * SEQ
    n_lat = DEC_BATCH * DEC_SEQ
    lat_seg = DEC_SEQ // S5_SEG

    cond = jnp.zeros((SUBLANES, D_MODEL), F32).at[0].set(c_ctx).at[1:1 + DEC_BATCH].set(c)
    mods = _modulation(cond, mod_w, mod_b).reshape(DEPTH, SUBLANES, N_MOD, D_MODEL)

    rope_tabs = _rope_tables()
    s5_a, s5_bm, s5_cre, s5_cim = _s5_prepare(s5_lam_re, s5_lam_im, s5_log_dt, s5_b_re, s5_b_im,
                                              s5_c_re, s5_c_im)
    cak = cache_a_k.reshape(DEC_BATCH, DEPTH, PAST_LEN, A_KV_HEADS * HEAD_DIM)
    cav = cache_a_v.reshape(DEC_BATCH, DEPTH, PAST_LEN, A_KV_HEADS * HEAD_DIM)
    cbk = cache_b_k.reshape(DEC_BATCH, DEPTH, PAST_LEN, B_HEADS * HEAD_DIM)
    cbv = cache_b_v.reshape(DEC_BATCH, DEPTH, PAST_LEN, B_HEADS * HEAD_DIM)

    xc = x_prompt.reshape(n_ctx, D_MODEL)
    xs = x_sample.reshape(n_lat, D_MODEL)
    w1_all, w3_all, w2_all = _moe_weights(moe_w1, moe_w3, moe_w2)
    eye_h = jnp.eye(C_HEADS, dtype=F32)
    s0_bd = (state_ret[:, :, :, :, :, None, :] * eye_h[None, None, None, :, None, :, None]).reshape(
        DEC_BATCH, DEPTH, 2, C_HEADS * HEAD_DIM, C_HEADS * HEAD_DIM)
    ctx_state = ssm_states = None
    h0_zero = jnp.zeros((2, SUBLANES, 2 * S5_SP), F32)
    w_in_bf = w_in.astype(BF16)
    wo_bf = w_out.astype(BF16)
    glu_bf = s5_glu_w.astype(BF16)
    for l in range(DEPTH):
        final = l == DEPTH - 1
        g1 = norm1_g[l].reshape(1, D_MODEL)
        g2 = norm2_g[l].reshape(1, D_MODEL)
        fg = final_norm_g.reshape(1, D_MODEL)
        qn = jnp.tile(a_qn_g[l], A_HEADS).reshape(1, GROUP_WIDTH)
        kn = jnp.tile(a_kn_g[l], A_KV_HEADS).reshape(1, KV_WIDTH)
        dec = jnp.broadcast_to(ret_decay[l].reshape(2 * C_HEADS, 1), (2 * C_HEADS, LANES))
        gn = ret_gn_g[l].reshape(1, GROUP_WIDTH)
        dvec = s5_d[l].reshape(1, GROUP_WIDTH)
        wr = jnp.zeros((D_MODEL, LANES), F32).at[:, :MOE_GROUPS].set(moe_gw[l]).at[
            :, ROUTER_OFF:ROUTER_OFF + MOE_EXPERTS].set(moe_ew[l])
        br = jnp.zeros((1, LANES), F32).at[0, :MOE_GROUPS].set(moe_gb[l]).at[
            0, ROUTER_OFF:ROUTER_OFF + MOE_EXPERTS].set(moe_eb[l])
        wr_hi = wr.astype(BF16)
        wr_lo = (wr - wr_hi.astype(F32)).astype(BF16)
        na_bias = _na_bias(b_rel_bias[l])

        oa, ob, oc, du_tm, ctx_state = _ctx_front(xc, mods, g1, w_in_bf, qn, kn, dec, gn, l, ctx_state)
        od_tm, ssm_states = _s5(du_tm, h0_zero, s5_a, s5_bm, s5_cre, s5_cim, dvec, glu_bf, l,
                                nseg=1, fin_layer=l, fin_layers=DEPTH, prev_fin=ssm_states)
        xm, h2, route, counts = _output_stage(xc, (oa, ob, oc, od_tm), mods, 0, n_ctx, wo_bf, g2,
                                              wr_hi, wr_lo, br, l)
        xc = _moe(h2, route, counts, xm, mods, 0, n_ctx, w1_all, w3_all, w2_all, fg, l, final=final)

        zs, cg, du_tm = _project(xs, mods, 1, DEC_SEQ, g1, w_in_bf, qn, kn, rope_tabs, l, seq_len=DEC_SEQ)
        zs3 = zs.reshape(DEC_BATCH, DEC_SEQ, OFF_CG)
        oa = _lat_attention_a(zs3, cak, cav, l).reshape(n_lat, GROUP_WIDTH)
        ob = _lat_attention_b(zs3, cbk, cbv, na_bias, l).reshape(n_lat, GROUP_WIDTH)
        oc = _retention(zs3, cg.reshape(DEC_BATCH, DEC_SEQ, GROUP_WIDTH), dec, gn, s0_bd, l).reshape(
            n_lat, GROUP_WIDTH)
        h0 = state_ssm[:, l].reshape(DEC_BATCH, 2, 2 * S5_SP).transpose(1, 0, 2)
        h0_seg = jnp.zeros((2, DEC_BATCH, lat_seg, 2 * S5_SP), F32)
        h0_seg = h0_seg.at[0, :, 0].set(h0[0]).at[1, :, lat_seg - 1].set(h0[1])
        od_tm, _ = _s5(du_tm, h0_seg.reshape(2, SUBLANES, 2 * S5_SP),
                       s5_a, s5_bm, s5_cre, s5_cim, dvec, glu_bf, l, nseg=lat_seg)
        xm, h2, route, counts = _output_stage(xs, (oa, ob, oc, od_tm), mods, 1, DEC_SEQ, wo_bf, g2,
                                              wr_hi, wr_lo, br, l)
        xs = _moe(h2, route, counts, xm, mods, 1, DEC_SEQ, w1_all, w3_all, w2_all, fg, l, final=final)

    new_ak, new_av, new_bk, new_bv, ret_states = ctx_state
    return (xc.reshape(BATCH, SEQ, D_MODEL), xs.reshape(DEC_BATCH, DEC_SEQ, D_MODEL),
            new_ak.reshape(BATCH, DEPTH, SEQ, A_KV_HEADS, HEAD_DIM),
            new_av.reshape(BATCH, DEPTH, SEQ, A_KV_HEADS, HEAD_DIM),
            new_bk.reshape(BATCH, DEPTH, SEQ, B_HEADS, HEAD_DIM),
            new_bv.reshape(BATCH, DEPTH, SEQ, B_HEADS, HEAD_DIM),
            ret_states,
            ssm_states.reshape(BATCH, DEPTH, 2, 2, S5_GROUPS, S5_STATE))
```

```python
import functools
import math

import numpy as np
import jax
import jax.numpy as jnp
from jax import lax
from jax.experimental import pallas as pl
from jax.experimental.pallas import tpu as pltpu

F32 = jnp.float32
BF16 = jnp.bfloat16

D_MODEL = 1024
BATCH = 32
SEQ = 256
DEPTH = 2
DEC_BATCH = 2
DEC_SEQ = 1024
PAST_LEN = 256
GRID_W = 64
HEAD_DIM = 64
GROUP_WIDTH = 256
A_HEADS = 4
A_KV_HEADS = 2
B_HEADS = 4
NA_ROWS = 8
NA_COLS = 16
C_HEADS = 4
S5_CH = 16
S5_GROUPS = 16
S5_STATE = 64
MOE_GROUPS = 4
MOE_PER_GROUP = 8
MOE_EXPERTS = 32
MOE_HIDDEN = 128
ROPE_THETA = 10000.0
EPS = 1e-6
IN_WIDTH = 2560
Q_SCALE = HEAD_DIM ** -0.5
KV_WIDTH = A_KV_HEADS * HEAD_DIM
N_MOD = 6
ROPE_PAIR = HEAD_DIM // 4
LAT_TQ = 256
MOD_TILE = 1536

OFF_AQ, OFF_AK, OFF_AV = 0, 256, 384
OFF_BQ, OFF_BK, OFF_BV = 512, 768, 1024
OFF_CQ, OFF_CK, OFF_CV, OFF_CG = 1280, 1536, 1792, 2048
OFF_DU = 2304

LANES = 128
SUBLANES = 8
BF16_ROWS = 16
S5_SP = S5_GROUPS * S5_STATE
S5_SEG = 256
S5_CHUNK = 256
ROUTER_OFF = 4
NEG_BIG = -1e30
VMEM_LIMIT = 56 * 1024 * 1024


def _cparams(*sem):
    return pltpu.CompilerParams(dimension_semantics=sem, vmem_limit_bytes=VMEM_LIMIT)


def _mod_spec(layer, first_row, tiles_per_row):
    return pl.BlockSpec((None, None, N_MOD, D_MODEL), lambda i: (layer, first_row + i // tiles_per_row, 0, 0))


def _bdot(a, b):
    return jnp.dot(a.astype(BF16), b.astype(BF16), preferred_element_type=F32)


def _bdot_nt(a, b):
    return lax.dot_general(a.astype(BF16), b.astype(BF16), (((1,), (1,)), ((), ())),
                           preferred_element_type=F32)


def _bdot_tn(a, b):
    return lax.dot_general(a.astype(BF16), b.astype(BF16), (((0,), (0,)), ((), ())),
                           preferred_element_type=F32)


def _split(a):
    hi = a.astype(BF16)
    lo = (a - hi.astype(F32)).astype(BF16)
    return hi, lo


def _dot_hilo_lhs(a, b_bf16):
    hi, lo = _split(a)
    return (jnp.dot(hi, b_bf16, preferred_element_type=F32)
            + jnp.dot(lo, b_bf16, preferred_element_type=F32))


def _rms_rows(x):
    return x * lax.rsqrt(jnp.mean(x * x, axis=-1, keepdims=True) + EPS)


def _mod_kernel(cond_ref, w_ref, b_ref, o_ref):
    o_ref[...] = _bdot(jax.nn.silu(cond_ref[...]), w_ref[...]) + b_ref[...]


def _modulation(cond, mod_w, mod_b):
    tn = MOD_TILE
    width = N_MOD * D_MODEL
    return pl.pallas_call(
        _mod_kernel,
        grid=(DEPTH, width // tn),
        in_specs=[pl.BlockSpec((SUBLANES, D_MODEL), lambda l, j: (0, 0)),
                  pl.BlockSpec((None, D_MODEL, tn), lambda l, j: (l, 0, j)),
                  pl.BlockSpec((None, 1, tn), lambda l, j: (l, 0, j))],
        out_specs=pl.BlockSpec((None, SUBLANES, tn), lambda l, j: (l, 0, j)),
        out_shape=jax.ShapeDtypeStruct((DEPTH, SUBLANES, width), F32),
        compiler_params=_cparams("arbitrary", "arbitrary"),
        name="modulation",
    )(cond, mod_w, mod_b.reshape(DEPTH, 1, width))


def _group_mean_matrix(w):
    ri = lax.broadcasted_iota(jnp.int32, (w, w), 0) // HEAD_DIM
    ci = lax.broadcasted_iota(jnp.int32, (w, w), 1) // HEAD_DIM
    return jnp.where(ri == ci, 1.0 / HEAD_DIM, 0.0).astype(BF16)


def _head_norm(t, g):
    ms = _dot_hilo_lhs(t * t, _group_mean_matrix(t.shape[1]))
    return t * lax.rsqrt(ms + EPS) * g


def _rope(t, cos, sa, sb):
    return (t * cos + pltpu.roll(t, LANES - ROPE_PAIR, 1) * sa + pltpu.roll(t, ROPE_PAIR, 1) * sb)


def _store_layer_slot(ref, slot, value):
    for s in range(ref.shape[0]):
        ref[s] = value if s == slot else jnp.zeros_like(value)


def _layer_slot_block(layer, first_call, tail):
    if first_call:
        return (None, DEPTH) + tail, (0,) * (1 + len(tail)), layer
    return (None, 1) + tail, (layer,) + (0,) * len(tail), 0


def _proj_kernel(x_ref, mod_ref, g1_ref, w_ref, qn_ref, kn_ref, cos_ref, sa_ref, sb_ref, z_ref, cg_ref, du_ref):
    h = _rms_rows(x_ref[...]) * g1_ref[...] * (1.0 + mod_ref[1:2, :]) + mod_ref[0:1, :]
    z = jnp.dot(h.astype(BF16), w_ref[...], preferred_element_type=F32)
    aq = _head_norm(z[:, OFF_AQ:OFF_AK], qn_ref[...])
    ak = _head_norm(z[:, OFF_AK:OFF_AV], kn_ref[...])
    for j in range(3):
        t = aq[:, j * LANES:(j + 1) * LANES] if j < 2 else ak
        sl = slice(0, LANES) if j == 2 else slice(j * LANES, (j + 1) * LANES)
        t = _rope(t, cos_ref[:, sl], sa_ref[:, sl], sb_ref[:, sl])
        z_ref[:, j * LANES:(j + 1) * LANES] = t.astype(BF16)
    z_ref[:, OFF_AV:OFF_CK] = z[:, OFF_AV:OFF_CK].astype(BF16)
    z_ref[:, OFF_CK:OFF_CV] = (z[:, OFF_CK:OFF_CV] * Q_SCALE).astype(BF16)
    z_ref[:, OFF_CV:OFF_CG] = z[:, OFF_CV:OFF_CG].astype(BF16)
    cg_ref[...] = z[:, OFF_CG:OFF_DU]
    du_ref[...] = z[:, OFF_DU:]


def _du_spec():
    return pl.BlockSpec((None, S5_SEG, GROUP_WIDTH), lambda i: (i // SUBLANES, 0, i % SUBLANES))


def _project(x, mods, mod_row, mod_tokens, g1, w_in_bf, qn, kn, rope_tabs, layer, *, seq_len):
    tm = S5_SEG
    n = x.shape[0]
    tps = seq_len // tm
    return pl.pallas_call(
        _proj_kernel,
        grid=(n // tm,),
        in_specs=[pl.BlockSpec((tm, D_MODEL), lambda i: (i, 0)),
                  _mod_spec(layer, mod_row, mod_tokens // tm),
                  pl.BlockSpec((1, D_MODEL), lambda i: (0, 0)),
                  pl.BlockSpec((None, D_MODEL, IN_WIDTH), lambda i: (layer, 0, 0)),
                  pl.BlockSpec((1, GROUP_WIDTH), lambda i: (0, 0)),
                  pl.BlockSpec((1, KV_WIDTH), lambda i: (0, 0))]
                 + [pl.BlockSpec((tm, GROUP_WIDTH), lambda i: (i % tps, 0))] * 3,
        out_specs=[pl.BlockSpec((tm, OFF_CG), lambda i: (i, 0)),
                   pl.BlockSpec((tm, GROUP_WIDTH), lambda i: (i, 0)), _du_spec()],
        out_shape=[jax.ShapeDtypeStruct((n, OFF_CG), BF16),
                   jax.ShapeDtypeStruct((n, GROUP_WIDTH), F32),
                   jax.ShapeDtypeStruct((n // (tm * SUBLANES), S5_SEG, SUBLANES * GROUP_WIDTH), F32)],
        compiler_params=_cparams("parallel"),
        name="project",
    )(x, mods, g1, w_in_bf, qn, kn, *rope_tabs)


def _rope_tables():
    t = jnp.arange(DEC_SEQ)
    row = (t // GRID_W).astype(F32)
    col = (t % GRID_W).astype(F32)
    nf = HEAD_DIM // 4
    inv = ROPE_THETA ** (-jnp.arange(nf, dtype=F32) / nf)
    ang_r = row[:, None] * inv[None, :]
    ang_c = col[:, None] * inv[None, :]
    zeros = jnp.zeros_like(ang_r)
    cos = jnp.concatenate([jnp.cos(ang_r), jnp.cos(ang_r), jnp.cos(ang_c), jnp.cos(ang_c)], axis=-1)
    sa = jnp.concatenate([-jnp.sin(ang_r), zeros, -jnp.sin(ang_c), zeros], axis=-1)
    sb = jnp.concatenate([zeros, jnp.sin(ang_r), zeros, jnp.sin(ang_c)], axis=-1)
    return tuple(jnp.tile(a, (1, 4)) for a in (cos, sa, sb))


N_HEADS = 4


def _lane_head(width):
    return lax.broadcasted_iota(jnp.int32, (1, width), 1) // HEAD_DIM


def _stack_heads(q):
    head = _lane_head(q.shape[1])
    return jnp.concatenate([jnp.where(head == h, q, 0.0) for h in range(N_HEADS)], axis=0).astype(BF16)


def _stack_heads_gqa(q):
    lo = lax.broadcasted_iota(jnp.int32, (1, LANES), 1) < HEAD_DIM
    q = q.astype(F32)
    q01, q23 = q[:, :LANES], q[:, LANES:]
    blocks = [jnp.where(lo, q01, 0.0), jnp.where(lo, pltpu.roll(q01, HEAD_DIM, 1), 0.0),
              jnp.where(lo, 0.0, pltpu.roll(q23, HEAD_DIM, 1)), jnp.where(lo, 0.0, q23)]
    return jnp.concatenate(blocks, axis=0).astype(BF16)


def _spread_kv_gqa(v):
    lo = lax.broadcasted_iota(jnp.int32, (1, LANES), 1) < HEAD_DIM
    v = v.astype(F32)
    vr = pltpu.roll(v, HEAD_DIM, 1)
    return jnp.concatenate([jnp.where(lo, v, vr), jnp.where(lo, vr, v)], axis=1)


def _mha(qs, blocks, tq):
    scores = []
    for k, _, bias in blocks:
        s = _bdot_nt(qs, k)
        scores.append(s if bias is None else s + bias)
    m = functools.reduce(jnp.maximum, [jnp.max(s, axis=-1, keepdims=True) for s in scores])
    es = [jnp.exp(s - m) for s in scores]
    denom = functools.reduce(jnp.add, [jnp.sum(e, axis=-1, keepdims=True) for e in es])
    ps = [e.astype(BF16) for e in es]
    head = _lane_head(N_HEADS * HEAD_DIM)
    vals = [v.astype(BF16) for _, v, _ in blocks]
    o = None
    dall = None
    for h in range(N_HEADS):
        rows = slice(h * tq, (h + 1) * tq)
        for p, v in zip(ps, vals):
            t = jnp.dot(p[rows], jnp.where(head == h, v, jnp.zeros_like(v)), preferred_element_type=F32)
            o = t if o is None else o + t
        d = jnp.where(head == h, denom[rows], 0.0)
        dall = d if dall is None else dall + d
    return (o / dall).astype(BF16)


def _lat_attn_a_kernel(q_ref, kn_ref, vn_ref, kc_ref, vc_ref, o_ref):
    for b in range(DEC_BATCH):
        o_ref[b] = _mha(_stack_heads_gqa(q_ref[b] * Q_SCALE),
                        [(kc_ref[b], _spread_kv_gqa(vc_ref[b]), None),
                         (kn_ref[b], _spread_kv_gqa(vn_ref[b]), None)], q_ref.shape[1])


def _lat_attention_a(z, cache_k, cache_v, layer, tq=LAT_TQ):
    cache_spec = pl.BlockSpec((DEC_BATCH, None, PAST_LEN, KV_WIDTH), lambda j: (0, layer, 0, 0))
    return pl.pallas_call(
        _lat_attn_a_kernel,
        grid=(DEC_SEQ // tq,),
        in_specs=[pl.BlockSpec((DEC_BATCH, tq, GROUP_WIDTH), lambda j: (0, j, OFF_AQ // GROUP_WIDTH)),
                  pl.BlockSpec((DEC_BATCH, DEC_SEQ, KV_WIDTH), lambda j: (0, 0, OFF_AK // KV_WIDTH)),
                  pl.BlockSpec((DEC_BATCH, DEC_SEQ, KV_WIDTH), lambda j: (0, 0, OFF_AV // KV_WIDTH)),
                  cache_spec, cache_spec],
        out_specs=pl.BlockSpec((DEC_BATCH, tq, GROUP_WIDTH), lambda j: (0, j, 0)),
        out_shape=jax.ShapeDtypeStruct((DEC_BATCH, DEC_SEQ, GROUP_WIDTH), BF16),
        compiler_params=_cparams("parallel"),
        name="lat_attention_a",
    )(z, z, z, cache_k, cache_v)


NA_KEYS = NA_ROWS * GRID_W


NA_PAIRS = 2 * NA_ROWS - 2


NA_STEP_ROWS = 2


def _na_kernel(q_ref, k_ref, v_ref, kc_ref, vc_ref, bias_ref, o_ref):
    rows = DEC_SEQ // GRID_W
    for b in range(q_ref.shape[0]):
        outs = []
        for rr in range(NA_STEP_ROWS):
            r = pl.program_id(0) * NA_STEP_ROWS + rr
            row_start = jnp.clip(r - NA_ROWS // 2, 0, rows - NA_ROWS)
            start = pl.multiple_of(row_start * GRID_W, GRID_W)
            rel0 = row_start - r + NA_ROWS - 1
            kl = k_ref[b, pl.ds(start, NA_KEYS), :]
            vl = v_ref[b, pl.ds(start, NA_KEYS), :]
            bias = jnp.concatenate(
                [jnp.concatenate([bias_ref[h, rel0 + 2 * jp] for jp in range(NA_ROWS // 2)], axis=1)
                 for h in range(B_HEADS)], axis=0)
            qrows = slice(rr * GRID_W, (rr + 1) * GRID_W)
            outs.append(_mha(_stack_heads(q_ref[b, qrows, :] * Q_SCALE),
                             [(kl, vl, bias), (kc_ref[b], vc_ref[b], None)], GRID_W))
        o_ref[b] = jnp.concatenate(outs, axis=0)


def _na_bias(rel_bias):
    nrel = 2 * NA_COLS - 1
    period = 2 * GRID_W
    b = rel_bias.astype(F32)
    ext = jnp.concatenate([b[..., NA_COLS - 1:],
                           jnp.zeros(b.shape[:-1] + (period - nrel,), F32),
                           b[..., :NA_COLS - 1]], axis=-1)
    flat = jnp.tile(ext, (1, 1, GRID_W))[..., :GRID_W * (period - 1)]
    toe = flat.reshape(b.shape[:-1] + (GRID_W, period - 1))[..., :GRID_W]
    col_start = np.clip(np.arange(GRID_W) - NA_COLS // 2, 0, GRID_W - NA_COLS)
    kc = np.arange(GRID_W)
    inside = (kc[None, :] >= col_start[:, None]) & (kc[None, :] < col_start[:, None] + NA_COLS)
    toe = jnp.where(jnp.asarray(inside), toe, NEG_BIG)
    return jnp.concatenate([toe[:, :-1], toe[:, 1:]], axis=-1)


def _lat_attention_b(z, cache_k, cache_v, bias, layer):
    tq = NA_STEP_ROWS * GRID_W
    cache_spec = pl.BlockSpec((DEC_BATCH, None, PAST_LEN, GROUP_WIDTH), lambda r: (0, layer, 0, 0))
    return pl.pallas_call(
        _na_kernel,
        grid=(DEC_SEQ // tq,),
        in_specs=[pl.BlockSpec((DEC_BATCH, tq, GROUP_WIDTH), lambda r: (0, r, OFF_BQ // GROUP_WIDTH)),
                  pl.BlockSpec((DEC_BATCH, DEC_SEQ, GROUP_WIDTH), lambda r: (0, 0, OFF_BK // GROUP_WIDTH)),
                  pl.BlockSpec((DEC_BATCH, DEC_SEQ, GROUP_WIDTH), lambda r: (0, 0, OFF_BV // GROUP_WIDTH)),
                  cache_spec, cache_spec,
                  pl.BlockSpec((B_HEADS, NA_PAIRS, GRID_W, 2 * GRID_W), lambda r: (0, 0, 0, 0))],
        out_specs=pl.BlockSpec((DEC_BATCH, tq, GROUP_WIDTH), lambda r: (0, r, 0)),
        out_shape=jax.ShapeDtypeStruct((DEC_BATCH, DEC_SEQ, GROUP_WIDTH), BF16),
        compiler_params=_cparams("parallel"),
        name="lat_attention_b",
    )(z, z, z, cache_k, cache_v, bias)


def _retention_core(q, k, v, g, dec_ref, gn_ref, dec_scr, *, seq_len, i0, decay_fill, s0_ref=None,
                    want_state=False):
    tq = q.shape[0]
    head = _lane_head(C_HEADS * HEAD_DIM)
    lg = jax.nn.log_sigmoid(dec_ref[...])

    def per_lane(row0):
        out = jnp.zeros((1, C_HEADS * HEAD_DIM), F32)
        for h in range(C_HEADS):
            out = jnp.where(head == h, lg[row0 + h:row0 + h + 1, 0:1], out)
        return out

    lgf_l, lgb_l = per_lane(0), per_lane(C_HEADS)
    qi = (i0 + lax.broadcasted_iota(jnp.int32, (tq, 1), 0)).astype(F32)

    def fill_decay():
        kj = lax.broadcasted_iota(jnp.int32, (1, seq_len), 1).astype(F32)
        diff = qi - kj
        for h in range(C_HEADS):
            lgf = lg[h:h + 1, 0:1]
            lgb = lg[C_HEADS + h:C_HEADS + h + 1, 0:1]
            dec_scr[h * tq:(h + 1) * tq, :] = (
                jnp.where(diff >= 0, jnp.exp(lgf * jnp.maximum(diff, 0.0)), 0.0)
                + jnp.where(diff <= 0, jnp.exp(lgb * jnp.maximum(-diff, 0.0)), 0.0))

    if decay_fill == "first_step":
        pl.when(pl.program_id(0) == 0)(fill_decay)
    elif decay_fill == "every_step":
        fill_decay()
    else:
        assert decay_fill == "filled"

    v = v.astype(BF16)
    sc = (_bdot_nt(_stack_heads(q), k) * dec_scr[...]).astype(BF16)
    o = None
    for h in range(C_HEADS):
        t = jnp.dot(sc[h * tq:(h + 1) * tq], jnp.where(head == h, v, jnp.zeros_like(v)),
                    preferred_element_type=F32)
        o = t if o is None else o + t
    if s0_ref is not None:
        o = (o + _bdot(q, s0_ref[0]) * jnp.exp(lgf_l * (qi + 1.0))
             + _bdot(q, s0_ref[1]) * jnp.exp(lgb_l * (seq_len - qi)))
    gm = _group_mean_matrix(C_HEADS * HEAD_DIM)
    dlt = o - _dot_hilo_lhs(o, gm)
    var = _dot_hilo_lhs(dlt * dlt, gm)
    out = (dlt * lax.rsqrt(var + EPS) * gn_ref[...] * jax.nn.silu(g)).astype(BF16)
    if not want_state:
        return out, None
    kpos = lax.broadcasted_iota(jnp.int32, (seq_len, 1), 0).astype(F32)
    sf = _bdot_tn(k * jnp.exp(lgf_l * (seq_len - 1.0 - kpos)), v)
    sb = _bdot_tn(k * jnp.exp(lgb_l * kpos), v)
    return out, (sf, sb)


def _store_retention_state(st_ref, slot, state):
    for s in range(st_ref.shape[0]):
        for d in range(2):
            for h in range(C_HEADS):
                sl = slice(h * HEAD_DIM, (h + 1) * HEAD_DIM)
                st_ref[s, d, h] = state[d][sl, sl] if s == slot else jnp.zeros((HEAD_DIM, HEAD_DIM), F32)


def _retention_kernel(q_ref, g_ref, k_ref, v_ref, dec_ref, gn_ref, s0_ref, o_ref, dec_scr, *, seq_len, tq):
    for b in range(q_ref.shape[0]):
        o_ref[b], _ = _retention_core(q_ref[b], k_ref[b], v_ref[b], g_ref[b], dec_ref, gn_ref, dec_scr,
                                      seq_len=seq_len, i0=pl.program_id(0) * tq,
                                      decay_fill="every_step" if b == 0 else "filled", s0_ref=s0_ref.at[b])


def _retention(z, cg, dec, gn, s0, layer, *, tq=LAT_TQ):
    nb, seq_len = z.shape[:2]
    return pl.pallas_call(
        functools.partial(_retention_kernel, seq_len=seq_len, tq=tq),
        grid=(seq_len // tq,),
        in_specs=[pl.BlockSpec((nb, tq, GROUP_WIDTH), lambda j: (0, j, OFF_CQ // GROUP_WIDTH)),
                  pl.BlockSpec((nb, tq, GROUP_WIDTH), lambda j: (0, j, 0)),
                  pl.BlockSpec((nb, seq_len, GROUP_WIDTH), lambda j: (0, 0, OFF_CK // GROUP_WIDTH)),
                  pl.BlockSpec((nb, seq_len, GROUP_WIDTH), lambda j: (0, 0, OFF_CV // GROUP_WIDTH)),
                  pl.BlockSpec((SUBLANES, LANES), lambda j: (0, 0)),
                  pl.BlockSpec((1, GROUP_WIDTH), lambda j: (0, 0)),
                  pl.BlockSpec((nb, None, 2, GROUP_WIDTH, GROUP_WIDTH), lambda j: (0, layer, 0, 0, 0))],
        out_specs=pl.BlockSpec((nb, tq, GROUP_WIDTH), lambda j: (0, j, 0)),
        out_shape=jax.ShapeDtypeStruct((nb, seq_len, GROUP_WIDTH), BF16),
        scratch_shapes=[pltpu.VMEM((C_HEADS * tq, seq_len), F32)],
        compiler_params=_cparams("arbitrary"),
        name="retention",
    )(z, cg, z, z, dec, gn, s0)


CTX_SEQS = 4


def _ctx_front_kernel(x_ref, mod_ref, g1_ref, w_ref, qn_ref, kn_ref, dec_ref, gn_ref, *rest, n_alias, slot):
    (oa_ref, ob_ref, oc_ref, du_ref, ak_ref, av_ref, bk_ref, bv_ref, st_ref, dec_scr) = rest[n_alias:]
    tq = x_ref.shape[0] // CTX_SEQS
    h = _rms_rows(x_ref[...]) * g1_ref[...] * (1.0 + mod_ref[1:2, :]) + mod_ref[0:1, :]
    zz = jnp.dot(h.astype(BF16), w_ref[...], preferred_element_type=F32)
    for s in range(CTX_SEQS):
        rows = slice(s * tq, (s + 1) * tq)
        z = zz[rows, :]
        aq = _head_norm(z[:, OFF_AQ:OFF_AK], qn_ref[...])
        ak = _head_norm(z[:, OFF_AK:OFF_AV], kn_ref[...])
        av, bq, bk, bv = (z[:, OFF_AV:OFF_BQ], z[:, OFF_BQ:OFF_BK], z[:, OFF_BK:OFF_BV], z[:, OFF_BV:OFF_CQ])
        oa_ref[rows, :] = _mha(_stack_heads_gqa(aq * Q_SCALE), [(ak, _spread_kv_gqa(av), None)], tq)
        ob_ref[rows, :] = _mha(_stack_heads(bq * Q_SCALE), [(bk, bv, None)], tq)
        oc_ref[rows, :], state = _retention_core(
            z[:, OFF_CQ:OFF_CK], z[:, OFF_CK:OFF_CV] * Q_SCALE, z[:, OFF_CV:OFF_CG], z[:, OFF_CG:OFF_DU],
            dec_ref, gn_ref, dec_scr, seq_len=tq, i0=0, decay_fill="first_step" if s == 0 else "filled",
            want_state=True)
        du_ref[:, s * GROUP_WIDTH:(s + 1) * GROUP_WIDTH] = z[:, OFF_DU:]
        _store_layer_slot(ak_ref.at[s], slot, ak)
        _store_layer_slot(av_ref.at[s], slot, av)
        _store_layer_slot(bk_ref.at[s], slot, bk)
        _store_layer_slot(bv_ref.at[s], slot, bv)
        _store_retention_state(st_ref.at[s], slot, state)


def _ctx_front(x, mods, g1, w_in_bf, qn, kn, dec, gn, layer, prev):
    assert SEQ == S5_SEG
    tm = CTX_SEQS * SEQ
    n = x.shape[0]
    nb = n // SEQ
    steps = n // tm
    per_blk = SUBLANES // CTX_SEQS
    const = lambda *shape: pl.BlockSpec(shape, lambda i: (0,) * len(shape))
    row = lambda w: pl.BlockSpec((tm, w), lambda i: (i, 0))
    in_specs = [row(D_MODEL), _mod_spec(layer, 0, steps), const(1, D_MODEL),
                pl.BlockSpec((None, D_MODEL, IN_WIDTH), lambda i: (layer, 0, 0)),
                const(1, GROUP_WIDTH), const(1, KV_WIDTH), const(SUBLANES, LANES), const(1, GROUP_WIDTH)]
    args = [x, mods, g1, w_in_bf, qn, kn, dec, gn]
    out_specs = [row(GROUP_WIDTH), row(GROUP_WIDTH), row(GROUP_WIDTH),
                 pl.BlockSpec((None, S5_SEG, CTX_SEQS * GROUP_WIDTH), lambda i: (i // per_blk, 0, i % per_blk))]
    out_shape = [jax.ShapeDtypeStruct((n, GROUP_WIDTH), BF16)] * 3 + [
        jax.ShapeDtypeStruct((nb // SUBLANES, S5_SEG, SUBLANES * GROUP_WIDTH), F32)]
    first = prev is None
    slot = 0
    for tail in ((SEQ, KV_WIDTH), (SEQ, KV_WIDTH), (SEQ, GROUP_WIDTH), (SEQ, GROUP_WIDTH),
                 (2, C_HEADS, HEAD_DIM, HEAD_DIM)):
        blk, idx, slot = _layer_slot_block(layer, first, tail)
        out_specs.append(pl.BlockSpec((CTX_SEQS,) + blk[1:], lambda i, idx=idx: (i,) + idx))
        out_shape.append(jax.ShapeDtypeStruct((nb, DEPTH) + tail, F32))
    aliases = {}
    if not first:
        for k, arr in enumerate(prev):
            aliases[len(args)] = 4 + k
            in_specs.append(pl.BlockSpec(memory_space=pl.ANY))
            args.append(arr)
    outs = pl.pallas_call(
        functools.partial(_ctx_front_kernel, n_alias=len(aliases), slot=slot),
        grid=(steps,),
        in_specs=in_specs,
        out_specs=out_specs,
        out_shape=out_shape,
        scratch_shapes=[pltpu.VMEM((C_HEADS * SEQ, SEQ), F32)],
        input_output_aliases=aliases,
        compiler_params=_cparams("arbitrary"),
        name="ctx_front",
    )(*args)
    return outs[0], outs[1], outs[2], outs[3], tuple(outs[4:])


def _s5_prep_kernel(lre_ref, lim_ref, ldt_ref, bre_ref, bim_ref, cre_ref, cim_ref,
                    a_ref, bm_ref, cro_ref, cio_ref, bm_scr, cr_scr, ci_scr):
    lre = lre_ref[...]
    lim = lim_ref[...]
    dt = jnp.exp(ldt_ref[...])
    mag = jnp.exp(lre * dt)
    a_re = mag * jnp.cos(lim * dt)
    a_im = mag * jnp.sin(lim * dt)
    den = lre * lre + lim * lim
    r_re = ((a_re - 1.0) * lre + a_im * lim) / den
    r_im = (a_im * lre - (a_re - 1.0) * lim) / den
    bm_scr[...] = jnp.zeros_like(bm_scr)
    cr_scr[...] = jnp.zeros_like(cr_scr)
    ci_scr[...] = jnp.zeros_like(ci_scr)
    for g in range(S5_GROUPS):
        rows = slice(g * S5_CH, (g + 1) * S5_CH)
        cols = slice(g * S5_STATE, (g + 1) * S5_STATE)
        a_ref[0:1, cols] = a_re[g:g + 1, :]
        a_ref[1:2, cols] = a_im[g:g + 1, :]
        rr, ri = r_re[g:g + 1, :], r_im[g:g + 1, :]
        br, bi = bre_ref[g], bim_ref[g]
        bm_scr[rows, cols] = rr * br - ri * bi
        bm_scr[rows, S5_SP + g * S5_STATE:S5_SP + (g + 1) * S5_STATE] = rr * bi + ri * br
        cr_scr[cols, rows] = cre_ref[g]
        ci_scr[cols, rows] = cim_ref[g]
    bm_ref[...] = bm_scr[...].astype(BF16)
    cro_ref[...] = cr_scr[...].astype(BF16)
    cio_ref[...] = ci_scr[...].astype(BF16)


def _s5_prepare(lam_re, lam_im, log_dt, b_re, b_im, c_re, c_im):
    gp = (S5_GROUPS, S5_STATE)
    ldt = jnp.broadcast_to(log_dt[..., None], (DEPTH, 2) + gp)
    bt = [jnp.swapaxes(t, -1, -2) for t in (b_re, b_im)]
    ct = [jnp.swapaxes(t, -1, -2) for t in (c_re, c_im)]

    def spec(*tail):
        return pl.BlockSpec((None, None) + tail, lambda l, d: (l, d) + (0,) * len(tail))

    return pl.pallas_call(
        _s5_prep_kernel,
        grid=(DEPTH, 2),
        in_specs=[spec(*gp)] * 3 + [spec(S5_GROUPS, S5_CH, S5_STATE)] * 2 + [spec(S5_GROUPS, S5_STATE, S5_CH)] * 2,
        out_specs=[spec(2, S5_SP), spec(GROUP_WIDTH, 2 * S5_SP), spec(S5_SP, GROUP_WIDTH), spec(S5_SP, GROUP_WIDTH)],
        out_shape=[jax.ShapeDtypeStruct((DEPTH, 2, 2, S5_SP), F32),
                   jax.ShapeDtypeStruct((DEPTH, 2, GROUP_WIDTH, 2 * S5_SP), BF16),
                   jax.ShapeDtypeStruct((DEPTH, 2, S5_SP, GROUP_WIDTH), BF16),
                   jax.ShapeDtypeStruct((DEPTH, 2, S5_SP, GROUP_WIDTH), BF16)],
        scratch_shapes=[pltpu.VMEM((GROUP_WIDTH, 2 * S5_SP), F32), pltpu.VMEM((S5_SP, GROUP_WIDTH), F32),
                        pltpu.VMEM((S5_SP, GROUP_WIDTH), F32)],
        compiler_params=_cparams("parallel", "parallel"),
        name="s5_prepare",
    )(lam_re, lam_im, ldt, bt[0], bt[1], ct[0], ct[1])


def _cmul(ar, ai, br, bi):
    return ar * br - ai * bi, ar * bi + ai * br


def _s5_kernel(u_ref, h0_ref, a_ref, bm_ref, cre_ref, cim_ref, dvec_ref, glu_ref, *rest, nseg, slot):
    od_ref, fin_ref, x_scr, s_scr, y_scr = rest[-5:]
    steps = S5_SEG
    rows = steps * SUBLANES
    chunk = S5_CHUNK
    chunk_steps = chunk // SUBLANES
    nchunk = rows // chunk
    seg = lax.broadcasted_iota(jnp.int32, (SUBLANES, S5_SP), 0) % nseg

    for d in range(2):
        ar = jnp.broadcast_to(a_ref[d, 0:1, :], (SUBLANES, S5_SP))
        ai = jnp.broadcast_to(a_ref[d, 1:2, :], (SUBLANES, S5_SP))

        def row0(k):
            c = k if d == 0 else nchunk - 1 - k
            return c * chunk if isinstance(c, int) else pl.multiple_of(c * chunk, chunk)

        def input_part(k, buf):
            x_scr[buf] = jnp.dot(u_ref[pl.ds(row0(k), chunk), :].astype(BF16), bm_ref[d],
                                 preferred_element_type=F32)

        def scan_part(buf, carry, store):
            sr, si = carry
            for t in range(chunk_steps):
                r = (t if d == 0 else chunk_steps - 1 - t) * SUBLANES
                pr, pi = _cmul(ar, ai, sr, si)
                sr = pr + x_scr[buf, r:r + SUBLANES, 0:S5_SP]
                si = pi + x_scr[buf, r:r + SUBLANES, S5_SP:]
                if store:
                    s_scr[buf, r:r + SUBLANES, 0:S5_SP] = sr
                    s_scr[buf, r:r + SUBLANES, S5_SP:] = si
            return sr, si

        def output_part(k, buf):
            y = _bdot(s_scr[buf, :, 0:S5_SP], cre_ref[d]) - _bdot(s_scr[buf, :, S5_SP:], cim_ref[d])
            rows_k = pl.ds(row0(k), chunk)
            if d == 0:
                y_scr[rows_k, :] = y
            else:
                zz = jax.nn.gelu(y_scr[rows_k, :] + y + dvec_ref[...] * u_ref[rows_k, :])
                od_ref[rows_k, :] = (zz * jax.nn.sigmoid(_bdot(zz, glu_ref[...]))).astype(BF16)

        def half(k, buf, carry, store, nxt=True, prev=True):
            if nxt:
                input_part(k + 1, 1 - buf)
            carry = scan_part(buf, carry, store)
            if store and prev:
                output_part(k - 1, 1 - buf)
            return carry

        def run_pass(carry, store):
            input_part(0, 0)
            carry = half(0, 0, carry, store, prev=False)
            carry = half(1, 1, carry, store)

            def pair(j, c):
                c = half(2 * j, 0, c, store)
                return half(2 * j + 1, 1, c, store)
            carry = lax.fori_loop(1, nchunk // 2 - 1, pair, carry)
            carry = half(nchunk - 2, 0, carry, store)
            carry = half(nchunk - 1, 1, carry, store, nxt=False)
            if store:
                output_part(nchunk - 1, 1)
            return carry

        init = (h0_ref[d, :, 0:S5_SP], h0_ref[d, :, S5_SP:])
        if nseg > 1:
            zero = jnp.zeros((SUBLANES, S5_SP), F32)
            fr, fi = run_pass((zero, zero), store=False)
            pr, pi = ar, ai
            for _ in range(int(math.log2(steps))):
                pr, pi = _cmul(pr, pi, pr, pi)
            cr, ci = init
            shift = 1 if d == 0 else SUBLANES - 1
            order = range(1, nseg) if d == 0 else range(nseg - 2, -1, -1)
            for s in order:
                ncr, nci = pltpu.roll(cr, shift, 0), pltpu.roll(ci, shift, 0)
                nfr, nfi = pltpu.roll(fr, shift, 0), pltpu.roll(fi, shift, 0)
                qr, qi = _cmul(pr, pi, ncr, nci)
                cr = jnp.where(seg == s, qr + nfr, cr)
                ci = jnp.where(seg == s, qi + nfi, ci)
            init = (cr, ci)
        sr, si = run_pass(init, store=True)
        for s in range(fin_ref.shape[1] // (4 * S5_SP)):
            base = (4 * s + 2 * d) * S5_SP
            fin_ref[:, base:base + S5_SP] = sr if s == slot else jnp.zeros_like(sr)
            fin_ref[:, base + S5_SP:base + 2 * S5_SP] = si if s == slot else jnp.zeros_like(si)


def _s5(du_tm, h0, a, bmat, cre, cim, dvec, glu_bf, layer, *, nseg, fin_layer=0, fin_layers=1,
        prev_fin=None):
    nblk = du_tm.shape[0]
    rows = S5_SEG * SUBLANES
    fin_w = 4 * S5_SP
    in_specs = [pl.BlockSpec((None, rows, GROUP_WIDTH), lambda i: (i, 0, 0)),
                pl.BlockSpec((2, SUBLANES, 2 * S5_SP), lambda i: (0, 0, 0)),
                pl.BlockSpec((None, 2, 2, S5_SP), lambda i: (layer, 0, 0, 0)),
                pl.BlockSpec((None, 2, GROUP_WIDTH, 2 * S5_SP), lambda i: (layer, 0, 0, 0)),
                pl.BlockSpec((None, 2, S5_SP, GROUP_WIDTH), lambda i: (layer, 0, 0, 0)),
                pl.BlockSpec((None, 2, S5_SP, GROUP_WIDTH), lambda i: (layer, 0, 0, 0)),
                pl.BlockSpec((1, GROUP_WIDTH), lambda i: (0, 0)),
                pl.BlockSpec((None, GROUP_WIDTH, GROUP_WIDTH), lambda i: (layer, 0, 0))]
    args = [du_tm.reshape(nblk, rows, GROUP_WIDTH), h0, a, bmat, cre, cim, dvec, glu_bf]
    aliases = {}
    if prev_fin is not None:
        aliases[len(args)] = 1
        in_specs.append(pl.BlockSpec(memory_space=pl.ANY))
        args.append(prev_fin)
        fin_spec, slot = pl.BlockSpec((SUBLANES, fin_w), lambda i: (i, fin_layer)), 0
    else:
        fin_spec, slot = pl.BlockSpec((SUBLANES, fin_layers * fin_w), lambda i: (i, 0)), fin_layer
    od, fin = pl.pallas_call(
        functools.partial(_s5_kernel, nseg=nseg, slot=slot),
        grid=(nblk,),
        in_specs=in_specs,
        out_specs=[pl.BlockSpec((None, rows, GROUP_WIDTH), lambda i: (i, 0, 0)), fin_spec],
        out_shape=[jax.ShapeDtypeStruct((nblk, rows, GROUP_WIDTH), BF16),
                   jax.ShapeDtypeStruct((nblk * SUBLANES, fin_layers * fin_w), F32)],
        scratch_shapes=[pltpu.VMEM((2, S5_CHUNK, 2 * S5_SP), F32), pltpu.VMEM((2, S5_CHUNK, 2 * S5_SP), F32),
                        pltpu.VMEM((rows, GROUP_WIDTH), F32)],
        input_output_aliases=aliases,
        compiler_params=_cparams("parallel"),
        name="s5",
    )(*args)
    return od.reshape(nblk, S5_SEG, SUBLANES * GROUP_WIDTH), fin


ROUTE_GROUP = MOE_PER_GROUP
OUT_SEQS = 2
MOE_TILE = 512


def _output_core(x_ref, oa_ref, ob_ref, oc_ref, od_ref, mod_ref, wo_ref, g2_ref, wrh_ref, wrl_ref, br_ref):
    od = jnp.concatenate([od_ref[:, s * GROUP_WIDTH:(s + 1) * GROUP_WIDTH] for s in range(OUT_SEQS)], axis=0)
    mix = functools.reduce(jnp.add, [
        _bdot(o, wo_ref[i * GROUP_WIDTH:(i + 1) * GROUP_WIDTH, :])
        for i, o in enumerate((oa_ref[...], ob_ref[...], oc_ref[...], od))])
    xm = x_ref[...] + mod_ref[2:3, :] * mix
    h2 = _rms_rows(xm) * g2_ref[...] * (1.0 + mod_ref[4:5, :]) + mod_ref[3:4, :]

    h_hi, h_lo = _split(h2)
    logits = (jnp.dot(h_hi, wrh_ref[...], preferred_element_type=F32)
              + jnp.dot(h_hi, wrl_ref[...], preferred_element_type=F32)
              + jnp.dot(h_lo, wrh_ref[...], preferred_element_type=F32)) + br_ref[...]
    lane_i = lax.broadcasted_iota(jnp.int32, logits.shape, 1)
    lane = lane_i.astype(F32)
    big = jnp.float32(2 ** 30)
    gmask = lane_i < MOE_GROUPS
    gl = jnp.where(gmask, logits, -jnp.inf)
    gmax = jnp.max(gl, axis=-1, keepdims=True)
    p_top = 1.0 / jnp.sum(jnp.exp(gl - gmax), axis=-1, keepdims=True)
    g_top = jnp.min(jnp.where(gl == gmax, lane, big), axis=-1, keepdims=True)
    e_lane = lane_i - ROUTER_OFF
    lane_group = (e_lane // MOE_PER_GROUP).astype(F32)
    emask = (e_lane >= 0) & (e_lane < MOE_EXPERTS) & (lane_group == g_top)
    el = jnp.where(emask, logits, -jnp.inf)
    m1 = jnp.max(el, axis=-1, keepdims=True)
    i1 = jnp.min(jnp.where(el == m1, lane, big), axis=-1, keepdims=True)
    el2 = jnp.where(lane == i1, -jnp.inf, el)
    m2 = jnp.max(el2, axis=-1, keepdims=True)
    i2 = jnp.min(jnp.where(el2 == m2, lane, big), axis=-1, keepdims=True)
    e2 = jnp.exp(m2 - m1)
    den = 1.0 + e2
    gates = (jnp.where(lane == i1, (1.0 / den) * p_top, 0.0)
             + jnp.where(lane == i2, (e2 / den) * p_top, 0.0))
    route = jnp.where(lane == ROUTE_GROUP + g_top, 1.0, 0.0)
    for g in range(MOE_GROUPS):
        local = pltpu.roll(gates, LANES - ROUTER_OFF - g * MOE_PER_GROUP, 1)
        route = route + jnp.where((g_top == g) & (lane_i < MOE_PER_GROUP), local, 0.0)
    return xm, h2, route


GROUP_HID = MOE_PER_GROUP * MOE_HIDDEN


MOE_CHUNK = 160


def _tail_kernel(x_ref, oa_ref, ob_ref, oc_ref, od_ref, mod_ref, wo_ref, g2_ref, wrh_ref, wrl_ref, br_ref,
                 w1_ref, w3_ref, w2_ref, fg_ref, o_ref, xm_scr, hs_scr, rs_scr, os_scr, before_scr,
                 *, final, tm):
    i = pl.program_id(0)
    xm, h2, route = _output_core(x_ref, oa_ref, ob_ref, oc_ref, od_ref, mod_ref, wo_ref, g2_ref,
                                 wrh_ref, wrl_ref, br_ref)
    xm_scr[...] = xm
    h2 = h2.astype(BF16)
    r_hi, r_lo = _split(route)
    pick = (lax.broadcasted_iota(jnp.int32, (SUBLANES, LANES), 1)
            == ROUTE_GROUP + lax.broadcasted_iota(jnp.int32, (SUBLANES, LANES), 0))
    gt = lax.dot_general(jnp.where(pick, 1.0, 0.0).astype(BF16), r_hi, (((1,), (1,)), ((), ())),
                         preferred_element_type=F32)
    off1 = jnp.sum(gt[0:1, :]).astype(jnp.int32)
    off2 = off1 + jnp.sum(gt[1:2, :]).astype(jnp.int32)
    off3 = off2 + jnp.sum(gt[2:3, :]).astype(jnp.int32)
    starts = (jnp.int32(0), off1, off2, off3)
    ends = (off1, off2, off3, jnp.int32(tm))
    @pl.when(i == 0)
    def _():
        before_scr[...] = jnp.where(lax.broadcasted_iota(jnp.int32, (tm, tm), 0)
                                    < lax.broadcasted_iota(jnp.int32, (tm, tm), 1), 1.0, 0.0).astype(BF16)

    rank = jnp.dot(gt.astype(BF16), before_scr[...], preferred_element_type=F32)
    gt_i = gt.astype(jnp.int32)
    rank_i = rank.astype(jnp.int32)
    pos = jnp.zeros((1, tm), jnp.int32)
    for g in range(MOE_GROUPS):
        pos = pos + gt_i[g:g + 1, :] * (rank_i[g:g + 1, :] + starts[g])
    perm = jnp.where(lax.broadcasted_iota(jnp.int32, (tm, tm), 0) == pos, 1.0, 0.0).astype(BF16)
    hs_scr[...] = jnp.dot(perm, h2, preferred_element_type=F32).astype(BF16)
    rs_scr[...] = (jnp.dot(perm, r_hi, preferred_element_type=F32)
                   + jnp.dot(perm, r_lo, preferred_element_type=F32))

    os_scr[...] = jnp.zeros_like(os_scr)
    for g in range(MOE_GROUPS):
        lo, hi = starts[g], ends[g]
        base = (lo // BF16_ROWS) * BF16_ROWS
        n_chunks = jnp.where(hi > lo, (hi - base + MOE_CHUNK - 1) // MOE_CHUNK, 0)

        def chunk_body(c, carry, g=g, lo=lo, hi=hi, base=base):
            r0 = pl.multiple_of(jnp.minimum(base + c * MOE_CHUNK, tm - MOE_CHUNK), BF16_ROWS)
            rows = pl.ds(r0, MOE_CHUNK)
            x = hs_scr[rows, :]
            gates = rs_scr[rows, :]
            a = jnp.dot(x, w1_ref[g], preferred_element_type=F32)
            b = jnp.dot(x, w3_ref[g], preferred_element_type=F32)
            hid = []
            for e in range(MOE_PER_GROUP):
                sl = slice(e * MOE_HIDDEN, (e + 1) * MOE_HIDDEN)
                hid.append((jax.nn.silu(a[:, sl]) * b[:, sl] * gates[:, e:e + 1]).astype(BF16))
            y = jnp.dot(jnp.concatenate(hid, axis=1), w2_ref[g], preferred_element_type=F32)
            rowid = r0 + lax.broadcasted_iota(jnp.int32, (MOE_CHUNK, 1), 0)
            member = (rowid >= lo) & (rowid < hi)
            os_scr[rows, :] = jnp.where(member, y, os_scr[rows, :])
            return carry

        lax.fori_loop(0, n_chunks, chunk_body, 0)

    o_hi, o_lo = _split(os_scr[...])
    moe = (lax.dot_general(perm, o_hi, (((0,), (0,)), ((), ())), preferred_element_type=F32)
           + lax.dot_general(perm, o_lo, (((0,), (0,)), ((), ())), preferred_element_type=F32))
    out = xm_scr[...] + mod_ref[5:6, :] * moe
    if final:
        out = _rms_rows(out) * fg_ref[...]
    o_ref[...] = out


def _moe_weight_kernel(w1_ref, w3_ref, w2_ref, o1_ref, o3_ref, o2_ref):
    for e in range(MOE_PER_GROUP):
        sl = slice(e * MOE_HIDDEN, (e + 1) * MOE_HIDDEN)
        o1_ref[:, sl] = w1_ref[e].astype(BF16)
        o3_ref[:, sl] = w3_ref[e].astype(BF16)
        o2_ref[sl, :] = w2_ref[e].astype(BF16)


def _moe_weights(w1, w3, w2):
    up = pl.BlockSpec((None, MOE_PER_GROUP, D_MODEL, MOE_HIDDEN), lambda l, g: (l, g, 0, 0))
    down = pl.BlockSpec((None, MOE_PER_GROUP, MOE_HIDDEN, D_MODEL), lambda l, g: (l, g, 0, 0))
    out = pl.BlockSpec((None, None, D_MODEL, GROUP_HID), lambda l, g: (l, g, 0, 0))
    shape = jax.ShapeDtypeStruct((DEPTH, MOE_GROUPS, D_MODEL, GROUP_HID), BF16)
    return pl.pallas_call(
        _moe_weight_kernel,
        grid=(DEPTH, MOE_GROUPS),
        in_specs=[up, up, down],
        out_specs=[out, out, out],
        out_shape=[shape, shape, shape],
        compiler_params=_cparams("parallel", "parallel"),
        name="moe_weights",
    )(w1, w3, w2)


def _tail(x, mixes, mods, mod_row, mod_tokens, wo_bf, g2, wr_hi, wr_lo, br, w1g, w3g, w2g, fg, layer, *, final):
    tm = MOE_TILE
    assert tm == OUT_SEQS * S5_SEG
    n = x.shape[0]
    row = lambda w: pl.BlockSpec((tm, w), lambda i: (i, 0))
    const = lambda shape: pl.BlockSpec(shape, lambda i: (0,) * len(shape))
    per_blk = SUBLANES // OUT_SEQS
    wspec = pl.BlockSpec((None, MOE_GROUPS, D_MODEL, GROUP_HID), lambda i: (layer, 0, 0, 0),
                         pipeline_mode=pl.Buffered(1))
    return pl.pallas_call(
        functools.partial(_tail_kernel, final=final, tm=tm),
        grid=(n // tm,),
        in_specs=[row(D_MODEL), row(GROUP_WIDTH), row(GROUP_WIDTH), row(GROUP_WIDTH),
                  pl.BlockSpec((None, S5_SEG, OUT_SEQS * GROUP_WIDTH), lambda i: (i // per_blk, 0, i % per_blk)),
                  _mod_spec(layer, mod_row, mod_tokens // tm),
                  pl.BlockSpec((None, D_MODEL, D_MODEL), lambda i: (layer, 0, 0)), const((1, D_MODEL)),
                  const((D_MODEL, LANES)), const((D_MODEL, LANES)), const((1, LANES)),
                  wspec, wspec, wspec, const((1, D_MODEL))],
        out_specs=row(D_MODEL),
        out_shape=jax.ShapeDtypeStruct((n, D_MODEL), F32),
        scratch_shapes=[pltpu.VMEM((tm, D_MODEL), F32), pltpu.VMEM((tm, D_MODEL), BF16),
                        pltpu.VMEM((tm, LANES), F32), pltpu.VMEM((tm, D_MODEL), F32),
                        pltpu.VMEM((tm, tm), BF16)],
        compiler_params=_cparams("arbitrary"),
        name="layer_tail",
    )(x, *mixes, mods, wo_bf, g2, wr_hi, wr_lo, br, w1g, w3g, w2g, fg)


def kernel(x_prompt, x_sample, cache_a_k, cache_a_v, cache_b_k, cache_b_v, state_ret, state_ssm, c, c_ctx, mod_w, mod_b, norm1_g, norm2_g, w_in, a_qn_g, a_kn_g, b_rel_bias, ret_decay, ret_gn_g, s5_lam_re, s5_lam_im, s5_log_dt, s5_b_re, s5_b_im, s5_c_re, s5_c_im, s5_d, s5_glu_w, w_out, moe_gw, moe_gb, moe_ew, moe_eb, moe_w1, moe_w3, moe_w2, final_norm_g):
    n_ctx = BATCH * SEQ
    n_lat = DEC_BATCH * DEC_SEQ
    lat_seg = DEC_SEQ // S5_SEG

    cond = jnp.zeros((SUBLANES, D_MODEL), F32).at[0].set(c_ctx).at[1:1 + DEC_BATCH].set(c)
    mods = _modulation(cond, mod_w, mod_b).reshape(DEPTH, SUBLANES, N_MOD, D_MODEL)

    rope_tabs = _rope_tables()
    s5_a, s5_bm, s5_cre, s5_cim = _s5_prepare(s5_lam_re, s5_lam_im, s5_log_dt, s5_b_re, s5_b_im,
                                              s5_c_re, s5_c_im)
    cak = cache_a_k.reshape(DEC_BATCH, DEPTH, PAST_LEN, A_KV_HEADS * HEAD_DIM)
    cav = cache_a_v.reshape(DEC_BATCH, DEPTH, PAST_LEN, A_KV_HEADS * HEAD_DIM)
    cbk = cache_b_k.reshape(DEC_BATCH, DEPTH, PAST_LEN, B_HEADS * HEAD_DIM)
    cbv = cache_b_v.reshape(DEC_BATCH, DEPTH, PAST_LEN, B_HEADS * HEAD_DIM)

    xc = x_prompt.reshape(n_ctx, D_MODEL)
    xs = x_sample.reshape(n_lat, D_MODEL)
    w1_all, w3_all, w2_all = _moe_weights(moe_w1, moe_w3, moe_w2)
    eye_h = jnp.eye(C_HEADS, dtype=F32)
    s0_bd = (state_ret[:, :, :, :, :, None, :] * eye_h[None, None, None, :, None, :, None]).reshape(
        DEC_BATCH, DEPTH, 2, C_HEADS * HEAD_DIM, C_HEADS * HEAD_DIM)
    ctx_state = ssm_states = None
    h0_zero = jnp.zeros((2, SUBLANES, 2 * S5_SP), F32)
    w_in_bf = w_in.astype(BF16)
    wo_bf = w_out.astype(BF16)
    glu_bf = s5_glu_w.astype(BF16)
    for l in range(DEPTH):
        final = l == DEPTH - 1
        g1 = norm1_g[l].reshape(1, D_MODEL)
        g2 = norm2_g[l].reshape(1, D_MODEL)
        fg = final_norm_g.reshape(1, D_MODEL)
        qn = jnp.tile(a_qn_g[l], A_HEADS).reshape(1, GROUP_WIDTH)
        kn = jnp.tile(a_kn_g[l], A_KV_HEADS).reshape(1, KV_WIDTH)
        dec = jnp.broadcast_to(ret_decay[l].reshape(2 * C_HEADS, 1), (2 * C_HEADS, LANES))
        gn = ret_gn_g[l].reshape(1, GROUP_WIDTH)
        dvec = s5_d[l].reshape(1, GROUP_WIDTH)
        wr = jnp.zeros((D_MODEL, LANES), F32).at[:, :MOE_GROUPS].set(moe_gw[l]).at[
            :, ROUTER_OFF:ROUTER_OFF + MOE_EXPERTS].set(moe_ew[l])
        br = jnp.zeros((1, LANES), F32).at[0, :MOE_GROUPS].set(moe_gb[l]).at[
            0, ROUTER_OFF:ROUTER_OFF + MOE_EXPERTS].set(moe_eb[l])
        wr_hi = wr.astype(BF16)
        wr_lo = (wr - wr_hi.astype(F32)).astype(BF16)
        na_bias = _na_bias(b_rel_bias[l])

        oa, ob, oc, du_tm, ctx_state = _ctx_front(xc, mods, g1, w_in_bf, qn, kn, dec, gn, l, ctx_state)
        od_tm, ssm_states = _s5(du_tm, h0_zero, s5_a, s5_bm, s5_cre, s5_cim, dvec, glu_bf, l,
                                nseg=1, fin_layer=l, fin_layers=DEPTH, prev_fin=ssm_states)
        xc = _tail(xc, (oa, ob, oc, od_tm), mods, 0, n_ctx, wo_bf, g2, wr_hi, wr_lo, br,
                   w1_all, w3_all, w2_all, fg, l, final=final)

        zs, cg, du_tm = _project(xs, mods, 1, DEC_SEQ, g1, w_in_bf, qn, kn, rope_tabs, l, seq_len=DEC_SEQ)
        zs3 = zs.reshape(DEC_BATCH, DEC_SEQ, OFF_CG)
        oa = _lat_attention_a(zs3, cak, cav, l).reshape(n_lat, GROUP_WIDTH)
        ob = _lat_attention_b(zs3, cbk, cbv, na_bias, l).reshape(n_lat, GROUP_WIDTH)
        oc = _retention(zs3, cg.reshape(DEC_BATCH, DEC_SEQ, GROUP_WIDTH), dec, gn, s0_bd, l).reshape(
            n_lat, GROUP_WIDTH)
        h0 = state_ssm[:, l].reshape(DEC_BATCH, 2, 2 * S5_SP).transpose(1, 0, 2)
        h0_seg = jnp.zeros((2, DEC_BATCH, lat_seg, 2 * S5_SP), F32)
        h0_seg = h0_seg.at[0, :, 0].set(h0[0]).at[1, :, lat_seg - 1].set(h0[1])
        od_tm, _ = _s5(du_tm, h0_seg.reshape(2, SUBLANES, 2 * S5_SP),
                       s5_a, s5_bm, s5_cre, s5_cim, dvec, glu_bf, l, nseg=lat_seg)
        xs = _tail(xs, (oa, ob, oc, od_tm), mods, 1, DEC_SEQ, wo_bf, g2, wr_hi, wr_lo, br,
                   w1_all, w3_all, w2_all, fg, l, final=final)

    new_ak, new_av, new_bk, new_bv, ret_states = ctx_state
    return (xc.reshape(BATCH, SEQ, D_MODEL), xs.reshape(DEC_BATCH, DEC_SEQ, D_MODEL),
            new_ak.reshape(BATCH, DEPTH, SEQ, A_KV_HEADS, HEAD_DIM),
            new_av.reshape(BATCH, DEPTH, SEQ, A_KV_HEADS, HEAD_DIM),
            new_bk.reshape(BATCH, DEPTH, SEQ, B_HEADS, HEAD_DIM),
            new_bv.reshape(BATCH, DEPTH, SEQ, B_HEADS, HEAD_DIM),
            ret_states,
            ssm_states.reshape(BATCH, DEPTH, 2, 2, S5_GROUPS, S5_STATE))
```

```python
import functools
import math

import numpy as np
import jax
import jax.numpy as jnp
from jax import lax
from jax.experimental import pallas as pl
from jax.experimental.pallas import tpu as pltpu

F32 = jnp.float32
BF16 = jnp.bfloat16

D_MODEL = 1024
BATCH = 32
SEQ = 256
DEPTH = 2
DEC_BATCH = 2
DEC_SEQ = 1024
PAST_LEN = 256
GRID_W = 64
HEAD_DIM = 64
GROUP_WIDTH = 256
A_HEADS = 4
A_KV_HEADS = 2
B_HEADS = 4
NA_ROWS = 8
NA_COLS = 16
C_HEADS = 4
S5_CH = 16
S5_GROUPS = 16
S5_STATE = 64
MOE_GROUPS = 4
MOE_PER_GROUP = 8
MOE_EXPERTS = 32
MOE_HIDDEN = 128
ROPE_THETA = 10000.0
EPS = 1e-6
IN_WIDTH = 2560
Q_SCALE = HEAD_DIM ** -0.5
KV_WIDTH = A_KV_HEADS * HEAD_DIM
N_MOD = 6
ROPE_PAIR = HEAD_DIM // 4
LAT_TQ = 256
MOD_TILE = 1536

OFF_AQ, OFF_AK, OFF_AV = 0, 256, 384
OFF_BQ, OFF_BK, OFF_BV = 512, 768, 1024
OFF_CQ, OFF_CK, OFF_CV, OFF_CG = 1280, 1536, 1792, 2048
OFF_DU = 2304

LANES = 128
SUBLANES = 8
BF16_ROWS = 16
S5_SP = S5_GROUPS * S5_STATE
S5_SEG = 256
S5_CHUNK = 256
ROUTER_OFF = 4
NEG_BIG = -1e30
VMEM_LIMIT = 56 * 1024 * 1024


def _cparams(*sem):
    return pltpu.CompilerParams(dimension_semantics=sem, vmem_limit_bytes=VMEM_LIMIT)


def _mod_spec(layer, first_row, tiles_per_row):
    return pl.BlockSpec((None, None, N_MOD, D_MODEL), lambda i: (layer, first_row + i // tiles_per_row, 0, 0))


def _bdot(a, b):
    return jnp.dot(a.astype(BF16), b.astype(BF16), preferred_element_type=F32)


def _bdot_nt(a, b):
    return lax.dot_general(a.astype(BF16), b.astype(BF16), (((1,), (1,)), ((), ())),
                           preferred_element_type=F32)


def _bdot_tn(a, b):
    return lax.dot_general(a.astype(BF16), b.astype(BF16), (((0,), (0,)), ((), ())),
                           preferred_element_type=F32)


def _split(a):
    hi = a.astype(BF16)
    lo = (a - hi.astype(F32)).astype(BF16)
    return hi, lo


def _dot_hilo_lhs(a, b_bf16):
    hi, lo = _split(a)
    return (jnp.dot(hi, b_bf16, preferred_element_type=F32)
            + jnp.dot(lo, b_bf16, preferred_element_type=F32))


def _rms_rows(x):
    return x * lax.rsqrt(jnp.mean(x * x, axis=-1, keepdims=True) + EPS)


def _mod_kernel(cond_ref, w_ref, b_ref, o_ref):
    o_ref[...] = _bdot(jax.nn.silu(cond_ref[...]), w_ref[...]) + b_ref[...]


def _modulation(cond, mod_w, mod_b):
    tn = MOD_TILE
    width = N_MOD * D_MODEL
    return pl.pallas_call(
        _mod_kernel,
        grid=(DEPTH, width // tn),
        in_specs=[pl.BlockSpec((SUBLANES, D_MODEL), lambda l, j: (0, 0)),
                  pl.BlockSpec((None, D_MODEL, tn), lambda l, j: (l, 0, j)),
                  pl.BlockSpec((None, 1, tn), lambda l, j: (l, 0, j))],
        out_specs=pl.BlockSpec((None, SUBLANES, tn), lambda l, j: (l, 0, j)),
        out_shape=jax.ShapeDtypeStruct((DEPTH, SUBLANES, width), F32),
        compiler_params=_cparams("arbitrary", "arbitrary"),
        name="modulation",
    )(cond, mod_w, mod_b.reshape(DEPTH, 1, width))


def _group_mean_matrix(w):
    ri = lax.broadcasted_iota(jnp.int32, (w, w), 0) // HEAD_DIM
    ci = lax.broadcasted_iota(jnp.int32, (w, w), 1) // HEAD_DIM
    return jnp.where(ri == ci, 1.0 / HEAD_DIM, 0.0).astype(BF16)


def _head_norm(t, g):
    ms = _dot_hilo_lhs(t * t, _group_mean_matrix(t.shape[1]))
    return t * lax.rsqrt(ms + EPS) * g


def _rope(t, cos, sa, sb):
    return (t * cos + pltpu.roll(t, LANES - ROPE_PAIR, 1) * sa + pltpu.roll(t, ROPE_PAIR, 1) * sb)


def _store_layer_slot(ref, slot, value):
    for s in range(ref.shape[0]):
        ref[s] = value if s == slot else jnp.zeros_like(value)


def _layer_slot_block(layer, first_call, tail):
    if first_call:
        return (None, DEPTH) + tail, (0,) * (1 + len(tail)), layer
    return (None, 1) + tail, (layer,) + (0,) * len(tail), 0


def _proj_kernel(x_ref, mod_ref, g1_ref, w_ref, qn_ref, kn_ref, cos_ref, sa_ref, sb_ref, z_ref, cg_ref, du_ref):
    h = _rms_rows(x_ref[...]) * g1_ref[...] * (1.0 + mod_ref[1:2, :]) + mod_ref[0:1, :]
    z = jnp.dot(h.astype(BF16), w_ref[...], preferred_element_type=F32)
    aq = _head_norm(z[:, OFF_AQ:OFF_AK], qn_ref[...])
    ak = _head_norm(z[:, OFF_AK:OFF_AV], kn_ref[...])
    for j in range(3):
        t = aq[:, j * LANES:(j + 1) * LANES] if j < 2 else ak
        sl = slice(0, LANES) if j == 2 else slice(j * LANES, (j + 1) * LANES)
        t = _rope(t, cos_ref[:, sl], sa_ref[:, sl], sb_ref[:, sl])
        z_ref[:, j * LANES:(j + 1) * LANES] = t.astype(BF16)
    z_ref[:, OFF_AV:OFF_CK] = z[:, OFF_AV:OFF_CK].astype(BF16)
    z_ref[:, OFF_CK:OFF_CV] = (z[:, OFF_CK:OFF_CV] * Q_SCALE).astype(BF16)
    z_ref[:, OFF_CV:OFF_CG] = z[:, OFF_CV:OFF_CG].astype(BF16)
    cg_ref[...] = z[:, OFF_CG:OFF_DU]
    du_ref[...] = z[:, OFF_DU:]


def _du_spec():
    return pl.BlockSpec((None, S5_SEG, GROUP_WIDTH), lambda i: (i // SUBLANES, 0, i % SUBLANES))


def _project(x, mods, mod_row, mod_tokens, g1, w_in_bf, qn, kn, rope_tabs, layer, *, seq_len):
    tm = S5_SEG
    n = x.shape[0]
    tps = seq_len // tm
    return pl.pallas_call(
        _proj_kernel,
        grid=(n // tm,),
        in_specs=[pl.BlockSpec((tm, D_MODEL), lambda i: (i, 0)),
                  _mod_spec(layer, mod_row, mod_tokens // tm),
                  pl.BlockSpec((1, D_MODEL), lambda i: (0, 0)),
                  pl.BlockSpec((None, D_MODEL, IN_WIDTH), lambda i: (layer, 0, 0)),
                  pl.BlockSpec((1, GROUP_WIDTH), lambda i: (0, 0)),
                  pl.BlockSpec((1, KV_WIDTH), lambda i: (0, 0))]
                 + [pl.BlockSpec((tm, GROUP_WIDTH), lambda i: (i % tps, 0))] * 3,
        out_specs=[pl.BlockSpec((tm, OFF_CG), lambda i: (i, 0)),
                   pl.BlockSpec((tm, GROUP_WIDTH), lambda i: (i, 0)), _du_spec()],
        out_shape=[jax.ShapeDtypeStruct((n, OFF_CG), BF16),
                   jax.ShapeDtypeStruct((n, GROUP_WIDTH), F32),
                   jax.ShapeDtypeStruct((n // (tm * SUBLANES), S5_SEG, SUBLANES * GROUP_WIDTH), F32)],
        compiler_params=_cparams("parallel"),
        name="project",
    )(x, mods, g1, w_in_bf, qn, kn, *rope_tabs)


def _rope_tables():
    t = jnp.arange(DEC_SEQ)
    row = (t // GRID_W).astype(F32)
    col = (t % GRID_W).astype(F32)
    nf = HEAD_DIM // 4
    inv = ROPE_THETA ** (-jnp.arange(nf, dtype=F32) / nf)
    ang_r = row[:, None] * inv[None, :]
    ang_c = col[:, None] * inv[None, :]
    zeros = jnp.zeros_like(ang_r)
    cos = jnp.concatenate([jnp.cos(ang_r), jnp.cos(ang_r), jnp.cos(ang_c), jnp.cos(ang_c)], axis=-1)
    sa = jnp.concatenate([-jnp.sin(ang_r), zeros, -jnp.sin(ang_c), zeros], axis=-1)
    sb = jnp.concatenate([zeros, jnp.sin(ang_r), zeros, jnp.sin(ang_c)], axis=-1)
    return tuple(jnp.tile(a, (1, 4)) for a in (cos, sa, sb))


N_HEADS = 4


def _lane_head(width):
    return lax.broadcasted_iota(jnp.int32, (1, width), 1) // HEAD_DIM


def _stack_heads(q):
    head = _lane_head(q.shape[1])
    return jnp.concatenate([jnp.where(head == h, q, 0.0) for h in range(N_HEADS)], axis=0).astype(BF16)


def _stack_heads_gqa(q):
    lo = lax.broadcasted_iota(jnp.int32, (1, LANES), 1) < HEAD_DIM
    q = q.astype(F32)
    q01, q23 = q[:, :LANES], q[:, LANES:]
    blocks = [jnp.where(lo, q01, 0.0), jnp.where(lo, pltpu.roll(q01, HEAD_DIM, 1), 0.0),
              jnp.where(lo, 0.0, pltpu.roll(q23, HEAD_DIM, 1)), jnp.where(lo, 0.0, q23)]
    return jnp.concatenate(blocks, axis=0).astype(BF16)


def _spread_kv_gqa(v):
    lo = lax.broadcasted_iota(jnp.int32, (1, LANES), 1) < HEAD_DIM
    v = v.astype(F32)
    vr = pltpu.roll(v, HEAD_DIM, 1)
    return jnp.concatenate([jnp.where(lo, v, vr), jnp.where(lo, vr, v)], axis=1)


def _mha(qs, blocks, tq):
    scores = []
    for k, _, bias in blocks:
        s = _bdot_nt(qs, k)
        scores.append(s if bias is None else s + bias)
    m = functools.reduce(jnp.maximum, [jnp.max(s, axis=-1, keepdims=True) for s in scores])
    es = [jnp.exp(s - m) for s in scores]
    denom = functools.reduce(jnp.add, [jnp.sum(e, axis=-1, keepdims=True) for e in es])
    ps = [e.astype(BF16) for e in es]
    head = _lane_head(N_HEADS * HEAD_DIM)
    vals = [v.astype(BF16) for _, v, _ in blocks]
    o = None
    dall = None
    for h in range(N_HEADS):
        rows = slice(h * tq, (h + 1) * tq)
        for p, v in zip(ps, vals):
            t = jnp.dot(p[rows], jnp.where(head == h, v, jnp.zeros_like(v)), preferred_element_type=F32)
            o = t if o is None else o + t
        d = jnp.where(head == h, denom[rows], 0.0)
        dall = d if dall is None else dall + d
    return (o / dall).astype(BF16)


def _lat_attn_a_kernel(q_ref, kn_ref, vn_ref, kc_ref, vc_ref, o_ref):
    for b in range(DEC_BATCH):
        o_ref[b] = _mha(_stack_heads_gqa(q_ref[b] * Q_SCALE),
                        [(kc_ref[b], _spread_kv_gqa(vc_ref[b]), None),
                         (kn_ref[b], _spread_kv_gqa(vn_ref[b]), None)], q_ref.shape[1])


def _lat_attention_a(z, cache_k, cache_v, layer, tq=LAT_TQ):
    cache_spec = pl.BlockSpec((DEC_BATCH, None, PAST_LEN, KV_WIDTH), lambda j: (0, layer, 0, 0))
    return pl.pallas_call(
        _lat_attn_a_kernel,
        grid=(DEC_SEQ // tq,),
        in_specs=[pl.BlockSpec((DEC_BATCH, tq, GROUP_WIDTH), lambda j: (0, j, OFF_AQ // GROUP_WIDTH)),
                  pl.BlockSpec((DEC_BATCH, DEC_SEQ, KV_WIDTH), lambda j: (0, 0, OFF_AK // KV_WIDTH)),
                  pl.BlockSpec((DEC_BATCH, DEC_SEQ, KV_WIDTH), lambda j: (0, 0, OFF_AV // KV_WIDTH)),
                  cache_spec, cache_spec],
        out_specs=pl.BlockSpec((DEC_BATCH, tq, GROUP_WIDTH), lambda j: (0, j, 0)),
        out_shape=jax.ShapeDtypeStruct((DEC_BATCH, DEC_SEQ, GROUP_WIDTH), BF16),
        compiler_params=_cparams("parallel"),
        name="lat_attention_a",
    )(z, z, z, cache_k, cache_v)


NA_KEYS = NA_ROWS * GRID_W


NA_PAIRS = 2 * NA_ROWS - 2


NA_STEP_ROWS = 2


def _na_kernel(q_ref, k_ref, v_ref, kc_ref, vc_ref, bias_ref, o_ref):
    rows = DEC_SEQ // GRID_W
    for b in range(q_ref.shape[0]):
        outs = []
        for rr in range(NA_STEP_ROWS):
            r = pl.program_id(0) * NA_STEP_ROWS + rr
            row_start = jnp.clip(r - NA_ROWS // 2, 0, rows - NA_ROWS)
            start = pl.multiple_of(row_start * GRID_W, GRID_W)
            rel0 = row_start - r + NA_ROWS - 1
            kl = k_ref[b, pl.ds(start, NA_KEYS), :]
            vl = v_ref[b, pl.ds(start, NA_KEYS), :]
            bias = jnp.concatenate(
                [jnp.concatenate([bias_ref[h, rel0 + 2 * jp] for jp in range(NA_ROWS // 2)], axis=1)
                 for h in range(B_HEADS)], axis=0)
            qrows = slice(rr * GRID_W, (rr + 1) * GRID_W)
            outs.append(_mha(_stack_heads(q_ref[b, qrows, :] * Q_SCALE),
                             [(kl, vl, bias), (kc_ref[b], vc_ref[b], None)], GRID_W))
        o_ref[b] = jnp.concatenate(outs, axis=0)


def _na_bias(rel_bias):
    nrel = 2 * NA_COLS - 1
    period = 2 * GRID_W
    b = rel_bias.astype(F32)
    ext = jnp.concatenate([b[..., NA_COLS - 1:],
                           jnp.zeros(b.shape[:-1] + (period - nrel,), F32),
                           b[..., :NA_COLS - 1]], axis=-1)
    flat = jnp.tile(ext, (1, 1, GRID_W))[..., :GRID_W * (period - 1)]
    toe = flat.reshape(b.shape[:-1] + (GRID_W, period - 1))[..., :GRID_W]
    col_start = np.clip(np.arange(GRID_W) - NA_COLS // 2, 0, GRID_W - NA_COLS)
    kc = np.arange(GRID_W)
    inside = (kc[None, :] >= col_start[:, None]) & (kc[None, :] < col_start[:, None] + NA_COLS)
    toe = jnp.where(jnp.asarray(inside), toe, NEG_BIG)
    return jnp.concatenate([toe[:, :-1], toe[:, 1:]], axis=-1)


def _lat_attention_b(z, cache_k, cache_v, bias, layer):
    tq = NA_STEP_ROWS * GRID_W
    cache_spec = pl.BlockSpec((DEC_BATCH, None, PAST_LEN, GROUP_WIDTH), lambda r: (0, layer, 0, 0))
    return pl.pallas_call(
        _na_kernel,
        grid=(DEC_SEQ // tq,),
        in_specs=[pl.BlockSpec((DEC_BATCH, tq, GROUP_WIDTH), lambda r: (0, r, OFF_BQ // GROUP_WIDTH)),
                  pl.BlockSpec((DEC_BATCH, DEC_SEQ, GROUP_WIDTH), lambda r: (0, 0, OFF_BK // GROUP_WIDTH)),
                  pl.BlockSpec((DEC_BATCH, DEC_SEQ, GROUP_WIDTH), lambda r: (0, 0, OFF_BV // GROUP_WIDTH)),
                  cache_spec, cache_spec,
                  pl.BlockSpec((B_HEADS, NA_PAIRS, GRID_W, 2 * GRID_W), lambda r: (0, 0, 0, 0))],
        out_specs=pl.BlockSpec((DEC_BATCH, tq, GROUP_WIDTH), lambda r: (0, r, 0)),
        out_shape=jax.ShapeDtypeStruct((DEC_BATCH, DEC_SEQ, GROUP_WIDTH), BF16),
        compiler_params=_cparams("parallel"),
        name="lat_attention_b",
    )(z, z, z, cache_k, cache_v, bias)


def _retention_core(q, k, v, g, dec_ref, gn_ref, dec_scr, *, seq_len, i0, decay_fill, s0_ref=None,
                    want_state=False):
    tq = q.shape[0]
    head = _lane_head(C_HEADS * HEAD_DIM)
    lg = jax.nn.log_sigmoid(dec_ref[...])

    def per_lane(row0):
        out = jnp.zeros((1, C_HEADS * HEAD_DIM), F32)
        for h in range(C_HEADS):
            out = jnp.where(head == h, lg[row0 + h:row0 + h + 1, 0:1], out)
        return out

    lgf_l, lgb_l = per_lane(0), per_lane(C_HEADS)
    qi = (i0 + lax.broadcasted_iota(jnp.int32, (tq, 1), 0)).astype(F32)

    def fill_decay():
        kj = lax.broadcasted_iota(jnp.int32, (1, seq_len), 1).astype(F32)
        diff = qi - kj
        for h in range(C_HEADS):
            lgf = lg[h:h + 1, 0:1]
            lgb = lg[C_HEADS + h:C_HEADS + h + 1, 0:1]
            dec_scr[h * tq:(h + 1) * tq, :] = (
                jnp.where(diff >= 0, jnp.exp(lgf * jnp.maximum(diff, 0.0)), 0.0)
                + jnp.where(diff <= 0, jnp.exp(lgb * jnp.maximum(-diff, 0.0)), 0.0))

    if decay_fill == "first_step":
        pl.when(pl.program_id(0) == 0)(fill_decay)
    elif decay_fill == "every_step":
        fill_decay()
    else:
        assert decay_fill == "filled"

    v = v.astype(BF16)
    sc = (_bdot_nt(_stack_heads(q), k) * dec_scr[...]).astype(BF16)
    o = None
    for h in range(C_HEADS):
        t = jnp.dot(sc[h * tq:(h + 1) * tq], jnp.where(head == h, v, jnp.zeros_like(v)),
                    preferred_element_type=F32)
        o = t if o is None else o + t
    if s0_ref is not None:
        o = (o + _bdot(q, s0_ref[0]) * jnp.exp(lgf_l * (qi + 1.0))
             + _bdot(q, s0_ref[1]) * jnp.exp(lgb_l * (seq_len - qi)))
    gm = _group_mean_matrix(C_HEADS * HEAD_DIM)
    dlt = o - _dot_hilo_lhs(o, gm)
    var = _dot_hilo_lhs(dlt * dlt, gm)
    out = (dlt * lax.rsqrt(var + EPS) * gn_ref[...] * jax.nn.silu(g)).astype(BF16)
    if not want_state:
        return out, None
    kpos = lax.broadcasted_iota(jnp.int32, (seq_len, 1), 0).astype(F32)
    sf = _bdot_tn(k * jnp.exp(lgf_l * (seq_len - 1.0 - kpos)), v)
    sb = _bdot_tn(k * jnp.exp(lgb_l * kpos), v)
    return out, (sf, sb)


def _store_retention_state(st_ref, slot, state):
    for s in range(st_ref.shape[0]):
        for d in range(2):
            for h in range(C_HEADS):
                sl = slice(h * HEAD_DIM, (h + 1) * HEAD_DIM)
                st_ref[s, d, h] = state[d][sl, sl] if s == slot else jnp.zeros((HEAD_DIM, HEAD_DIM), F32)


def _retention_kernel(q_ref, g_ref, k_ref, v_ref, dec_ref, gn_ref, s0_ref, o_ref, dec_scr, *, seq_len, tq):
    for b in range(q_ref.shape[0]):
        o_ref[b], _ = _retention_core(q_ref[b], k_ref[b], v_ref[b], g_ref[b], dec_ref, gn_ref, dec_scr,
                                      seq_len=seq_len, i0=pl.program_id(0) * tq,
                                      decay_fill="every_step" if b == 0 else "filled", s0_ref=s0_ref.at[b])


def _retention(z, cg, dec, gn, s0, layer, *, tq=LAT_TQ):
    nb, seq_len = z.shape[:2]
    return pl.pallas_call(
        functools.partial(_retention_kernel, seq_len=seq_len, tq=tq),
        grid=(seq_len // tq,),
        in_specs=[pl.BlockSpec((nb, tq, GROUP_WIDTH), lambda j: (0, j, OFF_CQ // GROUP_WIDTH)),
                  pl.BlockSpec((nb, tq, GROUP_WIDTH), lambda j: (0, j, 0)),
                  pl.BlockSpec((nb, seq_len, GROUP_WIDTH), lambda j: (0, 0, OFF_CK // GROUP_WIDTH)),
                  pl.BlockSpec((nb, seq_len, GROUP_WIDTH), lambda j: (0, 0, OFF_CV // GROUP_WIDTH)),
                  pl.BlockSpec((SUBLANES, LANES), lambda j: (0, 0)),
                  pl.BlockSpec((1, GROUP_WIDTH), lambda j: (0, 0)),
                  pl.BlockSpec((nb, None, 2, GROUP_WIDTH, GROUP_WIDTH), lambda j: (0, layer, 0, 0, 0))],
        out_specs=pl.BlockSpec((nb, tq, GROUP_WIDTH), lambda j: (0, j, 0)),
        out_shape=jax.ShapeDtypeStruct((nb, seq_len, GROUP_WIDTH), BF16),
        scratch_shapes=[pltpu.VMEM((C_HEADS * tq, seq_len), F32)],
        compiler_params=_cparams("arbitrary"),
        name="retention",
    )(z, cg, z, z, dec, gn, s0)


CTX_SEQS = 4


def _ctx_front_kernel(x_ref, mod_ref, g1_ref, w_ref, qn_ref, kn_ref, dec_ref, gn_ref, *rest, n_alias, slot):
    (oa_ref, ob_ref, oc_ref, du_ref, ak_ref, av_ref, bk_ref, bv_ref, st_ref, dec_scr) = rest[n_alias:]
    tq = x_ref.shape[0] // CTX_SEQS
    h = _rms_rows(x_ref[...]) * g1_ref[...] * (1.0 + mod_ref[1:2, :]) + mod_ref[0:1, :]
    zz = jnp.dot(h.astype(BF16), w_ref[...], preferred_element_type=F32)
    for s in range(CTX_SEQS):
        rows = slice(s * tq, (s + 1) * tq)
        z = zz[rows, :]
        aq = _head_norm(z[:, OFF_AQ:OFF_AK], qn_ref[...])
        ak = _head_norm(z[:, OFF_AK:OFF_AV], kn_ref[...])
        av, bq, bk, bv = (z[:, OFF_AV:OFF_BQ], z[:, OFF_BQ:OFF_BK], z[:, OFF_BK:OFF_BV], z[:, OFF_BV:OFF_CQ])
        oa_ref[rows, :] = _mha(_stack_heads_gqa(aq * Q_SCALE), [(ak, _spread_kv_gqa(av), None)], tq)
        ob_ref[rows, :] = _mha(_stack_heads(bq * Q_SCALE), [(bk, bv, None)], tq)
        oc_ref[rows, :], state = _retention_core(
            z[:, OFF_CQ:OFF_CK], z[:, OFF_CK:OFF_CV] * Q_SCALE, z[:, OFF_CV:OFF_CG], z[:, OFF_CG:OFF_DU],
            dec_ref, gn_ref, dec_scr, seq_len=tq, i0=0, decay_fill="first_step" if s == 0 else "filled",
            want_state=True)
        du_ref[:, s * GROUP_WIDTH:(s + 1) * GROUP_WIDTH] = z[:, OFF_DU:]
        _store_layer_slot(ak_ref.at[s], slot, ak)
        _store_layer_slot(av_ref.at[s], slot, av)
        _store_layer_slot(bk_ref.at[s], slot, bk)
        _store_layer_slot(bv_ref.at[s], slot, bv)
        _store_retention_state(st_ref.at[s], slot, state)


def _ctx_front(x, mods, g1, w_in_bf, qn, kn, dec, gn, layer, prev):
    assert SEQ == S5_SEG
    tm = CTX_SEQS * SEQ
    n = x.shape[0]
    nb = n // SEQ
    steps = n // tm
    per_blk = SUBLANES // CTX_SEQS
    const = lambda *shape: pl.BlockSpec(shape, lambda i: (0,) * len(shape))
    row = lambda w: pl.BlockSpec((tm, w), lambda i: (i, 0))
    in_specs = [row(D_MODEL), _mod_spec(layer, 0, steps), const(1, D_MODEL),
                pl.BlockSpec((None, D_MODEL, IN_WIDTH), lambda i: (layer, 0, 0)),
                const(1, GROUP_WIDTH), const(1, KV_WIDTH), const(SUBLANES, LANES), const(1, GROUP_WIDTH)]
    args = [x, mods, g1, w_in_bf, qn, kn, dec, gn]
    out_specs = [row(GROUP_WIDTH), row(GROUP_WIDTH), row(GROUP_WIDTH),
                 pl.BlockSpec((None, S5_SEG, CTX_SEQS * GROUP_WIDTH), lambda i: (i // per_blk, 0, i % per_blk))]
    out_shape = [jax.ShapeDtypeStruct((n, GROUP_WIDTH), BF16)] * 3 + [
        jax.ShapeDtypeStruct((nb // SUBLANES, S5_SEG, SUBLANES * GROUP_WIDTH), F32)]
    first = prev is None
    slot = 0
    for tail in ((SEQ, KV_WIDTH), (SEQ, KV_WIDTH), (SEQ, GROUP_WIDTH), (SEQ, GROUP_WIDTH),
                 (2, C_HEADS, HEAD_DIM, HEAD_DIM)):
        blk, idx, slot = _layer_slot_block(layer, first, tail)
        out_specs.append(pl.BlockSpec((CTX_SEQS,) + blk[1:], lambda i, idx=idx: (i,) + idx))
        out_shape.append(jax.ShapeDtypeStruct((nb, DEPTH) + tail, F32))
    aliases = {}
    if not first:
        for k, arr in enumerate(prev):
            aliases[len(args)] = 4 + k
            in_specs.append(pl.BlockSpec(memory_space=pl.ANY))
            args.append(arr)
    outs = pl.pallas_call(
        functools.partial(_ctx_front_kernel, n_alias=len(aliases), slot=slot),
        grid=(steps,),
        in_specs=in_specs,
        out_specs=out_specs,
        out_shape=out_shape,
        scratch_shapes=[pltpu.VMEM((C_HEADS * SEQ, SEQ), F32)],
        input_output_aliases=aliases,
        compiler_params=_cparams("arbitrary"),
        name="ctx_front",
    )(*args)
    return outs[0], outs[1], outs[2], outs[3], tuple(outs[4:])


def _s5_prep_kernel(lre_ref, lim_ref, ldt_ref, bre_ref, bim_ref, cre_ref, cim_ref,
                    a_ref, bm_ref, cro_ref, cio_ref, bm_scr, cr_scr, ci_scr):
    lre = lre_ref[...]
    lim = lim_ref[...]
    dt = jnp.exp(ldt_ref[...])
    mag = jnp.exp(lre * dt)
    a_re = mag * jnp.cos(lim * dt)
    a_im = mag * jnp.sin(lim * dt)
    den = lre * lre + lim * lim
    r_re = ((a_re - 1.0) * lre + a_im * lim) / den
    r_im = (a_im * lre - (a_re - 1.0) * lim) / den
    bm_scr[...] = jnp.zeros_like(bm_scr)
    cr_scr[...] = jnp.zeros_like(cr_scr)
    ci_scr[...] = jnp.zeros_like(ci_scr)
    for g in range(S5_GROUPS):
        rows = slice(g * S5_CH, (g + 1) * S5_CH)
        cols = slice(g * S5_STATE, (g + 1) * S5_STATE)
        a_ref[0:1, cols] = a_re[g:g + 1, :]
        a_ref[1:2, cols] = a_im[g:g + 1, :]
        rr, ri = r_re[g:g + 1, :], r_im[g:g + 1, :]
        br, bi = bre_ref[g], bim_ref[g]
        bm_scr[rows, cols] = rr * br - ri * bi
        bm_scr[rows, S5_SP + g * S5_STATE:S5_SP + (g + 1) * S5_STATE] = rr * bi + ri * br
        cr_scr[cols, rows] = cre_ref[g]
        ci_scr[cols, rows] = cim_ref[g]
    bm_ref[...] = bm_scr[...].astype(BF16)
    cro_ref[...] = cr_scr[...].astype(BF16)
    cio_ref[...] = ci_scr[...].astype(BF16)


def _s5_prepare(lam_re, lam_im, log_dt, b_re, b_im, c_re, c_im):
    gp = (S5_GROUPS, S5_STATE)
    ldt = jnp.broadcast_to(log_dt[..., None], (DEPTH, 2) + gp)
    bt = [jnp.swapaxes(t, -1, -2) for t in (b_re, b_im)]
    ct = [jnp.swapaxes(t, -1, -2) for t in (c_re, c_im)]

    def spec(*tail):
        return pl.BlockSpec((None, None) + tail, lambda l, d: (l, d) + (0,) * len(tail))

    return pl.pallas_call(
        _s5_prep_kernel,
        grid=(DEPTH, 2),
        in_specs=[spec(*gp)] * 3 + [spec(S5_GROUPS, S5_CH, S5_STATE)] * 2 + [spec(S5_GROUPS, S5_STATE, S5_CH)] * 2,
        out_specs=[spec(2, S5_SP), spec(GROUP_WIDTH, 2 * S5_SP), spec(S5_SP, GROUP_WIDTH), spec(S5_SP, GROUP_WIDTH)],
        out_shape=[jax.ShapeDtypeStruct((DEPTH, 2, 2, S5_SP), F32),
                   jax.ShapeDtypeStruct((DEPTH, 2, GROUP_WIDTH, 2 * S5_SP), BF16),
                   jax.ShapeDtypeStruct((DEPTH, 2, S5_SP, GROUP_WIDTH), BF16),
                   jax.ShapeDtypeStruct((DEPTH, 2, S5_SP, GROUP_WIDTH), BF16)],
        scratch_shapes=[pltpu.VMEM((GROUP_WIDTH, 2 * S5_SP), F32), pltpu.VMEM((S5_SP, GROUP_WIDTH), F32),
                        pltpu.VMEM((S5_SP, GROUP_WIDTH), F32)],
        compiler_params=_cparams("parallel", "parallel"),
        name="s5_prepare",
    )(lam_re, lam_im, ldt, bt[0], bt[1], ct[0], ct[1])


def _cmul(ar, ai, br, bi):
    return ar * br - ai * bi, ar * bi + ai * br


def _s5_kernel(u_ref, h0_ref, a_ref, bm_ref, cre_ref, cim_ref, dvec_ref, glu_ref, *rest, nseg, slot):
    od_ref, fin_ref, x_scr, s_scr, y_scr = rest[-5:]
    steps = S5_SEG
    rows = steps * SUBLANES
    chunk = S5_CHUNK
    chunk_steps = chunk // SUBLANES
    nchunk = rows // chunk
    seg = lax.broadcasted_iota(jnp.int32, (SUBLANES, S5_SP), 0) % nseg

    for d in range(2):
        ar = jnp.broadcast_to(a_ref[d, 0:1, :], (SUBLANES, S5_SP))
        ai = jnp.broadcast_to(a_ref[d, 1:2, :], (SUBLANES, S5_SP))

        def row0(k):
            c = k if d == 0 else nchunk - 1 - k
            return c * chunk if isinstance(c, int) else pl.multiple_of(c * chunk, chunk)

        def input_part(k, buf):
            x_scr[buf] = jnp.dot(u_ref[pl.ds(row0(k), chunk), :].astype(BF16), bm_ref[d],
                                 preferred_element_type=F32)

        def scan_part(buf, carry, store):
            sr, si = carry
            for t in range(chunk_steps):
                r = (t if d == 0 else chunk_steps - 1 - t) * SUBLANES
                pr, pi = _cmul(ar, ai, sr, si)
                sr = pr + x_scr[buf, r:r + SUBLANES, 0:S5_SP]
                si = pi + x_scr[buf, r:r + SUBLANES, S5_SP:]
                if store:
                    s_scr[buf, r:r + SUBLANES, 0:S5_SP] = sr
                    s_scr[buf, r:r + SUBLANES, S5_SP:] = si
            return sr, si

        def output_part(k, buf):
            y = _bdot(s_scr[buf, :, 0:S5_SP], cre_ref[d]) - _bdot(s_scr[buf, :, S5_SP:], cim_ref[d])
            rows_k = pl.ds(row0(k), chunk)
            if d == 0:
                y_scr[rows_k, :] = y
            else:
                zz = jax.nn.gelu(y_scr[rows_k, :] + y + dvec_ref[...] * u_ref[rows_k, :])
                od_ref[rows_k, :] = (zz * jax.nn.sigmoid(_bdot(zz, glu_ref[...]))).astype(BF16)

        def half(k, buf, carry, store, nxt=True, prev=True):
            if nxt:
                input_part(k + 1, 1 - buf)
            carry = scan_part(buf, carry, store)
            if store and prev:
                output_part(k - 1, 1 - buf)
            return carry

        def run_pass(carry, store):
            input_part(0, 0)
            carry = half(0, 0, carry, store, prev=False)
            carry = half(1, 1, carry, store)

            def pair(j, c):
                c = half(2 * j, 0, c, store)
                return half(2 * j + 1, 1, c, store)
            carry = lax.fori_loop(1, nchunk // 2 - 1, pair, carry)
            carry = half(nchunk - 2, 0, carry, store)
            carry = half(nchunk - 1, 1, carry, store, nxt=False)
            if store:
                output_part(nchunk - 1, 1)
            return carry

        init = (h0_ref[d, :, 0:S5_SP], h0_ref[d, :, S5_SP:])
        if nseg > 1:
            zero = jnp.zeros((SUBLANES, S5_SP), F32)
            fr, fi = run_pass((zero, zero), store=False)
            pr, pi = ar, ai
            for _ in range(int(math.log2(steps))):
                pr, pi = _cmul(pr, pi, pr, pi)
            cr, ci = init
            shift = 1 if d == 0 else SUBLANES - 1
            order = range(1, nseg) if d == 0 else range(nseg - 2, -1, -1)
            for s in order:
                ncr, nci = pltpu.roll(cr, shift, 0), pltpu.roll(ci, shift, 0)
                nfr, nfi = pltpu.roll(fr, shift, 0), pltpu.roll(fi, shift, 0)
                qr, qi = _cmul(pr, pi, ncr, nci)
                cr = jnp.where(seg == s, qr + nfr, cr)
                ci = jnp.where(seg == s, qi + nfi, ci)
            init = (cr, ci)
        sr, si = run_pass(init, store=True)
        for s in range(fin_ref.shape[1] // (4 * S5_SP)):
            base = (4 * s + 2 * d) * S5_SP
            fin_ref[:, base:base + S5_SP] = sr if s == slot else jnp.zeros_like(sr)
            fin_ref[:, base + S5_SP:base + 2 * S5_SP] = si if s == slot else jnp.zeros_like(si)


def _s5(du_tm, h0, a, bmat, cre, cim, dvec, glu_bf, layer, *, nseg, fin_layer=0, fin_layers=1,
        prev_fin=None):
    nblk = du_tm.shape[0]
    rows = S5_SEG * SUBLANES
    fin_w = 4 * S5_SP
    in_specs = [pl.BlockSpec((None, rows, GROUP_WIDTH), lambda i: (i, 0, 0)),
                pl.BlockSpec((2, SUBLANES, 2 * S5_SP), lambda i: (0, 0, 0)),
                pl.BlockSpec((None, 2, 2, S5_SP), lambda i: (layer, 0, 0, 0)),
                pl.BlockSpec((None, 2, GROUP_WIDTH, 2 * S5_SP), lambda i: (layer, 0, 0, 0)),
                pl.BlockSpec((None, 2, S5_SP, GROUP_WIDTH), lambda i: (layer, 0, 0, 0)),
                pl.BlockSpec((None, 2, S5_SP, GROUP_WIDTH), lambda i: (layer, 0, 0, 0)),
                pl.BlockSpec((1, GROUP_WIDTH), lambda i: (0, 0)),
                pl.BlockSpec((None, GROUP_WIDTH, GROUP_WIDTH), lambda i: (layer, 0, 0))]
    args = [du_tm.reshape(nblk, rows, GROUP_WIDTH), h0, a, bmat, cre, cim, dvec, glu_bf]
    aliases = {}
    if prev_fin is not None:
        aliases[len(args)] = 1
        in_specs.append(pl.BlockSpec(memory_space=pl.ANY))
        args.append(prev_fin)
        fin_spec, slot = pl.BlockSpec((SUBLANES, fin_w), lambda i: (i, fin_layer)), 0
    else:
        fin_spec, slot = pl.BlockSpec((SUBLANES, fin_layers * fin_w), lambda i: (i, 0)), fin_layer
    od, fin = pl.pallas_call(
        functools.partial(_s5_kernel, nseg=nseg, slot=slot),
        grid=(nblk,),
        in_specs=in_specs,
        out_specs=[pl.BlockSpec((None, rows, GROUP_WIDTH), lambda i: (i, 0, 0)), fin_spec],
        out_shape=[jax.ShapeDtypeStruct((nblk, rows, GROUP_WIDTH), BF16),
                   jax.ShapeDtypeStruct((nblk * SUBLANES, fin_layers * fin_w), F32)],
        scratch_shapes=[pltpu.VMEM((2, S5_CHUNK, 2 * S5_SP), F32), pltpu.VMEM((2, S5_CHUNK, 2 * S5_SP), F32),
                        pltpu.VMEM((rows, GROUP_WIDTH), F32)],
        input_output_aliases=aliases,
        compiler_params=_cparams("parallel"),
        name="s5",
    )(*args)
    return od.reshape(nblk, S5_SEG, SUBLANES * GROUP_WIDTH), fin


ROUTE_GROUP = MOE_PER_GROUP
OUT_SEQS = 2
MOE_TILE = 512


def _out_kernel(x_ref, oa_ref, ob_ref, oc_ref, od_ref, mod_ref, wo_ref, g2_ref, wrh_ref, wrl_ref, br_ref,
                xm_ref, h2_ref, route_ref, cnt_ref):
    od = jnp.concatenate([od_ref[:, s * GROUP_WIDTH:(s + 1) * GROUP_WIDTH] for s in range(OUT_SEQS)], axis=0)
    mix = functools.reduce(jnp.add, [
        _bdot(o, wo_ref[i * GROUP_WIDTH:(i + 1) * GROUP_WIDTH, :])
        for i, o in enumerate((oa_ref[...], ob_ref[...], oc_ref[...], od))])
    xm = x_ref[...] + mod_ref[2:3, :] * mix
    xm_ref[...] = xm
    h2 = _rms_rows(xm) * g2_ref[...] * (1.0 + mod_ref[4:5, :]) + mod_ref[3:4, :]
    h2_ref[...] = h2.astype(BF16)

    h_hi, h_lo = _split(h2)
    logits = (jnp.dot(h_hi, wrh_ref[...], preferred_element_type=F32)
              + jnp.dot(h_hi, wrl_ref[...], preferred_element_type=F32)
              + jnp.dot(h_lo, wrh_ref[...], preferred_element_type=F32)) + br_ref[...]
    lane_i = lax.broadcasted_iota(jnp.int32, logits.shape, 1)
    lane = lane_i.astype(F32)
    big = jnp.float32(2 ** 30)
    gmask = lane_i < MOE_GROUPS
    gl = jnp.where(gmask, logits, -jnp.inf)
    gmax = jnp.max(gl, axis=-1, keepdims=True)
    p_top = 1.0 / jnp.sum(jnp.exp(gl - gmax), axis=-1, keepdims=True)
    g_top = jnp.min(jnp.where(gl == gmax, lane, big), axis=-1, keepdims=True)
    e_lane = lane_i - ROUTER_OFF
    lane_group = (e_lane // MOE_PER_GROUP).astype(F32)
    emask = (e_lane >= 0) & (e_lane < MOE_EXPERTS) & (lane_group == g_top)
    el = jnp.where(emask, logits, -jnp.inf)
    m1 = jnp.max(el, axis=-1, keepdims=True)
    i1 = jnp.min(jnp.where(el == m1, lane, big), axis=-1, keepdims=True)
    el2 = jnp.where(lane == i1, -jnp.inf, el)
    m2 = jnp.max(el2, axis=-1, keepdims=True)
    i2 = jnp.min(jnp.where(el2 == m2, lane, big), axis=-1, keepdims=True)
    e2 = jnp.exp(m2 - m1)
    den = 1.0 + e2
    gates = (jnp.where(lane == i1, (1.0 / den) * p_top, 0.0)
             + jnp.where(lane == i2, (e2 / den) * p_top, 0.0))
    route = jnp.where(lane == ROUTE_GROUP + g_top, 1.0, 0.0)
    for g in range(MOE_GROUPS):
        local = pltpu.roll(gates, LANES - ROUTER_OFF - g * MOE_PER_GROUP, 1)
        route = route + jnp.where((g_top == g) & (lane_i < MOE_PER_GROUP), local, 0.0)
    route_ref[...] = route
    for t in range(cnt_ref.shape[0]):
        part = jnp.sum(route[t * MOE_TILE:(t + 1) * MOE_TILE], axis=0, keepdims=True)
        cnt_ref[t] = jnp.broadcast_to(part, (SUBLANES, LANES)).astype(jnp.int32)


def _output_stage(x, mixes, mods, mod_row, mod_tokens, wo_bf, g2, wr_hi, wr_lo, br, layer):
    tm = OUT_SEQS * S5_SEG
    n = x.shape[0]
    row = lambda w: pl.BlockSpec((tm, w), lambda i: (i, 0))
    const = lambda shape: pl.BlockSpec(shape, lambda i: (0,) * len(shape))
    per_blk = SUBLANES // OUT_SEQS
    return pl.pallas_call(
        _out_kernel,
        grid=(n // tm,),
        in_specs=[row(D_MODEL), row(GROUP_WIDTH), row(GROUP_WIDTH), row(GROUP_WIDTH),
                  pl.BlockSpec((None, S5_SEG, OUT_SEQS * GROUP_WIDTH), lambda i: (i // per_blk, 0, i % per_blk)),
                  _mod_spec(layer, mod_row, mod_tokens // tm),
                  pl.BlockSpec((None, D_MODEL, D_MODEL), lambda i: (layer, 0, 0)), const((1, D_MODEL)),
                  const((D_MODEL, LANES)), const((D_MODEL, LANES)), const((1, LANES))],
        out_specs=[row(D_MODEL), row(D_MODEL), row(LANES),
                   pl.BlockSpec((tm // MOE_TILE, SUBLANES, LANES), lambda i: (i, 0, 0))],
        out_shape=[jax.ShapeDtypeStruct((n, D_MODEL), F32),
                   jax.ShapeDtypeStruct((n, D_MODEL), BF16),
                   jax.ShapeDtypeStruct((n, LANES), F32),
                   jax.ShapeDtypeStruct((n // MOE_TILE, SUBLANES, LANES), jnp.int32)],
        compiler_params=_cparams("parallel"),
        name="output_stage",
    )(x, *mixes, mods, wo_bf, g2, wr_hi, wr_lo, br)


GROUP_HID = MOE_PER_GROUP * MOE_HIDDEN


def _moe_kernel(h2_ref, route_ref, xm_ref, mod_ref, w1_ref, w3_ref, w2_ref, fg_ref, o_ref, acc_ref, *, final):
    g = pl.program_id(1)
    h2 = h2_ref[...]
    a = jnp.dot(h2, w1_ref[...], preferred_element_type=F32)
    b = jnp.dot(h2, w3_ref[...], preferred_element_type=F32)
    route = route_ref[...]
    lane = lax.broadcasted_iota(jnp.int32, route.shape, 1)
    in_group = jnp.sum(jnp.where(lane == ROUTE_GROUP + g, route, 0.0), axis=-1, keepdims=True)
    gates = route * in_group
    hid = []
    for e in range(MOE_PER_GROUP):
        sl = slice(e * MOE_HIDDEN, (e + 1) * MOE_HIDDEN)
        hid.append((jax.nn.silu(a[:, sl]) * b[:, sl] * gates[:, e:e + 1]).astype(BF16))
    part = jnp.dot(jnp.concatenate(hid, axis=1), w2_ref[...], preferred_element_type=F32)

    @pl.when(g == 0)
    def _():
        acc_ref[...] = part

    @pl.when(g > 0)
    def _():
        acc_ref[...] += part

    @pl.when(g == MOE_GROUPS - 1)
    def _():
        out = xm_ref[...] + mod_ref[5:6, :] * acc_ref[...]
        if final:
            out = _rms_rows(out) * fg_ref[...]
        o_ref[...] = out


def _moe_weight_kernel(w1_ref, w3_ref, w2_ref, o1_ref, o3_ref, o2_ref):
    for e in range(MOE_PER_GROUP):
        sl = slice(e * MOE_HIDDEN, (e + 1) * MOE_HIDDEN)
        o1_ref[:, sl] = w1_ref[e].astype(BF16)
        o3_ref[:, sl] = w3_ref[e].astype(BF16)
        o2_ref[sl, :] = w2_ref[e].astype(BF16)


def _moe_weights(w1, w3, w2):
    up = pl.BlockSpec((None, MOE_PER_GROUP, D_MODEL, MOE_HIDDEN), lambda l, g: (l, g, 0, 0))
    down = pl.BlockSpec((None, MOE_PER_GROUP, MOE_HIDDEN, D_MODEL), lambda l, g: (l, g, 0, 0))
    out = pl.BlockSpec((None, None, D_MODEL, GROUP_HID), lambda l, g: (l, g, 0, 0))
    shape = jax.ShapeDtypeStruct((DEPTH, MOE_GROUPS, D_MODEL, GROUP_HID), BF16)
    return pl.pallas_call(
        _moe_weight_kernel,
        grid=(DEPTH, MOE_GROUPS),
        in_specs=[up, up, down],
        out_specs=[out, out, out],
        out_shape=[shape, shape, shape],
        compiler_params=_cparams("parallel", "parallel"),
        name="moe_weights",
    )(w1, w3, w2)


def _moe(h2, route, tile_counts, xm, mods, mod_row, mod_tokens, w1g, w3g, w2g, fg, layer, *, final):
    del tile_counts
    tm = MOE_TILE
    n = h2.shape[0]
    row = lambda w: pl.BlockSpec((tm, w), lambda i, g: (i, 0))
    mod_tiles = mod_tokens // tm
    wspec = pl.BlockSpec((None, None, D_MODEL, GROUP_HID), lambda i, g: (layer, g, 0, 0))
    return pl.pallas_call(
        functools.partial(_moe_kernel, final=final),
        grid=(n // tm, MOE_GROUPS),
        in_specs=[row(D_MODEL), row(LANES), row(D_MODEL),
                  pl.BlockSpec((None, None, N_MOD, D_MODEL),
                               lambda i, g: (layer, mod_row + i // mod_tiles, 0, 0)),
                  wspec, wspec, wspec,
                  pl.BlockSpec((1, D_MODEL), lambda i, g: (0, 0))],
        out_specs=row(D_MODEL),
        out_shape=jax.ShapeDtypeStruct((n, D_MODEL), F32),
        scratch_shapes=[pltpu.VMEM((tm, D_MODEL), F32)],
        compiler_params=_cparams("parallel", "arbitrary"),
        name="moe",
    )(h2, route, xm, mods, w1g, w3g, w2g, fg)


def kernel(x_prompt, x_sample, cache_a_k, cache_a_v, cache_b_k, cache_b_v, state_ret, state_ssm, c, c_ctx, mod_w, mod_b, norm1_g, norm2_g, w_in, a_qn_g, a_kn_g, b_rel_bias, ret_decay, ret_gn_g, s5_lam_re, s5_lam_im, s5_log_dt, s5_b_re, s5_b_im, s5_c_re, s5_c_im, s5_d, s5_glu_w, w_out, moe_gw, moe_gb, moe_ew, moe_eb, moe_w1, moe_w3, moe_w2, final_norm_g):
    n_ctx = BATCH * SEQ
    n_lat = DEC_BATCH * DEC_SEQ
    lat_seg = DEC_SEQ // S5_SEG

    cond = jnp.zeros((SUBLANES, D_MODEL), F32).at[0].set(c_ctx).at[1:1 + DEC_BATCH].set(c)
    mods = _modulation(cond, mod_w, mod_b).reshape(DEPTH, SUBLANES, N_MOD, D_MODEL)

    rope_tabs = _rope_tables()
    s5_a, s5_bm, s5_cre, s5_cim = _s5_prepare(s5_lam_re, s5_lam_im, s5_log_dt, s5_b_re, s5_b_im,
                                              s5_c_re, s5_c_im)
    cak = cache_a_k.reshape(DEC_BATCH, DEPTH, PAST_LEN, A_KV_HEADS * HEAD_DIM)
    cav = cache_a_v.reshape(DEC_BATCH, DEPTH, PAST_LEN, A_KV_HEADS * HEAD_DIM)
    cbk = cache_b_k.reshape(DEC_BATCH, DEPTH, PAST_LEN, B_HEADS * HEAD_DIM)
    cbv = cache_b_v.reshape(DEC_BATCH, DEPTH, PAST_LEN, B_HEADS * HEAD_DIM)

    xc = x_prompt.reshape(n_ctx, D_MODEL)
    xs = x_sample.reshape(n_lat, D_MODEL)
    w1_all, w3_all, w2_all = _moe_weights(moe_w1, moe_w3, moe_w2)
    eye_h = jnp.eye(C_HEADS, dtype=F32)
    s0_bd = (state_ret[:, :, :, :, :, None, :] * eye_h[None, None, None, :, None, :, None]).reshape(
        DEC_BATCH, DEPTH, 2, C_HEADS * HEAD_DIM, C_HEADS * HEAD_DIM)
    ctx_state = ssm_states = None
    h0_zero = jnp.zeros((2, SUBLANES, 2 * S5_SP), F32)
    w_in_bf = w_in.astype(BF16)
    wo_bf = w_out.astype(BF16)
    glu_bf = s5_glu_w.astype(BF16)
    for l in range(DEPTH):
        final = l == DEPTH - 1
        g1 = norm1_g[l].reshape(1, D_MODEL)
        g2 = norm2_g[l].reshape(1, D_MODEL)
        fg = final_norm_g.reshape(1, D_MODEL)
        qn = jnp.tile(a_qn_g[l], A_HEADS).reshape(1, GROUP_WIDTH)
        kn = jnp.tile(a_kn_g[l], A_KV_HEADS).reshape(1, KV_WIDTH)
        dec = jnp.broadcast_to(ret_decay[l].reshape(2 * C_HEADS, 1), (2 * C_HEADS, LANES))
        gn = ret_gn_g[l].reshape(1, GROUP_WIDTH)
        dvec = s5_d[l].reshape(1, GROUP_WIDTH)
        wr = jnp.zeros((D_MODEL, LANES), F32).at[:, :MOE_GROUPS].set(moe_gw[l]).at[
            :, ROUTER_OFF:ROUTER_OFF + MOE_EXPERTS].set(moe_ew[l])
        br = jnp.zeros((1, LANES), F32).at[0, :MOE_GROUPS].set(moe_gb[l]).at[
            0, ROUTER_OFF:ROUTER_OFF + MOE_EXPERTS].set(moe_eb[l])
        wr_hi = wr.astype(BF16)
        wr_lo = (wr - wr_hi.astype(F32)).astype(BF16)
        na_bias = _na_bias(b_rel_bias[l])

        oa, ob, oc, du_tm, ctx_state = _ctx_front(xc, mods, g1, w_in_bf, qn, kn, dec, gn, l, ctx_state)
        od_tm, ssm_states = _s5(du_tm, h0_zero, s5_a, s5_bm, s5_cre, s5_cim, dvec, glu_bf, l,
                                nseg=1, fin_layer=l, fin_layers=DEPTH, prev_fin=ssm_states)
        xm, h2, route, counts = _output_stage(xc, (oa, ob, oc, od_tm), mods, 0, n_ctx, wo_bf, g2,
                                              wr_hi, wr_lo, br, l)
        xc = _moe(h2, route, counts, xm, mods, 0, n_ctx, w1_all, w3_all, w2_all, fg, l, final=final)

        zs, cg, du_tm = _project(xs, mods, 1, DEC_SEQ, g1, w_in_bf, qn, kn, rope_tabs, l, seq_len=DEC_SEQ)
        zs3 = zs.reshape(DEC_BATCH, DEC_SEQ, OFF_CG)
        oa = _lat_attention_a(zs3, cak, cav, l).reshape(n_lat, GROUP_WIDTH)
        ob = _lat_attention_b(zs3, cbk, cbv, na_bias, l).reshape(n_lat, GROUP_WIDTH)
        oc = _retention(zs3, cg.reshape(DEC_BATCH, DEC_SEQ, GROUP_WIDTH), dec, gn, s0_bd, l).reshape(
            n_lat, GROUP_WIDTH)
        h0 = state_ssm[:, l].reshape(DEC_BATCH, 2, 2 * S5_SP).transpose(1, 0, 2)
        h0_seg = jnp.zeros((2, DEC_BATCH, lat_seg, 2 * S5_SP), F32)
        h0_seg = h0_seg.at[0, :, 0].set(h0[0]).at[1, :, lat_seg - 1].set(h0[1])
        od_tm, _ = _s5(du_tm, h0_seg.reshape(2, SUBLANES, 2 * S5_SP),
                       s5_a, s5_bm, s5_cre, s5_cim, dvec, glu_bf, l, nseg=lat_seg)
        xm, h2, route, counts = _output_stage(xs, (oa, ob, oc, od_tm), mods, 1, DEC_SEQ, wo_bf, g2,
                                              wr_hi, wr_lo, br, l)
        xs = _moe(h2, route, counts, xm, mods, 1, DEC_SEQ, w1_all, w3_all, w2_all, fg, l, final=final)

    new_ak, new_av, new_bk, new_bv, ret_states = ctx_state
    return (xc.reshape(BATCH, SEQ, D_MODEL), xs.reshape(DEC_BATCH, DEC_SEQ, D_MODEL),
            new_ak.reshape(BATCH, DEPTH, SEQ, A_KV_HEADS, HEAD_DIM),
            new_av.reshape(BATCH, DEPTH, SEQ, A_KV_HEADS, HEAD_DIM),
            new_bk.reshape(BATCH, DEPTH, SEQ, B_HEADS, HEAD_DIM),
            new_bv.reshape(BATCH, DEPTH, SEQ, B_HEADS, HEAD_DIM),
            ret_states,
            ssm_states.reshape(BATCH, DEPTH, 2, 2, S5_GROUPS, S5_STATE))
```

```python
import functools
import math

import numpy as np
import jax
import jax.numpy as jnp
from jax import lax
from jax.experimental import pallas as pl
from jax.experimental.pallas import tpu as pltpu

F32 = jnp.float32
BF16 = jnp.bfloat16

D_MODEL = 1024
BATCH = 32
SEQ = 256
DEPTH = 2
DEC_BATCH = 2
DEC_SEQ = 1024
PAST_LEN = 256
GRID_W = 64
HEAD_DIM = 64
GROUP_WIDTH = 256
A_HEADS = 4
A_KV_HEADS = 2
B_HEADS = 4
NA_ROWS = 8
NA_COLS = 16
C_HEADS = 4
S5_CH = 16
S5_GROUPS = 16
S5_STATE = 64
MOE_GROUPS = 4
MOE_PER_GROUP = 8
MOE_EXPERTS = 32
MOE_HIDDEN = 128
ROPE_THETA = 10000.0
EPS = 1e-6
IN_WIDTH = 2560
Q_SCALE = HEAD_DIM ** -0.5
KV_WIDTH = A_KV_HEADS * HEAD_DIM
N_MOD = 6
ROPE_PAIR = HEAD_DIM // 4
LAT_TQ = 256
MOD_TILE = 1536

OFF_AQ, OFF_AK, OFF_AV = 0, 256, 384
OFF_BQ, OFF_BK, OFF_BV = 512, 768, 1024
OFF_CQ, OFF_CK, OFF_CV, OFF_CG = 1280, 1536, 1792, 2048
OFF_DU = 2304

LANES = 128
SUBLANES = 8
BF16_ROWS = 16
S5_SP = S5_GROUPS * S5_STATE
S5_SEG = 256
S5_CHUNK = 256
ROUTER_OFF = 4
NEG_BIG = -1e30
VMEM_LIMIT = 56 * 1024 * 1024


def _cparams(*sem):
    return pltpu.CompilerParams(dimension_semantics=sem, vmem_limit_bytes=VMEM_LIMIT)


def _mod_spec(layer, first_row, tiles_per_row):
    return pl.BlockSpec((None, None, N_MOD, D_MODEL), lambda i: (layer, first_row + i // tiles_per_row, 0, 0))


def _bdot(a, b):
    return jnp.dot(a.astype(BF16), b.astype(BF16), preferred_element_type=F32)


def _bdot_nt(a, b):
    return lax.dot_general(a.astype(BF16), b.astype(BF16), (((1,), (1,)), ((), ())),
                           preferred_element_type=F32)


def _bdot_tn(a, b):
    return lax.dot_general(a.astype(BF16), b.astype(BF16), (((0,), (0,)), ((), ())),
                           preferred_element_type=F32)


def _split(a):
    hi = a.astype(BF16)
    lo = (a - hi.astype(F32)).astype(BF16)
    return hi, lo


def _dot_hilo_lhs(a, b_bf16):
    hi, lo = _split(a)
    return (jnp.dot(hi, b_bf16, preferred_element_type=F32)
            + jnp.dot(lo, b_bf16, preferred_element_type=F32))


def _rms_rows(x):
    return x * lax.rsqrt(jnp.mean(x * x, axis=-1, keepdims=True) + EPS)


def _mod_kernel(cond_ref, w_ref, b_ref, o_ref):
    o_ref[...] = _bdot(jax.nn.silu(cond_ref[...]), w_ref[...]) + b_ref[...]


def _modulation(cond, mod_w, mod_b):
    tn = MOD_TILE
    width = N_MOD * D_MODEL
    return pl.pallas_call(
        _mod_kernel,
        grid=(DEPTH, width // tn),
        in_specs=[pl.BlockSpec((SUBLANES, D_MODEL), lambda l, j: (0, 0)),
                  pl.BlockSpec((None, D_MODEL, tn), lambda l, j: (l, 0, j)),
                  pl.BlockSpec((None, 1, tn), lambda l, j: (l, 0, j))],
        out_specs=pl.BlockSpec((None, SUBLANES, tn), lambda l, j: (l, 0, j)),
        out_shape=jax.ShapeDtypeStruct((DEPTH, SUBLANES, width), F32),
        compiler_params=_cparams("arbitrary", "arbitrary"),
        name="modulation",
    )(cond, mod_w, mod_b.reshape(DEPTH, 1, width))


def _group_mean_matrix(w):
    ri = lax.broadcasted_iota(jnp.int32, (w, w), 0) // HEAD_DIM
    ci = lax.broadcasted_iota(jnp.int32, (w, w), 1) // HEAD_DIM
    return jnp.where(ri == ci, 1.0 / HEAD_DIM, 0.0).astype(BF16)


def _head_norm(t, g):
    ms = _dot_hilo_lhs(t * t, _group_mean_matrix(t.shape[1]))
    return t * lax.rsqrt(ms + EPS) * g


def _rope(t, cos, sa, sb):
    return (t * cos + pltpu.roll(t, LANES - ROPE_PAIR, 1) * sa + pltpu.roll(t, ROPE_PAIR, 1) * sb)


def _store_layer_slot(ref, slot, value):
    for s in range(ref.shape[0]):
        ref[s] = value if s == slot else jnp.zeros_like(value)


def _layer_slot_block(layer, first_call, tail):
    if first_call:
        return (None, DEPTH) + tail, (0,) * (1 + len(tail)), layer
    return (None, 1) + tail, (layer,) + (0,) * len(tail), 0


def _proj_kernel(x_ref, mod_ref, g1_ref, w_ref, qn_ref, kn_ref, cos_ref, sa_ref, sb_ref, z_ref, cg_ref, du_ref):
    h = _rms_rows(x_ref[...]) * g1_ref[...] * (1.0 + mod_ref[1:2, :]) + mod_ref[0:1, :]
    z = jnp.dot(h.astype(BF16), w_ref[...], preferred_element_type=F32)
    aq = _head_norm(z[:, OFF_AQ:OFF_AK], qn_ref[...])
    ak = _head_norm(z[:, OFF_AK:OFF_AV], kn_ref[...])
    for j in range(3):
        t = aq[:, j * LANES:(j + 1) * LANES] if j < 2 else ak
        sl = slice(0, LANES) if j == 2 else slice(j * LANES, (j + 1) * LANES)
        t = _rope(t, cos_ref[:, sl], sa_ref[:, sl], sb_ref[:, sl])
        z_ref[:, j * LANES:(j + 1) * LANES] = t.astype(BF16)
    z_ref[:, OFF_AV:OFF_CK] = z[:, OFF_AV:OFF_CK].astype(BF16)
    z_ref[:, OFF_CK:OFF_CV] = (z[:, OFF_CK:OFF_CV] * Q_SCALE).astype(BF16)
    z_ref[:, OFF_CV:OFF_CG] = z[:, OFF_CV:OFF_CG].astype(BF16)
    cg_ref[...] = z[:, OFF_CG:OFF_DU]
    du_ref[...] = z[:, OFF_DU:]


def _du_spec():
    return pl.BlockSpec((None, S5_SEG, GROUP_WIDTH), lambda i: (i // SUBLANES, 0, i % SUBLANES))


def _project(x, mods, mod_row, mod_tokens, g1, w_in_bf, qn, kn, rope_tabs, layer, *, seq_len):
    tm = S5_SEG
    n = x.shape[0]
    tps = seq_len // tm
    return pl.pallas_call(
        _proj_kernel,
        grid=(n // tm,),
        in_specs=[pl.BlockSpec((tm, D_MODEL), lambda i: (i, 0)),
                  _mod_spec(layer, mod_row, mod_tokens // tm),
                  pl.BlockSpec((1, D_MODEL), lambda i: (0, 0)),
                  pl.BlockSpec((None, D_MODEL, IN_WIDTH), lambda i: (layer, 0, 0)),
                  pl.BlockSpec((1, GROUP_WIDTH), lambda i: (0, 0)),
                  pl.BlockSpec((1, KV_WIDTH), lambda i: (0, 0))]
                 + [pl.BlockSpec((tm, GROUP_WIDTH), lambda i: (i % tps, 0))] * 3,
        out_specs=[pl.BlockSpec((tm, OFF_CG), lambda i: (i, 0)),
                   pl.BlockSpec((tm, GROUP_WIDTH), lambda i: (i, 0)), _du_spec()],
        out_shape=[jax.ShapeDtypeStruct((n, OFF_CG), BF16),
                   jax.ShapeDtypeStruct((n, GROUP_WIDTH), F32),
                   jax.ShapeDtypeStruct((n // (tm * SUBLANES), S5_SEG, SUBLANES * GROUP_WIDTH), F32)],
        compiler_params=_cparams("parallel"),
        name="project",
    )(x, mods, g1, w_in_bf, qn, kn, *rope_tabs)


def _rope_tables():
    t = jnp.arange(DEC_SEQ)
    row = (t // GRID_W).astype(F32)
    col = (t % GRID_W).astype(F32)
    nf = HEAD_DIM // 4
    inv = ROPE_THETA ** (-jnp.arange(nf, dtype=F32) / nf)
    ang_r = row[:, None] * inv[None, :]
    ang_c = col[:, None] * inv[None, :]
    zeros = jnp.zeros_like(ang_r)
    cos = jnp.concatenate([jnp.cos(ang_r), jnp.cos(ang_r), jnp.cos(ang_c), jnp.cos(ang_c)], axis=-1)
    sa = jnp.concatenate([-jnp.sin(ang_r), zeros, -jnp.sin(ang_c), zeros], axis=-1)
    sb = jnp.concatenate([zeros, jnp.sin(ang_r), zeros, jnp.sin(ang_c)], axis=-1)
    return tuple(jnp.tile(a, (1, 4)) for a in (cos, sa, sb))


N_HEADS = 4


def _lane_head(width):
    return lax.broadcasted_iota(jnp.int32, (1, width), 1) // HEAD_DIM


def _stack_heads(q):
    head = _lane_head(q.shape[1])
    return jnp.concatenate([jnp.where(head == h, q, 0.0) for h in range(N_HEADS)], axis=0).astype(BF16)


def _stack_heads_gqa(q):
    lo = lax.broadcasted_iota(jnp.int32, (1, LANES), 1) < HEAD_DIM
    q = q.astype(F32)
    q01, q23 = q[:, :LANES], q[:, LANES:]
    blocks = [jnp.where(lo, q01, 0.0), jnp.where(lo, pltpu.roll(q01, HEAD_DIM, 1), 0.0),
              jnp.where(lo, 0.0, pltpu.roll(q23, HEAD_DIM, 1)), jnp.where(lo, 0.0, q23)]
    return jnp.concatenate(blocks, axis=0).astype(BF16)


def _spread_kv_gqa(v):
    lo = lax.broadcasted_iota(jnp.int32, (1, LANES), 1) < HEAD_DIM
    v = v.astype(F32)
    vr = pltpu.roll(v, HEAD_DIM, 1)
    return jnp.concatenate([jnp.where(lo, v, vr), jnp.where(lo, vr, v)], axis=1)


def _mha(qs, blocks, tq):
    scores = []
    for k, _, bias in blocks:
        s = _bdot_nt(qs, k)
        scores.append(s if bias is None else s + bias)
    m = functools.reduce(jnp.maximum, [jnp.max(s, axis=-1, keepdims=True) for s in scores])
    es = [jnp.exp(s - m) for s in scores]
    denom = functools.reduce(jnp.add, [jnp.sum(e, axis=-1, keepdims=True) for e in es])
    ps = [e.astype(BF16) for e in es]
    head = _lane_head(N_HEADS * HEAD_DIM)
    vals = [v.astype(BF16) for _, v, _ in blocks]
    o = None
    dall = None
    for h in range(N_HEADS):
        rows = slice(h * tq, (h + 1) * tq)
        for p, v in zip(ps, vals):
            t = jnp.dot(p[rows], jnp.where(head == h, v, jnp.zeros_like(v)), preferred_element_type=F32)
            o = t if o is None else o + t
        d = jnp.where(head == h, denom[rows], 0.0)
        dall = d if dall is None else dall + d
    return (o / dall).astype(BF16)


def _lat_attn_a_kernel(q_ref, kn_ref, vn_ref, kc_ref, vc_ref, o_ref):
    for b in range(DEC_BATCH):
        o_ref[b] = _mha(_stack_heads_gqa(q_ref[b] * Q_SCALE),
                        [(kc_ref[b], _spread_kv_gqa(vc_ref[b]), None),
                         (kn_ref[b], _spread_kv_gqa(vn_ref[b]), None)], q_ref.shape[1])


def _lat_attention_a(z, cache_k, cache_v, layer, tq=LAT_TQ):
    cache_spec = pl.BlockSpec((DEC_BATCH, None, PAST_LEN, KV_WIDTH), lambda j: (0, layer, 0, 0))
    return pl.pallas_call(
        _lat_attn_a_kernel,
        grid=(DEC_SEQ // tq,),
        in_specs=[pl.BlockSpec((DEC_BATCH, tq, GROUP_WIDTH), lambda j: (0, j, OFF_AQ // GROUP_WIDTH)),
                  pl.BlockSpec((DEC_BATCH, DEC_SEQ, KV_WIDTH), lambda j: (0, 0, OFF_AK // KV_WIDTH)),
                  pl.BlockSpec((DEC_BATCH, DEC_SEQ, KV_WIDTH), lambda j: (0, 0, OFF_AV // KV_WIDTH)),
                  cache_spec, cache_spec],
        out_specs=pl.BlockSpec((DEC_BATCH, tq, GROUP_WIDTH), lambda j: (0, j, 0)),
        out_shape=jax.ShapeDtypeStruct((DEC_BATCH, DEC_SEQ, GROUP_WIDTH), BF16),
        compiler_params=_cparams("parallel"),
        name="lat_attention_a",
    )(z, z, z, cache_k, cache_v)


NA_KEYS = NA_ROWS * GRID_W


NA_PAIRS = 2 * NA_ROWS - 2


NA_STEP_ROWS = 2


def _na_kernel(q_ref, k_ref, v_ref, kc_ref, vc_ref, bias_ref, o_ref):
    rows = DEC_SEQ // GRID_W
    for b in range(q_ref.shape[0]):
        outs = []
        for rr in range(NA_STEP_ROWS):
            r = pl.program_id(0) * NA_STEP_ROWS + rr
            row_start = jnp.clip(r - NA_ROWS // 2, 0, rows - NA_ROWS)
            start = pl.multiple_of(row_start * GRID_W, GRID_W)
            rel0 = row_start - r + NA_ROWS - 1
            kl = k_ref[b, pl.ds(start, NA_KEYS), :]
            vl = v_ref[b, pl.ds(start, NA_KEYS), :]
            bias = jnp.concatenate(
                [jnp.concatenate([bias_ref[h, rel0 + 2 * jp] for jp in range(NA_ROWS // 2)], axis=1)
                 for h in range(B_HEADS)], axis=0)
            qrows = slice(rr * GRID_W, (rr + 1) * GRID_W)
            outs.append(_mha(_stack_heads(q_ref[b, qrows, :] * Q_SCALE),
                             [(kl, vl, bias), (kc_ref[b], vc_ref[b], None)], GRID_W))
        o_ref[b] = jnp.concatenate(outs, axis=0)


def _na_bias(rel_bias):
    nrel = 2 * NA_COLS - 1
    period = 2 * GRID_W
    b = rel_bias.astype(F32)
    ext = jnp.concatenate([b[..., NA_COLS - 1:],
                           jnp.zeros(b.shape[:-1] + (period - nrel,), F32),
                           b[..., :NA_COLS - 1]], axis=-1)
    flat = jnp.tile(ext, (1, 1, GRID_W))[..., :GRID_W * (period - 1)]
    toe = flat.reshape(b.shape[:-1] + (GRID_W, period - 1))[..., :GRID_W]
    col_start = np.clip(np.arange(GRID_W) - NA_COLS // 2, 0, GRID_W - NA_COLS)
    kc = np.arange(GRID_W)
    inside = (kc[None, :] >= col_start[:, None]) & (kc[None, :] < col_start[:, None] + NA_COLS)
    toe = jnp.where(jnp.asarray(inside), toe, NEG_BIG)
    return jnp.concatenate([toe[:, :-1], toe[:, 1:]], axis=-1)


def _lat_attention_b(z, cache_k, cache_v, bias, layer):
    tq = NA_STEP_ROWS * GRID_W
    cache_spec = pl.BlockSpec((DEC_BATCH, None, PAST_LEN, GROUP_WIDTH), lambda r: (0, layer, 0, 0))
    return pl.pallas_call(
        _na_kernel,
        grid=(DEC_SEQ // tq,),
        in_specs=[pl.BlockSpec((DEC_BATCH, tq, GROUP_WIDTH), lambda r: (0, r, OFF_BQ // GROUP_WIDTH)),
                  pl.BlockSpec((DEC_BATCH, DEC_SEQ, GROUP_WIDTH), lambda r: (0, 0, OFF_BK // GROUP_WIDTH)),
                  pl.BlockSpec((DEC_BATCH, DEC_SEQ, GROUP_WIDTH), lambda r: (0, 0, OFF_BV // GROUP_WIDTH)),
                  cache_spec, cache_spec,
                  pl.BlockSpec((B_HEADS, NA_PAIRS, GRID_W, 2 * GRID_W), lambda r: (0, 0, 0, 0))],
        out_specs=pl.BlockSpec((DEC_BATCH, tq, GROUP_WIDTH), lambda r: (0, r, 0)),
        out_shape=jax.ShapeDtypeStruct((DEC_BATCH, DEC_SEQ, GROUP_WIDTH), BF16),
        compiler_params=_cparams("parallel"),
        name="lat_attention_b",
    )(z, z, z, cache_k, cache_v, bias)


def _retention_core(q, k, v, g, dec_ref, gn_ref, dec_scr, *, seq_len, i0, decay_fill, s0_ref=None,
                    want_state=False):
    tq = q.shape[0]
    head = _lane_head(C_HEADS * HEAD_DIM)
    lg = jax.nn.log_sigmoid(dec_ref[...])

    def per_lane(row0):
        out = jnp.zeros((1, C_HEADS * HEAD_DIM), F32)
        for h in range(C_HEADS):
            out = jnp.where(head == h, lg[row0 + h:row0 + h + 1, 0:1], out)
        return out

    lgf_l, lgb_l = per_lane(0), per_lane(C_HEADS)
    qi = (i0 + lax.broadcasted_iota(jnp.int32, (tq, 1), 0)).astype(F32)

    def fill_decay():
        kj = lax.broadcasted_iota(jnp.int32, (1, seq_len), 1).astype(F32)
        diff = qi - kj
        for h in range(C_HEADS):
            lgf = lg[h:h + 1, 0:1]
            lgb = lg[C_HEADS + h:C_HEADS + h + 1, 0:1]
            dec_scr[h * tq:(h + 1) * tq, :] = (
                jnp.where(diff >= 0, jnp.exp(lgf * jnp.maximum(diff, 0.0)), 0.0)
                + jnp.where(diff <= 0, jnp.exp(lgb * jnp.maximum(-diff, 0.0)), 0.0))

    if decay_fill == "first_step":
        pl.when(pl.program_id(0) == 0)(fill_decay)
    elif decay_fill == "every_step":
        fill_decay()
    else:
        assert decay_fill == "filled"

    v = v.astype(BF16)
    sc = (_bdot_nt(_stack_heads(q), k) * dec_scr[...]).astype(BF16)
    o = None
    for h in range(C_HEADS):
        t = jnp.dot(sc[h * tq:(h + 1) * tq], jnp.where(head == h, v, jnp.zeros_like(v)),
                    preferred_element_type=F32)
        o = t if o is None else o + t
    if s0_ref is not None:
        o = (o + _bdot(q, s0_ref[0]) * jnp.exp(lgf_l * (qi + 1.0))
             + _bdot(q, s0_ref[1]) * jnp.exp(lgb_l * (seq_len - qi)))
    gm = _group_mean_matrix(C_HEADS * HEAD_DIM)
    dlt = o - _dot_hilo_lhs(o, gm)
    var = _dot_hilo_lhs(dlt * dlt, gm)
    out = (dlt * lax.rsqrt(var + EPS) * gn_ref[...] * jax.nn.silu(g)).astype(BF16)
    if not want_state:
        return out, None
    kpos = lax.broadcasted_iota(jnp.int32, (seq_len, 1), 0).astype(F32)
    sf = _bdot_tn(k * jnp.exp(lgf_l * (seq_len - 1.0 - kpos)), v)
    sb = _bdot_tn(k * jnp.exp(lgb_l * kpos), v)
    return out, (sf, sb)


def _store_retention_state(st_ref, slot, state):
    for s in range(st_ref.shape[0]):
        for d in range(2):
            for h in range(C_HEADS):
                sl = slice(h * HEAD_DIM, (h + 1) * HEAD_DIM)
                st_ref[s, d, h] = state[d][sl, sl] if s == slot else jnp.zeros((HEAD_DIM, HEAD_DIM), F32)


def _retention_kernel(q_ref, g_ref, k_ref, v_ref, dec_ref, gn_ref, s0_ref, o_ref, dec_scr, *, seq_len, tq):
    for b in range(q_ref.shape[0]):
        o_ref[b], _ = _retention_core(q_ref[b], k_ref[b], v_ref[b], g_ref[b], dec_ref, gn_ref, dec_scr,
                                      seq_len=seq_len, i0=pl.program_id(0) * tq,
                                      decay_fill="every_step" if b == 0 else "filled", s0_ref=s0_ref.at[b])


def _retention(z, cg, dec, gn, s0, layer, *, tq=LAT_TQ):
    nb, seq_len = z.shape[:2]
    return pl.pallas_call(
        functools.partial(_retention_kernel, seq_len=seq_len, tq=tq),
        grid=(seq_len // tq,),
        in_specs=[pl.BlockSpec((nb, tq, GROUP_WIDTH), lambda j: (0, j, OFF_CQ // GROUP_WIDTH)),
                  pl.BlockSpec((nb, tq, GROUP_WIDTH), lambda j: (0, j, 0)),
                  pl.BlockSpec((nb, seq_len, GROUP_WIDTH), lambda j: (0, 0, OFF_CK // GROUP_WIDTH)),
                  pl.BlockSpec((nb, seq_len, GROUP_WIDTH), lambda j: (0, 0, OFF_CV // GROUP_WIDTH)),
                  pl.BlockSpec((SUBLANES, LANES), lambda j: (0, 0)),
                  pl.BlockSpec((1, GROUP_WIDTH), lambda j: (0, 0)),
                  pl.BlockSpec((nb, None, 2, GROUP_WIDTH, GROUP_WIDTH), lambda j: (0, layer, 0, 0, 0))],
        out_specs=pl.BlockSpec((nb, tq, GROUP_WIDTH), lambda j: (0, j, 0)),
        out_shape=jax.ShapeDtypeStruct((nb, seq_len, GROUP_WIDTH), BF16),
        scratch_shapes=[pltpu.VMEM((C_HEADS * tq, seq_len), F32)],
        compiler_params=_cparams("arbitrary"),
        name="retention",
    )(z, cg, z, z, dec, gn, s0)


CTX_SEQS = 4


def _ctx_front_kernel(x_ref, mod_ref, g1_ref, w_ref, qn_ref, kn_ref, dec_ref, gn_ref, *rest, n_alias, slot):
    (oa_ref, ob_ref, oc_ref, du_ref, ak_ref, av_ref, bk_ref, bv_ref, st_ref, dec_scr) = rest[n_alias:]
    tq = x_ref.shape[0] // CTX_SEQS
    h = _rms_rows(x_ref[...]) * g1_ref[...] * (1.0 + mod_ref[1:2, :]) + mod_ref[0:1, :]
    zz = jnp.dot(h.astype(BF16), w_ref[...], preferred_element_type=F32)
    for s in range(CTX_SEQS):
        rows = slice(s * tq, (s + 1) * tq)
        z = zz[rows, :]
        aq = _head_norm(z[:, OFF_AQ:OFF_AK], qn_ref[...])
        ak = _head_norm(z[:, OFF_AK:OFF_AV], kn_ref[...])
        av, bq, bk, bv = (z[:, OFF_AV:OFF_BQ], z[:, OFF_BQ:OFF_BK], z[:, OFF_BK:OFF_BV], z[:, OFF_BV:OFF_CQ])
        oa_ref[rows, :] = _mha(_stack_heads_gqa(aq * Q_SCALE), [(ak, _spread_kv_gqa(av), None)], tq)
        ob_ref[rows, :] = _mha(_stack_heads(bq * Q_SCALE), [(bk, bv, None)], tq)
        oc_ref[rows, :], state = _retention_core(
            z[:, OFF_CQ:OFF_CK], z[:, OFF_CK:OFF_CV] * Q_SCALE, z[:, OFF_CV:OFF_CG], z[:, OFF_CG:OFF_DU],
            dec_ref, gn_ref, dec_scr, seq_len=tq, i0=0, decay_fill="first_step" if s == 0 else "filled",
            want_state=True)
        du_ref[:, s * GROUP_WIDTH:(s + 1) * GROUP_WIDTH] = z[:, OFF_DU:]
        _store_layer_slot(ak_ref.at[s], slot, ak)
        _store_layer_slot(av_ref.at[s], slot, av)
        _store_layer_slot(bk_ref.at[s], slot, bk)
        _store_layer_slot(bv_ref.at[s], slot, bv)
        _store_retention_state(st_ref.at[s], slot, state)


def _ctx_front(x, mods, g1, w_in_bf, qn, kn, dec, gn, layer, prev):
    assert SEQ == S5_SEG
    tm = CTX_SEQS * SEQ
    n = x.shape[0]
    nb = n // SEQ
    steps = n // tm
    per_blk = SUBLANES // CTX_SEQS
    const = lambda *shape: pl.BlockSpec(shape, lambda i: (0,) * len(shape))
    row = lambda w: pl.BlockSpec((tm, w), lambda i: (i, 0))
    in_specs = [row(D_MODEL), _mod_spec(layer, 0, steps), const(1, D_MODEL),
                pl.BlockSpec((None, D_MODEL, IN_WIDTH), lambda i: (layer, 0, 0)),
                const(1, GROUP_WIDTH), const(1, KV_WIDTH), const(SUBLANES, LANES), const(1, GROUP_WIDTH)]
    args = [x, mods, g1, w_in_bf, qn, kn, dec, gn]
    out_specs = [row(GROUP_WIDTH), row(GROUP_WIDTH), row(GROUP_WIDTH),
                 pl.BlockSpec((None, S5_SEG, CTX_SEQS * GROUP_WIDTH), lambda i: (i // per_blk, 0, i % per_blk))]
    out_shape = [jax.ShapeDtypeStruct((n, GROUP_WIDTH), BF16)] * 3 + [
        jax.ShapeDtypeStruct((nb // SUBLANES, S5_SEG, SUBLANES * GROUP_WIDTH), F32)]
    first = prev is None
    slot = 0
    for tail in ((SEQ, KV_WIDTH), (SEQ, KV_WIDTH), (SEQ, GROUP_WIDTH), (SEQ, GROUP_WIDTH),
                 (2, C_HEADS, HEAD_DIM, HEAD_DIM)):
        blk, idx, slot = _layer_slot_block(layer, first, tail)
        out_specs.append(pl.BlockSpec((CTX_SEQS,) + blk[1:], lambda i, idx=idx: (i,) + idx))
        out_shape.append(jax.ShapeDtypeStruct((nb, DEPTH) + tail, F32))
    aliases = {}
    if not first:
        for k, arr in enumerate(prev):
            aliases[len(args)] = 4 + k
            in_specs.append(pl.BlockSpec(memory_space=pl.ANY))
            args.append(arr)
    outs = pl.pallas_call(
        functools.partial(_ctx_front_kernel, n_alias=len(aliases), slot=slot),
        grid=(steps,),
        in_specs=in_specs,
        out_specs=out_specs,
        out_shape=out_shape,
        scratch_shapes=[pltpu.VMEM((C_HEADS * SEQ, SEQ), F32)],
        input_output_aliases=aliases,
        compiler_params=_cparams("arbitrary"),
        name="ctx_front",
    )(*args)
    return outs[0], outs[1], outs[2], outs[3], tuple(outs[4:])


def _s5_prep_kernel(lre_ref, lim_ref, ldt_ref, bre_ref, bim_ref, cre_ref, cim_ref,
                    a_ref, bm_ref, cro_ref, cio_ref, bm_scr, cr_scr, ci_scr):
    lre = lre_ref[...]
    lim = lim_ref[...]
    dt = jnp.exp(ldt_ref[...])
    mag = jnp.exp(lre * dt)
    a_re = mag * jnp.cos(lim * dt)
    a_im = mag * jnp.sin(lim * dt)
    den = lre * lre + lim * lim
    r_re = ((a_re - 1.0) * lre + a_im * lim) / den
    r_im = (a_im * lre - (a_re - 1.0) * lim) / den
    bm_scr[...] = jnp.zeros_like(bm_scr)
    cr_scr[...] = jnp.zeros_like(cr_scr)
    ci_scr[...] = jnp.zeros_like(ci_scr)
    for g in range(S5_GROUPS):
        rows = slice(g * S5_CH, (g + 1) * S5_CH)
        cols = slice(g * S5_STATE, (g + 1) * S5_STATE)
        a_ref[0:1, cols] = a_re[g:g + 1, :]
        a_ref[1:2, cols] = a_im[g:g + 1, :]
        rr, ri = r_re[g:g + 1, :], r_im[g:g + 1, :]
        br, bi = bre_ref[g], bim_ref[g]
        bm_scr[rows, cols] = rr * br - ri * bi
        bm_scr[rows, S5_SP + g * S5_STATE:S5_SP + (g + 1) * S5_STATE] = rr * bi + ri * br
        cr_scr[cols, rows] = cre_ref[g]
        ci_scr[cols, rows] = cim_ref[g]
    bm_ref[...] = bm_scr[...].astype(BF16)
    cro_ref[...] = cr_scr[...].astype(BF16)
    cio_ref[...] = ci_scr[...].astype(BF16)


def _s5_prepare(lam_re, lam_im, log_dt, b_re, b_im, c_re, c_im):
    gp = (S5_GROUPS, S5_STATE)
    ldt = jnp.broadcast_to(log_dt[..., None], (DEPTH, 2) + gp)
    bt = [jnp.swapaxes(t, -1, -2) for t in (b_re, b_im)]
    ct = [jnp.swapaxes(t, -1, -2) for t in (c_re, c_im)]

    def spec(*tail):
        return pl.BlockSpec((None, None) + tail, lambda l, d: (l, d) + (0,) * len(tail))

    return pl.pallas_call(
        _s5_prep_kernel,
        grid=(DEPTH, 2),
        in_specs=[spec(*gp)] * 3 + [spec(S5_GROUPS, S5_CH, S5_STATE)] * 2 + [spec(S5_GROUPS, S5_STATE, S5_CH)] * 2,
        out_specs=[spec(2, S5_SP), spec(GROUP_WIDTH, 2 * S5_SP), spec(S5_SP, GROUP_WIDTH), spec(S5_SP, GROUP_WIDTH)],
        out_shape=[jax.ShapeDtypeStruct((DEPTH, 2, 2, S5_SP), F32),
                   jax.ShapeDtypeStruct((DEPTH, 2, GROUP_WIDTH, 2 * S5_SP), BF16),
                   jax.ShapeDtypeStruct((DEPTH, 2, S5_SP, GROUP_WIDTH), BF16),
                   jax.ShapeDtypeStruct((DEPTH, 2, S5_SP, GROUP_WIDTH), BF16)],
        scratch_shapes=[pltpu.VMEM((GROUP_WIDTH, 2 * S5_SP), F32), pltpu.VMEM((S5_SP, GROUP_WIDTH), F32),
                        pltpu.VMEM((S5_SP, GROUP_WIDTH), F32)],
        compiler_params=_cparams("parallel", "parallel"),
        name="s5_prepare",
    )(lam_re, lam_im, ldt, bt[0], bt[1], ct[0], ct[1])


def _cmul(ar, ai, br, bi):
    return ar * br - ai * bi, ar * bi + ai * br


def _s5_kernel(u_ref, h0_ref, a_ref, bm_ref, cre_ref, cim_ref, dvec_ref, glu_ref, *rest, nseg, slot):
    od_ref, fin_ref, x_scr, s_scr, y_scr = rest[-5:]
    steps = S5_SEG
    rows = steps * SUBLANES
    chunk = S5_CHUNK
    chunk_steps = chunk // SUBLANES
    nchunk = rows // chunk
    seg = lax.broadcasted_iota(jnp.int32, (SUBLANES, S5_SP), 0) % nseg

    for d in range(2):
        ar = jnp.broadcast_to(a_ref[d, 0:1, :], (SUBLANES, S5_SP))
        ai = jnp.broadcast_to(a_ref[d, 1:2, :], (SUBLANES, S5_SP))

        def row0(k):
            c = k if d == 0 else nchunk - 1 - k
            return c * chunk if isinstance(c, int) else pl.multiple_of(c * chunk, chunk)

        def input_part(k, buf):
            x_scr[buf] = jnp.dot(u_ref[pl.ds(row0(k), chunk), :].astype(BF16), bm_ref[d],
                                 preferred_element_type=F32)

        def scan_part(buf, carry, store):
            sr, si = carry
            for t in range(chunk_steps):
                r = (t if d == 0 else chunk_steps - 1 - t) * SUBLANES
                pr, pi = _cmul(ar, ai, sr, si)
                sr = pr + x_scr[buf, r:r + SUBLANES, 0:S5_SP]
                si = pi + x_scr[buf, r:r + SUBLANES, S5_SP:]
                if store:
                    s_scr[buf, r:r + SUBLANES, 0:S5_SP] = sr
                    s_scr[buf, r:r + SUBLANES, S5_SP:] = si
            return sr, si

        def output_part(k, buf):
            y = _bdot(s_scr[buf, :, 0:S5_SP], cre_ref[d]) - _bdot(s_scr[buf, :, S5_SP:], cim_ref[d])
            rows_k = pl.ds(row0(k), chunk)
            if d == 0:
                y_scr[rows_k, :] = y
            else:
                zz = jax.nn.gelu(y_scr[rows_k, :] + y + dvec_ref[...] * u_ref[rows_k, :])
                od_ref[rows_k, :] = (zz * jax.nn.sigmoid(_bdot(zz, glu_ref[...]))).astype(BF16)

        def half(k, buf, carry, store, nxt=True, prev=True):
            if nxt:
                input_part(k + 1, 1 - buf)
            carry = scan_part(buf, carry, store)
            if store and prev:
                output_part(k - 1, 1 - buf)
            return carry

        def run_pass(carry, store):
            input_part(0, 0)
            carry = half(0, 0, carry, store, prev=False)
            carry = half(1, 1, carry, store)

            def pair(j, c):
                c = half(2 * j, 0, c, store)
                return half(2 * j + 1, 1, c, store)
            carry = lax.fori_loop(1, nchunk // 2 - 1, pair, carry)
            carry = half(nchunk - 2, 0, carry, store)
            carry = half(nchunk - 1, 1, carry, store, nxt=False)
            if store:
                output_part(nchunk - 1, 1)
            return carry

        init = (h0_ref[d, :, 0:S5_SP], h0_ref[d, :, S5_SP:])
        if nseg > 1:
            zero = jnp.zeros((SUBLANES, S5_SP), F32)
            fr, fi = run_pass((zero, zero), store=False)
            pr, pi = ar, ai
            for _ in range(int(math.log2(steps))):
                pr, pi = _cmul(pr, pi, pr, pi)
            cr, ci = init
            shift = 1 if d == 0 else SUBLANES - 1
            order = range(1, nseg) if d == 0 else range(nseg - 2, -1, -1)
            for s in order:
                ncr, nci = pltpu.roll(cr, shift, 0), pltpu.roll(ci, shift, 0)
                nfr, nfi = pltpu.roll(fr, shift, 0), pltpu.roll(fi, shift, 0)
                qr, qi = _cmul(pr, pi, ncr, nci)
                cr = jnp.where(seg == s, qr + nfr, cr)
                ci = jnp.where(seg == s, qi + nfi, ci)
            init = (cr, ci)
        sr, si = run_pass(init, store=True)
        for s in range(fin_ref.shape[1] // (4 * S5_SP)):
            base = (4 * s + 2 * d) * S5_SP
            fin_ref[:, base:base + S5_SP] = sr if s == slot else jnp.zeros_like(sr)
            fin_ref[:, base + S5_SP:base + 2 * S5_SP] = si if s == slot else jnp.zeros_like(si)


def _s5(du_tm, h0, a, bmat, cre, cim, dvec, glu_bf, layer, *, nseg, fin_layer=0, fin_layers=1,
        prev_fin=None):
    nblk = du_tm.shape[0]
    rows = S5_SEG * SUBLANES
    fin_w = 4 * S5_SP
    in_specs = [pl.BlockSpec((None, rows, GROUP_WIDTH), lambda i: (i, 0, 0)),
                pl.BlockSpec((2, SUBLANES, 2 * S5_SP), lambda i: (0, 0, 0)),
                pl.BlockSpec((None, 2, 2, S5_SP), lambda i: (layer, 0, 0, 0)),
                pl.BlockSpec((None, 2, GROUP_WIDTH, 2 * S5_SP), lambda i: (layer, 0, 0, 0)),
                pl.BlockSpec((None, 2, S5_SP, GROUP_WIDTH), lambda i: (layer, 0, 0, 0)),
                pl.BlockSpec((None, 2, S5_SP, GROUP_WIDTH), lambda i: (layer, 0, 0, 0)),
                pl.BlockSpec((1, GROUP_WIDTH), lambda i: (0, 0)),
                pl.BlockSpec((None, GROUP_WIDTH, GROUP_WIDTH), lambda i: (layer, 0, 0))]
    args = [du_tm.reshape(nblk, rows, GROUP_WIDTH), h0, a, bmat, cre, cim, dvec, glu_bf]
    aliases = {}
    if prev_fin is not None:
        aliases[len(args)] = 1
        in_specs.append(pl.BlockSpec(memory_space=pl.ANY))
        args.append(prev_fin)
        fin_spec, slot = pl.BlockSpec((SUBLANES, fin_w), lambda i: (i, fin_layer)), 0
    else:
        fin_spec, slot = pl.BlockSpec((SUBLANES, fin_layers * fin_w), lambda i: (i, 0)), fin_layer
    od, fin = pl.pallas_call(
        functools.partial(_s5_kernel, nseg=nseg, slot=slot),
        grid=(nblk,),
        in_specs=in_specs,
        out_specs=[pl.BlockSpec((None, rows, GROUP_WIDTH), lambda i: (i, 0, 0)), fin_spec],
        out_shape=[jax.ShapeDtypeStruct((nblk, rows, GROUP_WIDTH), BF16),
                   jax.ShapeDtypeStruct((nblk * SUBLANES, fin_layers * fin_w), F32)],
        scratch_shapes=[pltpu.VMEM((2, S5_CHUNK, 2 * S5_SP), F32), pltpu.VMEM((2, S5_CHUNK, 2 * S5_SP), F32),
                        pltpu.VMEM((rows, GROUP_WIDTH), F32)],
        input_output_aliases=aliases,
        compiler_params=_cparams("parallel"),
        name="s5",
    )(*args)
    return od.reshape(nblk, S5_SEG, SUBLANES * GROUP_WIDTH), fin


ROUTE_GROUP = MOE_PER_GROUP
OUT_SEQS = 2
MOE_TILE = 512


def _out_kernel(x_ref, oa_ref, ob_ref, oc_ref, od_ref, mod_ref, wo_ref, g2_ref, wrh_ref, wrl_ref, br_ref,
                xm_ref, h2_ref, route_ref, cnt_ref):
    od = jnp.concatenate([od_ref[:, s * GROUP_WIDTH:(s + 1) * GROUP_WIDTH] for s in range(OUT_SEQS)], axis=0)
    mix = functools.reduce(jnp.add, [
        _bdot(o, wo_ref[i * GROUP_WIDTH:(i + 1) * GROUP_WIDTH, :])
        for i, o in enumerate((oa_ref[...], ob_ref[...], oc_ref[...], od))])
    xm = x_ref[...] + mod_ref[2:3, :] * mix
    xm_ref[...] = xm
    h2 = _rms_rows(xm) * g2_ref[...] * (1.0 + mod_ref[4:5, :]) + mod_ref[3:4, :]
    h2_ref[...] = h2.astype(BF16)

    h_hi, h_lo = _split(h2)
    logits = (jnp.dot(h_hi, wrh_ref[...], preferred_element_type=F32)
              + jnp.dot(h_hi, wrl_ref[...], preferred_element_type=F32)
              + jnp.dot(h_lo, wrh_ref[...], preferred_element_type=F32)) + br_ref[...]
    lane_i = lax.broadcasted_iota(jnp.int32, logits.shape, 1)
    lane = lane_i.astype(F32)
    big = jnp.float32(2 ** 30)
    gmask = lane_i < MOE_GROUPS
    gl = jnp.where(gmask, logits, -jnp.inf)
    gmax = jnp.max(gl, axis=-1, keepdims=True)
    p_top = 1.0 / jnp.sum(jnp.exp(gl - gmax), axis=-1, keepdims=True)
    g_top = jnp.min(jnp.where(gl == gmax, lane, big), axis=-1, keepdims=True)
    e_lane = lane_i - ROUTER_OFF
    lane_group = (e_lane // MOE_PER_GROUP).astype(F32)
    emask = (e_lane >= 0) & (e_lane < MOE_EXPERTS) & (lane_group == g_top)
    el = jnp.where(emask, logits, -jnp.inf)
    m1 = jnp.max(el, axis=-1, keepdims=True)
    i1 = jnp.min(jnp.where(el == m1, lane, big), axis=-1, keepdims=True)
    el2 = jnp.where(lane == i1, -jnp.inf, el)
    m2 = jnp.max(el2, axis=-1, keepdims=True)
    i2 = jnp.min(jnp.where(el2 == m2, lane, big), axis=-1, keepdims=True)
    e2 = jnp.exp(m2 - m1)
    den = 1.0 + e2
    gates = (jnp.where(lane == i1, (1.0 / den) * p_top, 0.0)
             + jnp.where(lane == i2, (e2 / den) * p_top, 0.0))
    route = jnp.where(lane == ROUTE_GROUP + g_top, 1.0, 0.0)
    for g in range(MOE_GROUPS):
        local = pltpu.roll(gates, LANES - ROUTER_OFF - g * MOE_PER_GROUP, 1)
        route = route + jnp.where((g_top == g) & (lane_i < MOE_PER_GROUP), local, 0.0)
    route_ref[...] = route
    for t in range(cnt_ref.shape[0]):
        part = jnp.sum(route[t * MOE_TILE:(t + 1) * MOE_TILE], axis=0, keepdims=True)
        cnt_ref[t] = jnp.broadcast_to(part, (SUBLANES, LANES)).astype(jnp.int32)


def _output_stage(x, mixes, mods, mod_row, mod_tokens, wo_bf, g2, wr_hi, wr_lo, br, layer):
    tm = OUT_SEQS * S5_SEG
    n = x.shape[0]
    row = lambda w: pl.BlockSpec((tm, w), lambda i: (i, 0))
    const = lambda shape: pl.BlockSpec(shape, lambda i: (0,) * len(shape))
    per_blk = SUBLANES // OUT_SEQS
    return pl.pallas_call(
        _out_kernel,
        grid=(n // tm,),
        in_specs=[row(D_MODEL), row(GROUP_WIDTH), row(GROUP_WIDTH), row(GROUP_WIDTH),
                  pl.BlockSpec((None, S5_SEG, OUT_SEQS * GROUP_WIDTH), lambda i: (i // per_blk, 0, i % per_blk)),
                  _mod_spec(layer, mod_row, mod_tokens // tm),
                  pl.BlockSpec((None, D_MODEL, D_MODEL), lambda i: (layer, 0, 0)), const((1, D_MODEL)),
                  const((D_MODEL, LANES)), const((D_MODEL, LANES)), const((1, LANES))],
        out_specs=[row(D_MODEL), row(D_MODEL), row(LANES),
                   pl.BlockSpec((tm // MOE_TILE, SUBLANES, LANES), lambda i: (i, 0, 0))],
        out_shape=[jax.ShapeDtypeStruct((n, D_MODEL), F32),
                   jax.ShapeDtypeStruct((n, D_MODEL), BF16),
                   jax.ShapeDtypeStruct((n, LANES), F32),
                   jax.ShapeDtypeStruct((n // MOE_TILE, SUBLANES, LANES), jnp.int32)],
        compiler_params=_cparams("parallel"),
        name="output_stage",
    )(x, *mixes, mods, wo_bf, g2, wr_hi, wr_lo, br)


GROUP_HID = MOE_PER_GROUP * MOE_HIDDEN


MOE_CHUNK = 160


def _moe_kernel(cnt_ref, h2_ref, route_ref, xm_ref, mod_ref, w1_ref, w3_ref, w2_ref, fg_ref, o_ref,
                hs_scr, rs_scr, os_scr, before_scr, *, final, tm):
    i = pl.program_id(0)
    off1 = cnt_ref[i, 0]
    off2 = off1 + cnt_ref[i, 1]
    off3 = off2 + cnt_ref[i, 2]
    starts = (jnp.int32(0), off1, off2, off3)
    ends = (off1, off2, off3, jnp.int32(tm))

    route = route_ref[...]
    r_hi, r_lo = _split(route)
    pick = (lax.broadcasted_iota(jnp.int32, (SUBLANES, LANES), 1)
            == ROUTE_GROUP + lax.broadcasted_iota(jnp.int32, (SUBLANES, LANES), 0))
    gt = lax.dot_general(jnp.where(pick, 1.0, 0.0).astype(BF16), r_hi, (((1,), (1,)), ((), ())),
                         preferred_element_type=F32)
    @pl.when(i == 0)
    def _():
        before_scr[...] = jnp.where(lax.broadcasted_iota(jnp.int32, (tm, tm), 0)
                                    < lax.broadcasted_iota(jnp.int32, (tm, tm), 1), 1.0, 0.0).astype(BF16)

    rank = jnp.dot(gt.astype(BF16), before_scr[...], preferred_element_type=F32)
    gt_i = gt.astype(jnp.int32)
    rank_i = rank.astype(jnp.int32)
    pos = jnp.zeros((1, tm), jnp.int32)
    for g in range(MOE_GROUPS):
        pos = pos + gt_i[g:g + 1, :] * (rank_i[g:g + 1, :] + starts[g])
    perm = jnp.where(lax.broadcasted_iota(jnp.int32, (tm, tm), 0) == pos, 1.0, 0.0).astype(BF16)
    hs_scr[...] = jnp.dot(perm, h2_ref[...], preferred_element_type=F32).astype(BF16)
    rs_scr[...] = (jnp.dot(perm, r_hi, preferred_element_type=F32)
                   + jnp.dot(perm, r_lo, preferred_element_type=F32))

    os_scr[...] = jnp.zeros_like(os_scr)

    def evaluate(g, lo, hi, chunk):
        base = (lo // BF16_ROWS) * BF16_ROWS
        r0 = pl.multiple_of(jnp.minimum(base + chunk * MOE_CHUNK, tm - MOE_CHUNK), BF16_ROWS)
        rows = pl.ds(r0, MOE_CHUNK)
        x = hs_scr[rows, :]
        gates = rs_scr[rows, :]
        a = jnp.dot(x, w1_ref[g], preferred_element_type=F32)
        b = jnp.dot(x, w3_ref[g], preferred_element_type=F32)
        hid = []
        for e in range(MOE_PER_GROUP):
            sl = slice(e * MOE_HIDDEN, (e + 1) * MOE_HIDDEN)
            hid.append((jax.nn.silu(a[:, sl]) * b[:, sl] * gates[:, e:e + 1]).astype(BF16))
        y = jnp.dot(jnp.concatenate(hid, axis=1), w2_ref[g], preferred_element_type=F32)
        rowid = r0 + lax.broadcasted_iota(jnp.int32, (MOE_CHUNK, 1), 0)
        member = (rowid >= lo) & (rowid < hi)
        os_scr[rows, :] = jnp.where(member, y, os_scr[rows, :])

    extra = []
    for g in range(MOE_GROUPS):
        lo, hi = starts[g], ends[g]
        base = (lo // BF16_ROWS) * BF16_ROWS
        n_chunks = jnp.where(hi > lo, (hi - base + MOE_CHUNK - 1) // MOE_CHUNK, 0)
        evaluate(g, lo, hi, 0)
        extra.append(jnp.maximum(n_chunks - 1, 0))
    first = (jnp.int32(0), extra[0], extra[0] + extra[1], extra[0] + extra[1] + extra[2])

    def pick_by_group(g, vals):
        return jnp.where(g == 0, vals[0], jnp.where(g == 1, vals[1], jnp.where(g == 2, vals[2], vals[3])))

    def extra_body(idx, carry):
        g = sum((idx >= f).astype(jnp.int32) for f in first[1:])
        evaluate(g, pick_by_group(g, starts), pick_by_group(g, ends), idx - pick_by_group(g, first) + 1)
        return carry

    lax.fori_loop(0, first[3] + extra[3], extra_body, 0)

    o_hi, o_lo = _split(os_scr[...])
    moe = (lax.dot_general(perm, o_hi, (((0,), (0,)), ((), ())), preferred_element_type=F32)
           + lax.dot_general(perm, o_lo, (((0,), (0,)), ((), ())), preferred_element_type=F32))
    out = xm_ref[...] + mod_ref[5:6, :] * moe
    if final:
        out = _rms_rows(out) * fg_ref[...]
    o_ref[...] = out


def _moe_weight_kernel(w1_ref, w3_ref, w2_ref, o1_ref, o3_ref, o2_ref):
    for e in range(MOE_PER_GROUP):
        sl = slice(e * MOE_HIDDEN, (e + 1) * MOE_HIDDEN)
        o1_ref[:, sl] = w1_ref[e].astype(BF16)
        o3_ref[:, sl] = w3_ref[e].astype(BF16)
        o2_ref[sl, :] = w2_ref[e].astype(BF16)


def _moe_weights(w1, w3, w2):
    up = pl.BlockSpec((None, MOE_PER_GROUP, D_MODEL, MOE_HIDDEN), lambda l, g: (l, g, 0, 0))
    down = pl.BlockSpec((None, MOE_PER_GROUP, MOE_HIDDEN, D_MODEL), lambda l, g: (l, g, 0, 0))
    out = pl.BlockSpec((None, None, D_MODEL, GROUP_HID), lambda l, g: (l, g, 0, 0))
    shape = jax.ShapeDtypeStruct((DEPTH, MOE_GROUPS, D_MODEL, GROUP_HID), BF16)
    return pl.pallas_call(
        _moe_weight_kernel,
        grid=(DEPTH, MOE_GROUPS),
        in_specs=[up, up, down],
        out_specs=[out, out, out],
        out_shape=[shape, shape, shape],
        compiler_params=_cparams("parallel", "parallel"),
        name="moe_weights",
    )(w1, w3, w2)


def _moe(h2, route, tile_counts, xm, mods, mod_row, mod_tokens, w1g, w3g, w2g, fg, layer, *, final):
    tm = MOE_TILE
    n = h2.shape[0]
    cnt = tile_counts[:, 0, ROUTE_GROUP:ROUTE_GROUP + MOE_GROUPS]
    row = lambda w: pl.BlockSpec((tm, w), lambda i, c: (i, 0))
    mod_tiles = mod_tokens // tm
    wspec = pl.BlockSpec((None, MOE_GROUPS, D_MODEL, GROUP_HID), lambda i, c: (layer, 0, 0, 0),
                         pipeline_mode=pl.Buffered(1))
    return pl.pallas_call(
        functools.partial(_moe_kernel, final=final, tm=tm),
        grid_spec=pltpu.PrefetchScalarGridSpec(
            num_scalar_prefetch=1,
            grid=(n // tm,),
            in_specs=[row(D_MODEL), row(LANES), row(D_MODEL),
                      pl.BlockSpec((None, None, N_MOD, D_MODEL),
                                   lambda i, c: (layer, mod_row + i // mod_tiles, 0, 0)),
                      wspec, wspec, wspec,
                      pl.BlockSpec((1, D_MODEL), lambda i, c: (0, 0))],
            out_specs=row(D_MODEL),
            scratch_shapes=[pltpu.VMEM((tm, D_MODEL), BF16), pltpu.VMEM((tm, LANES), F32),
                            pltpu.VMEM((tm, D_MODEL), F32), pltpu.VMEM((tm, tm), BF16)]),
        out_shape=jax.ShapeDtypeStruct((n, D_MODEL), F32),
        compiler_params=_cparams("arbitrary"),
        name="moe",
    )(cnt, h2, route, xm, mods, w1g, w3g, w2g, fg)


def kernel(x_prompt, x_sample, cache_a_k, cache_a_v, cache_b_k, cache_b_v, state_ret, state_ssm, c, c_ctx, mod_w, mod_b, norm1_g, norm2_g, w_in, a_qn_g, a_kn_g, b_rel_bias, ret_decay, ret_gn_g, s5_lam_re, s5_lam_im, s5_log_dt, s5_b_re, s5_b_im, s5_c_re, s5_c_im, s5_d, s5_glu_w, w_out, moe_gw, moe_gb, moe_ew, moe_eb, moe_w1, moe_w3, moe_w2, final_norm_g):
    n_ctx = BATCH * SEQ
    n_lat = DEC_BATCH * DEC_SEQ
    lat_seg = DEC_SEQ // S5_SEG

    cond = jnp.zeros((SUBLANES, D_MODEL), F32).at[0].set(c_ctx).at[1:1 + DEC_BATCH].set(c)
    mods = _modulation(cond, mod_w, mod_b).reshape(DEPTH, SUBLANES, N_MOD, D_MODEL)

    rope_tabs = _rope_tables()
    s5_a, s5_bm, s5_cre, s5_cim = _s5_prepare(s5_lam_re, s5_lam_im, s5_log_dt, s5_b_re, s5_b_im,
                                              s5_c_re, s5_c_im)
    cak = cache_a_k.reshape(DEC_BATCH, DEPTH, PAST_LEN, A_KV_HEADS * HEAD_DIM)
    cav = cache_a_v.reshape(DEC_BATCH, DEPTH, PAST_LEN, A_KV_HEADS * HEAD_DIM)
    cbk = cache_b_k.reshape(DEC_BATCH, DEPTH, PAST_LEN, B_HEADS * HEAD_DIM)
    cbv = cache_b_v.reshape(DEC_BATCH, DEPTH, PAST_LEN, B_HEADS * HEAD_DIM)

    xc = x_prompt.reshape(n_ctx, D_MODEL)
    xs = x_sample.reshape(n_lat, D_MODEL)
    w1_all, w3_all, w2_all = _moe_weights(moe_w1, moe_w3, moe_w2)
    eye_h = jnp.eye(C_HEADS, dtype=F32)
    s0_bd = (state_ret[:, :, :, :, :, None, :] * eye_h[None, None, None, :, None, :, None]).reshape(
        DEC_BATCH, DEPTH, 2, C_HEADS * HEAD_DIM, C_HEADS * HEAD_DIM)
    ctx_state = ssm_states = None
    h0_zero = jnp.zeros((2, SUBLANES, 2 * S5_SP), F32)
    w_in_bf = w_in.astype(BF16)
    wo_bf = w_out.astype(BF16)
    glu_bf = s5_glu_w.astype(BF16)
    for l in range(DEPTH):
        final = l == DEPTH - 1
        g1 = norm1_g[l].reshape(1, D_MODEL)
        g2 = norm2_g[l].reshape(1, D_MODEL)
        fg = final_norm_g.reshape(1, D_MODEL)
        qn = jnp.tile(a_qn_g[l], A_HEADS).reshape(1, GROUP_WIDTH)
        kn = jnp.tile(a_kn_g[l], A_KV_HEADS).reshape(1, KV_WIDTH)
        dec = jnp.broadcast_to(ret_decay[l].reshape(2 * C_HEADS, 1), (2 * C_HEADS, LANES))
        gn = ret_gn_g[l].reshape(1, GROUP_WIDTH)
        dvec = s5_d[l].reshape(1, GROUP_WIDTH)
        wr = jnp.zeros((D_MODEL, LANES), F32).at[:, :MOE_GROUPS].set(moe_gw[l]).at[
            :, ROUTER_OFF:ROUTER_OFF + MOE_EXPERTS].set(moe_ew[l])
        br = jnp.zeros((1, LANES), F32).at[0, :MOE_GROUPS].set(moe_gb[l]).at[
            0, ROUTER_OFF:ROUTER_OFF + MOE_EXPERTS].set(moe_eb[l])
        wr_hi = wr.astype(BF16)
        wr_lo = (wr - wr_hi.astype(F32)).astype(BF16)
        na_bias = _na_bias(b_rel_bias[l])

        oa, ob, oc, du_tm, ctx_state = _ctx_front(xc, mods, g1, w_in_bf, qn, kn, dec, gn, l, ctx_state)
        od_tm, ssm_states = _s5(du_tm, h0_zero, s5_a, s5_bm, s5_cre, s5_cim, dvec, glu_bf, l,
                                nseg=1, fin_layer=l, fin_layers=DEPTH, prev_fin=ssm_states)
        xm, h2, route, counts = _output_stage(xc, (oa, ob, oc, od_tm), mods, 0, n_ctx, wo_bf, g2,
                                              wr_hi, wr_lo, br, l)
        xc = _moe(h2, route, counts, xm, mods, 0, n_ctx, w1_all, w3_all, w2_all, fg, l, final=final)

        zs, cg, du_tm = _project(xs, mods, 1, DEC_SEQ, g1, w_in_bf, qn, kn, rope_tabs, l, seq_len=DEC_SEQ)
        zs3 = zs.reshape(DEC_BATCH, DEC_SEQ, OFF_CG)
        oa = _lat_attention_a(zs3, cak, cav, l).reshape(n_lat, GROUP_WIDTH)
        ob = _lat_attention_b(zs3, cbk, cbv, na_bias, l).reshape(n_lat, GROUP_WIDTH)
        oc = _retention(zs3, cg.reshape(DEC_BATCH, DEC_SEQ, GROUP_WIDTH), dec, gn, s0_bd, l).reshape(
            n_lat, GROUP_WIDTH)
        h0 = state_ssm[:, l].reshape(DEC_BATCH, 2, 2 * S5_SP).transpose(1, 0, 2)
        h0_seg = jnp.zeros((2, DEC_BATCH, lat_seg, 2 * S5_SP), F32)
        h0_seg = h0_seg.at[0, :, 0].set(h0[0]).at[1, :, lat_seg - 1].set(h0[1])
        od_tm, _ = _s5(du_tm, h0_seg.reshape(2, SUBLANES, 2 * S5_SP),
                       s5_a, s5_bm, s5_cre, s5_cim, dvec, glu_bf, l, nseg=lat_seg)
        xm, h2, route, counts = _output_stage(xs, (oa, ob, oc, od_tm), mods, 1, DEC_SEQ, wo_bf, g2,
                                              wr_hi, wr_lo, br, l)
        xs = _moe(h2, route, counts, xm, mods, 1, DEC_SEQ, w1_all, w3_all, w2_all, fg, l, final=final)

    new_ak, new_av, new_bk, new_bv, ret_states = ctx_state
    return (xc.reshape(BATCH, SEQ, D_MODEL), xs.reshape(DEC_BATCH, DEC_SEQ, D_MODEL),
            new_ak.reshape(BATCH, DEPTH, SEQ, A_KV_HEADS, HEAD_DIM),
            new_av.reshape(BATCH, DEPTH, SEQ, A_KV_HEADS, HEAD_DIM),
            new_bk.reshape(BATCH, DEPTH, SEQ, B_HEADS, HEAD_DIM),
            new_bv.reshape(BATCH, DEPTH, SEQ, B_HEADS, HEAD_DIM),
            ret_states,
            ssm_states.reshape(BATCH, DEPTH, 2, 2, S5_GROUPS, S5_STATE))
```

```python
import functools
import math

import numpy as np
import jax
import jax.numpy as jnp
from jax import lax
from jax.experimental import pallas as pl
from jax.experimental.pallas import tpu as pltpu

F32 = jnp.float32
BF16 = jnp.bfloat16

D_MODEL = 1024
BATCH = 32
SEQ = 256
DEPTH = 2
DEC_BATCH = 2
DEC_SEQ = 1024
PAST_LEN = 256
GRID_W = 64
HEAD_DIM = 64
GROUP_WIDTH = 256
A_HEADS = 4
A_KV_HEADS = 2
B_HEADS = 4
NA_ROWS = 8
NA_COLS = 16
C_HEADS = 4
S5_CH = 16
S5_GROUPS = 16
S5_STATE = 64
MOE_GROUPS = 4
MOE_PER_GROUP = 8
MOE_EXPERTS = 32
MOE_HIDDEN = 128
ROPE_THETA = 10000.0
EPS = 1e-6
IN_WIDTH = 2560
Q_SCALE = HEAD_DIM ** -0.5
KV_WIDTH = A_KV_HEADS * HEAD_DIM
N_MOD = 6
ROPE_PAIR = HEAD_DIM // 4
LAT_TQ = 256
MOD_TILE = 1536

OFF_AQ, OFF_AK, OFF_AV = 0, 256, 384
OFF_BQ, OFF_BK, OFF_BV = 512, 768, 1024
OFF_CQ, OFF_CK, OFF_CV, OFF_CG = 1280, 1536, 1792, 2048
OFF_DU = 2304

LANES = 128
SUBLANES = 8
BF16_ROWS = 16
S5_SP = S5_GROUPS * S5_STATE
S5_SEG = 256
S5_CHUNK = 256
ROUTER_OFF = 4
NEG_BIG = -1e30
VMEM_LIMIT = 56 * 1024 * 1024


def _cparams(*sem):
    return pltpu.CompilerParams(dimension_semantics=sem, vmem_limit_bytes=VMEM_LIMIT)


def _mod_spec(layer, first_row, tiles_per_row):
    return pl.BlockSpec((None, None, N_MOD, D_MODEL), lambda i: (layer, first_row + i // tiles_per_row, 0, 0))


def _bdot(a, b):
    return jnp.dot(a.astype(BF16), b.astype(BF16), preferred_element_type=F32)


def _bdot_nt(a, b):
    return lax.dot_general(a.astype(BF16), b.astype(BF16), (((1,), (1,)), ((), ())),
                           preferred_element_type=F32)


def _bdot_tn(a, b):
    return lax.dot_general(a.astype(BF16), b.astype(BF16), (((0,), (0,)), ((), ())),
                           preferred_element_type=F32)


def _split(a):
    hi = a.astype(BF16)
    lo = (a - hi.astype(F32)).astype(BF16)
    return hi, lo


def _dot_hilo_lhs(a, b_bf16):
    hi, lo = _split(a)
    return (jnp.dot(hi, b_bf16, preferred_element_type=F32)
            + jnp.dot(lo, b_bf16, preferred_element_type=F32))


def _rms_rows(x):
    return x * lax.rsqrt(jnp.mean(x * x, axis=-1, keepdims=True) + EPS)


def _mod_kernel(cond_ref, w_ref, b_ref, o_ref):
    o_ref[...] = _bdot(jax.nn.silu(cond_ref[...]), w_ref[...]) + b_ref[...]


def _modulation(cond, mod_w, mod_b):
    tn = MOD_TILE
    width = N_MOD * D_MODEL
    return pl.pallas_call(
        _mod_kernel,
        grid=(DEPTH, width // tn),
        in_specs=[pl.BlockSpec((SUBLANES, D_MODEL), lambda l, j: (0, 0)),
                  pl.BlockSpec((None, D_MODEL, tn), lambda l, j: (l, 0, j)),
                  pl.BlockSpec((None, 1, tn), lambda l, j: (l, 0, j))],
        out_specs=pl.BlockSpec((None, SUBLANES, tn), lambda l, j: (l, 0, j)),
        out_shape=jax.ShapeDtypeStruct((DEPTH, SUBLANES, width), F32),
        compiler_params=_cparams("arbitrary", "arbitrary"),
        name="modulation",
    )(cond, mod_w, mod_b.reshape(DEPTH, 1, width))


def _group_mean_matrix(w):
    ri = lax.broadcasted_iota(jnp.int32, (w, w), 0) // HEAD_DIM
    ci = lax.broadcasted_iota(jnp.int32, (w, w), 1) // HEAD_DIM
    return jnp.where(ri == ci, 1.0 / HEAD_DIM, 0.0).astype(BF16)


def _head_norm(t, g):
    ms = _dot_hilo_lhs(t * t, _group_mean_matrix(t.shape[1]))
    return t * lax.rsqrt(ms + EPS) * g


def _rope(t, cos, sa, sb):
    return (t * cos + pltpu.roll(t, LANES - ROPE_PAIR, 1) * sa + pltpu.roll(t, ROPE_PAIR, 1) * sb)


def _store_layer_slot(ref, slot, value):
    for s in range(ref.shape[0]):
        ref[s] = value if s == slot else jnp.zeros_like(value)


def _layer_slot_block(layer, first_call, tail):
    if first_call:
        return (None, DEPTH) + tail, (0,) * (1 + len(tail)), layer
    return (None, 1) + tail, (layer,) + (0,) * len(tail), 0


def _proj_kernel(x_ref, mod_ref, g1_ref, w_ref, qn_ref, kn_ref, cos_ref, sa_ref, sb_ref, z_ref, cg_ref, du_ref):
    h = _rms_rows(x_ref[...]) * g1_ref[...] * (1.0 + mod_ref[1:2, :]) + mod_ref[0:1, :]
    z = jnp.dot(h.astype(BF16), w_ref[...], preferred_element_type=F32)
    aq = _head_norm(z[:, OFF_AQ:OFF_AK], qn_ref[...])
    ak = _head_norm(z[:, OFF_AK:OFF_AV], kn_ref[...])
    for j in range(3):
        t = aq[:, j * LANES:(j + 1) * LANES] if j < 2 else ak
        sl = slice(0, LANES) if j == 2 else slice(j * LANES, (j + 1) * LANES)
        t = _rope(t, cos_ref[:, sl], sa_ref[:, sl], sb_ref[:, sl])
        z_ref[:, j * LANES:(j + 1) * LANES] = t.astype(BF16)
    z_ref[:, OFF_AV:OFF_CK] = z[:, OFF_AV:OFF_CK].astype(BF16)
    z_ref[:, OFF_CK:OFF_CV] = (z[:, OFF_CK:OFF_CV] * Q_SCALE).astype(BF16)
    z_ref[:, OFF_CV:OFF_CG] = z[:, OFF_CV:OFF_CG].astype(BF16)
    cg_ref[...] = z[:, OFF_CG:OFF_DU]
    du_ref[...] = z[:, OFF_DU:]


def _du_spec():
    return pl.BlockSpec((None, S5_SEG, GROUP_WIDTH), lambda i: (i // SUBLANES, 0, i % SUBLANES))


def _project(x, mods, mod_row, mod_tokens, g1, w_in_bf, qn, kn, rope_tabs, layer, *, seq_len):
    tm = S5_SEG
    n = x.shape[0]
    tps = seq_len // tm
    return pl.pallas_call(
        _proj_kernel,
        grid=(n // tm,),
        in_specs=[pl.BlockSpec((tm, D_MODEL), lambda i: (i, 0)),
                  _mod_spec(layer, mod_row, mod_tokens // tm),
                  pl.BlockSpec((1, D_MODEL), lambda i: (0, 0)),
                  pl.BlockSpec((None, D_MODEL, IN_WIDTH), lambda i: (layer, 0, 0)),
                  pl.BlockSpec((1, GROUP_WIDTH), lambda i: (0, 0)),
                  pl.BlockSpec((1, KV_WIDTH), lambda i: (0, 0))]
                 + [pl.BlockSpec((tm, GROUP_WIDTH), lambda i: (i % tps, 0))] * 3,
        out_specs=[pl.BlockSpec((tm, OFF_CG), lambda i: (i, 0)),
                   pl.BlockSpec((tm, GROUP_WIDTH), lambda i: (i, 0)), _du_spec()],
        out_shape=[jax.ShapeDtypeStruct((n, OFF_CG), BF16),
                   jax.ShapeDtypeStruct((n, GROUP_WIDTH), F32),
                   jax.ShapeDtypeStruct((n // (tm * SUBLANES), S5_SEG, SUBLANES * GROUP_WIDTH), F32)],
        compiler_params=_cparams("parallel"),
        name="project",
    )(x, mods, g1, w_in_bf, qn, kn, *rope_tabs)


def _rope_tables():
    t = jnp.arange(DEC_SEQ)
    row = (t // GRID_W).astype(F32)
    col = (t % GRID_W).astype(F32)
    nf = HEAD_DIM // 4
    inv = ROPE_THETA ** (-jnp.arange(nf, dtype=F32) / nf)
    ang_r = row[:, None] * inv[None, :]
    ang_c = col[:, None] * inv[None, :]
    zeros = jnp.zeros_like(ang_r)
    cos = jnp.concatenate([jnp.cos(ang_r), jnp.cos(ang_r), jnp.cos(ang_c), jnp.cos(ang_c)], axis=-1)
    sa = jnp.concatenate([-jnp.sin(ang_r), zeros, -jnp.sin(ang_c), zeros], axis=-1)
    sb = jnp.concatenate([zeros, jnp.sin(ang_r), zeros, jnp.sin(ang_c)], axis=-1)
    return tuple(jnp.tile(a, (1, 4)) for a in (cos, sa, sb))


N_HEADS = 4


def _lane_head(width):
    return lax.broadcasted_iota(jnp.int32, (1, width), 1) // HEAD_DIM


def _stack_heads(q):
    head = _lane_head(q.shape[1])
    return jnp.concatenate([jnp.where(head == h, q, 0.0) for h in range(N_HEADS)], axis=0).astype(BF16)


def _stack_heads_gqa(q):
    lo = lax.broadcasted_iota(jnp.int32, (1, LANES), 1) < HEAD_DIM
    q = q.astype(F32)
    q01, q23 = q[:, :LANES], q[:, LANES:]
    blocks = [jnp.where(lo, q01, 0.0), jnp.where(lo, pltpu.roll(q01, HEAD_DIM, 1), 0.0),
              jnp.where(lo, 0.0, pltpu.roll(q23, HEAD_DIM, 1)), jnp.where(lo, 0.0, q23)]
    return jnp.concatenate(blocks, axis=0).astype(BF16)


def _spread_kv_gqa(v):
    lo = lax.broadcasted_iota(jnp.int32, (1, LANES), 1) < HEAD_DIM
    v = v.astype(F32)
    vr = pltpu.roll(v, HEAD_DIM, 1)
    return jnp.concatenate([jnp.where(lo, v, vr), jnp.where(lo, vr, v)], axis=1)


def _mha(qs, blocks, tq):
    scores = []
    for k, _, bias in blocks:
        s = _bdot_nt(qs, k)
        scores.append(s if bias is None else s + bias)
    m = functools.reduce(jnp.maximum, [jnp.max(s, axis=-1, keepdims=True) for s in scores])
    es = [jnp.exp(s - m) for s in scores]
    denom = functools.reduce(jnp.add, [jnp.sum(e, axis=-1, keepdims=True) for e in es])
    ps = [e.astype(BF16) for e in es]
    head = _lane_head(N_HEADS * HEAD_DIM)
    vals = [v.astype(BF16) for _, v, _ in blocks]
    o = None
    dall = None
    for h in range(N_HEADS):
        rows = slice(h * tq, (h + 1) * tq)
        for p, v in zip(ps, vals):
            t = jnp.dot(p[rows], jnp.where(head == h, v, jnp.zeros_like(v)), preferred_element_type=F32)
            o = t if o is None else o + t
        d = jnp.where(head == h, denom[rows], 0.0)
        dall = d if dall is None else dall + d
    return (o / dall).astype(BF16)


def _lat_attn_a_kernel(q_ref, kn_ref, vn_ref, kc_ref, vc_ref, o_ref):
    for b in range(DEC_BATCH):
        o_ref[b] = _mha(_stack_heads_gqa(q_ref[b] * Q_SCALE),
                        [(kc_ref[b], _spread_kv_gqa(vc_ref[b]), None),
                         (kn_ref[b], _spread_kv_gqa(vn_ref[b]), None)], q_ref.shape[1])


def _lat_attention_a(z, cache_k, cache_v, layer, tq=LAT_TQ):
    cache_spec = pl.BlockSpec((DEC_BATCH, None, PAST_LEN, KV_WIDTH), lambda j: (0, layer, 0, 0))
    return pl.pallas_call(
        _lat_attn_a_kernel,
        grid=(DEC_SEQ // tq,),
        in_specs=[pl.BlockSpec((DEC_BATCH, tq, GROUP_WIDTH), lambda j: (0, j, OFF_AQ // GROUP_WIDTH)),
                  pl.BlockSpec((DEC_BATCH, DEC_SEQ, KV_WIDTH), lambda j: (0, 0, OFF_AK // KV_WIDTH)),
                  pl.BlockSpec((DEC_BATCH, DEC_SEQ, KV_WIDTH), lambda j: (0, 0, OFF_AV // KV_WIDTH)),
                  cache_spec, cache_spec],
        out_specs=pl.BlockSpec((DEC_BATCH, tq, GROUP_WIDTH), lambda j: (0, j, 0)),
        out_shape=jax.ShapeDtypeStruct((DEC_BATCH, DEC_SEQ, GROUP_WIDTH), BF16),
        compiler_params=_cparams("parallel"),
        name="lat_attention_a",
    )(z, z, z, cache_k, cache_v)


NA_KEYS = NA_ROWS * GRID_W


NA_PAIRS = 2 * NA_ROWS - 2


NA_STEP_ROWS = 2


def _na_kernel(q_ref, k_ref, v_ref, kc_ref, vc_ref, bias_ref, o_ref):
    rows = DEC_SEQ // GRID_W
    for b in range(q_ref.shape[0]):
        outs = []
        for rr in range(NA_STEP_ROWS):
            r = pl.program_id(0) * NA_STEP_ROWS + rr
            row_start = jnp.clip(r - NA_ROWS // 2, 0, rows - NA_ROWS)
            start = pl.multiple_of(row_start * GRID_W, GRID_W)
            rel0 = row_start - r + NA_ROWS - 1
            kl = k_ref[b, pl.ds(start, NA_KEYS), :]
            vl = v_ref[b, pl.ds(start, NA_KEYS), :]
            bias = jnp.concatenate(
                [jnp.concatenate([bias_ref[h, rel0 + 2 * jp] for jp in range(NA_ROWS // 2)], axis=1)
                 for h in range(B_HEADS)], axis=0)
            qrows = slice(rr * GRID_W, (rr + 1) * GRID_W)
            outs.append(_mha(_stack_heads(q_ref[b, qrows, :] * Q_SCALE),
                             [(kl, vl, bias), (kc_ref[b], vc_ref[b], None)], GRID_W))
        o_ref[b] = jnp.concatenate(outs, axis=0)


def _na_bias(rel_bias):
    nrel = 2 * NA_COLS - 1
    period = 2 * GRID_W
    b = rel_bias.astype(F32)
    ext = jnp.concatenate([b[..., NA_COLS - 1:],
                           jnp.zeros(b.shape[:-1] + (period - nrel,), F32),
                           b[..., :NA_COLS - 1]], axis=-1)
    flat = jnp.tile(ext, (1, 1, GRID_W))[..., :GRID_W * (period - 1)]
    toe = flat.reshape(b.shape[:-1] + (GRID_W, period - 1))[..., :GRID_W]
    col_start = np.clip(np.arange(GRID_W) - NA_COLS // 2, 0, GRID_W - NA_COLS)
    kc = np.arange(GRID_W)
    inside = (kc[None, :] >= col_start[:, None]) & (kc[None, :] < col_start[:, None] + NA_COLS)
    toe = jnp.where(jnp.asarray(inside), toe, NEG_BIG)
    return jnp.concatenate([toe[:, :-1], toe[:, 1:]], axis=-1)


def _lat_attention_b(z, cache_k, cache_v, bias, layer):
    tq = NA_STEP_ROWS * GRID_W
    cache_spec = pl.BlockSpec((DEC_BATCH, None, PAST_LEN, GROUP_WIDTH), lambda r: (0, layer, 0, 0))
    return pl.pallas_call(
        _na_kernel,
        grid=(DEC_SEQ // tq,),
        in_specs=[pl.BlockSpec((DEC_BATCH, tq, GROUP_WIDTH), lambda r: (0, r, OFF_BQ // GROUP_WIDTH)),
                  pl.BlockSpec((DEC_BATCH, DEC_SEQ, GROUP_WIDTH), lambda r: (0, 0, OFF_BK // GROUP_WIDTH)),
                  pl.BlockSpec((DEC_BATCH, DEC_SEQ, GROUP_WIDTH), lambda r: (0, 0, OFF_BV // GROUP_WIDTH)),
                  cache_spec, cache_spec,
                  pl.BlockSpec((B_HEADS, NA_PAIRS, GRID_W, 2 * GRID_W), lambda r: (0, 0, 0, 0))],
        out_specs=pl.BlockSpec((DEC_BATCH, tq, GROUP_WIDTH), lambda r: (0, r, 0)),
        out_shape=jax.ShapeDtypeStruct((DEC_BATCH, DEC_SEQ, GROUP_WIDTH), BF16),
        compiler_params=_cparams("parallel"),
        name="lat_attention_b",
    )(z, z, z, cache_k, cache_v, bias)


def _retention_core(q, k, v, g, dec_ref, gn_ref, dec_scr, *, seq_len, i0, decay_fill, s0_ref=None,
                    want_state=False):
    tq = q.shape[0]
    head = _lane_head(C_HEADS * HEAD_DIM)
    lg = jax.nn.log_sigmoid(dec_ref[...])

    def per_lane(row0):
        out = jnp.zeros((1, C_HEADS * HEAD_DIM), F32)
        for h in range(C_HEADS):
            out = jnp.where(head == h, lg[row0 + h:row0 + h + 1, 0:1], out)
        return out

    lgf_l, lgb_l = per_lane(0), per_lane(C_HEADS)
    qi = (i0 + lax.broadcasted_iota(jnp.int32, (tq, 1), 0)).astype(F32)

    def fill_decay():
        kj = lax.broadcasted_iota(jnp.int32, (1, seq_len), 1).astype(F32)
        diff = qi - kj
        for h in range(C_HEADS):
            lgf = lg[h:h + 1, 0:1]
            lgb = lg[C_HEADS + h:C_HEADS + h + 1, 0:1]
            dec_scr[h * tq:(h + 1) * tq, :] = (
                jnp.where(diff >= 0, jnp.exp(lgf * jnp.maximum(diff, 0.0)), 0.0)
                + jnp.where(diff <= 0, jnp.exp(lgb * jnp.maximum(-diff, 0.0)), 0.0))

    if decay_fill == "first_step":
        pl.when(pl.program_id(0) == 0)(fill_decay)
    elif decay_fill == "every_step":
        fill_decay()
    else:
        assert decay_fill == "filled"

    v = v.astype(BF16)
    sc = (_bdot_nt(_stack_heads(q), k) * dec_scr[...]).astype(BF16)
    o = None
    for h in range(C_HEADS):
        t = jnp.dot(sc[h * tq:(h + 1) * tq], jnp.where(head == h, v, jnp.zeros_like(v)),
                    preferred_element_type=F32)
        o = t if o is None else o + t
    if s0_ref is not None:
        o = (o + _bdot(q, s0_ref[0]) * jnp.exp(lgf_l * (qi + 1.0))
             + _bdot(q, s0_ref[1]) * jnp.exp(lgb_l * (seq_len - qi)))
    gm = _group_mean_matrix(C_HEADS * HEAD_DIM)
    dlt = o - _dot_hilo_lhs(o, gm)
    var = _dot_hilo_lhs(dlt * dlt, gm)
    out = (dlt * lax.rsqrt(var + EPS) * gn_ref[...] * jax.nn.silu(g)).astype(BF16)
    if not want_state:
        return out, None
    kpos = lax.broadcasted_iota(jnp.int32, (seq_len, 1), 0).astype(F32)
    sf = _bdot_tn(k * jnp.exp(lgf_l * (seq_len - 1.0 - kpos)), v)
    sb = _bdot_tn(k * jnp.exp(lgb_l * kpos), v)
    return out, (sf, sb)


def _store_retention_state(st_ref, slot, state):
    for s in range(st_ref.shape[0]):
        for d in range(2):
            for h in range(C_HEADS):
                sl = slice(h * HEAD_DIM, (h + 1) * HEAD_DIM)
                st_ref[s, d, h] = state[d][sl, sl] if s == slot else jnp.zeros((HEAD_DIM, HEAD_DIM), F32)


def _retention_kernel(q_ref, g_ref, k_ref, v_ref, dec_ref, gn_ref, s0_ref, o_ref, dec_scr, *, seq_len, tq):
    for b in range(q_ref.shape[0]):
        o_ref[b], _ = _retention_core(q_ref[b], k_ref[b], v_ref[b], g_ref[b], dec_ref, gn_ref, dec_scr,
                                      seq_len=seq_len, i0=pl.program_id(0) * tq,
                                      decay_fill="every_step" if b == 0 else "filled", s0_ref=s0_ref.at[b])


def _retention(z, cg, dec, gn, s0, layer, *, tq=LAT_TQ):
    nb, seq_len = z.shape[:2]
    return pl.pallas_call(
        functools.partial(_retention_kernel, seq_len=seq_len, tq=tq),
        grid=(seq_len // tq,),
        in_specs=[pl.BlockSpec((nb, tq, GROUP_WIDTH), lambda j: (0, j, OFF_CQ // GROUP_WIDTH)),
                  pl.BlockSpec((nb, tq, GROUP_WIDTH), lambda j: (0, j, 0)),
                  pl.BlockSpec((nb, seq_len, GROUP_WIDTH), lambda j: (0, 0, OFF_CK // GROUP_WIDTH)),
                  pl.BlockSpec((nb, seq_len, GROUP_WIDTH), lambda j: (0, 0, OFF_CV // GROUP_WIDTH)),
                  pl.BlockSpec((SUBLANES, LANES), lambda j: (0, 0)),
                  pl.BlockSpec((1, GROUP_WIDTH), lambda j: (0, 0)),
                  pl.BlockSpec((nb, None, 2, GROUP_WIDTH, GROUP_WIDTH), lambda j: (0, layer, 0, 0, 0))],
        out_specs=pl.BlockSpec((nb, tq, GROUP_WIDTH), lambda j: (0, j, 0)),
        out_shape=jax.ShapeDtypeStruct((nb, seq_len, GROUP_WIDTH), BF16),
        scratch_shapes=[pltpu.VMEM((C_HEADS * tq, seq_len), F32)],
        compiler_params=_cparams("arbitrary"),
        name="retention",
    )(z, cg, z, z, dec, gn, s0)


CTX_SEQS = 4


def _ctx_front_kernel(x_ref, mod_ref, g1_ref, w_ref, qn_ref, kn_ref, dec_ref, gn_ref, *rest, n_alias, slot):
    (oa_ref, ob_ref, oc_ref, du_ref, ak_ref, av_ref, bk_ref, bv_ref, st_ref, dec_scr) = rest[n_alias:]
    tq = x_ref.shape[0] // CTX_SEQS
    h = _rms_rows(x_ref[...]) * g1_ref[...] * (1.0 + mod_ref[1:2, :]) + mod_ref[0:1, :]
    zz = jnp.dot(h.astype(BF16), w_ref[...], preferred_element_type=F32)
    for s in range(CTX_SEQS):
        rows = slice(s * tq, (s + 1) * tq)
        z = zz[rows, :]
        aq = _head_norm(z[:, OFF_AQ:OFF_AK], qn_ref[...])
        ak = _head_norm(z[:, OFF_AK:OFF_AV], kn_ref[...])
        av, bq, bk, bv = (z[:, OFF_AV:OFF_BQ], z[:, OFF_BQ:OFF_BK], z[:, OFF_BK:OFF_BV], z[:, OFF_BV:OFF_CQ])
        oa_ref[rows, :] = _mha(_stack_heads_gqa(aq * Q_SCALE), [(ak, _spread_kv_gqa(av), None)], tq)
        ob_ref[rows, :] = _mha(_stack_heads(bq * Q_SCALE), [(bk, bv, None)], tq)
        oc_ref[rows, :], state = _retention_core(
            z[:, OFF_CQ:OFF_CK], z[:, OFF_CK:OFF_CV] * Q_SCALE, z[:, OFF_CV:OFF_CG], z[:, OFF_CG:OFF_DU],
            dec_ref, gn_ref, dec_scr, seq_len=tq, i0=0, decay_fill="first_step" if s == 0 else "filled",
            want_state=True)
        du_ref[:, s * GROUP_WIDTH:(s + 1) * GROUP_WIDTH] = z[:, OFF_DU:]
        _store_layer_slot(ak_ref.at[s], slot, ak)
        _store_layer_slot(av_ref.at[s], slot, av)
        _store_layer_slot(bk_ref.at[s], slot, bk)
        _store_layer_slot(bv_ref.at[s], slot, bv)
        _store_retention_state(st_ref.at[s], slot, state)


def _ctx_front(x, mods, g1, w_in_bf, qn, kn, dec, gn, layer, prev):
    assert SEQ == S5_SEG
    tm = CTX_SEQS * SEQ
    n = x.shape[0]
    nb = n // SEQ
    steps = n // tm
    per_blk = SUBLANES // CTX_SEQS
    const = lambda *shape: pl.BlockSpec(shape, lambda i: (0,) * len(shape))
    row = lambda w: pl.BlockSpec((tm, w), lambda i: (i, 0))
    in_specs = [row(D_MODEL), _mod_spec(layer, 0, steps), const(1, D_MODEL),
                pl.BlockSpec((None, D_MODEL, IN_WIDTH), lambda i: (layer, 0, 0)),
                const(1, GROUP_WIDTH), const(1, KV_WIDTH), const(SUBLANES, LANES), const(1, GROUP_WIDTH)]
    args = [x, mods, g1, w_in_bf, qn, kn, dec, gn]
    out_specs = [row(GROUP_WIDTH), row(GROUP_WIDTH), row(GROUP_WIDTH),
                 pl.BlockSpec((None, S5_SEG, CTX_SEQS * GROUP_WIDTH), lambda i: (i // per_blk, 0, i % per_blk))]
    out_shape = [jax.ShapeDtypeStruct((n, GROUP_WIDTH), BF16)] * 3 + [
        jax.ShapeDtypeStruct((nb // SUBLANES, S5_SEG, SUBLANES * GROUP_WIDTH), F32)]
    first = prev is None
    slot = 0
    for tail in ((SEQ, KV_WIDTH), (SEQ, KV_WIDTH), (SEQ, GROUP_WIDTH), (SEQ, GROUP_WIDTH),
                 (2, C_HEADS, HEAD_DIM, HEAD_DIM)):
        blk, idx, slot = _layer_slot_block(layer, first, tail)
        out_specs.append(pl.BlockSpec((CTX_SEQS,) + blk[1:], lambda i, idx=idx: (i,) + idx))
        out_shape.append(jax.ShapeDtypeStruct((nb, DEPTH) + tail, F32))
    aliases = {}
    if not first:
        for k, arr in enumerate(prev):
            aliases[len(args)] = 4 + k
            in_specs.append(pl.BlockSpec(memory_space=pl.ANY))
            args.append(arr)
    outs = pl.pallas_call(
        functools.partial(_ctx_front_kernel, n_alias=len(aliases), slot=slot),
        grid=(steps,),
        in_specs=in_specs,
        out_specs=out_specs,
        out_shape=out_shape,
        scratch_shapes=[pltpu.VMEM((C_HEADS * SEQ, SEQ), F32)],
        input_output_aliases=aliases,
        compiler_params=_cparams("arbitrary"),
        name="ctx_front",
    )(*args)
    return outs[0], outs[1], outs[2], outs[3], tuple(outs[4:])


def _s5_prep_kernel(lre_ref, lim_ref, ldt_ref, bre_ref, bim_ref, cre_ref, cim_ref,
                    a_ref, bm_ref, cro_ref, cio_ref, bm_scr, cr_scr, ci_scr):
    lre = lre_ref[...]
    lim = lim_ref[...]
    dt = jnp.exp(ldt_ref[...])
    mag = jnp.exp(lre * dt)
    a_re = mag * jnp.cos(lim * dt)
    a_im = mag * jnp.sin(lim * dt)
    den = lre * lre + lim * lim
    r_re = ((a_re - 1.0) * lre + a_im * lim) / den
    r_im = (a_im * lre - (a_re - 1.0) * lim) / den
    bm_scr[...] = jnp.zeros_like(bm_scr)
    cr_scr[...] = jnp.zeros_like(cr_scr)
    ci_scr[...] = jnp.zeros_like(ci_scr)
    for g in range(S5_GROUPS):
        rows = slice(g * S5_CH, (g + 1) * S5_CH)
        cols = slice(g * S5_STATE, (g + 1) * S5_STATE)
        a_ref[0:1, cols] = a_re[g:g + 1, :]
        a_ref[1:2, cols] = a_im[g:g + 1, :]
        rr, ri = r_re[g:g + 1, :], r_im[g:g + 1, :]
        br, bi = bre_ref[g], bim_ref[g]
        bm_scr[rows, cols] = rr * br - ri * bi
        bm_scr[rows, S5_SP + g * S5_STATE:S5_SP + (g + 1) * S5_STATE] = rr * bi + ri * br
        cr_scr[cols, rows] = cre_ref[g]
        ci_scr[cols, rows] = cim_ref[g]
    bm_ref[...] = bm_scr[...].astype(BF16)
    cro_ref[...] = cr_scr[...].astype(BF16)
    cio_ref[...] = ci_scr[...].astype(BF16)


def _s5_prepare(lam_re, lam_im, log_dt, b_re, b_im, c_re, c_im):
    gp = (S5_GROUPS, S5_STATE)
    ldt = jnp.broadcast_to(log_dt[..., None], (DEPTH, 2) + gp)
    bt = [jnp.swapaxes(t, -1, -2) for t in (b_re, b_im)]
    ct = [jnp.swapaxes(t, -1, -2) for t in (c_re, c_im)]

    def spec(*tail):
        return pl.BlockSpec((None, None) + tail, lambda l, d: (l, d) + (0,) * len(tail))

    return pl.pallas_call(
        _s5_prep_kernel,
        grid=(DEPTH, 2),
        in_specs=[spec(*gp)] * 3 + [spec(S5_GROUPS, S5_CH, S5_STATE)] * 2 + [spec(S5_GROUPS, S5_STATE, S5_CH)] * 2,
        out_specs=[spec(2, S5_SP), spec(GROUP_WIDTH, 2 * S5_SP), spec(S5_SP, GROUP_WIDTH), spec(S5_SP, GROUP_WIDTH)],
        out_shape=[jax.ShapeDtypeStruct((DEPTH, 2, 2, S5_SP), F32),
                   jax.ShapeDtypeStruct((DEPTH, 2, GROUP_WIDTH, 2 * S5_SP), BF16),
                   jax.ShapeDtypeStruct((DEPTH, 2, S5_SP, GROUP_WIDTH), BF16),
                   jax.ShapeDtypeStruct((DEPTH, 2, S5_SP, GROUP_WIDTH), BF16)],
        scratch_shapes=[pltpu.VMEM((GROUP_WIDTH, 2 * S5_SP), F32), pltpu.VMEM((S5_SP, GROUP_WIDTH), F32),
                        pltpu.VMEM((S5_SP, GROUP_WIDTH), F32)],
        compiler_params=_cparams("parallel", "parallel"),
        name="s5_prepare",
    )(lam_re, lam_im, ldt, bt[0], bt[1], ct[0], ct[1])


def _cmul(ar, ai, br, bi):
    return ar * br - ai * bi, ar * bi + ai * br


def _s5_kernel(u_ref, h0_ref, a_ref, bm_ref, cre_ref, cim_ref, dvec_ref, glu_ref, *rest, nseg, slot):
    od_ref, fin_ref, x_scr, s_scr, y_scr = rest[-5:]
    steps = S5_SEG
    rows = steps * SUBLANES
    chunk = S5_CHUNK
    chunk_steps = chunk // SUBLANES
    nchunk = rows // chunk
    seg = lax.broadcasted_iota(jnp.int32, (SUBLANES, S5_SP), 0) % nseg

    for d in range(2):
        ar = jnp.broadcast_to(a_ref[d, 0:1, :], (SUBLANES, S5_SP))
        ai = jnp.broadcast_to(a_ref[d, 1:2, :], (SUBLANES, S5_SP))

        def row0(k):
            c = k if d == 0 else nchunk - 1 - k
            return c * chunk if isinstance(c, int) else pl.multiple_of(c * chunk, chunk)

        def input_part(k, buf):
            x_scr[buf] = jnp.dot(u_ref[pl.ds(row0(k), chunk), :].astype(BF16), bm_ref[d],
                                 preferred_element_type=F32)

        def scan_part(buf, carry, store):
            sr, si = carry
            for t in range(chunk_steps):
                r = (t if d == 0 else chunk_steps - 1 - t) * SUBLANES
                pr, pi = _cmul(ar, ai, sr, si)
                sr = pr + x_scr[buf, r:r + SUBLANES, 0:S5_SP]
                si = pi + x_scr[buf, r:r + SUBLANES, S5_SP:]
                if store:
                    s_scr[buf, r:r + SUBLANES, 0:S5_SP] = sr
                    s_scr[buf, r:r + SUBLANES, S5_SP:] = si
            return sr, si

        def output_part(k, buf):
            y = _bdot(s_scr[buf, :, 0:S5_SP], cre_ref[d]) - _bdot(s_scr[buf, :, S5_SP:], cim_ref[d])
            rows_k = pl.ds(row0(k), chunk)
            if d == 0:
                y_scr[rows_k, :] = y
            else:
                zz = jax.nn.gelu(y_scr[rows_k, :] + y + dvec_ref[...] * u_ref[rows_k, :])
                od_ref[rows_k, :] = (zz * jax.nn.sigmoid(_bdot(zz, glu_ref[...]))).astype(BF16)

        def half(k, buf, carry, store, nxt=True, prev=True):
            if nxt:
                input_part(k + 1, 1 - buf)
            carry = scan_part(buf, carry, store)
            if store and prev:
                output_part(k - 1, 1 - buf)
            return carry

        def run_pass(carry, store):
            input_part(0, 0)
            carry = half(0, 0, carry, store, prev=False)
            carry = half(1, 1, carry, store)

            for k in range(2, nchunk - 2):
                carry = half(k, k % 2, carry, store)
            carry = half(nchunk - 2, 0, carry, store)
            carry = half(nchunk - 1, 1, carry, store, nxt=False)
            if store:
                output_part(nchunk - 1, 1)
            return carry

        init = (h0_ref[d, :, 0:S5_SP], h0_ref[d, :, S5_SP:])
        if nseg > 1:
            zero = jnp.zeros((SUBLANES, S5_SP), F32)
            fr, fi = run_pass((zero, zero), store=False)
            pr, pi = ar, ai
            for _ in range(int(math.log2(steps))):
                pr, pi = _cmul(pr, pi, pr, pi)
            cr, ci = init
            shift = 1 if d == 0 else SUBLANES - 1
            order = range(1, nseg) if d == 0 else range(nseg - 2, -1, -1)
            for s in order:
                ncr, nci = pltpu.roll(cr, shift, 0), pltpu.roll(ci, shift, 0)
                nfr, nfi = pltpu.roll(fr, shift, 0), pltpu.roll(fi, shift, 0)
                qr, qi = _cmul(pr, pi, ncr, nci)
                cr = jnp.where(seg == s, qr + nfr, cr)
                ci = jnp.where(seg == s, qi + nfi, ci)
            init = (cr, ci)
        sr, si = run_pass(init, store=True)
        for s in range(fin_ref.shape[1] // (4 * S5_SP)):
            base = (4 * s + 2 * d) * S5_SP
            fin_ref[:, base:base + S5_SP] = sr if s == slot else jnp.zeros_like(sr)
            fin_ref[:, base + S5_SP:base + 2 * S5_SP] = si if s == slot else jnp.zeros_like(si)


def _s5(du_tm, h0, a, bmat, cre, cim, dvec, glu_bf, layer, *, nseg, fin_layer=0, fin_layers=1,
        prev_fin=None):
    nblk = du_tm.shape[0]
    rows = S5_SEG * SUBLANES
    fin_w = 4 * S5_SP
    in_specs = [pl.BlockSpec((None, rows, GROUP_WIDTH), lambda i: (i, 0, 0)),
                pl.BlockSpec((2, SUBLANES, 2 * S5_SP), lambda i: (0, 0, 0)),
                pl.BlockSpec((None, 2, 2, S5_SP), lambda i: (layer, 0, 0, 0)),
                pl.BlockSpec((None, 2, GROUP_WIDTH, 2 * S5_SP), lambda i: (layer, 0, 0, 0)),
                pl.BlockSpec((None, 2, S5_SP, GROUP_WIDTH), lambda i: (layer, 0, 0, 0)),
                pl.BlockSpec((None, 2, S5_SP, GROUP_WIDTH), lambda i: (layer, 0, 0, 0)),
                pl.BlockSpec((1, GROUP_WIDTH), lambda i: (0, 0)),
                pl.BlockSpec((None, GROUP_WIDTH, GROUP_WIDTH), lambda i: (layer, 0, 0))]
    args = [du_tm.reshape(nblk, rows, GROUP_WIDTH), h0, a, bmat, cre, cim, dvec, glu_bf]
    aliases = {}
    if prev_fin is not None:
        aliases[len(args)] = 1
        in_specs.append(pl.BlockSpec(memory_space=pl.ANY))
        args.append(prev_fin)
        fin_spec, slot = pl.BlockSpec((SUBLANES, fin_w), lambda i: (i, fin_layer)), 0
    else:
        fin_spec, slot = pl.BlockSpec((SUBLANES, fin_layers * fin_w), lambda i: (i, 0)), fin_layer
    od, fin = pl.pallas_call(
        functools.partial(_s5_kernel, nseg=nseg, slot=slot),
        grid=(nblk,),
        in_specs=in_specs,
        out_specs=[pl.BlockSpec((None, rows, GROUP_WIDTH), lambda i: (i, 0, 0)), fin_spec],
        out_shape=[jax.ShapeDtypeStruct((nblk, rows, GROUP_WIDTH), BF16),
                   jax.ShapeDtypeStruct((nblk * SUBLANES, fin_layers * fin_w), F32)],
        scratch_shapes=[pltpu.VMEM((2, S5_CHUNK, 2 * S5_SP), F32), pltpu.VMEM((2, S5_CHUNK, 2 * S5_SP), F32),
                        pltpu.VMEM((rows, GROUP_WIDTH), F32)],
        input_output_aliases=aliases,
        compiler_params=_cparams("parallel"),
        name="s5",
    )(*args)
    return od.reshape(nblk, S5_SEG, SUBLANES * GROUP_WIDTH), fin


ROUTE_GROUP = MOE_PER_GROUP
OUT_SEQS = 2
MOE_TILE = 512


def _out_kernel(x_ref, oa_ref, ob_ref, oc_ref, od_ref, mod_ref, wo_ref, g2_ref, wrh_ref, wrl_ref, br_ref,
                xm_ref, h2_ref, route_ref, cnt_ref):
    od = jnp.concatenate([od_ref[:, s * GROUP_WIDTH:(s + 1) * GROUP_WIDTH] for s in range(OUT_SEQS)], axis=0)
    mix = functools.reduce(jnp.add, [
        _bdot(o, wo_ref[i * GROUP_WIDTH:(i + 1) * GROUP_WIDTH, :])
        for i, o in enumerate((oa_ref[...], ob_ref[...], oc_ref[...], od))])
    xm = x_ref[...] + mod_ref[2:3, :] * mix
    xm_ref[...] = xm
    h2 = _rms_rows(xm) * g2_ref[...] * (1.0 + mod_ref[4:5, :]) + mod_ref[3:4, :]
    h2_ref[...] = h2.astype(BF16)

    h_hi, h_lo = _split(h2)
    logits = (jnp.dot(h_hi, wrh_ref[...], preferred_element_type=F32)
              + jnp.dot(h_hi, wrl_ref[...], preferred_element_type=F32)
              + jnp.dot(h_lo, wrh_ref[...], preferred_element_type=F32)) + br_ref[...]
    lane_i = lax.broadcasted_iota(jnp.int32, logits.shape, 1)
    lane = lane_i.astype(F32)
    big = jnp.float32(2 ** 30)
    gmask = lane_i < MOE_GROUPS
    gl = jnp.where(gmask, logits, -jnp.inf)
    gmax = jnp.max(gl, axis=-1, keepdims=True)
    p_top = 1.0 / jnp.sum(jnp.exp(gl - gmax), axis=-1, keepdims=True)
    g_top = jnp.min(jnp.where(gl == gmax, lane, big), axis=-1, keepdims=True)
    e_lane = lane_i - ROUTER_OFF
    lane_group = (e_lane // MOE_PER_GROUP).astype(F32)
    emask = (e_lane >= 0) & (e_lane < MOE_EXPERTS) & (lane_group == g_top)
    el = jnp.where(emask, logits, -jnp.inf)
    m1 = jnp.max(el, axis=-1, keepdims=True)
    i1 = jnp.min(jnp.where(el == m1, lane, big), axis=-1, keepdims=True)
    el2 = jnp.where(lane == i1, -jnp.inf, el)
    m2 = jnp.max(el2, axis=-1, keepdims=True)
    i2 = jnp.min(jnp.where(el2 == m2, lane, big), axis=-1, keepdims=True)
    e2 = jnp.exp(m2 - m1)
    den = 1.0 + e2
    gates = (jnp.where(lane == i1, (1.0 / den) * p_top, 0.0)
             + jnp.where(lane == i2, (e2 / den) * p_top, 0.0))
    route = jnp.where(lane == ROUTE_GROUP + g_top, 1.0, 0.0)
    for g in range(MOE_GROUPS):
        local = pltpu.roll(gates, LANES - ROUTER_OFF - g * MOE_PER_GROUP, 1)
        route = route + jnp.where((g_top == g) & (lane_i < MOE_PER_GROUP), local, 0.0)
    route_ref[...] = route
    for t in range(cnt_ref.shape[0]):
        part = jnp.sum(route[t * MOE_TILE:(t + 1) * MOE_TILE], axis=0, keepdims=True)
        cnt_ref[t] = jnp.broadcast_to(part, (SUBLANES, LANES)).astype(jnp.int32)


def _output_stage(x, mixes, mods, mod_row, mod_tokens, wo_bf, g2, wr_hi, wr_lo, br, layer):
    tm = OUT_SEQS * S5_SEG
    n = x.shape[0]
    row = lambda w: pl.BlockSpec((tm, w), lambda i: (i, 0))
    const = lambda shape: pl.BlockSpec(shape, lambda i: (0,) * len(shape))
    per_blk = SUBLANES // OUT_SEQS
    return pl.pallas_call(
        _out_kernel,
        grid=(n // tm,),
        in_specs=[row(D_MODEL), row(GROUP_WIDTH), row(GROUP_WIDTH), row(GROUP_WIDTH),
                  pl.BlockSpec((None, S5_SEG, OUT_SEQS * GROUP_WIDTH), lambda i: (i // per_blk, 0, i % per_blk)),
                  _mod_spec(layer, mod_row, mod_tokens // tm),
                  pl.BlockSpec((None, D_MODEL, D_MODEL), lambda i: (layer, 0, 0)), const((1, D_MODEL)),
                  const((D_MODEL, LANES)), const((D_MODEL, LANES)), const((1, LANES))],
        out_specs=[row(D_MODEL), row(D_MODEL), row(LANES),
                   pl.BlockSpec((tm // MOE_TILE, SUBLANES, LANES), lambda i: (i, 0, 0))],
        out_shape=[jax.ShapeDtypeStruct((n, D_MODEL), F32),
                   jax.ShapeDtypeStruct((n, D_MODEL), BF16),
                   jax.ShapeDtypeStruct((n, LANES), F32),
                   jax.ShapeDtypeStruct((n // MOE_TILE, SUBLANES, LANES), jnp.int32)],
        compiler_params=_cparams("parallel"),
        name="output_stage",
    )(x, *mixes, mods, wo_bf, g2, wr_hi, wr_lo, br)


GROUP_HID = MOE_PER_GROUP * MOE_HIDDEN


MOE_CHUNK = 160


def _moe_kernel(cnt_ref, h2_ref, route_ref, xm_ref, mod_ref, w1_ref, w3_ref, w2_ref, fg_ref, before_ref, o_ref,
                hs_scr, rs_scr, os_scr, *, final, tm):
    i = pl.program_id(0)
    off1 = cnt_ref[i, 0]
    off2 = off1 + cnt_ref[i, 1]
    off3 = off2 + cnt_ref[i, 2]
    starts = (jnp.int32(0), off1, off2, off3)
    ends = (off1, off2, off3, jnp.int32(tm))

    route = route_ref[...]
    r_hi, r_lo = _split(route)
    pick = (lax.broadcasted_iota(jnp.int32, (SUBLANES, LANES), 1)
            == ROUTE_GROUP + lax.broadcasted_iota(jnp.int32, (SUBLANES, LANES), 0))
    gt = lax.dot_general(jnp.where(pick, 1.0, 0.0).astype(BF16), r_hi, (((1,), (1,)), ((), ())),
                         preferred_element_type=F32)
    rank = jnp.dot(gt.astype(BF16), before_ref[...], preferred_element_type=F32)
    gt_i = gt.astype(jnp.int32)
    rank_i = rank.astype(jnp.int32)
    pos = jnp.zeros((1, tm), jnp.int32)
    for g in range(MOE_GROUPS):
        pos = pos + gt_i[g:g + 1, :] * (rank_i[g:g + 1, :] + starts[g])
    perm = jnp.where(lax.broadcasted_iota(jnp.int32, (tm, tm), 0) == pos, 1.0, 0.0).astype(BF16)
    hs_scr[...] = jnp.dot(perm, h2_ref[...], preferred_element_type=F32).astype(BF16)
    rs_scr[...] = (jnp.dot(perm, r_hi, preferred_element_type=F32)
                   + jnp.dot(perm, r_lo, preferred_element_type=F32))

    os_scr[...] = jnp.zeros_like(os_scr)

    def evaluate(g, lo, hi, chunk):
        base = (lo // BF16_ROWS) * BF16_ROWS
        r0 = pl.multiple_of(jnp.minimum(base + chunk * MOE_CHUNK, tm - MOE_CHUNK), BF16_ROWS)
        rows = pl.ds(r0, MOE_CHUNK)
        x = hs_scr[rows, :]
        gates = rs_scr[rows, :]
        a = jnp.dot(x, w1_ref[g], preferred_element_type=F32)
        b = jnp.dot(x, w3_ref[g], preferred_element_type=F32)
        hid = []
        for e in range(MOE_PER_GROUP):
            sl = slice(e * MOE_HIDDEN, (e + 1) * MOE_HIDDEN)
            hid.append((jax.nn.silu(a[:, sl]) * b[:, sl] * gates[:, e:e + 1]).astype(BF16))
        y = jnp.dot(jnp.concatenate(hid, axis=1), w2_ref[g], preferred_element_type=F32)
        rowid = r0 + lax.broadcasted_iota(jnp.int32, (MOE_CHUNK, 1), 0)
        member = (rowid >= lo) & (rowid < hi)
        os_scr[rows, :] = jnp.where(member, y, os_scr[rows, :])

    extra = []
    for g in range(MOE_GROUPS):
        lo, hi = starts[g], ends[g]
        base = (lo // BF16_ROWS) * BF16_ROWS
        n_chunks = jnp.where(hi > lo, (hi - base + MOE_CHUNK - 1) // MOE_CHUNK, 0)
        evaluate(g, lo, hi, 0)
        extra.append(jnp.maximum(n_chunks - 1, 0))
    first = (jnp.int32(0), extra[0], extra[0] + extra[1], extra[0] + extra[1] + extra[2])

    def pick_by_group(g, vals):
        return jnp.where(g == 0, vals[0], jnp.where(g == 1, vals[1], jnp.where(g == 2, vals[2], vals[3])))

    def extra_body(idx, carry):
        g = sum((idx >= f).astype(jnp.int32) for f in first[1:])
        evaluate(g, pick_by_group(g, starts), pick_by_group(g, ends), idx - pick_by_group(g, first) + 1)
        return carry

    lax.fori_loop(0, first[3] + extra[3], extra_body, 0)

    o_hi, o_lo = _split(os_scr[...])
    moe = (lax.dot_general(perm, o_hi, (((0,), (0,)), ((), ())), preferred_element_type=F32)
           + lax.dot_general(perm, o_lo, (((0,), (0,)), ((), ())), preferred_element_type=F32))
    out = xm_ref[...] + mod_ref[5:6, :] * moe
    if final:
        out = _rms_rows(out) * fg_ref[...]
    o_ref[...] = out


def _moe_weight_kernel(w1_ref, w3_ref, w2_ref, o1_ref, o3_ref, o2_ref):
    for e in range(MOE_PER_GROUP):
        sl = slice(e * MOE_HIDDEN, (e + 1) * MOE_HIDDEN)
        o1_ref[:, sl] = w1_ref[e].astype(BF16)
        o3_ref[:, sl] = w3_ref[e].astype(BF16)
        o2_ref[sl, :] = w2_ref[e].astype(BF16)


def _moe_weights(w1, w3, w2):
    up = pl.BlockSpec((None, MOE_PER_GROUP, D_MODEL, MOE_HIDDEN), lambda l, g: (l, g, 0, 0))
    down = pl.BlockSpec((None, MOE_PER_GROUP, MOE_HIDDEN, D_MODEL), lambda l, g: (l, g, 0, 0))
    out = pl.BlockSpec((None, None, D_MODEL, GROUP_HID), lambda l, g: (l, g, 0, 0))
    shape = jax.ShapeDtypeStruct((DEPTH, MOE_GROUPS, D_MODEL, GROUP_HID), BF16)
    return pl.pallas_call(
        _moe_weight_kernel,
        grid=(DEPTH, MOE_GROUPS),
        in_specs=[up, up, down],
        out_specs=[out, out, out],
        out_shape=[shape, shape, shape],
        compiler_params=_cparams("parallel", "parallel"),
        name="moe_weights",
    )(w1, w3, w2)


def _moe(h2, route, tile_counts, xm, mods, mod_row, mod_tokens, w1g, w3g, w2g, fg, layer, *, final):
    tm = MOE_TILE
    n = h2.shape[0]
    cnt = tile_counts[:, 0, ROUTE_GROUP:ROUTE_GROUP + MOE_GROUPS]
    before = jnp.asarray(np.triu(np.ones((tm, tm), np.float32), 1), BF16)
    row = lambda w: pl.BlockSpec((tm, w), lambda i, c: (i, 0))
    mod_tiles = mod_tokens // tm
    wspec = pl.BlockSpec((None, MOE_GROUPS, D_MODEL, GROUP_HID), lambda i, c: (layer, 0, 0, 0),
                         pipeline_mode=pl.Buffered(1))
    return pl.pallas_call(
        functools.partial(_moe_kernel, final=final, tm=tm),
        grid_spec=pltpu.PrefetchScalarGridSpec(
            num_scalar_prefetch=1,
            grid=(n // tm,),
            in_specs=[row(D_MODEL), row(LANES), row(D_MODEL),
                      pl.BlockSpec((None, None, N_MOD, D_MODEL),
                                   lambda i, c: (layer, mod_row + i // mod_tiles, 0, 0)),
                      wspec, wspec, wspec,
                      pl.BlockSpec((1, D_MODEL), lambda i, c: (0, 0)),
                      pl.BlockSpec((tm, tm), lambda i, c: (0, 0), pipeline_mode=pl.Buffered(1))],
            out_specs=row(D_MODEL),
            scratch_shapes=[pltpu.VMEM((tm, D_MODEL), BF16), pltpu.VMEM((tm, LANES), F32),
                            pltpu.VMEM((tm, D_MODEL), F32)]),
        out_shape=jax.ShapeDtypeStruct((n, D_MODEL), F32),
        compiler_params=_cparams("arbitrary"),
        name="moe",
    )(cnt, h2, route, xm, mods, w1g, w3g, w2g, fg, before)


def kernel(x_prompt, x_sample, cache_a_k, cache_a_v, cache_b_k, cache_b_v, state_ret, state_ssm, c, c_ctx, mod_w, mod_b, norm1_g, norm2_g, w_in, a_qn_g, a_kn_g, b_rel_bias, ret_decay, ret_gn_g, s5_lam_re, s5_lam_im, s5_log_dt, s5_b_re, s5_b_im, s5_c_re, s5_c_im, s5_d, s5_glu_w, w_out, moe_gw, moe_gb, moe_ew, moe_eb, moe_w1, moe_w3, moe_w2, final_norm_g):
    n_ctx = BATCH * SEQ
    n_lat = DEC_BATCH * DEC_SEQ
    lat_seg = DEC_SEQ // S5_SEG

    cond = jnp.zeros((SUBLANES, D_MODEL), F32).at[0].set(c_ctx).at[1:1 + DEC_BATCH].set(c)
    mods = _modulation(cond, mod_w, mod_b).reshape(DEPTH, SUBLANES, N_MOD, D_MODEL)

    rope_tabs = _rope_tables()
    s5_a, s5_bm, s5_cre, s5_cim = _s5_prepare(s5_lam_re, s5_lam_im, s5_log_dt, s5_b_re, s5_b_im,
                                              s5_c_re, s5_c_im)
    cak = cache_a_k.reshape(DEC_BATCH, DEPTH, PAST_LEN, A_KV_HEADS * HEAD_DIM)
    cav = cache_a_v.reshape(DEC_BATCH, DEPTH, PAST_LEN, A_KV_HEADS * HEAD_DIM)
    cbk = cache_b_k.reshape(DEC_BATCH, DEPTH, PAST_LEN, B_HEADS * HEAD_DIM)
    cbv = cache_b_v.reshape(DEC_BATCH, DEPTH, PAST_LEN, B_HEADS * HEAD_DIM)

    xc = x_prompt.reshape(n_ctx, D_MODEL)
    xs = x_sample.reshape(n_lat, D_MODEL)
    w1_all, w3_all, w2_all = _moe_weights(moe_w1, moe_w3, moe_w2)
    eye_h = jnp.eye(C_HEADS, dtype=F32)
    s0_bd = (state_ret[:, :, :, :, :, None, :] * eye_h[None, None, None, :, None, :, None]).reshape(
        DEC_BATCH, DEPTH, 2, C_HEADS * HEAD_DIM, C_HEADS * HEAD_DIM)
    ctx_state = ssm_states = None
    h0_zero = jnp.zeros((2, SUBLANES, 2 * S5_SP), F32)
    w_in_bf = w_in.astype(BF16)
    wo_bf = w_out.astype(BF16)
    glu_bf = s5_glu_w.astype(BF16)
    for l in range(DEPTH):
        final = l == DEPTH - 1
        g1 = norm1_g[l].reshape(1, D_MODEL)
        g2 = norm2_g[l].reshape(1, D_MODEL)
        fg = final_norm_g.reshape(1, D_MODEL)
        qn = jnp.tile(a_qn_g[l], A_HEADS).reshape(1, GROUP_WIDTH)
        kn = jnp.tile(a_kn_g[l], A_KV_HEADS).reshape(1, KV_WIDTH)
        dec = jnp.broadcast_to(ret_decay[l].reshape(2 * C_HEADS, 1), (2 * C_HEADS, LANES))
        gn = ret_gn_g[l].reshape(1, GROUP_WIDTH)
        dvec = s5_d[l].reshape(1, GROUP_WIDTH)
        wr = jnp.zeros((D_MODEL, LANES), F32).at[:, :MOE_GROUPS].set(moe_gw[l]).at[
            :, ROUTER_OFF:ROUTER_OFF + MOE_EXPERTS].set(moe_ew[l])
        br = jnp.zeros((1, LANES), F32).at[0, :MOE_GROUPS].set(moe_gb[l]).at[
            0, ROUTER_OFF:ROUTER_OFF + MOE_EXPERTS].set(moe_eb[l])
        wr_hi = wr.astype(BF16)
        wr_lo = (wr - wr_hi.astype(F32)).astype(BF16)
        na_bias = _na_bias(b_rel_bias[l])

        oa, ob, oc, du_tm, ctx_state = _ctx_front(xc, mods, g1, w_in_bf, qn, kn, dec, gn, l, ctx_state)
        od_tm, ssm_states = _s5(du_tm, h0_zero, s5_a, s5_bm, s5_cre, s5_cim, dvec, glu_bf, l,
                                nseg=1, fin_layer=l, fin_layers=DEPTH, prev_fin=ssm_states)
        xm, h2, route, counts = _output_stage(xc, (oa, ob, oc, od_tm), mods, 0, n_ctx, wo_bf, g2,
                                              wr_hi, wr_lo, br, l)
        xc = _moe(h2, route, counts, xm, mods, 0, n_ctx, w1_all, w3_all, w2_all, fg, l, final=final)

        zs, cg, du_tm = _project(xs, mods, 1, DEC_SEQ, g1, w_in_bf, qn, kn, rope_tabs, l, seq_len=DEC_SEQ)
        zs3 = zs.reshape(DEC_BATCH, DEC_SEQ, OFF_CG)
        oa = _lat_attention_a(zs3, cak, cav, l).reshape(n_lat, GROUP_WIDTH)
        ob = _lat_attention_b(zs3, cbk, cbv, na_bias, l).reshape(n_lat, GROUP_WIDTH)
        oc = _retention(zs3, cg.reshape(DEC_BATCH, DEC_SEQ, GROUP_WIDTH), dec, gn, s0_bd, l).reshape(
            n_lat, GROUP_WIDTH)
        h0 = state_ssm[:, l].reshape(DEC_BATCH, 2, 2 * S5_SP).transpose(1, 0, 2)
        h0_seg = jnp.zeros((2, DEC_BATCH, lat_seg, 2 * S5_SP), F32)
        h0_seg = h0_seg.at[0, :, 0].set(h0[0]).at[1, :, lat_seg - 1].set(h0[1])
        od_tm, _ = _s5(du_tm, h0_seg.reshape(2, SUBLANES, 2 * S5_SP),
                       s5_a, s5_bm, s5_cre, s5_cim, dvec, glu_bf, l, nseg=lat_seg)
        xm, h2, route, counts = _output_stage(xs, (oa, ob, oc, od_tm), mods, 1, DEC_SEQ, wo_bf, g2,
                                              wr_hi, wr_lo, br, l)
        xs = _moe(h2, route, counts, xm, mods, 1, DEC_SEQ, w1_all, w3_all, w2_all, fg, l, final=final)

    new_ak, new_av, new_bk, new_bv, ret_states = ctx_state
    return (xc.reshape(BATCH, SEQ, D_MODEL), xs.reshape(DEC_BATCH, DEC_SEQ, D_MODEL),
            new_ak.reshape(BATCH, DEPTH, SEQ, A_KV_HEADS, HEAD_DIM),
            new_av.reshape(BATCH, DEPTH, SEQ, A_KV_HEADS, HEAD_DIM),
            new_bk.reshape(BATCH, DEPTH, SEQ, B_HEADS, HEAD_DIM),
            new_bv.reshape(BATCH, DEPTH, SEQ, B_HEADS, HEAD_DIM),
            ret_states,
            ssm_states.reshape(BATCH, DEPTH, 2, 2, S5_GROUPS, S5_STATE))
```

```python
import functools
import math

import numpy as np
import jax
import jax.numpy as jnp
from jax import lax
from jax.experimental import pallas as pl
from jax.experimental.pallas import tpu as pltpu

F32 = jnp.float32
BF16 = jnp.bfloat16

D_MODEL = 1024
BATCH = 32
SEQ = 256
DEPTH = 2
DEC_BATCH = 2
DEC_SEQ = 1024
PAST_LEN = 256
GRID_W = 64
HEAD_DIM = 64
GROUP_WIDTH = 256
A_HEADS = 4
A_KV_HEADS = 2
B_HEADS = 4
NA_ROWS = 8
NA_COLS = 16
C_HEADS = 4
S5_CH = 16
S5_GROUPS = 16
S5_STATE = 64
MOE_GROUPS = 4
MOE_PER_GROUP = 8
MOE_EXPERTS = 32
MOE_HIDDEN = 128
ROPE_THETA = 10000.0
EPS = 1e-6
IN_WIDTH = 2560
Q_SCALE = HEAD_DIM ** -0.5
KV_WIDTH = A_KV_HEADS * HEAD_DIM
N_MOD = 6
ROPE_PAIR = HEAD_DIM // 4
LAT_TQ = 256
MOD_TILE = 1536

OFF_AQ, OFF_AK, OFF_AV = 0, 256, 384
OFF_BQ, OFF_BK, OFF_BV = 512, 768, 1024
OFF_CQ, OFF_CK, OFF_CV, OFF_CG = 1280, 1536, 1792, 2048
OFF_DU = 2304

LANES = 128
SUBLANES = 8
BF16_ROWS = 16
S5_SP = S5_GROUPS * S5_STATE
S5_SEG = 256
S5_CHUNK = 256
ROUTER_OFF = 4
NEG_BIG = -1e30
VMEM_LIMIT = 56 * 1024 * 1024


def _cparams(*sem):
    return pltpu.CompilerParams(dimension_semantics=sem, vmem_limit_bytes=VMEM_LIMIT)


def _mod_spec(layer, first_row, tiles_per_row):
    return pl.BlockSpec((None, None, N_MOD, D_MODEL), lambda i: (layer, first_row + i // tiles_per_row, 0, 0))


def _bdot(a, b):
    return jnp.dot(a.astype(BF16), b.astype(BF16), preferred_element_type=F32)


def _bdot_nt(a, b):
    return lax.dot_general(a.astype(BF16), b.astype(BF16), (((1,), (1,)), ((), ())),
                           preferred_element_type=F32)


def _bdot_tn(a, b):
    return lax.dot_general(a.astype(BF16), b.astype(BF16), (((0,), (0,)), ((), ())),
                           preferred_element_type=F32)


def _split(a):
    hi = a.astype(BF16)
    lo = (a - hi.astype(F32)).astype(BF16)
    return hi, lo


def _dot_hilo_lhs(a, b_bf16):
    hi, lo = _split(a)
    return (jnp.dot(hi, b_bf16, preferred_element_type=F32)
            + jnp.dot(lo, b_bf16, preferred_element_type=F32))


def _rms_rows(x):
    return x * lax.rsqrt(jnp.mean(x * x, axis=-1, keepdims=True) + EPS)


def _mod_kernel(cond_ref, w_ref, b_ref, o_ref):
    o_ref[...] = _bdot(jax.nn.silu(cond_ref[...]), w_ref[...]) + b_ref[...]


def _modulation(cond, mod_w, mod_b):
    tn = MOD_TILE
    width = N_MOD * D_MODEL
    return pl.pallas_call(
        _mod_kernel,
        grid=(DEPTH, width // tn),
        in_specs=[pl.BlockSpec((SUBLANES, D_MODEL), lambda l, j: (0, 0)),
                  pl.BlockSpec((None, D_MODEL, tn), lambda l, j: (l, 0, j)),
                  pl.BlockSpec((None, 1, tn), lambda l, j: (l, 0, j))],
        out_specs=pl.BlockSpec((None, SUBLANES, tn), lambda l, j: (l, 0, j)),
        out_shape=jax.ShapeDtypeStruct((DEPTH, SUBLANES, width), F32),
        compiler_params=_cparams("arbitrary", "arbitrary"),
        name="modulation",
    )(cond, mod_w, mod_b.reshape(DEPTH, 1, width))


def _group_mean_matrix(w):
    ri = lax.broadcasted_iota(jnp.int32, (w, w), 0) // HEAD_DIM
    ci = lax.broadcasted_iota(jnp.int32, (w, w), 1) // HEAD_DIM
    return jnp.where(ri == ci, 1.0 / HEAD_DIM, 0.0).astype(BF16)


def _head_norm(t, g):
    ms = _dot_hilo_lhs(t * t, _group_mean_matrix(t.shape[1]))
    return t * lax.rsqrt(ms + EPS) * g


def _rope(t, cos, sa, sb):
    return (t * cos + pltpu.roll(t, LANES - ROPE_PAIR, 1) * sa + pltpu.roll(t, ROPE_PAIR, 1) * sb)


def _store_layer_slot(ref, slot, value):
    for s in range(ref.shape[0]):
        ref[s] = value if s == slot else jnp.zeros_like(value)


def _layer_slot_block(layer, first_call, tail):
    if first_call:
        return (None, DEPTH) + tail, (0,) * (1 + len(tail)), layer
    return (None, 1) + tail, (layer,) + (0,) * len(tail), 0


def _proj_kernel(x_ref, mod_ref, g1_ref, w_ref, qn_ref, kn_ref, cos_ref, sa_ref, sb_ref, z_ref, cg_ref, du_ref):
    h = _rms_rows(x_ref[...]) * g1_ref[...] * (1.0 + mod_ref[1:2, :]) + mod_ref[0:1, :]
    z = jnp.dot(h.astype(BF16), w_ref[...], preferred_element_type=F32)
    aq = _head_norm(z[:, OFF_AQ:OFF_AK], qn_ref[...])
    ak = _head_norm(z[:, OFF_AK:OFF_AV], kn_ref[...])
    for j in range(3):
        t = aq[:, j * LANES:(j + 1) * LANES] if j < 2 else ak
        sl = slice(0, LANES) if j == 2 else slice(j * LANES, (j + 1) * LANES)
        t = _rope(t, cos_ref[:, sl], sa_ref[:, sl], sb_ref[:, sl])
        z_ref[:, j * LANES:(j + 1) * LANES] = t.astype(BF16)
    z_ref[:, OFF_AV:OFF_CK] = z[:, OFF_AV:OFF_CK].astype(BF16)
    z_ref[:, OFF_CK:OFF_CV] = (z[:, OFF_CK:OFF_CV] * Q_SCALE).astype(BF16)
    z_ref[:, OFF_CV:OFF_CG] = z[:, OFF_CV:OFF_CG].astype(BF16)
    cg_ref[...] = z[:, OFF_CG:OFF_DU]
    du_ref[...] = z[:, OFF_DU:]


def _du_spec():
    return pl.BlockSpec((None, S5_SEG, GROUP_WIDTH), lambda i: (i // SUBLANES, 0, i % SUBLANES))


def _project(x, mods, mod_row, mod_tokens, g1, w_in_bf, qn, kn, rope_tabs, layer, *, seq_len):
    tm = S5_SEG
    n = x.shape[0]
    tps = seq_len // tm
    return pl.pallas_call(
        _proj_kernel,
        grid=(n // tm,),
        in_specs=[pl.BlockSpec((tm, D_MODEL), lambda i: (i, 0)),
                  _mod_spec(layer, mod_row, mod_tokens // tm),
                  pl.BlockSpec((1, D_MODEL), lambda i: (0, 0)),
                  pl.BlockSpec((None, D_MODEL, IN_WIDTH), lambda i: (layer, 0, 0)),
                  pl.BlockSpec((1, GROUP_WIDTH), lambda i: (0, 0)),
                  pl.BlockSpec((1, KV_WIDTH), lambda i: (0, 0))]
                 + [pl.BlockSpec((tm, GROUP_WIDTH), lambda i: (i % tps, 0))] * 3,
        out_specs=[pl.BlockSpec((tm, OFF_CG), lambda i: (i, 0)),
                   pl.BlockSpec((tm, GROUP_WIDTH), lambda i: (i, 0)), _du_spec()],
        out_shape=[jax.ShapeDtypeStruct((n, OFF_CG), BF16),
                   jax.ShapeDtypeStruct((n, GROUP_WIDTH), F32),
                   jax.ShapeDtypeStruct((n // (tm * SUBLANES), S5_SEG, SUBLANES * GROUP_WIDTH), F32)],
        compiler_params=_cparams("parallel"),
        name="project",
    )(x, mods, g1, w_in_bf, qn, kn, *rope_tabs)


def _rope_tables():
    t = jnp.arange(DEC_SEQ)
    row = (t // GRID_W).astype(F32)
    col = (t % GRID_W).astype(F32)
    nf = HEAD_DIM // 4
    inv = ROPE_THETA ** (-jnp.arange(nf, dtype=F32) / nf)
    ang_r = row[:, None] * inv[None, :]
    ang_c = col[:, None] * inv[None, :]
    zeros = jnp.zeros_like(ang_r)
    cos = jnp.concatenate([jnp.cos(ang_r), jnp.cos(ang_r), jnp.cos(ang_c), jnp.cos(ang_c)], axis=-1)
    sa = jnp.concatenate([-jnp.sin(ang_r), zeros, -jnp.sin(ang_c), zeros], axis=-1)
    sb = jnp.concatenate([zeros, jnp.sin(ang_r), zeros, jnp.sin(ang_c)], axis=-1)
    return tuple(jnp.tile(a, (1, 4)) for a in (cos, sa, sb))


N_HEADS = 4


def _lane_head(width):
    return lax.broadcasted_iota(jnp.int32, (1, width), 1) // HEAD_DIM


def _stack_heads(q):
    head = _lane_head(q.shape[1])
    return jnp.concatenate([jnp.where(head == h, q, 0.0) for h in range(N_HEADS)], axis=0).astype(BF16)


def _stack_heads_gqa(q):
    lo = lax.broadcasted_iota(jnp.int32, (1, LANES), 1) < HEAD_DIM
    q = q.astype(F32)
    q01, q23 = q[:, :LANES], q[:, LANES:]
    blocks = [jnp.where(lo, q01, 0.0), jnp.where(lo, pltpu.roll(q01, HEAD_DIM, 1), 0.0),
              jnp.where(lo, 0.0, pltpu.roll(q23, HEAD_DIM, 1)), jnp.where(lo, 0.0, q23)]
    return jnp.concatenate(blocks, axis=0).astype(BF16)


def _spread_kv_gqa(v):
    lo = lax.broadcasted_iota(jnp.int32, (1, LANES), 1) < HEAD_DIM
    v = v.astype(F32)
    vr = pltpu.roll(v, HEAD_DIM, 1)
    return jnp.concatenate([jnp.where(lo, v, vr), jnp.where(lo, vr, v)], axis=1)


def _mha(qs, blocks, tq):
    scores = []
    for k, _, bias in blocks:
        s = _bdot_nt(qs, k)
        scores.append(s if bias is None else s + bias)
    m = functools.reduce(jnp.maximum, [jnp.max(s, axis=-1, keepdims=True) for s in scores])
    es = [jnp.exp(s - m) for s in scores]
    denom = functools.reduce(jnp.add, [jnp.sum(e, axis=-1, keepdims=True) for e in es])
    ps = [e.astype(BF16) for e in es]
    head = _lane_head(N_HEADS * HEAD_DIM)
    vals = [v.astype(BF16) for _, v, _ in blocks]
    o = None
    dall = None
    for h in range(N_HEADS):
        rows = slice(h * tq, (h + 1) * tq)
        for p, v in zip(ps, vals):
            t = jnp.dot(p[rows], jnp.where(head == h, v, jnp.zeros_like(v)), preferred_element_type=F32)
            o = t if o is None else o + t
        d = jnp.where(head == h, denom[rows], 0.0)
        dall = d if dall is None else dall + d
    return (o / dall).astype(BF16)


def _lat_attn_a_kernel(q_ref, kn_ref, vn_ref, kc_ref, vc_ref, o_ref):
    for b in range(DEC_BATCH):
        o_ref[b] = _mha(_stack_heads_gqa(q_ref[b] * Q_SCALE),
                        [(kc_ref[b], _spread_kv_gqa(vc_ref[b]), None),
                         (kn_ref[b], _spread_kv_gqa(vn_ref[b]), None)], q_ref.shape[1])


def _lat_attention_a(z, cache_k, cache_v, layer, tq=LAT_TQ):
    cache_spec = pl.BlockSpec((DEC_BATCH, None, PAST_LEN, KV_WIDTH), lambda j: (0, layer, 0, 0))
    return pl.pallas_call(
        _lat_attn_a_kernel,
        grid=(DEC_SEQ // tq,),
        in_specs=[pl.BlockSpec((DEC_BATCH, tq, GROUP_WIDTH), lambda j: (0, j, OFF_AQ // GROUP_WIDTH)),
                  pl.BlockSpec((DEC_BATCH, DEC_SEQ, KV_WIDTH), lambda j: (0, 0, OFF_AK // KV_WIDTH)),
                  pl.BlockSpec((DEC_BATCH, DEC_SEQ, KV_WIDTH), lambda j: (0, 0, OFF_AV // KV_WIDTH)),
                  cache_spec, cache_spec],
        out_specs=pl.BlockSpec((DEC_BATCH, tq, GROUP_WIDTH), lambda j: (0, j, 0)),
        out_shape=jax.ShapeDtypeStruct((DEC_BATCH, DEC_SEQ, GROUP_WIDTH), BF16),
        compiler_params=_cparams("parallel"),
        name="lat_attention_a",
    )(z, z, z, cache_k, cache_v)


NA_KEYS = NA_ROWS * GRID_W


NA_PAIRS = 2 * NA_ROWS - 2


NA_STEP_ROWS = 2


def _na_kernel(q_ref, k_ref, v_ref, kc_ref, vc_ref, bias_ref, o_ref):
    rows = DEC_SEQ // GRID_W
    for b in range(q_ref.shape[0]):
        outs = []
        for rr in range(NA_STEP_ROWS):
            r = pl.program_id(0) * NA_STEP_ROWS + rr
            row_start = jnp.clip(r - NA_ROWS // 2, 0, rows - NA_ROWS)
            start = pl.multiple_of(row_start * GRID_W, GRID_W)
            rel0 = row_start - r + NA_ROWS - 1
            kl = k_ref[b, pl.ds(start, NA_KEYS), :]
            vl = v_ref[b, pl.ds(start, NA_KEYS), :]
            bias = jnp.concatenate(
                [jnp.concatenate([bias_ref[h, rel0 + 2 * jp] for jp in range(NA_ROWS // 2)], axis=1)
                 for h in range(B_HEADS)], axis=0)
            qrows = slice(rr * GRID_W, (rr + 1) * GRID_W)
            outs.append(_mha(_stack_heads(q_ref[b, qrows, :] * Q_SCALE),
                             [(kl, vl, bias), (kc_ref[b], vc_ref[b], None)], GRID_W))
        o_ref[b] = jnp.concatenate(outs, axis=0)


def _na_bias(rel_bias):
    nrel = 2 * NA_COLS - 1
    period = 2 * GRID_W
    b = rel_bias.astype(F32)
    ext = jnp.concatenate([b[..., NA_COLS - 1:],
                           jnp.zeros(b.shape[:-1] + (period - nrel,), F32),
                           b[..., :NA_COLS - 1]], axis=-1)
    flat = jnp.tile(ext, (1, 1, GRID_W))[..., :GRID_W * (period - 1)]
    toe = flat.reshape(b.shape[:-1] + (GRID_W, period - 1))[..., :GRID_W]
    col_start = np.clip(np.arange(GRID_W) - NA_COLS // 2, 0, GRID_W - NA_COLS)
    kc = np.arange(GRID_W)
    inside = (kc[None, :] >= col_start[:, None]) & (kc[None, :] < col_start[:, None] + NA_COLS)
    toe = jnp.where(jnp.asarray(inside), toe, NEG_BIG)
    return jnp.concatenate([toe[:, :-1], toe[:, 1:]], axis=-1)


def _lat_attention_b(z, cache_k, cache_v, bias, layer):
    tq = NA_STEP_ROWS * GRID_W
    cache_spec = pl.BlockSpec((DEC_BATCH, None, PAST_LEN, GROUP_WIDTH), lambda r: (0, layer, 0, 0))
    return pl.pallas_call(
        _na_kernel,
        grid=(DEC_SEQ // tq,),
        in_specs=[pl.BlockSpec((DEC_BATCH, tq, GROUP_WIDTH), lambda r: (0, r, OFF_BQ // GROUP_WIDTH)),
                  pl.BlockSpec((DEC_BATCH, DEC_SEQ, GROUP_WIDTH), lambda r: (0, 0, OFF_BK // GROUP_WIDTH)),
                  pl.BlockSpec((DEC_BATCH, DEC_SEQ, GROUP_WIDTH), lambda r: (0, 0, OFF_BV // GROUP_WIDTH)),
                  cache_spec, cache_spec,
                  pl.BlockSpec((B_HEADS, NA_PAIRS, GRID_W, 2 * GRID_W), lambda r: (0, 0, 0, 0))],
        out_specs=pl.BlockSpec((DEC_BATCH, tq, GROUP_WIDTH), lambda r: (0, r, 0)),
        out_shape=jax.ShapeDtypeStruct((DEC_BATCH, DEC_SEQ, GROUP_WIDTH), BF16),
        compiler_params=_cparams("parallel"),
        name="lat_attention_b",
    )(z, z, z, cache_k, cache_v, bias)


def _retention_core(q, k, v, g, dec_ref, gn_ref, dec_scr, *, seq_len, i0, decay_fill, s0_ref=None,
                    want_state=False):
    tq = q.shape[0]
    head = _lane_head(C_HEADS * HEAD_DIM)
    lg = jax.nn.log_sigmoid(dec_ref[...])

    def per_lane(row0):
        out = jnp.zeros((1, C_HEADS * HEAD_DIM), F32)
        for h in range(C_HEADS):
            out = jnp.where(head == h, lg[row0 + h:row0 + h + 1, 0:1], out)
        return out

    lgf_l, lgb_l = per_lane(0), per_lane(C_HEADS)
    qi = (i0 + lax.broadcasted_iota(jnp.int32, (tq, 1), 0)).astype(F32)

    def fill_decay():
        kj = lax.broadcasted_iota(jnp.int32, (1, seq_len), 1).astype(F32)
        diff = qi - kj
        for h in range(C_HEADS):
            lgf = lg[h:h + 1, 0:1]
            lgb = lg[C_HEADS + h:C_HEADS + h + 1, 0:1]
            dec_scr[h * tq:(h + 1) * tq, :] = (
                jnp.where(diff >= 0, jnp.exp(lgf * jnp.maximum(diff, 0.0)), 0.0)
                + jnp.where(diff <= 0, jnp.exp(lgb * jnp.maximum(-diff, 0.0)), 0.0))

    if decay_fill == "first_step":
        pl.when(pl.program_id(0) == 0)(fill_decay)
    elif decay_fill == "every_step":
        fill_decay()
    else:
        assert decay_fill == "filled"

    v = v.astype(BF16)
    sc = (_bdot_nt(_stack_heads(q), k) * dec_scr[...]).astype(BF16)
    o = None
    for h in range(C_HEADS):
        t = jnp.dot(sc[h * tq:(h + 1) * tq], jnp.where(head == h, v, jnp.zeros_like(v)),
                    preferred_element_type=F32)
        o = t if o is None else o + t
    if s0_ref is not None:
        o = (o + _bdot(q, s0_ref[0]) * jnp.exp(lgf_l * (qi + 1.0))
             + _bdot(q, s0_ref[1]) * jnp.exp(lgb_l * (seq_len - qi)))
    gm = _group_mean_matrix(C_HEADS * HEAD_DIM)
    dlt = o - _dot_hilo_lhs(o, gm)
    var = _dot_hilo_lhs(dlt * dlt, gm)
    out = (dlt * lax.rsqrt(var + EPS) * gn_ref[...] * jax.nn.silu(g)).astype(BF16)
    if not want_state:
        return out, None
    kpos = lax.broadcasted_iota(jnp.int32, (seq_len, 1), 0).astype(F32)
    sf = _bdot_tn(k * jnp.exp(lgf_l * (seq_len - 1.0 - kpos)), v)
    sb = _bdot_tn(k * jnp.exp(lgb_l * kpos), v)
    return out, (sf, sb)


def _store_retention_state(st_ref, slot, state):
    for s in range(st_ref.shape[0]):
        for d in range(2):
            for h in range(C_HEADS):
                sl = slice(h * HEAD_DIM, (h + 1) * HEAD_DIM)
                st_ref[s, d, h] = state[d][sl, sl] if s == slot else jnp.zeros((HEAD_DIM, HEAD_DIM), F32)


def _retention_kernel(q_ref, g_ref, k_ref, v_ref, dec_ref, gn_ref, s0_ref, o_ref, dec_scr, *, seq_len, tq):
    for b in range(q_ref.shape[0]):
        o_ref[b], _ = _retention_core(q_ref[b], k_ref[b], v_ref[b], g_ref[b], dec_ref, gn_ref, dec_scr,
                                      seq_len=seq_len, i0=pl.program_id(0) * tq,
                                      decay_fill="every_step" if b == 0 else "filled", s0_ref=s0_ref.at[b])


def _retention(z, cg, dec, gn, s0, layer, *, tq=LAT_TQ):
    nb, seq_len = z.shape[:2]
    return pl.pallas_call(
        functools.partial(_retention_kernel, seq_len=seq_len, tq=tq),
        grid=(seq_len // tq,),
        in_specs=[pl.BlockSpec((nb, tq, GROUP_WIDTH), lambda j: (0, j, OFF_CQ // GROUP_WIDTH)),
                  pl.BlockSpec((nb, tq, GROUP_WIDTH), lambda j: (0, j, 0)),
                  pl.BlockSpec((nb, seq_len, GROUP_WIDTH), lambda j: (0, 0, OFF_CK // GROUP_WIDTH)),
                  pl.BlockSpec((nb, seq_len, GROUP_WIDTH), lambda j: (0, 0, OFF_CV // GROUP_WIDTH)),
                  pl.BlockSpec((SUBLANES, LANES), lambda j: (0, 0)),
                  pl.BlockSpec((1, GROUP_WIDTH), lambda j: (0, 0)),
                  pl.BlockSpec((nb, None, 2, GROUP_WIDTH, GROUP_WIDTH), lambda j: (0, layer, 0, 0, 0))],
        out_specs=pl.BlockSpec((nb, tq, GROUP_WIDTH), lambda j: (0, j, 0)),
        out_shape=jax.ShapeDtypeStruct((nb, seq_len, GROUP_WIDTH), BF16),
        scratch_shapes=[pltpu.VMEM((C_HEADS * tq, seq_len), F32)],
        compiler_params=_cparams("arbitrary"),
        name="retention",
    )(z, cg, z, z, dec, gn, s0)


CTX_SEQS = 4


def _decay_table_kernel(dec_ref, o_ref):
    lg = jax.nn.log_sigmoid(dec_ref[...])
    diff = (lax.broadcasted_iota(jnp.int32, (SEQ, 1), 0) - lax.broadcasted_iota(jnp.int32, (1, SEQ), 1)).astype(F32)
    for h in range(C_HEADS):
        lgf = lg[h:h + 1, 0:1]
        lgb = lg[C_HEADS + h:C_HEADS + h + 1, 0:1]
        o_ref[h * SEQ:(h + 1) * SEQ, :] = (
            jnp.where(diff >= 0, jnp.exp(lgf * jnp.maximum(diff, 0.0)), 0.0)
            + jnp.where(diff <= 0, jnp.exp(lgb * jnp.maximum(-diff, 0.0)), 0.0))


def _decay_table(dec):
    return pl.pallas_call(
        _decay_table_kernel,
        in_specs=[pl.BlockSpec((SUBLANES, LANES), lambda: (0, 0))],
        out_specs=pl.BlockSpec((C_HEADS * SEQ, SEQ), lambda: (0, 0)),
        out_shape=jax.ShapeDtypeStruct((C_HEADS * SEQ, SEQ), F32),
        name="decay_table",
    )(dec)


def _ctx_front_kernel(x_ref, mod_ref, g1_ref, w_ref, qn_ref, kn_ref, dec_ref, gn_ref, dtab_ref, *rest,
                      n_alias, slot):
    (oa_ref, ob_ref, oc_ref, du_ref, ak_ref, av_ref, bk_ref, bv_ref, st_ref) = rest[n_alias:]
    dec_scr = dtab_ref
    tq = x_ref.shape[0] // CTX_SEQS
    h = _rms_rows(x_ref[...]) * g1_ref[...] * (1.0 + mod_ref[1:2, :]) + mod_ref[0:1, :]
    zz = jnp.dot(h.astype(BF16), w_ref[...], preferred_element_type=F32)
    for s in range(CTX_SEQS):
        rows = slice(s * tq, (s + 1) * tq)
        z = zz[rows, :]
        aq = _head_norm(z[:, OFF_AQ:OFF_AK], qn_ref[...])
        ak = _head_norm(z[:, OFF_AK:OFF_AV], kn_ref[...])
        av, bq, bk, bv = (z[:, OFF_AV:OFF_BQ], z[:, OFF_BQ:OFF_BK], z[:, OFF_BK:OFF_BV], z[:, OFF_BV:OFF_CQ])
        oa_ref[rows, :] = _mha(_stack_heads_gqa(aq * Q_SCALE), [(ak, _spread_kv_gqa(av), None)], tq)
        ob_ref[rows, :] = _mha(_stack_heads(bq * Q_SCALE), [(bk, bv, None)], tq)
        oc_ref[rows, :], state = _retention_core(
            z[:, OFF_CQ:OFF_CK], z[:, OFF_CK:OFF_CV] * Q_SCALE, z[:, OFF_CV:OFF_CG], z[:, OFF_CG:OFF_DU],
            dec_ref, gn_ref, dec_scr, seq_len=tq, i0=0, decay_fill="filled",
            want_state=True)
        du_ref[:, s * GROUP_WIDTH:(s + 1) * GROUP_WIDTH] = z[:, OFF_DU:]
        _store_layer_slot(ak_ref.at[s], slot, ak)
        _store_layer_slot(av_ref.at[s], slot, av)
        _store_layer_slot(bk_ref.at[s], slot, bk)
        _store_layer_slot(bv_ref.at[s], slot, bv)
        _store_retention_state(st_ref.at[s], slot, state)


def _ctx_front(x, mods, g1, w_in_bf, qn, kn, dec, gn, layer, prev):
    assert SEQ == S5_SEG
    tm = CTX_SEQS * SEQ
    n = x.shape[0]
    nb = n // SEQ
    steps = n // tm
    per_blk = SUBLANES // CTX_SEQS
    const = lambda *shape: pl.BlockSpec(shape, lambda i: (0,) * len(shape))
    row = lambda w: pl.BlockSpec((tm, w), lambda i: (i, 0))
    in_specs = [row(D_MODEL), _mod_spec(layer, 0, steps), const(1, D_MODEL),
                pl.BlockSpec((None, D_MODEL, IN_WIDTH), lambda i: (layer, 0, 0)),
                const(1, GROUP_WIDTH), const(1, KV_WIDTH), const(SUBLANES, LANES), const(1, GROUP_WIDTH),
                const(C_HEADS * SEQ, SEQ)]
    args = [x, mods, g1, w_in_bf, qn, kn, dec, gn, _decay_table(dec)]
    out_specs = [row(GROUP_WIDTH), row(GROUP_WIDTH), row(GROUP_WIDTH),
                 pl.BlockSpec((None, S5_SEG, CTX_SEQS * GROUP_WIDTH), lambda i: (i // per_blk, 0, i % per_blk))]
    out_shape = [jax.ShapeDtypeStruct((n, GROUP_WIDTH), BF16)] * 3 + [
        jax.ShapeDtypeStruct((nb // SUBLANES, S5_SEG, SUBLANES * GROUP_WIDTH), F32)]
    first = prev is None
    slot = 0
    for tail in ((SEQ, KV_WIDTH), (SEQ, KV_WIDTH), (SEQ, GROUP_WIDTH), (SEQ, GROUP_WIDTH),
                 (2, C_HEADS, HEAD_DIM, HEAD_DIM)):
        blk, idx, slot = _layer_slot_block(layer, first, tail)
        out_specs.append(pl.BlockSpec((CTX_SEQS,) + blk[1:], lambda i, idx=idx: (i,) + idx))
        out_shape.append(jax.ShapeDtypeStruct((nb, DEPTH) + tail, F32))
    aliases = {}
    if not first:
        for k, arr in enumerate(prev):
            aliases[len(args)] = 4 + k
            in_specs.append(pl.BlockSpec(memory_space=pl.ANY))
            args.append(arr)
    outs = pl.pallas_call(
        functools.partial(_ctx_front_kernel, n_alias=len(aliases), slot=slot),
        grid=(steps,),
        in_specs=in_specs,
        out_specs=out_specs,
        out_shape=out_shape,
        input_output_aliases=aliases,
        compiler_params=_cparams("parallel"),
        name="ctx_front",
    )(*args)
    return outs[0], outs[1], outs[2], outs[3], tuple(outs[4:])


def _s5_prep_kernel(lre_ref, lim_ref, ldt_ref, bre_ref, bim_ref, cre_ref, cim_ref,
                    a_ref, bm_ref, cro_ref, cio_ref, bm_scr, cr_scr, ci_scr):
    lre = lre_ref[...]
    lim = lim_ref[...]
    dt = jnp.exp(ldt_ref[...])
    mag = jnp.exp(lre * dt)
    a_re = mag * jnp.cos(lim * dt)
    a_im = mag * jnp.sin(lim * dt)
    den = lre * lre + lim * lim
    r_re = ((a_re - 1.0) * lre + a_im * lim) / den
    r_im = (a_im * lre - (a_re - 1.0) * lim) / den
    bm_scr[...] = jnp.zeros_like(bm_scr)
    cr_scr[...] = jnp.zeros_like(cr_scr)
    ci_scr[...] = jnp.zeros_like(ci_scr)
    for g in range(S5_GROUPS):
        rows = slice(g * S5_CH, (g + 1) * S5_CH)
        cols = slice(g * S5_STATE, (g + 1) * S5_STATE)
        a_ref[0:1, cols] = a_re[g:g + 1, :]
        a_ref[1:2, cols] = a_im[g:g + 1, :]
        rr, ri = r_re[g:g + 1, :], r_im[g:g + 1, :]
        br, bi = bre_ref[g], bim_ref[g]
        bm_scr[rows, cols] = rr * br - ri * bi
        bm_scr[rows, S5_SP + g * S5_STATE:S5_SP + (g + 1) * S5_STATE] = rr * bi + ri * br
        cr_scr[cols, rows] = cre_ref[g]
        ci_scr[cols, rows] = cim_ref[g]
    bm_ref[...] = bm_scr[...].astype(BF16)
    cro_ref[...] = cr_scr[...].astype(BF16)
    cio_ref[...] = ci_scr[...].astype(BF16)


def _s5_prepare(lam_re, lam_im, log_dt, b_re, b_im, c_re, c_im):
    gp = (S5_GROUPS, S5_STATE)
    ldt = jnp.broadcast_to(log_dt[..., None], (DEPTH, 2) + gp)
    bt = [jnp.swapaxes(t, -1, -2) for t in (b_re, b_im)]
    ct = [jnp.swapaxes(t, -1, -2) for t in (c_re, c_im)]

    def spec(*tail):
        return pl.BlockSpec((None, None) + tail, lambda l, d: (l, d) + (0,) * len(tail))

    return pl.pallas_call(
        _s5_prep_kernel,
        grid=(DEPTH, 2),
        in_specs=[spec(*gp)] * 3 + [spec(S5_GROUPS, S5_CH, S5_STATE)] * 2 + [spec(S5_GROUPS, S5_STATE, S5_CH)] * 2,
        out_specs=[spec(2, S5_SP), spec(GROUP_WIDTH, 2 * S5_SP), spec(S5_SP, GROUP_WIDTH), spec(S5_SP, GROUP_WIDTH)],
        out_shape=[jax.ShapeDtypeStruct((DEPTH, 2, 2, S5_SP), F32),
                   jax.ShapeDtypeStruct((DEPTH, 2, GROUP_WIDTH, 2 * S5_SP), BF16),
                   jax.ShapeDtypeStruct((DEPTH, 2, S5_SP, GROUP_WIDTH), BF16),
                   jax.ShapeDtypeStruct((DEPTH, 2, S5_SP, GROUP_WIDTH), BF16)],
        scratch_shapes=[pltpu.VMEM((GROUP_WIDTH, 2 * S5_SP), F32), pltpu.VMEM((S5_SP, GROUP_WIDTH), F32),
                        pltpu.VMEM((S5_SP, GROUP_WIDTH), F32)],
        compiler_params=_cparams("parallel", "parallel"),
        name="s5_prepare",
    )(lam_re, lam_im, ldt, bt[0], bt[1], ct[0], ct[1])


def _cmul(ar, ai, br, bi):
    return ar * br - ai * bi, ar * bi + ai * br


def _s5_kernel(u_ref, h0_ref, a_ref, bm_ref, cre_ref, cim_ref, dvec_ref, glu_ref, *rest, nseg, slot):
    od_ref, fin_ref, x_scr, s_scr, y_scr = rest[-5:]
    steps = S5_SEG
    rows = steps * SUBLANES
    chunk = S5_CHUNK
    chunk_steps = chunk // SUBLANES
    nchunk = rows // chunk
    seg = lax.broadcasted_iota(jnp.int32, (SUBLANES, S5_SP), 0) % nseg

    for d in range(2):
        ar = jnp.broadcast_to(a_ref[d, 0:1, :], (SUBLANES, S5_SP))
        ai = jnp.broadcast_to(a_ref[d, 1:2, :], (SUBLANES, S5_SP))

        def row0(k):
            c = k if d == 0 else nchunk - 1 - k
            return c * chunk if isinstance(c, int) else pl.multiple_of(c * chunk, chunk)

        def input_part(k, buf):
            x_scr[buf] = jnp.dot(u_ref[pl.ds(row0(k), chunk), :].astype(BF16), bm_ref[d],
                                 preferred_element_type=F32)

        def scan_part(buf, carry, store):
            sr, si = carry
            for t in range(chunk_steps):
                r = (t if d == 0 else chunk_steps - 1 - t) * SUBLANES
                pr, pi = _cmul(ar, ai, sr, si)
                sr = pr + x_scr[buf, r:r + SUBLANES, 0:S5_SP]
                si = pi + x_scr[buf, r:r + SUBLANES, S5_SP:]
                if store:
                    s_scr[buf, r:r + SUBLANES, 0:S5_SP] = sr
                    s_scr[buf, r:r + SUBLANES, S5_SP:] = si
            return sr, si

        def output_part(k, buf):
            y = _bdot(s_scr[buf, :, 0:S5_SP], cre_ref[d]) - _bdot(s_scr[buf, :, S5_SP:], cim_ref[d])
            rows_k = pl.ds(row0(k), chunk)
            if d == 0:
                y_scr[rows_k, :] = y
            else:
                zz = jax.nn.gelu(y_scr[rows_k, :] + y + dvec_ref[...] * u_ref[rows_k, :])
                od_ref[rows_k, :] = (zz * jax.nn.sigmoid(_bdot(zz, glu_ref[...]))).astype(BF16)

        def half(k, buf, carry, store, nxt=True, prev=True):
            if nxt:
                input_part(k + 1, 1 - buf)
            carry = scan_part(buf, carry, store)
            if store and prev:
                output_part(k - 1, 1 - buf)
            return carry

        def run_pass(carry, store):
            input_part(0, 0)
            carry = half(0, 0, carry, store, prev=False)
            carry = half(1, 1, carry, store)

            for k in range(2, nchunk - 2):
                carry = half(k, k % 2, carry, store)
            carry = half(nchunk - 2, 0, carry, store)
            carry = half(nchunk - 1, 1, carry, store, nxt=False)
            if store:
                output_part(nchunk - 1, 1)
            return carry

        init = (h0_ref[d, :, 0:S5_SP], h0_ref[d, :, S5_SP:])
        if nseg > 1:
            zero = jnp.zeros((SUBLANES, S5_SP), F32)
            fr, fi = run_pass((zero, zero), store=False)
            pr, pi = ar, ai
            for _ in range(int(math.log2(steps))):
                pr, pi = _cmul(pr, pi, pr, pi)
            cr, ci = init
            shift = 1 if d == 0 else SUBLANES - 1
            order = range(1, nseg) if d == 0 else range(nseg - 2, -1, -1)
            for s in order:
                ncr, nci = pltpu.roll(cr, shift, 0), pltpu.roll(ci, shift, 0)
                nfr, nfi = pltpu.roll(fr, shift, 0), pltpu.roll(fi, shift, 0)
                qr, qi = _cmul(pr, pi, ncr, nci)
                cr = jnp.where(seg == s, qr + nfr, cr)
                ci = jnp.where(seg == s, qi + nfi, ci)
            init = (cr, ci)
        sr, si = run_pass(init, store=True)
        for s in range(fin_ref.shape[1] // (4 * S5_SP)):
            base = (4 * s + 2 * d) * S5_SP
            fin_ref[:, base:base + S5_SP] = sr if s == slot else jnp.zeros_like(sr)
            fin_ref[:, base + S5_SP:base + 2 * S5_SP] = si if s == slot else jnp.zeros_like(si)


def _s5(du_tm, h0, a, bmat, cre, cim, dvec, glu_bf, layer, *, nseg, fin_layer=0, fin_layers=1,
        prev_fin=None):
    nblk = du_tm.shape[0]
    rows = S5_SEG * SUBLANES
    fin_w = 4 * S5_SP
    in_specs = [pl.BlockSpec((None, rows, GROUP_WIDTH), lambda i: (i, 0, 0)),
                pl.BlockSpec((2, SUBLANES, 2 * S5_SP), lambda i: (0, 0, 0)),
                pl.BlockSpec((None, 2, 2, S5_SP), lambda i: (layer, 0, 0, 0)),
                pl.BlockSpec((None, 2, GROUP_WIDTH, 2 * S5_SP), lambda i: (layer, 0, 0, 0)),
                pl.BlockSpec((None, 2, S5_SP, GROUP_WIDTH), lambda i: (layer, 0, 0, 0)),
                pl.BlockSpec((None, 2, S5_SP, GROUP_WIDTH), lambda i: (layer, 0, 0, 0)),
                pl.BlockSpec((1, GROUP_WIDTH), lambda i: (0, 0)),
                pl.BlockSpec((None, GROUP_WIDTH, GROUP_WIDTH), lambda i: (layer, 0, 0))]
    args = [du_tm.reshape(nblk, rows, GROUP_WIDTH), h0, a, bmat, cre, cim, dvec, glu_bf]
    aliases = {}
    if prev_fin is not None:
        aliases[len(args)] = 1
        in_specs.append(pl.BlockSpec(memory_space=pl.ANY))
        args.append(prev_fin)
        fin_spec, slot = pl.BlockSpec((SUBLANES, fin_w), lambda i: (i, fin_layer)), 0
    else:
        fin_spec, slot = pl.BlockSpec((SUBLANES, fin_layers * fin_w), lambda i: (i, 0)), fin_layer
    od, fin = pl.pallas_call(
        functools.partial(_s5_kernel, nseg=nseg, slot=slot),
        grid=(nblk,),
        in_specs=in_specs,
        out_specs=[pl.BlockSpec((None, rows, GROUP_WIDTH), lambda i: (i, 0, 0)), fin_spec],
        out_shape=[jax.ShapeDtypeStruct((nblk, rows, GROUP_WIDTH), BF16),
                   jax.ShapeDtypeStruct((nblk * SUBLANES, fin_layers * fin_w), F32)],
        scratch_shapes=[pltpu.VMEM((2, S5_CHUNK, 2 * S5_SP), F32), pltpu.VMEM((2, S5_CHUNK, 2 * S5_SP), F32),
                        pltpu.VMEM((rows, GROUP_WIDTH), F32)],
        input_output_aliases=aliases,
        compiler_params=_cparams("parallel"),
        name="s5",
    )(*args)
    return od.reshape(nblk, S5_SEG, SUBLANES * GROUP_WIDTH), fin


ROUTE_GROUP = MOE_PER_GROUP
OUT_SEQS = 2
MOE_TILE = 512


def _out_kernel(x_ref, oa_ref, ob_ref, oc_ref, od_ref, mod_ref, wo_ref, g2_ref, wrh_ref, wrl_ref, br_ref,
                xm_ref, h2_ref, route_ref, cnt_ref):
    od = jnp.concatenate([od_ref[:, s * GROUP_WIDTH:(s + 1) * GROUP_WIDTH] for s in range(OUT_SEQS)], axis=0)
    mix = functools.reduce(jnp.add, [
        _bdot(o, wo_ref[i * GROUP_WIDTH:(i + 1) * GROUP_WIDTH, :])
        for i, o in enumerate((oa_ref[...], ob_ref[...], oc_ref[...], od))])
    xm = x_ref[...] + mod_ref[2:3, :] * mix
    xm_ref[...] = xm
    h2 = _rms_rows(xm) * g2_ref[...] * (1.0 + mod_ref[4:5, :]) + mod_ref[3:4, :]
    h2_ref[...] = h2.astype(BF16)

    h_hi, h_lo = _split(h2)
    logits = (jnp.dot(h_hi, wrh_ref[...], preferred_element_type=F32)
              + jnp.dot(h_hi, wrl_ref[...], preferred_element_type=F32)
              + jnp.dot(h_lo, wrh_ref[...], preferred_element_type=F32)) + br_ref[...]
    lane_i = lax.broadcasted_iota(jnp.int32, logits.shape, 1)
    lane = lane_i.astype(F32)
    big = jnp.float32(2 ** 30)
    gmask = lane_i < MOE_GROUPS
    gl = jnp.where(gmask, logits, -jnp.inf)
    gmax = jnp.max(gl, axis=-1, keepdims=True)
    p_top = 1.0 / jnp.sum(jnp.exp(gl - gmax), axis=-1, keepdims=True)
    g_top = jnp.min(jnp.where(gl == gmax, lane, big), axis=-1, keepdims=True)
    e_lane = lane_i - ROUTER_OFF
    lane_group = (e_lane // MOE_PER_GROUP).astype(F32)
    emask = (e_lane >= 0) & (e_lane < MOE_EXPERTS) & (lane_group == g_top)
    el = jnp.where(emask, logits, -jnp.inf)
    m1 = jnp.max(el, axis=-1, keepdims=True)
    i1 = jnp.min(jnp.where(el == m1, lane, big), axis=-1, keepdims=True)
    el2 = jnp.where(lane == i1, -jnp.inf, el)
    m2 = jnp.max(el2, axis=-1, keepdims=True)
    i2 = jnp.min(jnp.where(el2 == m2, lane, big), axis=-1, keepdims=True)
    e2 = jnp.exp(m2 - m1)
    den = 1.0 + e2
    gates = (jnp.where(lane == i1, (1.0 / den) * p_top, 0.0)
             + jnp.where(lane == i2, (e2 / den) * p_top, 0.0))
    route = jnp.where(lane == ROUTE_GROUP + g_top, 1.0, 0.0)
    for g in range(MOE_GROUPS):
        local = pltpu.roll(gates, LANES - ROUTER_OFF - g * MOE_PER_GROUP, 1)
        route = route + jnp.where((g_top == g) & (lane_i < MOE_PER_GROUP), local, 0.0)
    route_ref[...] = route
    for t in range(cnt_ref.shape[0]):
        part = jnp.sum(route[t * MOE_TILE:(t + 1) * MOE_TILE], axis=0, keepdims=True)
        cnt_ref[t] = jnp.broadcast_to(part, (SUBLANES, LANES)).astype(jnp.int32)


def _output_stage(x, mixes, mods, mod_row, mod_tokens, wo_bf, g2, wr_hi, wr_lo, br, layer):
    tm = OUT_SEQS * S5_SEG
    n = x.shape[0]
    row = lambda w: pl.BlockSpec((tm, w), lambda i: (i, 0))
    const = lambda shape: pl.BlockSpec(shape, lambda i: (0,) * len(shape))
    per_blk = SUBLANES // OUT_SEQS
    return pl.pallas_call(
        _out_kernel,
        grid=(n // tm,),
        in_specs=[row(D_MODEL), row(GROUP_WIDTH), row(GROUP_WIDTH), row(GROUP_WIDTH),
                  pl.BlockSpec((None, S5_SEG, OUT_SEQS * GROUP_WIDTH), lambda i: (i // per_blk, 0, i % per_blk)),
                  _mod_spec(layer, mod_row, mod_tokens // tm),
                  pl.BlockSpec((None, D_MODEL, D_MODEL), lambda i: (layer, 0, 0)), const((1, D_MODEL)),
                  const((D_MODEL, LANES)), const((D_MODEL, LANES)), const((1, LANES))],
        out_specs=[row(D_MODEL), row(D_MODEL), row(LANES),
                   pl.BlockSpec((tm // MOE_TILE, SUBLANES, LANES), lambda i: (i, 0, 0))],
        out_shape=[jax.ShapeDtypeStruct((n, D_MODEL), F32),
                   jax.ShapeDtypeStruct((n, D_MODEL), BF16),
                   jax.ShapeDtypeStruct((n, LANES), F32),
                   jax.ShapeDtypeStruct((n // MOE_TILE, SUBLANES, LANES), jnp.int32)],
        compiler_params=_cparams("parallel"),
        name="output_stage",
    )(x, *mixes, mods, wo_bf, g2, wr_hi, wr_lo, br)


GROUP_HID = MOE_PER_GROUP * MOE_HIDDEN


MOE_CHUNK = 160


def _moe_kernel(cnt_ref, h2_ref, route_ref, xm_ref, mod_ref, w1_ref, w3_ref, w2_ref, fg_ref, before_ref, o_ref,
                hs_scr, rs_scr, os_scr, *, final, tm):
    i = pl.program_id(0)
    off1 = cnt_ref[i, 0]
    off2 = off1 + cnt_ref[i, 1]
    off3 = off2 + cnt_ref[i, 2]
    starts = (jnp.int32(0), off1, off2, off3)
    ends = (off1, off2, off3, jnp.int32(tm))

    route = route_ref[...]
    r_hi, r_lo = _split(route)
    pick = (lax.broadcasted_iota(jnp.int32, (SUBLANES, LANES), 1)
            == ROUTE_GROUP + lax.broadcasted_iota(jnp.int32, (SUBLANES, LANES), 0))
    gt = lax.dot_general(jnp.where(pick, 1.0, 0.0).astype(BF16), r_hi, (((1,), (1,)), ((), ())),
                         preferred_element_type=F32)
    rank = jnp.dot(gt.astype(BF16), before_ref[...], preferred_element_type=F32)
    gt_i = gt.astype(jnp.int32)
    rank_i = rank.astype(jnp.int32)
    pos = jnp.zeros((1, tm), jnp.int32)
    for g in range(MOE_GROUPS):
        pos = pos + gt_i[g:g + 1, :] * (rank_i[g:g + 1, :] + starts[g])
    perm = jnp.where(lax.broadcasted_iota(jnp.int32, (tm, tm), 0) == pos, 1.0, 0.0).astype(BF16)
    hs_scr[...] = jnp.dot(perm, h2_ref[...], preferred_element_type=F32).astype(BF16)
    rs_scr[...] = (jnp.dot(perm, r_hi, preferred_element_type=F32)
                   + jnp.dot(perm, r_lo, preferred_element_type=F32))

    os_scr[...] = jnp.zeros_like(os_scr)

    def evaluate(g, lo, hi, chunk):
        base = (lo // BF16_ROWS) * BF16_ROWS
        r0 = pl.multiple_of(jnp.minimum(base + chunk * MOE_CHUNK, tm - MOE_CHUNK), BF16_ROWS)
        rows = pl.ds(r0, MOE_CHUNK)
        x = hs_scr[rows, :]
        gates = rs_scr[rows, :]
        a = jnp.dot(x, w1_ref[g], preferred_element_type=F32)
        b = jnp.dot(x, w3_ref[g], preferred_element_type=F32)
        hid = []
        for e in range(MOE_PER_GROUP):
            sl = slice(e * MOE_HIDDEN, (e + 1) * MOE_HIDDEN)
            hid.append((jax.nn.silu(a[:, sl]) * b[:, sl] * gates[:, e:e + 1]).astype(BF16))
        y = jnp.dot(jnp.concatenate(hid, axis=1), w2_ref[g], preferred_element_type=F32)
        rowid = r0 + lax.broadcasted_iota(jnp.int32, (MOE_CHUNK, 1), 0)
        member = (rowid >= lo) & (rowid < hi)
        os_scr[rows, :] = jnp.where(member, y, os_scr[rows, :])

    extra = []
    for g in range(MOE_GROUPS):
        lo, hi = starts[g], ends[g]
        base = (lo // BF16_ROWS) * BF16_ROWS
        n_chunks = jnp.where(hi > lo, (hi - base + MOE_CHUNK - 1) // MOE_CHUNK, 0)
        evaluate(g, lo, hi, 0)
        extra.append(jnp.maximum(n_chunks - 1, 0))
    first = (jnp.int32(0), extra[0], extra[0] + extra[1], extra[0] + extra[1] + extra[2])

    def pick_by_group(g, vals):
        return jnp.where(g == 0, vals[0], jnp.where(g == 1, vals[1], jnp.where(g == 2, vals[2], vals[3])))

    def extra_body(idx, carry):
        g = sum((idx >= f).astype(jnp.int32) for f in first[1:])
        evaluate(g, pick_by_group(g, starts), pick_by_group(g, ends), idx - pick_by_group(g, first) + 1)
        return carry

    lax.fori_loop(0, first[3] + extra[3], extra_body, 0)

    o_hi, o_lo = _split(os_scr[...])
    moe = (lax.dot_general(perm, o_hi, (((0,), (0,)), ((), ())), preferred_element_type=F32)
           + lax.dot_general(perm, o_lo, (((0,), (0,)), ((), ())), preferred_element_type=F32))
    out = xm_ref[...] + mod_ref[5:6, :] * moe
    if final:
        out = _rms_rows(out) * fg_ref[...]
    o_ref[...] = out


def _moe_weight_kernel(w1_ref, w3_ref, w2_ref, o1_ref, o3_ref, o2_ref):
    for e in range(MOE_PER_GROUP):
        sl = slice(e * MOE_HIDDEN, (e + 1) * MOE_HIDDEN)
        o1_ref[:, sl] = w1_ref[e].astype(BF16)
        o3_ref[:, sl] = w3_ref[e].astype(BF16)
        o2_ref[sl, :] = w2_ref[e].astype(BF16)


def _moe_weights(w1, w3, w2):
    up = pl.BlockSpec((None, MOE_PER_GROUP, D_MODEL, MOE_HIDDEN), lambda l, g: (l, g, 0, 0))
    down = pl.BlockSpec((None, MOE_PER_GROUP, MOE_HIDDEN, D_MODEL), lambda l, g: (l, g, 0, 0))
    out = pl.BlockSpec((None, None, D_MODEL, GROUP_HID), lambda l, g: (l, g, 0, 0))
    shape = jax.ShapeDtypeStruct((DEPTH, MOE_GROUPS, D_MODEL, GROUP_HID), BF16)
    return pl.pallas_call(
        _moe_weight_kernel,
        grid=(DEPTH, MOE_GROUPS),
        in_specs=[up, up, down],
        out_specs=[out, out, out],
        out_shape=[shape, shape, shape],
        compiler_params=_cparams("parallel", "parallel"),
        name="moe_weights",
    )(w1, w3, w2)


def _moe(h2, route, tile_counts, xm, mods, mod_row, mod_tokens, w1g, w3g, w2g, fg, layer, *, final):
    tm = MOE_TILE
    n = h2.shape[0]
    cnt = tile_counts[:, 0, ROUTE_GROUP:ROUTE_GROUP + MOE_GROUPS]
    before = jnp.asarray(np.triu(np.ones((tm, tm), np.float32), 1), BF16)
    row = lambda w: pl.BlockSpec((tm, w), lambda i, c: (i, 0))
    mod_tiles = mod_tokens // tm
    wspec = pl.BlockSpec((None, MOE_GROUPS, D_MODEL, GROUP_HID), lambda i, c: (layer, 0, 0, 0),
                         pipeline_mode=pl.Buffered(1))
    return pl.pallas_call(
        functools.partial(_moe_kernel, final=final, tm=tm),
        grid_spec=pltpu.PrefetchScalarGridSpec(
            num_scalar_prefetch=1,
            grid=(n // tm,),
            in_specs=[row(D_MODEL), row(LANES), row(D_MODEL),
                      pl.BlockSpec((None, None, N_MOD, D_MODEL),
                                   lambda i, c: (layer, mod_row + i // mod_tiles, 0, 0)),
                      wspec, wspec, wspec,
                      pl.BlockSpec((1, D_MODEL), lambda i, c: (0, 0)),
                      pl.BlockSpec((tm, tm), lambda i, c: (0, 0), pipeline_mode=pl.Buffered(1))],
            out_specs=row(D_MODEL),
            scratch_shapes=[pltpu.VMEM((tm, D_MODEL), BF16), pltpu.VMEM((tm, LANES), F32),
                            pltpu.VMEM((tm, D_MODEL), F32)]),
        out_shape=jax.ShapeDtypeStruct((n, D_MODEL), F32),
        compiler_params=_cparams("arbitrary"),
        name="moe",
    )(cnt, h2, route, xm, mods, w1g, w3g, w2g, fg, before)


def kernel(x_prompt, x_sample, cache_a_k, cache_a_v, cache_b_k, cache_b_v, state_ret, state_ssm, c, c_ctx, mod_w, mod_b, norm1_g, norm2_g, w_in, a_qn_g, a_kn_g, b_rel_bias, ret_decay, ret_gn_g, s5_lam_re, s5_lam_im, s5_log_dt, s5_b_re, s5_b_im, s5_c_re, s5_c_im, s5_d, s5_glu_w, w_out, moe_gw, moe_gb, moe_ew, moe_eb, moe_w1, moe_w3, moe_w2, final_norm_g):
    n_ctx = BATCH * SEQ
    n_lat = DEC_BATCH * DEC_SEQ
    lat_seg = DEC_SEQ // S5_SEG

    cond = jnp.zeros((SUBLANES, D_MODEL), F32).at[0].set(c_ctx).at[1:1 + DEC_BATCH].set(c)
    mods = _modulation(cond, mod_w, mod_b).reshape(DEPTH, SUBLANES, N_MOD, D_MODEL)

    rope_tabs = _rope_tables()
    s5_a, s5_bm, s5_cre, s5_cim = _s5_prepare(s5_lam_re, s5_lam_im, s5_log_dt, s5_b_re, s5_b_im,
                                              s5_c_re, s5_c_im)
    cak = cache_a_k.reshape(DEC_BATCH, DEPTH, PAST_LEN, A_KV_HEADS * HEAD_DIM)
    cav = cache_a_v.reshape(DEC_BATCH, DEPTH, PAST_LEN, A_KV_HEADS * HEAD_DIM)
    cbk = cache_b_k.reshape(DEC_BATCH, DEPTH, PAST_LEN, B_HEADS * HEAD_DIM)
    cbv = cache_b_v.reshape(DEC_BATCH, DEPTH, PAST_LEN, B_HEADS * HEAD_DIM)

    xc = x_prompt.reshape(n_ctx, D_MODEL)
    xs = x_sample.reshape(n_lat, D_MODEL)
    w1_all, w3_all, w2_all = _moe_weights(moe_w1, moe_w3, moe_w2)
    eye_h = jnp.eye(C_HEADS, dtype=F32)
    s0_bd = (state_ret[:, :, :, :, :, None, :] * eye_h[None, None, None, :, None, :, None]).reshape(
        DEC_BATCH, DEPTH, 2, C_HEADS * HEAD_DIM, C_HEADS * HEAD_DIM)
    ctx_state = ssm_states = None
    h0_zero = jnp.zeros((2, SUBLANES, 2 * S5_SP), F32)
    w_in_bf = w_in.astype(BF16)
    wo_bf = w_out.astype(BF16)
    glu_bf = s5_glu_w.astype(BF16)
    for l in range(DEPTH):
        final = l == DEPTH - 1
        g1 = norm1_g[l].reshape(1, D_MODEL)
        g2 = norm2_g[l].reshape(1, D_MODEL)
        fg = final_norm_g.reshape(1, D_MODEL)
        qn = jnp.tile(a_qn_g[l], A_HEADS).reshape(1, GROUP_WIDTH)
        kn = jnp.tile(a_kn_g[l], A_KV_HEADS).reshape(1, KV_WIDTH)
        dec = jnp.broadcast_to(ret_decay[l].reshape(2 * C_HEADS, 1), (2 * C_HEADS, LANES))
        gn = ret_gn_g[l].reshape(1, GROUP_WIDTH)
        dvec = s5_d[l].reshape(1, GROUP_WIDTH)
        wr = jnp.zeros((D_MODEL, LANES), F32).at[:, :MOE_GROUPS].set(moe_gw[l]).at[
            :, ROUTER_OFF:ROUTER_OFF + MOE_EXPERTS].set(moe_ew[l])
        br = jnp.zeros((1, LANES), F32).at[0, :MOE_GROUPS].set(moe_gb[l]).at[
            0, ROUTER_OFF:ROUTER_OFF + MOE_EXPERTS].set(moe_eb[l])
        wr_hi = wr.astype(BF16)
        wr_lo = (wr - wr_hi.astype(F32)).astype(BF16)
        na_bias = _na_bias(b_rel_bias[l])

        oa, ob, oc, du_tm, ctx_state = _ctx_front(xc, mods, g1, w_in_bf, qn, kn, dec, gn, l, ctx_state)
        od_tm, ssm_states = _s5(du_tm, h0_zero, s5_a, s5_bm, s5_cre, s5_cim, dvec, glu_bf, l,
                                nseg=1, fin_layer=l, fin_layers=DEPTH, prev_fin=ssm_states)
        xm, h2, route, counts = _output_stage(xc, (oa, ob, oc, od_tm), mods, 0, n_ctx, wo_bf, g2,
                                              wr_hi, wr_lo, br, l)
        xc = _moe(h2, route, counts, xm, mods, 0, n_ctx, w1_all, w3_all, w2_all, fg, l, final=final)

        zs, cg, du_tm = _project(xs, mods, 1, DEC_SEQ, g1, w_in_bf, qn, kn, rope_tabs, l, seq_len=DEC_SEQ)
        zs3 = zs.reshape(DEC_BATCH, DEC_SEQ, OFF_CG)
        oa = _lat_attention_a(zs3, cak, cav, l).reshape(n_lat, GROUP_WIDTH)
        ob = _lat_attention_b(zs3, cbk, cbv, na_bias, l).reshape(n_lat, GROUP_WIDTH)
        oc = _retention(zs3, cg.reshape(DEC_BATCH, DEC_SEQ, GROUP_WIDTH), dec, gn, s0_bd, l).reshape(
            n_lat, GROUP_WIDTH)
        h0 = state_ssm[:, l].reshape(DEC_BATCH, 2, 2 * S5_SP).transpose(1, 0, 2)
        h0_seg = jnp.zeros((2, DEC_BATCH, lat_seg, 2 * S5_SP), F32)
        h0_seg = h0_seg.at[0, :, 0].set(h0[0]).at[1, :, lat_seg - 1].set(h0[1])
        od_tm, _ = _s5(du_tm, h0_seg.reshape(2, SUBLANES, 2 * S5_SP),
                       s5_a, s5_bm, s5_cre, s5_cim, dvec, glu_bf, l, nseg=lat_seg)
        xm, h2, route, counts = _output_stage(xs, (oa, ob, oc, od_tm), mods, 1, DEC_SEQ, wo_bf, g2,
                                              wr_hi, wr_lo, br, l)
        xs = _moe(h2, route, counts, xm, mods, 1, DEC_SEQ, w1_all, w3_all, w2_all, fg, l, final=final)

    new_ak, new_av, new_bk, new_bv, ret_states = ctx_state
    return (xc.reshape(BATCH, SEQ, D_MODEL), xs.reshape(DEC_BATCH, DEC_SEQ, D_MODEL),
            new_ak.reshape(BATCH, DEPTH, SEQ, A_KV_HEADS, HEAD_DIM),
            new_av.reshape(BATCH, DEPTH, SEQ, A_KV_HEADS, HEAD_DIM),
            new_bk.reshape(BATCH, DEPTH, SEQ, B_HEADS, HEAD_DIM),
            new_bv.reshape(BATCH, DEPTH, SEQ, B_HEADS, HEAD_DIM),
            ret_states,
            ssm_states.reshape(BATCH, DEPTH, 2, 2, S5_GROUPS, S5_STATE))
```
